```python
import jax, jax.numpy as jnp
from jax import lax
import numpy as np

D_MODEL = 1024
BATCH = 8
SEQ = 8192
DEPTH = 2

HEAD_DIM = 64
D_MIX = D_MODEL
N_HEADS_FOX = D_MIX // (2 * HEAD_DIM)
N_HEADS_DIL = D_MIX // (2 * HEAD_DIM)
D_FOX = N_HEADS_FOX * HEAD_DIM
D_DIL = N_HEADS_DIL * HEAD_DIM
N_IN = 4 * D_FOX + N_HEADS_FOX + 4 * D_DIL
PLE_DIM = 256
ROPE_THETA = 500000.0
ROPE_DIM = HEAD_DIM // 4
DILATED_PATTERNS = ((128, 1), (512, 4), (2048, 16))
BLOCK = 128
EPS = 1e-6
NEG = -1e30
FORGET_BIAS_INIT = 3.0

kernel_name = "fox_dilated_hybrid_heads"


def rms_norm(x, g):
    xf = x.astype(jnp.float32)
    y = xf * lax.rsqrt(jnp.mean(xf * xf, axis=-1, keepdims=True) + EPS)
    return (y * g.astype(jnp.float32)).astype(x.dtype)


def partial_rope(x, positions):
    half = ROPE_DIM // 2
    inv_freq = ROPE_THETA ** (-jnp.arange(half, dtype=jnp.float32) / half)
    ang = positions.astype(jnp.float32)[..., None] * inv_freq
    cos = jnp.cos(ang)[:, :, None, :]
    sin = jnp.sin(ang)[:, :, None, :]
    xr = x[..., :ROPE_DIM].astype(jnp.float32)
    x1, x2 = xr[..., :half], xr[..., half:]
    rot = jnp.concatenate([x1 * cos - x2 * sin, x2 * cos + x1 * sin], axis=-1)
    return jnp.concatenate([rot.astype(x.dtype), x[..., ROPE_DIM:]], axis=-1)


def forgetting_attention(q, k, v, log_f):
    B, S, H, Dh = q.shape
    nb = S // BLOCK
    scale = Dh ** -0.5
    c = jnp.cumsum(log_f.astype(jnp.float32), axis=1).transpose(0, 2, 1)
    qb = q.reshape(B, nb, BLOCK, H, Dh).transpose(1, 0, 2, 3, 4)
    cb = c.reshape(B, H, nb, BLOCK).transpose(2, 0, 1, 3)
    key_pos = jnp.arange(S)

    def one_block(args):
        i, q_i, c_i = args
        s = jnp.einsum('bqhd,bkhd->bhqk', q_i, k, preferred_element_type=jnp.float32) * scale
        s = s + c_i[..., :, None] - c[:, :, None, :]
        q_pos = i * BLOCK + jnp.arange(BLOCK)
        causal = key_pos[None, :] <= q_pos[:, None]
        s = jnp.where(causal, s, NEG)
        prob = jax.nn.softmax(s, axis=-1)
        return jnp.einsum('bhqk,bkhd->bqhd', prob.astype(v.dtype), v)

    out = lax.map(one_block, (jnp.arange(nb), qb, cb))
    return out.transpose(1, 0, 2, 3, 4).reshape(B, S, H, Dh)


def dilated_branch(q, k, v, window, dilation):
    B, S, H, Dh = q.shape
    L = S // dilation
    n_back = window // dilation
    n_prev = -(-n_back // BLOCK)
    nb = -(-L // BLOCK)
    Lp = nb * BLOCK
    N = B * dilation
    KW = (n_prev + 1) * BLOCK

    def to_streams(x):
        x = x.reshape(B, L, dilation, H, Dh).transpose(0, 2, 1, 3, 4).reshape(N, L, H, Dh)
        return jnp.pad(x, ((0, 0), (0, Lp - L), (0, 0), (0, 0)))

    def key_band(x):
        xs = jnp.pad(to_streams(x), ((0, 0), (n_prev * BLOCK, 0), (0, 0), (0, 0)))
        xs = xs.reshape(N, nb + n_prev, BLOCK, H, Dh)
        return jnp.concatenate([xs[:, j:j + nb] for j in range(n_prev + 1)], axis=2)

    qs = to_streams(q).reshape(N, nb, BLOCK, H, Dh)
    kb, vb = key_band(k), key_band(v)
    s = jnp.einsum('nbqhd,nbkhd->nbhqk', qs, kb, preferred_element_type=jnp.float32) * (Dh ** -0.5)
    qq = jnp.arange(BLOCK)[:, None]
    kk = jnp.arange(KW)[None, :]
    dist = qq + n_prev * BLOCK - kk
    key_idx = jnp.arange(nb)[:, None, None] * BLOCK - n_prev * BLOCK + kk[None]
    valid = (dist >= 0) & (dist <= n_back) & (key_idx >= 0)
    s = jnp.where(valid[None, :, None], s, NEG)
    lse = jax.nn.logsumexp(s, axis=-1)
    prob = jnp.exp(s - lse[..., None])
    o = jnp.einsum('nbhqk,nbkhd->nbqhd', prob.astype(v.dtype), vb)

    def from_streams(x):
        tail = x.shape[3:]
        x = x.reshape((N, Lp) + tail)[:, :L]
        x = x.reshape((B, dilation, L) + tail).swapaxes(1, 2)
        return x.reshape((B, S) + tail)

    return from_streams(o), from_streams(lse.transpose(0, 1, 3, 2))


def dilated_attention(q, k, v):
    outs, lses = [], []
    for window, dilation in DILATED_PATTERNS:
        o, l = dilated_branch(q, k, v, window, dilation)
        outs.append(o)
        lses.append(l)
    w = jax.nn.softmax(jnp.stack(lses, axis=0), axis=0)
    o = jnp.stack(outs, axis=0).astype(jnp.float32)
    return jnp.sum(w[..., None] * o, axis=0).astype(q.dtype)


def hybrid_layer(h, p_i, positions, norm_g, w_in, b_f, qk_g, w_out, w_ple, ple_norm_g, w_ple_gate):
    B, S, _ = h.shape
    u = rms_norm(h, norm_g)
    z = u @ w_in
    cuts = np.cumsum([D_FOX, D_FOX, D_FOX, D_FOX, N_HEADS_FOX, D_DIL, D_DIL, D_DIL])
    qa, ka, va, ga, fa, qb, kb, vb, gb = jnp.split(z, cuts.tolist(), axis=-1)
    heads = lambda t, n: t.reshape(B, S, n, HEAD_DIM)

    qa = rms_norm(heads(qa, N_HEADS_FOX), qk_g[0])
    ka = rms_norm(heads(ka, N_HEADS_FOX), qk_g[1])
    log_f = jax.nn.log_sigmoid(fa.astype(jnp.float32) + b_f.astype(jnp.float32))
    oa = forgetting_attention(qa, ka, heads(va, N_HEADS_FOX), log_f).reshape(B, S, D_FOX)
    oa = oa * jax.nn.silu(ga)

    qb = partial_rope(rms_norm(heads(qb, N_HEADS_DIL), qk_g[2]), positions)
    kb = partial_rope(rms_norm(heads(kb, N_HEADS_DIL), qk_g[3]), positions)
    ob = dilated_attention(qb, kb, heads(vb, N_HEADS_DIL)).reshape(B, S, D_DIL)
    ob = ob * jax.nn.silu(gb)

    h = h + jnp.concatenate([oa, ob], axis=-1) @ w_out

    gate = jax.nn.sigmoid(rms_norm(h, ple_norm_g) @ w_ple_gate)
    return h + (p_i @ w_ple) * gate


def _fwd_setup_inputs(seed: int = 0) -> dict:
    key = jax.random.key(seed)
    ks = jax.random.split(key, 12)
    f32 = jnp.float32
    x = jax.random.normal(ks[0], (BATCH, SEQ, D_MODEL), f32)
    p = jax.random.normal(ks[1], (DEPTH, BATCH, SEQ, PLE_DIM), f32)
    positions = jnp.broadcast_to(jnp.arange(SEQ, dtype=jnp.int32), (BATCH, SEQ))
    norm_g = 1.0 + 0.05 * jax.random.normal(ks[2], (DEPTH, D_MODEL), f32)
    w_in = jax.random.normal(ks[3], (DEPTH, D_MODEL, N_IN), f32) * D_MODEL ** -0.5
    b_f = FORGET_BIAS_INIT + 0.1 * jax.random.normal(ks[4], (DEPTH, N_HEADS_FOX), f32)
    qk_norm_g = 1.0 + 0.05 * jax.random.normal(ks[5], (DEPTH, 4, HEAD_DIM), f32)
    w_out = jax.random.normal(ks[6], (DEPTH, D_MIX, D_MODEL), f32) * D_MIX ** -0.5
    w_ple = jax.random.normal(ks[7], (DEPTH, PLE_DIM, D_MODEL), f32) * PLE_DIM ** -0.5
    ple_norm_g = 1.0 + 0.05 * jax.random.normal(ks[8], (DEPTH, D_MODEL), f32)
    w_ple_gate = jax.random.normal(ks[9], (DEPTH, D_MODEL, D_MODEL), f32) * D_MODEL ** -0.5
    return {"x": x, "p": p, "positions": positions, "norm_g": norm_g, "w_in": w_in,
            "b_f": b_f, "qk_norm_g": qk_norm_g, "w_out": w_out, "w_ple": w_ple,
            "ple_norm_g": ple_norm_g, "w_ple_gate": w_ple_gate}


def _fwd_reference(x, p, positions, norm_g, w_in, b_f, qk_norm_g, w_out, w_ple, ple_norm_g, w_ple_gate):
    h = x
    for i in range(DEPTH):
        h = hybrid_layer(h, p[i], positions, norm_g[i], w_in[i], b_f[i], qk_norm_g[i],
                         w_out[i], w_ple[i], ple_norm_g[i], w_ple_gate[i])
    return h


import jax as _jax
import jax.numpy as _jnp

TWIN_FORMAT = 'train_step'
FWD_PARAMS = ['x', 'p', 'positions', 'norm_g', 'w_in', 'b_f', 'qk_norm_g', 'w_out', 'w_ple', 'ple_norm_g', 'w_ple_gate']
TWIN_WEIGHTS = ['norm_g', 'w_in', 'b_f', 'qk_norm_g', 'w_out', 'w_ple', 'ple_norm_g', 'w_ple_gate']
TWIN_DIFF_INPUT = 'x'
TWIN_INPUTS = ['x', 'p', 'positions', 'norm_g', 'w_in', 'b_f', 'qk_norm_g', 'w_out', 'w_ple', 'ple_norm_g', 'w_ple_gate', 'loss_target', 'm_norm_g', 'm_w_in', 'm_b_f', 'm_qk_norm_g', 'm_w_out', 'm_w_ple', 'm_ple_norm_g', 'm_w_ple_gate', 'v_norm_g', 'v_w_in', 'v_b_f', 'v_qk_norm_g', 'v_w_out', 'v_w_ple', 'v_ple_norm_g', 'v_w_ple_gate']
TWIN_OUTPUTS = ['loss', 'grad_x', 'grad_norm_g', 'grad_w_in', 'grad_b_f', 'grad_qk_norm_g', 'grad_w_out', 'grad_w_ple', 'grad_ple_norm_g', 'grad_w_ple_gate', 'delta_norm_g', 'delta_w_in', 'delta_b_f', 'delta_qk_norm_g', 'delta_w_out', 'delta_w_ple', 'delta_ple_norm_g', 'delta_w_ple_gate', 'new_m_norm_g', 'new_m_w_in', 'new_m_b_f', 'new_m_qk_norm_g', 'new_m_w_out', 'new_m_w_ple', 'new_m_ple_norm_g', 'new_m_w_ple_gate', 'new_v_norm_g', 'new_v_w_in', 'new_v_b_f', 'new_v_qk_norm_g', 'new_v_w_out', 'new_v_w_ple', 'new_v_ple_norm_g', 'new_v_w_ple_gate']
TWIN_LEAF_KINDS = {'loss': 'loss', 'grad_x': 'grad_x', 'grad_norm_g': 'grad_w', 'grad_w_in': 'grad_w', 'grad_b_f': 'grad_w', 'grad_qk_norm_g': 'grad_w', 'grad_w_out': 'grad_w', 'grad_w_ple': 'grad_w', 'grad_ple_norm_g': 'grad_w', 'grad_w_ple_gate': 'grad_w', 'delta_norm_g': 'delta_w', 'delta_w_in': 'delta_w', 'delta_b_f': 'delta_w', 'delta_qk_norm_g': 'delta_w', 'delta_w_out': 'delta_w', 'delta_w_ple': 'delta_w', 'delta_ple_norm_g': 'delta_w', 'delta_w_ple_gate': 'delta_w', 'new_m_norm_g': 'new_m', 'new_m_w_in': 'new_m', 'new_m_b_f': 'new_m', 'new_m_qk_norm_g': 'new_m', 'new_m_w_out': 'new_m', 'new_m_w_ple': 'new_m', 'new_m_ple_norm_g': 'new_m', 'new_m_w_ple_gate': 'new_m', 'new_v_norm_g': 'new_v', 'new_v_w_in': 'new_v', 'new_v_b_f': 'new_v', 'new_v_qk_norm_g': 'new_v', 'new_v_w_out': 'new_v', 'new_v_w_ple': 'new_v', 'new_v_ple_norm_g': 'new_v', 'new_v_w_ple_gate': 'new_v'}


def _forward(args):
    return _fwd_reference(*[args[k] for k in FWD_PARAMS])


def _output_shape():
    def fwd():
        inp = _fwd_setup_inputs(0)
        return _fwd_reference(*[inp[k] for k in FWD_PARAMS])
    out = _jax.eval_shape(fwd)
    return out.shape, out.dtype

N_MICROBATCH = 1
ADAM_LR = 0.001
ADAM_B1 = 0.9
ADAM_B2 = 0.999
ADAM_EPS = 1e-08
ADAM_WD = 0.01
ADAM_STEP = 10
PER_EXAMPLE_BATCH_AXIS = {'x': 0, 'p': 1, 'positions': 0, 'loss_target': 0}
SHARED_INPUTS = []
_WEIGHT_DTYPES = {'norm_g': _jnp.float32, 'w_in': _jnp.float32, 'b_f': _jnp.float32, 'qk_norm_g': _jnp.float32, 'w_out': _jnp.float32, 'w_ple': _jnp.float32, 'ple_norm_g': _jnp.float32, 'w_ple_gate': _jnp.float32}
MOMENT_SCALE = {'norm_g': 1.959174e+00, 'w_in': 7.664919e-02, 'b_f': 4.018725e+01, 'qk_norm_g': 5.299322e+00, 'w_out': 7.611332e-02, 'w_ple': 8.909194e-01, 'ple_norm_g': 1.931175e+00, 'w_ple_gate': 1.007561e-01}


def _to_microbatches(a, axis):
    t = _jnp.moveaxis(a, axis, 0)
    t = t.reshape((N_MICROBATCH, t.shape[0] // N_MICROBATCH) + t.shape[1:])
    return _jnp.moveaxis(t, 1, axis + 1)


def setup_inputs(seed: int = 0) -> dict:
    inp = _fwd_setup_inputs(seed)
    key = _jax.random.fold_in(_jax.random.key(seed), 7919)
    shape, _ = _output_shape()
    out = dict(inp)
    out["loss_target"] = _jax.random.normal(_jax.random.fold_in(key, 0), shape, _jnp.float32)
    for i, name in enumerate(TWIN_WEIGHTS):
        w = inp[name].astype(_jnp.float32)
        if MOMENT_SCALE is None:
            s = _jnp.sqrt(_jnp.mean(_jnp.square(w)) + 1e-30)
        else:
            s = MOMENT_SCALE[name]
        km, kv = _jax.random.split(_jax.random.fold_in(key, i + 1))
        out[name] = w
        out["m_" + name] = s * _jax.random.normal(km, w.shape, _jnp.float32)
        out["v_" + name] = (s * s) * _jax.random.uniform(kv, w.shape, _jnp.float32, 0.5, 1.5)
    if N_MICROBATCH > 1:
        for name, axis in PER_EXAMPLE_BATCH_AXIS.items():
            out[name] = _to_microbatches(out[name], axis)
    return {'x': out['x'], 'p': out['p'], 'positions': out['positions'], 'norm_g': out['norm_g'], 'w_in': out['w_in'], 'b_f': out['b_f'], 'qk_norm_g': out['qk_norm_g'], 'w_out': out['w_out'], 'w_ple': out['w_ple'], 'ple_norm_g': out['ple_norm_g'], 'w_ple_gate': out['w_ple_gate'], 'loss_target': out['loss_target'], 'm_norm_g': out['m_norm_g'], 'm_w_in': out['m_w_in'], 'm_b_f': out['m_b_f'], 'm_qk_norm_g': out['m_qk_norm_g'], 'm_w_out': out['m_w_out'], 'm_w_ple': out['m_w_ple'], 'm_ple_norm_g': out['m_ple_norm_g'], 'm_w_ple_gate': out['m_w_ple_gate'], 'v_norm_g': out['v_norm_g'], 'v_w_in': out['v_w_in'], 'v_b_f': out['v_b_f'], 'v_qk_norm_g': out['v_qk_norm_g'], 'v_w_out': out['v_w_out'], 'v_w_ple': out['v_w_ple'], 'v_ple_norm_g': out['v_ple_norm_g'], 'v_w_ple_gate': out['v_w_ple_gate']}


def _loss(weights, diff, rest, loss_target):
    with _jax.named_scope("forward"):
        args = {**rest, TWIN_DIFF_INPUT: diff, **{k: w.astype(_WEIGHT_DTYPES[k]) for k, w in weights.items()}}
        y = _forward(args)
    with _jax.named_scope("loss_head"):
        err = _jnp.square(y.astype(_jnp.float32) - loss_target)
        return 0.5 * _jnp.sum(_jnp.mean(err, axis=-1)) if err.ndim else 0.5 * err


def _adamw(w, g, m, v):
    m = ADAM_B1 * m + (1.0 - ADAM_B1) * g
    v = ADAM_B2 * v + (1.0 - ADAM_B2) * _jnp.square(g)
    m_hat = m / (1.0 - ADAM_B1 ** ADAM_STEP)
    v_hat = v / (1.0 - ADAM_B2 ** ADAM_STEP)
    delta = -ADAM_LR * (m_hat / (_jnp.sqrt(v_hat) + ADAM_EPS) + ADAM_WD * w)
    return delta, m, v


def reference(x, p, positions, norm_g, w_in, b_f, qk_norm_g, w_out, w_ple, ple_norm_g, w_ple_gate, loss_target, m_norm_g, m_w_in, m_b_f, m_qk_norm_g, m_w_out, m_w_ple, m_ple_norm_g, m_w_ple_gate, v_norm_g, v_w_in, v_b_f, v_qk_norm_g, v_w_out, v_w_ple, v_ple_norm_g, v_w_ple_gate):
    given = dict(x=x, p=p, positions=positions, norm_g=norm_g, w_in=w_in, b_f=b_f, qk_norm_g=qk_norm_g, w_out=w_out, w_ple=w_ple, ple_norm_g=ple_norm_g, w_ple_gate=w_ple_gate, loss_target=loss_target, m_norm_g=m_norm_g, m_w_in=m_w_in, m_b_f=m_b_f, m_qk_norm_g=m_qk_norm_g, m_w_out=m_w_out, m_w_ple=m_w_ple, m_ple_norm_g=m_ple_norm_g, m_w_ple_gate=m_w_ple_gate, v_norm_g=v_norm_g, v_w_in=v_w_in, v_b_f=v_b_f, v_qk_norm_g=v_qk_norm_g, v_w_out=v_w_out, v_w_ple=v_w_ple, v_ple_norm_g=v_ple_norm_g, v_w_ple_gate=v_w_ple_gate)
    weights = {n: given[n] for n in TWIN_WEIGHTS}
    shared = {n: given[n] for n in SHARED_INPUTS}
    per_example = {n: given[n] for n in ['x', 'p', 'positions']}
    grad_fn = _jax.value_and_grad(_loss, argnums=(0, 1))

    def one_microbatch(ex, loss_target):
        ex = dict(ex)
        diff = ex.pop(TWIN_DIFF_INPUT)
        return grad_fn(weights, diff, {**shared, **ex}, loss_target)

    if N_MICROBATCH == 1:
        loss, (grad_w, grad_x) = one_microbatch(per_example, given["loss_target"])
    else:
        def body(carry, xs):
            loss_sum, grad_sum = carry
            l_k, (gw_k, gx_k) = one_microbatch(xs[0], xs[1])
            with _jax.named_scope("update"):
                return (loss_sum + l_k, _jax.tree.map(_jnp.add, grad_sum, gw_k)), gx_k

        init = (_jnp.zeros((), _jnp.float32), _jax.tree.map(_jnp.zeros_like, weights))
        (loss, grad_w), grad_x = _jax.lax.scan(body, init, (per_example, given["loss_target"]))
    with _jax.named_scope("update"):
        delta_w, new_m, new_v = {}, {}, {}
        for n in TWIN_WEIGHTS:
            delta_w[n], new_m[n], new_v[n] = _adamw(weights[n], grad_w[n], given["m_" + n], given["v_" + n])
    return (loss, grad_x, *[grad_w[n] for n in TWIN_WEIGHTS], *[delta_w[n] for n in TWIN_WEIGHTS],
            *[new_m[n] for n in TWIN_WEIGHTS], *[new_v[n] for n in TWIN_WEIGHTS])
```

```python
import functools
import math

import numpy as np
import jax
import jax.numpy as jnp
from jax import lax
from jax.experimental import pallas as pl
from jax.experimental.pallas import tpu as pltpu

F32 = jnp.float32
BF16 = jnp.bfloat16
MESH = pl.DeviceIdType.MESH

D_MODEL = 1024
HEAD_DIM = 64
D_BRANCH = 512
N_HEADS = 8
N_PAIRS = 4
N_IN = 4104
N_MAIN = 4096
N_ALL = 4224
PLE_DIM = 256
ROPE_THETA = 500000.0
ROPE_HALF = 8
EPS = 1e-6
NEG = -1e30
M_INIT = -1e29
Q_SCALE = HEAD_DIM ** -0.5
DIL_PATTERNS = ((128, 1), (512, 4), (2048, 16))
DIL_BACK = 2048
ADAM_LR, ADAM_B1, ADAM_B2, ADAM_EPS, ADAM_WD, ADAM_STEP = 0.001, 0.9, 0.999, 1e-08, 0.01, 10
VMEM_LIMIT = 56 * 1024 * 1024
LANE = 128


def _dot(a, b):
    return jnp.dot(a, b, preferred_element_type=F32)


def _dot_nt(a, b):
    return lax.dot_general(a, b, (((1,), (1,)), ((), ())), preferred_element_type=F32)


def _dot_tn(a, b):
    return lax.dot_general(a, b, (((0,), (0,)), ((), ())), preferred_element_type=F32)


def _split_dot(x, w):
    hi = x.astype(BF16)
    lo = (x - hi.astype(F32)).astype(BF16)
    return _dot(hi, w) + _dot(lo, w)


def _split3_dot(w, x):
    hi = x.astype(BF16)
    r1 = x - hi.astype(F32)
    mid = r1.astype(BF16)
    lo = (r1 - mid.astype(F32)).astype(BF16)
    return _dot(w, hi) + _dot(w, mid) + _dot(w, lo)


def _sigmoid(x):
    return 1.0 / (1.0 + jnp.exp(-x))


def _params(n_grid):
    return pltpu.CompilerParams(dimension_semantics=("arbitrary",) * n_grid,
                                vmem_limit_bytes=VMEM_LIMIT)


def _full(shape):
    nd = len(shape)
    return pl.BlockSpec(shape, lambda *_: (0,) * nd)


def _head_block_diag():
    i = np.arange(D_BRANCH)
    return jnp.asarray((i[:, None] // HEAD_DIM == i[None, :] // HEAD_DIM).astype(np.float32), BF16)


def _head_select():
    i = np.arange(2 * D_BRANCH)
    j = np.arange(LANE)
    return jnp.asarray((i[:, None] // HEAD_DIM == j[None, :]).astype(np.float32), BF16)


def _dil_bias(t, transposed):
    nb = DIL_BACK // t + 1
    qi = np.arange(t)[:, None]
    ki = np.arange(t)[None, :]
    tiles = []
    for r in range(nb):
        d = r * t + qi - ki
        mult = np.zeros((t, t), np.int64)
        for window, dil in DIL_PATTERNS:
            mult += ((d >= 0) & (d <= window) & (d % dil == 0)).astype(np.int64)
        b = np.where(mult > 0, np.log(np.maximum(mult, 1)), NEG).astype(np.float32)
        tiles.append(b.T if transposed else b)
    return jnp.asarray(np.stack(tiles))


def _rope_tables(positions):
    inv_freq = ROPE_THETA ** (-jnp.arange(ROPE_HALF, dtype=F32) / ROPE_HALF)
    ang = positions.astype(F32)[:, None] * inv_freq
    cos, sin = jnp.cos(ang), jnp.sin(ang)
    s = positions.shape[0]
    rest = HEAD_DIM - 2 * ROPE_HALF
    one, zero, zero8 = jnp.ones((s, rest), F32), jnp.zeros((s, rest), F32), jnp.zeros((s, ROPE_HALF), F32)
    c = jnp.concatenate([cos, cos, one], axis=1)
    s1 = jnp.concatenate([zero8, sin, zero], axis=1)
    s2 = jnp.concatenate([-sin, zero8, zero], axis=1)
    return tuple(jnp.tile(t, (1, 2)) for t in (c, s1, s2))


def _rope_fwd(x, c, s1, s2):
    return x * c + pltpu.roll(x, ROPE_HALF, 1) * s1 + pltpu.roll(x, LANE - ROPE_HALF, 1) * s2


def _rope_bwd(dy, c, s1, s2):
    return dy * c + pltpu.roll(dy * s1, LANE - ROPE_HALF, 1) + pltpu.roll(dy * s2, ROPE_HALF, 1)


def _log_sigmoid(x):
    return jnp.minimum(x, 0.0) - jnp.log(1.0 + jnp.exp(-jnp.abs(x)))


def _inproj(l, h, norm_g, w_all, b_f, qkg, bsum, rope, tm):
    s = h.shape[0]
    rc, rs1, rs2 = rope

    def body(h_ref, g_ref, w_ref, bf_ref, qkg_ref, bsum_ref, rc_ref, rs1_ref, rs2_ref,
             u_ref, z_ref, qa_ref, ka_ref, va_ref, qb_ref, kb_ref, vb_ref, lf_ref):
        hh = h_ref[...]
        r = lax.rsqrt(jnp.mean(hh * hh, axis=-1, keepdims=True) + EPS)
        u = (hh * r * g_ref[...]).astype(BF16)
        u_ref[...] = u
        for k in range(N_ALL // LANE // 3):
            cols = slice(3 * LANE * k, 3 * LANE * (k + 1))
            z_ref[:, cols] = _dot(u, w_ref[:, cols])
        bs = bsum_ref[...]

        def head_norm(x, row):
            ms = _split_dot(x * x, bs) * (1.0 / HEAD_DIM)
            return x * lax.rsqrt(ms + EPS) * qkg_ref[row:row + 1, :]

        def seg(k):
            return z_ref[:, D_BRANCH * k:D_BRANCH * (k + 1)]

        qa_ref[...] = (head_norm(seg(0), 0) * Q_SCALE).astype(BF16)
        ka_ref[...] = head_norm(seg(1), 1).astype(BF16)
        va_ref[...] = seg(2).astype(BF16)
        qn = head_norm(seg(4), 2) * Q_SCALE
        kn = head_norm(seg(5), 3)
        c, s1, s2 = rc_ref[...], rs1_ref[...], rs2_ref[...]
        for k in range(D_BRANCH // LANE):
            cols = slice(LANE * k, LANE * (k + 1))
            qb_ref[:, cols] = _rope_fwd(qn[:, cols], c, s1, s2).astype(BF16)
            kb_ref[:, cols] = _rope_fwd(kn[:, cols], c, s1, s2).astype(BF16)
        vb_ref[...] = seg(6).astype(BF16)
        lf_ref[...] = _log_sigmoid(z_ref[:, N_MAIN:N_ALL] + bf_ref[...])

    row = lambda w: pl.BlockSpec((tm, w), lambda i: (i, 0))
    bf = lambda: jax.ShapeDtypeStruct((s, D_BRANCH), BF16)
    return pl.pallas_call(
        body, name=f"inproj_l{l}", grid=(s // tm,),
        in_specs=[row(D_MODEL), _full((1, D_MODEL)), _full((D_MODEL, N_ALL)), _full((1, LANE)),
                  _full((8, D_BRANCH)), _full((D_BRANCH, D_BRANCH)), row(LANE), row(LANE), row(LANE)],
        out_specs=[row(D_MODEL), row(N_ALL)] + [row(D_BRANCH)] * 6 + [row(LANE)],
        out_shape=[jax.ShapeDtypeStruct((s, D_MODEL), BF16), jax.ShapeDtypeStruct((s, N_ALL), F32),
                   bf(), bf(), bf(), bf(), bf(), bf(), jax.ShapeDtypeStruct((s, LANE), F32)],
        compiler_params=_params(1),
    )(h, norm_g, w_all, b_f, qkg, bsum, rc, rs1, rs2)


def _tri(t, upper):
    a = lax.broadcasted_iota(jnp.int32, (t, t), 0)
    b = lax.broadcasted_iota(jnp.int32, (t, t), 1)
    return jnp.where((b >= a) if upper else (b <= a), 1.0, 0.0).astype(BF16)


def _cumsum_gates(l, logf, t):
    s = logf.shape[0]

    def body(lf_ref, cs_ref, ct_ref, carry):
        @pl.when(pl.program_id(0) == 0)
        def _():
            carry[...] = jnp.zeros_like(carry)

        x = lf_ref[...]
        c = _split3_dot(_tri(t, False), x) + carry[0:1, :]
        carry[...] = jnp.broadcast_to(c[t - 1:t, :], carry.shape)
        ct = c.T
        for p in range(N_PAIRS):
            cs_ref[:, LANE * p:LANE * (p + 1)] = c if p == 0 else pltpu.roll(c, LANE - 2 * p, 1)
            ct_ref[p, :, :] = ct[2 * p:2 * p + 2, :]

    return pl.pallas_call(
        body, name=f"cumsum_l{l}", grid=(s // t,),
        in_specs=[pl.BlockSpec((t, LANE), lambda i: (i, 0))],
        out_specs=[pl.BlockSpec((t, N_PAIRS * LANE), lambda i: (i, 0)),
                   pl.BlockSpec((N_PAIRS, 2, t), lambda i: (0, 0, i))],
        out_shape=[jax.ShapeDtypeStruct((s, N_PAIRS * LANE), F32), jax.ShapeDtypeStruct((N_PAIRS, 2, s), F32)],
        scratch_shapes=[pltpu.VMEM((8, LANE), F32)],
        compiler_params=_params(1),
    )(logf)


def _rev_cumsum_gates(l, dc_spread, drow, t):
    s = dc_spread.shape[0]
    n = s // t

    def body(dc_ref, drow_ref, out_ref, carry):
        @pl.when(pl.program_id(0) == 0)
        def _():
            carry[...] = jnp.zeros_like(carry)

        lane = lax.broadcasted_iota(jnp.int32, (t, LANE), 1)
        rows = jnp.concatenate([drow_ref[p] for p in range(N_PAIRS)] + [jnp.zeros((LANE - N_HEADS, t), F32)], axis=0)
        x = rows.T
        for p in range(N_PAIRS):
            xp = jnp.where(lane < 2, dc_ref[:, LANE * p:LANE * (p + 1)], 0.0)
            x = x + (xp if p == 0 else pltpu.roll(xp, 2 * p, 1))
        out = _split3_dot(_tri(t, True), x) + carry[0:1, :]
        out_ref[...] = out
        carry[...] = jnp.broadcast_to(out[0:1, :], carry.shape)

    return pl.pallas_call(
        body, name=f"revcumsum_l{l}", grid=(n,),
        in_specs=[pl.BlockSpec((t, N_PAIRS * LANE), lambda i: (n - 1 - i, 0)),
                  pl.BlockSpec((N_PAIRS, 2, t), lambda i: (0, 0, n - 1 - i))],
        out_specs=pl.BlockSpec((t, LANE), lambda i: (n - 1 - i, 0)),
        out_shape=jax.ShapeDtypeStruct((s, LANE), F32),
        scratch_shapes=[pltpu.VMEM((8, LANE), F32)],
        compiler_params=_params(1),
    )(dc_spread, drow)


def _attn_fwd(name, fox, q, k, v, extra, t):
    s = q.shape[0]
    nq = s // t
    nb = DIL_BACK // t + 1

    def body(*refs):
        if fox:
            q_ref, k_ref, v_ref, ccol_ref, crow_ref, o_ref, lse_ref, m_scr, l_scr, acc_scr = refs
        else:
            q_ref, k_ref, v_ref, bias_ref, o_ref, lse_ref, m_scr, l_scr, acc_scr = refs
        i = pl.program_id(1)
        lane = lax.broadcasted_iota(jnp.int32, (t, LANE), 1)
        first = lane < HEAD_DIM
        qq = q_ref[...]
        zero = jnp.zeros_like(qq)
        qh = (jnp.where(first, qq, zero), jnp.where(first, zero, qq))
        m_scr[...] = jnp.full(m_scr.shape, M_INIT, F32)
        l_scr[...] = jnp.zeros_like(l_scr)
        acc_scr[...] = jnp.zeros_like(acc_scr)
        if fox:
            ccol = ccol_ref[...]

        def step(j, diag):
            rows = pl.ds(pl.multiple_of(j * t, t), t)
            ks = k_ref[rows, :]
            vs = v_ref[rows, :]
            for h in range(2):
                sc = _dot_nt(qh[h], ks)
                if fox:
                    sc = sc + (ccol[:, h:h + 1] - crow_ref[h:h + 1, rows])
                    if diag:
                        qi = lax.broadcasted_iota(jnp.int32, (t, t), 0)
                        ki = lax.broadcasted_iota(jnp.int32, (t, t), 1)
                        sc = jnp.where(ki <= qi, sc, NEG)
                else:
                    sc = sc + bias_ref[i - j]
                m_old = m_scr[h]
                m_new = jnp.maximum(m_old, jnp.max(sc, axis=1, keepdims=True))
                alpha = jnp.exp(m_old - m_new)
                p = jnp.exp(sc - m_new)
                l_scr[h] = alpha * l_scr[h] + jnp.sum(p, axis=1, keepdims=True)
                acc_scr[h] = alpha * acc_scr[h] + _dot(p.astype(BF16), vs)
                m_scr[h] = m_new

        if fox:
            lax.fori_loop(0, i, lambda j, c: (step(j, False), c)[1], 0)
            step(i, True)
        else:
            lax.fori_loop(jnp.maximum(i - (nb - 1), 0), i + 1, lambda j, c: (step(j, False), c)[1], 0)

        o_ref[...] = jnp.where(first, acc_scr[0] / l_scr[0], acc_scr[1] / l_scr[1])
        lse = [m_scr[h] + jnp.log(l_scr[h]) for h in range(2)]
        x = jnp.where(lane == 0, lse[0], jnp.where(lane == 1, lse[1], 0.0))
        lse_ref[...] = x.T[0:2, :]

    qspec = pl.BlockSpec((t, LANE), lambda hp, i: (i, hp))
    kspec = pl.BlockSpec((s, LANE), lambda hp, i: (0, hp))
    rowspec = pl.BlockSpec((None, 2, s), lambda hp, i: (hp, 0, 0))
    in_specs = [qspec, kspec, kspec]
    if fox:
        in_specs += [qspec, rowspec]
    else:
        in_specs += [_full((nb, t, t))]
    return pl.pallas_call(
        body, name=name, grid=(N_PAIRS, nq),
        in_specs=in_specs,
        out_specs=[qspec, pl.BlockSpec((None, 2, t), lambda hp, i: (hp, 0, i))],
        out_shape=[jax.ShapeDtypeStruct((s, D_BRANCH), F32), jax.ShapeDtypeStruct((N_PAIRS, 2, s), F32)],
        scratch_shapes=[pltpu.VMEM((2, t, 1), F32), pltpu.VMEM((2, t, 1), F32), pltpu.VMEM((2, t, LANE), F32)],
        compiler_params=_params(2),
    )(q, k, v, *extra)


def _attn_bwd(name, fox, q, k, v, do, lse_t, delta_t, pair_offset, extra, t):
    s = q.shape[0]
    nk = s // t
    nb = DIL_BACK // t + 1

    def body(*refs):
        if fox:
            (q_ref, k_ref, v_ref, do_ref, lse_ref, delta_ref, ccol_ref, crow_ref,
             dqt_ref, dk_ref, dv_ref, dc_ref, drow_ref) = refs
        else:
            q_ref, k_ref, v_ref, do_ref, lse_ref, delta_ref, bias_ref, dqt_ref, dk_ref, dv_ref = refs
        j = pl.program_id(1)

        @pl.when(j == 0)
        def _():
            dqt_ref[...] = jnp.zeros_like(dqt_ref)
            if fox:
                drow_ref[...] = jnp.zeros_like(drow_ref)

        lane = lax.broadcasted_iota(jnp.int32, (t, LANE), 1)
        first = lane < HEAD_DIM
        ks = k_ref[...]
        vs = v_ref[...]
        kt = ks.astype(F32).T
        sub = lax.broadcasted_iota(jnp.int32, (LANE, t), 0)
        kth = (jnp.where(sub < HEAD_DIM, kt, 0.0).astype(BF16), jnp.where(sub < HEAD_DIM, 0.0, kt).astype(BF16))
        dk_ref[...] = jnp.zeros_like(dk_ref)
        dv_ref[...] = jnp.zeros_like(dv_ref)
        if fox:
            dc_ref[...] = jnp.zeros_like(dc_ref)
            ccol = ccol_ref[...]

        def step(i, diag):
            rows = pl.ds(pl.multiple_of(i * t, t), t)
            qq = q_ref[rows, :]
            dd = do_ref[rows, :]
            zero = jnp.zeros_like(qq)
            qh = (jnp.where(first, qq, zero), jnp.where(first, zero, qq))
            dh = (jnp.where(first, dd, zero), jnp.where(first, zero, dd))
            for h in range(2):
                st = _dot_nt(ks, qh[h])
                if fox:
                    st = st + (crow_ref[h:h + 1, rows] - ccol[:, h:h + 1])
                    if diag:
                        ki = lax.broadcasted_iota(jnp.int32, (t, t), 0)
                        qi = lax.broadcasted_iota(jnp.int32, (t, t), 1)
                        st = jnp.where(ki <= qi, st, NEG)
                else:
                    st = st + bias_ref[i - j]
                pt = jnp.exp(st - lse_ref[h:h + 1, rows])
                dpt = _dot_nt(vs, dh[h])
                dst = pt * (dpt - delta_ref[h:h + 1, rows])
                dv_ref[...] += _dot(pt.astype(BF16), dh[h])
                dsb = dst.astype(BF16)
                dk_ref[...] += _dot(dsb, qh[h])
                dqt_ref[:, rows] += _dot(kth[h], dsb)
                if fox:
                    dc_ref[...] -= jnp.where(lane == h, jnp.sum(dst, axis=1, keepdims=True), 0.0)
                    drow_ref[h:h + 1, rows] += jnp.sum(dst, axis=0, keepdims=True)

        if fox:
            step(j, True)
            lax.fori_loop(j + 1, nk, lambda i, c: (step(i, False), c)[1], 0)
        else:
            lax.fori_loop(j, jnp.minimum(j + nb, nk), lambda i, c: (step(i, False), c)[1], 0)

    kspec = pl.BlockSpec((t, LANE), lambda hp, j: (j, hp))
    qspec = pl.BlockSpec((s, LANE), lambda hp, j: (0, hp))
    rowspec = pl.BlockSpec((None, 2, s), lambda hp, j: (hp, 0, 0))
    drowspec = pl.BlockSpec((None, 2, s), lambda hp, j: (hp + pair_offset, 0, 0))
    in_specs = [qspec, kspec, kspec, qspec, rowspec, drowspec]
    out_specs = [pl.BlockSpec((LANE, s), lambda hp, j: (hp, 0)), kspec, kspec]
    out_shape = [jax.ShapeDtypeStruct((D_BRANCH, s), F32), jax.ShapeDtypeStruct((s, D_BRANCH), F32),
                 jax.ShapeDtypeStruct((s, D_BRANCH), F32)]
    if fox:
        in_specs += [kspec, rowspec]
        out_specs += [kspec, rowspec]
        out_shape += [jax.ShapeDtypeStruct((s, N_PAIRS * LANE), F32), jax.ShapeDtypeStruct((N_PAIRS, 2, s), F32)]
    else:
        in_specs += [_full((nb, t, t))]
    return pl.pallas_call(
        body, name=name, grid=(N_PAIRS, nk), in_specs=in_specs, out_specs=out_specs, out_shape=out_shape,
        compiler_params=_params(2),
    )(q, k, v, do, lse_t, delta_t, *extra)


def _silu(x):
    return x * _sigmoid(x)


def _outproj(l, h, oa, ob, z, p_i, w_out, ple_g, w_gate, w_ple, tm):
    s = h.shape[0]

    def body(h_ref, oa_ref, ob_ref, ga_ref, gb_ref, p_ref, wo_ref, pg_ref, wg_ref, wp_ref,
             h2_ref, e_ref, gate_ref, out_ref):
        a = jnp.concatenate([oa_ref[...] * _silu(ga_ref[...]), ob_ref[...] * _silu(gb_ref[...])], axis=1)
        h2 = h_ref[...] + _dot(a.astype(BF16), wo_ref[...])
        h2_ref[...] = h2
        r = lax.rsqrt(jnp.mean(h2 * h2, axis=-1, keepdims=True) + EPS)
        n2 = (h2 * r * pg_ref[...]).astype(BF16)
        gate = _sigmoid(_dot(n2, wg_ref[...]))
        e = _dot(p_ref[...].astype(BF16), wp_ref[...])
        e_ref[...] = e
        gate_ref[...] = gate
        out_ref[...] = h2 + e * gate

    row = lambda w: pl.BlockSpec((tm, w), lambda i: (i, 0))
    zcol = lambda k: pl.BlockSpec((tm, D_BRANCH), lambda i: (i, k))
    f = lambda: jax.ShapeDtypeStruct((s, D_MODEL), F32)
    return pl.pallas_call(
        body, name=f"outproj_l{l}", grid=(s // tm,),
        in_specs=[row(D_MODEL), row(D_BRANCH), row(D_BRANCH), zcol(3), zcol(7), row(PLE_DIM),
                  _full((D_MODEL, D_MODEL)), _full((1, D_MODEL)), _full((D_MODEL, D_MODEL)),
                  _full((PLE_DIM, D_MODEL))],
        out_specs=[row(D_MODEL)] * 4, out_shape=[f(), f(), f(), f()],
        compiler_params=_params(1),
    )(h, oa, ob, z, z, p_i, w_out, ple_g, w_gate, w_ple)


def _loss_head(y, target, tm):
    s = y.shape[0]

    def body(y_ref, t_ref, acc_ref, dy_ref):
        @pl.when(pl.program_id(0) == 0)
        def _():
            acc_ref[...] = jnp.zeros_like(acc_ref)

        err = y_ref[...] - t_ref[...]
        dy_ref[...] = err * (1.0 / D_MODEL)
        e2 = err * err
        part = jnp.zeros((8, LANE), F32)
        for r in range(tm // 8):
            for c in range(D_MODEL // LANE):
                part = part + e2[8 * r:8 * (r + 1), LANE * c:LANE * (c + 1)]
        acc_ref[...] += part

    row = pl.BlockSpec((tm, D_MODEL), lambda i: (i, 0))
    return pl.pallas_call(
        body, name="loss_head", grid=(s // tm,), in_specs=[row, row],
        out_specs=[_full((8, LANE)), row],
        out_shape=[jax.ShapeDtypeStruct((8, LANE), F32), jax.ShapeDtypeStruct((s, D_MODEL), F32)],
        compiler_params=_params(1),
    )(y, target)


def _outproj_bwd(l, dout, h2, e, gate, p_i, oa, ob, z, w_out_t, ple_g, w_gate_t, hsel, tm):
    s = dout.shape[0]

    def body(do_ref, h2_ref, e_ref, gate_ref, p_ref, oa_ref, ob_ref, ga_ref, gb_ref, wot_ref, pg_ref,
             wgt_ref, hsel_ref,
             dh2_ref, doa_ref, dob_ref, dga_ref, dgb_ref, delta_ref, dwo_ref, dwg_ref, dwp_ref, dpg_ref):
        @pl.when(pl.program_id(0) == 0)
        def _():
            dwo_ref[...] = jnp.zeros_like(dwo_ref)
            dwg_ref[...] = jnp.zeros_like(dwg_ref)
            dwp_ref[...] = jnp.zeros_like(dwp_ref)
            dpg_ref[...] = jnp.zeros_like(dpg_ref)

        dho = do_ref[...]
        g = gate_ref[...]
        de = (dho * g).astype(BF16)
        dwp_ref[...] += _dot_tn(p_ref[...].astype(BF16), de)
        dpre = (dho * e_ref[...] * g * (1.0 - g)).astype(BF16)
        h2 = h2_ref[...]
        pg = pg_ref[...]
        r = lax.rsqrt(jnp.mean(h2 * h2, axis=-1, keepdims=True) + EPS)
        n2 = (h2 * r * pg).astype(BF16)
        dwg_ref[...] += _dot_tn(n2, dpre)
        dn2 = _dot(dpre, wgt_ref[...])
        dpg_ref[0:1, :] += jnp.sum(dn2 * h2 * r, axis=0, keepdims=True)
        wv = dn2 * pg
        dh2 = dho + r * wv - h2 * (r * r * r) * jnp.mean(wv * h2, axis=-1, keepdims=True)
        dh2_ref[...] = dh2
        dh2b = dh2.astype(BF16)
        ga, gb, oa, ob = ga_ref[...], gb_ref[...], oa_ref[...], ob_ref[...]
        sga, sgb = _sigmoid(ga), _sigmoid(gb)
        a = jnp.concatenate([oa * ga * sga, ob * gb * sgb], axis=1).astype(BF16)
        dwo_ref[...] += _dot_tn(a, dh2b)
        da = _dot(dh2b, wot_ref[...])
        da_a, da_b = da[:, :D_BRANCH], da[:, D_BRANCH:]
        doa = da_a * ga * sga
        dob = da_b * gb * sgb
        doa_ref[...] = doa.astype(BF16)
        dob_ref[...] = dob.astype(BF16)
        dga_ref[...] = (da_a * oa * sga * (1.0 + ga * (1.0 - sga))).astype(BF16)
        dgb_ref[...] = (da_b * ob * sgb * (1.0 + gb * (1.0 - sgb))).astype(BF16)
        prod = jnp.concatenate([doa * oa, dob * ob], axis=1)
        dt = _split_dot(prod, hsel_ref[...]).T
        for pp in range(2 * N_PAIRS):
            delta_ref[pp, :, :] = dt[2 * pp:2 * pp + 2, :]

    row = lambda w: pl.BlockSpec((tm, w), lambda i: (i, 0))
    zcol = lambda k: pl.BlockSpec((tm, D_BRANCH), lambda i: (i, k))
    return pl.pallas_call(
        body, name=f"outproj_bwd_l{l}", grid=(s // tm,),
        in_specs=[row(D_MODEL)] * 4 + [row(PLE_DIM), row(D_BRANCH), row(D_BRANCH), zcol(3), zcol(7),
                                        _full((D_MODEL, D_MODEL)), _full((1, D_MODEL)), _full((D_MODEL, D_MODEL)),
                                        _full((2 * D_BRANCH, LANE))],
        out_specs=[row(D_MODEL)] + [row(D_BRANCH)] * 4
        + [pl.BlockSpec((2 * N_PAIRS, 2, tm), lambda i: (0, 0, i)), _full((D_MODEL, D_MODEL)),
           _full((D_MODEL, D_MODEL)), _full((PLE_DIM, D_MODEL)), _full((8, D_MODEL))],
        out_shape=[jax.ShapeDtypeStruct((s, D_MODEL), F32)] + [jax.ShapeDtypeStruct((s, D_BRANCH), BF16)] * 4
        + [jax.ShapeDtypeStruct((2 * N_PAIRS, 2, s), F32), jax.ShapeDtypeStruct((D_MODEL, D_MODEL), F32),
           jax.ShapeDtypeStruct((D_MODEL, D_MODEL), F32), jax.ShapeDtypeStruct((PLE_DIM, D_MODEL), F32),
           jax.ShapeDtypeStruct((8, D_MODEL), F32)],
        compiler_params=_params(1),
    )(dout, h2, e, gate, p_i, oa, ob, z, z, w_out_t, ple_g, w_gate_t, hsel)


def _inproj_bwd_prep(l, z, dqt_a, dk_a, dv_a, dqt_b, dk_b, dv_b, dga, dgb, dlogf, b_f, qkg, bsum, rope, tm):
    s = z.shape[0]
    rc, rs1, rs2 = rope

    def body(z_ref, dqta_ref, dka_ref, dva_ref, dqtb_ref, dkb_ref, dvb_ref, dga_ref, dgb_ref, dlf_ref,
             bf_ref, qkg_ref, bsum_ref, rc_ref, rs1_ref, rs2_ref, dz_ref, dqkg_ref, dbf_ref):
        @pl.when(pl.program_id(0) == 0)
        def _():
            dqkg_ref[...] = jnp.zeros_like(dqkg_ref)
            dbf_ref[...] = jnp.zeros_like(dbf_ref)

        bs = bsum_ref[...]
        c, s1, s2 = rc_ref[...], rs1_ref[...], rs2_ref[...]

        def unrope(dy):
            return jnp.concatenate([_rope_bwd(dy[:, LANE * k:LANE * (k + 1)], c, s1, s2)
                                    for k in range(D_BRANCH // LANE)], axis=1)

        def norm_bwd(k, row, dy):
            x = z_ref[:, D_BRANCH * k:D_BRANCH * (k + 1)]
            r = lax.rsqrt(_split_dot(x * x, bs) * (1.0 / HEAD_DIM) + EPS)
            dqkg_ref[row:row + 1, :] += jnp.sum(dy * x * r, axis=0, keepdims=True)
            w = dy * qkg_ref[row:row + 1, :]
            dx = r * w - x * (r * r * r) * (_split_dot(w * x, bs) * (1.0 / HEAD_DIM))
            dz_ref[:, D_BRANCH * k:D_BRANCH * (k + 1)] = dx.astype(BF16)

        norm_bwd(0, 0, dqta_ref[...].T * Q_SCALE)
        norm_bwd(1, 1, dka_ref[...])
        dz_ref[:, 2 * D_BRANCH:3 * D_BRANCH] = dva_ref[...].astype(BF16)
        dz_ref[:, 3 * D_BRANCH:4 * D_BRANCH] = dga_ref[...]
        norm_bwd(4, 2, unrope(dqtb_ref[...].T * Q_SCALE))
        norm_bwd(5, 3, unrope(dkb_ref[...]))
        dz_ref[:, 6 * D_BRANCH:7 * D_BRANCH] = dvb_ref[...].astype(BF16)
        dz_ref[:, 7 * D_BRANCH:8 * D_BRANCH] = dgb_ref[...]
        dfa = dlf_ref[...] * _sigmoid(-(z_ref[:, N_MAIN:N_ALL] + bf_ref[...]))
        dz_ref[:, N_MAIN:N_ALL] = dfa.astype(BF16)
        dbf_ref[0:1, :] += jnp.sum(dfa, axis=0, keepdims=True)

    row = lambda w: pl.BlockSpec((tm, w), lambda i: (i, 0))
    colt = pl.BlockSpec((D_BRANCH, tm), lambda i: (0, i))
    return pl.pallas_call(
        body, name=f"inproj_bwd_prep_l{l}", grid=(s // tm,),
        in_specs=[row(N_ALL), colt, row(D_BRANCH), row(D_BRANCH), colt, row(D_BRANCH), row(D_BRANCH),
                  row(D_BRANCH), row(D_BRANCH), row(LANE), _full((1, LANE)), _full((8, D_BRANCH)),
                  _full((D_BRANCH, D_BRANCH)), row(LANE), row(LANE), row(LANE)],
        out_specs=[row(N_ALL), _full((8, D_BRANCH)), _full((8, LANE))],
        out_shape=[jax.ShapeDtypeStruct((s, N_ALL), BF16), jax.ShapeDtypeStruct((8, D_BRANCH), F32),
                   jax.ShapeDtypeStruct((8, LANE), F32)],
        compiler_params=_params(1),
    )(z, dqt_a, dk_a, dv_a, dqt_b, dk_b, dv_b, dga, dgb, dlogf, b_f, qkg, bsum, rc, rs1, rs2)


def _inproj_bwd_dx(l, dz, w_all_t, h, norm_g, dh2, tm):
    s = dz.shape[0]

    def body(dz_ref, wt_ref, h_ref, g_ref, dh2_ref, dh_ref, dg_ref):
        @pl.when(pl.program_id(0) == 0)
        def _():
            dg_ref[...] = jnp.zeros_like(dg_ref)

        du = _dot(dz_ref[...], wt_ref[...])
        hh = h_ref[...]
        g = g_ref[...]
        r = lax.rsqrt(jnp.mean(hh * hh, axis=-1, keepdims=True) + EPS)
        dg_ref[0:1, :] += jnp.sum(du * hh * r, axis=0, keepdims=True)
        wv = du * g
        dh_ref[...] = dh2_ref[...] + r * wv - hh * (r * r * r) * jnp.mean(wv * hh, axis=-1, keepdims=True)

    row = lambda w: pl.BlockSpec((tm, w), lambda i: (i, 0))
    return pl.pallas_call(
        body, name=f"inproj_bwd_dx_l{l}", grid=(s // tm,),
        in_specs=[row(N_ALL), _full((N_ALL, D_MODEL)), row(D_MODEL), _full((1, D_MODEL)), row(D_MODEL)],
        out_specs=[row(D_MODEL), _full((8, D_MODEL))],
        out_shape=[jax.ShapeDtypeStruct((s, D_MODEL), F32), jax.ShapeDtypeStruct((8, D_MODEL), F32)],
        compiler_params=_params(1),
    )(dz, w_all_t, h, norm_g, dh2)


def _inproj_bwd_dw(l, u, dz, tm, tn):
    s = u.shape[0]

    def body(u_ref, dz_ref, dw_ref):
        @pl.when(pl.program_id(1) == 0)
        def _():
            dw_ref[...] = jnp.zeros_like(dw_ref)

        dw_ref[...] += _dot_tn(u_ref[...], dz_ref[...])

    return pl.pallas_call(
        body, name=f"inproj_bwd_dw_l{l}", grid=(N_ALL // tn, s // tm),
        in_specs=[pl.BlockSpec((tm, D_MODEL), lambda n, i: (i, 0)), pl.BlockSpec((tm, tn), lambda n, i: (i, n))],
        out_specs=pl.BlockSpec((D_MODEL, tn), lambda n, i: (0, n)),
        out_shape=jax.ShapeDtypeStruct((D_MODEL, N_ALL), F32),
        compiler_params=_params(2),
    )(u, dz)


def _adamw_math(w, g, m, v):
    m = ADAM_B1 * m + (1.0 - ADAM_B1) * g
    v = ADAM_B2 * v + (1.0 - ADAM_B2) * (g * g)
    m_hat = m / (1.0 - ADAM_B1 ** ADAM_STEP)
    v_hat = v / (1.0 - ADAM_B2 ** ADAM_STEP)
    delta = -ADAM_LR * (m_hat / (jnp.sqrt(v_hat) + ADAM_EPS) + ADAM_WD * w)
    return delta, m, v


def _adamw(name, w, g, m, v):
    nl, r, c = w.shape
    tr = 256 if r % 256 == 0 else r

    def body(w_ref, g_ref, m_ref, v_ref, d_ref, nm_ref, nv_ref):
        d, nm, nv = _adamw_math(w_ref[...], g_ref[...], m_ref[...], v_ref[...])
        d_ref[...] = d
        nm_ref[...] = nm
        nv_ref[...] = nv

    spec = pl.BlockSpec((None, tr, c), lambda a, b: (a, b, 0))
    shp = jax.ShapeDtypeStruct(w.shape, F32)
    return pl.pallas_call(
        body, name=name, grid=(nl, r // tr), in_specs=[spec] * 4, out_specs=[spec] * 3,
        out_shape=[shp, shp, shp], compiler_params=_params(2),
    )(w, g, m, v)


def _add_pair(name, a, b):
    n, r, c = a.shape
    tr = 256 if r % 256 == 0 else r

    def body(a_ref, b_ref, o_ref):
        o_ref[...] = a_ref[...] + b_ref[...]

    spec = pl.BlockSpec((None, tr, c), lambda i, j: (i, j, 0))
    return pl.pallas_call(
        body, name=name, grid=(n, r // tr), in_specs=[spec, spec], out_specs=spec,
        out_shape=jax.ShapeDtypeStruct(a.shape, F32), compiler_params=_params(2),
    )(a, b)


def _sum_slots(name, x):
    n, r, c = x.shape
    tr = 256 if r % 256 == 0 else r

    def body(x_ref, o_ref):
        o_ref[...] = ((x_ref[0] + x_ref[1]) + x_ref[2]) + x_ref[3]

    return pl.pallas_call(
        body, name=name, grid=(r // tr,),
        in_specs=[pl.BlockSpec((n, tr, c), lambda j: (0, j, 0))],
        out_specs=pl.BlockSpec((tr, c), lambda j: (j, 0)),
        out_shape=jax.ShapeDtypeStruct((r, c), F32), compiler_params=_params(1),
    )(x)


def _me():
    return lax.axis_index("x"), lax.axis_index("y"), lax.axis_index("c")


def _other_chips(x, y):
    return [(1 - x, y), (x, 1 - y), (1 - x, 1 - y)]


ANY = pl.BlockSpec(memory_space=pl.ANY)
VMEM_SPEC = pl.BlockSpec(memory_space=pltpu.VMEM)


def _gather_weights(shards):
    n = len(shards)

    def body(*refs):
        ins, outs, stage = refs[:n], refs[n:2 * n], refs[2 * n:3 * n]
        send_sems, recv_sems, local_sems = refs[3 * n:]
        x, y, c = _me()
        k = 2 * x + y
        copies = []
        for t in range(n):
            stage[t][...] = ins[t][...].astype(BF16)
            cp = pltpu.make_async_copy(stage[t], outs[t].at[k], local_sems.at[t])
            cp.start()
            copies.append(cp)
        remote = []
        for t in range(n):
            for j, (px, py) in enumerate(_other_chips(x, y)):
                cp = pltpu.make_async_remote_copy(
                    src_ref=stage[t], dst_ref=outs[t].at[k], send_sem=send_sems.at[3 * t + j],
                    recv_sem=recv_sems.at[3 * t + j], device_id=(px, py, c), device_id_type=MESH)
                cp.start()
                remote.append(cp)
        for cp in copies:
            cp.wait()
        for cp in remote:
            cp.wait()

    return pl.pallas_call(
        body, name="gather_weights",
        in_specs=[VMEM_SPEC] * n, out_specs=[ANY] * n,
        out_shape=[jax.ShapeDtypeStruct((4,) + s.shape, BF16) for s in shards],
        scratch_shapes=[pltpu.VMEM(s.shape, BF16) for s in shards]
        + [pltpu.SemaphoreType.DMA((3 * n,)), pltpu.SemaphoreType.DMA((3 * n,)), pltpu.SemaphoreType.DMA((n,))],
        compiler_params=pltpu.CompilerParams(vmem_limit_bytes=VMEM_LIMIT),
    )(*shards)


def _swap_layers(grads):
    n = len(grads)

    def body(*refs):
        ins, outs = refs[:n], refs[n:2 * n]
        send_sems, recv_sems = refs[2 * n:]
        x, y, c = _me()
        copies = []
        for t in range(n):
            cp = pltpu.make_async_remote_copy(
                src_ref=ins[t].at[1 - c], dst_ref=outs[t], send_sem=send_sems.at[t], recv_sem=recv_sems.at[t],
                device_id=(x, y, 1 - c), device_id_type=MESH)
            cp.start()
            copies.append(cp)
        for cp in copies:
            cp.wait()

    return pl.pallas_call(
        body, name="reduce_swap_layers", in_specs=[ANY] * n, out_specs=[ANY] * n,
        out_shape=[jax.ShapeDtypeStruct(g.shape[1:], F32) for g in grads],
        scratch_shapes=[pltpu.SemaphoreType.DMA((n,)), pltpu.SemaphoreType.DMA((n,))],
    )(*grads)


def _scatter_chips(parts):
    n = len(parts)

    def body(*refs):
        ins, outs = refs[:n], refs[n:2 * n]
        send_sems, recv_sems, local_sems = refs[2 * n:]
        x, y, c = _me()
        k = 2 * x + y
        local, remote = [], []
        for t in range(n):
            cp = pltpu.make_async_copy(ins[t].at[k], outs[t].at[k], local_sems.at[t])
            cp.start()
            local.append(cp)
            for j, (px, py) in enumerate(_other_chips(x, y)):
                cp = pltpu.make_async_remote_copy(
                    src_ref=ins[t].at[2 * px + py], dst_ref=outs[t].at[k], send_sem=send_sems.at[3 * t + j],
                    recv_sem=recv_sems.at[3 * t + j], device_id=(px, py, c), device_id_type=MESH)
                cp.start()
                remote.append(cp)
        for cp in local:
            cp.wait()
        for cp in remote:
            cp.wait()

    return pl.pallas_call(
        body, name="reduce_scatter_chips", in_specs=[ANY] * n, out_specs=[ANY] * n,
        out_shape=[jax.ShapeDtypeStruct(p.shape, F32) for p in parts],
        scratch_shapes=[pltpu.SemaphoreType.DMA((3 * n,)), pltpu.SemaphoreType.DMA((3 * n,)),
                        pltpu.SemaphoreType.DMA((n,))],
    )(*parts)


def _share_layers(totals):
    n = len(totals)

    def body(*refs):
        ins, outs = refs[:n], refs[n:2 * n]
        send_sems, recv_sems, local_sems = refs[2 * n:]
        x, y, c = _me()
        local, remote = [], []
        for t in range(n):
            cp = pltpu.make_async_copy(ins[t], outs[t].at[c], local_sems.at[t])
            cp.start()
            local.append(cp)
            cp = pltpu.make_async_remote_copy(
                src_ref=ins[t], dst_ref=outs[t].at[c], send_sem=send_sems.at[t], recv_sem=recv_sems.at[t],
                device_id=(x, y, 1 - c), device_id_type=MESH)
            cp.start()
            remote.append(cp)
        for cp in local:
            cp.wait()
        for cp in remote:
            cp.wait()

    return pl.pallas_call(
        body, name="reduce_share_layers", in_specs=[ANY] * n, out_specs=[ANY] * n,
        out_shape=[jax.ShapeDtypeStruct((2,) + t.shape, F32) for t in totals],
        scratch_shapes=[pltpu.SemaphoreType.DMA((n,)), pltpu.SemaphoreType.DMA((n,)), pltpu.SemaphoreType.DMA((n,))],
    )(*totals)


def _small_allreduce_adamw(part, w, m, v):
    shape = part.shape

    def body(part_ref, w_ref, m_ref, v_ref, g_ref, d_ref, nm_ref, nv_ref, slots, send_sems, recv_sems):
        x, y, c = _me()
        me = 4 * x + 2 * y + c
        slots[me] = part_ref[...]
        copies = []
        for d in range(1, 8):
            peer = (x ^ (d >> 2), y ^ ((d >> 1) & 1), c ^ (d & 1))
            cp = pltpu.make_async_remote_copy(
                src_ref=part_ref, dst_ref=slots.at[me], send_sem=send_sems.at[d - 1], recv_sem=recv_sems.at[d - 1],
                device_id=peer, device_id_type=MESH)
            cp.start()
            copies.append(cp)
        for cp in copies:
            cp.wait()
        g = slots[0]
        for i in range(1, 8):
            g = g + slots[i]
        g_ref[...] = g
        d, nm, nv = _adamw_math(w_ref[...], g, m_ref[...], v_ref[...])
        d_ref[...] = d
        nm_ref[...] = nm
        nv_ref[...] = nv

    shp = jax.ShapeDtypeStruct(shape, F32)
    return pl.pallas_call(
        body, name="small_allreduce_adamw", in_specs=[VMEM_SPEC] * 4, out_specs=[VMEM_SPEC] * 4,
        out_shape=[shp, shp, shp, shp],
        scratch_shapes=[pltpu.VMEM((8,) + shape, F32), pltpu.SemaphoreType.DMA((7,)), pltpu.SemaphoreType.DMA((7,))],
    )(part, w, m, v)


TM = 256
TA = 512
TN_DW = 1408


def _layer_fwd(l, h, p_i, wts, consts):
    w_all, _, w_out, _, w_gate, _, w_ple, norm_g, b_f, qkg, ple_g = wts
    bsum, _, bias, _, rope = consts
    u, z, qa, ka, va, qb, kb, vb, logf = _inproj(l, h, norm_g, w_all, b_f, qkg, bsum, rope, TM)
    c_spread, c_t = _cumsum_gates(l, logf, TA)
    oa, lse_a = _attn_fwd(f"fox_fwd_l{l}", True, qa, ka, va, (c_spread, c_t), TA)
    ob, lse_b = _attn_fwd(f"dil_fwd_l{l}", False, qb, kb, vb, (bias,), TA)
    h2, e, gate, out = _outproj(l, h, oa, ob, z, p_i, w_out, ple_g, w_gate, w_ple, TM)
    saved = (h, u, z, qa, ka, va, qb, kb, vb, c_spread, c_t, oa, lse_a, ob, lse_b, h2, e, gate)
    return out, saved


def _layer_bwd(l, dout, p_i, wts, consts, saved):
    _, w_all_t, _, w_out_t, _, w_gate_t, _, norm_g, b_f, qkg, ple_g = wts
    bsum, hsel, _, bias_t, rope = consts
    h, u, z, qa, ka, va, qb, kb, vb, c_spread, c_t, oa, lse_a, ob, lse_b, h2, e, gate = saved
    dh2, doa, dob, dga, dgb, delta_t, dw_out, dw_gate, dw_ple, dple_g = _outproj_bwd(
        l, dout, h2, e, gate, p_i, oa, ob, z, w_out_t, ple_g, w_gate_t, hsel, TM)
    dqt_a, dk_a, dv_a, dc, drow = _attn_bwd(f"fox_bwd_l{l}", True, qa, ka, va, doa, lse_a, delta_t, 0,
                                      (c_spread, c_t), TA)
    dqt_b, dk_b, dv_b = _attn_bwd(f"dil_bwd_l{l}", False, qb, kb, vb, dob, lse_b, delta_t, N_PAIRS,
                                  (bias_t,), TA)
    dlogf = _rev_cumsum_gates(l, dc, drow, TA)
    dz, dqkg, dbf = _inproj_bwd_prep(l, z, dqt_a, dk_a, dv_a, dqt_b, dk_b, dv_b, dga, dgb, dlogf, b_f, qkg,
                                     bsum, rope, TM)
    dh, dnorm_g = _inproj_bwd_dx(l, dz, w_all_t, h, norm_g, dh2, TM)
    dw_all = _inproj_bwd_dw(l, u, dz, TM * 2, TN_DW)
    return dh, (dw_all, dw_out, dw_ple, dw_gate, dnorm_g[0], dbf[0, :N_HEADS], dqkg[:4], dple_g[0])


N_FA = 2048


def _layer_weights(l, g_in, g_out, g_ple, g_gate, norm_g, b_f, qk_norm_g, ple_norm_g):
    w_in = jnp.transpose(g_in[:, l], (1, 0, 2)).reshape(D_MODEL, N_IN)
    w_all = jnp.concatenate([w_in[:, :N_FA], w_in[:, N_FA + N_HEADS:],
                             jnp.pad(w_in[:, N_FA:N_FA + N_HEADS], ((0, 0), (0, LANE - N_HEADS)))], axis=1)
    w_out = g_out[:, l].reshape(D_MODEL, D_MODEL)
    w_gate = g_gate[:, l].reshape(D_MODEL, D_MODEL)
    w_ple = jnp.transpose(g_ple[:, l], (1, 0, 2)).reshape(PLE_DIM, D_MODEL)
    qkg = jnp.pad(jnp.tile(qk_norm_g[l], (1, N_HEADS)), ((0, 4), (0, 0)))
    bf = jnp.pad(b_f[l], (0, LANE - N_HEADS))[None, :]
    return (w_all, w_all.T, w_out, w_out.T, w_gate, w_gate.T, w_ple, norm_g[l][None, :], bf, qkg,
            ple_norm_g[l][None, :])


def _slot_layout(dw_all, dw_out, dw_ple, dw_gate):
    dw_in = jnp.concatenate([dw_all[:, :N_FA], dw_all[:, N_MAIN:N_MAIN + N_HEADS], dw_all[:, N_FA:N_MAIN]], axis=1)
    return (jnp.transpose(dw_in.reshape(D_MODEL, 4, N_IN // 4), (1, 0, 2)),
            dw_out.reshape(4, D_MODEL // 4, D_MODEL),
            jnp.transpose(dw_ple.reshape(PLE_DIM, 4, D_MODEL // 4), (1, 0, 2)),
            dw_gate.reshape(4, D_MODEL // 4, D_MODEL))


SMALL_ROWS = 40


def _pack_small(norm_g, ple_norm_g, qk_norm_g, b_f):
    flat = jnp.concatenate([norm_g.reshape(-1), ple_norm_g.reshape(-1), qk_norm_g.reshape(-1), b_f.reshape(-1)])
    return jnp.pad(flat, (0, SMALL_ROWS * LANE - flat.shape[0])).reshape(SMALL_ROWS, LANE)


def _unpack_small(packed):
    flat = packed.reshape(-1)
    n1, n2, n3 = 2 * D_MODEL, 4 * D_MODEL, 4 * D_MODEL + 2 * 4 * HEAD_DIM
    return (flat[:n1].reshape(2, D_MODEL), flat[n1:n2].reshape(2, D_MODEL), flat[n2:n3].reshape(2, 4, HEAD_DIM),
            flat[n3:n3 + 2 * N_HEADS].reshape(2, N_HEADS))


def kernel(x, p, positions, norm_g, w_in, b_f, qk_norm_g, w_out, w_ple, ple_norm_g, w_ple_gate, loss_target,
           m_norm_g, m_w_in, m_b_f, m_qk_norm_g, m_w_out, m_w_ple, m_ple_norm_g, m_w_ple_gate,
           v_norm_g, v_w_in, v_b_f, v_qk_norm_g, v_w_out, v_w_ple, v_ple_norm_g, v_w_ple_gate):
    n_layers = w_in.shape[0]
    g_in, g_out, g_ple, g_gate = _gather_weights([w_in, w_out, w_ple, w_ple_gate])
    consts = (_head_block_diag(), _head_select(), _dil_bias(TA, False), _dil_bias(TA, True),
              _rope_tables(positions[0]))
    wts = [_layer_weights(l, g_in, g_out, g_ple, g_gate, norm_g, b_f, qk_norm_g, ple_norm_g)
           for l in range(n_layers)]

    h = x[0]
    saved = []
    for l in range(n_layers):
        h, sv = _layer_fwd(l, h, p[l, 0], wts[l], consts)
        saved.append(sv)
    sq, dh = _loss_head(h, loss_target[0], TM)
    loss = lax.psum(0.5 / D_MODEL * jnp.sum(sq), ("x", "y", "c"))

    big = [None] * n_layers
    small = [None] * n_layers
    for l in reversed(range(n_layers)):
        dh, grads = _layer_bwd(l, dh, p[l, 0], wts[l], consts, saved[l])
        big[l] = _slot_layout(*grads[:4])
        small[l] = grads[4:]
    grad_x = dh[None]

    c = lax.axis_index("c")
    stacked = [jnp.stack([big[l][t] for l in range(n_layers)]) for t in range(4)]
    from_sibling = _swap_layers(stacked)
    pair = [_add_pair(f"reduce_pair_sum_{t}", lax.dynamic_index_in_dim(stacked[t], c, 0, keepdims=False),
                      from_sibling[t]) for t in range(4)]
    landed = _scatter_chips(pair)
    totals = [_sum_slots(f"reduce_chip_sum_{t}", landed[t]) for t in range(4)]
    g_w_in, g_w_out, g_w_ple, g_w_gate = _share_layers(totals)

    outs = {}
    for name, w, g, m, v in (("w_in", w_in, g_w_in, m_w_in, v_w_in), ("w_out", w_out, g_w_out, m_w_out, v_w_out),
                             ("w_ple", w_ple, g_w_ple, m_w_ple, v_w_ple),
                             ("w_ple_gate", w_ple_gate, g_w_gate, m_w_ple_gate, v_w_ple_gate)):
        outs[name] = (g,) + tuple(_adamw(f"adamw_{name}", w, g, m, v))

    part = _pack_small(jnp.stack([s[0] for s in small]), jnp.stack([s[3] for s in small]),
                       jnp.stack([s[2] for s in small]).reshape(n_layers, 4, N_HEADS, HEAD_DIM).sum(axis=2),
                       jnp.stack([s[1] for s in small]))
    packed = _small_allreduce_adamw(part, _pack_small(norm_g, ple_norm_g, qk_norm_g, b_f),
                                    _pack_small(m_norm_g, m_ple_norm_g, m_qk_norm_g, m_b_f),
                                    _pack_small(v_norm_g, v_ple_norm_g, v_qk_norm_g, v_b_f))
    sm = [_unpack_small(a) for a in packed]
    for i, name in enumerate(("norm_g", "ple_norm_g", "qk_norm_g", "b_f")):
        outs[name] = tuple(sm[j][i] for j in range(4))

    order = ("norm_g", "w_in", "b_f", "qk_norm_g", "w_out", "w_ple", "ple_norm_g", "w_ple_gate")
    return (loss, grad_x) + tuple(outs[n][j] for j in range(4) for n in order)
```

```python
import functools
import math

import numpy as np
import jax
import jax.numpy as jnp
from jax import lax
from jax.experimental import pallas as pl
from jax.experimental.pallas import tpu as pltpu

F32 = jnp.float32
BF16 = jnp.bfloat16
MESH = pl.DeviceIdType.MESH

D_MODEL = 1024
HEAD_DIM = 64
D_BRANCH = 512
N_HEADS = 8
N_PAIRS = 4
N_IN = 4104
N_MAIN = 4096
N_ALL = 4224
PLE_DIM = 256
ROPE_THETA = 500000.0
ROPE_HALF = 8
EPS = 1e-6
NEG = -1e30
M_INIT = -1e29
Q_SCALE = HEAD_DIM ** -0.5
DIL_PATTERNS = ((128, 1), (512, 4), (2048, 16))
DIL_BACK = 2048
ADAM_LR, ADAM_B1, ADAM_B2, ADAM_EPS, ADAM_WD, ADAM_STEP = 0.001, 0.9, 0.999, 1e-08, 0.01, 10
VMEM_LIMIT = 56 * 1024 * 1024
LANE = 128


def _dot(a, b):
    return jnp.dot(a, b, preferred_element_type=F32)


def _dot_nt(a, b):
    return lax.dot_general(a, b, (((1,), (1,)), ((), ())), preferred_element_type=F32)


def _dot_tn(a, b):
    return lax.dot_general(a, b, (((0,), (0,)), ((), ())), preferred_element_type=F32)


def _split_dot(x, w):
    hi = x.astype(BF16)
    lo = (x - hi.astype(F32)).astype(BF16)
    return _dot(hi, w) + _dot(lo, w)


def _split3_dot(w, x):
    hi = x.astype(BF16)
    r1 = x - hi.astype(F32)
    mid = r1.astype(BF16)
    lo = (r1 - mid.astype(F32)).astype(BF16)
    return _dot(w, hi) + _dot(w, mid) + _dot(w, lo)


def _sigmoid(x):
    return 1.0 / (1.0 + jnp.exp(-x))


def _params(n_grid):
    return pltpu.CompilerParams(dimension_semantics=("arbitrary",) * n_grid,
                                vmem_limit_bytes=VMEM_LIMIT)


def _full(shape):
    nd = len(shape)
    return pl.BlockSpec(shape, lambda *_: (0,) * nd)


def _head_block_diag():
    i = np.arange(D_BRANCH)
    return jnp.asarray((i[:, None] // HEAD_DIM == i[None, :] // HEAD_DIM).astype(np.float32), BF16)


def _head_select():
    i = np.arange(2 * D_BRANCH)
    j = np.arange(LANE)
    return jnp.asarray((i[:, None] // HEAD_DIM == j[None, :]).astype(np.float32), BF16)


def _dil_bias(t):
    nb = DIL_BACK // t + 1
    qi = np.arange(t)[:, None]
    ki = np.arange(t)[None, :]
    tiles = []
    for r in range(nb):
        d = r * t + qi - ki
        mult = np.zeros((t, t), np.int64)
        for window, dil in DIL_PATTERNS:
            mult += ((d >= 0) & (d <= window) & (d % dil == 0)).astype(np.int64)
        b = np.where(mult > 0, np.log(np.maximum(mult, 1)), NEG).astype(np.float32)
        tiles.append(b.T)
    return jnp.asarray(np.stack(tiles))


def _rope_tables(positions):
    inv_freq = ROPE_THETA ** (-jnp.arange(ROPE_HALF, dtype=F32) / ROPE_HALF)
    ang = positions.astype(F32)[:, None] * inv_freq
    cos, sin = jnp.cos(ang), jnp.sin(ang)
    s = positions.shape[0]
    rest = HEAD_DIM - 2 * ROPE_HALF
    one, zero, zero8 = jnp.ones((s, rest), F32), jnp.zeros((s, rest), F32), jnp.zeros((s, ROPE_HALF), F32)
    c = jnp.concatenate([cos, cos, one], axis=1)
    s1 = jnp.concatenate([zero8, sin, zero], axis=1)
    s2 = jnp.concatenate([-sin, zero8, zero], axis=1)
    return tuple(jnp.tile(t, (1, 2)) for t in (c, s1, s2))


def _rope_fwd(x, c, s1, s2):
    return x * c + pltpu.roll(x, ROPE_HALF, 1) * s1 + pltpu.roll(x, LANE - ROPE_HALF, 1) * s2


def _rope_bwd(dy, c, s1, s2):
    return dy * c + pltpu.roll(dy * s1, LANE - ROPE_HALF, 1) + pltpu.roll(dy * s2, ROPE_HALF, 1)


def _log_sigmoid(x):
    return jnp.minimum(x, 0.0) - jnp.log(1.0 + jnp.exp(-jnp.abs(x)))


def _inproj(l, h, norm_g, w_all, b_f, qkg, bsum, rope, tm):
    s = h.shape[0]
    rc, rs1, rs2 = rope

    def body(h_ref, g_ref, w_ref, bf_ref, qkg_ref, bsum_ref, rc_ref, rs1_ref, rs2_ref,
             u_ref, z_ref, qa_ref, ka_ref, va_ref, qb_ref, kb_ref, vb_ref, lf_ref, vat_ref, vbt_ref):
        hh = h_ref[...]
        r = lax.rsqrt(jnp.mean(hh * hh, axis=-1, keepdims=True) + EPS)
        u = (hh * r * g_ref[...]).astype(BF16)
        u_ref[...] = u
        for k in range(N_ALL // LANE // 3):
            cols = slice(3 * LANE * k, 3 * LANE * (k + 1))
            z_ref[:, cols] = _dot(u, w_ref[:, cols])
        bs = bsum_ref[...]

        def head_norm(x, row):
            ms = _split_dot(x * x, bs) * (1.0 / HEAD_DIM)
            return x * lax.rsqrt(ms + EPS) * qkg_ref[row:row + 1, :]

        def seg(k):
            return z_ref[:, D_BRANCH * k:D_BRANCH * (k + 1)]

        qa_ref[...] = (head_norm(seg(0), 0) * Q_SCALE).astype(BF16)
        ka_ref[...] = head_norm(seg(1), 1).astype(BF16)
        va_ref[...] = seg(2).astype(BF16)
        vat_ref[...] = seg(2).T.astype(BF16)
        qn = head_norm(seg(4), 2) * Q_SCALE
        kn = head_norm(seg(5), 3)
        c, s1, s2 = rc_ref[...], rs1_ref[...], rs2_ref[...]
        for k in range(D_BRANCH // LANE):
            cols = slice(LANE * k, LANE * (k + 1))
            qb_ref[:, cols] = _rope_fwd(qn[:, cols], c, s1, s2).astype(BF16)
            kb_ref[:, cols] = _rope_fwd(kn[:, cols], c, s1, s2).astype(BF16)
        vb_ref[...] = seg(6).astype(BF16)
        vbt_ref[...] = seg(6).T.astype(BF16)
        lf_ref[...] = _log_sigmoid(z_ref[:, N_MAIN:N_ALL] + bf_ref[...])

    row = lambda w: pl.BlockSpec((tm, w), lambda i: (i, 0))
    colt = pl.BlockSpec((D_BRANCH, tm), lambda i: (0, i))
    bf = lambda: jax.ShapeDtypeStruct((s, D_BRANCH), BF16)
    bft = lambda: jax.ShapeDtypeStruct((D_BRANCH, s), BF16)
    return pl.pallas_call(
        body, name=f"inproj_l{l}", grid=(s // tm,),
        in_specs=[row(D_MODEL), _full((1, D_MODEL)), _full((D_MODEL, N_ALL)), _full((1, LANE)),
                  _full((8, D_BRANCH)), _full((D_BRANCH, D_BRANCH)), row(LANE), row(LANE), row(LANE)],
        out_specs=[row(D_MODEL), row(N_ALL)] + [row(D_BRANCH)] * 6 + [row(LANE), colt, colt],
        out_shape=[jax.ShapeDtypeStruct((s, D_MODEL), BF16), jax.ShapeDtypeStruct((s, N_ALL), F32),
                   bf(), bf(), bf(), bf(), bf(), bf(), jax.ShapeDtypeStruct((s, LANE), F32), bft(), bft()],
        compiler_params=_params(1),
    )(h, norm_g, w_all, b_f, qkg, bsum, rc, rs1, rs2)


def _tri(t, upper):
    a = lax.broadcasted_iota(jnp.int32, (t, t), 0)
    b = lax.broadcasted_iota(jnp.int32, (t, t), 1)
    return jnp.where((b >= a) if upper else (b <= a), 1.0, 0.0).astype(BF16)


def _cumsum_gates(l, logf, t):
    s = logf.shape[0]

    def body(lf_ref, cs_ref, ct_ref, carry):
        @pl.when(pl.program_id(0) == 0)
        def _():
            carry[...] = jnp.zeros_like(carry)

        x = lf_ref[...]
        c = _split3_dot(_tri(t, False), x) + carry[0:1, :]
        carry[...] = jnp.broadcast_to(c[t - 1:t, :], carry.shape)
        ct = c.T
        for p in range(N_PAIRS):
            cs_ref[:, LANE * p:LANE * (p + 1)] = c if p == 0 else pltpu.roll(c, LANE - 2 * p, 1)
            ct_ref[p, :, :] = ct[2 * p:2 * p + 2, :]

    return pl.pallas_call(
        body, name=f"cumsum_l{l}", grid=(s // t,),
        in_specs=[pl.BlockSpec((t, LANE), lambda i: (i, 0))],
        out_specs=[pl.BlockSpec((t, N_PAIRS * LANE), lambda i: (i, 0)),
                   pl.BlockSpec((N_PAIRS, 2, t), lambda i: (0, 0, i))],
        out_shape=[jax.ShapeDtypeStruct((s, N_PAIRS * LANE), F32), jax.ShapeDtypeStruct((N_PAIRS, 2, s), F32)],
        scratch_shapes=[pltpu.VMEM((8, LANE), F32)],
        compiler_params=_params(1),
    )(logf)


def _rev_cumsum_gates(l, dc_spread, drow, t):
    s = dc_spread.shape[0]
    n = s // t

    def body(dc_ref, drow_ref, out_ref, carry):
        @pl.when(pl.program_id(0) == 0)
        def _():
            carry[...] = jnp.zeros_like(carry)

        lane = lax.broadcasted_iota(jnp.int32, (t, LANE), 1)
        rows = jnp.concatenate([drow_ref[p] for p in range(N_PAIRS)] + [jnp.zeros((LANE - N_HEADS, t), F32)], axis=0)
        x = rows.T
        for p in range(N_PAIRS):
            xp = jnp.where(lane < 2, dc_ref[:, LANE * p:LANE * (p + 1)], 0.0)
            x = x + (xp if p == 0 else pltpu.roll(xp, 2 * p, 1))
        out = _split3_dot(_tri(t, True), x) + carry[0:1, :]
        out_ref[...] = out
        carry[...] = jnp.broadcast_to(out[0:1, :], carry.shape)

    return pl.pallas_call(
        body, name=f"revcumsum_l{l}", grid=(n,),
        in_specs=[pl.BlockSpec((t, N_PAIRS * LANE), lambda i: (n - 1 - i, 0)),
                  pl.BlockSpec((N_PAIRS, 2, t), lambda i: (0, 0, n - 1 - i))],
        out_specs=pl.BlockSpec((t, LANE), lambda i: (n - 1 - i, 0)),
        out_shape=jax.ShapeDtypeStruct((s, LANE), F32),
        scratch_shapes=[pltpu.VMEM((8, LANE), F32)],
        compiler_params=_params(1),
    )(dc_spread, drow)


def _attn_fwd(name, fox, q, k, vt, extra, t):
    s = q.shape[0]
    nq = s // t
    nb = DIL_BACK // t + 1

    def body(*refs):
        if fox:
            q_ref, k_ref, vt_ref, ccol_ref, crow_ref, o_ref, lse_ref, m_scr, l_scr, acc_scr = refs
        else:
            q_ref, k_ref, vt_ref, bias_ref, o_ref, lse_ref, m_scr, l_scr, acc_scr = refs
        i = pl.program_id(1)
        lane = lax.broadcasted_iota(jnp.int32, (t, LANE), 1)
        first = lane < HEAD_DIM
        qq = q_ref[...]
        zero = jnp.zeros_like(qq)
        qh = (jnp.where(first, qq, zero), jnp.where(first, zero, qq))
        m_scr[...] = jnp.full(m_scr.shape, M_INIT, F32)
        l_scr[...] = jnp.zeros_like(l_scr)
        acc_scr[...] = jnp.zeros_like(acc_scr)

        def step(j, diag):
            rows = pl.ds(pl.multiple_of(j * t, t), t)
            ks = k_ref[rows, :]
            vts = vt_ref[:, rows]
            if fox:
                ccol = ccol_ref[rows, :]
            for h in range(2):
                st = _dot_nt(ks, qh[h])
                if fox:
                    st = st + (crow_ref[h:h + 1, :] - ccol[:, h:h + 1])
                    if diag:
                        ki = lax.broadcasted_iota(jnp.int32, (t, t), 0)
                        qi = lax.broadcasted_iota(jnp.int32, (t, t), 1)
                        st = jnp.where(ki <= qi, st, NEG)
                else:
                    st = st + bias_ref[i - j]
                m_old = m_scr[h]
                m_new = jnp.maximum(m_old, jnp.max(st, axis=0, keepdims=True))
                alpha = jnp.exp(m_old - m_new)
                p = jnp.exp(st - m_new)
                l_scr[h] = alpha * l_scr[h] + jnp.sum(p, axis=0, keepdims=True)
                acc_scr[h] = alpha * acc_scr[h] + _dot(vts, p.astype(BF16))
                m_scr[h] = m_new

        if fox:
            lax.fori_loop(0, i, lambda j, c: (step(j, False), c)[1], 0)
            step(i, True)
        else:
            lax.fori_loop(jnp.maximum(i - (nb - 1), 0), i + 1, lambda j, c: (step(j, False), c)[1], 0)

        sub = lax.broadcasted_iota(jnp.int32, (LANE, t), 0)
        ot = jnp.where(sub < HEAD_DIM, acc_scr[0] / l_scr[0], acc_scr[1] / l_scr[1])
        o_ref[...] = ot.T
        for h in range(2):
            lse_ref[h:h + 1, :] = m_scr[h] + jnp.log(l_scr[h])

    qspec = pl.BlockSpec((t, LANE), lambda hp, i: (i, hp))
    kspec = pl.BlockSpec((s, LANE), lambda hp, i: (0, hp))
    vtspec = pl.BlockSpec((LANE, s), lambda hp, i: (hp, 0))
    in_specs = [qspec, kspec, vtspec]
    if fox:
        in_specs += [kspec, pl.BlockSpec((None, 2, t), lambda hp, i: (hp, 0, i))]
    else:
        in_specs += [_full((nb, t, t))]
    return pl.pallas_call(
        body, name=name, grid=(N_PAIRS, nq),
        in_specs=in_specs,
        out_specs=[qspec, pl.BlockSpec((None, 2, t), lambda hp, i: (hp, 0, i))],
        out_shape=[jax.ShapeDtypeStruct((s, D_BRANCH), F32), jax.ShapeDtypeStruct((N_PAIRS, 2, s), F32)],
        scratch_shapes=[pltpu.VMEM((2, 1, t), F32), pltpu.VMEM((2, 1, t), F32), pltpu.VMEM((2, LANE, t), F32)],
        compiler_params=_params(2),
    )(q, k, vt, *extra)


def _attn_bwd(name, fox, q, k, v, do, lse_t, delta_t, pair_offset, extra, t):
    s = q.shape[0]
    nk = s // t
    nb = DIL_BACK // t + 1

    def body(*refs):
        if fox:
            (q_ref, k_ref, v_ref, do_ref, lse_ref, delta_ref, ccol_ref, crow_ref,
             dqt_ref, dk_ref, dv_ref, dc_ref, drow_ref) = refs
        else:
            q_ref, k_ref, v_ref, do_ref, lse_ref, delta_ref, bias_ref, dqt_ref, dk_ref, dv_ref = refs
        j = pl.program_id(1)

        @pl.when(j == 0)
        def _():
            dqt_ref[...] = jnp.zeros_like(dqt_ref)
            if fox:
                drow_ref[...] = jnp.zeros_like(drow_ref)

        lane = lax.broadcasted_iota(jnp.int32, (t, LANE), 1)
        first = lane < HEAD_DIM
        ks = k_ref[...]
        vs = v_ref[...]
        kt = ks.astype(F32).T
        sub = lax.broadcasted_iota(jnp.int32, (LANE, t), 0)
        kth = (jnp.where(sub < HEAD_DIM, kt, 0.0).astype(BF16), jnp.where(sub < HEAD_DIM, 0.0, kt).astype(BF16))
        dk_ref[...] = jnp.zeros_like(dk_ref)
        dv_ref[...] = jnp.zeros_like(dv_ref)
        if fox:
            dc_ref[...] = jnp.zeros_like(dc_ref)
            ccol = ccol_ref[...]

        def step(i, diag):
            rows = pl.ds(pl.multiple_of(i * t, t), t)
            qq = q_ref[rows, :]
            dd = do_ref[rows, :]
            zero = jnp.zeros_like(qq)
            qh = (jnp.where(first, qq, zero), jnp.where(first, zero, qq))
            dh = (jnp.where(first, dd, zero), jnp.where(first, zero, dd))
            for h in range(2):
                st = _dot_nt(ks, qh[h])
                if fox:
                    st = st + (crow_ref[h:h + 1, rows] - ccol[:, h:h + 1])
                    if diag:
                        ki = lax.broadcasted_iota(jnp.int32, (t, t), 0)
                        qi = lax.broadcasted_iota(jnp.int32, (t, t), 1)
                        st = jnp.where(ki <= qi, st, NEG)
                else:
                    st = st + bias_ref[i - j]
                pt = jnp.exp(st - lse_ref[h:h + 1, rows])
                dpt = _dot_nt(vs, dh[h])
                dst = pt * (dpt - delta_ref[h:h + 1, rows])
                dv_ref[...] += _dot(pt.astype(BF16), dh[h])
                dsb = dst.astype(BF16)
                dk_ref[...] += _dot(dsb, qh[h])
                dqt_ref[:, rows] += _dot(kth[h], dsb)
                if fox:
                    dc_ref[...] -= jnp.where(lane == h, jnp.sum(dst, axis=1, keepdims=True), 0.0)
                    drow_ref[h:h + 1, rows] += jnp.sum(dst, axis=0, keepdims=True)

        if fox:
            step(j, True)
            lax.fori_loop(j + 1, nk, lambda i, c: (step(i, False), c)[1], 0)
        else:
            lax.fori_loop(j, jnp.minimum(j + nb, nk), lambda i, c: (step(i, False), c)[1], 0)

    kspec = pl.BlockSpec((t, LANE), lambda hp, j: (j, hp))
    qspec = pl.BlockSpec((s, LANE), lambda hp, j: (0, hp))
    rowspec = pl.BlockSpec((None, 2, s), lambda hp, j: (hp, 0, 0))
    drowspec = pl.BlockSpec((None, 2, s), lambda hp, j: (hp + pair_offset, 0, 0))
    in_specs = [qspec, kspec, kspec, qspec, rowspec, drowspec]
    out_specs = [pl.BlockSpec((LANE, s), lambda hp, j: (hp, 0)), kspec, kspec]
    out_shape = [jax.ShapeDtypeStruct((D_BRANCH, s), F32), jax.ShapeDtypeStruct((s, D_BRANCH), F32),
                 jax.ShapeDtypeStruct((s, D_BRANCH), F32)]
    if fox:
        in_specs += [kspec, rowspec]
        out_specs += [kspec, rowspec]
        out_shape += [jax.ShapeDtypeStruct((s, N_PAIRS * LANE), F32), jax.ShapeDtypeStruct((N_PAIRS, 2, s), F32)]
    else:
        in_specs += [_full((nb, t, t))]
    return pl.pallas_call(
        body, name=name, grid=(N_PAIRS, nk), in_specs=in_specs, out_specs=out_specs, out_shape=out_shape,
        compiler_params=_params(2),
    )(q, k, v, do, lse_t, delta_t, *extra)


def _silu(x):
    return x * _sigmoid(x)


def _outproj(l, h, oa, ob, z, p_i, w_out, ple_g, w_gate, w_ple, tm):
    s = h.shape[0]

    def body(h_ref, oa_ref, ob_ref, ga_ref, gb_ref, p_ref, wo_ref, pg_ref, wg_ref, wp_ref,
             h2_ref, e_ref, gate_ref, out_ref):
        a = jnp.concatenate([oa_ref[...] * _silu(ga_ref[...]), ob_ref[...] * _silu(gb_ref[...])], axis=1)
        h2 = h_ref[...] + _dot(a.astype(BF16), wo_ref[...])
        h2_ref[...] = h2
        r = lax.rsqrt(jnp.mean(h2 * h2, axis=-1, keepdims=True) + EPS)
        n2 = (h2 * r * pg_ref[...]).astype(BF16)
        gate = _sigmoid(_dot(n2, wg_ref[...]))
        e = _dot(p_ref[...].astype(BF16), wp_ref[...])
        e_ref[...] = e
        gate_ref[...] = gate
        out_ref[...] = h2 + e * gate

    row = lambda w: pl.BlockSpec((tm, w), lambda i: (i, 0))
    zcol = lambda k: pl.BlockSpec((tm, D_BRANCH), lambda i: (i, k))
    f = lambda: jax.ShapeDtypeStruct((s, D_MODEL), F32)
    return pl.pallas_call(
        body, name=f"outproj_l{l}", grid=(s // tm,),
        in_specs=[row(D_MODEL), row(D_BRANCH), row(D_BRANCH), zcol(3), zcol(7), row(PLE_DIM),
                  _full((D_MODEL, D_MODEL)), _full((1, D_MODEL)), _full((D_MODEL, D_MODEL)),
                  _full((PLE_DIM, D_MODEL))],
        out_specs=[row(D_MODEL)] * 4, out_shape=[f(), f(), f(), f()],
        compiler_params=_params(1),
    )(h, oa, ob, z, z, p_i, w_out, ple_g, w_gate, w_ple)


def _loss_head(y, target, tm):
    s = y.shape[0]

    def body(y_ref, t_ref, acc_ref, dy_ref):
        @pl.when(pl.program_id(0) == 0)
        def _():
            acc_ref[...] = jnp.zeros_like(acc_ref)

        err = y_ref[...] - t_ref[...]
        dy_ref[...] = err * (1.0 / D_MODEL)
        e2 = err * err
        part = jnp.zeros((8, LANE), F32)
        for r in range(tm // 8):
            for c in range(D_MODEL // LANE):
                part = part + e2[8 * r:8 * (r + 1), LANE * c:LANE * (c + 1)]
        acc_ref[...] += part

    row = pl.BlockSpec((tm, D_MODEL), lambda i: (i, 0))
    return pl.pallas_call(
        body, name="loss_head", grid=(s // tm,), in_specs=[row, row],
        out_specs=[_full((8, LANE)), row],
        out_shape=[jax.ShapeDtypeStruct((8, LANE), F32), jax.ShapeDtypeStruct((s, D_MODEL), F32)],
        compiler_params=_params(1),
    )(y, target)


def _outproj_bwd(l, dout, h2, e, gate, p_i, oa, ob, z, w_out_t, ple_g, w_gate_t, hsel, tm):
    s = dout.shape[0]

    def body(do_ref, h2_ref, e_ref, gate_ref, p_ref, oa_ref, ob_ref, ga_ref, gb_ref, wot_ref, pg_ref,
             wgt_ref, hsel_ref,
             dh2_ref, doa_ref, dob_ref, dga_ref, dgb_ref, delta_ref, dwo_ref, dwg_ref, dwp_ref, dpg_ref):
        @pl.when(pl.program_id(0) == 0)
        def _():
            dwo_ref[...] = jnp.zeros_like(dwo_ref)
            dwg_ref[...] = jnp.zeros_like(dwg_ref)
            dwp_ref[...] = jnp.zeros_like(dwp_ref)
            dpg_ref[...] = jnp.zeros_like(dpg_ref)

        dho = do_ref[...]
        g = gate_ref[...]
        de = (dho * g).astype(BF16)
        dwp_ref[...] += _dot_tn(p_ref[...].astype(BF16), de)
        dpre = (dho * e_ref[...] * g * (1.0 - g)).astype(BF16)
        h2 = h2_ref[...]
        pg = pg_ref[...]
        r = lax.rsqrt(jnp.mean(h2 * h2, axis=-1, keepdims=True) + EPS)
        n2 = (h2 * r * pg).astype(BF16)
        dwg_ref[...] += _dot_tn(n2, dpre)
        dn2 = _dot(dpre, wgt_ref[...])
        dpg_ref[0:1, :] += jnp.sum(dn2 * h2 * r, axis=0, keepdims=True)
        wv = dn2 * pg
        dh2 = dho + r * wv - h2 * (r * r * r) * jnp.mean(wv * h2, axis=-1, keepdims=True)
        dh2_ref[...] = dh2
        dh2b = dh2.astype(BF16)
        ga, gb, oa, ob = ga_ref[...], gb_ref[...], oa_ref[...], ob_ref[...]
        sga, sgb = _sigmoid(ga), _sigmoid(gb)
        a = jnp.concatenate([oa * ga * sga, ob * gb * sgb], axis=1).astype(BF16)
        dwo_ref[...] += _dot_tn(a, dh2b)
        da = _dot(dh2b, wot_ref[...])
        da_a, da_b = da[:, :D_BRANCH], da[:, D_BRANCH:]
        doa = da_a * ga * sga
        dob = da_b * gb * sgb
        doa_ref[...] = doa.astype(BF16)
        dob_ref[...] = dob.astype(BF16)
        dga_ref[...] = (da_a * oa * sga * (1.0 + ga * (1.0 - sga))).astype(BF16)
        dgb_ref[...] = (da_b * ob * sgb * (1.0 + gb * (1.0 - sgb))).astype(BF16)
        prod = jnp.concatenate([doa * oa, dob * ob], axis=1)
        dt = _split_dot(prod, hsel_ref[...]).T
        for pp in range(2 * N_PAIRS):
            delta_ref[pp, :, :] = dt[2 * pp:2 * pp + 2, :]

    row = lambda w: pl.BlockSpec((tm, w), lambda i: (i, 0))
    zcol = lambda k: pl.BlockSpec((tm, D_BRANCH), lambda i: (i, k))
    return pl.pallas_call(
        body, name=f"outproj_bwd_l{l}", grid=(s // tm,),
        in_specs=[row(D_MODEL)] * 4 + [row(PLE_DIM), row(D_BRANCH), row(D_BRANCH), zcol(3), zcol(7),
                                        _full((D_MODEL, D_MODEL)), _full((1, D_MODEL)), _full((D_MODEL, D_MODEL)),
                                        _full((2 * D_BRANCH, LANE))],
        out_specs=[row(D_MODEL)] + [row(D_BRANCH)] * 4
        + [pl.BlockSpec((2 * N_PAIRS, 2, tm), lambda i: (0, 0, i)), _full((D_MODEL, D_MODEL)),
           _full((D_MODEL, D_MODEL)), _full((PLE_DIM, D_MODEL)), _full((8, D_MODEL))],
        out_shape=[jax.ShapeDtypeStruct((s, D_MODEL), F32)] + [jax.ShapeDtypeStruct((s, D_BRANCH), BF16)] * 4
        + [jax.ShapeDtypeStruct((2 * N_PAIRS, 2, s), F32), jax.ShapeDtypeStruct((D_MODEL, D_MODEL), F32),
           jax.ShapeDtypeStruct((D_MODEL, D_MODEL), F32), jax.ShapeDtypeStruct((PLE_DIM, D_MODEL), F32),
           jax.ShapeDtypeStruct((8, D_MODEL), F32)],
        compiler_params=_params(1),
    )(dout, h2, e, gate, p_i, oa, ob, z, z, w_out_t, ple_g, w_gate_t, hsel)


def _inproj_bwd_prep(l, z, dqt_a, dk_a, dv_a, dqt_b, dk_b, dv_b, dga, dgb, dlogf, b_f, qkg, bsum, rope, tm):
    s = z.shape[0]
    rc, rs1, rs2 = rope

    def body(z_ref, dqta_ref, dka_ref, dva_ref, dqtb_ref, dkb_ref, dvb_ref, dga_ref, dgb_ref, dlf_ref,
             bf_ref, qkg_ref, bsum_ref, rc_ref, rs1_ref, rs2_ref, dz_ref, dqkg_ref, dbf_ref):
        @pl.when(pl.program_id(0) == 0)
        def _():
            dqkg_ref[...] = jnp.zeros_like(dqkg_ref)
            dbf_ref[...] = jnp.zeros_like(dbf_ref)

        bs = bsum_ref[...]
        c, s1, s2 = rc_ref[...], rs1_ref[...], rs2_ref[...]

        def unrope(dy):
            return jnp.concatenate([_rope_bwd(dy[:, LANE * k:LANE * (k + 1)], c, s1, s2)
                                    for k in range(D_BRANCH // LANE)], axis=1)

        def norm_bwd(k, row, dy):
            x = z_ref[:, D_BRANCH * k:D_BRANCH * (k + 1)]
            r = lax.rsqrt(_split_dot(x * x, bs) * (1.0 / HEAD_DIM) + EPS)
            dqkg_ref[row:row + 1, :] += jnp.sum(dy * x * r, axis=0, keepdims=True)
            w = dy * qkg_ref[row:row + 1, :]
            dx = r * w - x * (r * r * r) * (_split_dot(w * x, bs) * (1.0 / HEAD_DIM))
            dz_ref[:, D_BRANCH * k:D_BRANCH * (k + 1)] = dx.astype(BF16)

        norm_bwd(0, 0, dqta_ref[...].T * Q_SCALE)
        norm_bwd(1, 1, dka_ref[...])
        dz_ref[:, 2 * D_BRANCH:3 * D_BRANCH] = dva_ref[...].astype(BF16)
        dz_ref[:, 3 * D_BRANCH:4 * D_BRANCH] = dga_ref[...]
        norm_bwd(4, 2, unrope(dqtb_ref[...].T * Q_SCALE))
        norm_bwd(5, 3, unrope(dkb_ref[...]))
        dz_ref[:, 6 * D_BRANCH:7 * D_BRANCH] = dvb_ref[...].astype(BF16)
        dz_ref[:, 7 * D_BRANCH:8 * D_BRANCH] = dgb_ref[...]
        dfa = dlf_ref[...] * _sigmoid(-(z_ref[:, N_MAIN:N_ALL] + bf_ref[...]))
        dz_ref[:, N_MAIN:N_ALL] = dfa.astype(BF16)
        dbf_ref[0:1, :] += jnp.sum(dfa, axis=0, keepdims=True)

    row = lambda w: pl.BlockSpec((tm, w), lambda i: (i, 0))
    colt = pl.BlockSpec((D_BRANCH, tm), lambda i: (0, i))
    return pl.pallas_call(
        body, name=f"inproj_bwd_prep_l{l}", grid=(s // tm,),
        in_specs=[row(N_ALL), colt, row(D_BRANCH), row(D_BRANCH), colt, row(D_BRANCH), row(D_BRANCH),
                  row(D_BRANCH), row(D_BRANCH), row(LANE), _full((1, LANE)), _full((8, D_BRANCH)),
                  _full((D_BRANCH, D_BRANCH)), row(LANE), row(LANE), row(LANE)],
        out_specs=[row(N_ALL), _full((8, D_BRANCH)), _full((8, LANE))],
        out_shape=[jax.ShapeDtypeStruct((s, N_ALL), BF16), jax.ShapeDtypeStruct((8, D_BRANCH), F32),
                   jax.ShapeDtypeStruct((8, LANE), F32)],
        compiler_params=_params(1),
    )(z, dqt_a, dk_a, dv_a, dqt_b, dk_b, dv_b, dga, dgb, dlogf, b_f, qkg, bsum, rc, rs1, rs2)


def _inproj_bwd_dx(l, dz, w_all_t, h, norm_g, dh2, tm):
    s = dz.shape[0]

    def body(dz_ref, wt_ref, h_ref, g_ref, dh2_ref, dh_ref, dg_ref):
        @pl.when(pl.program_id(0) == 0)
        def _():
            dg_ref[...] = jnp.zeros_like(dg_ref)

        du = _dot(dz_ref[...], wt_ref[...])
        hh = h_ref[...]
        g = g_ref[...]
        r = lax.rsqrt(jnp.mean(hh * hh, axis=-1, keepdims=True) + EPS)
        dg_ref[0:1, :] += jnp.sum(du * hh * r, axis=0, keepdims=True)
        wv = du * g
        dh_ref[...] = dh2_ref[...] + r * wv - hh * (r * r * r) * jnp.mean(wv * hh, axis=-1, keepdims=True)

    row = lambda w: pl.BlockSpec((tm, w), lambda i: (i, 0))
    return pl.pallas_call(
        body, name=f"inproj_bwd_dx_l{l}", grid=(s // tm,),
        in_specs=[row(N_ALL), _full((N_ALL, D_MODEL)), row(D_MODEL), _full((1, D_MODEL)), row(D_MODEL)],
        out_specs=[row(D_MODEL), _full((8, D_MODEL))],
        out_shape=[jax.ShapeDtypeStruct((s, D_MODEL), F32), jax.ShapeDtypeStruct((8, D_MODEL), F32)],
        compiler_params=_params(1),
    )(dz, w_all_t, h, norm_g, dh2)


def _inproj_bwd_dw(l, u, dz, tm, tn):
    s = u.shape[0]

    def body(u_ref, dz_ref, dw_ref):
        @pl.when(pl.program_id(1) == 0)
        def _():
            dw_ref[...] = jnp.zeros_like(dw_ref)

        dw_ref[...] += _dot_tn(u_ref[...], dz_ref[...])

    return pl.pallas_call(
        body, name=f"inproj_bwd_dw_l{l}", grid=(N_ALL // tn, s // tm),
        in_specs=[pl.BlockSpec((tm, D_MODEL), lambda n, i: (i, 0)), pl.BlockSpec((tm, tn), lambda n, i: (i, n))],
        out_specs=pl.BlockSpec((D_MODEL, tn), lambda n, i: (0, n)),
        out_shape=jax.ShapeDtypeStruct((D_MODEL, N_ALL), F32),
        compiler_params=_params(2),
    )(u, dz)


def _adamw_math(w, g, m, v):
    m = ADAM_B1 * m + (1.0 - ADAM_B1) * g
    v = ADAM_B2 * v + (1.0 - ADAM_B2) * (g * g)
    m_hat = m / (1.0 - ADAM_B1 ** ADAM_STEP)
    v_hat = v / (1.0 - ADAM_B2 ** ADAM_STEP)
    delta = -ADAM_LR * (m_hat / (jnp.sqrt(v_hat) + ADAM_EPS) + ADAM_WD * w)
    return delta, m, v


def _adamw(name, w, g, m, v):
    nl, r, c = w.shape
    tr = 256 if r % 256 == 0 else r

    def body(w_ref, g_ref, m_ref, v_ref, d_ref, nm_ref, nv_ref):
        d, nm, nv = _adamw_math(w_ref[...], g_ref[...], m_ref[...], v_ref[...])
        d_ref[...] = d
        nm_ref[...] = nm
        nv_ref[...] = nv

    spec = pl.BlockSpec((None, tr, c), lambda a, b: (a, b, 0))
    shp = jax.ShapeDtypeStruct(w.shape, F32)
    return pl.pallas_call(
        body, name=name, grid=(nl, r // tr), in_specs=[spec] * 4, out_specs=[spec] * 3,
        out_shape=[shp, shp, shp], compiler_params=_params(2),
    )(w, g, m, v)


def _add_pair(name, a, b):
    n, r, c = a.shape
    tr = 256 if r % 256 == 0 else r

    def body(a_ref, b_ref, o_ref):
        o_ref[...] = a_ref[...] + b_ref[...]

    spec = pl.BlockSpec((None, tr, c), lambda i, j: (i, j, 0))
    return pl.pallas_call(
        body, name=name, grid=(n, r // tr), in_specs=[spec, spec], out_specs=spec,
        out_shape=jax.ShapeDtypeStruct(a.shape, F32), compiler_params=_params(2),
    )(a, b)


def _sum_slots(name, x):
    n, r, c = x.shape
    tr = 256 if r % 256 == 0 else r

    def body(x_ref, o_ref):
        o_ref[...] = ((x_ref[0] + x_ref[1]) + x_ref[2]) + x_ref[3]

    return pl.pallas_call(
        body, name=name, grid=(r // tr,),
        in_specs=[pl.BlockSpec((n, tr, c), lambda j: (0, j, 0))],
        out_specs=pl.BlockSpec((tr, c), lambda j: (j, 0)),
        out_shape=jax.ShapeDtypeStruct((r, c), F32), compiler_params=_params(1),
    )(x)


def _me():
    return lax.axis_index("x"), lax.axis_index("y"), lax.axis_index("c")


def _other_chips(x, y):
    return [(1 - x, y), (x, 1 - y), (1 - x, 1 - y)]


ANY = pl.BlockSpec(memory_space=pl.ANY)
VMEM_SPEC = pl.BlockSpec(memory_space=pltpu.VMEM)


def _gather_weights(shards):
    n = len(shards)

    def body(*refs):
        ins, outs, stage = refs[:n], refs[n:2 * n], refs[2 * n:3 * n]
        send_sems, recv_sems, local_sems = refs[3 * n:]
        x, y, c = _me()
        k = 2 * x + y
        copies = []
        for t in range(n):
            stage[t][...] = ins[t][...].astype(BF16)
            cp = pltpu.make_async_copy(stage[t], outs[t].at[k], local_sems.at[t])
            cp.start()
            copies.append(cp)
        remote = []
        for t in range(n):
            for j, (px, py) in enumerate(_other_chips(x, y)):
                cp = pltpu.make_async_remote_copy(
                    src_ref=stage[t], dst_ref=outs[t].at[k], send_sem=send_sems.at[3 * t + j],
                    recv_sem=recv_sems.at[3 * t + j], device_id=(px, py, c), device_id_type=MESH)
                cp.start()
                remote.append(cp)
        for cp in copies:
            cp.wait()
        for cp in remote:
            cp.wait()

    return pl.pallas_call(
        body, name="gather_weights",
        in_specs=[VMEM_SPEC] * n, out_specs=[ANY] * n,
        out_shape=[jax.ShapeDtypeStruct((4,) + s.shape, BF16) for s in shards],
        scratch_shapes=[pltpu.VMEM(s.shape, BF16) for s in shards]
        + [pltpu.SemaphoreType.DMA((3 * n,)), pltpu.SemaphoreType.DMA((3 * n,)), pltpu.SemaphoreType.DMA((n,))],
        compiler_params=pltpu.CompilerParams(vmem_limit_bytes=VMEM_LIMIT),
    )(*shards)


def _swap_layers(grads):
    n = len(grads)

    def body(*refs):
        ins, outs = refs[:n], refs[n:2 * n]
        send_sems, recv_sems = refs[2 * n:]
        x, y, c = _me()
        copies = []
        for t in range(n):
            cp = pltpu.make_async_remote_copy(
                src_ref=ins[t].at[1 - c], dst_ref=outs[t], send_sem=send_sems.at[t], recv_sem=recv_sems.at[t],
                device_id=(x, y, 1 - c), device_id_type=MESH)
            cp.start()
            copies.append(cp)
        for cp in copies:
            cp.wait()

    return pl.pallas_call(
        body, name="reduce_swap_layers", in_specs=[ANY] * n, out_specs=[ANY] * n,
        out_shape=[jax.ShapeDtypeStruct(g.shape[1:], F32) for g in grads],
        scratch_shapes=[pltpu.SemaphoreType.DMA((n,)), pltpu.SemaphoreType.DMA((n,))],
    )(*grads)


def _scatter_chips(parts):
    n = len(parts)

    def body(*refs):
        ins, outs = refs[:n], refs[n:2 * n]
        send_sems, recv_sems, local_sems = refs[2 * n:]
        x, y, c = _me()
        k = 2 * x + y
        local, remote = [], []
        for t in range(n):
            cp = pltpu.make_async_copy(ins[t].at[k], outs[t].at[k], local_sems.at[t])
            cp.start()
            local.append(cp)
            for j, (px, py) in enumerate(_other_chips(x, y)):
                cp = pltpu.make_async_remote_copy(
                    src_ref=ins[t].at[2 * px + py], dst_ref=outs[t].at[k], send_sem=send_sems.at[3 * t + j],
                    recv_sem=recv_sems.at[3 * t + j], device_id=(px, py, c), device_id_type=MESH)
                cp.start()
                remote.append(cp)
        for cp in local:
            cp.wait()
        for cp in remote:
            cp.wait()

    return pl.pallas_call(
        body, name="reduce_scatter_chips", in_specs=[ANY] * n, out_specs=[ANY] * n,
        out_shape=[jax.ShapeDtypeStruct(p.shape, F32) for p in parts],
        scratch_shapes=[pltpu.SemaphoreType.DMA((3 * n,)), pltpu.SemaphoreType.DMA((3 * n,)),
                        pltpu.SemaphoreType.DMA((n,))],
    )(*parts)


def _share_layers(totals):
    n = len(totals)

    def body(*refs):
        ins, outs = refs[:n], refs[n:2 * n]
        send_sems, recv_sems, local_sems = refs[2 * n:]
        x, y, c = _me()
        local, remote = [], []
        for t in range(n):
            cp = pltpu.make_async_copy(ins[t], outs[t].at[c], local_sems.at[t])
            cp.start()
            local.append(cp)
            cp = pltpu.make_async_remote_copy(
                src_ref=ins[t], dst_ref=outs[t].at[c], send_sem=send_sems.at[t], recv_sem=recv_sems.at[t],
                device_id=(x, y, 1 - c), device_id_type=MESH)
            cp.start()
            remote.append(cp)
        for cp in local:
            cp.wait()
        for cp in remote:
            cp.wait()

    return pl.pallas_call(
        body, name="reduce_share_layers", in_specs=[ANY] * n, out_specs=[ANY] * n,
        out_shape=[jax.ShapeDtypeStruct((2,) + t.shape, F32) for t in totals],
        scratch_shapes=[pltpu.SemaphoreType.DMA((n,)), pltpu.SemaphoreType.DMA((n,)), pltpu.SemaphoreType.DMA((n,))],
    )(*totals)


def _small_allreduce_adamw(part, w, m, v):
    shape = part.shape

    def body(part_ref, w_ref, m_ref, v_ref, g_ref, d_ref, nm_ref, nv_ref, slots, send_sems, recv_sems):
        x, y, c = _me()
        me = 4 * x + 2 * y + c
        slots[me] = part_ref[...]
        copies = []
        for d in range(1, 8):
            peer = (x ^ (d >> 2), y ^ ((d >> 1) & 1), c ^ (d & 1))
            cp = pltpu.make_async_remote_copy(
                src_ref=part_ref, dst_ref=slots.at[me], send_sem=send_sems.at[d - 1], recv_sem=recv_sems.at[d - 1],
                device_id=peer, device_id_type=MESH)
            cp.start()
            copies.append(cp)
        for cp in copies:
            cp.wait()
        g = slots[0]
        for i in range(1, 8):
            g = g + slots[i]
        g_ref[...] = g
        d, nm, nv = _adamw_math(w_ref[...], g, m_ref[...], v_ref[...])
        d_ref[...] = d
        nm_ref[...] = nm
        nv_ref[...] = nv

    shp = jax.ShapeDtypeStruct(shape, F32)
    return pl.pallas_call(
        body, name="small_allreduce_adamw", in_specs=[VMEM_SPEC] * 4, out_specs=[VMEM_SPEC] * 4,
        out_shape=[shp, shp, shp, shp],
        scratch_shapes=[pltpu.VMEM((8,) + shape, F32), pltpu.SemaphoreType.DMA((7,)), pltpu.SemaphoreType.DMA((7,))],
    )(part, w, m, v)


TM = 256
TA = 512
TN_DW = 1408


def _layer_fwd(l, h, p_i, wts, consts):
    w_all, _, w_out, _, w_gate, _, w_ple, norm_g, b_f, qkg, ple_g = wts
    bsum, _, bias_t, rope = consts
    u, z, qa, ka, va, qb, kb, vb, logf, va_t, vb_t = _inproj(l, h, norm_g, w_all, b_f, qkg, bsum, rope, TM)
    c_spread, c_t = _cumsum_gates(l, logf, TA)
    oa, lse_a = _attn_fwd(f"fox_fwd_l{l}", True, qa, ka, va_t, (c_spread, c_t), TA)
    ob, lse_b = _attn_fwd(f"dil_fwd_l{l}", False, qb, kb, vb_t, (bias_t,), TA)
    h2, e, gate, out = _outproj(l, h, oa, ob, z, p_i, w_out, ple_g, w_gate, w_ple, TM)
    saved = (h, u, z, qa, ka, va, qb, kb, vb, c_spread, c_t, oa, lse_a, ob, lse_b, h2, e, gate)
    return out, saved


def _layer_bwd(l, dout, p_i, wts, consts, saved):
    _, w_all_t, _, w_out_t, _, w_gate_t, _, norm_g, b_f, qkg, ple_g = wts
    bsum, hsel, bias_t, rope = consts
    h, u, z, qa, ka, va, qb, kb, vb, c_spread, c_t, oa, lse_a, ob, lse_b, h2, e, gate = saved
    dh2, doa, dob, dga, dgb, delta_t, dw_out, dw_gate, dw_ple, dple_g = _outproj_bwd(
        l, dout, h2, e, gate, p_i, oa, ob, z, w_out_t, ple_g, w_gate_t, hsel, TM)
    dqt_a, dk_a, dv_a, dc, drow = _attn_bwd(f"fox_bwd_l{l}", True, qa, ka, va, doa, lse_a, delta_t, 0,
                                      (c_spread, c_t), TA)
    dqt_b, dk_b, dv_b = _attn_bwd(f"dil_bwd_l{l}", False, qb, kb, vb, dob, lse_b, delta_t, N_PAIRS,
                                  (bias_t,), TA)
    dlogf = _rev_cumsum_gates(l, dc, drow, TA)
    dz, dqkg, dbf = _inproj_bwd_prep(l, z, dqt_a, dk_a, dv_a, dqt_b, dk_b, dv_b, dga, dgb, dlogf, b_f, qkg,
                                     bsum, rope, TM)
    dh, dnorm_g = _inproj_bwd_dx(l, dz, w_all_t, h, norm_g, dh2, TM)
    dw_all = _inproj_bwd_dw(l, u, dz, TM * 2, TN_DW)
    return dh, (dw_all, dw_out, dw_ple, dw_gate, dnorm_g[0], dbf[0, :N_HEADS], dqkg[:4], dple_g[0])


N_FA = 2048


def _layer_weights(l, g_in, g_out, g_ple, g_gate, norm_g, b_f, qk_norm_g, ple_norm_g):
    w_in = jnp.transpose(g_in[:, l], (1, 0, 2)).reshape(D_MODEL, N_IN)
    w_all = jnp.concatenate([w_in[:, :N_FA], w_in[:, N_FA + N_HEADS:],
                             jnp.pad(w_in[:, N_FA:N_FA + N_HEADS], ((0, 0), (0, LANE - N_HEADS)))], axis=1)
    w_out = g_out[:, l].reshape(D_MODEL, D_MODEL)
    w_gate = g_gate[:, l].reshape(D_MODEL, D_MODEL)
    w_ple = jnp.transpose(g_ple[:, l], (1, 0, 2)).reshape(PLE_DIM, D_MODEL)
    qkg = jnp.pad(jnp.tile(qk_norm_g[l], (1, N_HEADS)), ((0, 4), (0, 0)))
    bf = jnp.pad(b_f[l], (0, LANE - N_HEADS))[None, :]
    return (w_all, w_all.T, w_out, w_out.T, w_gate, w_gate.T, w_ple, norm_g[l][None, :], bf, qkg,
            ple_norm_g[l][None, :])


def _slot_layout(dw_all, dw_out, dw_ple, dw_gate):
    dw_in = jnp.concatenate([dw_all[:, :N_FA], dw_all[:, N_MAIN:N_MAIN + N_HEADS], dw_all[:, N_FA:N_MAIN]], axis=1)
    return (jnp.transpose(dw_in.reshape(D_MODEL, 4, N_IN // 4), (1, 0, 2)),
            dw_out.reshape(4, D_MODEL // 4, D_MODEL),
            jnp.transpose(dw_ple.reshape(PLE_DIM, 4, D_MODEL // 4), (1, 0, 2)),
            dw_gate.reshape(4, D_MODEL // 4, D_MODEL))


SMALL_ROWS = 40


def _pack_small(norm_g, ple_norm_g, qk_norm_g, b_f):
    flat = jnp.concatenate([norm_g.reshape(-1), ple_norm_g.reshape(-1), qk_norm_g.reshape(-1), b_f.reshape(-1)])
    return jnp.pad(flat, (0, SMALL_ROWS * LANE - flat.shape[0])).reshape(SMALL_ROWS, LANE)


def _unpack_small(packed):
    flat = packed.reshape(-1)
    n1, n2, n3 = 2 * D_MODEL, 4 * D_MODEL, 4 * D_MODEL + 2 * 4 * HEAD_DIM
    return (flat[:n1].reshape(2, D_MODEL), flat[n1:n2].reshape(2, D_MODEL), flat[n2:n3].reshape(2, 4, HEAD_DIM),
            flat[n3:n3 + 2 * N_HEADS].reshape(2, N_HEADS))


def kernel(x, p, positions, norm_g, w_in, b_f, qk_norm_g, w_out, w_ple, ple_norm_g, w_ple_gate, loss_target,
           m_norm_g, m_w_in, m_b_f, m_qk_norm_g, m_w_out, m_w_ple, m_ple_norm_g, m_w_ple_gate,
           v_norm_g, v_w_in, v_b_f, v_qk_norm_g, v_w_out, v_w_ple, v_ple_norm_g, v_w_ple_gate):
    n_layers = w_in.shape[0]
    g_in, g_out, g_ple, g_gate = _gather_weights([w_in, w_out, w_ple, w_ple_gate])
    consts = (_head_block_diag(), _head_select(), _dil_bias(TA), _rope_tables(positions[0]))
    wts = [_layer_weights(l, g_in, g_out, g_ple, g_gate, norm_g, b_f, qk_norm_g, ple_norm_g)
           for l in range(n_layers)]

    h = x[0]
    saved = []
    for l in range(n_layers):
        h, sv = _layer_fwd(l, h, p[l, 0], wts[l], consts)
        saved.append(sv)
    sq, dh = _loss_head(h, loss_target[0], TM)
    loss = lax.psum(0.5 / D_MODEL * jnp.sum(sq), ("x", "y", "c"))

    big = [None] * n_layers
    small = [None] * n_layers
    for l in reversed(range(n_layers)):
        dh, grads = _layer_bwd(l, dh, p[l, 0], wts[l], consts, saved[l])
        big[l] = _slot_layout(*grads[:4])
        small[l] = grads[4:]
    grad_x = dh[None]

    c = lax.axis_index("c")
    stacked = [jnp.stack([big[l][t] for l in range(n_layers)]) for t in range(4)]
    from_sibling = _swap_layers(stacked)
    pair = [_add_pair(f"reduce_pair_sum_{t}", lax.dynamic_index_in_dim(stacked[t], c, 0, keepdims=False),
                      from_sibling[t]) for t in range(4)]
    landed = _scatter_chips(pair)
    totals = [_sum_slots(f"reduce_chip_sum_{t}", landed[t]) for t in range(4)]
    g_w_in, g_w_out, g_w_ple, g_w_gate = _share_layers(totals)

    outs = {}
    for name, w, g, m, v in (("w_in", w_in, g_w_in, m_w_in, v_w_in), ("w_out", w_out, g_w_out, m_w_out, v_w_out),
                             ("w_ple", w_ple, g_w_ple, m_w_ple, v_w_ple),
                             ("w_ple_gate", w_ple_gate, g_w_gate, m_w_ple_gate, v_w_ple_gate)):
        outs[name] = (g,) + tuple(_adamw(f"adamw_{name}", w, g, m, v))

    part = _pack_small(jnp.stack([s[0] for s in small]), jnp.stack([s[3] for s in small]),
                       jnp.stack([s[2] for s in small]).reshape(n_layers, 4, N_HEADS, HEAD_DIM).sum(axis=2),
                       jnp.stack([s[1] for s in small]))
    packed = _small_allreduce_adamw(part, _pack_small(norm_g, ple_norm_g, qk_norm_g, b_f),
                                    _pack_small(m_norm_g, m_ple_norm_g, m_qk_norm_g, m_b_f),
                                    _pack_small(v_norm_g, v_ple_norm_g, v_qk_norm_g, v_b_f))
    sm = [_unpack_small(a) for a in packed]
    for i, name in enumerate(("norm_g", "ple_norm_g", "qk_norm_g", "b_f")):
        outs[name] = tuple(sm[j][i] for j in range(4))

    order = ("norm_g", "w_in", "b_f", "qk_norm_g", "w_out", "w_ple", "ple_norm_g", "w_ple_gate")
    return (loss, grad_x) + tuple(outs[n][j] for j in range(4) for n in order)
```

```python
import functools
import math

import numpy as np
import jax
import jax.numpy as jnp
from jax import lax
from jax.experimental import pallas as pl
from jax.experimental.pallas import tpu as pltpu

F32 = jnp.float32
BF16 = jnp.bfloat16
MESH = pl.DeviceIdType.MESH

D_MODEL = 1024
HEAD_DIM = 64
D_BRANCH = 512
N_HEADS = 8
N_PAIRS = 4
N_IN = 4104
N_MAIN = 4096
N_ALL = 4224
PLE_DIM = 256
ROPE_THETA = 500000.0
ROPE_HALF = 8
EPS = 1e-6
NEG = -1e30
M_INIT = -1e29
Q_SCALE = HEAD_DIM ** -0.5
DIL_PATTERNS = ((128, 1), (512, 4), (2048, 16))
DIL_BACK = 2048
ADAM_LR, ADAM_B1, ADAM_B2, ADAM_EPS, ADAM_WD, ADAM_STEP = 0.001, 0.9, 0.999, 1e-08, 0.01, 10
VMEM_LIMIT = 56 * 1024 * 1024
LANE = 128


def _dot(a, b):
    return jnp.dot(a, b, preferred_element_type=F32)


def _dot_nt(a, b):
    return lax.dot_general(a, b, (((1,), (1,)), ((), ())), preferred_element_type=F32)


def _dot_tn(a, b):
    return lax.dot_general(a, b, (((0,), (0,)), ((), ())), preferred_element_type=F32)


def _split_dot(x, w):
    hi = x.astype(BF16)
    lo = (x - hi.astype(F32)).astype(BF16)
    return _dot(hi, w) + _dot(lo, w)


def _split3_dot(w, x):
    hi = x.astype(BF16)
    r1 = x - hi.astype(F32)
    mid = r1.astype(BF16)
    lo = (r1 - mid.astype(F32)).astype(BF16)
    return _dot(w, hi) + _dot(w, mid) + _dot(w, lo)


def _sigmoid(x):
    return 1.0 / (1.0 + jnp.exp(-x))


def _params(n_grid):
    return pltpu.CompilerParams(dimension_semantics=("arbitrary",) * n_grid,
                                vmem_limit_bytes=VMEM_LIMIT)


def _full(shape):
    nd = len(shape)
    return pl.BlockSpec(shape, lambda *_: (0,) * nd)


def _head_block_diag():
    i = np.arange(D_BRANCH)
    return jnp.asarray((i[:, None] // HEAD_DIM == i[None, :] // HEAD_DIM).astype(np.float32), BF16)


def _head_select():
    i = np.arange(2 * D_BRANCH)
    j = np.arange(LANE)
    return jnp.asarray((i[:, None] // HEAD_DIM == j[None, :]).astype(np.float32), BF16)


def _dil_bias(t):
    nb = DIL_BACK // t + 1
    qi = np.arange(t)[:, None]
    ki = np.arange(t)[None, :]
    tiles = []
    for r in range(nb):
        d = r * t + qi - ki
        mult = np.zeros((t, t), np.int64)
        for window, dil in DIL_PATTERNS:
            mult += ((d >= 0) & (d <= window) & (d % dil == 0)).astype(np.int64)
        b = np.where(mult > 0, np.log(np.maximum(mult, 1)), NEG).astype(np.float32)
        tiles.append(b.T)
    return jnp.asarray(np.stack(tiles))


def _rope_tables(positions):
    inv_freq = ROPE_THETA ** (-jnp.arange(ROPE_HALF, dtype=F32) / ROPE_HALF)
    ang = positions.astype(F32)[:, None] * inv_freq
    cos, sin = jnp.cos(ang), jnp.sin(ang)
    s = positions.shape[0]
    rest = HEAD_DIM - 2 * ROPE_HALF
    one, zero, zero8 = jnp.ones((s, rest), F32), jnp.zeros((s, rest), F32), jnp.zeros((s, ROPE_HALF), F32)
    c = jnp.concatenate([cos, cos, one], axis=1)
    s1 = jnp.concatenate([zero8, sin, zero], axis=1)
    s2 = jnp.concatenate([-sin, zero8, zero], axis=1)
    return tuple(jnp.tile(t, (1, 2)) for t in (c, s1, s2))


def _rope_fwd(x, c, s1, s2):
    return x * c + pltpu.roll(x, ROPE_HALF, 1) * s1 + pltpu.roll(x, LANE - ROPE_HALF, 1) * s2


def _rope_bwd(dy, c, s1, s2):
    return dy * c + pltpu.roll(dy * s1, LANE - ROPE_HALF, 1) + pltpu.roll(dy * s2, ROPE_HALF, 1)


def _log_sigmoid(x):
    return jnp.minimum(x, 0.0) - jnp.log(1.0 + jnp.exp(-jnp.abs(x)))


def _inproj(l, h, norm_g, w_all, b_f, qkg, bsum, rope, tm):
    s = h.shape[0]
    rc, rs1, rs2 = rope

    def body(h_ref, g_ref, w_ref, bf_ref, qkg_ref, bsum_ref, rc_ref, rs1_ref, rs2_ref,
             u_ref, z_ref, qa_ref, ka_ref, va_ref, qb_ref, kb_ref, vb_ref, lf_ref, vat_ref, vbt_ref):
        hh = h_ref[...]
        r = lax.rsqrt(jnp.mean(hh * hh, axis=-1, keepdims=True) + EPS)
        u = (hh * r * g_ref[...]).astype(BF16)
        u_ref[...] = u
        for k in range(N_ALL // LANE // 3):
            cols = slice(3 * LANE * k, 3 * LANE * (k + 1))
            z_ref[:, cols] = _dot(u, w_ref[:, cols])
        bs = bsum_ref[...]

        def head_norm(x, row):
            ms = _split_dot(x * x, bs) * (1.0 / HEAD_DIM)
            return x * lax.rsqrt(ms + EPS) * qkg_ref[row:row + 1, :]

        def seg(k):
            return z_ref[:, D_BRANCH * k:D_BRANCH * (k + 1)]

        qa_ref[...] = (head_norm(seg(0), 0) * Q_SCALE).astype(BF16)
        ka_ref[...] = head_norm(seg(1), 1).astype(BF16)
        va_ref[...] = seg(2).astype(BF16)
        vat_ref[...] = seg(2).T.astype(BF16)
        qn = head_norm(seg(4), 2) * Q_SCALE
        kn = head_norm(seg(5), 3)
        c, s1, s2 = rc_ref[...], rs1_ref[...], rs2_ref[...]
        for k in range(D_BRANCH // LANE):
            cols = slice(LANE * k, LANE * (k + 1))
            qb_ref[:, cols] = _rope_fwd(qn[:, cols], c, s1, s2).astype(BF16)
            kb_ref[:, cols] = _rope_fwd(kn[:, cols], c, s1, s2).astype(BF16)
        vb_ref[...] = seg(6).astype(BF16)
        vbt_ref[...] = seg(6).T.astype(BF16)
        lf_ref[...] = _log_sigmoid(z_ref[:, N_MAIN:N_ALL] + bf_ref[...])

    row = lambda w: pl.BlockSpec((tm, w), lambda i: (i, 0))
    colt = pl.BlockSpec((D_BRANCH, tm), lambda i: (0, i))
    bf = lambda: jax.ShapeDtypeStruct((s, D_BRANCH), BF16)
    bft = lambda: jax.ShapeDtypeStruct((D_BRANCH, s), BF16)
    return pl.pallas_call(
        body, name=f"inproj_l{l}", grid=(s // tm,),
        in_specs=[row(D_MODEL), _full((1, D_MODEL)), _full((D_MODEL, N_ALL)), _full((1, LANE)),
                  _full((8, D_BRANCH)), _full((D_BRANCH, D_BRANCH)), row(LANE), row(LANE), row(LANE)],
        out_specs=[row(D_MODEL), row(N_ALL)] + [row(D_BRANCH)] * 6 + [row(LANE), colt, colt],
        out_shape=[jax.ShapeDtypeStruct((s, D_MODEL), BF16), jax.ShapeDtypeStruct((s, N_ALL), F32),
                   bf(), bf(), bf(), bf(), bf(), bf(), jax.ShapeDtypeStruct((s, LANE), F32), bft(), bft()],
        compiler_params=_params(1),
    )(h, norm_g, w_all, b_f, qkg, bsum, rc, rs1, rs2)


def _tri(t, upper):
    a = lax.broadcasted_iota(jnp.int32, (t, t), 0)
    b = lax.broadcasted_iota(jnp.int32, (t, t), 1)
    return jnp.where((b >= a) if upper else (b <= a), 1.0, 0.0).astype(BF16)


def _cumsum_gates(l, logf, t):
    s = logf.shape[0]

    def body(lf_ref, cs_ref, ct_ref, carry):
        @pl.when(pl.program_id(0) == 0)
        def _():
            carry[...] = jnp.zeros_like(carry)

        x = lf_ref[...]
        c = _split3_dot(_tri(t, False), x) + carry[0:1, :]
        carry[...] = jnp.broadcast_to(c[t - 1:t, :], carry.shape)
        ct = c.T
        for p in range(N_PAIRS):
            cs_ref[:, LANE * p:LANE * (p + 1)] = c if p == 0 else pltpu.roll(c, LANE - 2 * p, 1)
            ct_ref[p, :, :] = ct[2 * p:2 * p + 2, :]

    return pl.pallas_call(
        body, name=f"cumsum_l{l}", grid=(s // t,),
        in_specs=[pl.BlockSpec((t, LANE), lambda i: (i, 0))],
        out_specs=[pl.BlockSpec((t, N_PAIRS * LANE), lambda i: (i, 0)),
                   pl.BlockSpec((N_PAIRS, 2, t), lambda i: (0, 0, i))],
        out_shape=[jax.ShapeDtypeStruct((s, N_PAIRS * LANE), F32), jax.ShapeDtypeStruct((N_PAIRS, 2, s), F32)],
        scratch_shapes=[pltpu.VMEM((8, LANE), F32)],
        compiler_params=_params(1),
    )(logf)


def _rev_cumsum_gates(l, dc_spread, drow, t):
    s = dc_spread.shape[0]
    n = s // t

    def body(dc_ref, drow_ref, out_ref, carry):
        @pl.when(pl.program_id(0) == 0)
        def _():
            carry[...] = jnp.zeros_like(carry)

        lane = lax.broadcasted_iota(jnp.int32, (t, LANE), 1)
        rows = jnp.concatenate([drow_ref[p] for p in range(N_PAIRS)] + [jnp.zeros((LANE - N_HEADS, t), F32)], axis=0)
        x = rows.T
        for p in range(N_PAIRS):
            xp = jnp.where(lane < 2, dc_ref[:, LANE * p:LANE * (p + 1)], 0.0)
            x = x + (xp if p == 0 else pltpu.roll(xp, 2 * p, 1))
        out = _split3_dot(_tri(t, True), x) + carry[0:1, :]
        out_ref[...] = out
        carry[...] = jnp.broadcast_to(out[0:1, :], carry.shape)

    return pl.pallas_call(
        body, name=f"revcumsum_l{l}", grid=(n,),
        in_specs=[pl.BlockSpec((t, N_PAIRS * LANE), lambda i: (n - 1 - i, 0)),
                  pl.BlockSpec((N_PAIRS, 2, t), lambda i: (0, 0, n - 1 - i))],
        out_specs=pl.BlockSpec((t, LANE), lambda i: (n - 1 - i, 0)),
        out_shape=jax.ShapeDtypeStruct((s, LANE), F32),
        scratch_shapes=[pltpu.VMEM((8, LANE), F32)],
        compiler_params=_params(1),
    )(dc_spread, drow)


def _attn_fwd(name, fox, q, k, vt, extra, t):
    s = q.shape[0]
    nq = s // t
    nb = DIL_BACK // t + 1

    def body(*refs):
        if fox:
            q_ref, k_ref, vt_ref, ccol_ref, crow_ref, o_ref, lse_ref, m_scr, l_scr, acc_scr = refs
        else:
            q_ref, k_ref, vt_ref, bias_ref, o_ref, lse_ref, m_scr, l_scr, acc_scr = refs
        i = pl.program_id(1)
        lane = lax.broadcasted_iota(jnp.int32, (t, LANE), 1)
        first = lane < HEAD_DIM
        qq = q_ref[...]
        zero = jnp.zeros_like(qq)
        qh = (jnp.where(first, qq, zero), jnp.where(first, zero, qq))
        m_scr[...] = jnp.full(m_scr.shape, M_INIT, F32)
        l_scr[...] = jnp.zeros_like(l_scr)
        acc_scr[...] = jnp.zeros_like(acc_scr)

        def step(j, diag):
            rows = pl.ds(pl.multiple_of(j * t, t), t)
            ks = k_ref[rows, :]
            vts = vt_ref[:, rows]
            if fox:
                ccol = ccol_ref[rows, :]
            for h in range(2):
                st = _dot_nt(ks, qh[h])
                if fox:
                    st = st + (crow_ref[h:h + 1, :] - ccol[:, h:h + 1])
                    if diag:
                        ki = lax.broadcasted_iota(jnp.int32, (t, t), 0)
                        qi = lax.broadcasted_iota(jnp.int32, (t, t), 1)
                        st = jnp.where(ki <= qi, st, NEG)
                else:
                    st = st + bias_ref[i - j]
                m_old = m_scr[h]
                m_new = jnp.maximum(m_old, jnp.max(st, axis=0, keepdims=True))
                alpha = jnp.exp(m_old - m_new)
                p = jnp.exp(st - m_new)
                l_scr[h] = alpha * l_scr[h] + jnp.sum(p, axis=0, keepdims=True)
                acc_scr[h] = alpha * acc_scr[h] + _dot(vts, p.astype(BF16))
                m_scr[h] = m_new

        if fox:
            lax.fori_loop(0, i, lambda j, c: (step(j, False), c)[1], 0)
            step(i, True)
        else:
            lax.fori_loop(jnp.maximum(i - (nb - 1), 0), i + 1, lambda j, c: (step(j, False), c)[1], 0)

        sub = lax.broadcasted_iota(jnp.int32, (LANE, t), 0)
        ot = jnp.where(sub < HEAD_DIM, acc_scr[0] / l_scr[0], acc_scr[1] / l_scr[1])
        o_ref[...] = ot.T
        for h in range(2):
            lse_ref[h:h + 1, :] = m_scr[h] + jnp.log(l_scr[h])

    qspec = pl.BlockSpec((t, LANE), lambda hp, i: (i, hp))
    kspec = pl.BlockSpec((s, LANE), lambda hp, i: (0, hp))
    vtspec = pl.BlockSpec((LANE, s), lambda hp, i: (hp, 0))
    in_specs = [qspec, kspec, vtspec]
    if fox:
        in_specs += [kspec, pl.BlockSpec((None, 2, t), lambda hp, i: (hp, 0, i))]
    else:
        in_specs += [_full((nb, t, t))]
    return pl.pallas_call(
        body, name=name, grid=(N_PAIRS, nq),
        in_specs=in_specs,
        out_specs=[qspec, pl.BlockSpec((None, 2, t), lambda hp, i: (hp, 0, i))],
        out_shape=[jax.ShapeDtypeStruct((s, D_BRANCH), F32), jax.ShapeDtypeStruct((N_PAIRS, 2, s), F32)],
        scratch_shapes=[pltpu.VMEM((2, 1, t), F32), pltpu.VMEM((2, 1, t), F32), pltpu.VMEM((2, LANE, t), F32)],
        compiler_params=_params(2),
    )(q, k, vt, *extra)


def _attn_bwd(name, fox, q, k, v, do, lse_t, delta_t, pair_offset, extra, t):
    s = q.shape[0]
    nk = s // t
    nb = DIL_BACK // t + 1

    def body(*refs):
        if fox:
            (q_ref, k_ref, v_ref, do_ref, lse_ref, delta_ref, ccol_ref, crow_ref,
             dqt_ref, dk_ref, dv_ref, dc_ref, drow_ref) = refs
        else:
            q_ref, k_ref, v_ref, do_ref, lse_ref, delta_ref, bias_ref, dqt_ref, dk_ref, dv_ref = refs
        j = pl.program_id(1)

        @pl.when(j == 0)
        def _():
            dqt_ref[...] = jnp.zeros_like(dqt_ref)
            if fox:
                drow_ref[...] = jnp.zeros_like(drow_ref)

        lane = lax.broadcasted_iota(jnp.int32, (t, LANE), 1)
        first = lane < HEAD_DIM
        ks = k_ref[...]
        vs = v_ref[...]
        kt = ks.astype(F32).T
        sub = lax.broadcasted_iota(jnp.int32, (LANE, t), 0)
        kth = (jnp.where(sub < HEAD_DIM, kt, 0.0).astype(BF16), jnp.where(sub < HEAD_DIM, 0.0, kt).astype(BF16))
        dk_ref[...] = jnp.zeros_like(dk_ref)
        dv_ref[...] = jnp.zeros_like(dv_ref)
        if fox:
            dc_ref[...] = jnp.zeros_like(dc_ref)
            ccol = ccol_ref[...]

        def step(i, diag):
            rows = pl.ds(pl.multiple_of(i * t, t), t)
            qq = q_ref[rows, :]
            dd = do_ref[rows, :]
            zero = jnp.zeros_like(qq)
            qh = (jnp.where(first, qq, zero), jnp.where(first, zero, qq))
            dh = (jnp.where(first, dd, zero), jnp.where(first, zero, dd))
            for h in range(2):
                st = _dot_nt(ks, qh[h])
                if fox:
                    st = st + (crow_ref[h:h + 1, rows] - ccol[:, h:h + 1])
                    if diag:
                        ki = lax.broadcasted_iota(jnp.int32, (t, t), 0)
                        qi = lax.broadcasted_iota(jnp.int32, (t, t), 1)
                        st = jnp.where(ki <= qi, st, NEG)
                else:
                    st = st + bias_ref[i - j]
                pt = jnp.exp(st - lse_ref[h:h + 1, rows])
                dpt = _dot_nt(vs, dh[h])
                dst = pt * (dpt - delta_ref[h:h + 1, rows])
                dv_ref[...] += _dot(pt.astype(BF16), dh[h])
                dsb = dst.astype(BF16)
                dk_ref[...] += _dot(dsb, qh[h])
                dqt_ref[:, rows] += _dot(kth[h], dsb)
                if fox:
                    dc_ref[...] -= jnp.where(lane == h, jnp.sum(dst, axis=1, keepdims=True), 0.0)
                    drow_ref[h:h + 1, rows] += jnp.sum(dst, axis=0, keepdims=True)

        if fox:
            step(j, True)
            lax.fori_loop(j + 1, nk, lambda i, c: (step(i, False), c)[1], 0)
        else:
            lax.fori_loop(j, jnp.minimum(j + nb, nk), lambda i, c: (step(i, False), c)[1], 0)

    kspec = pl.BlockSpec((t, LANE), lambda hp, j: (j, hp))
    qspec = pl.BlockSpec((s, LANE), lambda hp, j: (0, hp))
    rowspec = pl.BlockSpec((None, 2, s), lambda hp, j: (hp, 0, 0))
    drowspec = pl.BlockSpec((None, 2, s), lambda hp, j: (hp + pair_offset, 0, 0))
    in_specs = [qspec, kspec, kspec, qspec, rowspec, drowspec]
    out_specs = [pl.BlockSpec((LANE, s), lambda hp, j: (hp, 0)), kspec, kspec]
    out_shape = [jax.ShapeDtypeStruct((D_BRANCH, s), F32), jax.ShapeDtypeStruct((s, D_BRANCH), F32),
                 jax.ShapeDtypeStruct((s, D_BRANCH), F32)]
    if fox:
        in_specs += [kspec, rowspec]
        out_specs += [kspec, rowspec]
        out_shape += [jax.ShapeDtypeStruct((s, N_PAIRS * LANE), F32), jax.ShapeDtypeStruct((N_PAIRS, 2, s), F32)]
    else:
        in_specs += [_full((nb, t, t))]
    return pl.pallas_call(
        body, name=name, grid=(N_PAIRS, nk), in_specs=in_specs, out_specs=out_specs, out_shape=out_shape,
        compiler_params=_params(2),
    )(q, k, v, do, lse_t, delta_t, *extra)


def _silu(x):
    return x * _sigmoid(x)


def _outproj(l, h, oa, ob, z, p_i, w_out, ple_g, w_gate, w_ple, tm):
    s = h.shape[0]

    def body(h_ref, oa_ref, ob_ref, ga_ref, gb_ref, p_ref, wo_ref, pg_ref, wg_ref, wp_ref,
             h2_ref, e_ref, gate_ref, out_ref):
        a = jnp.concatenate([oa_ref[...] * _silu(ga_ref[...]), ob_ref[...] * _silu(gb_ref[...])], axis=1)
        h2 = h_ref[...] + _dot(a.astype(BF16), wo_ref[...])
        h2_ref[...] = h2
        r = lax.rsqrt(jnp.mean(h2 * h2, axis=-1, keepdims=True) + EPS)
        n2 = (h2 * r * pg_ref[...]).astype(BF16)
        gate = _sigmoid(_dot(n2, wg_ref[...]))
        e = _dot(p_ref[...].astype(BF16), wp_ref[...])
        e_ref[...] = e
        gate_ref[...] = gate
        out_ref[...] = h2 + e * gate

    row = lambda w: pl.BlockSpec((tm, w), lambda i: (i, 0))
    zcol = lambda k: pl.BlockSpec((tm, D_BRANCH), lambda i: (i, k))
    f = lambda: jax.ShapeDtypeStruct((s, D_MODEL), F32)
    return pl.pallas_call(
        body, name=f"outproj_l{l}", grid=(s // tm,),
        in_specs=[row(D_MODEL), row(D_BRANCH), row(D_BRANCH), zcol(3), zcol(7), row(PLE_DIM),
                  _full((D_MODEL, D_MODEL)), _full((1, D_MODEL)), _full((D_MODEL, D_MODEL)),
                  _full((PLE_DIM, D_MODEL))],
        out_specs=[row(D_MODEL)] * 4, out_shape=[f(), f(), f(), f()],
        compiler_params=_params(1),
    )(h, oa, ob, z, z, p_i, w_out, ple_g, w_gate, w_ple)


def _loss_head(y, target, tm):
    s = y.shape[0]

    def body(y_ref, t_ref, acc_ref, dy_ref):
        @pl.when(pl.program_id(0) == 0)
        def _():
            acc_ref[...] = jnp.zeros_like(acc_ref)

        err = y_ref[...] - t_ref[...]
        dy_ref[...] = err * (1.0 / D_MODEL)
        e2 = err * err
        part = jnp.zeros((8, LANE), F32)
        for r in range(tm // 8):
            for c in range(D_MODEL // LANE):
                part = part + e2[8 * r:8 * (r + 1), LANE * c:LANE * (c + 1)]
        acc_ref[...] += part

    row = pl.BlockSpec((tm, D_MODEL), lambda i: (i, 0))
    return pl.pallas_call(
        body, name="loss_head", grid=(s // tm,), in_specs=[row, row],
        out_specs=[_full((8, LANE)), row],
        out_shape=[jax.ShapeDtypeStruct((8, LANE), F32), jax.ShapeDtypeStruct((s, D_MODEL), F32)],
        compiler_params=_params(1),
    )(y, target)


def _outproj_bwd(l, dout, h2, e, gate, p_i, oa, ob, z, w_out_t, ple_g, w_gate_t, hsel, tm):
    s = dout.shape[0]

    def body(do_ref, h2_ref, e_ref, gate_ref, p_ref, oa_ref, ob_ref, ga_ref, gb_ref, wot_ref, pg_ref,
             wgt_ref, hsel_ref,
             dh2_ref, doa_ref, dob_ref, dga_ref, dgb_ref, delta_ref, dwo_ref, dwg_ref, dwp_ref, dpg_ref):
        @pl.when(pl.program_id(0) == 0)
        def _():
            dwo_ref[...] = jnp.zeros_like(dwo_ref)
            dwg_ref[...] = jnp.zeros_like(dwg_ref)
            dwp_ref[...] = jnp.zeros_like(dwp_ref)
            dpg_ref[...] = jnp.zeros_like(dpg_ref)

        dho = do_ref[...]
        g = gate_ref[...]
        de = (dho * g).astype(BF16)
        dwp_ref[...] += _dot_tn(p_ref[...].astype(BF16), de)
        dpre = (dho * e_ref[...] * g * (1.0 - g)).astype(BF16)
        h2 = h2_ref[...]
        pg = pg_ref[...]
        r = lax.rsqrt(jnp.mean(h2 * h2, axis=-1, keepdims=True) + EPS)
        n2 = (h2 * r * pg).astype(BF16)
        dwg_ref[...] += _dot_tn(n2, dpre)
        dn2 = _dot(dpre, wgt_ref[...])
        dpg_ref[0:1, :] += jnp.sum(dn2 * h2 * r, axis=0, keepdims=True)
        wv = dn2 * pg
        dh2 = dho + r * wv - h2 * (r * r * r) * jnp.mean(wv * h2, axis=-1, keepdims=True)
        dh2_ref[...] = dh2
        dh2b = dh2.astype(BF16)
        ga, gb, oa, ob = ga_ref[...], gb_ref[...], oa_ref[...], ob_ref[...]
        sga, sgb = _sigmoid(ga), _sigmoid(gb)
        a = jnp.concatenate([oa * ga * sga, ob * gb * sgb], axis=1).astype(BF16)
        dwo_ref[...] += _dot_tn(a, dh2b)
        da = _dot(dh2b, wot_ref[...])
        da_a, da_b = da[:, :D_BRANCH], da[:, D_BRANCH:]
        doa = da_a * ga * sga
        dob = da_b * gb * sgb
        doa_ref[...] = doa.astype(BF16)
        dob_ref[...] = dob.astype(BF16)
        dga_ref[...] = (da_a * oa * sga * (1.0 + ga * (1.0 - sga))).astype(BF16)
        dgb_ref[...] = (da_b * ob * sgb * (1.0 + gb * (1.0 - sgb))).astype(BF16)
        prod = jnp.concatenate([doa * oa, dob * ob], axis=1)
        dt = _split_dot(prod, hsel_ref[...]).T
        for pp in range(2 * N_PAIRS):
            delta_ref[pp, :, :] = dt[2 * pp:2 * pp + 2, :]

    row = lambda w: pl.BlockSpec((tm, w), lambda i: (i, 0))
    zcol = lambda k: pl.BlockSpec((tm, D_BRANCH), lambda i: (i, k))
    return pl.pallas_call(
        body, name=f"outproj_bwd_l{l}", grid=(s // tm,),
        in_specs=[row(D_MODEL)] * 4 + [row(PLE_DIM), row(D_BRANCH), row(D_BRANCH), zcol(3), zcol(7),
                                        _full((D_MODEL, D_MODEL)), _full((1, D_MODEL)), _full((D_MODEL, D_MODEL)),
                                        _full((2 * D_BRANCH, LANE))],
        out_specs=[row(D_MODEL)] + [row(D_BRANCH)] * 4
        + [pl.BlockSpec((2 * N_PAIRS, 2, tm), lambda i: (0, 0, i)), _full((D_MODEL, D_MODEL)),
           _full((D_MODEL, D_MODEL)), _full((PLE_DIM, D_MODEL)), _full((8, D_MODEL))],
        out_shape=[jax.ShapeDtypeStruct((s, D_MODEL), F32)] + [jax.ShapeDtypeStruct((s, D_BRANCH), BF16)] * 4
        + [jax.ShapeDtypeStruct((2 * N_PAIRS, 2, s), F32), jax.ShapeDtypeStruct((D_MODEL, D_MODEL), F32),
           jax.ShapeDtypeStruct((D_MODEL, D_MODEL), F32), jax.ShapeDtypeStruct((PLE_DIM, D_MODEL), F32),
           jax.ShapeDtypeStruct((8, D_MODEL), F32)],
        compiler_params=_params(1),
    )(dout, h2, e, gate, p_i, oa, ob, z, z, w_out_t, ple_g, w_gate_t, hsel)


def _inproj_bwd_prep(l, z, dqt_a, dk_a, dv_a, dqt_b, dk_b, dv_b, dga, dgb, dlogf, b_f, qkg, bsum, rope, tm):
    s = z.shape[0]
    rc, rs1, rs2 = rope

    def body(z_ref, dqta_ref, dka_ref, dva_ref, dqtb_ref, dkb_ref, dvb_ref, dga_ref, dgb_ref, dlf_ref,
             bf_ref, qkg_ref, bsum_ref, rc_ref, rs1_ref, rs2_ref, dz_ref, dqkg_ref, dbf_ref):
        @pl.when(pl.program_id(0) == 0)
        def _():
            dqkg_ref[...] = jnp.zeros_like(dqkg_ref)
            dbf_ref[...] = jnp.zeros_like(dbf_ref)

        bs = bsum_ref[...]
        c, s1, s2 = rc_ref[...], rs1_ref[...], rs2_ref[...]

        def unrope(dy):
            return jnp.concatenate([_rope_bwd(dy[:, LANE * k:LANE * (k + 1)], c, s1, s2)
                                    for k in range(D_BRANCH // LANE)], axis=1)

        def norm_bwd(k, row, dy):
            x = z_ref[:, D_BRANCH * k:D_BRANCH * (k + 1)]
            r = lax.rsqrt(_split_dot(x * x, bs) * (1.0 / HEAD_DIM) + EPS)
            dqkg_ref[row:row + 1, :] += jnp.sum(dy * x * r, axis=0, keepdims=True)
            w = dy * qkg_ref[row:row + 1, :]
            dx = r * w - x * (r * r * r) * (_split_dot(w * x, bs) * (1.0 / HEAD_DIM))
            dz_ref[:, D_BRANCH * k:D_BRANCH * (k + 1)] = dx.astype(BF16)

        norm_bwd(0, 0, dqta_ref[...].T * Q_SCALE)
        norm_bwd(1, 1, dka_ref[...])
        dz_ref[:, 2 * D_BRANCH:3 * D_BRANCH] = dva_ref[...].astype(BF16)
        dz_ref[:, 3 * D_BRANCH:4 * D_BRANCH] = dga_ref[...]
        norm_bwd(4, 2, unrope(dqtb_ref[...].T * Q_SCALE))
        norm_bwd(5, 3, unrope(dkb_ref[...]))
        dz_ref[:, 6 * D_BRANCH:7 * D_BRANCH] = dvb_ref[...].astype(BF16)
        dz_ref[:, 7 * D_BRANCH:8 * D_BRANCH] = dgb_ref[...]
        dfa = dlf_ref[...] * _sigmoid(-(z_ref[:, N_MAIN:N_ALL] + bf_ref[...]))
        dz_ref[:, N_MAIN:N_ALL] = dfa.astype(BF16)
        dbf_ref[0:1, :] += jnp.sum(dfa, axis=0, keepdims=True)

    row = lambda w: pl.BlockSpec((tm, w), lambda i: (i, 0))
    colt = pl.BlockSpec((D_BRANCH, tm), lambda i: (0, i))
    return pl.pallas_call(
        body, name=f"inproj_bwd_prep_l{l}", grid=(s // tm,),
        in_specs=[row(N_ALL), colt, row(D_BRANCH), row(D_BRANCH), colt, row(D_BRANCH), row(D_BRANCH),
                  row(D_BRANCH), row(D_BRANCH), row(LANE), _full((1, LANE)), _full((8, D_BRANCH)),
                  _full((D_BRANCH, D_BRANCH)), row(LANE), row(LANE), row(LANE)],
        out_specs=[row(N_ALL), _full((8, D_BRANCH)), _full((8, LANE))],
        out_shape=[jax.ShapeDtypeStruct((s, N_ALL), BF16), jax.ShapeDtypeStruct((8, D_BRANCH), F32),
                   jax.ShapeDtypeStruct((8, LANE), F32)],
        compiler_params=_params(1),
    )(z, dqt_a, dk_a, dv_a, dqt_b, dk_b, dv_b, dga, dgb, dlogf, b_f, qkg, bsum, rc, rs1, rs2)


def _inproj_bwd_dx(l, dz, w_all_t, h, norm_g, dh2, tm):
    s = dz.shape[0]

    def body(dz_ref, wt_ref, h_ref, g_ref, dh2_ref, dh_ref, dg_ref):
        @pl.when(pl.program_id(0) == 0)
        def _():
            dg_ref[...] = jnp.zeros_like(dg_ref)

        du = _dot(dz_ref[...], wt_ref[...])
        hh = h_ref[...]
        g = g_ref[...]
        r = lax.rsqrt(jnp.mean(hh * hh, axis=-1, keepdims=True) + EPS)
        dg_ref[0:1, :] += jnp.sum(du * hh * r, axis=0, keepdims=True)
        wv = du * g
        dh_ref[...] = dh2_ref[...] + r * wv - hh * (r * r * r) * jnp.mean(wv * hh, axis=-1, keepdims=True)

    row = lambda w: pl.BlockSpec((tm, w), lambda i: (i, 0))
    return pl.pallas_call(
        body, name=f"inproj_bwd_dx_l{l}", grid=(s // tm,),
        in_specs=[row(N_ALL), _full((N_ALL, D_MODEL)), row(D_MODEL), _full((1, D_MODEL)), row(D_MODEL)],
        out_specs=[row(D_MODEL), _full((8, D_MODEL))],
        out_shape=[jax.ShapeDtypeStruct((s, D_MODEL), F32), jax.ShapeDtypeStruct((8, D_MODEL), F32)],
        compiler_params=_params(1),
    )(dz, w_all_t, h, norm_g, dh2)


def _inproj_bwd_dw(l, u, dz, tm, tn):
    s = u.shape[0]

    def body(u_ref, dz_ref, dw_ref):
        @pl.when(pl.program_id(1) == 0)
        def _():
            dw_ref[...] = jnp.zeros_like(dw_ref)

        dw_ref[...] += _dot_tn(u_ref[...], dz_ref[...])

    return pl.pallas_call(
        body, name=f"inproj_bwd_dw_l{l}", grid=(N_ALL // tn, s // tm),
        in_specs=[pl.BlockSpec((tm, D_MODEL), lambda n, i: (i, 0)), pl.BlockSpec((tm, tn), lambda n, i: (i, n))],
        out_specs=pl.BlockSpec((D_MODEL, tn), lambda n, i: (0, n)),
        out_shape=jax.ShapeDtypeStruct((D_MODEL, N_ALL), F32),
        compiler_params=_params(2),
    )(u, dz)


def _adamw_math(w, g, m, v):
    m = ADAM_B1 * m + (1.0 - ADAM_B1) * g
    v = ADAM_B2 * v + (1.0 - ADAM_B2) * (g * g)
    m_hat = m / (1.0 - ADAM_B1 ** ADAM_STEP)
    v_hat = v / (1.0 - ADAM_B2 ** ADAM_STEP)
    delta = -ADAM_LR * (m_hat / (jnp.sqrt(v_hat) + ADAM_EPS) + ADAM_WD * w)
    return delta, m, v


def _adamw(name, w, g, m, v):
    nl, r, c = w.shape
    tr = 256 if r % 256 == 0 else r

    def body(w_ref, g_ref, m_ref, v_ref, d_ref, nm_ref, nv_ref):
        d, nm, nv = _adamw_math(w_ref[...], g_ref[...], m_ref[...], v_ref[...])
        d_ref[...] = d
        nm_ref[...] = nm
        nv_ref[...] = nv

    spec = pl.BlockSpec((None, tr, c), lambda a, b: (a, b, 0))
    shp = jax.ShapeDtypeStruct(w.shape, F32)
    return pl.pallas_call(
        body, name=name, grid=(nl, r // tr), in_specs=[spec] * 4, out_specs=[spec] * 3,
        out_shape=[shp, shp, shp], compiler_params=_params(2),
    )(w, g, m, v)


def _add_pair(name, a, b):
    n, r, c = a.shape
    tr = 256 if r % 256 == 0 else r

    def body(a_ref, b_ref, o_ref):
        o_ref[...] = a_ref[...] + b_ref[...]

    spec = pl.BlockSpec((None, tr, c), lambda i, j: (i, j, 0))
    return pl.pallas_call(
        body, name=name, grid=(n, r // tr), in_specs=[spec, spec], out_specs=spec,
        out_shape=jax.ShapeDtypeStruct(a.shape, F32), compiler_params=_params(2),
    )(a, b)


def _sum_slots(name, x):
    n, r, c = x.shape
    tr = 256 if r % 256 == 0 else r

    def body(x_ref, o_ref):
        o_ref[...] = ((x_ref[0] + x_ref[1]) + x_ref[2]) + x_ref[3]

    return pl.pallas_call(
        body, name=name, grid=(r // tr,),
        in_specs=[pl.BlockSpec((n, tr, c), lambda j: (0, j, 0))],
        out_specs=pl.BlockSpec((tr, c), lambda j: (j, 0)),
        out_shape=jax.ShapeDtypeStruct((r, c), F32), compiler_params=_params(1),
    )(x)


def _me():
    return lax.axis_index("x"), lax.axis_index("y"), lax.axis_index("c")


def _other_chips(x, y):
    return [(1 - x, y), (x, 1 - y), (1 - x, 1 - y)]


ANY = pl.BlockSpec(memory_space=pl.ANY)
VMEM_SPEC = pl.BlockSpec(memory_space=pltpu.VMEM)


def _gather_weights(shards):
    n = len(shards)

    def body(*refs):
        ins, outs, stage = refs[:n], refs[n:2 * n], refs[2 * n:3 * n]
        send_sems, recv_sems, local_sems = refs[3 * n:]
        x, y, c = _me()
        k = 2 * x + y
        copies = []
        for t in range(n):
            stage[t][...] = ins[t][...].astype(BF16)
            cp = pltpu.make_async_copy(stage[t], outs[t].at[k], local_sems.at[t])
            cp.start()
            copies.append(cp)
        remote = []
        for t in range(n):
            for j, (px, py) in enumerate(_other_chips(x, y)):
                cp = pltpu.make_async_remote_copy(
                    src_ref=stage[t], dst_ref=outs[t].at[k], send_sem=send_sems.at[3 * t + j],
                    recv_sem=recv_sems.at[3 * t + j], device_id=(px, py, c), device_id_type=MESH)
                cp.start()
                remote.append(cp)
        for cp in copies:
            cp.wait()
        for cp in remote:
            cp.wait()

    return pl.pallas_call(
        body, name="gather_weights",
        in_specs=[VMEM_SPEC] * n, out_specs=[ANY] * n,
        out_shape=[jax.ShapeDtypeStruct((4,) + s.shape, BF16) for s in shards],
        scratch_shapes=[pltpu.VMEM(s.shape, BF16) for s in shards]
        + [pltpu.SemaphoreType.DMA((3 * n,)), pltpu.SemaphoreType.DMA((3 * n,)), pltpu.SemaphoreType.DMA((n,))],
        compiler_params=pltpu.CompilerParams(vmem_limit_bytes=VMEM_LIMIT),
    )(*shards)


def _swap_layers(grads):
    n = len(grads)

    def body(*refs):
        ins, outs = refs[:n], refs[n:2 * n]
        send_sems, recv_sems = refs[2 * n:]
        x, y, c = _me()
        copies = []
        for t in range(n):
            cp = pltpu.make_async_remote_copy(
                src_ref=ins[t].at[1 - c], dst_ref=outs[t], send_sem=send_sems.at[t], recv_sem=recv_sems.at[t],
                device_id=(x, y, 1 - c), device_id_type=MESH)
            cp.start()
            copies.append(cp)
        for cp in copies:
            cp.wait()

    return pl.pallas_call(
        body, name="reduce_swap_layers", in_specs=[ANY] * n, out_specs=[ANY] * n,
        out_shape=[jax.ShapeDtypeStruct(g.shape[1:], F32) for g in grads],
        scratch_shapes=[pltpu.SemaphoreType.DMA((n,)), pltpu.SemaphoreType.DMA((n,))],
    )(*grads)


def _scatter_chips(parts):
    n = len(parts)

    def body(*refs):
        ins, outs = refs[:n], refs[n:2 * n]
        send_sems, recv_sems, local_sems = refs[2 * n:]
        x, y, c = _me()
        k = 2 * x + y
        local, remote = [], []
        for t in range(n):
            cp = pltpu.make_async_copy(ins[t].at[k], outs[t].at[k], local_sems.at[t])
            cp.start()
            local.append(cp)
            for j, (px, py) in enumerate(_other_chips(x, y)):
                cp = pltpu.make_async_remote_copy(
                    src_ref=ins[t].at[2 * px + py], dst_ref=outs[t].at[k], send_sem=send_sems.at[3 * t + j],
                    recv_sem=recv_sems.at[3 * t + j], device_id=(px, py, c), device_id_type=MESH)
                cp.start()
                remote.append(cp)
        for cp in local:
            cp.wait()
        for cp in remote:
            cp.wait()

    return pl.pallas_call(
        body, name="reduce_scatter_chips", in_specs=[ANY] * n, out_specs=[ANY] * n,
        out_shape=[jax.ShapeDtypeStruct(p.shape, F32) for p in parts],
        scratch_shapes=[pltpu.SemaphoreType.DMA((3 * n,)), pltpu.SemaphoreType.DMA((3 * n,)),
                        pltpu.SemaphoreType.DMA((n,))],
    )(*parts)


def _share_layers(totals):
    n = len(totals)

    def body(*refs):
        ins, outs = refs[:n], refs[n:2 * n]
        send_sems, recv_sems, local_sems = refs[2 * n:]
        x, y, c = _me()
        local, remote = [], []
        for t in range(n):
            cp = pltpu.make_async_copy(ins[t], outs[t].at[c], local_sems.at[t])
            cp.start()
            local.append(cp)
            cp = pltpu.make_async_remote_copy(
                src_ref=ins[t], dst_ref=outs[t].at[c], send_sem=send_sems.at[t], recv_sem=recv_sems.at[t],
                device_id=(x, y, 1 - c), device_id_type=MESH)
            cp.start()
            remote.append(cp)
        for cp in local:
            cp.wait()
        for cp in remote:
            cp.wait()

    return pl.pallas_call(
        body, name="reduce_share_layers", in_specs=[ANY] * n, out_specs=[ANY] * n,
        out_shape=[jax.ShapeDtypeStruct((2,) + t.shape, F32) for t in totals],
        scratch_shapes=[pltpu.SemaphoreType.DMA((n,)), pltpu.SemaphoreType.DMA((n,)), pltpu.SemaphoreType.DMA((n,))],
    )(*totals)


def _small_allreduce_adamw(part, w, m, v):
    shape = part.shape

    def body(part_ref, w_ref, m_ref, v_ref, g_ref, d_ref, nm_ref, nv_ref, slots, send_sems, recv_sems):
        x, y, c = _me()
        me = 4 * x + 2 * y + c
        slots[me] = part_ref[...]
        copies = []
        for d in range(1, 8):
            peer = (x ^ (d >> 2), y ^ ((d >> 1) & 1), c ^ (d & 1))
            cp = pltpu.make_async_remote_copy(
                src_ref=part_ref, dst_ref=slots.at[me], send_sem=send_sems.at[d - 1], recv_sem=recv_sems.at[d - 1],
                device_id=peer, device_id_type=MESH)
            cp.start()
            copies.append(cp)
        for cp in copies:
            cp.wait()
        g = slots[0]
        for i in range(1, 8):
            g = g + slots[i]
        g_ref[...] = g
        d, nm, nv = _adamw_math(w_ref[...], g, m_ref[...], v_ref[...])
        d_ref[...] = d
        nm_ref[...] = nm
        nv_ref[...] = nv

    shp = jax.ShapeDtypeStruct(shape, F32)
    return pl.pallas_call(
        body, name="small_allreduce_adamw", in_specs=[VMEM_SPEC] * 4, out_specs=[VMEM_SPEC] * 4,
        out_shape=[shp, shp, shp, shp],
        scratch_shapes=[pltpu.VMEM((8,) + shape, F32), pltpu.SemaphoreType.DMA((7,)), pltpu.SemaphoreType.DMA((7,))],
    )(part, w, m, v)


TM = 256
T_FOX = 1024
T_DIL = 512
T_SCAN = 512
TN_DW = 1408


def _layer_fwd(l, h, p_i, wts, consts):
    w_all, _, w_out, _, w_gate, _, w_ple, norm_g, b_f, qkg, ple_g = wts
    bsum, _, bias_t, rope = consts
    u, z, qa, ka, va, qb, kb, vb, logf, va_t, vb_t = _inproj(l, h, norm_g, w_all, b_f, qkg, bsum, rope, TM)
    c_spread, c_t = _cumsum_gates(l, logf, T_SCAN)
    oa, lse_a = _attn_fwd(f"fox_fwd_l{l}", True, qa, ka, va_t, (c_spread, c_t), T_FOX)
    ob, lse_b = _attn_fwd(f"dil_fwd_l{l}", False, qb, kb, vb_t, (bias_t,), T_DIL)
    h2, e, gate, out = _outproj(l, h, oa, ob, z, p_i, w_out, ple_g, w_gate, w_ple, TM)
    saved = (h, u, z, qa, ka, va, qb, kb, vb, c_spread, c_t, oa, lse_a, ob, lse_b, h2, e, gate)
    return out, saved


def _layer_bwd(l, dout, p_i, wts, consts, saved):
    _, w_all_t, _, w_out_t, _, w_gate_t, _, norm_g, b_f, qkg, ple_g = wts
    bsum, hsel, bias_t, rope = consts
    h, u, z, qa, ka, va, qb, kb, vb, c_spread, c_t, oa, lse_a, ob, lse_b, h2, e, gate = saved
    dh2, doa, dob, dga, dgb, delta_t, dw_out, dw_gate, dw_ple, dple_g = _outproj_bwd(
        l, dout, h2, e, gate, p_i, oa, ob, z, w_out_t, ple_g, w_gate_t, hsel, TM)
    dqt_a, dk_a, dv_a, dc, drow = _attn_bwd(f"fox_bwd_l{l}", True, qa, ka, va, doa, lse_a, delta_t, 0,
                                      (c_spread, c_t), T_FOX)
    dqt_b, dk_b, dv_b = _attn_bwd(f"dil_bwd_l{l}", False, qb, kb, vb, dob, lse_b, delta_t, N_PAIRS,
                                  (bias_t,), T_DIL)
    dlogf = _rev_cumsum_gates(l, dc, drow, T_SCAN)
    dz, dqkg, dbf = _inproj_bwd_prep(l, z, dqt_a, dk_a, dv_a, dqt_b, dk_b, dv_b, dga, dgb, dlogf, b_f, qkg,
                                     bsum, rope, TM)
    dh, dnorm_g = _inproj_bwd_dx(l, dz, w_all_t, h, norm_g, dh2, TM)
    dw_all = _inproj_bwd_dw(l, u, dz, TM * 2, TN_DW)
    return dh, (dw_all, dw_out, dw_ple, dw_gate, dnorm_g[0], dbf[0, :N_HEADS], dqkg[:4], dple_g[0])


N_FA = 2048


def _layer_weights(l, g_in, g_out, g_ple, g_gate, norm_g, b_f, qk_norm_g, ple_norm_g):
    w_in = jnp.transpose(g_in[:, l], (1, 0, 2)).reshape(D_MODEL, N_IN)
    w_all = jnp.concatenate([w_in[:, :N_FA], w_in[:, N_FA + N_HEADS:],
                             jnp.pad(w_in[:, N_FA:N_FA + N_HEADS], ((0, 0), (0, LANE - N_HEADS)))], axis=1)
    w_out = g_out[:, l].reshape(D_MODEL, D_MODEL)
    w_gate = g_gate[:, l].reshape(D_MODEL, D_MODEL)
    w_ple = jnp.transpose(g_ple[:, l], (1, 0, 2)).reshape(PLE_DIM, D_MODEL)
    qkg = jnp.pad(jnp.tile(qk_norm_g[l], (1, N_HEADS)), ((0, 4), (0, 0)))
    bf = jnp.pad(b_f[l], (0, LANE - N_HEADS))[None, :]
    return (w_all, w_all.T, w_out, w_out.T, w_gate, w_gate.T, w_ple, norm_g[l][None, :], bf, qkg,
            ple_norm_g[l][None, :])


def _slot_layout(dw_all, dw_out, dw_ple, dw_gate):
    dw_in = jnp.concatenate([dw_all[:, :N_FA], dw_all[:, N_MAIN:N_MAIN + N_HEADS], dw_all[:, N_FA:N_MAIN]], axis=1)
    return (jnp.transpose(dw_in.reshape(D_MODEL, 4, N_IN // 4), (1, 0, 2)),
            dw_out.reshape(4, D_MODEL // 4, D_MODEL),
            jnp.transpose(dw_ple.reshape(PLE_DIM, 4, D_MODEL // 4), (1, 0, 2)),
            dw_gate.reshape(4, D_MODEL // 4, D_MODEL))


SMALL_ROWS = 40


def _pack_small(norm_g, ple_norm_g, qk_norm_g, b_f):
    flat = jnp.concatenate([norm_g.reshape(-1), ple_norm_g.reshape(-1), qk_norm_g.reshape(-1), b_f.reshape(-1)])
    return jnp.pad(flat, (0, SMALL_ROWS * LANE - flat.shape[0])).reshape(SMALL_ROWS, LANE)


def _unpack_small(packed):
    flat = packed.reshape(-1)
    n1, n2, n3 = 2 * D_MODEL, 4 * D_MODEL, 4 * D_MODEL + 2 * 4 * HEAD_DIM
    return (flat[:n1].reshape(2, D_MODEL), flat[n1:n2].reshape(2, D_MODEL), flat[n2:n3].reshape(2, 4, HEAD_DIM),
            flat[n3:n3 + 2 * N_HEADS].reshape(2, N_HEADS))


def kernel(x, p, positions, norm_g, w_in, b_f, qk_norm_g, w_out, w_ple, ple_norm_g, w_ple_gate, loss_target,
           m_norm_g, m_w_in, m_b_f, m_qk_norm_g, m_w_out, m_w_ple, m_ple_norm_g, m_w_ple_gate,
           v_norm_g, v_w_in, v_b_f, v_qk_norm_g, v_w_out, v_w_ple, v_ple_norm_g, v_w_ple_gate):
    n_layers = w_in.shape[0]
    g_in, g_out, g_ple, g_gate = _gather_weights([w_in, w_out, w_ple, w_ple_gate])
    consts = (_head_block_diag(), _head_select(), _dil_bias(T_DIL), _rope_tables(positions[0]))
    wts = [_layer_weights(l, g_in, g_out, g_ple, g_gate, norm_g, b_f, qk_norm_g, ple_norm_g)
           for l in range(n_layers)]

    h = x[0]
    saved = []
    for l in range(n_layers):
        h, sv = _layer_fwd(l, h, p[l, 0], wts[l], consts)
        saved.append(sv)
    sq, dh = _loss_head(h, loss_target[0], TM)
    loss = lax.psum(0.5 / D_MODEL * jnp.sum(sq), ("x", "y", "c"))

    big = [None] * n_layers
    small = [None] * n_layers
    for l in reversed(range(n_layers)):
        dh, grads = _layer_bwd(l, dh, p[l, 0], wts[l], consts, saved[l])
        big[l] = _slot_layout(*grads[:4])
        small[l] = grads[4:]
    grad_x = dh[None]

    c = lax.axis_index("c")
    stacked = [jnp.stack([big[l][t] for l in range(n_layers)]) for t in range(4)]
    from_sibling = _swap_layers(stacked)
    pair = [_add_pair(f"reduce_pair_sum_{t}", lax.dynamic_index_in_dim(stacked[t], c, 0, keepdims=False),
                      from_sibling[t]) for t in range(4)]
    landed = _scatter_chips(pair)
    totals = [_sum_slots(f"reduce_chip_sum_{t}", landed[t]) for t in range(4)]
    g_w_in, g_w_out, g_w_ple, g_w_gate = _share_layers(totals)

    outs = {}
    for name, w, g, m, v in (("w_in", w_in, g_w_in, m_w_in, v_w_in), ("w_out", w_out, g_w_out, m_w_out, v_w_out),
                             ("w_ple", w_ple, g_w_ple, m_w_ple, v_w_ple),
                             ("w_ple_gate", w_ple_gate, g_w_gate, m_w_ple_gate, v_w_ple_gate)):
        outs[name] = (g,) + tuple(_adamw(f"adamw_{name}", w, g, m, v))

    part = _pack_small(jnp.stack([s[0] for s in small]), jnp.stack([s[3] for s in small]),
                       jnp.stack([s[2] for s in small]).reshape(n_layers, 4, N_HEADS, HEAD_DIM).sum(axis=2),
                       jnp.stack([s[1] for s in small]))
    packed = _small_allreduce_adamw(part, _pack_small(norm_g, ple_norm_g, qk_norm_g, b_f),
                                    _pack_small(m_norm_g, m_ple_norm_g, m_qk_norm_g, m_b_f),
                                    _pack_small(v_norm_g, v_ple_norm_g, v_qk_norm_g, v_b_f))
    sm = [_unpack_small(a) for a in packed]
    for i, name in enumerate(("norm_g", "ple_norm_g", "qk_norm_g", "b_f")):
        outs[name] = tuple(sm[j][i] for j in range(4))

    order = ("norm_g", "w_in", "b_f", "qk_norm_g", "w_out", "w_ple", "ple_norm_g", "w_ple_gate")
    return (loss, grad_x) + tuple(outs[n][j] for j in range(4) for n in order)
```

```python
import functools
from typing import Any, Callable, NamedTuple, Sequence

import numpy as np
import jax
import jax.numpy as jnp
from jax import lax
from jax.experimental import pallas as pl
from jax.experimental.pallas import tpu as pltpu

F32 = jnp.float32
BF16 = jnp.bfloat16
MESH = pl.DeviceIdType.MESH

D_MODEL = 1024
HEAD_DIM = 64
D_BRANCH = 512
N_HEADS = 8
N_PAIRS = 4
N_IN = 4104
N_MAIN = 4096
N_ALL = 4224
PLE_DIM = 256
ROPE_THETA = 500000.0
ROPE_HALF = 8
EPS = 1e-6
NEG = -1e30
M_INIT = -1e29
Q_SCALE = HEAD_DIM ** -0.5
DIL_PATTERNS = ((128, 1), (512, 4), (2048, 16))
DIL_BACK = 2048
ADAM_LR, ADAM_B1, ADAM_B2, ADAM_EPS, ADAM_WD, ADAM_STEP = 0.001, 0.9, 0.999, 1e-08, 0.01, 10
VMEM_LIMIT = 56 * 1024 * 1024
LANE = 128


def _dot(a, b):
    return jnp.dot(a, b, preferred_element_type=F32)


def _dot_nt(a, b):
    return lax.dot_general(a, b, (((1,), (1,)), ((), ())), preferred_element_type=F32)


def _dot_tn(a, b):
    return lax.dot_general(a, b, (((0,), (0,)), ((), ())), preferred_element_type=F32)


def _split_dot(x, w):
    hi = x.astype(BF16)
    lo = (x - hi.astype(F32)).astype(BF16)
    return _dot(hi, w) + _dot(lo, w)


def _split3_dot(w, x):
    hi = x.astype(BF16)
    r1 = x - hi.astype(F32)
    mid = r1.astype(BF16)
    lo = (r1 - mid.astype(F32)).astype(BF16)
    return _dot(w, hi) + _dot(w, mid) + _dot(w, lo)


def _sigmoid(x):
    return 1.0 / (1.0 + jnp.exp(-x))


def _params(n_grid):
    return pltpu.CompilerParams(dimension_semantics=("arbitrary",) * n_grid,
                                vmem_limit_bytes=VMEM_LIMIT)


def _full(shape):
    nd = len(shape)
    return pl.BlockSpec(shape, lambda *_: (0,) * nd)


ANY = pl.BlockSpec(memory_space=pl.ANY)
VMEM_SPEC = pl.BlockSpec(memory_space=pltpu.VMEM)


class _Comm(NamedTuple):
    ins: Sequence[Any]
    out_shapes: Sequence[Any]
    sems: Sequence[Any]
    make: Callable[..., Any]


def _fuse_comm(body, n_in, n_out, comm, grid):
    if comm is None:
        return body
    nci, nco, ncs = len(comm.ins), len(comm.out_shapes), len(comm.sems)

    def fused(*refs):
        a, b = n_in + nci, n_in + nci + n_out
        ins, cins, outs, couts = refs[:n_in], refs[n_in:a], refs[a:b], refs[b:b + nco]
        scratch, sems = refs[b + nco:len(refs) - ncs], refs[len(refs) - ncs:]
        first = functools.reduce(jnp.logical_and, [pl.program_id(d) == 0 for d in range(len(grid))])
        last = functools.reduce(jnp.logical_and, [pl.program_id(d) == n - 1 for d, n in enumerate(grid)])

        @pl.when(first)
        def _():
            for cp in comm.make(cins, couts, sems):
                cp.start()

        body(*ins, *outs, *scratch)

        @pl.when(last)
        def _():
            for cp in comm.make(cins, couts, sems):
                cp.wait()

    return fused


def _comm_args(comm):
    if comm is None:
        return [], [], [], [], []
    return (list(comm.ins), [ANY] * len(comm.ins), [ANY] * len(comm.out_shapes), list(comm.out_shapes),
            list(comm.sems))


def _head_block_diag():
    i = np.arange(D_BRANCH)
    return jnp.asarray((i[:, None] // HEAD_DIM == i[None, :] // HEAD_DIM).astype(np.float32), BF16)


def _head_select():
    i = np.arange(2 * D_BRANCH)
    j = np.arange(LANE)
    return jnp.asarray((i[:, None] // HEAD_DIM == j[None, :]).astype(np.float32), BF16)


def _dil_bias(t):
    nb = DIL_BACK // t + 1
    qi = np.arange(t)[:, None]
    ki = np.arange(t)[None, :]
    tiles = []
    for r in range(nb):
        d = r * t + qi - ki
        mult = np.zeros((t, t), np.int64)
        for window, dil in DIL_PATTERNS:
            mult += ((d >= 0) & (d <= window) & (d % dil == 0)).astype(np.int64)
        b = np.where(mult > 0, np.log(np.maximum(mult, 1)), NEG).astype(np.float32)
        tiles.append(b.T)
    return jnp.asarray(np.stack(tiles))


def _rope_tables(positions):
    inv_freq = ROPE_THETA ** (-jnp.arange(ROPE_HALF, dtype=F32) / ROPE_HALF)
    ang = positions.astype(F32)[:, None] * inv_freq
    cos, sin = jnp.cos(ang), jnp.sin(ang)
    s = positions.shape[0]
    rest = HEAD_DIM - 2 * ROPE_HALF
    one, zero, zero8 = jnp.ones((s, rest), F32), jnp.zeros((s, rest), F32), jnp.zeros((s, ROPE_HALF), F32)
    c = jnp.concatenate([cos, cos, one], axis=1)
    s1 = jnp.concatenate([zero8, sin, zero], axis=1)
    s2 = jnp.concatenate([-sin, zero8, zero], axis=1)
    return tuple(jnp.tile(t, (1, 2)) for t in (c, s1, s2))


def _rope_fwd(x, c, s1, s2):
    return x * c + pltpu.roll(x, ROPE_HALF, 1) * s1 + pltpu.roll(x, LANE - ROPE_HALF, 1) * s2


def _rope_bwd(dy, c, s1, s2):
    return dy * c + pltpu.roll(dy * s1, LANE - ROPE_HALF, 1) + pltpu.roll(dy * s2, ROPE_HALF, 1)


def _log_sigmoid(x):
    return jnp.minimum(x, 0.0) - jnp.log(1.0 + jnp.exp(-jnp.abs(x)))


def _inproj(l, h, norm_g, w_all, b_f, qkg, bsum, rope, tm):
    s = h.shape[0]
    rc, rs1, rs2 = rope

    def body(h_ref, g_ref, w_ref, bf_ref, qkg_ref, bsum_ref, rc_ref, rs1_ref, rs2_ref,
             u_ref, z_ref, qa_ref, ka_ref, va_ref, qb_ref, kb_ref, vb_ref, lf_ref, vat_ref, vbt_ref):
        hh = h_ref[...]
        r = lax.rsqrt(jnp.mean(hh * hh, axis=-1, keepdims=True) + EPS)
        u = (hh * r * g_ref[...]).astype(BF16)
        u_ref[...] = u
        for k in range(N_ALL // LANE // 3):
            cols = slice(3 * LANE * k, 3 * LANE * (k + 1))
            z_ref[:, cols] = _dot(u, w_ref[:, cols])
        bs = bsum_ref[...]

        def head_norm(x, row):
            ms = _split_dot(x * x, bs) * (1.0 / HEAD_DIM)
            return x * lax.rsqrt(ms + EPS) * qkg_ref[row:row + 1, :]

        def seg(k):
            return z_ref[:, D_BRANCH * k:D_BRANCH * (k + 1)]

        qa_ref[...] = (head_norm(seg(0), 0) * Q_SCALE).astype(BF16)
        ka_ref[...] = head_norm(seg(1), 1).astype(BF16)
        va_ref[...] = seg(2).astype(BF16)
        vat_ref[...] = seg(2).T.astype(BF16)
        qn = head_norm(seg(4), 2) * Q_SCALE
        kn = head_norm(seg(5), 3)
        c, s1, s2 = rc_ref[...], rs1_ref[...], rs2_ref[...]
        for k in range(D_BRANCH // LANE):
            cols = slice(LANE * k, LANE * (k + 1))
            qb_ref[:, cols] = _rope_fwd(qn[:, cols], c, s1, s2).astype(BF16)
            kb_ref[:, cols] = _rope_fwd(kn[:, cols], c, s1, s2).astype(BF16)
        vb_ref[...] = seg(6).astype(BF16)
        vbt_ref[...] = seg(6).T.astype(BF16)
        lf_ref[...] = _log_sigmoid(z_ref[:, N_MAIN:N_ALL] + bf_ref[...])

    row = lambda w: pl.BlockSpec((tm, w), lambda i: (i, 0))
    colt = pl.BlockSpec((D_BRANCH, tm), lambda i: (0, i))
    bf = lambda: jax.ShapeDtypeStruct((s, D_BRANCH), BF16)
    bft = lambda: jax.ShapeDtypeStruct((D_BRANCH, s), BF16)
    return pl.pallas_call(
        body, name=f"inproj_l{l}", grid=(s // tm,),
        in_specs=[row(D_MODEL), _full((1, D_MODEL)), _full((D_MODEL, N_ALL)), _full((1, LANE)),
                  _full((8, D_BRANCH)), _full((D_BRANCH, D_BRANCH)), row(LANE), row(LANE), row(LANE)],
        out_specs=[row(D_MODEL), row(N_ALL)] + [row(D_BRANCH)] * 6 + [row(LANE), colt, colt],
        out_shape=[jax.ShapeDtypeStruct((s, D_MODEL), BF16), jax.ShapeDtypeStruct((s, N_ALL), F32),
                   bf(), bf(), bf(), bf(), bf(), bf(), jax.ShapeDtypeStruct((s, LANE), F32), bft(), bft()],
        compiler_params=_params(1),
    )(h, norm_g, w_all, b_f, qkg, bsum, rc, rs1, rs2)


def _tri(t, upper):
    a = lax.broadcasted_iota(jnp.int32, (t, t), 0)
    b = lax.broadcasted_iota(jnp.int32, (t, t), 1)
    return jnp.where((b >= a) if upper else (b <= a), 1.0, 0.0).astype(BF16)


def _cumsum_gates(l, logf, t):
    s = logf.shape[0]

    def body(lf_ref, cs_ref, ct_ref, carry):
        @pl.when(pl.program_id(0) == 0)
        def _():
            carry[...] = jnp.zeros_like(carry)

        x = lf_ref[...]
        c = _split3_dot(_tri(t, False), x) + carry[0:1, :]
        carry[...] = jnp.broadcast_to(c[t - 1:t, :], carry.shape)
        ct = c.T
        for p in range(N_PAIRS):
            cs_ref[:, LANE * p:LANE * (p + 1)] = c if p == 0 else pltpu.roll(c, LANE - 2 * p, 1)
            ct_ref[p, :, :] = ct[2 * p:2 * p + 2, :]

    return pl.pallas_call(
        body, name=f"cumsum_l{l}", grid=(s // t,),
        in_specs=[pl.BlockSpec((t, LANE), lambda i: (i, 0))],
        out_specs=[pl.BlockSpec((t, N_PAIRS * LANE), lambda i: (i, 0)),
                   pl.BlockSpec((N_PAIRS, 2, t), lambda i: (0, 0, i))],
        out_shape=[jax.ShapeDtypeStruct((s, N_PAIRS * LANE), F32), jax.ShapeDtypeStruct((N_PAIRS, 2, s), F32)],
        scratch_shapes=[pltpu.VMEM((8, LANE), F32)],
        compiler_params=_params(1),
    )(logf)


def _rev_cumsum_gates(l, dc_spread, drow, t):
    s = dc_spread.shape[0]
    n = s // t

    def body(dc_ref, drow_ref, out_ref, carry):
        @pl.when(pl.program_id(0) == 0)
        def _():
            carry[...] = jnp.zeros_like(carry)

        lane = lax.broadcasted_iota(jnp.int32, (t, LANE), 1)
        rows = jnp.concatenate([drow_ref[p] for p in range(N_PAIRS)] + [jnp.zeros((LANE - N_HEADS, t), F32)], axis=0)
        x = rows.T
        for p in range(N_PAIRS):
            xp = jnp.where(lane < 2, dc_ref[:, LANE * p:LANE * (p + 1)], 0.0)
            x = x + (xp if p == 0 else pltpu.roll(xp, 2 * p, 1))
        out = _split3_dot(_tri(t, True), x) + carry[0:1, :]
        out_ref[...] = out
        carry[...] = jnp.broadcast_to(out[0:1, :], carry.shape)

    return pl.pallas_call(
        body, name=f"revcumsum_l{l}", grid=(n,),
        in_specs=[pl.BlockSpec((t, N_PAIRS * LANE), lambda i: (n - 1 - i, 0)),
                  pl.BlockSpec((N_PAIRS, 2, t), lambda i: (0, 0, n - 1 - i))],
        out_specs=pl.BlockSpec((t, LANE), lambda i: (n - 1 - i, 0)),
        out_shape=jax.ShapeDtypeStruct((s, LANE), F32),
        scratch_shapes=[pltpu.VMEM((8, LANE), F32)],
        compiler_params=_params(1),
    )(dc_spread, drow)


def _attn_fwd(name, fox, q, k, vt, extra, t, comm=None):
    s = q.shape[0]
    nq = s // t
    nb = DIL_BACK // t + 1

    def body(*refs):
        if fox:
            q_ref, k_ref, vt_ref, ccol_ref, crow_ref, o_ref, lse_ref, m_scr, l_scr, acc_scr = refs
        else:
            q_ref, k_ref, vt_ref, bias_ref, o_ref, lse_ref, m_scr, l_scr, acc_scr = refs
        i = pl.program_id(1)
        lane = lax.broadcasted_iota(jnp.int32, (t, LANE), 1)
        first = lane < HEAD_DIM
        qq = q_ref[...]
        zero = jnp.zeros_like(qq)
        qh = (jnp.where(first, qq, zero), jnp.where(first, zero, qq))
        m_scr[...] = jnp.full(m_scr.shape, M_INIT, F32)
        l_scr[...] = jnp.zeros_like(l_scr)
        acc_scr[...] = jnp.zeros_like(acc_scr)

        def step(j, diag):
            rows = pl.ds(pl.multiple_of(j * t, t), t)
            ks = k_ref[rows, :]
            vts = vt_ref[:, rows]
            if fox:
                ccol = ccol_ref[rows, :]
            for h in range(2):
                st = _dot_nt(ks, qh[h])
                if fox:
                    st = st + (crow_ref[h:h + 1, :] - ccol[:, h:h + 1])
                    if diag:
                        ki = lax.broadcasted_iota(jnp.int32, (t, t), 0)
                        qi = lax.broadcasted_iota(jnp.int32, (t, t), 1)
                        st = jnp.where(ki <= qi, st, NEG)
                else:
                    st = st + bias_ref[i - j]
                m_old = m_scr[h]
                m_new = jnp.maximum(m_old, jnp.max(st, axis=0, keepdims=True))
                alpha = jnp.exp(m_old - m_new)
                p = jnp.exp(st - m_new)
                l_scr[h] = alpha * l_scr[h] + jnp.sum(p, axis=0, keepdims=True)
                acc_scr[h] = alpha * acc_scr[h] + _dot(vts, p.astype(BF16))
                m_scr[h] = m_new

        if fox:
            lax.fori_loop(0, i, lambda j, c: (step(j, False), c)[1], 0)
            step(i, True)
        else:
            lax.fori_loop(jnp.maximum(i - (nb - 1), 0), i + 1, lambda j, c: (step(j, False), c)[1], 0)

        sub = lax.broadcasted_iota(jnp.int32, (LANE, t), 0)
        ot = jnp.where(sub < HEAD_DIM, acc_scr[0] / l_scr[0], acc_scr[1] / l_scr[1])
        o_ref[...] = ot.T
        for h in range(2):
            lse_ref[h:h + 1, :] = m_scr[h] + jnp.log(l_scr[h])

    qspec = pl.BlockSpec((t, LANE), lambda hp, i: (i, hp))
    kspec = pl.BlockSpec((s, LANE), lambda hp, i: (0, hp))
    vtspec = pl.BlockSpec((LANE, s), lambda hp, i: (hp, 0))
    in_specs = [qspec, kspec, vtspec]
    if fox:
        in_specs += [kspec, pl.BlockSpec((None, 2, t), lambda hp, i: (hp, 0, i))]
    else:
        in_specs += [_full((nb, t, t))]
    grid = (N_PAIRS, nq)
    c_in, c_ispec, c_ospec, c_oshape, c_scr = _comm_args(comm)
    return pl.pallas_call(
        _fuse_comm(body, len(in_specs), 2, comm, grid), name=name, grid=grid,
        in_specs=in_specs + c_ispec,
        out_specs=[qspec, pl.BlockSpec((None, 2, t), lambda hp, i: (hp, 0, i))] + c_ospec,
        out_shape=[jax.ShapeDtypeStruct((s, D_BRANCH), F32), jax.ShapeDtypeStruct((N_PAIRS, 2, s), F32)] + c_oshape,
        scratch_shapes=[pltpu.VMEM((2, 1, t), F32), pltpu.VMEM((2, 1, t), F32), pltpu.VMEM((2, LANE, t), F32)]
        + c_scr,
        compiler_params=_params(2),
    )(q, k, vt, *extra, *c_in)


def _attn_bwd(name, fox, q, k, v, do, lse_t, delta_t, pair_offset, extra, t, comm=None):
    s = q.shape[0]
    nk = s // t
    nb = DIL_BACK // t + 1

    def body(*refs):
        if fox:
            (q_ref, k_ref, v_ref, do_ref, lse_ref, delta_ref, ccol_ref, crow_ref,
             dqt_ref, dk_ref, dv_ref, dc_ref, drow_ref) = refs
        else:
            q_ref, k_ref, v_ref, do_ref, lse_ref, delta_ref, bias_ref, dqt_ref, dk_ref, dv_ref = refs
        j = pl.program_id(1)

        @pl.when(j == 0)
        def _():
            dqt_ref[...] = jnp.zeros_like(dqt_ref)
            if fox:
                drow_ref[...] = jnp.zeros_like(drow_ref)

        lane = lax.broadcasted_iota(jnp.int32, (t, LANE), 1)
        first = lane < HEAD_DIM
        ks = k_ref[...]
        vs = v_ref[...]
        kt = ks.astype(F32).T
        sub = lax.broadcasted_iota(jnp.int32, (LANE, t), 0)
        kth = (jnp.where(sub < HEAD_DIM, kt, 0.0).astype(BF16), jnp.where(sub < HEAD_DIM, 0.0, kt).astype(BF16))
        dk_ref[...] = jnp.zeros_like(dk_ref)
        dv_ref[...] = jnp.zeros_like(dv_ref)
        if fox:
            dc_ref[...] = jnp.zeros_like(dc_ref)
            ccol = ccol_ref[...]

        def step(i, diag):
            rows = pl.ds(pl.multiple_of(i * t, t), t)
            qq = q_ref[rows, :]
            dd = do_ref[rows, :]
            zero = jnp.zeros_like(qq)
            qh = (jnp.where(first, qq, zero), jnp.where(first, zero, qq))
            dh = (jnp.where(first, dd, zero), jnp.where(first, zero, dd))
            for h in range(2):
                st = _dot_nt(ks, qh[h])
                if fox:
                    st = st + (crow_ref[h:h + 1, rows] - ccol[:, h:h + 1])
                    if diag:
                        ki = lax.broadcasted_iota(jnp.int32, (t, t), 0)
                        qi = lax.broadcasted_iota(jnp.int32, (t, t), 1)
                        st = jnp.where(ki <= qi, st, NEG)
                else:
                    st = st + bias_ref[i - j]
                pt = jnp.exp(st - lse_ref[h:h + 1, rows])
                dpt = _dot_nt(vs, dh[h])
                dst = pt * (dpt - delta_ref[h:h + 1, rows])
                dv_ref[...] += _dot(pt.astype(BF16), dh[h])
                dsb = dst.astype(BF16)
                dk_ref[...] += _dot(dsb, qh[h])
                dqt_ref[:, rows] += _dot(kth[h], dsb)
                if fox:
                    dc_ref[...] -= jnp.where(lane == h, jnp.sum(dst, axis=1, keepdims=True), 0.0)
                    drow_ref[h:h + 1, rows] += jnp.sum(dst, axis=0, keepdims=True)

        if fox:
            step(j, True)
            lax.fori_loop(j + 1, nk, lambda i, c: (step(i, False), c)[1], 0)
        else:
            lax.fori_loop(j, jnp.minimum(j + nb, nk), lambda i, c: (step(i, False), c)[1], 0)

    kspec = pl.BlockSpec((t, LANE), lambda hp, j: (j, hp))
    qspec = pl.BlockSpec((s, LANE), lambda hp, j: (0, hp))
    rowspec = pl.BlockSpec((None, 2, s), lambda hp, j: (hp, 0, 0))
    drowspec = pl.BlockSpec((None, 2, s), lambda hp, j: (hp + pair_offset, 0, 0))
    in_specs = [qspec, kspec, kspec, qspec, rowspec, drowspec]
    out_specs = [pl.BlockSpec((LANE, s), lambda hp, j: (hp, 0)), kspec, kspec]
    out_shape = [jax.ShapeDtypeStruct((D_BRANCH, s), F32), jax.ShapeDtypeStruct((s, D_BRANCH), F32),
                 jax.ShapeDtypeStruct((s, D_BRANCH), F32)]
    if fox:
        in_specs += [kspec, rowspec]
        out_specs += [kspec, rowspec]
        out_shape += [jax.ShapeDtypeStruct((s, N_PAIRS * LANE), F32), jax.ShapeDtypeStruct((N_PAIRS, 2, s), F32)]
    else:
        in_specs += [_full((nb, t, t))]
    grid = (N_PAIRS, nk)
    c_in, c_ispec, c_ospec, c_oshape, c_scr = _comm_args(comm)
    return pl.pallas_call(
        _fuse_comm(body, len(in_specs), len(out_specs), comm, grid), name=name, grid=grid,
        in_specs=in_specs + c_ispec, out_specs=out_specs + c_ospec, out_shape=out_shape + c_oshape,
        scratch_shapes=c_scr, compiler_params=_params(2),
    )(q, k, v, do, lse_t, delta_t, *extra, *c_in)


def _silu(x):
    return x * _sigmoid(x)


def _outproj(l, h, oa, ob, z, p_i, w_out, ple_g, w_gate, w_ple, tm):
    s = h.shape[0]

    def body(h_ref, oa_ref, ob_ref, ga_ref, gb_ref, p_ref, wo_ref, pg_ref, wg_ref, wp_ref,
             h2_ref, e_ref, gate_ref, out_ref):
        a = jnp.concatenate([oa_ref[...] * _silu(ga_ref[...]), ob_ref[...] * _silu(gb_ref[...])], axis=1)
        h2 = h_ref[...] + _dot(a.astype(BF16), wo_ref[...])
        h2_ref[...] = h2
        r = lax.rsqrt(jnp.mean(h2 * h2, axis=-1, keepdims=True) + EPS)
        n2 = (h2 * r * pg_ref[...]).astype(BF16)
        gate = _sigmoid(_dot(n2, wg_ref[...]))
        e = _dot(p_ref[...].astype(BF16), wp_ref[...])
        e_ref[...] = e
        gate_ref[...] = gate
        out_ref[...] = h2 + e * gate

    row = lambda w: pl.BlockSpec((tm, w), lambda i: (i, 0))
    zcol = lambda k: pl.BlockSpec((tm, D_BRANCH), lambda i: (i, k))
    f = lambda: jax.ShapeDtypeStruct((s, D_MODEL), F32)
    return pl.pallas_call(
        body, name=f"outproj_l{l}", grid=(s // tm,),
        in_specs=[row(D_MODEL), row(D_BRANCH), row(D_BRANCH), zcol(3), zcol(7), row(PLE_DIM),
                  _full((D_MODEL, D_MODEL)), _full((1, D_MODEL)), _full((D_MODEL, D_MODEL)),
                  _full((PLE_DIM, D_MODEL))],
        out_specs=[row(D_MODEL)] * 4, out_shape=[f(), f(), f(), f()],
        compiler_params=_params(1),
    )(h, oa, ob, z, z, p_i, w_out, ple_g, w_gate, w_ple)


def _loss_head(y, target, tm):
    s = y.shape[0]

    def body(y_ref, t_ref, acc_ref, dy_ref):
        @pl.when(pl.program_id(0) == 0)
        def _():
            acc_ref[...] = jnp.zeros_like(acc_ref)

        err = y_ref[...] - t_ref[...]
        dy_ref[...] = err * (1.0 / D_MODEL)
        e2 = err * err
        part = jnp.zeros((8, LANE), F32)
        for r in range(tm // 8):
            for c in range(D_MODEL // LANE):
                part = part + e2[8 * r:8 * (r + 1), LANE * c:LANE * (c + 1)]
        acc_ref[...] += part

    row = pl.BlockSpec((tm, D_MODEL), lambda i: (i, 0))
    return pl.pallas_call(
        body, name="loss_head", grid=(s // tm,), in_specs=[row, row],
        out_specs=[_full((8, LANE)), row],
        out_shape=[jax.ShapeDtypeStruct((8, LANE), F32), jax.ShapeDtypeStruct((s, D_MODEL), F32)],
        compiler_params=_params(1),
    )(y, target)


def _outproj_bwd(l, dout, h2, e, gate, p_i, oa, ob, z, w_out_t, ple_g, w_gate_t, hsel, tm, comm=None):
    s = dout.shape[0]

    def body(do_ref, h2_ref, e_ref, gate_ref, p_ref, oa_ref, ob_ref, ga_ref, gb_ref, wot_ref, pg_ref,
             wgt_ref, hsel_ref,
             dh2_ref, doa_ref, dob_ref, dga_ref, dgb_ref, delta_ref, dwo_ref, dwg_ref, dwp_ref, dpg_ref):
        @pl.when(pl.program_id(0) == 0)
        def _():
            dwo_ref[...] = jnp.zeros_like(dwo_ref)
            dwg_ref[...] = jnp.zeros_like(dwg_ref)
            dwp_ref[...] = jnp.zeros_like(dwp_ref)
            dpg_ref[...] = jnp.zeros_like(dpg_ref)

        dho = do_ref[...]
        g = gate_ref[...]
        de = (dho * g).astype(BF16)
        dwp_ref[...] += _dot_tn(p_ref[...].astype(BF16), de)
        dpre = (dho * e_ref[...] * g * (1.0 - g)).astype(BF16)
        h2 = h2_ref[...]
        pg = pg_ref[...]
        r = lax.rsqrt(jnp.mean(h2 * h2, axis=-1, keepdims=True) + EPS)
        n2 = (h2 * r * pg).astype(BF16)
        dwg_ref[...] += _dot_tn(n2, dpre)
        dn2 = _dot(dpre, wgt_ref[...])
        dpg_ref[0:1, :] += jnp.sum(dn2 * h2 * r, axis=0, keepdims=True)
        wv = dn2 * pg
        dh2 = dho + r * wv - h2 * (r * r * r) * jnp.mean(wv * h2, axis=-1, keepdims=True)
        dh2_ref[...] = dh2
        dh2b = dh2.astype(BF16)
        ga, gb, oa, ob = ga_ref[...], gb_ref[...], oa_ref[...], ob_ref[...]
        sga, sgb = _sigmoid(ga), _sigmoid(gb)
        a = jnp.concatenate([oa * ga * sga, ob * gb * sgb], axis=1).astype(BF16)
        dwo_ref[...] += _dot_tn(a, dh2b)
        da = _dot(dh2b, wot_ref[...])
        da_a, da_b = da[:, :D_BRANCH], da[:, D_BRANCH:]
        doa = da_a * ga * sga
        dob = da_b * gb * sgb
        doa_ref[...] = doa.astype(BF16)
        dob_ref[...] = dob.astype(BF16)
        dga_ref[...] = (da_a * oa * sga * (1.0 + ga * (1.0 - sga))).astype(BF16)
        dgb_ref[...] = (da_b * ob * sgb * (1.0 + gb * (1.0 - sgb))).astype(BF16)
        prod = jnp.concatenate([doa * oa, dob * ob], axis=1)
        dt = _split_dot(prod, hsel_ref[...]).T
        for pp in range(2 * N_PAIRS):
            delta_ref[pp, :, :] = dt[2 * pp:2 * pp + 2, :]

    row = lambda w: pl.BlockSpec((tm, w), lambda i: (i, 0))
    zcol = lambda k: pl.BlockSpec((tm, D_BRANCH), lambda i: (i, k))
    grid = (s // tm,)
    c_in, c_ispec, c_ospec, c_oshape, c_scr = _comm_args(comm)
    return pl.pallas_call(
        _fuse_comm(body, 13, 10, comm, grid), name=f"outproj_bwd_l{l}", grid=grid,
        in_specs=[row(D_MODEL)] * 4 + [row(PLE_DIM), row(D_BRANCH), row(D_BRANCH), zcol(3), zcol(7),
                                        _full((D_MODEL, D_MODEL)), _full((1, D_MODEL)), _full((D_MODEL, D_MODEL)),
                                        _full((2 * D_BRANCH, LANE))] + c_ispec,
        out_specs=[row(D_MODEL)] + [row(D_BRANCH)] * 4
        + [pl.BlockSpec((2 * N_PAIRS, 2, tm), lambda i: (0, 0, i)), _full((D_MODEL, D_MODEL)),
           _full((D_MODEL, D_MODEL)), _full((PLE_DIM, D_MODEL)), _full((8, D_MODEL))] + c_ospec,
        out_shape=[jax.ShapeDtypeStruct((s, D_MODEL), F32)] + [jax.ShapeDtypeStruct((s, D_BRANCH), BF16)] * 4
        + [jax.ShapeDtypeStruct((2 * N_PAIRS, 2, s), F32), jax.ShapeDtypeStruct((D_MODEL, D_MODEL), F32),
           jax.ShapeDtypeStruct((D_MODEL, D_MODEL), F32), jax.ShapeDtypeStruct((PLE_DIM, D_MODEL), F32),
           jax.ShapeDtypeStruct((8, D_MODEL), F32)] + c_oshape,
        scratch_shapes=c_scr, compiler_params=_params(1),
    )(dout, h2, e, gate, p_i, oa, ob, z, z, w_out_t, ple_g, w_gate_t, hsel, *c_in)


def _inproj_bwd_prep(l, z, dqt_a, dk_a, dv_a, dqt_b, dk_b, dv_b, dga, dgb, dlogf, b_f, qkg, bsum, rope, tm):
    s = z.shape[0]
    rc, rs1, rs2 = rope

    def body(z_ref, dqta_ref, dka_ref, dva_ref, dqtb_ref, dkb_ref, dvb_ref, dga_ref, dgb_ref, dlf_ref,
             bf_ref, qkg_ref, bsum_ref, rc_ref, rs1_ref, rs2_ref, dz_ref, dqkg_ref, dbf_ref):
        @pl.when(pl.program_id(0) == 0)
        def _():
            dqkg_ref[...] = jnp.zeros_like(dqkg_ref)
            dbf_ref[...] = jnp.zeros_like(dbf_ref)

        bs = bsum_ref[...]
        c, s1, s2 = rc_ref[...], rs1_ref[...], rs2_ref[...]

        def unrope(dy):
            return jnp.concatenate([_rope_bwd(dy[:, LANE * k:LANE * (k + 1)], c, s1, s2)
                                    for k in range(D_BRANCH // LANE)], axis=1)

        def norm_bwd(k, row, dy):
            x = z_ref[:, D_BRANCH * k:D_BRANCH * (k + 1)]
            r = lax.rsqrt(_split_dot(x * x, bs) * (1.0 / HEAD_DIM) + EPS)
            dqkg_ref[row:row + 1, :] += jnp.sum(dy * x * r, axis=0, keepdims=True)
            w = dy * qkg_ref[row:row + 1, :]
            dx = r * w - x * (r * r * r) * (_split_dot(w * x, bs) * (1.0 / HEAD_DIM))
            dz_ref[:, D_BRANCH * k:D_BRANCH * (k + 1)] = dx.astype(BF16)

        norm_bwd(0, 0, dqta_ref[...].T * Q_SCALE)
        norm_bwd(1, 1, dka_ref[...])
        dz_ref[:, 2 * D_BRANCH:3 * D_BRANCH] = dva_ref[...].astype(BF16)
        dz_ref[:, 3 * D_BRANCH:4 * D_BRANCH] = dga_ref[...]
        norm_bwd(4, 2, unrope(dqtb_ref[...].T * Q_SCALE))
        norm_bwd(5, 3, unrope(dkb_ref[...]))
        dz_ref[:, 6 * D_BRANCH:7 * D_BRANCH] = dvb_ref[...].astype(BF16)
        dz_ref[:, 7 * D_BRANCH:8 * D_BRANCH] = dgb_ref[...]
        dfa = dlf_ref[...] * _sigmoid(-(z_ref[:, N_MAIN:N_ALL] + bf_ref[...]))
        dz_ref[:, N_MAIN:N_ALL] = dfa.astype(BF16)
        dbf_ref[0:1, :] += jnp.sum(dfa, axis=0, keepdims=True)

    row = lambda w: pl.BlockSpec((tm, w), lambda i: (i, 0))
    colt = pl.BlockSpec((D_BRANCH, tm), lambda i: (0, i))
    return pl.pallas_call(
        body, name=f"inproj_bwd_prep_l{l}", grid=(s // tm,),
        in_specs=[row(N_ALL), colt, row(D_BRANCH), row(D_BRANCH), colt, row(D_BRANCH), row(D_BRANCH),
                  row(D_BRANCH), row(D_BRANCH), row(LANE), _full((1, LANE)), _full((8, D_BRANCH)),
                  _full((D_BRANCH, D_BRANCH)), row(LANE), row(LANE), row(LANE)],
        out_specs=[row(N_ALL), _full((8, D_BRANCH)), _full((8, LANE))],
        out_shape=[jax.ShapeDtypeStruct((s, N_ALL), BF16), jax.ShapeDtypeStruct((8, D_BRANCH), F32),
                   jax.ShapeDtypeStruct((8, LANE), F32)],
        compiler_params=_params(1),
    )(z, dqt_a, dk_a, dv_a, dqt_b, dk_b, dv_b, dga, dgb, dlogf, b_f, qkg, bsum, rc, rs1, rs2)


def _inproj_bwd_dx(l, dz, w_all_t, h, norm_g, dh2, tm):
    s = dz.shape[0]

    def body(dz_ref, wt_ref, h_ref, g_ref, dh2_ref, dh_ref, dg_ref):
        @pl.when(pl.program_id(0) == 0)
        def _():
            dg_ref[...] = jnp.zeros_like(dg_ref)

        du = _dot(dz_ref[...], wt_ref[...])
        hh = h_ref[...]
        g = g_ref[...]
        r = lax.rsqrt(jnp.mean(hh * hh, axis=-1, keepdims=True) + EPS)
        dg_ref[0:1, :] += jnp.sum(du * hh * r, axis=0, keepdims=True)
        wv = du * g
        dh_ref[...] = dh2_ref[...] + r * wv - hh * (r * r * r) * jnp.mean(wv * hh, axis=-1, keepdims=True)

    row = lambda w: pl.BlockSpec((tm, w), lambda i: (i, 0))
    return pl.pallas_call(
        body, name=f"inproj_bwd_dx_l{l}", grid=(s // tm,),
        in_specs=[row(N_ALL), _full((N_ALL, D_MODEL)), row(D_MODEL), _full((1, D_MODEL)), row(D_MODEL)],
        out_specs=[row(D_MODEL), _full((8, D_MODEL))],
        out_shape=[jax.ShapeDtypeStruct((s, D_MODEL), F32), jax.ShapeDtypeStruct((8, D_MODEL), F32)],
        compiler_params=_params(1),
    )(dz, w_all_t, h, norm_g, dh2)


def _inproj_bwd_dw(l, u, dz, tm, tn):
    s = u.shape[0]

    def body(u_ref, dz_ref, dw_ref):
        @pl.when(pl.program_id(1) == 0)
        def _():
            dw_ref[...] = jnp.zeros_like(dw_ref)

        dw_ref[...] += _dot_tn(u_ref[...], dz_ref[...])

    return pl.pallas_call(
        body, name=f"inproj_bwd_dw_l{l}", grid=(N_ALL // tn, s // tm),
        in_specs=[pl.BlockSpec((tm, D_MODEL), lambda n, i: (i, 0)), pl.BlockSpec((tm, tn), lambda n, i: (i, n))],
        out_specs=pl.BlockSpec((D_MODEL, tn), lambda n, i: (0, n)),
        out_shape=jax.ShapeDtypeStruct((D_MODEL, N_ALL), F32),
        compiler_params=_params(2),
    )(u, dz)


def _adamw_math(w, g, m, v):
    m = ADAM_B1 * m + (1.0 - ADAM_B1) * g
    v = ADAM_B2 * v + (1.0 - ADAM_B2) * (g * g)
    m_hat = m / (1.0 - ADAM_B1 ** ADAM_STEP)
    v_hat = v / (1.0 - ADAM_B2 ** ADAM_STEP)
    delta = -ADAM_LR * (m_hat / (jnp.sqrt(v_hat) + ADAM_EPS) + ADAM_WD * w)
    return delta, m, v


def _adamw(name, w, g_layers, m, v):
    nl, r, c = w.shape
    hr = r // 2
    tr = 128 if hr % 128 == 0 else hr
    nb = hr // tr

    def body(w_ref, g0_ref, g1_ref, m_ref, v_ref, g_ref, d_ref, nm_ref, nv_ref):
        g = jnp.where(pl.program_id(0) == 0, g0_ref[...], g1_ref[...])
        d, nm, nv = _adamw_math(w_ref[...], g, m_ref[...], v_ref[...])
        g_ref[...] = g
        d_ref[...] = d
        nm_ref[...] = nm
        nv_ref[...] = nv

    spec = pl.BlockSpec((None, tr, c), lambda a, b: (a, b, 0))
    gspec = pl.BlockSpec((None, tr, c), lambda a, b: (b // nb, b % nb, 0))
    shp = jax.ShapeDtypeStruct(w.shape, F32)
    return pl.pallas_call(
        body, name=name, grid=(nl, r // tr), in_specs=[spec, gspec, gspec, spec, spec], out_specs=[spec] * 4,
        out_shape=[shp, shp, shp, shp], compiler_params=_params(2),
    )(w, g_layers[0], g_layers[1], m, v)


def _pair_sum(name, g, x, c):
    n, r, cc = g.shape
    hr = r // 2
    tr = 128 if hr % 128 == 0 else hr
    nb = hr // tr

    def body(c_ref, g_ref, x_ref, o_ref):
        o_ref[...] = g_ref[...] + x_ref[...]

    spec = pl.BlockSpec((None, tr, cc), lambda i, j, c_ref: (i, j, 0))
    return pl.pallas_call(
        body, name=name,
        grid_spec=pltpu.PrefetchScalarGridSpec(
            num_scalar_prefetch=1, grid=(n, nb),
            in_specs=[pl.BlockSpec((None, tr, cc), lambda i, j, c_ref: (i, c_ref[0] * nb + j, 0)), spec],
            out_specs=spec),
        out_shape=jax.ShapeDtypeStruct((n, hr, cc), F32), compiler_params=_params(2),
    )(c, g, x)


def _sum_slots(name, x):
    n, r, c = x.shape
    tr = 128 if r % 128 == 0 else r

    def body(x_ref, o_ref):
        o_ref[...] = ((x_ref[0] + x_ref[1]) + x_ref[2]) + x_ref[3]

    return pl.pallas_call(
        body, name=name, grid=(r // tr,),
        in_specs=[pl.BlockSpec((n, tr, c), lambda j: (0, j, 0))],
        out_specs=pl.BlockSpec((tr, c), lambda j: (j, 0)),
        out_shape=jax.ShapeDtypeStruct((r, c), F32), compiler_params=_params(1),
    )(x)


def _me():
    return lax.axis_index("x"), lax.axis_index("y"), lax.axis_index("c")


def _other_chips(x, y):
    return [(1 - x, y), (x, 1 - y), (1 - x, 1 - y)]


def _dma_sems(*counts):
    return [pltpu.SemaphoreType.DMA((n,)) for n in counts]


def _half_rows(rows, which, align):
    return pl.ds(pl.multiple_of(which * (rows // 2), align), rows // 2)


def _gather_first_layer(shards):
    n = len(shards)

    def body(*refs):
        ins, outs, keep, stage = refs[:n], refs[n:2 * n], refs[2 * n:3 * n], refs[3 * n:4 * n]
        ici_send, ici_recv, d2d_send, d2d_recv, local_sems = refs[4 * n:]
        x, y, c = _me()
        k = 2 * x + y
        chips = _other_chips(x, y)
        local, first, passed = [], [], []
        for t in range(n):
            stage[t][...] = ins[t][0].astype(BF16)
            keep[t][...] = ins[t][1].astype(BF16)
            cp = pltpu.make_async_copy(stage[t], outs[t].at[k], local_sems.at[t])
            cp.start()
            local.append(cp)
        for t in range(n):
            mine = _half_rows(shards[t].shape[1], c, 16)
            for j, (px, py) in enumerate(chips):
                cp = pltpu.make_async_remote_copy(
                    src_ref=stage[t].at[mine], dst_ref=outs[t].at[k, mine], send_sem=ici_send.at[3 * t + j],
                    recv_sem=ici_recv.at[3 * t + j], device_id=(px, py, c), device_id_type=MESH)
                cp.start()
                first.append(cp)
        for t in range(n):
            mine = _half_rows(shards[t].shape[1], c, 16)
            for j, (px, py) in enumerate(chips):
                landed = outs[t].at[2 * px + py, mine]
                first[3 * t + j].wait_recv()
                cp = pltpu.make_async_remote_copy(
                    src_ref=landed, dst_ref=landed, send_sem=d2d_send.at[3 * t + j],
                    recv_sem=d2d_recv.at[3 * t + j], device_id=(x, y, 1 - c), device_id_type=MESH)
                cp.start()
                passed.append(cp)
        for cp in passed:
            cp.wait_recv()
        for cp in first + passed:
            cp.wait_send()
        for cp in local:
            cp.wait()

    return pl.pallas_call(
        body, name="gather_first_layer",
        in_specs=[VMEM_SPEC] * n, out_specs=[ANY] * n + [VMEM_SPEC] * n,
        out_shape=[jax.ShapeDtypeStruct((4,) + s.shape[1:], BF16) for s in shards]
        + [jax.ShapeDtypeStruct(s.shape[1:], BF16) for s in shards],
        scratch_shapes=[pltpu.VMEM(s.shape[1:], BF16) for s in shards] + _dma_sems(3 * n, 3 * n, 3 * n, 3 * n, n),
        compiler_params=pltpu.CompilerParams(vmem_limit_bytes=VMEM_LIMIT),
    )(*shards)


def _run_comm(name, comm):
    nci, nco = len(comm.ins), len(comm.out_shapes)

    def body(*refs):
        copies = comm.make(refs[:nci], refs[nci:nci + nco], refs[nci + nco:])
        for cp in copies:
            cp.start()
        for cp in copies:
            cp.wait()

    return pl.pallas_call(body, name=name, in_specs=[ANY] * nci, out_specs=[ANY] * nco,
                          out_shape=list(comm.out_shapes), scratch_shapes=list(comm.sems))(*comm.ins)


def _gather_comm(mine):
    n = len(mine)

    def make(ins, outs, sems):
        send_sems, recv_sems, local_sems = sems
        x, y, c = _me()
        k = 2 * x + y
        copies = []
        for t in range(n):
            copies.append(pltpu.make_async_copy(ins[t], outs[t].at[k], local_sems.at[t]))
            for j, (px, py) in enumerate(_other_chips(x, y)):
                copies.append(pltpu.make_async_remote_copy(
                    src_ref=ins[t], dst_ref=outs[t].at[k], send_sem=send_sems.at[3 * t + j],
                    recv_sem=recv_sems.at[3 * t + j], device_id=(px, py, c), device_id_type=MESH))
        return copies

    return _Comm(mine, [jax.ShapeDtypeStruct((4,) + a.shape, a.dtype) for a in mine], _dma_sems(3 * n, 3 * n, n), make)


def _swap_comm(grads):
    n = len(grads)

    def make(ins, outs, sems):
        send_sems, recv_sems = sems
        x, y, c = _me()
        return [pltpu.make_async_remote_copy(
            src_ref=ins[t].at[:, _half_rows(grads[t].shape[1], 1 - c, 8)], dst_ref=outs[t],
            send_sem=send_sems.at[t], recv_sem=recv_sems.at[t], device_id=(x, y, 1 - c), device_id_type=MESH)
            for t in range(n)]

    shapes = [jax.ShapeDtypeStruct((g.shape[0], g.shape[1] // 2, g.shape[2]), F32) for g in grads]
    return _Comm(grads, shapes, _dma_sems(n, n), make)


def _scatter_comm(parts):
    n = len(parts)

    def make(ins, outs, sems):
        send_sems, recv_sems, local_sems = sems
        x, y, c = _me()
        k = 2 * x + y
        copies = []
        for t in range(n):
            copies.append(pltpu.make_async_copy(ins[t].at[k], outs[t].at[k], local_sems.at[t]))
            for j, (px, py) in enumerate(_other_chips(x, y)):
                copies.append(pltpu.make_async_remote_copy(
                    src_ref=ins[t].at[2 * px + py], dst_ref=outs[t].at[k], send_sem=send_sems.at[3 * t + j],
                    recv_sem=recv_sems.at[3 * t + j], device_id=(px, py, c), device_id_type=MESH))
        return copies

    return _Comm(parts, [jax.ShapeDtypeStruct(p.shape, F32) for p in parts], _dma_sems(3 * n, 3 * n, n), make)


def _share_comm(totals):
    n = len(totals)

    def make(ins, outs, sems):
        send_sems, recv_sems, local_sems = sems
        x, y, c = _me()
        copies = []
        for t in range(n):
            copies.append(pltpu.make_async_copy(ins[t], outs[t].at[c], local_sems.at[t]))
            copies.append(pltpu.make_async_remote_copy(
                src_ref=ins[t], dst_ref=outs[t].at[c], send_sem=send_sems.at[t], recv_sem=recv_sems.at[t],
                device_id=(x, y, 1 - c), device_id_type=MESH))
        return copies

    return _Comm(totals, [jax.ShapeDtypeStruct((2,) + t.shape, F32) for t in totals], _dma_sems(n, n, n), make)


def _small_allreduce_adamw(part, w, m, v):
    shape = part.shape

    def body(part_ref, w_ref, m_ref, v_ref, g_ref, d_ref, nm_ref, nv_ref, slots, send_sems, recv_sems):
        x, y, c = _me()
        me = 4 * x + 2 * y + c
        slots[me] = part_ref[...]
        copies = []
        for d in range(1, 8):
            peer = (x ^ (d >> 2), y ^ ((d >> 1) & 1), c ^ (d & 1))
            cp = pltpu.make_async_remote_copy(
                src_ref=part_ref, dst_ref=slots.at[me], send_sem=send_sems.at[d - 1], recv_sem=recv_sems.at[d - 1],
                device_id=peer, device_id_type=MESH)
            cp.start()
            copies.append(cp)
        for cp in copies:
            cp.wait()
        g = slots[0]
        for i in range(1, 8):
            g = g + slots[i]
        g_ref[...] = g
        d, nm, nv = _adamw_math(w_ref[...], g, m_ref[...], v_ref[...])
        d_ref[...] = d
        nm_ref[...] = nm
        nv_ref[...] = nv

    shp = jax.ShapeDtypeStruct(shape, F32)
    return pl.pallas_call(
        body, name="small_allreduce_adamw", in_specs=[VMEM_SPEC] * 4, out_specs=[VMEM_SPEC] * 4,
        out_shape=[shp, shp, shp, shp],
        scratch_shapes=[pltpu.VMEM((8,) + shape, F32), pltpu.SemaphoreType.DMA((7,)), pltpu.SemaphoreType.DMA((7,))],
    )(part, w, m, v)


TM = 256
T_FOX = 1024
T_DIL = 512
T_SCAN = 512
TN_DW = 1408


def _layer_fwd(l, h, p_i, wts, consts, comm=None):
    w_all, _, w_out, _, w_gate, _, w_ple, norm_g, b_f, qkg, ple_g = wts
    bsum, _, bias_t, rope = consts
    u, z, qa, ka, va, qb, kb, vb, logf, va_t, vb_t = _inproj(l, h, norm_g, w_all, b_f, qkg, bsum, rope, TM)
    c_spread, c_t = _cumsum_gates(l, logf, T_SCAN)
    oa, lse_a, *landed = _attn_fwd(f"fox_fwd_l{l}", True, qa, ka, va_t, (c_spread, c_t), T_FOX, comm)
    ob, lse_b = _attn_fwd(f"dil_fwd_l{l}", False, qb, kb, vb_t, (bias_t,), T_DIL)
    h2, e, gate, out = _outproj(l, h, oa, ob, z, p_i, w_out, ple_g, w_gate, w_ple, TM)
    saved = (h, u, z, qa, ka, va, qb, kb, vb, c_spread, c_t, oa, lse_a, ob, lse_b, h2, e, gate)
    return out, saved, landed


def _reduce_names(tag):
    return [f"reduce_{tag}_{w}" for w in ("w_in", "w_out", "w_ple", "w_gate")]


def _layer_bwd(l, dout, p_i, wts, consts, saved, pending=None, c=None):
    _, w_all_t, _, w_out_t, _, w_gate_t, _, norm_g, b_f, qkg, ple_g = wts
    bsum, hsel, bias_t, rope = consts
    h, u, z, qa, ka, va, qb, kb, vb, c_spread, c_t, oa, lse_a, ob, lse_b, h2, e, gate = saved
    fused = pending is not None
    dh2, doa, dob, dga, dgb, delta_t, dw_out, dw_gate, dw_ple, dple_g, *sib = _outproj_bwd(
        l, dout, h2, e, gate, p_i, oa, ob, z, w_out_t, ple_g, w_gate_t, hsel, TM,
        _swap_comm(pending) if fused else None)
    if fused:
        pair = [_pair_sum(n, g, x, c) for n, g, x in zip(_reduce_names(f"pair_l{l + 1}"), pending, sib)]
    dqt_a, dk_a, dv_a, dc, drow, *landed = _attn_bwd(
        f"fox_bwd_l{l}", True, qa, ka, va, doa, lse_a, delta_t, 0, (c_spread, c_t), T_FOX,
        _scatter_comm(pair) if fused else None)
    if fused:
        totals = [_sum_slots(n, y) for n, y in zip(_reduce_names(f"chips_l{l + 1}"), landed)]
    dqt_b, dk_b, dv_b, *reduced = _attn_bwd(f"dil_bwd_l{l}", False, qb, kb, vb, dob, lse_b, delta_t, N_PAIRS,
                                            (bias_t,), T_DIL, _share_comm(totals) if fused else None)
    dlogf = _rev_cumsum_gates(l, dc, drow, T_SCAN)
    dz, dqkg, dbf = _inproj_bwd_prep(l, z, dqt_a, dk_a, dv_a, dqt_b, dk_b, dv_b, dga, dgb, dlogf, b_f, qkg,
                                     bsum, rope, TM)
    dh, dnorm_g = _inproj_bwd_dx(l, dz, w_all_t, h, norm_g, dh2, TM)
    dw_all = _inproj_bwd_dw(l, u, dz, TM * 2, TN_DW)
    return dh, (dw_all, dw_out, dw_ple, dw_gate, dnorm_g[0], dbf[0, :N_HEADS], dqkg[:4], dple_g[0]), reduced


def _reduce_last(grads, c, l):
    sib = _run_comm(f"reduce_swap_l{l}", _swap_comm(grads))
    pair = [_pair_sum(n, g, x, c) for n, g, x in zip(_reduce_names(f"pair_l{l}"), grads, sib)]
    landed = _run_comm(f"reduce_scatter_l{l}", _scatter_comm(pair))
    totals = [_sum_slots(n, y) for n, y in zip(_reduce_names(f"chips_l{l}"), landed)]
    return _run_comm(f"reduce_share_l{l}", _share_comm(totals))


N_FA = 2048


def _layer_weights(l, gathered, norm_g, b_f, qk_norm_g, ple_norm_g):
    g_in, g_out, g_ple, g_gate = gathered
    w_in = jnp.transpose(g_in, (1, 0, 2)).reshape(D_MODEL, N_IN)
    w_all = jnp.concatenate([w_in[:, :N_FA], w_in[:, N_FA + N_HEADS:],
                             jnp.pad(w_in[:, N_FA:N_FA + N_HEADS], ((0, 0), (0, LANE - N_HEADS)))], axis=1)
    w_out = g_out.reshape(D_MODEL, D_MODEL)
    w_gate = g_gate.reshape(D_MODEL, D_MODEL)
    w_ple = jnp.transpose(g_ple, (1, 0, 2)).reshape(PLE_DIM, D_MODEL)
    qkg = jnp.pad(jnp.tile(qk_norm_g[l], (1, N_HEADS)), ((0, 4), (0, 0)))
    bf = jnp.pad(b_f[l], (0, LANE - N_HEADS))[None, :]
    return (w_all, w_all.T, w_out, w_out.T, w_gate, w_gate.T, w_ple, norm_g[l][None, :], bf, qkg,
            ple_norm_g[l][None, :])


def _slot_layout(dw_all, dw_out, dw_ple, dw_gate):
    dw_in = jnp.concatenate([dw_all[:, :N_FA], dw_all[:, N_MAIN:N_MAIN + N_HEADS], dw_all[:, N_FA:N_MAIN]], axis=1)
    return (jnp.transpose(dw_in.reshape(D_MODEL, 4, N_IN // 4), (1, 0, 2)),
            dw_out.reshape(4, D_MODEL // 4, D_MODEL),
            jnp.transpose(dw_ple.reshape(PLE_DIM, 4, D_MODEL // 4), (1, 0, 2)),
            dw_gate.reshape(4, D_MODEL // 4, D_MODEL))


SMALL_ROWS = 40


def _pack_small(norm_g, ple_norm_g, qk_norm_g, b_f):
    flat = jnp.concatenate([norm_g.reshape(-1), ple_norm_g.reshape(-1), qk_norm_g.reshape(-1), b_f.reshape(-1)])
    return jnp.pad(flat, (0, SMALL_ROWS * LANE - flat.shape[0])).reshape(SMALL_ROWS, LANE)


def _unpack_small(packed):
    flat = packed.reshape(-1)
    n1, n2, n3 = 2 * D_MODEL, 4 * D_MODEL, 4 * D_MODEL + 2 * 4 * HEAD_DIM
    return (flat[:n1].reshape(2, D_MODEL), flat[n1:n2].reshape(2, D_MODEL), flat[n2:n3].reshape(2, 4, HEAD_DIM),
            flat[n3:n3 + 2 * N_HEADS].reshape(2, N_HEADS))


def kernel(x, p, positions, norm_g, w_in, b_f, qk_norm_g, w_out, w_ple, ple_norm_g, w_ple_gate, loss_target,
           m_norm_g, m_w_in, m_b_f, m_qk_norm_g, m_w_out, m_w_ple, m_ple_norm_g, m_w_ple_gate,
           v_norm_g, v_w_in, v_b_f, v_qk_norm_g, v_w_out, v_w_ple, v_ple_norm_g, v_w_ple_gate):
    assert w_in.shape[0] == 2, "the schedule below is written for two layers"
    *first, = _gather_first_layer([w_in, w_out, w_ple, w_ple_gate])
    consts = (_head_block_diag(), _head_select(), _dil_bias(T_DIL), _rope_tables(positions[0]))
    small_w = (norm_g, b_f, qk_norm_g, ple_norm_g)
    wts0 = _layer_weights(0, first[:4], *small_w)
    h1, saved0, second = _layer_fwd(0, x[0], p[0, 0], wts0, consts, _gather_comm(first[4:]))
    wts1 = _layer_weights(1, second, *small_w)
    h2, saved1, _ = _layer_fwd(1, h1, p[1, 0], wts1, consts)
    sq, dh = _loss_head(h2, loss_target[0], TM)
    loss = lax.psum(0.5 / D_MODEL * jnp.sum(sq), ("x", "y", "c"))

    c = lax.axis_index("c").astype(jnp.int32).reshape(1)
    dh, grads1, _ = _layer_bwd(1, dh, p[1, 0], wts1, consts, saved1)
    dh, grads0, reduced1 = _layer_bwd(0, dh, p[0, 0], wts0, consts, saved0, _slot_layout(*grads1[:4]), c)
    reduced0 = _reduce_last(_slot_layout(*grads0[:4]), c, 0)
    grad_x = dh[None]
    small = [grads0[4:], grads1[4:]]
    n_layers = 2

    outs = {}
    for t, (name, w, m, v) in enumerate((("w_in", w_in, m_w_in, v_w_in), ("w_out", w_out, m_w_out, v_w_out),
                                         ("w_ple", w_ple, m_w_ple, v_w_ple),
                                         ("w_ple_gate", w_ple_gate, m_w_ple_gate, v_w_ple_gate))):
        outs[name] = tuple(_adamw(f"adamw_{name}", w, (reduced0[t], reduced1[t]), m, v))

    part = _pack_small(jnp.stack([s[0] for s in small]), jnp.stack([s[3] for s in small]),
                       jnp.stack([s[2] for s in small]).reshape(n_layers, 4, N_HEADS, HEAD_DIM).sum(axis=2),
                       jnp.stack([s[1] for s in small]))
    packed = _small_allreduce_adamw(part, _pack_small(norm_g, ple_norm_g, qk_norm_g, b_f),
                                    _pack_small(m_norm_g, m_ple_norm_g, m_qk_norm_g, m_b_f),
                                    _pack_small(v_norm_g, v_ple_norm_g, v_qk_norm_g, v_b_f))
    sm = [_unpack_small(a) for a in packed]
    for i, name in enumerate(("norm_g", "ple_norm_g", "qk_norm_g", "b_f")):
        outs[name] = tuple(sm[j][i] for j in range(4))

    order = ("norm_g", "w_in", "b_f", "qk_norm_g", "w_out", "w_ple", "ple_norm_g", "w_ple_gate")
    return (loss, grad_x) + tuple(outs[n][j] for j in range(4) for n in order)
```

```python
import functools
from typing import Any, Callable, NamedTuple, Sequence

import numpy as np
import jax
import jax.numpy as jnp
from jax import lax
from jax.experimental import pallas as pl
from jax.experimental.pallas import tpu as pltpu

F32 = jnp.float32
BF16 = jnp.bfloat16
MESH = pl.DeviceIdType.MESH

D_MODEL = 1024
HEAD_DIM = 64
D_BRANCH = 512
N_HEADS = 8
N_PAIRS = 4
N_IN = 4104
N_MAIN = 4096
N_ALL = 4224
PLE_DIM = 256
ROPE_THETA = 500000.0
ROPE_HALF = 8
EPS = 1e-6
NEG = -1e30
M_INIT = -1e29
Q_SCALE = HEAD_DIM ** -0.5
LOG2E = 1.4426950408889634
LN2 = 0.6931471805599453
DIL_PATTERNS = ((128, 1), (512, 4), (2048, 16))
DIL_BACK = 2048
ADAM_LR, ADAM_B1, ADAM_B2, ADAM_EPS, ADAM_WD, ADAM_STEP = 0.001, 0.9, 0.999, 1e-08, 0.01, 10
VMEM_LIMIT = 56 * 1024 * 1024
LANE = 128


def _dot(a, b):
    return jnp.dot(a, b, preferred_element_type=F32)


def _dot_nt(a, b):
    return lax.dot_general(a, b, (((1,), (1,)), ((), ())), preferred_element_type=F32)


def _dot_tn(a, b):
    return lax.dot_general(a, b, (((0,), (0,)), ((), ())), preferred_element_type=F32)


def _split_dot(x, w):
    hi = x.astype(BF16)
    lo = (x - hi.astype(F32)).astype(BF16)
    return _dot(hi, w) + _dot(lo, w)


def _split3_dot(w, x):
    hi = x.astype(BF16)
    r1 = x - hi.astype(F32)
    mid = r1.astype(BF16)
    lo = (r1 - mid.astype(F32)).astype(BF16)
    return _dot(w, hi) + _dot(w, mid) + _dot(w, lo)


def _sigmoid(x):
    return 1.0 / (1.0 + jnp.exp(-x))


def _params(n_grid):
    return pltpu.CompilerParams(dimension_semantics=("arbitrary",) * n_grid,
                                vmem_limit_bytes=VMEM_LIMIT)


def _full(shape):
    nd = len(shape)
    return pl.BlockSpec(shape, lambda *_: (0,) * nd)


ANY = pl.BlockSpec(memory_space=pl.ANY)
VMEM_SPEC = pl.BlockSpec(memory_space=pltpu.VMEM)


class _Comm(NamedTuple):
    ins: Sequence[Any]
    out_shapes: Sequence[Any]
    sems: Sequence[Any]
    make: Callable[..., Any]


def _fuse_comm(body, n_in, n_out, comm, grid):
    if comm is None:
        return body
    nci, nco, ncs = len(comm.ins), len(comm.out_shapes), len(comm.sems)

    def fused(*refs):
        a, b = n_in + nci, n_in + nci + n_out
        ins, cins, outs, couts = refs[:n_in], refs[n_in:a], refs[a:b], refs[b:b + nco]
        scratch, sems = refs[b + nco:len(refs) - ncs], refs[len(refs) - ncs:]
        first = functools.reduce(jnp.logical_and, [pl.program_id(d) == 0 for d in range(len(grid))])
        last = functools.reduce(jnp.logical_and, [pl.program_id(d) == n - 1 for d, n in enumerate(grid)])

        @pl.when(first)
        def _():
            for cp in comm.make(cins, couts, sems):
                cp.start()

        body(*ins, *outs, *scratch)

        @pl.when(last)
        def _():
            for cp in comm.make(cins, couts, sems):
                cp.wait()

    return fused


def _comm_args(comm):
    if comm is None:
        return [], [], [], [], []
    return (list(comm.ins), [ANY] * len(comm.ins), [ANY] * len(comm.out_shapes), list(comm.out_shapes),
            list(comm.sems))


def _head_block_diag():
    i = np.arange(D_BRANCH)
    return jnp.asarray((i[:, None] // HEAD_DIM == i[None, :] // HEAD_DIM).astype(np.float32), BF16)


def _head_select():
    i = np.arange(2 * D_BRANCH)
    j = np.arange(LANE)
    return jnp.asarray((i[:, None] // HEAD_DIM == j[None, :]).astype(np.float32), BF16)


def _dil_bias(t):
    nb = DIL_BACK // t + 1
    qi = np.arange(t)[:, None]
    ki = np.arange(t)[None, :]
    tiles = []
    for r in range(nb):
        d = r * t + qi - ki
        mult = np.zeros((t, t), np.int64)
        for window, dil in DIL_PATTERNS:
            mult += ((d >= 0) & (d <= window) & (d % dil == 0)).astype(np.int64)
        b = np.where(mult > 0, np.log2(np.maximum(mult, 1)), NEG).astype(np.float32)
        tiles.append(b.T)
    return jnp.asarray(np.stack(tiles))


def _rope_tables(positions):
    inv_freq = ROPE_THETA ** (-jnp.arange(ROPE_HALF, dtype=F32) / ROPE_HALF)
    ang = positions.astype(F32)[:, None] * inv_freq
    cos, sin = jnp.cos(ang), jnp.sin(ang)
    s = positions.shape[0]
    rest = HEAD_DIM - 2 * ROPE_HALF
    one, zero, zero8 = jnp.ones((s, rest), F32), jnp.zeros((s, rest), F32), jnp.zeros((s, ROPE_HALF), F32)
    c = jnp.concatenate([cos, cos, one], axis=1)
    s1 = jnp.concatenate([zero8, sin, zero], axis=1)
    s2 = jnp.concatenate([-sin, zero8, zero], axis=1)
    return tuple(jnp.tile(t, (1, 2)) for t in (c, s1, s2))


def _rope_fwd(x, c, s1, s2):
    return x * c + pltpu.roll(x, ROPE_HALF, 1) * s1 + pltpu.roll(x, LANE - ROPE_HALF, 1) * s2


def _rope_bwd(dy, c, s1, s2):
    return dy * c + pltpu.roll(dy * s1, LANE - ROPE_HALF, 1) + pltpu.roll(dy * s2, ROPE_HALF, 1)


def _log_sigmoid(x):
    return jnp.minimum(x, 0.0) - jnp.log(1.0 + jnp.exp(-jnp.abs(x)))


def _inproj(l, h, norm_g, w_all, b_f, qkg, bsum, rope, tm):
    s = h.shape[0]
    rc, rs1, rs2 = rope

    def body(h_ref, g_ref, w_ref, bf_ref, qkg_ref, bsum_ref, rc_ref, rs1_ref, rs2_ref,
             u_ref, z_ref, qa_ref, ka_ref, va_ref, qb_ref, kb_ref, vb_ref, lf_ref, vat_ref, vbt_ref):
        hh = h_ref[...]
        r = lax.rsqrt(jnp.mean(hh * hh, axis=-1, keepdims=True) + EPS)
        u = (hh * r * g_ref[...]).astype(BF16)
        u_ref[...] = u
        for k in range(N_ALL // LANE // 3):
            cols = slice(3 * LANE * k, 3 * LANE * (k + 1))
            z_ref[:, cols] = _dot(u, w_ref[:, cols])
        bs = bsum_ref[...]

        def head_norm(x, row):
            ms = _split_dot(x * x, bs) * (1.0 / HEAD_DIM)
            return x * lax.rsqrt(ms + EPS) * qkg_ref[row:row + 1, :]

        def seg(k):
            return z_ref[:, D_BRANCH * k:D_BRANCH * (k + 1)]

        qa_ref[...] = (head_norm(seg(0), 0) * (Q_SCALE * LOG2E)).astype(BF16)
        ka_ref[...] = head_norm(seg(1), 1).astype(BF16)
        va_ref[...] = seg(2).astype(BF16)
        vat_ref[...] = seg(2).T.astype(BF16)
        qn = head_norm(seg(4), 2) * (Q_SCALE * LOG2E)
        kn = head_norm(seg(5), 3)
        c, s1, s2 = rc_ref[...], rs1_ref[...], rs2_ref[...]
        for k in range(D_BRANCH // LANE):
            cols = slice(LANE * k, LANE * (k + 1))
            qb_ref[:, cols] = _rope_fwd(qn[:, cols], c, s1, s2).astype(BF16)
            kb_ref[:, cols] = _rope_fwd(kn[:, cols], c, s1, s2).astype(BF16)
        vb_ref[...] = seg(6).astype(BF16)
        vbt_ref[...] = seg(6).T.astype(BF16)
        lf_ref[...] = _log_sigmoid(z_ref[:, N_MAIN:N_ALL] + bf_ref[...])

    row = lambda w: pl.BlockSpec((tm, w), lambda i: (i, 0))
    colt = pl.BlockSpec((D_BRANCH, tm), lambda i: (0, i))
    bf = lambda: jax.ShapeDtypeStruct((s, D_BRANCH), BF16)
    bft = lambda: jax.ShapeDtypeStruct((D_BRANCH, s), BF16)
    return pl.pallas_call(
        body, name=f"inproj_l{l}", grid=(s // tm,),
        in_specs=[row(D_MODEL), _full((1, D_MODEL)), _full((D_MODEL, N_ALL)), _full((1, LANE)),
                  _full((8, D_BRANCH)), _full((D_BRANCH, D_BRANCH)), row(LANE), row(LANE), row(LANE)],
        out_specs=[row(D_MODEL), row(N_ALL)] + [row(D_BRANCH)] * 6 + [row(LANE), colt, colt],
        out_shape=[jax.ShapeDtypeStruct((s, D_MODEL), BF16), jax.ShapeDtypeStruct((s, N_ALL), F32),
                   bf(), bf(), bf(), bf(), bf(), bf(), jax.ShapeDtypeStruct((s, LANE), F32), bft(), bft()],
        compiler_params=_params(1),
    )(h, norm_g, w_all, b_f, qkg, bsum, rc, rs1, rs2)


def _tri(t, upper):
    a = lax.broadcasted_iota(jnp.int32, (t, t), 0)
    b = lax.broadcasted_iota(jnp.int32, (t, t), 1)
    return jnp.where((b >= a) if upper else (b <= a), 1.0, 0.0).astype(BF16)


def _cumsum_gates(l, logf, t):
    s = logf.shape[0]

    def body(lf_ref, cs_ref, ct_ref, carry):
        @pl.when(pl.program_id(0) == 0)
        def _():
            carry[...] = jnp.zeros_like(carry)

        x = lf_ref[...]
        c = _split3_dot(_tri(t, False), x) + carry[0:1, :]
        carry[...] = jnp.broadcast_to(c[t - 1:t, :], carry.shape)
        c = c * LOG2E
        ct = c.T
        for p in range(N_PAIRS):
            cs_ref[:, LANE * p:LANE * (p + 1)] = c if p == 0 else pltpu.roll(c, LANE - 2 * p, 1)
            ct_ref[p, :, :] = ct[2 * p:2 * p + 2, :]

    return pl.pallas_call(
        body, name=f"cumsum_l{l}", grid=(s // t,),
        in_specs=[pl.BlockSpec((t, LANE), lambda i: (i, 0))],
        out_specs=[pl.BlockSpec((t, N_PAIRS * LANE), lambda i: (i, 0)),
                   pl.BlockSpec((N_PAIRS, 2, t), lambda i: (0, 0, i))],
        out_shape=[jax.ShapeDtypeStruct((s, N_PAIRS * LANE), F32), jax.ShapeDtypeStruct((N_PAIRS, 2, s), F32)],
        scratch_shapes=[pltpu.VMEM((8, LANE), F32)],
        compiler_params=_params(1),
    )(logf)


def _rev_cumsum_gates(l, dc_spread, drow, t):
    s = dc_spread.shape[0]
    n = s // t

    def body(dc_ref, drow_ref, out_ref, carry):
        @pl.when(pl.program_id(0) == 0)
        def _():
            carry[...] = jnp.zeros_like(carry)

        lane = lax.broadcasted_iota(jnp.int32, (t, LANE), 1)
        rows = jnp.concatenate([drow_ref[p] for p in range(N_PAIRS)] + [jnp.zeros((LANE - N_HEADS, t), F32)], axis=0)
        x = rows.T
        for p in range(N_PAIRS):
            xp = jnp.where(lane < 2, dc_ref[:, LANE * p:LANE * (p + 1)], 0.0)
            x = x + (xp if p == 0 else pltpu.roll(xp, 2 * p, 1))
        out = _split3_dot(_tri(t, True), x) + carry[0:1, :]
        out_ref[...] = out
        carry[...] = jnp.broadcast_to(out[0:1, :], carry.shape)

    return pl.pallas_call(
        body, name=f"revcumsum_l{l}", grid=(n,),
        in_specs=[pl.BlockSpec((t, N_PAIRS * LANE), lambda i: (n - 1 - i, 0)),
                  pl.BlockSpec((N_PAIRS, 2, t), lambda i: (0, 0, n - 1 - i))],
        out_specs=pl.BlockSpec((t, LANE), lambda i: (n - 1 - i, 0)),
        out_shape=jax.ShapeDtypeStruct((s, LANE), F32),
        scratch_shapes=[pltpu.VMEM((8, LANE), F32)],
        compiler_params=_params(1),
    )(dc_spread, drow)


def _attn_fwd(name, fox, q, k, vt, extra, t, comm=None):
    s = q.shape[0]
    nq = s // t
    nb = DIL_BACK // t + 1

    def body(*refs):
        if fox:
            q_ref, k_ref, vt_ref, ccol_ref, crow_ref, o_ref, lse_ref, m_scr, l_scr, acc_scr = refs
        else:
            q_ref, k_ref, vt_ref, bias_ref, o_ref, lse_ref, m_scr, l_scr, acc_scr = refs
        i = pl.program_id(1)
        lane = lax.broadcasted_iota(jnp.int32, (t, LANE), 1)
        first = lane < HEAD_DIM
        qq = q_ref[...]
        zero = jnp.zeros_like(qq)
        qh = (jnp.where(first, qq, zero), jnp.where(first, zero, qq))
        m_scr[...] = jnp.full(m_scr.shape, M_INIT, F32)
        l_scr[...] = jnp.zeros_like(l_scr)
        acc_scr[...] = jnp.zeros_like(acc_scr)
        ones = jnp.ones((8, t), BF16)

        def step(j, diag):
            rows = pl.ds(pl.multiple_of(j * t, t), t)
            ks = k_ref[rows, :]
            vts = vt_ref[:, rows]
            if fox:
                ccol = ccol_ref[rows, :]
            for h in range(2):
                st = _dot_nt(ks, qh[h])
                if fox:
                    st = st + (crow_ref[h:h + 1, :] - ccol[:, h:h + 1])
                    if diag:
                        ki = lax.broadcasted_iota(jnp.int32, (t, t), 0)
                        qi = lax.broadcasted_iota(jnp.int32, (t, t), 1)
                        st = jnp.where(ki <= qi, st, NEG)
                else:
                    st = st + bias_ref[i - j]
                m_old = m_scr[h]
                m_new = jnp.maximum(m_old, jnp.max(st, axis=0, keepdims=True))
                alpha = jnp.exp2(m_old - m_new)
                pb = jnp.exp2(st - m_new).astype(BF16)
                l_scr[h] = alpha * l_scr[h] + _dot(ones, pb)[0:1, :]
                acc_scr[h] = alpha * acc_scr[h] + _dot(vts, pb)
                m_scr[h] = m_new

        if fox:
            lax.fori_loop(0, i, lambda j, c: (step(j, False), c)[1], 0)
            step(i, True)
        else:
            lax.fori_loop(jnp.maximum(i - (nb - 1), 0), i + 1, lambda j, c: (step(j, False), c)[1], 0)

        sub = lax.broadcasted_iota(jnp.int32, (LANE, t), 0)
        ot = jnp.where(sub < HEAD_DIM, acc_scr[0] / l_scr[0], acc_scr[1] / l_scr[1])
        o_ref[...] = ot.T
        for h in range(2):
            lse_ref[h:h + 1, :] = m_scr[h] + jnp.log2(l_scr[h])

    qspec = pl.BlockSpec((t, LANE), lambda hp, i: (i, hp))
    kspec = pl.BlockSpec((s, LANE), lambda hp, i: (0, hp))
    vtspec = pl.BlockSpec((LANE, s), lambda hp, i: (hp, 0))
    in_specs = [qspec, kspec, vtspec]
    if fox:
        in_specs += [kspec, pl.BlockSpec((None, 2, t), lambda hp, i: (hp, 0, i))]
    else:
        in_specs += [_full((nb, t, t))]
    grid = (N_PAIRS, nq)
    c_in, c_ispec, c_ospec, c_oshape, c_scr = _comm_args(comm)
    return pl.pallas_call(
        _fuse_comm(body, len(in_specs), 2, comm, grid), name=name, grid=grid,
        in_specs=in_specs + c_ispec,
        out_specs=[qspec, pl.BlockSpec((None, 2, t), lambda hp, i: (hp, 0, i))] + c_ospec,
        out_shape=[jax.ShapeDtypeStruct((s, D_BRANCH), F32), jax.ShapeDtypeStruct((N_PAIRS, 2, s), F32)] + c_oshape,
        scratch_shapes=[pltpu.VMEM((2, 1, t), F32), pltpu.VMEM((2, 1, t), F32), pltpu.VMEM((2, LANE, t), F32)]
        + c_scr,
        compiler_params=_params(2),
    )(q, k, vt, *extra, *c_in)


def _attn_bwd(name, fox, q, k, v, do, lse_t, delta_t, pair_offset, extra, t, comm=None):
    s = q.shape[0]
    nk = s // t
    nb = DIL_BACK // t + 1

    def body(*refs):
        if fox:
            (q_ref, k_ref, v_ref, do_ref, lse_ref, delta_ref, ccol_ref, crow_ref,
             dqt_ref, dk_ref, dv_ref, dc_ref, drow_ref) = refs
        else:
            q_ref, k_ref, v_ref, do_ref, lse_ref, delta_ref, bias_ref, dqt_ref, dk_ref, dv_ref = refs
        j = pl.program_id(1)

        @pl.when(j == 0)
        def _():
            dqt_ref[...] = jnp.zeros_like(dqt_ref)
            if fox:
                drow_ref[...] = jnp.zeros_like(drow_ref)

        lane = lax.broadcasted_iota(jnp.int32, (t, LANE), 1)
        first = lane < HEAD_DIM
        ks = k_ref[...]
        vs = v_ref[...]
        kt = ks.astype(F32).T
        sub = lax.broadcasted_iota(jnp.int32, (LANE, t), 0)
        kth = (jnp.where(sub < HEAD_DIM, kt, 0.0).astype(BF16), jnp.where(sub < HEAD_DIM, 0.0, kt).astype(BF16))
        dk_ref[...] = jnp.zeros_like(dk_ref)
        dv_ref[...] = jnp.zeros_like(dv_ref)
        if fox:
            dc_ref[...] = jnp.zeros_like(dc_ref)
            ccol = ccol_ref[...]

        def step(i, diag):
            rows = pl.ds(pl.multiple_of(i * t, t), t)
            qq = q_ref[rows, :]
            dd = do_ref[rows, :]
            zero = jnp.zeros_like(qq)
            qh = (jnp.where(first, qq, zero), jnp.where(first, zero, qq))
            dh = (jnp.where(first, dd, zero), jnp.where(first, zero, dd))
            for h in range(2):
                st = _dot_nt(ks, qh[h])
                if fox:
                    st = st + (crow_ref[h:h + 1, rows] - ccol[:, h:h + 1])
                    if diag:
                        ki = lax.broadcasted_iota(jnp.int32, (t, t), 0)
                        qi = lax.broadcasted_iota(jnp.int32, (t, t), 1)
                        st = jnp.where(ki <= qi, st, NEG)
                else:
                    st = st + bias_ref[i - j]
                pt = jnp.exp2(st - lse_ref[h:h + 1, rows])
                dpt = _dot_nt(vs, dh[h])
                dst = pt * (dpt - delta_ref[h:h + 1, rows])
                dv_ref[...] += _dot(pt.astype(BF16), dh[h])
                dsb = dst.astype(BF16)
                dk_ref[...] += _dot(dsb, qh[h])
                dqt_ref[:, rows] += _dot(kth[h], dsb)
                if fox:
                    dc_ref[...] -= jnp.where(lane == h, jnp.sum(dst, axis=1, keepdims=True), 0.0)
                    drow_ref[h:h + 1, rows] += jnp.sum(dst, axis=0, keepdims=True)

        if fox:
            step(j, True)
            lax.fori_loop(j + 1, nk, lambda i, c: (step(i, False), c)[1], 0)
        else:
            lax.fori_loop(j, jnp.minimum(j + nb, nk), lambda i, c: (step(i, False), c)[1], 0)

    kspec = pl.BlockSpec((t, LANE), lambda hp, j: (j, hp))
    qspec = pl.BlockSpec((s, LANE), lambda hp, j: (0, hp))
    rowspec = pl.BlockSpec((None, 2, s), lambda hp, j: (hp, 0, 0))
    drowspec = pl.BlockSpec((None, 2, s), lambda hp, j: (hp + pair_offset, 0, 0))
    in_specs = [qspec, kspec, kspec, qspec, rowspec, drowspec]
    out_specs = [pl.BlockSpec((LANE, s), lambda hp, j: (hp, 0)), kspec, kspec]
    out_shape = [jax.ShapeDtypeStruct((D_BRANCH, s), F32), jax.ShapeDtypeStruct((s, D_BRANCH), F32),
                 jax.ShapeDtypeStruct((s, D_BRANCH), F32)]
    if fox:
        in_specs += [kspec, rowspec]
        out_specs += [kspec, rowspec]
        out_shape += [jax.ShapeDtypeStruct((s, N_PAIRS * LANE), F32), jax.ShapeDtypeStruct((N_PAIRS, 2, s), F32)]
    else:
        in_specs += [_full((nb, t, t))]
    grid = (N_PAIRS, nk)
    c_in, c_ispec, c_ospec, c_oshape, c_scr = _comm_args(comm)
    return pl.pallas_call(
        _fuse_comm(body, len(in_specs), len(out_specs), comm, grid), name=name, grid=grid,
        in_specs=in_specs + c_ispec, out_specs=out_specs + c_ospec, out_shape=out_shape + c_oshape,
        scratch_shapes=c_scr, compiler_params=_params(2),
    )(q, k, v, do, lse_t, delta_t, *extra, *c_in)


def _silu(x):
    return x * _sigmoid(x)


def _outproj(l, h, oa, ob, z, p_i, w_out, ple_g, w_gate, w_ple, tm):
    s = h.shape[0]

    def body(h_ref, oa_ref, ob_ref, ga_ref, gb_ref, p_ref, wo_ref, pg_ref, wg_ref, wp_ref,
             h2_ref, e_ref, gate_ref, out_ref):
        a = jnp.concatenate([oa_ref[...] * _silu(ga_ref[...]), ob_ref[...] * _silu(gb_ref[...])], axis=1)
        h2 = h_ref[...] + _dot(a.astype(BF16), wo_ref[...])
        h2_ref[...] = h2
        r = lax.rsqrt(jnp.mean(h2 * h2, axis=-1, keepdims=True) + EPS)
        n2 = (h2 * r * pg_ref[...]).astype(BF16)
        gate = _sigmoid(_dot(n2, wg_ref[...]))
        e = _dot(p_ref[...].astype(BF16), wp_ref[...])
        e_ref[...] = e
        gate_ref[...] = gate
        out_ref[...] = h2 + e * gate

    row = lambda w: pl.BlockSpec((tm, w), lambda i: (i, 0))
    zcol = lambda k: pl.BlockSpec((tm, D_BRANCH), lambda i: (i, k))
    f = lambda: jax.ShapeDtypeStruct((s, D_MODEL), F32)
    return pl.pallas_call(
        body, name=f"outproj_l{l}", grid=(s // tm,),
        in_specs=[row(D_MODEL), row(D_BRANCH), row(D_BRANCH), zcol(3), zcol(7), row(PLE_DIM),
                  _full((D_MODEL, D_MODEL)), _full((1, D_MODEL)), _full((D_MODEL, D_MODEL)),
                  _full((PLE_DIM, D_MODEL))],
        out_specs=[row(D_MODEL)] * 4, out_shape=[f(), f(), f(), f()],
        compiler_params=_params(1),
    )(h, oa, ob, z, z, p_i, w_out, ple_g, w_gate, w_ple)


def _loss_head(y, target, tm):
    s = y.shape[0]

    def body(y_ref, t_ref, acc_ref, dy_ref):
        @pl.when(pl.program_id(0) == 0)
        def _():
            acc_ref[...] = jnp.zeros_like(acc_ref)

        err = y_ref[...] - t_ref[...]
        dy_ref[...] = err * (1.0 / D_MODEL)
        e2 = err * err
        part = jnp.zeros((8, LANE), F32)
        for r in range(tm // 8):
            for c in range(D_MODEL // LANE):
                part = part + e2[8 * r:8 * (r + 1), LANE * c:LANE * (c + 1)]
        acc_ref[...] += part

    row = pl.BlockSpec((tm, D_MODEL), lambda i: (i, 0))
    return pl.pallas_call(
        body, name="loss_head", grid=(s // tm,), in_specs=[row, row],
        out_specs=[_full((8, LANE)), row],
        out_shape=[jax.ShapeDtypeStruct((8, LANE), F32), jax.ShapeDtypeStruct((s, D_MODEL), F32)],
        compiler_params=_params(1),
    )(y, target)


def _outproj_bwd(l, dout, h2, e, gate, p_i, oa, ob, z, w_out_t, ple_g, w_gate_t, hsel, tm, comm=None):
    s = dout.shape[0]

    def body(do_ref, h2_ref, e_ref, gate_ref, p_ref, oa_ref, ob_ref, ga_ref, gb_ref, wot_ref, pg_ref,
             wgt_ref, hsel_ref,
             dh2_ref, doa_ref, dob_ref, dga_ref, dgb_ref, delta_ref, dwo_ref, dwg_ref, dwp_ref, dpg_ref):
        @pl.when(pl.program_id(0) == 0)
        def _():
            dwo_ref[...] = jnp.zeros_like(dwo_ref)
            dwg_ref[...] = jnp.zeros_like(dwg_ref)
            dwp_ref[...] = jnp.zeros_like(dwp_ref)
            dpg_ref[...] = jnp.zeros_like(dpg_ref)

        dho = do_ref[...]
        g = gate_ref[...]
        de = (dho * g).astype(BF16)
        dwp_ref[...] += _dot_tn(p_ref[...].astype(BF16), de)
        dpre = (dho * e_ref[...] * g * (1.0 - g)).astype(BF16)
        h2 = h2_ref[...]
        pg = pg_ref[...]
        r = lax.rsqrt(jnp.mean(h2 * h2, axis=-1, keepdims=True) + EPS)
        n2 = (h2 * r * pg).astype(BF16)
        dwg_ref[...] += _dot_tn(n2, dpre)
        dn2 = _dot(dpre, wgt_ref[...])
        dpg_ref[0:1, :] += jnp.sum(dn2 * h2 * r, axis=0, keepdims=True)
        wv = dn2 * pg
        dh2 = dho + r * wv - h2 * (r * r * r) * jnp.mean(wv * h2, axis=-1, keepdims=True)
        dh2_ref[...] = dh2
        dh2b = dh2.astype(BF16)
        ga, gb, oa, ob = ga_ref[...], gb_ref[...], oa_ref[...], ob_ref[...]
        sga, sgb = _sigmoid(ga), _sigmoid(gb)
        a = jnp.concatenate([oa * ga * sga, ob * gb * sgb], axis=1).astype(BF16)
        dwo_ref[...] += _dot_tn(a, dh2b)
        da = _dot(dh2b, wot_ref[...])
        da_a, da_b = da[:, :D_BRANCH], da[:, D_BRANCH:]
        doa = da_a * ga * sga
        dob = da_b * gb * sgb
        doa_ref[...] = doa.astype(BF16)
        dob_ref[...] = dob.astype(BF16)
        dga_ref[...] = (da_a * oa * sga * (1.0 + ga * (1.0 - sga))).astype(BF16)
        dgb_ref[...] = (da_b * ob * sgb * (1.0 + gb * (1.0 - sgb))).astype(BF16)
        prod = jnp.concatenate([doa * oa, dob * ob], axis=1)
        dt = _split_dot(prod, hsel_ref[...]).T
        for pp in range(2 * N_PAIRS):
            delta_ref[pp, :, :] = dt[2 * pp:2 * pp + 2, :]

    row = lambda w: pl.BlockSpec((tm, w), lambda i: (i, 0))
    zcol = lambda k: pl.BlockSpec((tm, D_BRANCH), lambda i: (i, k))
    grid = (s // tm,)
    c_in, c_ispec, c_ospec, c_oshape, c_scr = _comm_args(comm)
    return pl.pallas_call(
        _fuse_comm(body, 13, 10, comm, grid), name=f"outproj_bwd_l{l}", grid=grid,
        in_specs=[row(D_MODEL)] * 4 + [row(PLE_DIM), row(D_BRANCH), row(D_BRANCH), zcol(3), zcol(7),
                                        _full((D_MODEL, D_MODEL)), _full((1, D_MODEL)), _full((D_MODEL, D_MODEL)),
                                        _full((2 * D_BRANCH, LANE))] + c_ispec,
        out_specs=[row(D_MODEL)] + [row(D_BRANCH)] * 4
        + [pl.BlockSpec((2 * N_PAIRS, 2, tm), lambda i: (0, 0, i)), _full((D_MODEL, D_MODEL)),
           _full((D_MODEL, D_MODEL)), _full((PLE_DIM, D_MODEL)), _full((8, D_MODEL))] + c_ospec,
        out_shape=[jax.ShapeDtypeStruct((s, D_MODEL), F32)] + [jax.ShapeDtypeStruct((s, D_BRANCH), BF16)] * 4
        + [jax.ShapeDtypeStruct((2 * N_PAIRS, 2, s), F32), jax.ShapeDtypeStruct((D_MODEL, D_MODEL), F32),
           jax.ShapeDtypeStruct((D_MODEL, D_MODEL), F32), jax.ShapeDtypeStruct((PLE_DIM, D_MODEL), F32),
           jax.ShapeDtypeStruct((8, D_MODEL), F32)] + c_oshape,
        scratch_shapes=c_scr, compiler_params=_params(1),
    )(dout, h2, e, gate, p_i, oa, ob, z, z, w_out_t, ple_g, w_gate_t, hsel, *c_in)


def _inproj_bwd_prep(l, z, dqt_a, dk_a, dv_a, dqt_b, dk_b, dv_b, dga, dgb, dlogf, b_f, qkg, bsum, rope, tm):
    s = z.shape[0]
    rc, rs1, rs2 = rope

    def body(z_ref, dqta_ref, dka_ref, dva_ref, dqtb_ref, dkb_ref, dvb_ref, dga_ref, dgb_ref, dlf_ref,
             bf_ref, qkg_ref, bsum_ref, rc_ref, rs1_ref, rs2_ref, dz_ref, dqkg_ref, dbf_ref):
        @pl.when(pl.program_id(0) == 0)
        def _():
            dqkg_ref[...] = jnp.zeros_like(dqkg_ref)
            dbf_ref[...] = jnp.zeros_like(dbf_ref)

        bs = bsum_ref[...]
        c, s1, s2 = rc_ref[...], rs1_ref[...], rs2_ref[...]

        def unrope(dy):
            return jnp.concatenate([_rope_bwd(dy[:, LANE * k:LANE * (k + 1)], c, s1, s2)
                                    for k in range(D_BRANCH // LANE)], axis=1)

        def norm_bwd(k, row, dy):
            x = z_ref[:, D_BRANCH * k:D_BRANCH * (k + 1)]
            r = lax.rsqrt(_split_dot(x * x, bs) * (1.0 / HEAD_DIM) + EPS)
            dqkg_ref[row:row + 1, :] += jnp.sum(dy * x * r, axis=0, keepdims=True)
            w = dy * qkg_ref[row:row + 1, :]
            dx = r * w - x * (r * r * r) * (_split_dot(w * x, bs) * (1.0 / HEAD_DIM))
            dz_ref[:, D_BRANCH * k:D_BRANCH * (k + 1)] = dx.astype(BF16)

        norm_bwd(0, 0, dqta_ref[...].T * Q_SCALE)
        norm_bwd(1, 1, dka_ref[...] * LN2)
        dz_ref[:, 2 * D_BRANCH:3 * D_BRANCH] = dva_ref[...].astype(BF16)
        dz_ref[:, 3 * D_BRANCH:4 * D_BRANCH] = dga_ref[...]
        norm_bwd(4, 2, unrope(dqtb_ref[...].T * Q_SCALE))
        norm_bwd(5, 3, unrope(dkb_ref[...] * LN2))
        dz_ref[:, 6 * D_BRANCH:7 * D_BRANCH] = dvb_ref[...].astype(BF16)
        dz_ref[:, 7 * D_BRANCH:8 * D_BRANCH] = dgb_ref[...]
        dfa = dlf_ref[...] * _sigmoid(-(z_ref[:, N_MAIN:N_ALL] + bf_ref[...]))
        dz_ref[:, N_MAIN:N_ALL] = dfa.astype(BF16)
        dbf_ref[0:1, :] += jnp.sum(dfa, axis=0, keepdims=True)

    row = lambda w: pl.BlockSpec((tm, w), lambda i: (i, 0))
    colt = pl.BlockSpec((D_BRANCH, tm), lambda i: (0, i))
    return pl.pallas_call(
        body, name=f"inproj_bwd_prep_l{l}", grid=(s // tm,),
        in_specs=[row(N_ALL), colt, row(D_BRANCH), row(D_BRANCH), colt, row(D_BRANCH), row(D_BRANCH),
                  row(D_BRANCH), row(D_BRANCH), row(LANE), _full((1, LANE)), _full((8, D_BRANCH)),
                  _full((D_BRANCH, D_BRANCH)), row(LANE), row(LANE), row(LANE)],
        out_specs=[row(N_ALL), _full((8, D_BRANCH)), _full((8, LANE))],
        out_shape=[jax.ShapeDtypeStruct((s, N_ALL), BF16), jax.ShapeDtypeStruct((8, D_BRANCH), F32),
                   jax.ShapeDtypeStruct((8, LANE), F32)],
        compiler_params=_params(1),
    )(z, dqt_a, dk_a, dv_a, dqt_b, dk_b, dv_b, dga, dgb, dlogf, b_f, qkg, bsum, rc, rs1, rs2)


def _inproj_bwd_dx(l, dz, w_all_t, h, norm_g, dh2, tm):
    s = dz.shape[0]

    def body(dz_ref, wt_ref, h_ref, g_ref, dh2_ref, dh_ref, dg_ref):
        @pl.when(pl.program_id(0) == 0)
        def _():
            dg_ref[...] = jnp.zeros_like(dg_ref)

        du = _dot(dz_ref[...], wt_ref[...])
        hh = h_ref[...]
        g = g_ref[...]
        r = lax.rsqrt(jnp.mean(hh * hh, axis=-1, keepdims=True) + EPS)
        dg_ref[0:1, :] += jnp.sum(du * hh * r, axis=0, keepdims=True)
        wv = du * g
        dh_ref[...] = dh2_ref[...] + r * wv - hh * (r * r * r) * jnp.mean(wv * hh, axis=-1, keepdims=True)

    row = lambda w: pl.BlockSpec((tm, w), lambda i: (i, 0))
    return pl.pallas_call(
        body, name=f"inproj_bwd_dx_l{l}", grid=(s // tm,),
        in_specs=[row(N_ALL), _full((N_ALL, D_MODEL)), row(D_MODEL), _full((1, D_MODEL)), row(D_MODEL)],
        out_specs=[row(D_MODEL), _full((8, D_MODEL))],
        out_shape=[jax.ShapeDtypeStruct((s, D_MODEL), F32), jax.ShapeDtypeStruct((8, D_MODEL), F32)],
        compiler_params=_params(1),
    )(dz, w_all_t, h, norm_g, dh2)


def _inproj_bwd_dw(l, u, dz, tm, tn):
    s = u.shape[0]

    def body(u_ref, dz_ref, dw_ref):
        @pl.when(pl.program_id(1) == 0)
        def _():
            dw_ref[...] = jnp.zeros_like(dw_ref)

        dw_ref[...] += _dot_tn(u_ref[...], dz_ref[...])

    return pl.pallas_call(
        body, name=f"inproj_bwd_dw_l{l}", grid=(N_ALL // tn, s // tm),
        in_specs=[pl.BlockSpec((tm, D_MODEL), lambda n, i: (i, 0)), pl.BlockSpec((tm, tn), lambda n, i: (i, n))],
        out_specs=pl.BlockSpec((D_MODEL, tn), lambda n, i: (0, n)),
        out_shape=jax.ShapeDtypeStruct((D_MODEL, N_ALL), F32),
        compiler_params=_params(2),
    )(u, dz)


def _adamw_math(w, g, m, v):
    m = ADAM_B1 * m + (1.0 - ADAM_B1) * g
    v = ADAM_B2 * v + (1.0 - ADAM_B2) * (g * g)
    m_hat = m / (1.0 - ADAM_B1 ** ADAM_STEP)
    v_hat = v / (1.0 - ADAM_B2 ** ADAM_STEP)
    delta = -ADAM_LR * (m_hat / (jnp.sqrt(v_hat) + ADAM_EPS) + ADAM_WD * w)
    return delta, m, v


def _adamw(name, w, halves, m, v, core):
    nl, r, c = w.shape
    hr = r // 2
    tr = 128 if hr % 128 == 0 else hr
    nb = hr // tr

    def body(core_ref, w_ref, own0_ref, oth0_ref, own1_ref, oth1_ref, m_ref, v_ref, g_ref, d_ref, nm_ref, nv_ref):
        first = pl.program_id(0) == 0
        own = jnp.where(first, own0_ref[...], own1_ref[...])
        oth = jnp.where(first, oth0_ref[...], oth1_ref[...])
        g = jnp.where(pl.program_id(1) // nb == core_ref[0], own, oth)
        d, nm, nv = _adamw_math(w_ref[...], g, m_ref[...], v_ref[...])
        g_ref[...] = g
        d_ref[...] = d
        nm_ref[...] = nm
        nv_ref[...] = nv

    spec = pl.BlockSpec((None, tr, c), lambda a, b, core_ref: (a, b, 0))
    gspec = pl.BlockSpec((tr, c), lambda a, b, core_ref: (b % nb, 0))
    shp = jax.ShapeDtypeStruct(w.shape, F32)
    return pl.pallas_call(
        body, name=name,
        grid_spec=pltpu.PrefetchScalarGridSpec(
            num_scalar_prefetch=1, grid=(nl, r // tr), in_specs=[spec] + [gspec] * 4 + [spec, spec],
            out_specs=[spec] * 4),
        out_shape=[shp, shp, shp, shp], compiler_params=_params(2),
    )(core, w, halves[0][0], halves[0][1], halves[1][0], halves[1][1], m, v)


def _pair_sum(name, g, x, c):
    n, r, cc = g.shape
    hr = r // 2
    tr = 128 if hr % 128 == 0 else hr
    nb = hr // tr

    def body(c_ref, g_ref, x_ref, o_ref):
        o_ref[...] = g_ref[...] + x_ref[...]

    spec = pl.BlockSpec((None, tr, cc), lambda i, j, c_ref: (i, j, 0))
    return pl.pallas_call(
        body, name=name,
        grid_spec=pltpu.PrefetchScalarGridSpec(
            num_scalar_prefetch=1, grid=(n, nb),
            in_specs=[pl.BlockSpec((None, tr, cc), lambda i, j, c_ref: (i, c_ref[0] * nb + j, 0)), spec],
            out_specs=spec),
        out_shape=jax.ShapeDtypeStruct((n, hr, cc), F32), compiler_params=_params(2),
    )(c, g, x)


def _sum_slots(name, own, landed, chip):
    n, r, c = own.shape
    tr = 128 if r % 128 == 0 else r

    def body(chip_ref, a_ref, b_ref, c_ref, d_ref, o_ref):
        o_ref[...] = ((a_ref[...] + b_ref[...]) + c_ref[...]) + d_ref[...]

    slot = lambda d: pl.BlockSpec((None, tr, c), lambda j, chip_ref: ((chip_ref[0] + d) % n, j, 0))
    return pl.pallas_call(
        body, name=name,
        grid_spec=pltpu.PrefetchScalarGridSpec(
            num_scalar_prefetch=1, grid=(r // tr,), in_specs=[slot(0), slot(1), slot(2), slot(3)],
            out_specs=pl.BlockSpec((tr, c), lambda j, chip_ref: (j, 0))),
        out_shape=jax.ShapeDtypeStruct((r, c), F32), compiler_params=_params(1),
    )(chip, own, landed, landed, landed)


def _me():
    return lax.axis_index("x"), lax.axis_index("y"), lax.axis_index("c")


def _other_chips(x, y):
    return [(1 - x, y), (x, 1 - y), (1 - x, 1 - y)]


def _dma_sems(*counts):
    return [pltpu.SemaphoreType.DMA((n,)) for n in counts]


def _half_rows(rows, which, align):
    return pl.ds(pl.multiple_of(which * (rows // 2), align), rows // 2)


def _gather_first_layer(shards):
    n = len(shards)

    def body(*refs):
        ins, outs, keep, stage = refs[:n], refs[n:2 * n], refs[2 * n:3 * n], refs[3 * n:4 * n]
        ici_send, ici_recv, d2d_send, d2d_recv, local_sems = refs[4 * n:]
        x, y, c = _me()
        k = 2 * x + y
        chips = _other_chips(x, y)
        local, first, passed = [], [], []
        for t in range(n):
            stage[t][...] = ins[t][0].astype(BF16)
            keep[t][...] = ins[t][1].astype(BF16)
            cp = pltpu.make_async_copy(stage[t], outs[t].at[k], local_sems.at[t])
            cp.start()
            local.append(cp)
        for t in range(n):
            mine = _half_rows(shards[t].shape[1], c, 16)
            for j, (px, py) in enumerate(chips):
                cp = pltpu.make_async_remote_copy(
                    src_ref=stage[t].at[mine], dst_ref=outs[t].at[k, mine], send_sem=ici_send.at[3 * t + j],
                    recv_sem=ici_recv.at[3 * t + j], device_id=(px, py, c), device_id_type=MESH)
                cp.start()
                first.append(cp)
        for t in range(n):
            mine = _half_rows(shards[t].shape[1], c, 16)
            for j, (px, py) in enumerate(chips):
                landed = outs[t].at[2 * px + py, mine]
                first[3 * t + j].wait_recv()
                cp = pltpu.make_async_remote_copy(
                    src_ref=landed, dst_ref=landed, send_sem=d2d_send.at[3 * t + j],
                    recv_sem=d2d_recv.at[3 * t + j], device_id=(x, y, 1 - c), device_id_type=MESH)
                cp.start()
                passed.append(cp)
        for cp in passed:
            cp.wait_recv()
        for cp in first + passed:
            cp.wait_send()
        for cp in local:
            cp.wait()

    return pl.pallas_call(
        body, name="gather_first_layer",
        in_specs=[VMEM_SPEC] * n, out_specs=[ANY] * n + [VMEM_SPEC] * n,
        out_shape=[jax.ShapeDtypeStruct((4,) + s.shape[1:], BF16) for s in shards]
        + [jax.ShapeDtypeStruct(s.shape[1:], BF16) for s in shards],
        scratch_shapes=[pltpu.VMEM(s.shape[1:], BF16) for s in shards] + _dma_sems(3 * n, 3 * n, 3 * n, 3 * n, n),
        compiler_params=pltpu.CompilerParams(vmem_limit_bytes=VMEM_LIMIT),
    )(*shards)


def _run_comm(name, comm):
    nci, nco = len(comm.ins), len(comm.out_shapes)

    def body(*refs):
        copies = comm.make(refs[:nci], refs[nci:nci + nco], refs[nci + nco:])
        for cp in copies:
            cp.start()
        for cp in copies:
            cp.wait()

    return pl.pallas_call(body, name=name, in_specs=[ANY] * nci, out_specs=[ANY] * nco,
                          out_shape=list(comm.out_shapes), scratch_shapes=list(comm.sems))(*comm.ins)


def _gather_comm(mine):
    n = len(mine)

    def make(ins, outs, sems):
        send_sems, recv_sems, local_sems = sems
        x, y, c = _me()
        k = 2 * x + y
        copies = []
        for t in range(n):
            copies.append(pltpu.make_async_copy(ins[t], outs[t].at[k], local_sems.at[t]))
            for j, (px, py) in enumerate(_other_chips(x, y)):
                copies.append(pltpu.make_async_remote_copy(
                    src_ref=ins[t], dst_ref=outs[t].at[k], send_sem=send_sems.at[3 * t + j],
                    recv_sem=recv_sems.at[3 * t + j], device_id=(px, py, c), device_id_type=MESH))
        return copies

    return _Comm(mine, [jax.ShapeDtypeStruct((4,) + a.shape, a.dtype) for a in mine], _dma_sems(3 * n, 3 * n, n), make)


def _swap_comm(grads):
    n = len(grads)

    def make(ins, outs, sems):
        send_sems, recv_sems = sems
        x, y, c = _me()
        return [pltpu.make_async_remote_copy(
            src_ref=ins[t].at[:, _half_rows(grads[t].shape[1], 1 - c, 8)], dst_ref=outs[t],
            send_sem=send_sems.at[t], recv_sem=recv_sems.at[t], device_id=(x, y, 1 - c), device_id_type=MESH)
            for t in range(n)]

    shapes = [jax.ShapeDtypeStruct((g.shape[0], g.shape[1] // 2, g.shape[2]), F32) for g in grads]
    return _Comm(grads, shapes, _dma_sems(n, n), make)


def _scatter_comm(parts):
    n = len(parts)

    def make(ins, outs, sems):
        send_sems, recv_sems = sems
        x, y, c = _me()
        k = 2 * x + y
        return [pltpu.make_async_remote_copy(
            src_ref=ins[t].at[2 * px + py], dst_ref=outs[t].at[k], send_sem=send_sems.at[3 * t + j],
            recv_sem=recv_sems.at[3 * t + j], device_id=(px, py, c), device_id_type=MESH)
            for t in range(n) for j, (px, py) in enumerate(_other_chips(x, y))]

    return _Comm(parts, [jax.ShapeDtypeStruct(p.shape, F32) for p in parts], _dma_sems(3 * n, 3 * n), make)


def _share_comm(totals):
    n = len(totals)

    def make(ins, outs, sems):
        send_sems, recv_sems = sems
        x, y, c = _me()
        return [pltpu.make_async_remote_copy(
            src_ref=ins[t], dst_ref=outs[t], send_sem=send_sems.at[t], recv_sem=recv_sems.at[t],
            device_id=(x, y, 1 - c), device_id_type=MESH) for t in range(n)]

    return _Comm(totals, [jax.ShapeDtypeStruct(t.shape, F32) for t in totals], _dma_sems(n, n), make)


def _small_allreduce_adamw(part, w, m, v):
    shape = part.shape

    def body(part_ref, w_ref, m_ref, v_ref, g_ref, d_ref, nm_ref, nv_ref, slots, send_sems, recv_sems):
        x, y, c = _me()
        me = 4 * x + 2 * y + c
        slots[me] = part_ref[...]
        copies = []
        for d in range(1, 8):
            peer = (x ^ (d >> 2), y ^ ((d >> 1) & 1), c ^ (d & 1))
            cp = pltpu.make_async_remote_copy(
                src_ref=part_ref, dst_ref=slots.at[me], send_sem=send_sems.at[d - 1], recv_sem=recv_sems.at[d - 1],
                device_id=peer, device_id_type=MESH)
            cp.start()
            copies.append(cp)
        for cp in copies:
            cp.wait()
        g = slots[0]
        for i in range(1, 8):
            g = g + slots[i]
        g_ref[...] = g
        d, nm, nv = _adamw_math(w_ref[...], g, m_ref[...], v_ref[...])
        d_ref[...] = d
        nm_ref[...] = nm
        nv_ref[...] = nv

    shp = jax.ShapeDtypeStruct(shape, F32)
    return pl.pallas_call(
        body, name="small_allreduce_adamw", in_specs=[VMEM_SPEC] * 4, out_specs=[VMEM_SPEC] * 4,
        out_shape=[shp, shp, shp, shp],
        scratch_shapes=[pltpu.VMEM((8,) + shape, F32), pltpu.SemaphoreType.DMA((7,)), pltpu.SemaphoreType.DMA((7,))],
    )(part, w, m, v)


TM = 256
T_FOX = 1024
T_DIL = 512
T_SCAN = 512
TN_DW = 1408


def _layer_fwd(l, h, p_i, wts, consts, comm=None):
    w_all, _, w_out, _, w_gate, _, w_ple, norm_g, b_f, qkg, ple_g = wts
    bsum, _, bias_t, rope = consts
    u, z, qa, ka, va, qb, kb, vb, logf, va_t, vb_t = _inproj(l, h, norm_g, w_all, b_f, qkg, bsum, rope, TM)
    c_spread, c_t = _cumsum_gates(l, logf, T_SCAN)
    oa, lse_a, *landed = _attn_fwd(f"fox_fwd_l{l}", True, qa, ka, va_t, (c_spread, c_t), T_FOX, comm)
    ob, lse_b = _attn_fwd(f"dil_fwd_l{l}", False, qb, kb, vb_t, (bias_t,), T_DIL)
    h2, e, gate, out = _outproj(l, h, oa, ob, z, p_i, w_out, ple_g, w_gate, w_ple, TM)
    saved = (h, u, z, qa, ka, va, qb, kb, vb, c_spread, c_t, oa, lse_a, ob, lse_b, h2, e, gate)
    return out, saved, landed


def _reduce_names(tag):
    return [f"reduce_{tag}_{w}" for w in ("w_in", "w_out", "w_ple", "w_gate")]


def _layer_bwd(l, dout, p_i, wts, consts, saved, pending=None, core=None, chip=None):
    _, w_all_t, _, w_out_t, _, w_gate_t, _, norm_g, b_f, qkg, ple_g = wts
    bsum, hsel, bias_t, rope = consts
    h, u, z, qa, ka, va, qb, kb, vb, c_spread, c_t, oa, lse_a, ob, lse_b, h2, e, gate = saved
    fused = pending is not None
    dh2, doa, dob, dga, dgb, delta_t, dw_out, dw_gate, dw_ple, dple_g, *sib = _outproj_bwd(
        l, dout, h2, e, gate, p_i, oa, ob, z, w_out_t, ple_g, w_gate_t, hsel, TM,
        _swap_comm(pending) if fused else None)
    if fused:
        pair = [_pair_sum(n, g, x, core) for n, g, x in zip(_reduce_names(f"pair_l{l + 1}"), pending, sib)]
    dqt_a, dk_a, dv_a, dc, drow, *landed = _attn_bwd(
        f"fox_bwd_l{l}", True, qa, ka, va, doa, lse_a, delta_t, 0, (c_spread, c_t), T_FOX,
        _scatter_comm(pair) if fused else None)
    if fused:
        totals = [_sum_slots(n, a, y, chip) for n, a, y in zip(_reduce_names(f"chips_l{l + 1}"), pair, landed)]
    dqt_b, dk_b, dv_b, *other = _attn_bwd(f"dil_bwd_l{l}", False, qb, kb, vb, dob, lse_b, delta_t, N_PAIRS,
                                            (bias_t,), T_DIL, _share_comm(totals) if fused else None)
    dlogf = _rev_cumsum_gates(l, dc, drow, T_SCAN)
    dz, dqkg, dbf = _inproj_bwd_prep(l, z, dqt_a, dk_a, dv_a, dqt_b, dk_b, dv_b, dga, dgb, dlogf, b_f, qkg,
                                     bsum, rope, TM)
    dh, dnorm_g = _inproj_bwd_dx(l, dz, w_all_t, h, norm_g, dh2, TM)
    dw_all = _inproj_bwd_dw(l, u, dz, TM * 2, TN_DW)
    reduced = list(zip(totals, other)) if fused else None
    return dh, (dw_all, dw_out, dw_ple, dw_gate, dnorm_g[0], dbf[0, :N_HEADS], dqkg[:4], dple_g[0]), reduced


def _reduce_last(grads, core, chip, l):
    sib = _run_comm(f"reduce_swap_l{l}", _swap_comm(grads))
    pair = [_pair_sum(n, g, x, core) for n, g, x in zip(_reduce_names(f"pair_l{l}"), grads, sib)]
    landed = _run_comm(f"reduce_scatter_l{l}", _scatter_comm(pair))
    totals = [_sum_slots(n, a, y, chip) for n, a, y in zip(_reduce_names(f"chips_l{l}"), pair, landed)]
    return list(zip(totals, _run_comm(f"reduce_share_l{l}", _share_comm(totals))))


N_FA = 2048


def _layer_weights(l, gathered, norm_g, b_f, qk_norm_g, ple_norm_g):
    g_in, g_out, g_ple, g_gate = gathered
    w_in = jnp.transpose(g_in, (1, 0, 2)).reshape(D_MODEL, N_IN)
    w_all = jnp.concatenate([w_in[:, :N_FA], w_in[:, N_FA + N_HEADS:],
                             jnp.pad(w_in[:, N_FA:N_FA + N_HEADS], ((0, 0), (0, LANE - N_HEADS)))], axis=1)
    w_out = g_out.reshape(D_MODEL, D_MODEL)
    w_gate = g_gate.reshape(D_MODEL, D_MODEL)
    w_ple = jnp.transpose(g_ple, (1, 0, 2)).reshape(PLE_DIM, D_MODEL)
    qkg = jnp.pad(jnp.tile(qk_norm_g[l], (1, N_HEADS)), ((0, 4), (0, 0)))
    bf = jnp.pad(b_f[l], (0, LANE - N_HEADS))[None, :]
    return (w_all, w_all.T, w_out, w_out.T, w_gate, w_gate.T, w_ple, norm_g[l][None, :], bf, qkg,
            ple_norm_g[l][None, :])


def _slot_layout(dw_all, dw_out, dw_ple, dw_gate):
    dw_in = jnp.concatenate([dw_all[:, :N_FA], dw_all[:, N_MAIN:N_MAIN + N_HEADS], dw_all[:, N_FA:N_MAIN]], axis=1)
    return (jnp.transpose(dw_in.reshape(D_MODEL, 4, N_IN // 4), (1, 0, 2)),
            dw_out.reshape(4, D_MODEL // 4, D_MODEL),
            jnp.transpose(dw_ple.reshape(PLE_DIM, 4, D_MODEL // 4), (1, 0, 2)),
            dw_gate.reshape(4, D_MODEL // 4, D_MODEL))


SMALL_ROWS = 40


def _pack_small(norm_g, ple_norm_g, qk_norm_g, b_f):
    flat = jnp.concatenate([norm_g.reshape(-1), ple_norm_g.reshape(-1), qk_norm_g.reshape(-1), b_f.reshape(-1)])
    return jnp.pad(flat, (0, SMALL_ROWS * LANE - flat.shape[0])).reshape(SMALL_ROWS, LANE)


def _unpack_small(packed):
    flat = packed.reshape(-1)
    n1, n2, n3 = 2 * D_MODEL, 4 * D_MODEL, 4 * D_MODEL + 2 * 4 * HEAD_DIM
    return (flat[:n1].reshape(2, D_MODEL), flat[n1:n2].reshape(2, D_MODEL), flat[n2:n3].reshape(2, 4, HEAD_DIM),
            flat[n3:n3 + 2 * N_HEADS].reshape(2, N_HEADS))


def kernel(x, p, positions, norm_g, w_in, b_f, qk_norm_g, w_out, w_ple, ple_norm_g, w_ple_gate, loss_target,
           m_norm_g, m_w_in, m_b_f, m_qk_norm_g, m_w_out, m_w_ple, m_ple_norm_g, m_w_ple_gate,
           v_norm_g, v_w_in, v_b_f, v_qk_norm_g, v_w_out, v_w_ple, v_ple_norm_g, v_w_ple_gate):
    assert w_in.shape[0] == 2, "the schedule below is written for two layers"
    *first, = _gather_first_layer([w_in, w_out, w_ple, w_ple_gate])
    consts = (_head_block_diag(), _head_select(), _dil_bias(T_DIL), _rope_tables(positions[0]))
    small_w = (norm_g, b_f, qk_norm_g, ple_norm_g)
    wts0 = _layer_weights(0, first[:4], *small_w)
    h1, saved0, second = _layer_fwd(0, x[0], p[0, 0], wts0, consts, _gather_comm(first[4:]))
    wts1 = _layer_weights(1, second, *small_w)
    h2, saved1, _ = _layer_fwd(1, h1, p[1, 0], wts1, consts)
    sq, dh = _loss_head(h2, loss_target[0], TM)
    loss = lax.psum(0.5 / D_MODEL * jnp.sum(sq), ("x", "y", "c"))

    core = lax.axis_index("c").astype(jnp.int32).reshape(1)
    chip = (2 * lax.axis_index("x") + lax.axis_index("y")).astype(jnp.int32).reshape(1)
    dh, grads1, _ = _layer_bwd(1, dh, p[1, 0], wts1, consts, saved1)
    dh, grads0, reduced1 = _layer_bwd(0, dh, p[0, 0], wts0, consts, saved0, _slot_layout(*grads1[:4]), core, chip)
    reduced0 = _reduce_last(_slot_layout(*grads0[:4]), core, chip, 0)
    grad_x = dh[None]
    small = [grads0[4:], grads1[4:]]
    n_layers = 2

    outs = {}
    for t, (name, w, m, v) in enumerate((("w_in", w_in, m_w_in, v_w_in), ("w_out", w_out, m_w_out, v_w_out),
                                         ("w_ple", w_ple, m_w_ple, v_w_ple),
                                         ("w_ple_gate", w_ple_gate, m_w_ple_gate, v_w_ple_gate))):
        outs[name] = tuple(_adamw(f"adamw_{name}", w, (reduced0[t], reduced1[t]), m, v, core))

    part = _pack_small(jnp.stack([s[0] for s in small]), jnp.stack([s[3] for s in small]),
                       jnp.stack([s[2] for s in small]).reshape(n_layers, 4, N_HEADS, HEAD_DIM).sum(axis=2),
                       jnp.stack([s[1] for s in small]))
    packed = _small_allreduce_adamw(part, _pack_small(norm_g, ple_norm_g, qk_norm_g, b_f),
                                    _pack_small(m_norm_g, m_ple_norm_g, m_qk_norm_g, m_b_f),
                                    _pack_small(v_norm_g, v_ple_norm_g, v_qk_norm_g, v_b_f))
    sm = [_unpack_small(a) for a in packed]
    for i, name in enumerate(("norm_g", "ple_norm_g", "qk_norm_g", "b_f")):
        outs[name] = tuple(sm[j][i] for j in range(4))

    order = ("norm_g", "w_in", "b_f", "qk_norm_g", "w_out", "w_ple", "ple_norm_g", "w_ple_gate")
    return (loss, grad_x) + tuple(outs[n][j] for j in range(4) for n in order)
```

```python
import functools
from typing import Any, Callable, NamedTuple, Sequence

import numpy as np
import jax
import jax.numpy as jnp
from jax import lax
from jax.experimental import pallas as pl
from jax.experimental.pallas import tpu as pltpu

F32 = jnp.float32
BF16 = jnp.bfloat16
MESH = pl.DeviceIdType.MESH

D_MODEL = 1024
HEAD_DIM = 64
D_BRANCH = 512
N_HEADS = 8
N_PAIRS = 4
N_IN = 4104
N_MAIN = 4096
N_ALL = 4224
PLE_DIM = 256
ROPE_THETA = 500000.0
ROPE_HALF = 8
EPS = 1e-6
NEG = -1e30
M_INIT = -1e29
Q_SCALE = HEAD_DIM ** -0.5
LOG2E = 1.4426950408889634
LN2 = 0.6931471805599453
DIL_PATTERNS = ((128, 1), (512, 4), (2048, 16))
DIL_BACK = 2048
ADAM_LR, ADAM_B1, ADAM_B2, ADAM_EPS, ADAM_WD, ADAM_STEP = 0.001, 0.9, 0.999, 1e-08, 0.01, 10
VMEM_LIMIT = 56 * 1024 * 1024
LANE = 128


def _dot(a, b):
    return jnp.dot(a, b, preferred_element_type=F32)


def _dot_nt(a, b):
    return lax.dot_general(a, b, (((1,), (1,)), ((), ())), preferred_element_type=F32)


def _dot_tn(a, b):
    return lax.dot_general(a, b, (((0,), (0,)), ((), ())), preferred_element_type=F32)


def _split_dot(x, w):
    hi = x.astype(BF16)
    lo = (x - hi.astype(F32)).astype(BF16)
    return _dot(hi, w) + _dot(lo, w)


def _head_sums(x, bs):
    w = bs.shape[0]
    return jnp.concatenate([_split_dot(x[:, w * k:w * (k + 1)], bs) for k in range(x.shape[1] // w)], axis=1)


def _split3_dot(w, x):
    hi = x.astype(BF16)
    r1 = x - hi.astype(F32)
    mid = r1.astype(BF16)
    lo = (r1 - mid.astype(F32)).astype(BF16)
    return _dot(w, hi) + _dot(w, mid) + _dot(w, lo)


def _sigmoid(x):
    return 1.0 / (1.0 + jnp.exp(-x))


def _params(n_grid):
    return pltpu.CompilerParams(dimension_semantics=("arbitrary",) * n_grid,
                                vmem_limit_bytes=VMEM_LIMIT)


def _full(shape):
    nd = len(shape)
    return pl.BlockSpec(shape, lambda *_: (0,) * nd)


ANY = pl.BlockSpec(memory_space=pl.ANY)
VMEM_SPEC = pl.BlockSpec(memory_space=pltpu.VMEM)


class _Comm(NamedTuple):
    ins: Sequence[Any]
    out_shapes: Sequence[Any]
    sems: Sequence[Any]
    make: Callable[..., Any]


def _fuse_comm(body, n_in, n_out, comm, grid):
    if comm is None:
        return body
    nci, nco, ncs = len(comm.ins), len(comm.out_shapes), len(comm.sems)

    def fused(*refs):
        a, b = n_in + nci, n_in + nci + n_out
        ins, cins, outs, couts = refs[:n_in], refs[n_in:a], refs[a:b], refs[b:b + nco]
        scratch, sems = refs[b + nco:len(refs) - ncs], refs[len(refs) - ncs:]
        first = functools.reduce(jnp.logical_and, [pl.program_id(d) == 0 for d in range(len(grid))])
        last = functools.reduce(jnp.logical_and, [pl.program_id(d) == n - 1 for d, n in enumerate(grid)])

        @pl.when(first)
        def _():
            for cp in comm.make(cins, couts, sems):
                cp.start()

        body(*ins, *outs, *scratch)

        @pl.when(last)
        def _():
            for cp in comm.make(cins, couts, sems):
                cp.wait()

    return fused


def _comm_args(comm):
    if comm is None:
        return [], [], [], [], []
    return (list(comm.ins), [ANY] * len(comm.ins), [ANY] * len(comm.out_shapes), list(comm.out_shapes),
            list(comm.sems))


HEADS_PER_BLOCK = 4


def _head_block_diag():
    i = np.arange(HEADS_PER_BLOCK * HEAD_DIM)
    return jnp.asarray((i[:, None] // HEAD_DIM == i[None, :] // HEAD_DIM).astype(np.float32), BF16)


def _head_select():
    i = np.arange(2 * D_BRANCH)
    j = np.arange(LANE)
    return jnp.asarray((i[:, None] // HEAD_DIM == j[None, :]).astype(np.float32), BF16)


def _dil_bias(t):
    nb = DIL_BACK // t + 1
    qi = np.arange(t)[:, None]
    ki = np.arange(t)[None, :]
    tiles = []
    for r in range(nb):
        d = r * t + qi - ki
        mult = np.zeros((t, t), np.int64)
        for window, dil in DIL_PATTERNS:
            mult += ((d >= 0) & (d <= window) & (d % dil == 0)).astype(np.int64)
        b = np.where(mult > 0, np.log2(np.maximum(mult, 1)), NEG).astype(np.float32)
        tiles.append(b.T)
    return jnp.asarray(np.stack(tiles))


def _rope_tables(positions):
    inv_freq = ROPE_THETA ** (-jnp.arange(ROPE_HALF, dtype=F32) / ROPE_HALF)
    ang = positions.astype(F32)[:, None] * inv_freq
    cos, sin = jnp.cos(ang), jnp.sin(ang)
    s = positions.shape[0]
    rest = HEAD_DIM - 2 * ROPE_HALF
    one, zero, zero8 = jnp.ones((s, rest), F32), jnp.zeros((s, rest), F32), jnp.zeros((s, ROPE_HALF), F32)
    c = jnp.concatenate([cos, cos, one], axis=1)
    s1 = jnp.concatenate([zero8, sin, zero], axis=1)
    s2 = jnp.concatenate([-sin, zero8, zero], axis=1)
    return tuple(jnp.tile(t, (1, 2)) for t in (c, s1, s2))


def _rope_fwd(x, c, s1, s2):
    return x * c + pltpu.roll(x, ROPE_HALF, 1) * s1 + pltpu.roll(x, LANE - ROPE_HALF, 1) * s2


def _rope_bwd(dy, c, s1, s2):
    return dy * c + pltpu.roll(dy * s1, LANE - ROPE_HALF, 1) + pltpu.roll(dy * s2, ROPE_HALF, 1)


def _log_sigmoid(x):
    return jnp.minimum(x, 0.0) - jnp.log(1.0 + jnp.exp(-jnp.abs(x)))


def _inproj(l, h, norm_g, w_all, b_f, qkg, bsum, rope, tm):
    s = h.shape[0]
    rc, rs1, rs2 = rope

    def body(h_ref, g_ref, w_ref, bf_ref, qkg_ref, bsum_ref, rc_ref, rs1_ref, rs2_ref,
             u_ref, z_ref, qa_ref, ka_ref, va_ref, qb_ref, kb_ref, vb_ref, lf_ref, vat_ref, vbt_ref):
        hh = h_ref[...]
        r = lax.rsqrt(jnp.mean(hh * hh, axis=-1, keepdims=True) + EPS)
        u = (hh * r * g_ref[...]).astype(BF16)
        u_ref[...] = u
        for k in range(N_ALL // LANE // 3):
            cols = slice(3 * LANE * k, 3 * LANE * (k + 1))
            z_ref[:, cols] = _dot(u, w_ref[:, cols])
        bs = bsum_ref[...]

        def head_norm(x, row):
            ms = _head_sums(x * x, bs) * (1.0 / HEAD_DIM)
            return x * lax.rsqrt(ms + EPS) * qkg_ref[row:row + 1, :]

        def seg(k):
            return z_ref[:, D_BRANCH * k:D_BRANCH * (k + 1)]

        qa_ref[...] = (head_norm(seg(0), 0) * (Q_SCALE * LOG2E)).astype(BF16)
        ka_ref[...] = head_norm(seg(1), 1).astype(BF16)
        va_ref[...] = seg(2).astype(BF16)
        vat_ref[...] = seg(2).T.astype(BF16)
        qn = head_norm(seg(4), 2) * (Q_SCALE * LOG2E)
        kn = head_norm(seg(5), 3)
        c, s1, s2 = rc_ref[...], rs1_ref[...], rs2_ref[...]
        for k in range(D_BRANCH // LANE):
            cols = slice(LANE * k, LANE * (k + 1))
            qb_ref[:, cols] = _rope_fwd(qn[:, cols], c, s1, s2).astype(BF16)
            kb_ref[:, cols] = _rope_fwd(kn[:, cols], c, s1, s2).astype(BF16)
        vb_ref[...] = seg(6).astype(BF16)
        vbt_ref[...] = seg(6).T.astype(BF16)
        lf_ref[...] = _log_sigmoid(z_ref[:, N_MAIN:N_ALL] + bf_ref[...])

    row = lambda w: pl.BlockSpec((tm, w), lambda i: (i, 0))
    colt = pl.BlockSpec((D_BRANCH, tm), lambda i: (0, i))
    bf = lambda: jax.ShapeDtypeStruct((s, D_BRANCH), BF16)
    bft = lambda: jax.ShapeDtypeStruct((D_BRANCH, s), BF16)
    return pl.pallas_call(
        body, name=f"inproj_l{l}", grid=(s // tm,),
        in_specs=[row(D_MODEL), _full((1, D_MODEL)), _full((D_MODEL, N_ALL)), _full((1, LANE)),
                  _full((8, D_BRANCH)), _full((HEADS_PER_BLOCK * HEAD_DIM,) * 2), row(LANE), row(LANE), row(LANE)],
        out_specs=[row(D_MODEL), row(N_ALL)] + [row(D_BRANCH)] * 6 + [row(LANE), colt, colt],
        out_shape=[jax.ShapeDtypeStruct((s, D_MODEL), BF16), jax.ShapeDtypeStruct((s, N_ALL), F32),
                   bf(), bf(), bf(), bf(), bf(), bf(), jax.ShapeDtypeStruct((s, LANE), F32), bft(), bft()],
        compiler_params=_params(1),
    )(h, norm_g, w_all, b_f, qkg, bsum, rc, rs1, rs2)


def _tri(t, upper):
    a = lax.broadcasted_iota(jnp.int32, (t, t), 0)
    b = lax.broadcasted_iota(jnp.int32, (t, t), 1)
    return jnp.where((b >= a) if upper else (b <= a), 1.0, 0.0).astype(BF16)


def _cumsum_gates(l, logf, t):
    s = logf.shape[0]

    def body(lf_ref, cs_ref, ct_ref, carry):
        @pl.when(pl.program_id(0) == 0)
        def _():
            carry[...] = jnp.zeros_like(carry)

        x = lf_ref[...]
        c = _split3_dot(_tri(t, False), x) + carry[0:1, :]
        carry[...] = jnp.broadcast_to(c[t - 1:t, :], carry.shape)
        c = c * LOG2E
        ct = c.T
        for p in range(N_PAIRS):
            cs_ref[:, LANE * p:LANE * (p + 1)] = c if p == 0 else pltpu.roll(c, LANE - 2 * p, 1)
            ct_ref[p, :, :] = ct[2 * p:2 * p + 2, :]

    return pl.pallas_call(
        body, name=f"cumsum_l{l}", grid=(s // t,),
        in_specs=[pl.BlockSpec((t, LANE), lambda i: (i, 0))],
        out_specs=[pl.BlockSpec((t, N_PAIRS * LANE), lambda i: (i, 0)),
                   pl.BlockSpec((N_PAIRS, 2, t), lambda i: (0, 0, i))],
        out_shape=[jax.ShapeDtypeStruct((s, N_PAIRS * LANE), F32), jax.ShapeDtypeStruct((N_PAIRS, 2, s), F32)],
        scratch_shapes=[pltpu.VMEM((8, LANE), F32)],
        compiler_params=_params(1),
    )(logf)


def _rev_cumsum_gates(l, dc_spread, drow, t):
    s = dc_spread.shape[0]
    n = s // t

    def body(dc_ref, drow_ref, out_ref, carry):
        @pl.when(pl.program_id(0) == 0)
        def _():
            carry[...] = jnp.zeros_like(carry)

        lane = lax.broadcasted_iota(jnp.int32, (t, LANE), 1)
        rows = jnp.concatenate([drow_ref[p] for p in range(N_PAIRS)] + [jnp.zeros((LANE - N_HEADS, t), F32)], axis=0)
        x = rows.T
        for p in range(N_PAIRS):
            xp = jnp.where(lane < 2, dc_ref[:, LANE * p:LANE * (p + 1)], 0.0)
            x = x + (xp if p == 0 else pltpu.roll(xp, 2 * p, 1))
        out = _split3_dot(_tri(t, True), x) + carry[0:1, :]
        out_ref[...] = out
        carry[...] = jnp.broadcast_to(out[0:1, :], carry.shape)

    return pl.pallas_call(
        body, name=f"revcumsum_l{l}", grid=(n,),
        in_specs=[pl.BlockSpec((t, N_PAIRS * LANE), lambda i: (n - 1 - i, 0)),
                  pl.BlockSpec((N_PAIRS, 2, t), lambda i: (0, 0, n - 1 - i))],
        out_specs=pl.BlockSpec((t, LANE), lambda i: (n - 1 - i, 0)),
        out_shape=jax.ShapeDtypeStruct((s, LANE), F32),
        scratch_shapes=[pltpu.VMEM((8, LANE), F32)],
        compiler_params=_params(1),
    )(dc_spread, drow)


def _attn_fwd(name, fox, q, k, vt, extra, t, comm=None):
    s = q.shape[0]
    nq = s // t
    nb = DIL_BACK // t + 1

    def body(*refs):
        if fox:
            q_ref, k_ref, vt_ref, ccol_ref, crow_ref, o_ref, lse_ref, m_scr, l_scr, acc_scr = refs
        else:
            q_ref, k_ref, vt_ref, bias_ref, o_ref, lse_ref, m_scr, l_scr, acc_scr = refs
        i = pl.program_id(1)
        lane = lax.broadcasted_iota(jnp.int32, (t, LANE), 1)
        first = lane < HEAD_DIM
        qq = q_ref[...]
        zero = jnp.zeros_like(qq)
        qh = (jnp.where(first, qq, zero), jnp.where(first, zero, qq))
        m_scr[...] = jnp.full(m_scr.shape, M_INIT, F32)
        l_scr[...] = jnp.zeros_like(l_scr)
        acc_scr[...] = jnp.zeros_like(acc_scr)
        ones = jnp.ones((16, t), BF16)

        def step(j, diag):
            rows = pl.ds(pl.multiple_of(j * t, t), t)
            ks = k_ref[rows, :]
            vts = jnp.concatenate([vt_ref[:, rows], ones], axis=0)
            if fox:
                ccol = ccol_ref[rows, :]
            for h in range(2):
                st = _dot_nt(ks, qh[h])
                if fox:
                    st = st + (crow_ref[h:h + 1, :] - ccol[:, h:h + 1])
                    if diag:
                        ki = lax.broadcasted_iota(jnp.int32, (t, t), 0)
                        qi = lax.broadcasted_iota(jnp.int32, (t, t), 1)
                        st = jnp.where(ki <= qi, st, NEG)
                else:
                    st = st + bias_ref[i - j]
                m_old = m_scr[h]
                m_new = jnp.maximum(m_old, jnp.max(st, axis=0, keepdims=True))
                alpha = jnp.exp2(m_old - m_new)
                pb = jnp.exp2(st - m_new).astype(BF16)
                pv = _dot(vts, pb)
                l_scr[h] = alpha * l_scr[h] + pv[LANE:LANE + 1, :]
                acc_scr[h] = alpha * acc_scr[h] + pv[:LANE, :]
                m_scr[h] = m_new

        if fox:
            lax.fori_loop(0, i, lambda j, c: (step(j, False), c)[1], 0)
            step(i, True)
        else:
            lax.fori_loop(jnp.maximum(i - (nb - 1), 0), i + 1, lambda j, c: (step(j, False), c)[1], 0)

        sub = lax.broadcasted_iota(jnp.int32, (LANE, t), 0)
        ot = jnp.where(sub < HEAD_DIM, acc_scr[0] / l_scr[0], acc_scr[1] / l_scr[1])
        o_ref[...] = ot.T
        for h in range(2):
            lse_ref[h:h + 1, :] = m_scr[h] + jnp.log2(l_scr[h])

    qspec = pl.BlockSpec((t, LANE), lambda hp, i: (i, hp))
    kspec = pl.BlockSpec((s, LANE), lambda hp, i: (0, hp))
    vtspec = pl.BlockSpec((LANE, s), lambda hp, i: (hp, 0))
    in_specs = [qspec, kspec, vtspec]
    if fox:
        in_specs += [kspec, pl.BlockSpec((None, 2, t), lambda hp, i: (hp, 0, i))]
    else:
        in_specs += [_full((nb, t, t))]
    grid = (N_PAIRS, nq)
    c_in, c_ispec, c_ospec, c_oshape, c_scr = _comm_args(comm)
    return pl.pallas_call(
        _fuse_comm(body, len(in_specs), 2, comm, grid), name=name, grid=grid,
        in_specs=in_specs + c_ispec,
        out_specs=[qspec, pl.BlockSpec((None, 2, t), lambda hp, i: (hp, 0, i))] + c_ospec,
        out_shape=[jax.ShapeDtypeStruct((s, D_BRANCH), F32), jax.ShapeDtypeStruct((N_PAIRS, 2, s), F32)] + c_oshape,
        scratch_shapes=[pltpu.VMEM((2, 1, t), F32), pltpu.VMEM((2, 1, t), F32), pltpu.VMEM((2, LANE, t), F32)]
        + c_scr,
        compiler_params=_params(2),
    )(q, k, vt, *extra, *c_in)


def _attn_bwd(name, fox, q, k, v, do, lse_t, delta_t, pair_offset, extra, t, comm=None):
    s = q.shape[0]
    nk = s // t
    nb = DIL_BACK // t + 1

    def body(*refs):
        if fox:
            (q_ref, k_ref, v_ref, do_ref, lse_ref, delta_ref, ccol_ref, crow_ref,
             dqt_ref, dk_ref, dv_ref, dc_ref, drow_ref) = refs
        else:
            q_ref, k_ref, v_ref, do_ref, lse_ref, delta_ref, bias_ref, dqt_ref, dk_ref, dv_ref = refs
        j = pl.program_id(1)

        @pl.when(j == 0)
        def _():
            dqt_ref[...] = jnp.zeros_like(dqt_ref)
            if fox:
                drow_ref[...] = jnp.zeros_like(drow_ref)

        lane = lax.broadcasted_iota(jnp.int32, (t, LANE), 1)
        first = lane < HEAD_DIM
        ks = k_ref[...]
        vs = v_ref[...]
        kt = ks.astype(F32).T
        sub = lax.broadcasted_iota(jnp.int32, (LANE, t), 0)
        kth = (jnp.where(sub < HEAD_DIM, kt, 0.0).astype(BF16), jnp.where(sub < HEAD_DIM, 0.0, kt).astype(BF16))
        dk_ref[...] = jnp.zeros_like(dk_ref)
        dv_ref[...] = jnp.zeros_like(dv_ref)
        if fox:
            dc_ref[...] = jnp.zeros_like(dc_ref)
            ccol = ccol_ref[...]

        def step(i, diag):
            rows = pl.ds(pl.multiple_of(i * t, t), t)
            qq = q_ref[rows, :]
            dd = do_ref[rows, :]
            zero = jnp.zeros_like(qq)
            qh = (jnp.where(first, qq, zero), jnp.where(first, zero, qq))
            dh = (jnp.where(first, dd, zero), jnp.where(first, zero, dd))
            for h in range(2):
                st = _dot_nt(ks, qh[h])
                if fox:
                    st = st + (crow_ref[h:h + 1, rows] - ccol[:, h:h + 1])
                    if diag:
                        ki = lax.broadcasted_iota(jnp.int32, (t, t), 0)
                        qi = lax.broadcasted_iota(jnp.int32, (t, t), 1)
                        st = jnp.where(ki <= qi, st, NEG)
                else:
                    st = st + bias_ref[i - j]
                pt = jnp.exp2(st - lse_ref[h:h + 1, rows])
                dpt = _dot_nt(vs, dh[h])
                dst = pt * (dpt - delta_ref[h:h + 1, rows])
                dv_ref[...] += _dot(pt.astype(BF16), dh[h])
                dsb = dst.astype(BF16)
                dk_ref[...] += _dot(dsb, qh[h])
                dqt_ref[:, rows] += _dot(kth[h], dsb)
                if fox:
                    dc_ref[...] -= jnp.where(lane == h, jnp.sum(dst, axis=1, keepdims=True), 0.0)
                    drow_ref[h:h + 1, rows] += jnp.sum(dst, axis=0, keepdims=True)

        if fox:
            step(j, True)
            lax.fori_loop(j + 1, nk, lambda i, c: (step(i, False), c)[1], 0)
        else:
            lax.fori_loop(j, jnp.minimum(j + nb, nk), lambda i, c: (step(i, False), c)[1], 0)

    kspec = pl.BlockSpec((t, LANE), lambda hp, j: (j, hp))
    qspec = pl.BlockSpec((s, LANE), lambda hp, j: (0, hp))
    rowspec = pl.BlockSpec((None, 2, s), lambda hp, j: (hp, 0, 0))
    drowspec = pl.BlockSpec((None, 2, s), lambda hp, j: (hp + pair_offset, 0, 0))
    in_specs = [qspec, kspec, kspec, qspec, rowspec, drowspec]
    out_specs = [pl.BlockSpec((LANE, s), lambda hp, j: (hp, 0)), kspec, kspec]
    out_shape = [jax.ShapeDtypeStruct((D_BRANCH, s), F32), jax.ShapeDtypeStruct((s, D_BRANCH), F32),
                 jax.ShapeDtypeStruct((s, D_BRANCH), F32)]
    if fox:
        in_specs += [kspec, rowspec]
        out_specs += [kspec, rowspec]
        out_shape += [jax.ShapeDtypeStruct((s, N_PAIRS * LANE), F32), jax.ShapeDtypeStruct((N_PAIRS, 2, s), F32)]
    else:
        in_specs += [_full((nb, t, t))]
    grid = (N_PAIRS, nk)
    c_in, c_ispec, c_ospec, c_oshape, c_scr = _comm_args(comm)
    return pl.pallas_call(
        _fuse_comm(body, len(in_specs), len(out_specs), comm, grid), name=name, grid=grid,
        in_specs=in_specs + c_ispec, out_specs=out_specs + c_ospec, out_shape=out_shape + c_oshape,
        scratch_shapes=c_scr, compiler_params=_params(2),
    )(q, k, v, do, lse_t, delta_t, *extra, *c_in)


def _silu(x):
    return x * _sigmoid(x)


def _outproj(l, h, oa, ob, z, p_i, w_out, ple_g, w_gate, w_ple, tm):
    s = h.shape[0]

    def body(h_ref, oa_ref, ob_ref, ga_ref, gb_ref, p_ref, wo_ref, pg_ref, wg_ref, wp_ref,
             h2_ref, e_ref, gate_ref, out_ref):
        a = jnp.concatenate([oa_ref[...] * _silu(ga_ref[...]), ob_ref[...] * _silu(gb_ref[...])], axis=1)
        h2 = h_ref[...] + _dot(a.astype(BF16), wo_ref[...])
        h2_ref[...] = h2
        r = lax.rsqrt(jnp.mean(h2 * h2, axis=-1, keepdims=True) + EPS)
        n2 = (h2 * r * pg_ref[...]).astype(BF16)
        gate = _sigmoid(_dot(n2, wg_ref[...]))
        e = _dot(p_ref[...].astype(BF16), wp_ref[...])
        e_ref[...] = e
        gate_ref[...] = gate
        out_ref[...] = h2 + e * gate

    row = lambda w: pl.BlockSpec((tm, w), lambda i: (i, 0))
    zcol = lambda k: pl.BlockSpec((tm, D_BRANCH), lambda i: (i, k))
    f = lambda: jax.ShapeDtypeStruct((s, D_MODEL), F32)
    return pl.pallas_call(
        body, name=f"outproj_l{l}", grid=(s // tm,),
        in_specs=[row(D_MODEL), row(D_BRANCH), row(D_BRANCH), zcol(3), zcol(7), row(PLE_DIM),
                  _full((D_MODEL, D_MODEL)), _full((1, D_MODEL)), _full((D_MODEL, D_MODEL)),
                  _full((PLE_DIM, D_MODEL))],
        out_specs=[row(D_MODEL)] * 4, out_shape=[f(), f(), f(), f()],
        compiler_params=_params(1),
    )(h, oa, ob, z, z, p_i, w_out, ple_g, w_gate, w_ple)


def _loss_head(y, target, tm):
    s = y.shape[0]

    def body(y_ref, t_ref, acc_ref, dy_ref):
        @pl.when(pl.program_id(0) == 0)
        def _():
            acc_ref[...] = jnp.zeros_like(acc_ref)

        err = y_ref[...] - t_ref[...]
        dy_ref[...] = err * (1.0 / D_MODEL)
        e2 = err * err
        part = jnp.zeros((8, LANE), F32)
        for r in range(tm // 8):
            for c in range(D_MODEL // LANE):
                part = part + e2[8 * r:8 * (r + 1), LANE * c:LANE * (c + 1)]
        acc_ref[...] += part

    row = pl.BlockSpec((tm, D_MODEL), lambda i: (i, 0))
    return pl.pallas_call(
        body, name="loss_head", grid=(s // tm,), in_specs=[row, row],
        out_specs=[_full((8, LANE)), row],
        out_shape=[jax.ShapeDtypeStruct((8, LANE), F32), jax.ShapeDtypeStruct((s, D_MODEL), F32)],
        compiler_params=_params(1),
    )(y, target)


def _outproj_bwd(l, dout, h2, e, gate, p_i, oa, ob, z, w_out_t, ple_g, w_gate_t, hsel, tm, comm=None):
    s = dout.shape[0]

    def body(do_ref, h2_ref, e_ref, gate_ref, p_ref, oa_ref, ob_ref, ga_ref, gb_ref, wot_ref, pg_ref,
             wgt_ref, hsel_ref,
             dh2_ref, doa_ref, dob_ref, dga_ref, dgb_ref, delta_ref, dwo_ref, dwg_ref, dwp_ref, dpg_ref):
        @pl.when(pl.program_id(0) == 0)
        def _():
            dwo_ref[...] = jnp.zeros_like(dwo_ref)
            dwg_ref[...] = jnp.zeros_like(dwg_ref)
            dwp_ref[...] = jnp.zeros_like(dwp_ref)
            dpg_ref[...] = jnp.zeros_like(dpg_ref)

        dho = do_ref[...]
        g = gate_ref[...]
        de = (dho * g).astype(BF16)
        dwp_ref[...] += _dot_tn(p_ref[...].astype(BF16), de)
        dpre = (dho * e_ref[...] * g * (1.0 - g)).astype(BF16)
        h2 = h2_ref[...]
        pg = pg_ref[...]
        r = lax.rsqrt(jnp.mean(h2 * h2, axis=-1, keepdims=True) + EPS)
        n2 = (h2 * r * pg).astype(BF16)
        dwg_ref[...] += _dot_tn(n2, dpre)
        dn2 = _dot(dpre, wgt_ref[...])
        dpg_ref[0:1, :] += jnp.sum(dn2 * h2 * r, axis=0, keepdims=True)
        wv = dn2 * pg
        dh2 = dho + r * wv - h2 * (r * r * r) * jnp.mean(wv * h2, axis=-1, keepdims=True)
        dh2_ref[...] = dh2
        dh2b = dh2.astype(BF16)
        ga, gb, oa, ob = ga_ref[...], gb_ref[...], oa_ref[...], ob_ref[...]
        sga, sgb = _sigmoid(ga), _sigmoid(gb)
        a = jnp.concatenate([oa * ga * sga, ob * gb * sgb], axis=1).astype(BF16)
        dwo_ref[...] += _dot_tn(a, dh2b)
        da = _dot(dh2b, wot_ref[...])
        da_a, da_b = da[:, :D_BRANCH], da[:, D_BRANCH:]
        doa = da_a * ga * sga
        dob = da_b * gb * sgb
        doa_ref[...] = doa.astype(BF16)
        dob_ref[...] = dob.astype(BF16)
        dga_ref[...] = (da_a * oa * sga * (1.0 + ga * (1.0 - sga))).astype(BF16)
        dgb_ref[...] = (da_b * ob * sgb * (1.0 + gb * (1.0 - sgb))).astype(BF16)
        prod = jnp.concatenate([doa * oa, dob * ob], axis=1)
        dt = _split_dot(prod, hsel_ref[...]).T
        for pp in range(2 * N_PAIRS):
            delta_ref[pp, :, :] = dt[2 * pp:2 * pp + 2, :]

    row = lambda w: pl.BlockSpec((tm, w), lambda i: (i, 0))
    zcol = lambda k: pl.BlockSpec((tm, D_BRANCH), lambda i: (i, k))
    grid = (s // tm,)
    c_in, c_ispec, c_ospec, c_oshape, c_scr = _comm_args(comm)
    return pl.pallas_call(
        _fuse_comm(body, 13, 10, comm, grid), name=f"outproj_bwd_l{l}", grid=grid,
        in_specs=[row(D_MODEL)] * 4 + [row(PLE_DIM), row(D_BRANCH), row(D_BRANCH), zcol(3), zcol(7),
                                        _full((D_MODEL, D_MODEL)), _full((1, D_MODEL)), _full((D_MODEL, D_MODEL)),
                                        _full((2 * D_BRANCH, LANE))] + c_ispec,
        out_specs=[row(D_MODEL)] + [row(D_BRANCH)] * 4
        + [pl.BlockSpec((2 * N_PAIRS, 2, tm), lambda i: (0, 0, i)), _full((D_MODEL, D_MODEL)),
           _full((D_MODEL, D_MODEL)), _full((PLE_DIM, D_MODEL)), _full((8, D_MODEL))] + c_ospec,
        out_shape=[jax.ShapeDtypeStruct((s, D_MODEL), F32)] + [jax.ShapeDtypeStruct((s, D_BRANCH), BF16)] * 4
        + [jax.ShapeDtypeStruct((2 * N_PAIRS, 2, s), F32), jax.ShapeDtypeStruct((D_MODEL, D_MODEL), F32),
           jax.ShapeDtypeStruct((D_MODEL, D_MODEL), F32), jax.ShapeDtypeStruct((PLE_DIM, D_MODEL), F32),
           jax.ShapeDtypeStruct((8, D_MODEL), F32)] + c_oshape,
        scratch_shapes=c_scr, compiler_params=_params(1),
    )(dout, h2, e, gate, p_i, oa, ob, z, z, w_out_t, ple_g, w_gate_t, hsel, *c_in)


def _inproj_bwd_prep(l, z, dqt_a, dk_a, dv_a, dqt_b, dk_b, dv_b, dga, dgb, dlogf, b_f, qkg, bsum, rope, tm):
    s = z.shape[0]
    rc, rs1, rs2 = rope

    def body(z_ref, dqta_ref, dka_ref, dva_ref, dqtb_ref, dkb_ref, dvb_ref, dga_ref, dgb_ref, dlf_ref,
             bf_ref, qkg_ref, bsum_ref, rc_ref, rs1_ref, rs2_ref, dz_ref, dqkg_ref, dbf_ref):
        @pl.when(pl.program_id(0) == 0)
        def _():
            dqkg_ref[...] = jnp.zeros_like(dqkg_ref)
            dbf_ref[...] = jnp.zeros_like(dbf_ref)

        bs = bsum_ref[...]
        c, s1, s2 = rc_ref[...], rs1_ref[...], rs2_ref[...]

        def unrope(dy):
            return jnp.concatenate([_rope_bwd(dy[:, LANE * k:LANE * (k + 1)], c, s1, s2)
                                    for k in range(D_BRANCH // LANE)], axis=1)

        def norm_bwd(k, row, dy):
            x = z_ref[:, D_BRANCH * k:D_BRANCH * (k + 1)]
            r = lax.rsqrt(_head_sums(x * x, bs) * (1.0 / HEAD_DIM) + EPS)
            dqkg_ref[row:row + 1, :] += jnp.sum(dy * x * r, axis=0, keepdims=True)
            w = dy * qkg_ref[row:row + 1, :]
            dx = r * w - x * (r * r * r) * (_head_sums(w * x, bs) * (1.0 / HEAD_DIM))
            dz_ref[:, D_BRANCH * k:D_BRANCH * (k + 1)] = dx.astype(BF16)

        norm_bwd(0, 0, dqta_ref[...].T * Q_SCALE)
        norm_bwd(1, 1, dka_ref[...] * LN2)
        dz_ref[:, 2 * D_BRANCH:3 * D_BRANCH] = dva_ref[...].astype(BF16)
        dz_ref[:, 3 * D_BRANCH:4 * D_BRANCH] = dga_ref[...]
        norm_bwd(4, 2, unrope(dqtb_ref[...].T * Q_SCALE))
        norm_bwd(5, 3, unrope(dkb_ref[...] * LN2))
        dz_ref[:, 6 * D_BRANCH:7 * D_BRANCH] = dvb_ref[...].astype(BF16)
        dz_ref[:, 7 * D_BRANCH:8 * D_BRANCH] = dgb_ref[...]
        dfa = dlf_ref[...] * _sigmoid(-(z_ref[:, N_MAIN:N_ALL] + bf_ref[...]))
        dz_ref[:, N_MAIN:N_ALL] = dfa.astype(BF16)
        dbf_ref[0:1, :] += jnp.sum(dfa, axis=0, keepdims=True)

    row = lambda w: pl.BlockSpec((tm, w), lambda i: (i, 0))
    colt = pl.BlockSpec((D_BRANCH, tm), lambda i: (0, i))
    return pl.pallas_call(
        body, name=f"inproj_bwd_prep_l{l}", grid=(s // tm,),
        in_specs=[row(N_ALL), colt, row(D_BRANCH), row(D_BRANCH), colt, row(D_BRANCH), row(D_BRANCH),
                  row(D_BRANCH), row(D_BRANCH), row(LANE), _full((1, LANE)), _full((8, D_BRANCH)),
                  _full((HEADS_PER_BLOCK * HEAD_DIM,) * 2), row(LANE), row(LANE), row(LANE)],
        out_specs=[row(N_ALL), _full((8, D_BRANCH)), _full((8, LANE))],
        out_shape=[jax.ShapeDtypeStruct((s, N_ALL), BF16), jax.ShapeDtypeStruct((8, D_BRANCH), F32),
                   jax.ShapeDtypeStruct((8, LANE), F32)],
        compiler_params=_params(1),
    )(z, dqt_a, dk_a, dv_a, dqt_b, dk_b, dv_b, dga, dgb, dlogf, b_f, qkg, bsum, rc, rs1, rs2)


def _inproj_bwd_dx(l, dz, w_all_t, h, norm_g, dh2, tm):
    s = dz.shape[0]

    def body(dz_ref, wt_ref, h_ref, g_ref, dh2_ref, dh_ref, dg_ref):
        @pl.when(pl.program_id(0) == 0)
        def _():
            dg_ref[...] = jnp.zeros_like(dg_ref)

        du = _dot(dz_ref[...], wt_ref[...])
        hh = h_ref[...]
        g = g_ref[...]
        r = lax.rsqrt(jnp.mean(hh * hh, axis=-1, keepdims=True) + EPS)
        dg_ref[0:1, :] += jnp.sum(du * hh * r, axis=0, keepdims=True)
        wv = du * g
        dh_ref[...] = dh2_ref[...] + r * wv - hh * (r * r * r) * jnp.mean(wv * hh, axis=-1, keepdims=True)

    row = lambda w: pl.BlockSpec((tm, w), lambda i: (i, 0))
    return pl.pallas_call(
        body, name=f"inproj_bwd_dx_l{l}", grid=(s // tm,),
        in_specs=[row(N_ALL), _full((N_ALL, D_MODEL)), row(D_MODEL), _full((1, D_MODEL)), row(D_MODEL)],
        out_specs=[row(D_MODEL), _full((8, D_MODEL))],
        out_shape=[jax.ShapeDtypeStruct((s, D_MODEL), F32), jax.ShapeDtypeStruct((8, D_MODEL), F32)],
        compiler_params=_params(1),
    )(dz, w_all_t, h, norm_g, dh2)


def _inproj_bwd_dw(l, u, dz, tm, tn):
    s = u.shape[0]

    def body(u_ref, dz_ref, dw_ref):
        @pl.when(pl.program_id(1) == 0)
        def _():
            dw_ref[...] = jnp.zeros_like(dw_ref)

        dw_ref[...] += _dot_tn(u_ref[...], dz_ref[...])

    return pl.pallas_call(
        body, name=f"inproj_bwd_dw_l{l}", grid=(N_ALL // tn, s // tm),
        in_specs=[pl.BlockSpec((tm, D_MODEL), lambda n, i: (i, 0)), pl.BlockSpec((tm, tn), lambda n, i: (i, n))],
        out_specs=pl.BlockSpec((D_MODEL, tn), lambda n, i: (0, n)),
        out_shape=jax.ShapeDtypeStruct((D_MODEL, N_ALL), F32),
        compiler_params=_params(2),
    )(u, dz)


def _adamw_math(w, g, m, v):
    m = ADAM_B1 * m + (1.0 - ADAM_B1) * g
    v = ADAM_B2 * v + (1.0 - ADAM_B2) * (g * g)
    m_hat = m / (1.0 - ADAM_B1 ** ADAM_STEP)
    v_hat = v / (1.0 - ADAM_B2 ** ADAM_STEP)
    delta = -ADAM_LR * (m_hat / (jnp.sqrt(v_hat) + ADAM_EPS) + ADAM_WD * w)
    return delta, m, v


def _adamw(name, w, halves, m, v, core):
    nl, r, c = w.shape
    hr = r // 2
    tr = 128 if hr % 128 == 0 else hr
    nb = hr // tr

    def body(core_ref, w_ref, own0_ref, oth0_ref, own1_ref, oth1_ref, m_ref, v_ref, g_ref, d_ref, nm_ref, nv_ref):
        first = pl.program_id(0) == 0
        own = jnp.where(first, own0_ref[...], own1_ref[...])
        oth = jnp.where(first, oth0_ref[...], oth1_ref[...])
        g = jnp.where(pl.program_id(1) // nb == core_ref[0], own, oth)
        d, nm, nv = _adamw_math(w_ref[...], g, m_ref[...], v_ref[...])
        g_ref[...] = g
        d_ref[...] = d
        nm_ref[...] = nm
        nv_ref[...] = nv

    spec = pl.BlockSpec((None, tr, c), lambda a, b, core_ref: (a, b, 0))
    gspec = pl.BlockSpec((tr, c), lambda a, b, core_ref: (b % nb, 0))
    shp = jax.ShapeDtypeStruct(w.shape, F32)
    return pl.pallas_call(
        body, name=name,
        grid_spec=pltpu.PrefetchScalarGridSpec(
            num_scalar_prefetch=1, grid=(nl, r // tr), in_specs=[spec] + [gspec] * 4 + [spec, spec],
            out_specs=[spec] * 4),
        out_shape=[shp, shp, shp, shp], compiler_params=_params(2),
    )(core, w, halves[0][0], halves[0][1], halves[1][0], halves[1][1], m, v)


def _pair_sum(name, g, x, c):
    n, r, cc = g.shape
    hr = r // 2
    tr = 128 if hr % 128 == 0 else hr
    nb = hr // tr

    def body(c_ref, g_ref, x_ref, o_ref):
        o_ref[...] = g_ref[...] + x_ref[...]

    spec = pl.BlockSpec((None, tr, cc), lambda i, j, c_ref: (i, j, 0))
    return pl.pallas_call(
        body, name=name,
        grid_spec=pltpu.PrefetchScalarGridSpec(
            num_scalar_prefetch=1, grid=(n, nb),
            in_specs=[pl.BlockSpec((None, tr, cc), lambda i, j, c_ref: (i, c_ref[0] * nb + j, 0)), spec],
            out_specs=spec),
        out_shape=jax.ShapeDtypeStruct((n, hr, cc), F32), compiler_params=_params(2),
    )(c, g, x)


def _sum_slots(name, own, landed, chip):
    n, r, c = own.shape
    tr = 128 if r % 128 == 0 else r

    def body(chip_ref, a_ref, b_ref, c_ref, d_ref, o_ref):
        o_ref[...] = ((a_ref[...] + b_ref[...]) + c_ref[...]) + d_ref[...]

    slot = lambda d: pl.BlockSpec((None, tr, c), lambda j, chip_ref: ((chip_ref[0] + d) % n, j, 0))
    return pl.pallas_call(
        body, name=name,
        grid_spec=pltpu.PrefetchScalarGridSpec(
            num_scalar_prefetch=1, grid=(r // tr,), in_specs=[slot(0), slot(1), slot(2), slot(3)],
            out_specs=pl.BlockSpec((tr, c), lambda j, chip_ref: (j, 0))),
        out_shape=jax.ShapeDtypeStruct((r, c), F32), compiler_params=_params(1),
    )(chip, own, landed, landed, landed)


def _me():
    return lax.axis_index("x"), lax.axis_index("y"), lax.axis_index("c")


def _other_chips(x, y):
    return [(1 - x, y), (x, 1 - y), (1 - x, 1 - y)]


def _dma_sems(*counts):
    return [pltpu.SemaphoreType.DMA((n,)) for n in counts]


def _half_rows(rows, which, align):
    return pl.ds(pl.multiple_of(which * (rows // 2), align), rows // 2)


def _gather_first_layer(shards):
    n = len(shards)

    def body(*refs):
        ins, outs, keep, stage = refs[:n], refs[n:2 * n], refs[2 * n:3 * n], refs[3 * n:4 * n]
        ici_send, ici_recv, d2d_send, d2d_recv, local_sems = refs[4 * n:]
        x, y, c = _me()
        k = 2 * x + y
        chips = _other_chips(x, y)
        local, first, passed = [], [], []
        for t in range(n):
            stage[t][...] = ins[t][0].astype(BF16)
            keep[t][...] = ins[t][1].astype(BF16)
            cp = pltpu.make_async_copy(stage[t], outs[t].at[k], local_sems.at[t])
            cp.start()
            local.append(cp)
        for t in range(n):
            mine = _half_rows(shards[t].shape[1], c, 16)
            for j, (px, py) in enumerate(chips):
                cp = pltpu.make_async_remote_copy(
                    src_ref=stage[t].at[mine], dst_ref=outs[t].at[k, mine], send_sem=ici_send.at[3 * t + j],
                    recv_sem=ici_recv.at[3 * t + j], device_id=(px, py, c), device_id_type=MESH)
                cp.start()
                first.append(cp)
        for t in range(n):
            mine = _half_rows(shards[t].shape[1], c, 16)
            for j, (px, py) in enumerate(chips):
                landed = outs[t].at[2 * px + py, mine]
                first[3 * t + j].wait_recv()
                cp = pltpu.make_async_remote_copy(
                    src_ref=landed, dst_ref=landed, send_sem=d2d_send.at[3 * t + j],
                    recv_sem=d2d_recv.at[3 * t + j], device_id=(x, y, 1 - c), device_id_type=MESH)
                cp.start()
                passed.append(cp)
        for cp in passed:
            cp.wait_recv()
        for cp in first + passed:
            cp.wait_send()
        for cp in local:
            cp.wait()

    return pl.pallas_call(
        body, name="gather_first_layer",
        in_specs=[VMEM_SPEC] * n, out_specs=[ANY] * n + [VMEM_SPEC] * n,
        out_shape=[jax.ShapeDtypeStruct((4,) + s.shape[1:], BF16) for s in shards]
        + [jax.ShapeDtypeStruct(s.shape[1:], BF16) for s in shards],
        scratch_shapes=[pltpu.VMEM(s.shape[1:], BF16) for s in shards] + _dma_sems(3 * n, 3 * n, 3 * n, 3 * n, n),
        compiler_params=pltpu.CompilerParams(vmem_limit_bytes=VMEM_LIMIT),
    )(*shards)


def _run_comm(name, comm):
    nci, nco = len(comm.ins), len(comm.out_shapes)

    def body(*refs):
        copies = comm.make(refs[:nci], refs[nci:nci + nco], refs[nci + nco:])
        for cp in copies:
            cp.start()
        for cp in copies:
            cp.wait()

    return pl.pallas_call(body, name=name, in_specs=[ANY] * nci, out_specs=[ANY] * nco,
                          out_shape=list(comm.out_shapes), scratch_shapes=list(comm.sems))(*comm.ins)


def _gather_comm(mine):
    n = len(mine)

    def make(ins, outs, sems):
        send_sems, recv_sems, local_sems = sems
        x, y, c = _me()
        k = 2 * x + y
        copies = []
        for t in range(n):
            copies.append(pltpu.make_async_copy(ins[t], outs[t].at[k], local_sems.at[t]))
            for j, (px, py) in enumerate(_other_chips(x, y)):
                copies.append(pltpu.make_async_remote_copy(
                    src_ref=ins[t], dst_ref=outs[t].at[k], send_sem=send_sems.at[3 * t + j],
                    recv_sem=recv_sems.at[3 * t + j], device_id=(px, py, c), device_id_type=MESH))
        return copies

    return _Comm(mine, [jax.ShapeDtypeStruct((4,) + a.shape, a.dtype) for a in mine], _dma_sems(3 * n, 3 * n, n), make)


def _swap_comm(grads):
    n = len(grads)

    def make(ins, outs, sems):
        send_sems, recv_sems = sems
        x, y, c = _me()
        return [pltpu.make_async_remote_copy(
            src_ref=ins[t].at[:, _half_rows(grads[t].shape[1], 1 - c, 8)], dst_ref=outs[t],
            send_sem=send_sems.at[t], recv_sem=recv_sems.at[t], device_id=(x, y, 1 - c), device_id_type=MESH)
            for t in range(n)]

    shapes = [jax.ShapeDtypeStruct((g.shape[0], g.shape[1] // 2, g.shape[2]), F32) for g in grads]
    return _Comm(grads, shapes, _dma_sems(n, n), make)


def _scatter_comm(parts):
    n = len(parts)

    def make(ins, outs, sems):
        send_sems, recv_sems = sems
        x, y, c = _me()
        k = 2 * x + y
        return [pltpu.make_async_remote_copy(
            src_ref=ins[t].at[2 * px + py], dst_ref=outs[t].at[k], send_sem=send_sems.at[3 * t + j],
            recv_sem=recv_sems.at[3 * t + j], device_id=(px, py, c), device_id_type=MESH)
            for t in range(n) for j, (px, py) in enumerate(_other_chips(x, y))]

    return _Comm(parts, [jax.ShapeDtypeStruct(p.shape, F32) for p in parts], _dma_sems(3 * n, 3 * n), make)


def _share_comm(totals):
    n = len(totals)

    def make(ins, outs, sems):
        send_sems, recv_sems = sems
        x, y, c = _me()
        return [pltpu.make_async_remote_copy(
            src_ref=ins[t], dst_ref=outs[t], send_sem=send_sems.at[t], recv_sem=recv_sems.at[t],
            device_id=(x, y, 1 - c), device_id_type=MESH) for t in range(n)]

    return _Comm(totals, [jax.ShapeDtypeStruct(t.shape, F32) for t in totals], _dma_sems(n, n), make)


def _small_allreduce_adamw(part, w, m, v):
    shape = part.shape

    def body(part_ref, w_ref, m_ref, v_ref, g_ref, d_ref, nm_ref, nv_ref, slots, send_sems, recv_sems):
        x, y, c = _me()
        me = 4 * x + 2 * y + c
        slots[me] = part_ref[...]
        copies = []
        for d in range(1, 8):
            peer = (x ^ (d >> 2), y ^ ((d >> 1) & 1), c ^ (d & 1))
            cp = pltpu.make_async_remote_copy(
                src_ref=part_ref, dst_ref=slots.at[me], send_sem=send_sems.at[d - 1], recv_sem=recv_sems.at[d - 1],
                device_id=peer, device_id_type=MESH)
            cp.start()
            copies.append(cp)
        for cp in copies:
            cp.wait()
        g = slots[0]
        for i in range(1, 8):
            g = g + slots[i]
        g_ref[...] = g
        d, nm, nv = _adamw_math(w_ref[...], g, m_ref[...], v_ref[...])
        d_ref[...] = d
        nm_ref[...] = nm
        nv_ref[...] = nv

    shp = jax.ShapeDtypeStruct(shape, F32)
    return pl.pallas_call(
        body, name="small_allreduce_adamw", in_specs=[VMEM_SPEC] * 4, out_specs=[VMEM_SPEC] * 4,
        out_shape=[shp, shp, shp, shp],
        scratch_shapes=[pltpu.VMEM((8,) + shape, F32), pltpu.SemaphoreType.DMA((7,)), pltpu.SemaphoreType.DMA((7,))],
    )(part, w, m, v)


TM = 256
T_FOX = 1024
T_DIL = 512
T_SCAN = 512
TN_DW = 1408


def _layer_fwd(l, h, p_i, wts, consts, comm=None):
    w_all, _, w_out, _, w_gate, _, w_ple, norm_g, b_f, qkg, ple_g = wts
    bsum, _, bias_t, rope = consts
    u, z, qa, ka, va, qb, kb, vb, logf, va_t, vb_t = _inproj(l, h, norm_g, w_all, b_f, qkg, bsum, rope, TM)
    c_spread, c_t = _cumsum_gates(l, logf, T_SCAN)
    oa, lse_a, *landed = _attn_fwd(f"fox_fwd_l{l}", True, qa, ka, va_t, (c_spread, c_t), T_FOX, comm)
    ob, lse_b = _attn_fwd(f"dil_fwd_l{l}", False, qb, kb, vb_t, (bias_t,), T_DIL)
    h2, e, gate, out = _outproj(l, h, oa, ob, z, p_i, w_out, ple_g, w_gate, w_ple, TM)
    saved = (h, u, z, qa, ka, va, qb, kb, vb, c_spread, c_t, oa, lse_a, ob, lse_b, h2, e, gate)
    return out, saved, landed


def _reduce_names(tag):
    return [f"reduce_{tag}_{w}" for w in ("w_in", "w_out", "w_ple", "w_gate")]


def _layer_bwd(l, dout, p_i, wts, consts, saved, pending=None, core=None, chip=None):
    _, w_all_t, _, w_out_t, _, w_gate_t, _, norm_g, b_f, qkg, ple_g = wts
    bsum, hsel, bias_t, rope = consts
    h, u, z, qa, ka, va, qb, kb, vb, c_spread, c_t, oa, lse_a, ob, lse_b, h2, e, gate = saved
    fused = pending is not None
    dh2, doa, dob, dga, dgb, delta_t, dw_out, dw_gate, dw_ple, dple_g, *sib = _outproj_bwd(
        l, dout, h2, e, gate, p_i, oa, ob, z, w_out_t, ple_g, w_gate_t, hsel, TM,
        _swap_comm(pending) if fused else None)
    if fused:
        pair = [_pair_sum(n, g, x, core) for n, g, x in zip(_reduce_names(f"pair_l{l + 1}"), pending, sib)]
    dqt_a, dk_a, dv_a, dc, drow, *landed = _attn_bwd(
        f"fox_bwd_l{l}", True, qa, ka, va, doa, lse_a, delta_t, 0, (c_spread, c_t), T_FOX,
        _scatter_comm(pair) if fused else None)
    if fused:
        totals = [_sum_slots(n, a, y, chip) for n, a, y in zip(_reduce_names(f"chips_l{l + 1}"), pair, landed)]
    dqt_b, dk_b, dv_b, *other = _attn_bwd(f"dil_bwd_l{l}", False, qb, kb, vb, dob, lse_b, delta_t, N_PAIRS,
                                            (bias_t,), T_DIL, _share_comm(totals) if fused else None)
    dlogf = _rev_cumsum_gates(l, dc, drow, T_SCAN)
    dz, dqkg, dbf = _inproj_bwd_prep(l, z, dqt_a, dk_a, dv_a, dqt_b, dk_b, dv_b, dga, dgb, dlogf, b_f, qkg,
                                     bsum, rope, TM)
    dh, dnorm_g = _inproj_bwd_dx(l, dz, w_all_t, h, norm_g, dh2, TM)
    dw_all = _inproj_bwd_dw(l, u, dz, TM * 2, TN_DW)
    reduced = list(zip(totals, other)) if fused else None
    return dh, (dw_all, dw_out, dw_ple, dw_gate, dnorm_g[0], dbf[0, :N_HEADS], dqkg[:4], dple_g[0]), reduced


def _reduce_last(grads, core, chip, l):
    sib = _run_comm(f"reduce_swap_l{l}", _swap_comm(grads))
    pair = [_pair_sum(n, g, x, core) for n, g, x in zip(_reduce_names(f"pair_l{l}"), grads, sib)]
    landed = _run_comm(f"reduce_scatter_l{l}", _scatter_comm(pair))
    totals = [_sum_slots(n, a, y, chip) for n, a, y in zip(_reduce_names(f"chips_l{l}"), pair, landed)]
    return list(zip(totals, _run_comm(f"reduce_share_l{l}", _share_comm(totals))))


N_FA = 2048


def _layer_weights(l, gathered, norm_g, b_f, qk_norm_g, ple_norm_g):
    g_in, g_out, g_ple, g_gate = gathered
    w_in = jnp.transpose(g_in, (1, 0, 2)).reshape(D_MODEL, N_IN)
    w_all = jnp.concatenate([w_in[:, :N_FA], w_in[:, N_FA + N_HEADS:],
                             jnp.pad(w_in[:, N_FA:N_FA + N_HEADS], ((0, 0), (0, LANE - N_HEADS)))], axis=1)
    w_out = g_out.reshape(D_MODEL, D_MODEL)
    w_gate = g_gate.reshape(D_MODEL, D_MODEL)
    w_ple = jnp.transpose(g_ple, (1, 0, 2)).reshape(PLE_DIM, D_MODEL)
    qkg = jnp.pad(jnp.tile(qk_norm_g[l], (1, N_HEADS)), ((0, 4), (0, 0)))
    bf = jnp.pad(b_f[l], (0, LANE - N_HEADS))[None, :]
    return (w_all, w_all.T, w_out, w_out.T, w_gate, w_gate.T, w_ple, norm_g[l][None, :], bf, qkg,
            ple_norm_g[l][None, :])


def _slot_layout(dw_all, dw_out, dw_ple, dw_gate):
    dw_in = jnp.concatenate([dw_all[:, :N_FA], dw_all[:, N_MAIN:N_MAIN + N_HEADS], dw_all[:, N_FA:N_MAIN]], axis=1)
    return (jnp.transpose(dw_in.reshape(D_MODEL, 4, N_IN // 4), (1, 0, 2)),
            dw_out.reshape(4, D_MODEL // 4, D_MODEL),
            jnp.transpose(dw_ple.reshape(PLE_DIM, 4, D_MODEL // 4), (1, 0, 2)),
            dw_gate.reshape(4, D_MODEL // 4, D_MODEL))


SMALL_ROWS = 40


def _pack_small(norm_g, ple_norm_g, qk_norm_g, b_f):
    flat = jnp.concatenate([norm_g.reshape(-1), ple_norm_g.reshape(-1), qk_norm_g.reshape(-1), b_f.reshape(-1)])
    return jnp.pad(flat, (0, SMALL_ROWS * LANE - flat.shape[0])).reshape(SMALL_ROWS, LANE)


def _unpack_small(packed):
    flat = packed.reshape(-1)
    n1, n2, n3 = 2 * D_MODEL, 4 * D_MODEL, 4 * D_MODEL + 2 * 4 * HEAD_DIM
    return (flat[:n1].reshape(2, D_MODEL), flat[n1:n2].reshape(2, D_MODEL), flat[n2:n3].reshape(2, 4, HEAD_DIM),
            flat[n3:n3 + 2 * N_HEADS].reshape(2, N_HEADS))


def kernel(x, p, positions, norm_g, w_in, b_f, qk_norm_g, w_out, w_ple, ple_norm_g, w_ple_gate, loss_target,
           m_norm_g, m_w_in, m_b_f, m_qk_norm_g, m_w_out, m_w_ple, m_ple_norm_g, m_w_ple_gate,
           v_norm_g, v_w_in, v_b_f, v_qk_norm_g, v_w_out, v_w_ple, v_ple_norm_g, v_w_ple_gate):
    assert w_in.shape[0] == 2, "the schedule below is written for two layers"
    *first, = _gather_first_layer([w_in, w_out, w_ple, w_ple_gate])
    consts = (_head_block_diag(), _head_select(), _dil_bias(T_DIL), _rope_tables(positions[0]))
    small_w = (norm_g, b_f, qk_norm_g, ple_norm_g)
    wts0 = _layer_weights(0, first[:4], *small_w)
    h1, saved0, second = _layer_fwd(0, x[0], p[0, 0], wts0, consts, _gather_comm(first[4:]))
    wts1 = _layer_weights(1, second, *small_w)
    h2, saved1, _ = _layer_fwd(1, h1, p[1, 0], wts1, consts)
    sq, dh = _loss_head(h2, loss_target[0], TM)
    loss = lax.psum(0.5 / D_MODEL * jnp.sum(sq), ("x", "y", "c"))

    core = lax.axis_index("c").astype(jnp.int32).reshape(1)
    chip = (2 * lax.axis_index("x") + lax.axis_index("y")).astype(jnp.int32).reshape(1)
    dh, grads1, _ = _layer_bwd(1, dh, p[1, 0], wts1, consts, saved1)
    dh, grads0, reduced1 = _layer_bwd(0, dh, p[0, 0], wts0, consts, saved0, _slot_layout(*grads1[:4]), core, chip)
    reduced0 = _reduce_last(_slot_layout(*grads0[:4]), core, chip, 0)
    grad_x = dh[None]
    small = [grads0[4:], grads1[4:]]
    n_layers = 2

    outs = {}
    for t, (name, w, m, v) in enumerate((("w_in", w_in, m_w_in, v_w_in), ("w_out", w_out, m_w_out, v_w_out),
                                         ("w_ple", w_ple, m_w_ple, v_w_ple),
                                         ("w_ple_gate", w_ple_gate, m_w_ple_gate, v_w_ple_gate))):
        outs[name] = tuple(_adamw(f"adamw_{name}", w, (reduced0[t], reduced1[t]), m, v, core))

    part = _pack_small(jnp.stack([s[0] for s in small]), jnp.stack([s[3] for s in small]),
                       jnp.stack([s[2] for s in small]).reshape(n_layers, 4, N_HEADS, HEAD_DIM).sum(axis=2),
                       jnp.stack([s[1] for s in small]))
    packed = _small_allreduce_adamw(part, _pack_small(norm_g, ple_norm_g, qk_norm_g, b_f),
                                    _pack_small(m_norm_g, m_ple_norm_g, m_qk_norm_g, m_b_f),
                                    _pack_small(v_norm_g, v_ple_norm_g, v_qk_norm_g, v_b_f))
    sm = [_unpack_small(a) for a in packed]
    for i, name in enumerate(("norm_g", "ple_norm_g", "qk_norm_g", "b_f")):
        outs[name] = tuple(sm[j][i] for j in range(4))

    order = ("norm_g", "w_in", "b_f", "qk_norm_g", "w_out", "w_ple", "ple_norm_g", "w_ple_gate")
    return (loss, grad_x) + tuple(outs[n][j] for j in range(4) for n in order)
```

```python
import functools
from typing import Any, Callable, NamedTuple, Sequence

import numpy as np
import jax
import jax.numpy as jnp
from jax import lax
from jax.experimental import pallas as pl
from jax.experimental.pallas import tpu as pltpu

F32 = jnp.float32
BF16 = jnp.bfloat16
MESH = pl.DeviceIdType.MESH

D_MODEL = 1024
HEAD_DIM = 64
D_BRANCH = 512
N_HEADS = 8
N_PAIRS = 4
N_IN = 4104
N_MAIN = 4096
N_ALL = 4224
PLE_DIM = 256
ROPE_THETA = 500000.0
ROPE_HALF = 8
EPS = 1e-6
NEG = -1e30
M_INIT = -1e29
Q_SCALE = HEAD_DIM ** -0.5
LOG2E = 1.4426950408889634
LN2 = 0.6931471805599453
DIL_PATTERNS = ((128, 1), (512, 4), (2048, 16))
DIL_BACK = 2048
ADAM_LR, ADAM_B1, ADAM_B2, ADAM_EPS, ADAM_WD, ADAM_STEP = 0.001, 0.9, 0.999, 1e-08, 0.01, 10
VMEM_LIMIT = 56 * 1024 * 1024
LANE = 128


def _dot(a, b):
    return jnp.dot(a, b, preferred_element_type=F32)


def _dot_nt(a, b):
    return lax.dot_general(a, b, (((1,), (1,)), ((), ())), preferred_element_type=F32)


def _dot_tn(a, b):
    return lax.dot_general(a, b, (((0,), (0,)), ((), ())), preferred_element_type=F32)


def _split_dot(x, w):
    hi = x.astype(BF16)
    lo = (x - hi.astype(F32)).astype(BF16)
    return _dot(hi, w) + _dot(lo, w)


def _head_sums(x, bs):
    w = bs.shape[0]
    return jnp.concatenate([_split_dot(x[:, w * k:w * (k + 1)], bs) for k in range(x.shape[1] // w)], axis=1)


def _split3_dot(w, x):
    hi = x.astype(BF16)
    r1 = x - hi.astype(F32)
    mid = r1.astype(BF16)
    lo = (r1 - mid.astype(F32)).astype(BF16)
    return _dot(w, hi) + _dot(w, mid) + _dot(w, lo)


def _sigmoid(x):
    return 1.0 / (1.0 + jnp.exp(-x))


def _params(n_grid):
    return pltpu.CompilerParams(dimension_semantics=("arbitrary",) * n_grid,
                                vmem_limit_bytes=VMEM_LIMIT)


def _full(shape):
    nd = len(shape)
    return pl.BlockSpec(shape, lambda *_: (0,) * nd)


ANY = pl.BlockSpec(memory_space=pl.ANY)
VMEM_SPEC = pl.BlockSpec(memory_space=pltpu.VMEM)


class _Comm(NamedTuple):
    ins: Sequence[Any]
    out_shapes: Sequence[Any]
    sems: Sequence[Any]
    make: Callable[..., Any]


def _fuse_comm(body, n_in, n_out, comm, grid):
    if comm is None:
        return body
    nci, nco, ncs = len(comm.ins), len(comm.out_shapes), len(comm.sems)

    def fused(*refs):
        a, b = n_in + nci, n_in + nci + n_out
        ins, cins, outs, couts = refs[:n_in], refs[n_in:a], refs[a:b], refs[b:b + nco]
        scratch, sems = refs[b + nco:len(refs) - ncs], refs[len(refs) - ncs:]
        first = functools.reduce(jnp.logical_and, [pl.program_id(d) == 0 for d in range(len(grid))])
        last = functools.reduce(jnp.logical_and, [pl.program_id(d) == n - 1 for d, n in enumerate(grid)])

        @pl.when(first)
        def _():
            for cp in comm.make(cins, couts, sems):
                cp.start()

        body(*ins, *outs, *scratch)

        @pl.when(last)
        def _():
            for cp in comm.make(cins, couts, sems):
                cp.wait()

    return fused


def _comm_args(comm):
    if comm is None:
        return [], [], [], [], []
    return (list(comm.ins), [ANY] * len(comm.ins), [ANY] * len(comm.out_shapes), list(comm.out_shapes),
            list(comm.sems))


HEADS_PER_BLOCK = 4


def _head_block_diag():
    i = np.arange(HEADS_PER_BLOCK * HEAD_DIM)
    return jnp.asarray((i[:, None] // HEAD_DIM == i[None, :] // HEAD_DIM).astype(np.float32), BF16)


def _head_select():
    i = np.arange(2 * D_BRANCH)
    j = np.arange(LANE)
    return jnp.asarray((i[:, None] // HEAD_DIM == j[None, :]).astype(np.float32), BF16)


def _dil_bias(t):
    nb = DIL_BACK // t + 1
    qi = np.arange(t)[:, None]
    ki = np.arange(t)[None, :]
    tiles = []
    for r in range(nb):
        d = r * t + qi - ki
        mult = np.zeros((t, t), np.int64)
        for window, dil in DIL_PATTERNS:
            mult += ((d >= 0) & (d <= window) & (d % dil == 0)).astype(np.int64)
        b = np.where(mult > 0, np.log2(np.maximum(mult, 1)), NEG).astype(np.float32)
        tiles.append(b.T)
    return jnp.asarray(np.stack(tiles))


def _rope_tables(positions):
    inv_freq = ROPE_THETA ** (-jnp.arange(ROPE_HALF, dtype=F32) / ROPE_HALF)
    ang = positions.astype(F32)[:, None] * inv_freq
    cos, sin = jnp.cos(ang), jnp.sin(ang)
    s = positions.shape[0]
    rest = HEAD_DIM - 2 * ROPE_HALF
    one, zero, zero8 = jnp.ones((s, rest), F32), jnp.zeros((s, rest), F32), jnp.zeros((s, ROPE_HALF), F32)
    c = jnp.concatenate([cos, cos, one], axis=1)
    s1 = jnp.concatenate([zero8, sin, zero], axis=1)
    s2 = jnp.concatenate([-sin, zero8, zero], axis=1)
    return tuple(jnp.tile(t, (1, 2)) for t in (c, s1, s2))


def _rope_fwd(x, c, s1, s2):
    return x * c + pltpu.roll(x, ROPE_HALF, 1) * s1 + pltpu.roll(x, LANE - ROPE_HALF, 1) * s2


def _rope_bwd(dy, c, s1, s2):
    return dy * c + pltpu.roll(dy * s1, LANE - ROPE_HALF, 1) + pltpu.roll(dy * s2, ROPE_HALF, 1)


def _log_sigmoid(x):
    return jnp.minimum(x, 0.0) - jnp.log(1.0 + jnp.exp(-jnp.abs(x)))


def _inproj(l, h, norm_g, w_all, b_f, qkg, bsum, rope, tm):
    s = h.shape[0]
    rc, rs1, rs2 = rope

    def body(h_ref, g_ref, w_ref, bf_ref, qkg_ref, bsum_ref, rc_ref, rs1_ref, rs2_ref,
             u_ref, z_ref, qa_ref, ka_ref, va_ref, qb_ref, kb_ref, vb_ref, lf_ref, vat_ref, vbt_ref):
        hh = h_ref[...]
        r = lax.rsqrt(jnp.mean(hh * hh, axis=-1, keepdims=True) + EPS)
        u = (hh * r * g_ref[...]).astype(BF16)
        u_ref[...] = u
        for k in range(N_ALL // LANE // 3):
            cols = slice(3 * LANE * k, 3 * LANE * (k + 1))
            z_ref[:, cols] = _dot(u, w_ref[:, cols])
        bs = bsum_ref[...]

        def head_norm(x, row):
            ms = _head_sums(x * x, bs) * (1.0 / HEAD_DIM)
            return x * lax.rsqrt(ms + EPS) * qkg_ref[row:row + 1, :]

        def seg(k):
            return z_ref[:, D_BRANCH * k:D_BRANCH * (k + 1)]

        qa_ref[...] = (head_norm(seg(0), 0) * (Q_SCALE * LOG2E)).astype(BF16)
        ka_ref[...] = head_norm(seg(1), 1).astype(BF16)
        va_ref[...] = seg(2).astype(BF16)
        vat_ref[...] = seg(2).T.astype(BF16)
        qn = head_norm(seg(4), 2) * (Q_SCALE * LOG2E)
        kn = head_norm(seg(5), 3)
        c, s1, s2 = rc_ref[...], rs1_ref[...], rs2_ref[...]
        for k in range(D_BRANCH // LANE):
            cols = slice(LANE * k, LANE * (k + 1))
            qb_ref[:, cols] = _rope_fwd(qn[:, cols], c, s1, s2).astype(BF16)
            kb_ref[:, cols] = _rope_fwd(kn[:, cols], c, s1, s2).astype(BF16)
        vb_ref[...] = seg(6).astype(BF16)
        vbt_ref[...] = seg(6).T.astype(BF16)
        lf_ref[...] = _log_sigmoid(z_ref[:, N_MAIN:N_ALL] + bf_ref[...])

    row = lambda w: pl.BlockSpec((tm, w), lambda i: (i, 0))
    colt = pl.BlockSpec((D_BRANCH, tm), lambda i: (0, i))
    bf = lambda: jax.ShapeDtypeStruct((s, D_BRANCH), BF16)
    bft = lambda: jax.ShapeDtypeStruct((D_BRANCH, s), BF16)
    return pl.pallas_call(
        body, name=f"inproj_l{l}", grid=(s // tm,),
        in_specs=[row(D_MODEL), _full((1, D_MODEL)), _full((D_MODEL, N_ALL)), _full((1, LANE)),
                  _full((8, D_BRANCH)), _full((HEADS_PER_BLOCK * HEAD_DIM,) * 2), row(LANE), row(LANE), row(LANE)],
        out_specs=[row(D_MODEL), row(N_ALL)] + [row(D_BRANCH)] * 6 + [row(LANE), colt, colt],
        out_shape=[jax.ShapeDtypeStruct((s, D_MODEL), BF16), jax.ShapeDtypeStruct((s, N_ALL), F32),
                   bf(), bf(), bf(), bf(), bf(), bf(), jax.ShapeDtypeStruct((s, LANE), F32), bft(), bft()],
        compiler_params=_params(1),
    )(h, norm_g, w_all, b_f, qkg, bsum, rc, rs1, rs2)


def _tri(t, upper):
    a = lax.broadcasted_iota(jnp.int32, (t, t), 0)
    b = lax.broadcasted_iota(jnp.int32, (t, t), 1)
    return jnp.where((b >= a) if upper else (b <= a), 1.0, 0.0).astype(BF16)


def _cumsum_gates(l, logf, t):
    s = logf.shape[0]

    def body(lf_ref, cs_ref, ct_ref, carry):
        @pl.when(pl.program_id(0) == 0)
        def _():
            carry[...] = jnp.zeros_like(carry)

        x = lf_ref[...]
        c = _split3_dot(_tri(t, False), x) + carry[0:1, :]
        carry[...] = jnp.broadcast_to(c[t - 1:t, :], carry.shape)
        c = c * LOG2E
        ct = c.T
        for p in range(N_PAIRS):
            cs_ref[:, LANE * p:LANE * (p + 1)] = c if p == 0 else pltpu.roll(c, LANE - 2 * p, 1)
            ct_ref[p, :, :] = ct[2 * p:2 * p + 2, :]

    return pl.pallas_call(
        body, name=f"cumsum_l{l}", grid=(s // t,),
        in_specs=[pl.BlockSpec((t, LANE), lambda i: (i, 0))],
        out_specs=[pl.BlockSpec((t, N_PAIRS * LANE), lambda i: (i, 0)),
                   pl.BlockSpec((N_PAIRS, 2, t), lambda i: (0, 0, i))],
        out_shape=[jax.ShapeDtypeStruct((s, N_PAIRS * LANE), F32), jax.ShapeDtypeStruct((N_PAIRS, 2, s), F32)],
        scratch_shapes=[pltpu.VMEM((8, LANE), F32)],
        compiler_params=_params(1),
    )(logf)


def _rev_cumsum_gates(l, dc_spread, drow, t):
    s = dc_spread.shape[0]
    n = s // t

    def body(dc_ref, drow_ref, out_ref, carry):
        @pl.when(pl.program_id(0) == 0)
        def _():
            carry[...] = jnp.zeros_like(carry)

        lane = lax.broadcasted_iota(jnp.int32, (t, LANE), 1)
        rows = jnp.concatenate([drow_ref[p] for p in range(N_PAIRS)] + [jnp.zeros((LANE - N_HEADS, t), F32)], axis=0)
        x = rows.T
        for p in range(N_PAIRS):
            xp = jnp.where(lane < 2, dc_ref[:, LANE * p:LANE * (p + 1)], 0.0)
            x = x + (xp if p == 0 else pltpu.roll(xp, 2 * p, 1))
        out = _split3_dot(_tri(t, True), x) + carry[0:1, :]
        out_ref[...] = out
        carry[...] = jnp.broadcast_to(out[0:1, :], carry.shape)

    return pl.pallas_call(
        body, name=f"revcumsum_l{l}", grid=(n,),
        in_specs=[pl.BlockSpec((t, N_PAIRS * LANE), lambda i: (n - 1 - i, 0)),
                  pl.BlockSpec((N_PAIRS, 2, t), lambda i: (0, 0, n - 1 - i))],
        out_specs=pl.BlockSpec((t, LANE), lambda i: (n - 1 - i, 0)),
        out_shape=jax.ShapeDtypeStruct((s, LANE), F32),
        scratch_shapes=[pltpu.VMEM((8, LANE), F32)],
        compiler_params=_params(1),
    )(dc_spread, drow)


def _attn_fwd(name, fox, q, k, vt, extra, t, comm=None):
    s = q.shape[0]
    nq = s // t
    nb = DIL_BACK // t + 1

    def body(*refs):
        if fox:
            q_ref, k_ref, vt_ref, ccol_ref, crow_ref, o_ref, lse_ref, m_scr, l_scr, acc_scr = refs
        else:
            q_ref, k_ref, vt_ref, bias_ref, o_ref, lse_ref, m_scr, l_scr, acc_scr = refs
        i = pl.program_id(1)
        lane = lax.broadcasted_iota(jnp.int32, (t, LANE), 1)
        first = lane < HEAD_DIM
        qq = q_ref[...]
        zero = jnp.zeros_like(qq)
        qh = (jnp.where(first, qq, zero), jnp.where(first, zero, qq))
        m_scr[...] = jnp.full(m_scr.shape, M_INIT, F32)
        l_scr[...] = jnp.zeros_like(l_scr)
        acc_scr[...] = jnp.zeros_like(acc_scr)
        ones = jnp.ones((16, t), BF16)

        half = t // 2
        whole, lo, hi = slice(0, t), slice(0, half), slice(half, t)

        def block(j, ksl, qsl, causal):
            nk_, nq_ = ksl.stop - ksl.start, qsl.stop - qsl.start
            rows = pl.ds(pl.multiple_of(j * t + ksl.start, LANE), nk_)
            ks = k_ref[rows, :]
            vts = jnp.concatenate([vt_ref[:, rows], ones[:, :nk_]], axis=0)
            if fox:
                ccol = ccol_ref[rows, :]
            for h in range(2):
                st = _dot_nt(ks, qh[h][qsl, :])
                if fox:
                    st = st + (crow_ref[h:h + 1, qsl] - ccol[:, h:h + 1])
                    if causal:
                        ki = lax.broadcasted_iota(jnp.int32, (nk_, nq_), 0) + ksl.start
                        qi = lax.broadcasted_iota(jnp.int32, (nk_, nq_), 1) + qsl.start
                        st = jnp.where(ki <= qi, st, NEG)
                else:
                    st = st + bias_ref[i - j, ksl, qsl]
                m_old = m_scr[h, :, qsl]
                m_new = jnp.maximum(m_old, jnp.max(st, axis=0, keepdims=True))
                alpha = jnp.exp2(m_old - m_new)
                pb = jnp.exp2(st - m_new).astype(BF16)
                pv = _dot(vts, pb)
                l_scr[h, :, qsl] = alpha * l_scr[h, :, qsl] + pv[LANE:LANE + 1, :]
                acc_scr[h, :, qsl] = alpha * acc_scr[h, :, qsl] + pv[:LANE, :]
                m_scr[h, :, qsl] = m_new

        def full(j, c):
            block(j, whole, whole, False)
            return c

        if fox:
            lax.fori_loop(0, i, full, 0)
        else:
            @pl.when(i >= nb - 1)
            def _():
                block(i - (nb - 1), lo, lo, False)
                block(i - (nb - 1), hi, whole, False)

            lax.fori_loop(jnp.maximum(i - (nb - 2), 0), i, full, 0)
        block(i, lo, lo, True)
        block(i, whole, hi, True)

        sub = lax.broadcasted_iota(jnp.int32, (LANE, t), 0)
        ot = jnp.where(sub < HEAD_DIM, acc_scr[0] / l_scr[0], acc_scr[1] / l_scr[1])
        o_ref[...] = ot.T
        for h in range(2):
            lse_ref[h:h + 1, :] = m_scr[h] + jnp.log2(l_scr[h])

    qspec = pl.BlockSpec((t, LANE), lambda hp, i: (i, hp))
    kspec = pl.BlockSpec((s, LANE), lambda hp, i: (0, hp))
    vtspec = pl.BlockSpec((LANE, s), lambda hp, i: (hp, 0))
    in_specs = [qspec, kspec, vtspec]
    if fox:
        in_specs += [kspec, pl.BlockSpec((None, 2, t), lambda hp, i: (hp, 0, i))]
    else:
        in_specs += [_full((nb, t, t))]
    grid = (N_PAIRS, nq)
    c_in, c_ispec, c_ospec, c_oshape, c_scr = _comm_args(comm)
    return pl.pallas_call(
        _fuse_comm(body, len(in_specs), 2, comm, grid), name=name, grid=grid,
        in_specs=in_specs + c_ispec,
        out_specs=[qspec, pl.BlockSpec((None, 2, t), lambda hp, i: (hp, 0, i))] + c_ospec,
        out_shape=[jax.ShapeDtypeStruct((s, D_BRANCH), F32), jax.ShapeDtypeStruct((N_PAIRS, 2, s), F32)] + c_oshape,
        scratch_shapes=[pltpu.VMEM((2, 1, t), F32), pltpu.VMEM((2, 1, t), F32), pltpu.VMEM((2, LANE, t), F32)]
        + c_scr,
        compiler_params=_params(2),
    )(q, k, vt, *extra, *c_in)


def _attn_bwd(name, fox, q, k, v, do, lse_t, delta_t, pair_offset, extra, t, comm=None):
    s = q.shape[0]
    nk = s // t
    nb = DIL_BACK // t + 1

    def body(*refs):
        if fox:
            (q_ref, k_ref, v_ref, do_ref, lse_ref, delta_ref, ccol_ref, crow_ref,
             dqt_ref, dk_ref, dv_ref, dc_ref, drow_ref) = refs
        else:
            q_ref, k_ref, v_ref, do_ref, lse_ref, delta_ref, bias_ref, dqt_ref, dk_ref, dv_ref = refs
        j = pl.program_id(1)

        @pl.when(j == 0)
        def _():
            dqt_ref[...] = jnp.zeros_like(dqt_ref)
            if fox:
                drow_ref[...] = jnp.zeros_like(drow_ref)

        lane = lax.broadcasted_iota(jnp.int32, (t, LANE), 1)
        first = lane < HEAD_DIM
        ks = k_ref[...]
        vs = v_ref[...]
        kt = ks.astype(F32).T
        sub = lax.broadcasted_iota(jnp.int32, (LANE, t), 0)
        kth = (jnp.where(sub < HEAD_DIM, kt, 0.0).astype(BF16), jnp.where(sub < HEAD_DIM, 0.0, kt).astype(BF16))
        dk_ref[...] = jnp.zeros_like(dk_ref)
        dv_ref[...] = jnp.zeros_like(dv_ref)
        if fox:
            dc_ref[...] = jnp.zeros_like(dc_ref)
            ccol = ccol_ref[...]

        half = t // 2
        whole, lo, hi = slice(0, t), slice(0, half), slice(half, t)

        def block(i, ksl, qsl, causal):
            nk_, nq_ = ksl.stop - ksl.start, qsl.stop - qsl.start
            rows = pl.ds(pl.multiple_of(i * t + qsl.start, LANE), nq_)
            qq = q_ref[rows, :]
            dd = do_ref[rows, :]
            zero = jnp.zeros_like(qq)
            qh = (jnp.where(first[:nq_], qq, zero), jnp.where(first[:nq_], zero, qq))
            dh = (jnp.where(first[:nq_], dd, zero), jnp.where(first[:nq_], zero, dd))
            for h in range(2):
                st = _dot_nt(ks[ksl, :], qh[h])
                if fox:
                    st = st + (crow_ref[h:h + 1, rows] - ccol[ksl, h:h + 1])
                    if causal:
                        ki = lax.broadcasted_iota(jnp.int32, (nk_, nq_), 0) + ksl.start
                        qi = lax.broadcasted_iota(jnp.int32, (nk_, nq_), 1) + qsl.start
                        st = jnp.where(ki <= qi, st, NEG)
                else:
                    st = st + bias_ref[i - j, ksl, qsl]
                pt = jnp.exp2(st - lse_ref[h:h + 1, rows])
                dpt = _dot_nt(vs[ksl, :], dh[h])
                dst = pt * (dpt - delta_ref[h:h + 1, rows])
                dv_ref[ksl, :] += _dot(pt.astype(BF16), dh[h])
                dsb = dst.astype(BF16)
                dk_ref[ksl, :] += _dot(dsb, qh[h])
                dqt_ref[:, rows] += _dot(kth[h][:, ksl], dsb)
                if fox:
                    dc_ref[ksl, :] -= jnp.where(lane[:nk_] == h, jnp.sum(dst, axis=1, keepdims=True), 0.0)
                    drow_ref[h:h + 1, rows] += jnp.sum(dst, axis=0, keepdims=True)

        def full(i, c):
            block(i, whole, whole, False)
            return c

        block(j, lo, whole, True)
        block(j, hi, hi, True)
        if fox:
            lax.fori_loop(j + 1, nk, full, 0)
        else:
            lax.fori_loop(j + 1, jnp.minimum(j + nb - 1, nk), full, 0)

            @pl.when(j + nb - 1 < nk)
            def _():
                block(j + nb - 1, lo, lo, False)
                block(j + nb - 1, hi, whole, False)

    kspec = pl.BlockSpec((t, LANE), lambda hp, j: (j, hp))
    qspec = pl.BlockSpec((s, LANE), lambda hp, j: (0, hp))
    rowspec = pl.BlockSpec((None, 2, s), lambda hp, j: (hp, 0, 0))
    drowspec = pl.BlockSpec((None, 2, s), lambda hp, j: (hp + pair_offset, 0, 0))
    in_specs = [qspec, kspec, kspec, qspec, rowspec, drowspec]
    out_specs = [pl.BlockSpec((LANE, s), lambda hp, j: (hp, 0)), kspec, kspec]
    out_shape = [jax.ShapeDtypeStruct((D_BRANCH, s), F32), jax.ShapeDtypeStruct((s, D_BRANCH), F32),
                 jax.ShapeDtypeStruct((s, D_BRANCH), F32)]
    if fox:
        in_specs += [kspec, rowspec]
        out_specs += [kspec, rowspec]
        out_shape += [jax.ShapeDtypeStruct((s, N_PAIRS * LANE), F32), jax.ShapeDtypeStruct((N_PAIRS, 2, s), F32)]
    else:
        in_specs += [_full((nb, t, t))]
    grid = (N_PAIRS, nk)
    c_in, c_ispec, c_ospec, c_oshape, c_scr = _comm_args(comm)
    return pl.pallas_call(
        _fuse_comm(body, len(in_specs), len(out_specs), comm, grid), name=name, grid=grid,
        in_specs=in_specs + c_ispec, out_specs=out_specs + c_ospec, out_shape=out_shape + c_oshape,
        scratch_shapes=c_scr, compiler_params=_params(2),
    )(q, k, v, do, lse_t, delta_t, *extra, *c_in)


def _silu(x):
    return x * _sigmoid(x)


def _outproj(l, h, oa, ob, z, p_i, w_out, ple_g, w_gate, w_ple, tm):
    s = h.shape[0]

    def body(h_ref, oa_ref, ob_ref, ga_ref, gb_ref, p_ref, wo_ref, pg_ref, wg_ref, wp_ref,
             h2_ref, e_ref, gate_ref, out_ref):
        a = jnp.concatenate([oa_ref[...] * _silu(ga_ref[...]), ob_ref[...] * _silu(gb_ref[...])], axis=1)
        h2 = h_ref[...] + _dot(a.astype(BF16), wo_ref[...])
        h2_ref[...] = h2
        r = lax.rsqrt(jnp.mean(h2 * h2, axis=-1, keepdims=True) + EPS)
        n2 = (h2 * r * pg_ref[...]).astype(BF16)
        gate = _sigmoid(_dot(n2, wg_ref[...]))
        e = _dot(p_ref[...].astype(BF16), wp_ref[...])
        e_ref[...] = e
        gate_ref[...] = gate
        out_ref[...] = h2 + e * gate

    row = lambda w: pl.BlockSpec((tm, w), lambda i: (i, 0))
    zcol = lambda k: pl.BlockSpec((tm, D_BRANCH), lambda i: (i, k))
    f = lambda: jax.ShapeDtypeStruct((s, D_MODEL), F32)
    return pl.pallas_call(
        body, name=f"outproj_l{l}", grid=(s // tm,),
        in_specs=[row(D_MODEL), row(D_BRANCH), row(D_BRANCH), zcol(3), zcol(7), row(PLE_DIM),
                  _full((D_MODEL, D_MODEL)), _full((1, D_MODEL)), _full((D_MODEL, D_MODEL)),
                  _full((PLE_DIM, D_MODEL))],
        out_specs=[row(D_MODEL)] * 4, out_shape=[f(), f(), f(), f()],
        compiler_params=_params(1),
    )(h, oa, ob, z, z, p_i, w_out, ple_g, w_gate, w_ple)


def _loss_head(y, target, tm):
    s = y.shape[0]

    def body(y_ref, t_ref, acc_ref, dy_ref):
        @pl.when(pl.program_id(0) == 0)
        def _():
            acc_ref[...] = jnp.zeros_like(acc_ref)

        err = y_ref[...] - t_ref[...]
        dy_ref[...] = err * (1.0 / D_MODEL)
        e2 = err * err
        part = jnp.zeros((8, LANE), F32)
        for r in range(tm // 8):
            for c in range(D_MODEL // LANE):
                part = part + e2[8 * r:8 * (r + 1), LANE * c:LANE * (c + 1)]
        acc_ref[...] += part

    row = pl.BlockSpec((tm, D_MODEL), lambda i: (i, 0))
    return pl.pallas_call(
        body, name="loss_head", grid=(s // tm,), in_specs=[row, row],
        out_specs=[_full((8, LANE)), row],
        out_shape=[jax.ShapeDtypeStruct((8, LANE), F32), jax.ShapeDtypeStruct((s, D_MODEL), F32)],
        compiler_params=_params(1),
    )(y, target)


def _outproj_bwd(l, dout, h2, e, gate, p_i, oa, ob, z, w_out_t, ple_g, w_gate_t, hsel, tm, comm=None):
    s = dout.shape[0]

    def body(do_ref, h2_ref, e_ref, gate_ref, p_ref, oa_ref, ob_ref, ga_ref, gb_ref, wot_ref, pg_ref,
             wgt_ref, hsel_ref,
             dh2_ref, doa_ref, dob_ref, dga_ref, dgb_ref, delta_ref, dwo_ref, dwg_ref, dwp_ref, dpg_ref):
        @pl.when(pl.program_id(0) == 0)
        def _():
            dwo_ref[...] = jnp.zeros_like(dwo_ref)
            dwg_ref[...] = jnp.zeros_like(dwg_ref)
            dwp_ref[...] = jnp.zeros_like(dwp_ref)
            dpg_ref[...] = jnp.zeros_like(dpg_ref)

        dho = do_ref[...]
        g = gate_ref[...]
        de = (dho * g).astype(BF16)
        dwp_ref[...] += _dot_tn(p_ref[...].astype(BF16), de)
        dpre = (dho * e_ref[...] * g * (1.0 - g)).astype(BF16)
        h2 = h2_ref[...]
        pg = pg_ref[...]
        r = lax.rsqrt(jnp.mean(h2 * h2, axis=-1, keepdims=True) + EPS)
        n2 = (h2 * r * pg).astype(BF16)
        dwg_ref[...] += _dot_tn(n2, dpre)
        dn2 = _dot(dpre, wgt_ref[...])
        dpg_ref[0:1, :] += jnp.sum(dn2 * h2 * r, axis=0, keepdims=True)
        wv = dn2 * pg
        dh2 = dho + r * wv - h2 * (r * r * r) * jnp.mean(wv * h2, axis=-1, keepdims=True)
        dh2_ref[...] = dh2
        dh2b = dh2.astype(BF16)
        ga, gb, oa, ob = ga_ref[...], gb_ref[...], oa_ref[...], ob_ref[...]
        sga, sgb = _sigmoid(ga), _sigmoid(gb)
        a = jnp.concatenate([oa * ga * sga, ob * gb * sgb], axis=1).astype(BF16)
        dwo_ref[...] += _dot_tn(a, dh2b)
        da = _dot(dh2b, wot_ref[...])
        da_a, da_b = da[:, :D_BRANCH], da[:, D_BRANCH:]
        doa = da_a * ga * sga
        dob = da_b * gb * sgb
        doa_ref[...] = doa.astype(BF16)
        dob_ref[...] = dob.astype(BF16)
        dga_ref[...] = (da_a * oa * sga * (1.0 + ga * (1.0 - sga))).astype(BF16)
        dgb_ref[...] = (da_b * ob * sgb * (1.0 + gb * (1.0 - sgb))).astype(BF16)
        prod = jnp.concatenate([doa * oa, dob * ob], axis=1)
        dt = _split_dot(prod, hsel_ref[...]).T
        for pp in range(2 * N_PAIRS):
            delta_ref[pp, :, :] = dt[2 * pp:2 * pp + 2, :]

    row = lambda w: pl.BlockSpec((tm, w), lambda i: (i, 0))
    zcol = lambda k: pl.BlockSpec((tm, D_BRANCH), lambda i: (i, k))
    grid = (s // tm,)
    c_in, c_ispec, c_ospec, c_oshape, c_scr = _comm_args(comm)
    return pl.pallas_call(
        _fuse_comm(body, 13, 10, comm, grid), name=f"outproj_bwd_l{l}", grid=grid,
        in_specs=[row(D_MODEL)] * 4 + [row(PLE_DIM), row(D_BRANCH), row(D_BRANCH), zcol(3), zcol(7),
                                        _full((D_MODEL, D_MODEL)), _full((1, D_MODEL)), _full((D_MODEL, D_MODEL)),
                                        _full((2 * D_BRANCH, LANE))] + c_ispec,
        out_specs=[row(D_MODEL)] + [row(D_BRANCH)] * 4
        + [pl.BlockSpec((2 * N_PAIRS, 2, tm), lambda i: (0, 0, i)), _full((D_MODEL, D_MODEL)),
           _full((D_MODEL, D_MODEL)), _full((PLE_DIM, D_MODEL)), _full((8, D_MODEL))] + c_ospec,
        out_shape=[jax.ShapeDtypeStruct((s, D_MODEL), F32)] + [jax.ShapeDtypeStruct((s, D_BRANCH), BF16)] * 4
        + [jax.ShapeDtypeStruct((2 * N_PAIRS, 2, s), F32), jax.ShapeDtypeStruct((D_MODEL, D_MODEL), F32),
           jax.ShapeDtypeStruct((D_MODEL, D_MODEL), F32), jax.ShapeDtypeStruct((PLE_DIM, D_MODEL), F32),
           jax.ShapeDtypeStruct((8, D_MODEL), F32)] + c_oshape,
        scratch_shapes=c_scr, compiler_params=_params(1),
    )(dout, h2, e, gate, p_i, oa, ob, z, z, w_out_t, ple_g, w_gate_t, hsel, *c_in)


def _inproj_bwd_prep(l, z, dqt_a, dk_a, dv_a, dqt_b, dk_b, dv_b, dga, dgb, dlogf, b_f, qkg, bsum, rope, tm):
    s = z.shape[0]
    rc, rs1, rs2 = rope

    def body(z_ref, dqta_ref, dka_ref, dva_ref, dqtb_ref, dkb_ref, dvb_ref, dga_ref, dgb_ref, dlf_ref,
             bf_ref, qkg_ref, bsum_ref, rc_ref, rs1_ref, rs2_ref, dz_ref, dqkg_ref, dbf_ref):
        @pl.when(pl.program_id(0) == 0)
        def _():
            dqkg_ref[...] = jnp.zeros_like(dqkg_ref)
            dbf_ref[...] = jnp.zeros_like(dbf_ref)

        bs = bsum_ref[...]
        c, s1, s2 = rc_ref[...], rs1_ref[...], rs2_ref[...]

        def unrope(dy):
            return jnp.concatenate([_rope_bwd(dy[:, LANE * k:LANE * (k + 1)], c, s1, s2)
                                    for k in range(D_BRANCH // LANE)], axis=1)

        def norm_bwd(k, row, dy):
            x = z_ref[:, D_BRANCH * k:D_BRANCH * (k + 1)]
            r = lax.rsqrt(_head_sums(x * x, bs) * (1.0 / HEAD_DIM) + EPS)
            dqkg_ref[row:row + 1, :] += jnp.sum(dy * x * r, axis=0, keepdims=True)
            w = dy * qkg_ref[row:row + 1, :]
            dx = r * w - x * (r * r * r) * (_head_sums(w * x, bs) * (1.0 / HEAD_DIM))
            dz_ref[:, D_BRANCH * k:D_BRANCH * (k + 1)] = dx.astype(BF16)

        norm_bwd(0, 0, dqta_ref[...].T * Q_SCALE)
        norm_bwd(1, 1, dka_ref[...] * LN2)
        dz_ref[:, 2 * D_BRANCH:3 * D_BRANCH] = dva_ref[...].astype(BF16)
        dz_ref[:, 3 * D_BRANCH:4 * D_BRANCH] = dga_ref[...]
        norm_bwd(4, 2, unrope(dqtb_ref[...].T * Q_SCALE))
        norm_bwd(5, 3, unrope(dkb_ref[...] * LN2))
        dz_ref[:, 6 * D_BRANCH:7 * D_BRANCH] = dvb_ref[...].astype(BF16)
        dz_ref[:, 7 * D_BRANCH:8 * D_BRANCH] = dgb_ref[...]
        dfa = dlf_ref[...] * _sigmoid(-(z_ref[:, N_MAIN:N_ALL] + bf_ref[...]))
        dz_ref[:, N_MAIN:N_ALL] = dfa.astype(BF16)
        dbf_ref[0:1, :] += jnp.sum(dfa, axis=0, keepdims=True)

    row = lambda w: pl.BlockSpec((tm, w), lambda i: (i, 0))
    colt = pl.BlockSpec((D_BRANCH, tm), lambda i: (0, i))
    return pl.pallas_call(
        body, name=f"inproj_bwd_prep_l{l}", grid=(s // tm,),
        in_specs=[row(N_ALL), colt, row(D_BRANCH), row(D_BRANCH), colt, row(D_BRANCH), row(D_BRANCH),
                  row(D_BRANCH), row(D_BRANCH), row(LANE), _full((1, LANE)), _full((8, D_BRANCH)),
                  _full((HEADS_PER_BLOCK * HEAD_DIM,) * 2), row(LANE), row(LANE), row(LANE)],
        out_specs=[row(N_ALL), _full((8, D_BRANCH)), _full((8, LANE))],
        out_shape=[jax.ShapeDtypeStruct((s, N_ALL), BF16), jax.ShapeDtypeStruct((8, D_BRANCH), F32),
                   jax.ShapeDtypeStruct((8, LANE), F32)],
        compiler_params=_params(1),
    )(z, dqt_a, dk_a, dv_a, dqt_b, dk_b, dv_b, dga, dgb, dlogf, b_f, qkg, bsum, rc, rs1, rs2)


def _inproj_bwd_dx(l, dz, w_all_t, h, norm_g, dh2, tm):
    s = dz.shape[0]

    def body(dz_ref, wt_ref, h_ref, g_ref, dh2_ref, dh_ref, dg_ref):
        @pl.when(pl.program_id(0) == 0)
        def _():
            dg_ref[...] = jnp.zeros_like(dg_ref)

        du = _dot(dz_ref[...], wt_ref[...])
        hh = h_ref[...]
        g = g_ref[...]
        r = lax.rsqrt(jnp.mean(hh * hh, axis=-1, keepdims=True) + EPS)
        dg_ref[0:1, :] += jnp.sum(du * hh * r, axis=0, keepdims=True)
        wv = du * g
        dh_ref[...] = dh2_ref[...] + r * wv - hh * (r * r * r) * jnp.mean(wv * hh, axis=-1, keepdims=True)

    row = lambda w: pl.BlockSpec((tm, w), lambda i: (i, 0))
    return pl.pallas_call(
        body, name=f"inproj_bwd_dx_l{l}", grid=(s // tm,),
        in_specs=[row(N_ALL), _full((N_ALL, D_MODEL)), row(D_MODEL), _full((1, D_MODEL)), row(D_MODEL)],
        out_specs=[row(D_MODEL), _full((8, D_MODEL))],
        out_shape=[jax.ShapeDtypeStruct((s, D_MODEL), F32), jax.ShapeDtypeStruct((8, D_MODEL), F32)],
        compiler_params=_params(1),
    )(dz, w_all_t, h, norm_g, dh2)


def _inproj_bwd_dw(l, u, dz, tm, tn):
    s = u.shape[0]

    def body(u_ref, dz_ref, dw_ref):
        @pl.when(pl.program_id(1) == 0)
        def _():
            dw_ref[...] = jnp.zeros_like(dw_ref)

        dw_ref[...] += _dot_tn(u_ref[...], dz_ref[...])

    return pl.pallas_call(
        body, name=f"inproj_bwd_dw_l{l}", grid=(N_ALL // tn, s // tm),
        in_specs=[pl.BlockSpec((tm, D_MODEL), lambda n, i: (i, 0)), pl.BlockSpec((tm, tn), lambda n, i: (i, n))],
        out_specs=pl.BlockSpec((D_MODEL, tn), lambda n, i: (0, n)),
        out_shape=jax.ShapeDtypeStruct((D_MODEL, N_ALL), F32),
        compiler_params=_params(2),
    )(u, dz)


def _adamw_math(w, g, m, v):
    m = ADAM_B1 * m + (1.0 - ADAM_B1) * g
    v = ADAM_B2 * v + (1.0 - ADAM_B2) * (g * g)
    m_hat = m / (1.0 - ADAM_B1 ** ADAM_STEP)
    v_hat = v / (1.0 - ADAM_B2 ** ADAM_STEP)
    delta = -ADAM_LR * (m_hat / (jnp.sqrt(v_hat) + ADAM_EPS) + ADAM_WD * w)
    return delta, m, v


def _adamw(name, w, halves, m, v, core):
    nl, r, c = w.shape
    hr = r // 2
    tr = 128 if hr % 128 == 0 else hr
    nb = hr // tr

    def body(core_ref, w_ref, own0_ref, oth0_ref, own1_ref, oth1_ref, m_ref, v_ref, g_ref, d_ref, nm_ref, nv_ref):
        first = pl.program_id(0) == 0
        own = jnp.where(first, own0_ref[...], own1_ref[...])
        oth = jnp.where(first, oth0_ref[...], oth1_ref[...])
        g = jnp.where(pl.program_id(1) // nb == core_ref[0], own, oth)
        d, nm, nv = _adamw_math(w_ref[...], g, m_ref[...], v_ref[...])
        g_ref[...] = g
        d_ref[...] = d
        nm_ref[...] = nm
        nv_ref[...] = nv

    spec = pl.BlockSpec((None, tr, c), lambda a, b, core_ref: (a, b, 0))
    gspec = pl.BlockSpec((tr, c), lambda a, b, core_ref: (b % nb, 0))
    shp = jax.ShapeDtypeStruct(w.shape, F32)
    return pl.pallas_call(
        body, name=name,
        grid_spec=pltpu.PrefetchScalarGridSpec(
            num_scalar_prefetch=1, grid=(nl, r // tr), in_specs=[spec] + [gspec] * 4 + [spec, spec],
            out_specs=[spec] * 4),
        out_shape=[shp, shp, shp, shp], compiler_params=_params(2),
    )(core, w, halves[0][0], halves[0][1], halves[1][0], halves[1][1], m, v)


def _pair_sum(name, g, x, c):
    n, r, cc = g.shape
    hr = r // 2
    tr = 128 if hr % 128 == 0 else hr
    nb = hr // tr

    def body(c_ref, g_ref, x_ref, o_ref):
        o_ref[...] = g_ref[...] + x_ref[...]

    spec = pl.BlockSpec((None, tr, cc), lambda i, j, c_ref: (i, j, 0))
    return pl.pallas_call(
        body, name=name,
        grid_spec=pltpu.PrefetchScalarGridSpec(
            num_scalar_prefetch=1, grid=(n, nb),
            in_specs=[pl.BlockSpec((None, tr, cc), lambda i, j, c_ref: (i, c_ref[0] * nb + j, 0)), spec],
            out_specs=spec),
        out_shape=jax.ShapeDtypeStruct((n, hr, cc), F32), compiler_params=_params(2),
    )(c, g, x)


def _sum_slots(name, own, landed, chip):
    n, r, c = own.shape
    tr = 128 if r % 128 == 0 else r

    def body(chip_ref, a_ref, b_ref, c_ref, d_ref, o_ref):
        o_ref[...] = ((a_ref[...] + b_ref[...]) + c_ref[...]) + d_ref[...]

    slot = lambda d: pl.BlockSpec((None, tr, c), lambda j, chip_ref: ((chip_ref[0] + d) % n, j, 0))
    return pl.pallas_call(
        body, name=name,
        grid_spec=pltpu.PrefetchScalarGridSpec(
            num_scalar_prefetch=1, grid=(r // tr,), in_specs=[slot(0), slot(1), slot(2), slot(3)],
            out_specs=pl.BlockSpec((tr, c), lambda j, chip_ref: (j, 0))),
        out_shape=jax.ShapeDtypeStruct((r, c), F32), compiler_params=_params(1),
    )(chip, own, landed, landed, landed)


def _me():
    return lax.axis_index("x"), lax.axis_index("y"), lax.axis_index("c")


def _other_chips(x, y):
    return [(1 - x, y), (x, 1 - y), (1 - x, 1 - y)]


def _dma_sems(*counts):
    return [pltpu.SemaphoreType.DMA((n,)) for n in counts]


def _half_rows(rows, which, align):
    return pl.ds(pl.multiple_of(which * (rows // 2), align), rows // 2)


def _gather_first_layer(shards):
    n = len(shards)

    def body(*refs):
        ins, outs, keep, stage = refs[:n], refs[n:2 * n], refs[2 * n:3 * n], refs[3 * n:4 * n]
        ici_send, ici_recv, d2d_send, d2d_recv, local_sems = refs[4 * n:]
        x, y, c = _me()
        k = 2 * x + y
        chips = _other_chips(x, y)
        local, first, passed = [], [], []
        for t in range(n):
            stage[t][...] = ins[t][0].astype(BF16)
            keep[t][...] = ins[t][1].astype(BF16)
            cp = pltpu.make_async_copy(stage[t], outs[t].at[k], local_sems.at[t])
            cp.start()
            local.append(cp)
        for t in range(n):
            mine = _half_rows(shards[t].shape[1], c, 16)
            for j, (px, py) in enumerate(chips):
                cp = pltpu.make_async_remote_copy(
                    src_ref=stage[t].at[mine], dst_ref=outs[t].at[k, mine], send_sem=ici_send.at[3 * t + j],
                    recv_sem=ici_recv.at[3 * t + j], device_id=(px, py, c), device_id_type=MESH)
                cp.start()
                first.append(cp)
        for t in range(n):
            mine = _half_rows(shards[t].shape[1], c, 16)
            for j, (px, py) in enumerate(chips):
                landed = outs[t].at[2 * px + py, mine]
                first[3 * t + j].wait_recv()
                cp = pltpu.make_async_remote_copy(
                    src_ref=landed, dst_ref=landed, send_sem=d2d_send.at[3 * t + j],
                    recv_sem=d2d_recv.at[3 * t + j], device_id=(x, y, 1 - c), device_id_type=MESH)
                cp.start()
                passed.append(cp)
        for cp in passed:
            cp.wait_recv()
        for cp in first + passed:
            cp.wait_send()
        for cp in local:
            cp.wait()

    return pl.pallas_call(
        body, name="gather_first_layer",
        in_specs=[VMEM_SPEC] * n, out_specs=[ANY] * n + [VMEM_SPEC] * n,
        out_shape=[jax.ShapeDtypeStruct((4,) + s.shape[1:], BF16) for s in shards]
        + [jax.ShapeDtypeStruct(s.shape[1:], BF16) for s in shards],
        scratch_shapes=[pltpu.VMEM(s.shape[1:], BF16) for s in shards] + _dma_sems(3 * n, 3 * n, 3 * n, 3 * n, n),
        compiler_params=pltpu.CompilerParams(vmem_limit_bytes=VMEM_LIMIT),
    )(*shards)


def _run_comm(name, comm):
    nci, nco = len(comm.ins), len(comm.out_shapes)

    def body(*refs):
        copies = comm.make(refs[:nci], refs[nci:nci + nco], refs[nci + nco:])
        for cp in copies:
            cp.start()
        for cp in copies:
            cp.wait()

    return pl.pallas_call(body, name=name, in_specs=[ANY] * nci, out_specs=[ANY] * nco,
                          out_shape=list(comm.out_shapes), scratch_shapes=list(comm.sems))(*comm.ins)


def _gather_comm(mine):
    n = len(mine)

    def make(ins, outs, sems):
        send_sems, recv_sems, local_sems = sems
        x, y, c = _me()
        k = 2 * x + y
        copies = []
        for t in range(n):
            copies.append(pltpu.make_async_copy(ins[t], outs[t].at[k], local_sems.at[t]))
            for j, (px, py) in enumerate(_other_chips(x, y)):
                copies.append(pltpu.make_async_remote_copy(
                    src_ref=ins[t], dst_ref=outs[t].at[k], send_sem=send_sems.at[3 * t + j],
                    recv_sem=recv_sems.at[3 * t + j], device_id=(px, py, c), device_id_type=MESH))
        return copies

    return _Comm(mine, [jax.ShapeDtypeStruct((4,) + a.shape, a.dtype) for a in mine], _dma_sems(3 * n, 3 * n, n), make)


def _swap_comm(grads):
    n = len(grads)

    def make(ins, outs, sems):
        send_sems, recv_sems = sems
        x, y, c = _me()
        return [pltpu.make_async_remote_copy(
            src_ref=ins[t].at[:, _half_rows(grads[t].shape[1], 1 - c, 8)], dst_ref=outs[t],
            send_sem=send_sems.at[t], recv_sem=recv_sems.at[t], device_id=(x, y, 1 - c), device_id_type=MESH)
            for t in range(n)]

    shapes = [jax.ShapeDtypeStruct((g.shape[0], g.shape[1] // 2, g.shape[2]), F32) for g in grads]
    return _Comm(grads, shapes, _dma_sems(n, n), make)


def _scatter_comm(parts):
    n = len(parts)

    def make(ins, outs, sems):
        send_sems, recv_sems = sems
        x, y, c = _me()
        k = 2 * x + y
        return [pltpu.make_async_remote_copy(
            src_ref=ins[t].at[2 * px + py], dst_ref=outs[t].at[k], send_sem=send_sems.at[3 * t + j],
            recv_sem=recv_sems.at[3 * t + j], device_id=(px, py, c), device_id_type=MESH)
            for t in range(n) for j, (px, py) in enumerate(_other_chips(x, y))]

    return _Comm(parts, [jax.ShapeDtypeStruct(p.shape, F32) for p in parts], _dma_sems(3 * n, 3 * n), make)


def _share_comm(totals):
    n = len(totals)

    def make(ins, outs, sems):
        send_sems, recv_sems = sems
        x, y, c = _me()
        return [pltpu.make_async_remote_copy(
            src_ref=ins[t], dst_ref=outs[t], send_sem=send_sems.at[t], recv_sem=recv_sems.at[t],
            device_id=(x, y, 1 - c), device_id_type=MESH) for t in range(n)]

    return _Comm(totals, [jax.ShapeDtypeStruct(t.shape, F32) for t in totals], _dma_sems(n, n), make)


def _small_allreduce_adamw(part, w, m, v):
    shape = part.shape

    def body(part_ref, w_ref, m_ref, v_ref, g_ref, d_ref, nm_ref, nv_ref, slots, send_sems, recv_sems):
        x, y, c = _me()
        me = 4 * x + 2 * y + c
        slots[me] = part_ref[...]
        copies = []
        for d in range(1, 8):
            peer = (x ^ (d >> 2), y ^ ((d >> 1) & 1), c ^ (d & 1))
            cp = pltpu.make_async_remote_copy(
                src_ref=part_ref, dst_ref=slots.at[me], send_sem=send_sems.at[d - 1], recv_sem=recv_sems.at[d - 1],
                device_id=peer, device_id_type=MESH)
            cp.start()
            copies.append(cp)
        for cp in copies:
            cp.wait()
        g = slots[0]
        for i in range(1, 8):
            g = g + slots[i]
        g_ref[...] = g
        d, nm, nv = _adamw_math(w_ref[...], g, m_ref[...], v_ref[...])
        d_ref[...] = d
        nm_ref[...] = nm
        nv_ref[...] = nv

    shp = jax.ShapeDtypeStruct(shape, F32)
    return pl.pallas_call(
        body, name="small_allreduce_adamw", in_specs=[VMEM_SPEC] * 4, out_specs=[VMEM_SPEC] * 4,
        out_shape=[shp, shp, shp, shp],
        scratch_shapes=[pltpu.VMEM((8,) + shape, F32), pltpu.SemaphoreType.DMA((7,)), pltpu.SemaphoreType.DMA((7,))],
    )(part, w, m, v)


TM = 256
T_FOX = 1024
T_DIL_FWD = 1024
T_DIL_BWD = 512
T_SCAN = 512
TN_DW = 1408


def _layer_fwd(l, h, p_i, wts, consts, comm=None):
    w_all, _, w_out, _, w_gate, _, w_ple, norm_g, b_f, qkg, ple_g = wts
    bsum, _, bias_t, _, rope = consts
    u, z, qa, ka, va, qb, kb, vb, logf, va_t, vb_t = _inproj(l, h, norm_g, w_all, b_f, qkg, bsum, rope, TM)
    c_spread, c_t = _cumsum_gates(l, logf, T_SCAN)
    oa, lse_a, *landed = _attn_fwd(f"fox_fwd_l{l}", True, qa, ka, va_t, (c_spread, c_t), T_FOX, comm)
    ob, lse_b = _attn_fwd(f"dil_fwd_l{l}", False, qb, kb, vb_t, (bias_t,), T_DIL_FWD)
    h2, e, gate, out = _outproj(l, h, oa, ob, z, p_i, w_out, ple_g, w_gate, w_ple, TM)
    saved = (h, u, z, qa, ka, va, qb, kb, vb, c_spread, c_t, oa, lse_a, ob, lse_b, h2, e, gate)
    return out, saved, landed


def _reduce_names(tag):
    return [f"reduce_{tag}_{w}" for w in ("w_in", "w_out", "w_ple", "w_gate")]


def _layer_bwd(l, dout, p_i, wts, consts, saved, pending=None, core=None, chip=None):
    _, w_all_t, _, w_out_t, _, w_gate_t, _, norm_g, b_f, qkg, ple_g = wts
    bsum, hsel, _, bias_t, rope = consts
    h, u, z, qa, ka, va, qb, kb, vb, c_spread, c_t, oa, lse_a, ob, lse_b, h2, e, gate = saved
    fused = pending is not None
    dh2, doa, dob, dga, dgb, delta_t, dw_out, dw_gate, dw_ple, dple_g, *sib = _outproj_bwd(
        l, dout, h2, e, gate, p_i, oa, ob, z, w_out_t, ple_g, w_gate_t, hsel, TM,
        _swap_comm(pending) if fused else None)
    if fused:
        pair = [_pair_sum(n, g, x, core) for n, g, x in zip(_reduce_names(f"pair_l{l + 1}"), pending, sib)]
    dqt_a, dk_a, dv_a, dc, drow, *landed = _attn_bwd(
        f"fox_bwd_l{l}", True, qa, ka, va, doa, lse_a, delta_t, 0, (c_spread, c_t), T_FOX,
        _scatter_comm(pair) if fused else None)
    if fused:
        totals = [_sum_slots(n, a, y, chip) for n, a, y in zip(_reduce_names(f"chips_l{l + 1}"), pair, landed)]
    dqt_b, dk_b, dv_b, *other = _attn_bwd(f"dil_bwd_l{l}", False, qb, kb, vb, dob, lse_b, delta_t, N_PAIRS,
                                            (bias_t,), T_DIL_BWD, _share_comm(totals) if fused else None)
    dlogf = _rev_cumsum_gates(l, dc, drow, T_SCAN)
    dz, dqkg, dbf = _inproj_bwd_prep(l, z, dqt_a, dk_a, dv_a, dqt_b, dk_b, dv_b, dga, dgb, dlogf, b_f, qkg,
                                     bsum, rope, TM)
    dh, dnorm_g = _inproj_bwd_dx(l, dz, w_all_t, h, norm_g, dh2, TM)
    dw_all = _inproj_bwd_dw(l, u, dz, TM * 2, TN_DW)
    reduced = list(zip(totals, other)) if fused else None
    return dh, (dw_all, dw_out, dw_ple, dw_gate, dnorm_g[0], dbf[0, :N_HEADS], dqkg[:4], dple_g[0]), reduced


def _reduce_last(grads, core, chip, l):
    sib = _run_comm(f"reduce_swap_l{l}", _swap_comm(grads))
    pair = [_pair_sum(n, g, x, core) for n, g, x in zip(_reduce_names(f"pair_l{l}"), grads, sib)]
    landed = _run_comm(f"reduce_scatter_l{l}", _scatter_comm(pair))
    totals = [_sum_slots(n, a, y, chip) for n, a, y in zip(_reduce_names(f"chips_l{l}"), pair, landed)]
    return list(zip(totals, _run_comm(f"reduce_share_l{l}", _share_comm(totals))))


N_FA = 2048


def _layer_weights(l, gathered, norm_g, b_f, qk_norm_g, ple_norm_g):
    g_in, g_out, g_ple, g_gate = gathered
    w_in = jnp.transpose(g_in, (1, 0, 2)).reshape(D_MODEL, N_IN)
    w_all = jnp.concatenate([w_in[:, :N_FA], w_in[:, N_FA + N_HEADS:],
                             jnp.pad(w_in[:, N_FA:N_FA + N_HEADS], ((0, 0), (0, LANE - N_HEADS)))], axis=1)
    w_out = g_out.reshape(D_MODEL, D_MODEL)
    w_gate = g_gate.reshape(D_MODEL, D_MODEL)
    w_ple = jnp.transpose(g_ple, (1, 0, 2)).reshape(PLE_DIM, D_MODEL)
    qkg = jnp.pad(jnp.tile(qk_norm_g[l], (1, N_HEADS)), ((0, 4), (0, 0)))
    bf = jnp.pad(b_f[l], (0, LANE - N_HEADS))[None, :]
    return (w_all, w_all.T, w_out, w_out.T, w_gate, w_gate.T, w_ple, norm_g[l][None, :], bf, qkg,
            ple_norm_g[l][None, :])


def _slot_layout(dw_all, dw_out, dw_ple, dw_gate):
    dw_in = jnp.concatenate([dw_all[:, :N_FA], dw_all[:, N_MAIN:N_MAIN + N_HEADS], dw_all[:, N_FA:N_MAIN]], axis=1)
    return (jnp.transpose(dw_in.reshape(D_MODEL, 4, N_IN // 4), (1, 0, 2)),
            dw_out.reshape(4, D_MODEL // 4, D_MODEL),
            jnp.transpose(dw_ple.reshape(PLE_DIM, 4, D_MODEL // 4), (1, 0, 2)),
            dw_gate.reshape(4, D_MODEL // 4, D_MODEL))


SMALL_ROWS = 40


def _pack_small(norm_g, ple_norm_g, qk_norm_g, b_f):
    flat = jnp.concatenate([norm_g.reshape(-1), ple_norm_g.reshape(-1), qk_norm_g.reshape(-1), b_f.reshape(-1)])
    return jnp.pad(flat, (0, SMALL_ROWS * LANE - flat.shape[0])).reshape(SMALL_ROWS, LANE)


def _unpack_small(packed):
    flat = packed.reshape(-1)
    n1, n2, n3 = 2 * D_MODEL, 4 * D_MODEL, 4 * D_MODEL + 2 * 4 * HEAD_DIM
    return (flat[:n1].reshape(2, D_MODEL), flat[n1:n2].reshape(2, D_MODEL), flat[n2:n3].reshape(2, 4, HEAD_DIM),
            flat[n3:n3 + 2 * N_HEADS].reshape(2, N_HEADS))


def kernel(x, p, positions, norm_g, w_in, b_f, qk_norm_g, w_out, w_ple, ple_norm_g, w_ple_gate, loss_target,
           m_norm_g, m_w_in, m_b_f, m_qk_norm_g, m_w_out, m_w_ple, m_ple_norm_g, m_w_ple_gate,
           v_norm_g, v_w_in, v_b_f, v_qk_norm_g, v_w_out, v_w_ple, v_ple_norm_g, v_w_ple_gate):
    assert w_in.shape[0] == 2, "the schedule below is written for two layers"
    *first, = _gather_first_layer([w_in, w_out, w_ple, w_ple_gate])
    consts = (_head_block_diag(), _head_select(), _dil_bias(T_DIL_FWD), _dil_bias(T_DIL_BWD),
              _rope_tables(positions[0]))
    small_w = (norm_g, b_f, qk_norm_g, ple_norm_g)
    wts0 = _layer_weights(0, first[:4], *small_w)
    h1, saved0, second = _layer_fwd(0, x[0], p[0, 0], wts0, consts, _gather_comm(first[4:]))
    wts1 = _layer_weights(1, second, *small_w)
    h2, saved1, _ = _layer_fwd(1, h1, p[1, 0], wts1, consts)
    sq, dh = _loss_head(h2, loss_target[0], TM)
    loss = lax.psum(0.5 / D_MODEL * jnp.sum(sq), ("x", "y", "c"))

    core = lax.axis_index("c").astype(jnp.int32).reshape(1)
    chip = (2 * lax.axis_index("x") + lax.axis_index("y")).astype(jnp.int32).reshape(1)
    dh, grads1, _ = _layer_bwd(1, dh, p[1, 0], wts1, consts, saved1)
    dh, grads0, reduced1 = _layer_bwd(0, dh, p[0, 0], wts0, consts, saved0, _slot_layout(*grads1[:4]), core, chip)
    reduced0 = _reduce_last(_slot_layout(*grads0[:4]), core, chip, 0)
    grad_x = dh[None]
    small = [grads0[4:], grads1[4:]]
    n_layers = 2

    outs = {}
    for t, (name, w, m, v) in enumerate((("w_in", w_in, m_w_in, v_w_in), ("w_out", w_out, m_w_out, v_w_out),
                                         ("w_ple", w_ple, m_w_ple, v_w_ple),
                                         ("w_ple_gate", w_ple_gate, m_w_ple_gate, v_w_ple_gate))):
        outs[name] = tuple(_adamw(f"adamw_{name}", w, (reduced0[t], reduced1[t]), m, v, core))

    part = _pack_small(jnp.stack([s[0] for s in small]), jnp.stack([s[3] for s in small]),
                       jnp.stack([s[2] for s in small]).reshape(n_layers, 4, N_HEADS, HEAD_DIM).sum(axis=2),
                       jnp.stack([s[1] for s in small]))
    packed = _small_allreduce_adamw(part, _pack_small(norm_g, ple_norm_g, qk_norm_g, b_f),
                                    _pack_small(m_norm_g, m_ple_norm_g, m_qk_norm_g, m_b_f),
                                    _pack_small(v_norm_g, v_ple_norm_g, v_qk_norm_g, v_b_f))
    sm = [_unpack_small(a) for a in packed]
    for i, name in enumerate(("norm_g", "ple_norm_g", "qk_norm_g", "b_f")):
        outs[name] = tuple(sm[j][i] for j in range(4))

    order = ("norm_g", "w_in", "b_f", "qk_norm_g", "w_out", "w_ple", "ple_norm_g", "w_ple_gate")
    return (loss, grad_x) + tuple(outs[n][j] for j in range(4) for n in order)
```

```python
import functools
from typing import Any, Callable, NamedTuple, Sequence

import numpy as np
import jax
import jax.numpy as jnp
from jax import lax
from jax.experimental import pallas as pl
from jax.experimental.pallas import tpu as pltpu

F32 = jnp.float32
BF16 = jnp.bfloat16
MESH = pl.DeviceIdType.MESH

D_MODEL = 1024
HEAD_DIM = 64
D_BRANCH = 512
N_HEADS = 8
N_PAIRS = 4
N_IN = 4104
N_MAIN = 4096
N_ALL = 4224
PLE_DIM = 256
ROPE_THETA = 500000.0
ROPE_HALF = 8
EPS = 1e-6
NEG = -1e30
M_INIT = -1e29
Q_SCALE = HEAD_DIM ** -0.5
LOG2E = 1.4426950408889634
LN2 = 0.6931471805599453
DIL_PATTERNS = ((128, 1), (512, 4), (2048, 16))
DIL_BACK = 2048
ADAM_LR, ADAM_B1, ADAM_B2, ADAM_EPS, ADAM_WD, ADAM_STEP = 0.001, 0.9, 0.999, 1e-08, 0.01, 10
VMEM_LIMIT = 56 * 1024 * 1024
LANE = 128


def _dot(a, b):
    return jnp.dot(a, b, preferred_element_type=F32)


def _dot_nt(a, b):
    return lax.dot_general(a, b, (((1,), (1,)), ((), ())), preferred_element_type=F32)


def _dot_tn(a, b):
    return lax.dot_general(a, b, (((0,), (0,)), ((), ())), preferred_element_type=F32)


def _split_dot(x, w):
    hi = x.astype(BF16)
    lo = (x - hi.astype(F32)).astype(BF16)
    return _dot(hi, w) + _dot(lo, w)


def _head_sums(x, bs):
    w = bs.shape[0]
    return jnp.concatenate([_split_dot(x[:, w * k:w * (k + 1)], bs) for k in range(x.shape[1] // w)], axis=1)


def _split3_dot(w, x):
    hi = x.astype(BF16)
    r1 = x - hi.astype(F32)
    mid = r1.astype(BF16)
    lo = (r1 - mid.astype(F32)).astype(BF16)
    return _dot(w, hi) + _dot(w, mid) + _dot(w, lo)


def _sigmoid(x):
    return 1.0 / (1.0 + jnp.exp(-x))


def _params(n_grid):
    return pltpu.CompilerParams(dimension_semantics=("arbitrary",) * n_grid,
                                vmem_limit_bytes=VMEM_LIMIT)


def _full(shape):
    nd = len(shape)
    return pl.BlockSpec(shape, lambda *_: (0,) * nd)


ANY = pl.BlockSpec(memory_space=pl.ANY)
VMEM_SPEC = pl.BlockSpec(memory_space=pltpu.VMEM)


class _Comm(NamedTuple):
    ins: Sequence[Any]
    out_shapes: Sequence[Any]
    sems: Sequence[Any]
    make: Callable[..., Any]


def _fuse_comm(body, n_in, n_out, comm, grid):
    if comm is None:
        return body
    nci, nco, ncs = len(comm.ins), len(comm.out_shapes), len(comm.sems)

    def fused(*refs):
        a, b = n_in + nci, n_in + nci + n_out
        ins, cins, outs, couts = refs[:n_in], refs[n_in:a], refs[a:b], refs[b:b + nco]
        scratch, sems = refs[b + nco:len(refs) - ncs], refs[len(refs) - ncs:]
        first = functools.reduce(jnp.logical_and, [pl.program_id(d) == 0 for d in range(len(grid))])
        last = functools.reduce(jnp.logical_and, [pl.program_id(d) == n - 1 for d, n in enumerate(grid)])

        @pl.when(first)
        def _():
            for cp in comm.make(cins, couts, sems):
                cp.start()

        body(*ins, *outs, *scratch)

        @pl.when(last)
        def _():
            for cp in comm.make(cins, couts, sems):
                cp.wait()

    return fused


def _comm_args(comm):
    if comm is None:
        return [], [], [], [], []
    return (list(comm.ins), [ANY] * len(comm.ins), [ANY] * len(comm.out_shapes), list(comm.out_shapes),
            list(comm.sems))


HEADS_PER_BLOCK = 4


def _head_block_diag():
    i = np.arange(HEADS_PER_BLOCK * HEAD_DIM)
    return jnp.asarray((i[:, None] // HEAD_DIM == i[None, :] // HEAD_DIM).astype(np.float32), BF16)


def _head_select():
    i = np.arange(2 * D_BRANCH)
    j = np.arange(LANE)
    return jnp.asarray((i[:, None] // HEAD_DIM == j[None, :]).astype(np.float32), BF16)


def _dil_bias(t):
    nb = DIL_BACK // t + 1
    qi = np.arange(t)[:, None]
    ki = np.arange(t)[None, :]
    tiles = []
    for r in range(nb):
        d = r * t + qi - ki
        mult = np.zeros((t, t), np.int64)
        for window, dil in DIL_PATTERNS:
            mult += ((d >= 0) & (d <= window) & (d % dil == 0)).astype(np.int64)
        b = np.where(mult > 0, np.log2(np.maximum(mult, 1)), NEG).astype(np.float32)
        tiles.append(b.T)
    return jnp.asarray(np.stack(tiles))


def _rope_tables(positions):
    inv_freq = ROPE_THETA ** (-jnp.arange(ROPE_HALF, dtype=F32) / ROPE_HALF)
    ang = positions.astype(F32)[:, None] * inv_freq
    cos, sin = jnp.cos(ang), jnp.sin(ang)
    s = positions.shape[0]
    rest = HEAD_DIM - 2 * ROPE_HALF
    one, zero, zero8 = jnp.ones((s, rest), F32), jnp.zeros((s, rest), F32), jnp.zeros((s, ROPE_HALF), F32)
    c = jnp.concatenate([cos, cos, one], axis=1)
    s1 = jnp.concatenate([zero8, sin, zero], axis=1)
    s2 = jnp.concatenate([-sin, zero8, zero], axis=1)
    return tuple(jnp.tile(t, (1, 2)) for t in (c, s1, s2))


def _rope_fwd(x, c, s1, s2):
    return x * c + pltpu.roll(x, ROPE_HALF, 1) * s1 + pltpu.roll(x, LANE - ROPE_HALF, 1) * s2


def _rope_bwd(dy, c, s1, s2):
    return dy * c + pltpu.roll(dy * s1, LANE - ROPE_HALF, 1) + pltpu.roll(dy * s2, ROPE_HALF, 1)


def _log_sigmoid(x):
    return jnp.minimum(x, 0.0) - jnp.log(1.0 + jnp.exp(-jnp.abs(x)))


def _inproj(l, h, norm_g, w_all, b_f, qkg, bsum, rope, tm):
    s = h.shape[0]
    rc, rs1, rs2 = rope

    def body(h_ref, g_ref, w_ref, bf_ref, qkg_ref, bsum_ref, rc_ref, rs1_ref, rs2_ref,
             u_ref, z_ref, qa_ref, ka_ref, va_ref, qb_ref, kb_ref, vb_ref, lf_ref, vat_ref, vbt_ref):
        hh = h_ref[...]
        r = lax.rsqrt(jnp.mean(hh * hh, axis=-1, keepdims=True) + EPS)
        u = (hh * r * g_ref[...]).astype(BF16)
        u_ref[...] = u
        for k in range(N_ALL // LANE // 3):
            cols = slice(3 * LANE * k, 3 * LANE * (k + 1))
            z_ref[:, cols] = _dot(u, w_ref[:, cols])
        bs = bsum_ref[...]

        def head_norm(x, row):
            ms = _head_sums(x * x, bs) * (1.0 / HEAD_DIM)
            return x * lax.rsqrt(ms + EPS) * qkg_ref[row:row + 1, :]

        def seg(k):
            return z_ref[:, D_BRANCH * k:D_BRANCH * (k + 1)]

        qa_ref[...] = (head_norm(seg(0), 0) * (Q_SCALE * LOG2E)).astype(BF16)
        ka_ref[...] = head_norm(seg(1), 1).astype(BF16)
        va_ref[...] = seg(2).astype(BF16)
        vat_ref[...] = seg(2).T.astype(BF16)
        qn = head_norm(seg(4), 2) * (Q_SCALE * LOG2E)
        kn = head_norm(seg(5), 3)
        c, s1, s2 = rc_ref[...], rs1_ref[...], rs2_ref[...]
        for k in range(D_BRANCH // LANE):
            cols = slice(LANE * k, LANE * (k + 1))
            qb_ref[:, cols] = _rope_fwd(qn[:, cols], c, s1, s2).astype(BF16)
            kb_ref[:, cols] = _rope_fwd(kn[:, cols], c, s1, s2).astype(BF16)
        vb_ref[...] = seg(6).astype(BF16)
        vbt_ref[...] = seg(6).T.astype(BF16)
        lf_ref[...] = _log_sigmoid(z_ref[:, N_MAIN:N_ALL] + bf_ref[...])

    row = lambda w: pl.BlockSpec((tm, w), lambda i: (i, 0))
    colt = pl.BlockSpec((D_BRANCH, tm), lambda i: (0, i))
    bf = lambda: jax.ShapeDtypeStruct((s, D_BRANCH), BF16)
    bft = lambda: jax.ShapeDtypeStruct((D_BRANCH, s), BF16)
    return pl.pallas_call(
        body, name=f"inproj_l{l}", grid=(s // tm,),
        in_specs=[row(D_MODEL), _full((1, D_MODEL)), _full((D_MODEL, N_ALL)), _full((1, LANE)),
                  _full((8, D_BRANCH)), _full((HEADS_PER_BLOCK * HEAD_DIM,) * 2), row(LANE), row(LANE), row(LANE)],
        out_specs=[row(D_MODEL), row(N_ALL)] + [row(D_BRANCH)] * 6 + [row(LANE), colt, colt],
        out_shape=[jax.ShapeDtypeStruct((s, D_MODEL), BF16), jax.ShapeDtypeStruct((s, N_ALL), F32),
                   bf(), bf(), bf(), bf(), bf(), bf(), jax.ShapeDtypeStruct((s, LANE), F32), bft(), bft()],
        compiler_params=_params(1),
    )(h, norm_g, w_all, b_f, qkg, bsum, rc, rs1, rs2)


def _tri(t, upper):
    a = lax.broadcasted_iota(jnp.int32, (t, t), 0)
    b = lax.broadcasted_iota(jnp.int32, (t, t), 1)
    return jnp.where((b >= a) if upper else (b <= a), 1.0, 0.0).astype(BF16)


def _cumsum_gates(l, logf, t):
    s = logf.shape[0]

    def body(lf_ref, cs_ref, ct_ref, carry):
        @pl.when(pl.program_id(0) == 0)
        def _():
            carry[...] = jnp.zeros_like(carry)

        x = lf_ref[...]
        c = _split3_dot(_tri(t, False), x) + carry[0:1, :]
        carry[...] = jnp.broadcast_to(c[t - 1:t, :], carry.shape)
        c = c * LOG2E
        ct = c.T
        for p in range(N_PAIRS):
            cs_ref[:, LANE * p:LANE * (p + 1)] = c if p == 0 else pltpu.roll(c, LANE - 2 * p, 1)
            ct_ref[p, :, :] = ct[2 * p:2 * p + 2, :]

    return pl.pallas_call(
        body, name=f"cumsum_l{l}", grid=(s // t,),
        in_specs=[pl.BlockSpec((t, LANE), lambda i: (i, 0))],
        out_specs=[pl.BlockSpec((t, N_PAIRS * LANE), lambda i: (i, 0)),
                   pl.BlockSpec((N_PAIRS, 2, t), lambda i: (0, 0, i))],
        out_shape=[jax.ShapeDtypeStruct((s, N_PAIRS * LANE), F32), jax.ShapeDtypeStruct((N_PAIRS, 2, s), F32)],
        scratch_shapes=[pltpu.VMEM((8, LANE), F32)],
        compiler_params=_params(1),
    )(logf)


def _rev_cumsum_gates(l, dc_spread, drow, t):
    s = dc_spread.shape[0]
    n = s // t

    def body(dc_ref, drow_ref, out_ref, carry):
        @pl.when(pl.program_id(0) == 0)
        def _():
            carry[...] = jnp.zeros_like(carry)

        lane = lax.broadcasted_iota(jnp.int32, (t, LANE), 1)
        rows = jnp.concatenate([drow_ref[p] for p in range(N_PAIRS)] + [jnp.zeros((LANE - N_HEADS, t), F32)], axis=0)
        x = rows.T
        for p in range(N_PAIRS):
            xp = jnp.where(lane < 2, dc_ref[:, LANE * p:LANE * (p + 1)], 0.0)
            x = x + (xp if p == 0 else pltpu.roll(xp, 2 * p, 1))
        out = _split3_dot(_tri(t, True), x) + carry[0:1, :]
        out_ref[...] = out
        carry[...] = jnp.broadcast_to(out[0:1, :], carry.shape)

    return pl.pallas_call(
        body, name=f"revcumsum_l{l}", grid=(n,),
        in_specs=[pl.BlockSpec((t, N_PAIRS * LANE), lambda i: (n - 1 - i, 0)),
                  pl.BlockSpec((N_PAIRS, 2, t), lambda i: (0, 0, n - 1 - i))],
        out_specs=pl.BlockSpec((t, LANE), lambda i: (n - 1 - i, 0)),
        out_shape=jax.ShapeDtypeStruct((s, LANE), F32),
        scratch_shapes=[pltpu.VMEM((8, LANE), F32)],
        compiler_params=_params(1),
    )(dc_spread, drow)


def _attn_fwd(name, fox, q, k, vt, extra, t, comm=None):
    s = q.shape[0]
    nq = s // t
    nb = DIL_BACK // t + 1

    def body(*refs):
        if fox:
            q_ref, k_ref, vt_ref, ccol_ref, crow_ref, o_ref, lse_ref, m_scr, l_scr, acc_scr = refs
        else:
            q_ref, k_ref, vt_ref, bias_ref, o_ref, lse_ref, m_scr, l_scr, acc_scr = refs
        i = pl.program_id(1)
        lane = lax.broadcasted_iota(jnp.int32, (t, LANE), 1)
        first = lane < HEAD_DIM
        qq = q_ref[...]
        zero = jnp.zeros_like(qq)
        qh = (jnp.where(first, qq, zero), jnp.where(first, zero, qq))
        m_scr[...] = jnp.full(m_scr.shape, M_INIT, F32)
        l_scr[...] = jnp.zeros_like(l_scr)
        acc_scr[...] = jnp.zeros_like(acc_scr)
        ones = jnp.ones((16, t), BF16)

        half = t // 2
        whole, lo, hi = slice(0, t), slice(0, half), slice(half, t)

        def block(j, ksl, qsl, causal):
            nk_, nq_ = ksl.stop - ksl.start, qsl.stop - qsl.start
            rows = pl.ds(pl.multiple_of(j * t + ksl.start, LANE), nk_)
            ks = k_ref[rows, :]
            vts = jnp.concatenate([vt_ref[:, rows], ones[:, :nk_]], axis=0)
            if fox:
                ccol = ccol_ref[rows, :]
            for h in range(2):
                st = _dot_nt(ks, qh[h][qsl, :])
                if fox:
                    st = st + (crow_ref[h:h + 1, qsl] - ccol[:, h:h + 1])
                    if causal:
                        ki = lax.broadcasted_iota(jnp.int32, (nk_, nq_), 0) + ksl.start
                        qi = lax.broadcasted_iota(jnp.int32, (nk_, nq_), 1) + qsl.start
                        st = jnp.where(ki <= qi, st, NEG)
                else:
                    st = st + bias_ref[i - j, ksl, qsl]
                m_old = m_scr[h, :, qsl]
                m_new = jnp.maximum(m_old, jnp.max(st, axis=0, keepdims=True))
                alpha = jnp.exp2(m_old - m_new)
                pb = jnp.exp2(st - m_new).astype(BF16)
                pv = _dot(vts, pb)
                l_scr[h, :, qsl] = alpha * l_scr[h, :, qsl] + pv[LANE:LANE + 1, :]
                acc_scr[h, :, qsl] = alpha * acc_scr[h, :, qsl] + pv[:LANE, :]
                m_scr[h, :, qsl] = m_new

        def full(j, c):
            block(j, whole, whole, False)
            return c

        if fox:
            lax.fori_loop(0, i, full, 0)
        else:
            @pl.when(i >= nb - 1)
            def _():
                block(i - (nb - 1), lo, lo, False)
                block(i - (nb - 1), hi, whole, False)

            lax.fori_loop(jnp.maximum(i - (nb - 2), 0), i, full, 0)
        if fox:
            block(i, whole, whole, True)
        else:
            block(i, lo, lo, False)
            block(i, whole, hi, False)

        sub = lax.broadcasted_iota(jnp.int32, (LANE, t), 0)
        ot = jnp.where(sub < HEAD_DIM, acc_scr[0] / l_scr[0], acc_scr[1] / l_scr[1])
        o_ref[...] = ot.T
        for h in range(2):
            lse_ref[h:h + 1, :] = m_scr[h] + jnp.log2(l_scr[h])

    qspec = pl.BlockSpec((t, LANE), lambda hp, i: (i, hp))
    kspec = pl.BlockSpec((s, LANE), lambda hp, i: (0, hp))
    vtspec = pl.BlockSpec((LANE, s), lambda hp, i: (hp, 0))
    in_specs = [qspec, kspec, vtspec]
    if fox:
        in_specs += [kspec, pl.BlockSpec((None, 2, t), lambda hp, i: (hp, 0, i))]
    else:
        in_specs += [_full((nb, t, t))]
    grid = (N_PAIRS, nq)
    c_in, c_ispec, c_ospec, c_oshape, c_scr = _comm_args(comm)
    return pl.pallas_call(
        _fuse_comm(body, len(in_specs), 2, comm, grid), name=name, grid=grid,
        in_specs=in_specs + c_ispec,
        out_specs=[qspec, pl.BlockSpec((None, 2, t), lambda hp, i: (hp, 0, i))] + c_ospec,
        out_shape=[jax.ShapeDtypeStruct((s, D_BRANCH), F32), jax.ShapeDtypeStruct((N_PAIRS, 2, s), F32)] + c_oshape,
        scratch_shapes=[pltpu.VMEM((2, 1, t), F32), pltpu.VMEM((2, 1, t), F32), pltpu.VMEM((2, LANE, t), F32)]
        + c_scr,
        compiler_params=_params(2),
    )(q, k, vt, *extra, *c_in)


def _attn_bwd(name, fox, q, k, v, do, lse_t, delta_t, pair_offset, extra, t, comm=None):
    s = q.shape[0]
    nk = s // t
    nb = DIL_BACK // t + 1

    def body(*refs):
        if fox:
            (q_ref, k_ref, v_ref, do_ref, lse_ref, delta_ref, ccol_ref, crow_ref,
             dqt_ref, dk_ref, dv_ref, dc_ref, drow_ref) = refs
        else:
            q_ref, k_ref, v_ref, do_ref, lse_ref, delta_ref, bias_ref, dqt_ref, dk_ref, dv_ref = refs
        j = pl.program_id(1)

        @pl.when(j == 0)
        def _():
            dqt_ref[...] = jnp.zeros_like(dqt_ref)
            if fox:
                drow_ref[...] = jnp.zeros_like(drow_ref)

        lane = lax.broadcasted_iota(jnp.int32, (t, LANE), 1)
        first = lane < HEAD_DIM
        ks = k_ref[...]
        vs = v_ref[...]
        kt = ks.astype(F32).T
        sub = lax.broadcasted_iota(jnp.int32, (LANE, t), 0)
        kth = (jnp.where(sub < HEAD_DIM, kt, 0.0).astype(BF16), jnp.where(sub < HEAD_DIM, 0.0, kt).astype(BF16))
        dk_ref[...] = jnp.zeros_like(dk_ref)
        dv_ref[...] = jnp.zeros_like(dv_ref)
        if fox:
            dc_ref[...] = jnp.zeros_like(dc_ref)
            ccol = ccol_ref[...]

        half = t // 2
        whole, lo, hi = slice(0, t), slice(0, half), slice(half, t)

        def block(i, ksl, qsl, causal):
            nk_, nq_ = ksl.stop - ksl.start, qsl.stop - qsl.start
            rows = pl.ds(pl.multiple_of(i * t + qsl.start, LANE), nq_)
            qq = q_ref[rows, :]
            dd = do_ref[rows, :]
            zero = jnp.zeros_like(qq)
            qh = (jnp.where(first[:nq_], qq, zero), jnp.where(first[:nq_], zero, qq))
            dh = (jnp.where(first[:nq_], dd, zero), jnp.where(first[:nq_], zero, dd))
            for h in range(2):
                st = _dot_nt(ks[ksl, :], qh[h])
                if fox:
                    st = st + (crow_ref[h:h + 1, rows] - ccol[ksl, h:h + 1])
                    if causal:
                        ki = lax.broadcasted_iota(jnp.int32, (nk_, nq_), 0) + ksl.start
                        qi = lax.broadcasted_iota(jnp.int32, (nk_, nq_), 1) + qsl.start
                        st = jnp.where(ki <= qi, st, NEG)
                else:
                    st = st + bias_ref[i - j, ksl, qsl]
                pt = jnp.exp2(st - lse_ref[h:h + 1, rows])
                dpt = _dot_nt(vs[ksl, :], dh[h])
                dst = pt * (dpt - delta_ref[h:h + 1, rows])
                dv_ref[ksl, :] += _dot(pt.astype(BF16), dh[h])
                dsb = dst.astype(BF16)
                dk_ref[ksl, :] += _dot(dsb, qh[h])
                dqt_ref[:, rows] += _dot(kth[h][:, ksl], dsb)
                if fox:
                    dc_ref[ksl, :] -= jnp.where(lane[:nk_] == h, jnp.sum(dst, axis=1, keepdims=True), 0.0)
                    drow_ref[h:h + 1, rows] += jnp.sum(dst, axis=0, keepdims=True)

        def full(i, c):
            block(i, whole, whole, False)
            return c

        block(j, lo, whole, True)
        block(j, hi, hi, True)
        if fox:
            lax.fori_loop(j + 1, nk, full, 0)
        else:
            lax.fori_loop(j + 1, jnp.minimum(j + nb - 1, nk), full, 0)

            @pl.when(j + nb - 1 < nk)
            def _():
                block(j + nb - 1, lo, lo, False)
                block(j + nb - 1, hi, whole, False)

    kspec = pl.BlockSpec((t, LANE), lambda hp, j: (j, hp))
    qspec = pl.BlockSpec((s, LANE), lambda hp, j: (0, hp))
    rowspec = pl.BlockSpec((None, 2, s), lambda hp, j: (hp, 0, 0))
    drowspec = pl.BlockSpec((None, 2, s), lambda hp, j: (hp + pair_offset, 0, 0))
    in_specs = [qspec, kspec, kspec, qspec, rowspec, drowspec]
    out_specs = [pl.BlockSpec((LANE, s), lambda hp, j: (hp, 0)), kspec, kspec]
    out_shape = [jax.ShapeDtypeStruct((D_BRANCH, s), F32), jax.ShapeDtypeStruct((s, D_BRANCH), F32),
                 jax.ShapeDtypeStruct((s, D_BRANCH), F32)]
    if fox:
        in_specs += [kspec, rowspec]
        out_specs += [kspec, rowspec]
        out_shape += [jax.ShapeDtypeStruct((s, N_PAIRS * LANE), F32), jax.ShapeDtypeStruct((N_PAIRS, 2, s), F32)]
    else:
        in_specs += [_full((nb, t, t))]
    grid = (N_PAIRS, nk)
    c_in, c_ispec, c_ospec, c_oshape, c_scr = _comm_args(comm)
    return pl.pallas_call(
        _fuse_comm(body, len(in_specs), len(out_specs), comm, grid), name=name, grid=grid,
        in_specs=in_specs + c_ispec, out_specs=out_specs + c_ospec, out_shape=out_shape + c_oshape,
        scratch_shapes=c_scr, compiler_params=_params(2),
    )(q, k, v, do, lse_t, delta_t, *extra, *c_in)


def _silu(x):
    return x * _sigmoid(x)


def _outproj(l, h, oa, ob, z, p_i, w_out, ple_g, w_gate, w_ple, tm):
    s = h.shape[0]

    def body(h_ref, oa_ref, ob_ref, ga_ref, gb_ref, p_ref, wo_ref, pg_ref, wg_ref, wp_ref,
             h2_ref, e_ref, gate_ref, out_ref):
        a = jnp.concatenate([oa_ref[...] * _silu(ga_ref[...]), ob_ref[...] * _silu(gb_ref[...])], axis=1)
        h2 = h_ref[...] + _dot(a.astype(BF16), wo_ref[...])
        h2_ref[...] = h2
        r = lax.rsqrt(jnp.mean(h2 * h2, axis=-1, keepdims=True) + EPS)
        n2 = (h2 * r * pg_ref[...]).astype(BF16)
        gate = _sigmoid(_dot(n2, wg_ref[...]))
        e = _dot(p_ref[...].astype(BF16), wp_ref[...])
        e_ref[...] = e
        gate_ref[...] = gate
        out_ref[...] = h2 + e * gate

    row = lambda w: pl.BlockSpec((tm, w), lambda i: (i, 0))
    zcol = lambda k: pl.BlockSpec((tm, D_BRANCH), lambda i: (i, k))
    f = lambda: jax.ShapeDtypeStruct((s, D_MODEL), F32)
    return pl.pallas_call(
        body, name=f"outproj_l{l}", grid=(s // tm,),
        in_specs=[row(D_MODEL), row(D_BRANCH), row(D_BRANCH), zcol(3), zcol(7), row(PLE_DIM),
                  _full((D_MODEL, D_MODEL)), _full((1, D_MODEL)), _full((D_MODEL, D_MODEL)),
                  _full((PLE_DIM, D_MODEL))],
        out_specs=[row(D_MODEL)] * 4, out_shape=[f(), f(), f(), f()],
        compiler_params=_params(1),
    )(h, oa, ob, z, z, p_i, w_out, ple_g, w_gate, w_ple)


def _loss_head(y, target, tm):
    s = y.shape[0]

    def body(y_ref, t_ref, acc_ref, dy_ref):
        @pl.when(pl.program_id(0) == 0)
        def _():
            acc_ref[...] = jnp.zeros_like(acc_ref)

        err = y_ref[...] - t_ref[...]
        dy_ref[...] = err * (1.0 / D_MODEL)
        e2 = err * err
        part = jnp.zeros((8, LANE), F32)
        for r in range(tm // 8):
            for c in range(D_MODEL // LANE):
                part = part + e2[8 * r:8 * (r + 1), LANE * c:LANE * (c + 1)]
        acc_ref[...] += part

    row = pl.BlockSpec((tm, D_MODEL), lambda i: (i, 0))
    return pl.pallas_call(
        body, name="loss_head", grid=(s // tm,), in_specs=[row, row],
        out_specs=[_full((8, LANE)), row],
        out_shape=[jax.ShapeDtypeStruct((8, LANE), F32), jax.ShapeDtypeStruct((s, D_MODEL), F32)],
        compiler_params=_params(1),
    )(y, target)


def _outproj_bwd(l, dout, h2, e, gate, p_i, oa, ob, z, w_out_t, ple_g, w_gate_t, hsel, tm, comm=None):
    s = dout.shape[0]

    def body(do_ref, h2_ref, e_ref, gate_ref, p_ref, oa_ref, ob_ref, ga_ref, gb_ref, wot_ref, pg_ref,
             wgt_ref, hsel_ref,
             dh2_ref, doa_ref, dob_ref, dga_ref, dgb_ref, delta_ref, dwo_ref, dwg_ref, dwp_ref, dpg_ref):
        @pl.when(pl.program_id(0) == 0)
        def _():
            dwo_ref[...] = jnp.zeros_like(dwo_ref)
            dwg_ref[...] = jnp.zeros_like(dwg_ref)
            dwp_ref[...] = jnp.zeros_like(dwp_ref)
            dpg_ref[...] = jnp.zeros_like(dpg_ref)

        dho = do_ref[...]
        g = gate_ref[...]
        de = (dho * g).astype(BF16)
        dwp_ref[...] += _dot_tn(p_ref[...].astype(BF16), de)
        dpre = (dho * e_ref[...] * g * (1.0 - g)).astype(BF16)
        h2 = h2_ref[...]
        pg = pg_ref[...]
        r = lax.rsqrt(jnp.mean(h2 * h2, axis=-1, keepdims=True) + EPS)
        n2 = (h2 * r * pg).astype(BF16)
        dwg_ref[...] += _dot_tn(n2, dpre)
        dn2 = _dot(dpre, wgt_ref[...])
        dpg_ref[0:1, :] += jnp.sum(dn2 * h2 * r, axis=0, keepdims=True)
        wv = dn2 * pg
        dh2 = dho + r * wv - h2 * (r * r * r) * jnp.mean(wv * h2, axis=-1, keepdims=True)
        dh2_ref[...] = dh2
        dh2b = dh2.astype(BF16)
        ga, gb, oa, ob = ga_ref[...], gb_ref[...], oa_ref[...], ob_ref[...]
        sga, sgb = _sigmoid(ga), _sigmoid(gb)
        a = jnp.concatenate([oa * ga * sga, ob * gb * sgb], axis=1).astype(BF16)
        dwo_ref[...] += _dot_tn(a, dh2b)
        da = _dot(dh2b, wot_ref[...])
        da_a, da_b = da[:, :D_BRANCH], da[:, D_BRANCH:]
        doa = da_a * ga * sga
        dob = da_b * gb * sgb
        doa_ref[...] = doa.astype(BF16)
        dob_ref[...] = dob.astype(BF16)
        dga_ref[...] = (da_a * oa * sga * (1.0 + ga * (1.0 - sga))).astype(BF16)
        dgb_ref[...] = (da_b * ob * sgb * (1.0 + gb * (1.0 - sgb))).astype(BF16)
        prod = jnp.concatenate([doa * oa, dob * ob], axis=1)
        dt = _split_dot(prod, hsel_ref[...]).T
        for pp in range(2 * N_PAIRS):
            delta_ref[pp, :, :] = dt[2 * pp:2 * pp + 2, :]

    row = lambda w: pl.BlockSpec((tm, w), lambda i: (i, 0))
    zcol = lambda k: pl.BlockSpec((tm, D_BRANCH), lambda i: (i, k))
    grid = (s // tm,)
    c_in, c_ispec, c_ospec, c_oshape, c_scr = _comm_args(comm)
    return pl.pallas_call(
        _fuse_comm(body, 13, 10, comm, grid), name=f"outproj_bwd_l{l}", grid=grid,
        in_specs=[row(D_MODEL)] * 4 + [row(PLE_DIM), row(D_BRANCH), row(D_BRANCH), zcol(3), zcol(7),
                                        _full((D_MODEL, D_MODEL)), _full((1, D_MODEL)), _full((D_MODEL, D_MODEL)),
                                        _full((2 * D_BRANCH, LANE))] + c_ispec,
        out_specs=[row(D_MODEL)] + [row(D_BRANCH)] * 4
        + [pl.BlockSpec((2 * N_PAIRS, 2, tm), lambda i: (0, 0, i)), _full((D_MODEL, D_MODEL)),
           _full((D_MODEL, D_MODEL)), _full((PLE_DIM, D_MODEL)), _full((8, D_MODEL))] + c_ospec,
        out_shape=[jax.ShapeDtypeStruct((s, D_MODEL), F32)] + [jax.ShapeDtypeStruct((s, D_BRANCH), BF16)] * 4
        + [jax.ShapeDtypeStruct((2 * N_PAIRS, 2, s), F32), jax.ShapeDtypeStruct((D_MODEL, D_MODEL), F32),
           jax.ShapeDtypeStruct((D_MODEL, D_MODEL), F32), jax.ShapeDtypeStruct((PLE_DIM, D_MODEL), F32),
           jax.ShapeDtypeStruct((8, D_MODEL), F32)] + c_oshape,
        scratch_shapes=c_scr, compiler_params=_params(1),
    )(dout, h2, e, gate, p_i, oa, ob, z, z, w_out_t, ple_g, w_gate_t, hsel, *c_in)


def _inproj_bwd_prep(l, z, dqt_a, dk_a, dv_a, dqt_b, dk_b, dv_b, dga, dgb, dlogf, b_f, qkg, bsum, rope, tm):
    s = z.shape[0]
    rc, rs1, rs2 = rope

    def body(z_ref, dqta_ref, dka_ref, dva_ref, dqtb_ref, dkb_ref, dvb_ref, dga_ref, dgb_ref, dlf_ref,
             bf_ref, qkg_ref, bsum_ref, rc_ref, rs1_ref, rs2_ref, dz_ref, dqkg_ref, dbf_ref):
        @pl.when(pl.program_id(0) == 0)
        def _():
            dqkg_ref[...] = jnp.zeros_like(dqkg_ref)
            dbf_ref[...] = jnp.zeros_like(dbf_ref)

        bs = bsum_ref[...]
        c, s1, s2 = rc_ref[...], rs1_ref[...], rs2_ref[...]

        def unrope(dy):
            return jnp.concatenate([_rope_bwd(dy[:, LANE * k:LANE * (k + 1)], c, s1, s2)
                                    for k in range(D_BRANCH // LANE)], axis=1)

        def norm_bwd(k, row, dy):
            x = z_ref[:, D_BRANCH * k:D_BRANCH * (k + 1)]
            r = lax.rsqrt(_head_sums(x * x, bs) * (1.0 / HEAD_DIM) + EPS)
            dqkg_ref[row:row + 1, :] += jnp.sum(dy * x * r, axis=0, keepdims=True)
            w = dy * qkg_ref[row:row + 1, :]
            dx = r * w - x * (r * r * r) * (_head_sums(w * x, bs) * (1.0 / HEAD_DIM))
            dz_ref[:, D_BRANCH * k:D_BRANCH * (k + 1)] = dx.astype(BF16)

        norm_bwd(0, 0, dqta_ref[...].T * Q_SCALE)
        norm_bwd(1, 1, dka_ref[...] * LN2)
        dz_ref[:, 2 * D_BRANCH:3 * D_BRANCH] = dva_ref[...].astype(BF16)
        dz_ref[:, 3 * D_BRANCH:4 * D_BRANCH] = dga_ref[...]
        norm_bwd(4, 2, unrope(dqtb_ref[...].T * Q_SCALE))
        norm_bwd(5, 3, unrope(dkb_ref[...] * LN2))
        dz_ref[:, 6 * D_BRANCH:7 * D_BRANCH] = dvb_ref[...].astype(BF16)
        dz_ref[:, 7 * D_BRANCH:8 * D_BRANCH] = dgb_ref[...]
        dfa = dlf_ref[...] * _sigmoid(-(z_ref[:, N_MAIN:N_ALL] + bf_ref[...]))
        dz_ref[:, N_MAIN:N_ALL] = dfa.astype(BF16)
        dbf_ref[0:1, :] += jnp.sum(dfa, axis=0, keepdims=True)

    row = lambda w: pl.BlockSpec((tm, w), lambda i: (i, 0))
    colt = pl.BlockSpec((D_BRANCH, tm), lambda i: (0, i))
    return pl.pallas_call(
        body, name=f"inproj_bwd_prep_l{l}", grid=(s // tm,),
        in_specs=[row(N_ALL), colt, row(D_BRANCH), row(D_BRANCH), colt, row(D_BRANCH), row(D_BRANCH),
                  row(D_BRANCH), row(D_BRANCH), row(LANE), _full((1, LANE)), _full((8, D_BRANCH)),
                  _full((HEADS_PER_BLOCK * HEAD_DIM,) * 2), row(LANE), row(LANE), row(LANE)],
        out_specs=[row(N_ALL), _full((8, D_BRANCH)), _full((8, LANE))],
        out_shape=[jax.ShapeDtypeStruct((s, N_ALL), BF16), jax.ShapeDtypeStruct((8, D_BRANCH), F32),
                   jax.ShapeDtypeStruct((8, LANE), F32)],
        compiler_params=_params(1),
    )(z, dqt_a, dk_a, dv_a, dqt_b, dk_b, dv_b, dga, dgb, dlogf, b_f, qkg, bsum, rc, rs1, rs2)


def _inproj_bwd_dx(l, dz, w_all_t, h, norm_g, dh2, tm):
    s = dz.shape[0]

    def body(dz_ref, wt_ref, h_ref, g_ref, dh2_ref, dh_ref, dg_ref):
        @pl.when(pl.program_id(0) == 0)
        def _():
            dg_ref[...] = jnp.zeros_like(dg_ref)

        du = _dot(dz_ref[...], wt_ref[...])
        hh = h_ref[...]
        g = g_ref[...]
        r = lax.rsqrt(jnp.mean(hh * hh, axis=-1, keepdims=True) + EPS)
        dg_ref[0:1, :] += jnp.sum(du * hh * r, axis=0, keepdims=True)
        wv = du * g
        dh_ref[...] = dh2_ref[...] + r * wv - hh * (r * r * r) * jnp.mean(wv * hh, axis=-1, keepdims=True)

    row = lambda w: pl.BlockSpec((tm, w), lambda i: (i, 0))
    return pl.pallas_call(
        body, name=f"inproj_bwd_dx_l{l}", grid=(s // tm,),
        in_specs=[row(N_ALL), _full((N_ALL, D_MODEL)), row(D_MODEL), _full((1, D_MODEL)), row(D_MODEL)],
        out_specs=[row(D_MODEL), _full((8, D_MODEL))],
        out_shape=[jax.ShapeDtypeStruct((s, D_MODEL), F32), jax.ShapeDtypeStruct((8, D_MODEL), F32)],
        compiler_params=_params(1),
    )(dz, w_all_t, h, norm_g, dh2)


def _inproj_bwd_dw(l, u, dz, tm, tn):
    s = u.shape[0]

    def body(u_ref, dz_ref, dw_ref):
        @pl.when(pl.program_id(1) == 0)
        def _():
            dw_ref[...] = jnp.zeros_like(dw_ref)

        dw_ref[...] += _dot_tn(u_ref[...], dz_ref[...])

    return pl.pallas_call(
        body, name=f"inproj_bwd_dw_l{l}", grid=(N_ALL // tn, s // tm),
        in_specs=[pl.BlockSpec((tm, D_MODEL), lambda n, i: (i, 0)), pl.BlockSpec((tm, tn), lambda n, i: (i, n))],
        out_specs=pl.BlockSpec((D_MODEL, tn), lambda n, i: (0, n)),
        out_shape=jax.ShapeDtypeStruct((D_MODEL, N_ALL), F32),
        compiler_params=_params(2),
    )(u, dz)


def _adamw_math(w, g, m, v):
    m = ADAM_B1 * m + (1.0 - ADAM_B1) * g
    v = ADAM_B2 * v + (1.0 - ADAM_B2) * (g * g)
    m_hat = m / (1.0 - ADAM_B1 ** ADAM_STEP)
    v_hat = v / (1.0 - ADAM_B2 ** ADAM_STEP)
    delta = -ADAM_LR * (m_hat / (jnp.sqrt(v_hat) + ADAM_EPS) + ADAM_WD * w)
    return delta, m, v


def _adamw(name, w, halves, m, v, core):
    nl, r, c = w.shape
    hr = r // 2
    tr = 128 if hr % 128 == 0 else hr
    nb = hr // tr

    def body(core_ref, w_ref, own0_ref, oth0_ref, own1_ref, oth1_ref, m_ref, v_ref, g_ref, d_ref, nm_ref, nv_ref):
        first = pl.program_id(0) == 0
        own = jnp.where(first, own0_ref[...], own1_ref[...])
        oth = jnp.where(first, oth0_ref[...], oth1_ref[...])
        g = jnp.where(pl.program_id(1) // nb == core_ref[0], own, oth)
        d, nm, nv = _adamw_math(w_ref[...], g, m_ref[...], v_ref[...])
        g_ref[...] = g
        d_ref[...] = d
        nm_ref[...] = nm
        nv_ref[...] = nv

    spec = pl.BlockSpec((None, tr, c), lambda a, b, core_ref: (a, b, 0))
    gspec = pl.BlockSpec((tr, c), lambda a, b, core_ref: (b % nb, 0))
    shp = jax.ShapeDtypeStruct(w.shape, F32)
    return pl.pallas_call(
        body, name=name,
        grid_spec=pltpu.PrefetchScalarGridSpec(
            num_scalar_prefetch=1, grid=(nl, r // tr), in_specs=[spec] + [gspec] * 4 + [spec, spec],
            out_specs=[spec] * 4),
        out_shape=[shp, shp, shp, shp], compiler_params=_params(2),
    )(core, w, halves[0][0], halves[0][1], halves[1][0], halves[1][1], m, v)


def _pair_sum(name, g, x, c, narrow=False):
    n, r, cc = g.shape
    hr = r // 2
    tr = 128 if hr % 128 == 0 else hr
    nb = hr // tr

    def body(c_ref, g_ref, x_ref, o_ref, *narrow_ref):
        total = g_ref[...] + x_ref[...]
        o_ref[...] = total
        if narrow:
            narrow_ref[0][...] = total.astype(BF16)

    spec = pl.BlockSpec((None, tr, cc), lambda i, j, c_ref: (i, j, 0))
    shapes = [jax.ShapeDtypeStruct((n, hr, cc), F32)] + ([jax.ShapeDtypeStruct((n, hr, cc), BF16)] if narrow else [])
    return pl.pallas_call(
        body, name=name,
        grid_spec=pltpu.PrefetchScalarGridSpec(
            num_scalar_prefetch=1, grid=(n, nb),
            in_specs=[pl.BlockSpec((None, tr, cc), lambda i, j, c_ref: (i, c_ref[0] * nb + j, 0)), spec],
            out_specs=[spec] * len(shapes)),
        out_shape=shapes, compiler_params=_params(2),
    )(c, g, x)


def _sum_slots(name, own, landed, chip):
    n, r, c = own.shape
    tr = 128 if r % 128 == 0 else r

    def body(chip_ref, a_ref, b_ref, c_ref, d_ref, o_ref):
        o_ref[...] = ((a_ref[...] + b_ref[...].astype(F32)) + c_ref[...].astype(F32)) + d_ref[...].astype(F32)

    slot = lambda d: pl.BlockSpec((None, tr, c), lambda j, chip_ref: ((chip_ref[0] + d) % n, j, 0))
    return pl.pallas_call(
        body, name=name,
        grid_spec=pltpu.PrefetchScalarGridSpec(
            num_scalar_prefetch=1, grid=(r // tr,), in_specs=[slot(0), slot(1), slot(2), slot(3)],
            out_specs=pl.BlockSpec((tr, c), lambda j, chip_ref: (j, 0))),
        out_shape=jax.ShapeDtypeStruct((r, c), F32), compiler_params=_params(1),
    )(chip, own, landed, landed, landed)


def _me():
    return lax.axis_index("x"), lax.axis_index("y"), lax.axis_index("c")


def _other_chips(x, y):
    return [(1 - x, y), (x, 1 - y), (1 - x, 1 - y)]


def _dma_sems(*counts):
    return [pltpu.SemaphoreType.DMA((n,)) for n in counts]


def _half_rows(rows, which, align):
    return pl.ds(pl.multiple_of(which * (rows // 2), align), rows // 2)


def _gather_first_layer(shards):
    n = len(shards)

    def body(*refs):
        ins, outs, keep, stage = refs[:n], refs[n:2 * n], refs[2 * n:3 * n], refs[3 * n:4 * n]
        ici_send, ici_recv, d2d_send, d2d_recv, local_sems = refs[4 * n:]
        x, y, c = _me()
        k = 2 * x + y
        chips = _other_chips(x, y)
        local, first, passed = [], [], []
        for t in range(n):
            stage[t][...] = ins[t][0].astype(BF16)
            keep[t][...] = ins[t][1].astype(BF16)
            cp = pltpu.make_async_copy(stage[t], outs[t].at[k], local_sems.at[t])
            cp.start()
            local.append(cp)
        for t in range(n):
            mine = _half_rows(shards[t].shape[1], c, 16)
            for j, (px, py) in enumerate(chips):
                cp = pltpu.make_async_remote_copy(
                    src_ref=stage[t].at[mine], dst_ref=outs[t].at[k, mine], send_sem=ici_send.at[3 * t + j],
                    recv_sem=ici_recv.at[3 * t + j], device_id=(px, py, c), device_id_type=MESH)
                cp.start()
                first.append(cp)
        for t in range(n):
            mine = _half_rows(shards[t].shape[1], c, 16)
            for j, (px, py) in enumerate(chips):
                landed = outs[t].at[2 * px + py, mine]
                first[3 * t + j].wait_recv()
                cp = pltpu.make_async_remote_copy(
                    src_ref=landed, dst_ref=landed, send_sem=d2d_send.at[3 * t + j],
                    recv_sem=d2d_recv.at[3 * t + j], device_id=(x, y, 1 - c), device_id_type=MESH)
                cp.start()
                passed.append(cp)
        for cp in passed:
            cp.wait_recv()
        for cp in first + passed:
            cp.wait_send()
        for cp in local:
            cp.wait()

    return pl.pallas_call(
        body, name="gather_first_layer",
        in_specs=[VMEM_SPEC] * n, out_specs=[ANY] * n + [VMEM_SPEC] * n,
        out_shape=[jax.ShapeDtypeStruct((4,) + s.shape[1:], BF16) for s in shards]
        + [jax.ShapeDtypeStruct(s.shape[1:], BF16) for s in shards],
        scratch_shapes=[pltpu.VMEM(s.shape[1:], BF16) for s in shards] + _dma_sems(3 * n, 3 * n, 3 * n, 3 * n, n),
        compiler_params=pltpu.CompilerParams(vmem_limit_bytes=VMEM_LIMIT),
    )(*shards)


def _run_comm(name, comm):
    nci, nco = len(comm.ins), len(comm.out_shapes)

    def body(*refs):
        copies = comm.make(refs[:nci], refs[nci:nci + nco], refs[nci + nco:])
        for cp in copies:
            cp.start()
        for cp in copies:
            cp.wait()

    return pl.pallas_call(body, name=name, in_specs=[ANY] * nci, out_specs=[ANY] * nco,
                          out_shape=list(comm.out_shapes), scratch_shapes=list(comm.sems))(*comm.ins)


def _gather_comm(mine):
    n = len(mine)

    def make(ins, outs, sems):
        send_sems, recv_sems, local_sems = sems
        x, y, c = _me()
        k = 2 * x + y
        copies = []
        for t in range(n):
            copies.append(pltpu.make_async_copy(ins[t], outs[t].at[k], local_sems.at[t]))
            for j, (px, py) in enumerate(_other_chips(x, y)):
                copies.append(pltpu.make_async_remote_copy(
                    src_ref=ins[t], dst_ref=outs[t].at[k], send_sem=send_sems.at[3 * t + j],
                    recv_sem=recv_sems.at[3 * t + j], device_id=(px, py, c), device_id_type=MESH))
        return copies

    return _Comm(mine, [jax.ShapeDtypeStruct((4,) + a.shape, a.dtype) for a in mine], _dma_sems(3 * n, 3 * n, n), make)


def _swap_comm(grads):
    n = len(grads)

    def make(ins, outs, sems):
        send_sems, recv_sems = sems
        x, y, c = _me()
        return [pltpu.make_async_remote_copy(
            src_ref=ins[t].at[:, _half_rows(grads[t].shape[1], 1 - c, 8)], dst_ref=outs[t],
            send_sem=send_sems.at[t], recv_sem=recv_sems.at[t], device_id=(x, y, 1 - c), device_id_type=MESH)
            for t in range(n)]

    shapes = [jax.ShapeDtypeStruct((g.shape[0], g.shape[1] // 2, g.shape[2]), F32) for g in grads]
    return _Comm(grads, shapes, _dma_sems(n, n), make)


def _scatter_comm(parts):
    n = len(parts)

    def make(ins, outs, sems):
        send_sems, recv_sems = sems
        x, y, c = _me()
        k = 2 * x + y
        return [pltpu.make_async_remote_copy(
            src_ref=ins[t].at[2 * px + py], dst_ref=outs[t].at[k], send_sem=send_sems.at[3 * t + j],
            recv_sem=recv_sems.at[3 * t + j], device_id=(px, py, c), device_id_type=MESH)
            for t in range(n) for j, (px, py) in enumerate(_other_chips(x, y))]

    return _Comm(parts, [jax.ShapeDtypeStruct(p.shape, p.dtype) for p in parts], _dma_sems(3 * n, 3 * n), make)


def _share_comm(totals):
    n = len(totals)

    def make(ins, outs, sems):
        send_sems, recv_sems = sems
        x, y, c = _me()
        return [pltpu.make_async_remote_copy(
            src_ref=ins[t], dst_ref=outs[t], send_sem=send_sems.at[t], recv_sem=recv_sems.at[t],
            device_id=(x, y, 1 - c), device_id_type=MESH) for t in range(n)]

    return _Comm(totals, [jax.ShapeDtypeStruct(t.shape, F32) for t in totals], _dma_sems(n, n), make)


def _small_allreduce_adamw(part, w, m, v):
    shape = part.shape

    def body(part_ref, w_ref, m_ref, v_ref, g_ref, d_ref, nm_ref, nv_ref, slots, send_sems, recv_sems):
        x, y, c = _me()
        me = 4 * x + 2 * y + c
        slots[me] = part_ref[...]
        copies = []
        for d in range(1, 8):
            peer = (x ^ (d >> 2), y ^ ((d >> 1) & 1), c ^ (d & 1))
            cp = pltpu.make_async_remote_copy(
                src_ref=part_ref, dst_ref=slots.at[me], send_sem=send_sems.at[d - 1], recv_sem=recv_sems.at[d - 1],
                device_id=peer, device_id_type=MESH)
            cp.start()
            copies.append(cp)
        for cp in copies:
            cp.wait()
        g = slots[0]
        for i in range(1, 8):
            g = g + slots[i]
        g_ref[...] = g
        d, nm, nv = _adamw_math(w_ref[...], g, m_ref[...], v_ref[...])
        d_ref[...] = d
        nm_ref[...] = nm
        nv_ref[...] = nv

    shp = jax.ShapeDtypeStruct(shape, F32)
    return pl.pallas_call(
        body, name="small_allreduce_adamw", in_specs=[VMEM_SPEC] * 4, out_specs=[VMEM_SPEC] * 4,
        out_shape=[shp, shp, shp, shp],
        scratch_shapes=[pltpu.VMEM((8,) + shape, F32), pltpu.SemaphoreType.DMA((7,)), pltpu.SemaphoreType.DMA((7,))],
    )(part, w, m, v)


TM = 256
T_FOX = 1024
T_DIL_FWD = 1024
T_DIL_BWD = 512
T_SCAN = 512
TN_DW = 1408


def _layer_fwd(l, h, p_i, wts, consts, comm=None):
    w_all, _, w_out, _, w_gate, _, w_ple, norm_g, b_f, qkg, ple_g = wts
    bsum, _, bias_t, _, rope = consts
    u, z, qa, ka, va, qb, kb, vb, logf, va_t, vb_t = _inproj(l, h, norm_g, w_all, b_f, qkg, bsum, rope, TM)
    c_spread, c_t = _cumsum_gates(l, logf, T_SCAN)
    oa, lse_a, *landed = _attn_fwd(f"fox_fwd_l{l}", True, qa, ka, va_t, (c_spread, c_t), T_FOX, comm)
    ob, lse_b = _attn_fwd(f"dil_fwd_l{l}", False, qb, kb, vb_t, (bias_t,), T_DIL_FWD)
    h2, e, gate, out = _outproj(l, h, oa, ob, z, p_i, w_out, ple_g, w_gate, w_ple, TM)
    saved = (h, u, z, qa, ka, va, qb, kb, vb, c_spread, c_t, oa, lse_a, ob, lse_b, h2, e, gate)
    return out, saved, landed


def _reduce_names(tag):
    return [f"reduce_{tag}_{w}" for w in ("w_in", "w_out", "w_ple", "w_gate")]


def _layer_bwd(l, dout, p_i, wts, consts, saved, pending=None, core=None, chip=None):
    _, w_all_t, _, w_out_t, _, w_gate_t, _, norm_g, b_f, qkg, ple_g = wts
    bsum, hsel, _, bias_t, rope = consts
    h, u, z, qa, ka, va, qb, kb, vb, c_spread, c_t, oa, lse_a, ob, lse_b, h2, e, gate = saved
    fused = pending is not None
    dh2, doa, dob, dga, dgb, delta_t, dw_out, dw_gate, dw_ple, dple_g, *sib = _outproj_bwd(
        l, dout, h2, e, gate, p_i, oa, ob, z, w_out_t, ple_g, w_gate_t, hsel, TM,
        _swap_comm(pending) if fused else None)
    if fused:
        pair = [_pair_sum(n, g, x, core)[0] for n, g, x in zip(_reduce_names(f"pair_l{l + 1}"), pending, sib)]
    dqt_a, dk_a, dv_a, dc, drow, *landed = _attn_bwd(
        f"fox_bwd_l{l}", True, qa, ka, va, doa, lse_a, delta_t, 0, (c_spread, c_t), T_FOX,
        _scatter_comm(pair) if fused else None)
    if fused:
        totals = [_sum_slots(n, a, y, chip) for n, a, y in zip(_reduce_names(f"chips_l{l + 1}"), pair, landed)]
    dqt_b, dk_b, dv_b, *other = _attn_bwd(f"dil_bwd_l{l}", False, qb, kb, vb, dob, lse_b, delta_t, N_PAIRS,
                                            (bias_t,), T_DIL_BWD, _share_comm(totals) if fused else None)
    dlogf = _rev_cumsum_gates(l, dc, drow, T_SCAN)
    dz, dqkg, dbf = _inproj_bwd_prep(l, z, dqt_a, dk_a, dv_a, dqt_b, dk_b, dv_b, dga, dgb, dlogf, b_f, qkg,
                                     bsum, rope, TM)
    dh, dnorm_g = _inproj_bwd_dx(l, dz, w_all_t, h, norm_g, dh2, TM)
    dw_all = _inproj_bwd_dw(l, u, dz, TM * 2, TN_DW)
    reduced = list(zip(totals, other)) if fused else None
    return dh, (dw_all, dw_out, dw_ple, dw_gate, dnorm_g[0], dbf[0, :N_HEADS], dqkg[:4], dple_g[0]), reduced


def _reduce_last(grads, core, chip, l):
    sib = _run_comm(f"reduce_swap_l{l}", _swap_comm(grads))
    pair = [_pair_sum(n, g, x, core, narrow=True) for n, g, x in zip(_reduce_names(f"pair_l{l}"), grads, sib)]
    landed = _run_comm(f"reduce_scatter_l{l}", _scatter_comm([p[1] for p in pair]))
    totals = [_sum_slots(n, p[0], y, chip) for n, p, y in zip(_reduce_names(f"chips_l{l}"), pair, landed)]
    return list(zip(totals, _run_comm(f"reduce_share_l{l}", _share_comm(totals))))


N_FA = 2048


def _layer_weights(l, gathered, norm_g, b_f, qk_norm_g, ple_norm_g):
    g_in, g_out, g_ple, g_gate = gathered
    w_in = jnp.transpose(g_in, (1, 0, 2)).reshape(D_MODEL, N_IN)
    w_all = jnp.concatenate([w_in[:, :N_FA], w_in[:, N_FA + N_HEADS:],
                             jnp.pad(w_in[:, N_FA:N_FA + N_HEADS], ((0, 0), (0, LANE - N_HEADS)))], axis=1)
    w_out = g_out.reshape(D_MODEL, D_MODEL)
    w_gate = g_gate.reshape(D_MODEL, D_MODEL)
    w_ple = jnp.transpose(g_ple, (1, 0, 2)).reshape(PLE_DIM, D_MODEL)
    qkg = jnp.pad(jnp.tile(qk_norm_g[l], (1, N_HEADS)), ((0, 4), (0, 0)))
    bf = jnp.pad(b_f[l], (0, LANE - N_HEADS))[None, :]
    return (w_all, w_all.T, w_out, w_out.T, w_gate, w_gate.T, w_ple, norm_g[l][None, :], bf, qkg,
            ple_norm_g[l][None, :])


def _slot_layout(dw_all, dw_out, dw_ple, dw_gate):
    dw_in = jnp.concatenate([dw_all[:, :N_FA], dw_all[:, N_MAIN:N_MAIN + N_HEADS], dw_all[:, N_FA:N_MAIN]], axis=1)
    return (jnp.transpose(dw_in.reshape(D_MODEL, 4, N_IN // 4), (1, 0, 2)),
            dw_out.reshape(4, D_MODEL // 4, D_MODEL),
            jnp.transpose(dw_ple.reshape(PLE_DIM, 4, D_MODEL // 4), (1, 0, 2)),
            dw_gate.reshape(4, D_MODEL // 4, D_MODEL))


SMALL_ROWS = 40


def _pack_small(norm_g, ple_norm_g, qk_norm_g, b_f, last=0.0):
    flat = jnp.concatenate([norm_g.reshape(-1), ple_norm_g.reshape(-1), qk_norm_g.reshape(-1), b_f.reshape(-1)])
    flat = jnp.pad(flat, (0, SMALL_ROWS * LANE - flat.shape[0] - 1))
    return jnp.concatenate([flat, jnp.reshape(last, (1,)).astype(F32)]).reshape(SMALL_ROWS, LANE)


def _unpack_small(packed):
    flat = packed.reshape(-1)
    n1, n2, n3 = 2 * D_MODEL, 4 * D_MODEL, 4 * D_MODEL + 2 * 4 * HEAD_DIM
    return (flat[:n1].reshape(2, D_MODEL), flat[n1:n2].reshape(2, D_MODEL), flat[n2:n3].reshape(2, 4, HEAD_DIM),
            flat[n3:n3 + 2 * N_HEADS].reshape(2, N_HEADS))


def kernel(x, p, positions, norm_g, w_in, b_f, qk_norm_g, w_out, w_ple, ple_norm_g, w_ple_gate, loss_target,
           m_norm_g, m_w_in, m_b_f, m_qk_norm_g, m_w_out, m_w_ple, m_ple_norm_g, m_w_ple_gate,
           v_norm_g, v_w_in, v_b_f, v_qk_norm_g, v_w_out, v_w_ple, v_ple_norm_g, v_w_ple_gate):
    assert w_in.shape[0] == 2, "the schedule below is written for two layers"
    *first, = _gather_first_layer([w_in, w_out, w_ple, w_ple_gate])
    consts = (_head_block_diag(), _head_select(), _dil_bias(T_DIL_FWD), _dil_bias(T_DIL_BWD),
              _rope_tables(positions[0]))
    small_w = (norm_g, b_f, qk_norm_g, ple_norm_g)
    wts0 = _layer_weights(0, first[:4], *small_w)
    h1, saved0, second = _layer_fwd(0, x[0], p[0, 0], wts0, consts, _gather_comm(first[4:]))
    wts1 = _layer_weights(1, second, *small_w)
    h2, saved1, _ = _layer_fwd(1, h1, p[1, 0], wts1, consts)
    sq, dh = _loss_head(h2, loss_target[0], TM)

    core = lax.axis_index("c").astype(jnp.int32).reshape(1)
    chip = (2 * lax.axis_index("x") + lax.axis_index("y")).astype(jnp.int32).reshape(1)
    dh, grads1, _ = _layer_bwd(1, dh, p[1, 0], wts1, consts, saved1)
    dh, grads0, reduced1 = _layer_bwd(0, dh, p[0, 0], wts0, consts, saved0, _slot_layout(*grads1[:4]), core, chip)
    reduced0 = _reduce_last(_slot_layout(*grads0[:4]), core, chip, 0)
    grad_x = dh[None]
    small = [grads0[4:], grads1[4:]]
    n_layers = 2

    outs = {}
    for t, (name, w, m, v) in enumerate((("w_in", w_in, m_w_in, v_w_in), ("w_out", w_out, m_w_out, v_w_out),
                                         ("w_ple", w_ple, m_w_ple, v_w_ple),
                                         ("w_ple_gate", w_ple_gate, m_w_ple_gate, v_w_ple_gate))):
        outs[name] = tuple(_adamw(f"adamw_{name}", w, (reduced0[t], reduced1[t]), m, v, core))

    part = _pack_small(jnp.stack([s[0] for s in small]), jnp.stack([s[3] for s in small]),
                       jnp.stack([s[2] for s in small]).reshape(n_layers, 4, N_HEADS, HEAD_DIM).sum(axis=2),
                       jnp.stack([s[1] for s in small]), 0.5 / D_MODEL * jnp.sum(sq))
    packed = _small_allreduce_adamw(part, _pack_small(norm_g, ple_norm_g, qk_norm_g, b_f),
                                    _pack_small(m_norm_g, m_ple_norm_g, m_qk_norm_g, m_b_f),
                                    _pack_small(v_norm_g, v_ple_norm_g, v_qk_norm_g, v_b_f))
    loss = packed[0][SMALL_ROWS - 1, LANE - 1]
    sm = [_unpack_small(a) for a in packed]
    for i, name in enumerate(("norm_g", "ple_norm_g", "qk_norm_g", "b_f")):
        outs[name] = tuple(sm[j][i] for j in range(4))

    order = ("norm_g", "w_in", "b_f", "qk_norm_g", "w_out", "w_ple", "ple_norm_g", "w_ple_gate")
    return (loss, grad_x) + tuple(outs[n][j] for j in range(4) for n in order)
```

```python
import functools
from typing import Any, Callable, NamedTuple, Sequence

import numpy as np
import jax
import jax.numpy as jnp
from jax import lax
from jax.experimental import pallas as pl
from jax.experimental.pallas import tpu as pltpu

F32 = jnp.float32
BF16 = jnp.bfloat16
MESH = pl.DeviceIdType.MESH

D_MODEL = 1024
HEAD_DIM = 64
D_BRANCH = 512
N_HEADS = 8
N_PAIRS = 4
N_IN = 4104
N_MAIN = 4096
N_ALL = 4224
PLE_DIM = 256
ROPE_THETA = 500000.0
ROPE_HALF = 8
EPS = 1e-6
NEG = -1e30
M_INIT = -1e29
Q_SCALE = HEAD_DIM ** -0.5
LOG2E = 1.4426950408889634
LN2 = 0.6931471805599453
DIL_PATTERNS = ((128, 1), (512, 4), (2048, 16))
DIL_BACK = 2048
ADAM_LR, ADAM_B1, ADAM_B2, ADAM_EPS, ADAM_WD, ADAM_STEP = 0.001, 0.9, 0.999, 1e-08, 0.01, 10
VMEM_LIMIT = 56 * 1024 * 1024
LANE = 128


def _dot(a, b):
    return jnp.dot(a, b, preferred_element_type=F32)


def _dot_nt(a, b):
    return lax.dot_general(a, b, (((1,), (1,)), ((), ())), preferred_element_type=F32)


def _dot_tn(a, b):
    return lax.dot_general(a, b, (((0,), (0,)), ((), ())), preferred_element_type=F32)


def _split_dot(x, w):
    hi = x.astype(BF16)
    lo = (x - hi.astype(F32)).astype(BF16)
    return _dot(hi, w) + _dot(lo, w)


def _head_sums(x, bs):
    w = bs.shape[0]
    return jnp.concatenate([_split_dot(x[:, w * k:w * (k + 1)], bs) for k in range(x.shape[1] // w)], axis=1)


def _split3_dot(w, x):
    hi = x.astype(BF16)
    r1 = x - hi.astype(F32)
    mid = r1.astype(BF16)
    lo = (r1 - mid.astype(F32)).astype(BF16)
    return _dot(w, hi) + _dot(w, mid) + _dot(w, lo)


def _sigmoid(x):
    return 1.0 / (1.0 + jnp.exp(-x))


def _params(n_grid):
    return pltpu.CompilerParams(dimension_semantics=("arbitrary",) * n_grid,
                                vmem_limit_bytes=VMEM_LIMIT)


def _full(shape):
    nd = len(shape)
    return pl.BlockSpec(shape, lambda *_: (0,) * nd)


ANY = pl.BlockSpec(memory_space=pl.ANY)
VMEM_SPEC = pl.BlockSpec(memory_space=pltpu.VMEM)


class _Comm(NamedTuple):
    ins: Sequence[Any]
    out_shapes: Sequence[Any]
    sems: Sequence[Any]
    make: Callable[..., Any]


def _fuse_comm(body, n_in, n_out, comm, grid):
    if comm is None:
        return body
    nci, nco, ncs = len(comm.ins), len(comm.out_shapes), len(comm.sems)

    def fused(*refs):
        a, b = n_in + nci, n_in + nci + n_out
        ins, cins, outs, couts = refs[:n_in], refs[n_in:a], refs[a:b], refs[b:b + nco]
        scratch, sems = refs[b + nco:len(refs) - ncs], refs[len(refs) - ncs:]
        first = functools.reduce(jnp.logical_and, [pl.program_id(d) == 0 for d in range(len(grid))])
        last = functools.reduce(jnp.logical_and, [pl.program_id(d) == n - 1 for d, n in enumerate(grid)])

        @pl.when(first)
        def _():
            for cp in comm.make(cins, couts, sems):
                cp.start()

        body(*ins, *outs, *scratch)

        @pl.when(last)
        def _():
            for cp in comm.make(cins, couts, sems):
                cp.wait()

    return fused


def _comm_args(comm):
    if comm is None:
        return [], [], [], [], []
    return (list(comm.ins), [ANY] * len(comm.ins), [ANY] * len(comm.out_shapes), list(comm.out_shapes),
            list(comm.sems))


HEADS_PER_BLOCK = 4


def _head_block_diag():
    i = np.arange(HEADS_PER_BLOCK * HEAD_DIM)
    return jnp.asarray((i[:, None] // HEAD_DIM == i[None, :] // HEAD_DIM).astype(np.float32), BF16)


def _head_select():
    i = np.arange(2 * D_BRANCH)
    j = np.arange(LANE)
    return jnp.asarray((i[:, None] // HEAD_DIM == j[None, :]).astype(np.float32), BF16)


def _dil_bias(t):
    nb = DIL_BACK // t + 1
    qi = np.arange(t)[:, None]
    ki = np.arange(t)[None, :]
    tiles = []
    for r in range(nb):
        d = r * t + qi - ki
        mult = np.zeros((t, t), np.int64)
        for window, dil in DIL_PATTERNS:
            mult += ((d >= 0) & (d <= window) & (d % dil == 0)).astype(np.int64)
        b = np.where(mult > 0, np.log2(np.maximum(mult, 1)), NEG).astype(np.float32)
        tiles.append(b.T)
    return jnp.asarray(np.stack(tiles))


def _rope_tables(positions):
    inv_freq = ROPE_THETA ** (-jnp.arange(ROPE_HALF, dtype=F32) / ROPE_HALF)
    ang = positions.astype(F32)[:, None] * inv_freq
    cos, sin = jnp.cos(ang), jnp.sin(ang)
    s = positions.shape[0]
    rest = HEAD_DIM - 2 * ROPE_HALF
    one, zero, zero8 = jnp.ones((s, rest), F32), jnp.zeros((s, rest), F32), jnp.zeros((s, ROPE_HALF), F32)
    c = jnp.concatenate([cos, cos, one], axis=1)
    s1 = jnp.concatenate([zero8, sin, zero], axis=1)
    s2 = jnp.concatenate([-sin, zero8, zero], axis=1)
    return tuple(jnp.tile(t, (1, 2)) for t in (c, s1, s2))


def _rope_fwd(x, c, s1, s2):
    return x * c + pltpu.roll(x, ROPE_HALF, 1) * s1 + pltpu.roll(x, LANE - ROPE_HALF, 1) * s2


def _rope_bwd(dy, c, s1, s2):
    return dy * c + pltpu.roll(dy * s1, LANE - ROPE_HALF, 1) + pltpu.roll(dy * s2, ROPE_HALF, 1)


def _log_sigmoid(x):
    return jnp.minimum(x, 0.0) - jnp.log(1.0 + jnp.exp(-jnp.abs(x)))


def _inproj(l, h, norm_g, w_all, b_f, qkg, bsum, rope, tm):
    s = h.shape[0]
    rc, rs1, rs2 = rope

    def body(h_ref, g_ref, w_ref, bf_ref, qkg_ref, bsum_ref, rc_ref, rs1_ref, rs2_ref,
             u_ref, z_ref, qa_ref, ka_ref, va_ref, qb_ref, kb_ref, vb_ref, lf_ref, vat_ref, vbt_ref):
        hh = h_ref[...]
        r = lax.rsqrt(jnp.mean(hh * hh, axis=-1, keepdims=True) + EPS)
        u = (hh * r * g_ref[...]).astype(BF16)
        u_ref[...] = u
        for k in range(N_ALL // LANE // 3):
            cols = slice(3 * LANE * k, 3 * LANE * (k + 1))
            z_ref[:, cols] = _dot(u, w_ref[:, cols])
        bs = bsum_ref[...]

        def head_norm(x, row):
            ms = _head_sums(x * x, bs) * (1.0 / HEAD_DIM)
            return x * lax.rsqrt(ms + EPS) * qkg_ref[row:row + 1, :]

        def seg(k):
            return z_ref[:, D_BRANCH * k:D_BRANCH * (k + 1)]

        qa_ref[...] = (head_norm(seg(0), 0) * (Q_SCALE * LOG2E)).astype(BF16)
        ka_ref[...] = head_norm(seg(1), 1).astype(BF16)
        va_ref[...] = seg(2).astype(BF16)
        vat_ref[...] = seg(2).T.astype(BF16)
        qn = head_norm(seg(4), 2) * (Q_SCALE * LOG2E)
        kn = head_norm(seg(5), 3)
        c, s1, s2 = rc_ref[...], rs1_ref[...], rs2_ref[...]
        for k in range(D_BRANCH // LANE):
            cols = slice(LANE * k, LANE * (k + 1))
            qb_ref[:, cols] = _rope_fwd(qn[:, cols], c, s1, s2).astype(BF16)
            kb_ref[:, cols] = _rope_fwd(kn[:, cols], c, s1, s2).astype(BF16)
        vb_ref[...] = seg(6).astype(BF16)
        vbt_ref[...] = seg(6).T.astype(BF16)
        lf_ref[...] = _log_sigmoid(z_ref[:, N_MAIN:N_ALL] + bf_ref[...])

    row = lambda w: pl.BlockSpec((tm, w), lambda i: (i, 0))
    colt = pl.BlockSpec((D_BRANCH, tm), lambda i: (0, i))
    bf = lambda: jax.ShapeDtypeStruct((s, D_BRANCH), BF16)
    bft = lambda: jax.ShapeDtypeStruct((D_BRANCH, s), BF16)
    return pl.pallas_call(
        body, name=f"inproj_l{l}", grid=(s // tm,),
        in_specs=[row(D_MODEL), _full((1, D_MODEL)), _full((D_MODEL, N_ALL)), _full((1, LANE)),
                  _full((8, D_BRANCH)), _full((HEADS_PER_BLOCK * HEAD_DIM,) * 2), row(LANE), row(LANE), row(LANE)],
        out_specs=[row(D_MODEL), row(N_ALL)] + [row(D_BRANCH)] * 6 + [row(LANE), colt, colt],
        out_shape=[jax.ShapeDtypeStruct((s, D_MODEL), BF16), jax.ShapeDtypeStruct((s, N_ALL), F32),
                   bf(), bf(), bf(), bf(), bf(), bf(), jax.ShapeDtypeStruct((s, LANE), F32), bft(), bft()],
        compiler_params=_params(1),
    )(h, norm_g, w_all, b_f, qkg, bsum, rc, rs1, rs2)


def _tri(t, upper):
    a = lax.broadcasted_iota(jnp.int32, (t, t), 0)
    b = lax.broadcasted_iota(jnp.int32, (t, t), 1)
    return jnp.where((b >= a) if upper else (b <= a), 1.0, 0.0).astype(BF16)


def _cumsum_gates(l, logf, t):
    s = logf.shape[0]

    def body(lf_ref, cs_ref, ct_ref, carry):
        @pl.when(pl.program_id(0) == 0)
        def _():
            carry[...] = jnp.zeros_like(carry)

        x = lf_ref[...]
        c = _split3_dot(_tri(t, False), x) + carry[0:1, :]
        carry[...] = jnp.broadcast_to(c[t - 1:t, :], carry.shape)
        c = c * LOG2E
        ct = c.T
        for p in range(N_PAIRS):
            cs_ref[:, LANE * p:LANE * (p + 1)] = c if p == 0 else pltpu.roll(c, LANE - 2 * p, 1)
            ct_ref[p, :, :] = ct[2 * p:2 * p + 2, :]

    return pl.pallas_call(
        body, name=f"cumsum_l{l}", grid=(s // t,),
        in_specs=[pl.BlockSpec((t, LANE), lambda i: (i, 0))],
        out_specs=[pl.BlockSpec((t, N_PAIRS * LANE), lambda i: (i, 0)),
                   pl.BlockSpec((N_PAIRS, 2, t), lambda i: (0, 0, i))],
        out_shape=[jax.ShapeDtypeStruct((s, N_PAIRS * LANE), F32), jax.ShapeDtypeStruct((N_PAIRS, 2, s), F32)],
        scratch_shapes=[pltpu.VMEM((8, LANE), F32)],
        compiler_params=_params(1),
    )(logf)


def _rev_cumsum_gates(l, dc_spread, drow, t):
    s = dc_spread.shape[0]
    n = s // t

    def body(dc_ref, drow_ref, out_ref, carry):
        @pl.when(pl.program_id(0) == 0)
        def _():
            carry[...] = jnp.zeros_like(carry)

        lane = lax.broadcasted_iota(jnp.int32, (t, LANE), 1)
        rows = jnp.concatenate([drow_ref[p] for p in range(N_PAIRS)] + [jnp.zeros((LANE - N_HEADS, t), F32)], axis=0)
        x = rows.T
        for p in range(N_PAIRS):
            xp = jnp.where(lane < 2, dc_ref[:, LANE * p:LANE * (p + 1)], 0.0)
            x = x + (xp if p == 0 else pltpu.roll(xp, 2 * p, 1))
        out = _split3_dot(_tri(t, True), x) + carry[0:1, :]
        out_ref[...] = out
        carry[...] = jnp.broadcast_to(out[0:1, :], carry.shape)

    return pl.pallas_call(
        body, name=f"revcumsum_l{l}", grid=(n,),
        in_specs=[pl.BlockSpec((t, N_PAIRS * LANE), lambda i: (n - 1 - i, 0)),
                  pl.BlockSpec((N_PAIRS, 2, t), lambda i: (0, 0, n - 1 - i))],
        out_specs=pl.BlockSpec((t, LANE), lambda i: (n - 1 - i, 0)),
        out_shape=jax.ShapeDtypeStruct((s, LANE), F32),
        scratch_shapes=[pltpu.VMEM((8, LANE), F32)],
        compiler_params=_params(1),
    )(dc_spread, drow)


def _attn_fwd(name, fox, q, k, vt, extra, t, comm=None):
    s = q.shape[0]
    nq = s // t
    nb = DIL_BACK // t + 1

    def body(*refs):
        if fox:
            q_ref, k_ref, vt_ref, ccol_ref, crow_ref, o_ref, lse_ref, m_scr, l_scr, acc_scr = refs
        else:
            q_ref, k_ref, vt_ref, bias_ref, o_ref, lse_ref, m_scr, l_scr, acc_scr = refs
        i = pl.program_id(1)
        lane = lax.broadcasted_iota(jnp.int32, (t, LANE), 1)
        first = lane < HEAD_DIM
        qq = q_ref[...]
        zero = jnp.zeros_like(qq)
        qh = (jnp.where(first, qq, zero), jnp.where(first, zero, qq))
        m_scr[...] = jnp.full(m_scr.shape, M_INIT, F32)
        l_scr[...] = jnp.zeros_like(l_scr)
        acc_scr[...] = jnp.zeros_like(acc_scr)
        ones = jnp.ones((16, t), BF16)

        half = t // 2
        whole, lo, hi = slice(0, t), slice(0, half), slice(half, t)

        def block(j, ksl, qsl, causal):
            nk_, nq_ = ksl.stop - ksl.start, qsl.stop - qsl.start
            rows = pl.ds(pl.multiple_of(j * t + ksl.start, LANE), nk_)
            ks = k_ref[rows, :]
            vts = jnp.concatenate([vt_ref[:, rows], ones[:, :nk_]], axis=0)
            if fox:
                ccol = ccol_ref[rows, :]
            for h in range(2):
                st = _dot_nt(ks, qh[h][qsl, :])
                if fox:
                    st = st + (crow_ref[h:h + 1, qsl] - ccol[:, h:h + 1])
                    if causal:
                        ki = lax.broadcasted_iota(jnp.int32, (nk_, nq_), 0) + ksl.start
                        qi = lax.broadcasted_iota(jnp.int32, (nk_, nq_), 1) + qsl.start
                        st = jnp.where(ki <= qi, st, NEG)
                else:
                    st = st + bias_ref[i - j, ksl, qsl]
                m_old = m_scr[h, :, qsl]
                m_new = jnp.maximum(m_old, jnp.max(st, axis=0, keepdims=True))
                alpha = jnp.exp2(m_old - m_new)
                pb = jnp.exp2(st - m_new).astype(BF16)
                pv = _dot(vts, pb)
                l_scr[h, :, qsl] = alpha * l_scr[h, :, qsl] + pv[LANE:LANE + 1, :]
                acc_scr[h, :, qsl] = alpha * acc_scr[h, :, qsl] + pv[:LANE, :]
                m_scr[h, :, qsl] = m_new

        def full(j, c):
            block(j, whole, whole, False)
            return c

        if fox:
            lax.fori_loop(0, i, full, 0)
        else:
            @pl.when(i >= nb - 1)
            def _():
                block(i - (nb - 1), lo, lo, False)
                block(i - (nb - 1), hi, whole, False)

            lax.fori_loop(jnp.maximum(i - (nb - 2), 0), i, full, 0)
        if fox:
            block(i, whole, whole, True)
        else:
            block(i, lo, lo, False)
            block(i, whole, hi, False)

        sub = lax.broadcasted_iota(jnp.int32, (LANE, t), 0)
        ot = jnp.where(sub < HEAD_DIM, acc_scr[0] / l_scr[0], acc_scr[1] / l_scr[1])
        o_ref[...] = ot.T
        for h in range(2):
            lse_ref[h:h + 1, :] = m_scr[h] + jnp.log2(l_scr[h])

    qspec = pl.BlockSpec((t, LANE), lambda hp, i: (i, hp))
    kspec = pl.BlockSpec((s, LANE), lambda hp, i: (0, hp))
    vtspec = pl.BlockSpec((LANE, s), lambda hp, i: (hp, 0))
    in_specs = [qspec, kspec, vtspec]
    if fox:
        in_specs += [kspec, pl.BlockSpec((None, 2, t), lambda hp, i: (hp, 0, i))]
    else:
        in_specs += [_full((nb, t, t))]
    grid = (N_PAIRS, nq)
    c_in, c_ispec, c_ospec, c_oshape, c_scr = _comm_args(comm)
    return pl.pallas_call(
        _fuse_comm(body, len(in_specs), 2, comm, grid), name=name, grid=grid,
        in_specs=in_specs + c_ispec,
        out_specs=[qspec, pl.BlockSpec((None, 2, t), lambda hp, i: (hp, 0, i))] + c_ospec,
        out_shape=[jax.ShapeDtypeStruct((s, D_BRANCH), F32), jax.ShapeDtypeStruct((N_PAIRS, 2, s), F32)] + c_oshape,
        scratch_shapes=[pltpu.VMEM((2, 1, t), F32), pltpu.VMEM((2, 1, t), F32), pltpu.VMEM((2, LANE, t), F32)]
        + c_scr,
        compiler_params=_params(2),
    )(q, k, vt, *extra, *c_in)


def _attn_bwd(name, fox, q, k, v, do, lse_t, delta_t, pair_offset, extra, t, comm=None):
    s = q.shape[0]
    nk = s // t
    nb = DIL_BACK // t + 1

    def body(*refs):
        if fox:
            (q_ref, k_ref, v_ref, do_ref, lse_ref, delta_ref, ccol_ref, crow_ref,
             dqt_ref, dk_ref, dv_ref, dc_ref, drow_ref) = refs
        else:
            q_ref, k_ref, v_ref, do_ref, lse_ref, delta_ref, bias_ref, dqt_ref, dk_ref, dv_ref = refs
        j = pl.program_id(1)

        @pl.when(j == 0)
        def _():
            dqt_ref[...] = jnp.zeros_like(dqt_ref)
            if fox:
                drow_ref[...] = jnp.zeros_like(drow_ref)

        lane = lax.broadcasted_iota(jnp.int32, (t, LANE), 1)
        first = lane < HEAD_DIM
        ks = k_ref[...]
        vs = v_ref[...]
        kt = ks.astype(F32).T
        sub = lax.broadcasted_iota(jnp.int32, (LANE, t), 0)
        kth = (jnp.where(sub < HEAD_DIM, kt, 0.0).astype(BF16), jnp.where(sub < HEAD_DIM, 0.0, kt).astype(BF16))
        dk_ref[...] = jnp.zeros_like(dk_ref)
        dv_ref[...] = jnp.zeros_like(dv_ref)
        if fox:
            dc_ref[...] = jnp.zeros_like(dc_ref)
            ccol = ccol_ref[...]

        half = t // 2
        whole, lo, hi = slice(0, t), slice(0, half), slice(half, t)

        def block(i, ksl, qsl, causal):
            nk_, nq_ = ksl.stop - ksl.start, qsl.stop - qsl.start
            rows = pl.ds(pl.multiple_of(i * t + qsl.start, LANE), nq_)
            qq = q_ref[rows, :]
            dd = do_ref[rows, :]
            zero = jnp.zeros_like(qq)
            qh = (jnp.where(first[:nq_], qq, zero), jnp.where(first[:nq_], zero, qq))
            dh = (jnp.where(first[:nq_], dd, zero), jnp.where(first[:nq_], zero, dd))
            for h in range(2):
                st = _dot_nt(ks[ksl, :], qh[h])
                if fox:
                    st = st + (crow_ref[h:h + 1, rows] - ccol[ksl, h:h + 1])
                    if causal:
                        ki = lax.broadcasted_iota(jnp.int32, (nk_, nq_), 0) + ksl.start
                        qi = lax.broadcasted_iota(jnp.int32, (nk_, nq_), 1) + qsl.start
                        st = jnp.where(ki <= qi, st, NEG)
                else:
                    st = st + bias_ref[i - j, ksl, qsl]
                pt = jnp.exp2(st - lse_ref[h:h + 1, rows])
                dpt = _dot_nt(vs[ksl, :], dh[h])
                dst = pt * (dpt - delta_ref[h:h + 1, rows])
                dv_ref[ksl, :] += _dot(pt.astype(BF16), dh[h])
                dsb = dst.astype(BF16)
                dk_ref[ksl, :] += _dot(dsb, qh[h])
                dqt_ref[:, rows] += _dot(kth[h][:, ksl], dsb)
                if fox:
                    dc_ref[ksl, :] -= jnp.where(lane[:nk_] == h, jnp.sum(dst, axis=1, keepdims=True), 0.0)
                    drow_ref[h:h + 1, rows] += jnp.sum(dst, axis=0, keepdims=True)

        def full(i, c):
            block(i, whole, whole, False)
            return c

        block(j, lo, whole, True)
        block(j, hi, hi, True)
        if fox:
            lax.fori_loop(j + 1, nk, full, 0)
        else:
            lax.fori_loop(j + 1, jnp.minimum(j + nb - 1, nk), full, 0)

            @pl.when(j + nb - 1 < nk)
            def _():
                block(j + nb - 1, lo, lo, False)
                block(j + nb - 1, hi, whole, False)

    kspec = pl.BlockSpec((t, LANE), lambda hp, j: (j, hp))
    qspec = pl.BlockSpec((s, LANE), lambda hp, j: (0, hp))
    rowspec = pl.BlockSpec((None, 2, s), lambda hp, j: (hp, 0, 0))
    drowspec = pl.BlockSpec((None, 2, s), lambda hp, j: (hp + pair_offset, 0, 0))
    in_specs = [qspec, kspec, kspec, qspec, rowspec, drowspec]
    out_specs = [pl.BlockSpec((LANE, s), lambda hp, j: (hp, 0)), kspec, kspec]
    out_shape = [jax.ShapeDtypeStruct((D_BRANCH, s), F32), jax.ShapeDtypeStruct((s, D_BRANCH), F32),
                 jax.ShapeDtypeStruct((s, D_BRANCH), F32)]
    if fox:
        in_specs += [kspec, rowspec]
        out_specs += [kspec, rowspec]
        out_shape += [jax.ShapeDtypeStruct((s, N_PAIRS * LANE), F32), jax.ShapeDtypeStruct((N_PAIRS, 2, s), F32)]
    else:
        in_specs += [_full((nb, t, t))]
    grid = (N_PAIRS, nk)
    c_in, c_ispec, c_ospec, c_oshape, c_scr = _comm_args(comm)
    return pl.pallas_call(
        _fuse_comm(body, len(in_specs), len(out_specs), comm, grid), name=name, grid=grid,
        in_specs=in_specs + c_ispec, out_specs=out_specs + c_ospec, out_shape=out_shape + c_oshape,
        scratch_shapes=c_scr, compiler_params=_params(2),
    )(q, k, v, do, lse_t, delta_t, *extra, *c_in)


def _silu(x):
    return x * _sigmoid(x)


def _outproj(l, h, oa, ob, z, p_i, w_out, ple_g, w_gate, w_ple, tm, target=None):
    s = h.shape[0]
    last = target is not None

    def body(h_ref, oa_ref, ob_ref, ga_ref, gb_ref, p_ref, wo_ref, pg_ref, wg_ref, wp_ref, *rest):
        if last:
            t_ref, h2_ref, e_ref, gate_ref, out_ref, acc_ref = rest
        else:
            h2_ref, e_ref, gate_ref, out_ref = rest
        a = jnp.concatenate([oa_ref[...] * _silu(ga_ref[...]), ob_ref[...] * _silu(gb_ref[...])], axis=1)
        h2 = h_ref[...] + _dot(a.astype(BF16), wo_ref[...])
        h2_ref[...] = h2
        r = lax.rsqrt(jnp.mean(h2 * h2, axis=-1, keepdims=True) + EPS)
        n2 = (h2 * r * pg_ref[...]).astype(BF16)
        gate = _sigmoid(_dot(n2, wg_ref[...]))
        e = _dot(p_ref[...].astype(BF16), wp_ref[...])
        e_ref[...] = e
        gate_ref[...] = gate
        out = h2 + e * gate
        if not last:
            out_ref[...] = out
            return

        @pl.when(pl.program_id(0) == 0)
        def _():
            acc_ref[...] = jnp.zeros_like(acc_ref)

        err = out - t_ref[...]
        out_ref[...] = err * (1.0 / D_MODEL)
        e2 = err * err
        rows = e2[0:8, :]
        for k in range(1, tm // 8):
            rows = rows + e2[8 * k:8 * (k + 1), :]
        part = rows[:, 0:LANE]
        for k in range(1, D_MODEL // LANE):
            part = part + rows[:, LANE * k:LANE * (k + 1)]
        acc_ref[...] += part

    row = lambda w: pl.BlockSpec((tm, w), lambda i: (i, 0))
    zcol = lambda k: pl.BlockSpec((tm, D_BRANCH), lambda i: (i, k))
    f = lambda: jax.ShapeDtypeStruct((s, D_MODEL), F32)
    return pl.pallas_call(
        body, name=f"outproj_l{l}", grid=(s // tm,),
        in_specs=[row(D_MODEL), row(D_BRANCH), row(D_BRANCH), zcol(3), zcol(7), row(PLE_DIM),
                  _full((D_MODEL, D_MODEL)), _full((1, D_MODEL)), _full((D_MODEL, D_MODEL)),
                  _full((PLE_DIM, D_MODEL))] + ([row(D_MODEL)] if last else []),
        out_specs=[row(D_MODEL)] * 4 + ([_full((8, LANE))] if last else []),
        out_shape=[f(), f(), f(), f()] + ([jax.ShapeDtypeStruct((8, LANE), F32)] if last else []),
        compiler_params=_params(1),
    )(h, oa, ob, z, z, p_i, w_out, ple_g, w_gate, w_ple, *([target] if last else []))


def _outproj_bwd(l, dout, h2, e, gate, p_i, oa, ob, z, w_out_t, ple_g, w_gate_t, hsel, tm, comm=None):
    s = dout.shape[0]

    def body(do_ref, h2_ref, e_ref, gate_ref, p_ref, oa_ref, ob_ref, ga_ref, gb_ref, wot_ref, pg_ref,
             wgt_ref, hsel_ref,
             dh2_ref, doa_ref, dob_ref, dga_ref, dgb_ref, delta_ref, dwo_ref, dwg_ref, dwp_ref, dpg_ref):
        @pl.when(pl.program_id(0) == 0)
        def _():
            dwo_ref[...] = jnp.zeros_like(dwo_ref)
            dwg_ref[...] = jnp.zeros_like(dwg_ref)
            dwp_ref[...] = jnp.zeros_like(dwp_ref)
            dpg_ref[...] = jnp.zeros_like(dpg_ref)

        dho = do_ref[...]
        g = gate_ref[...]
        de = (dho * g).astype(BF16)
        dwp_ref[...] += _dot_tn(p_ref[...].astype(BF16), de)
        dpre = (dho * e_ref[...] * g * (1.0 - g)).astype(BF16)
        h2 = h2_ref[...]
        pg = pg_ref[...]
        r = lax.rsqrt(jnp.mean(h2 * h2, axis=-1, keepdims=True) + EPS)
        n2 = (h2 * r * pg).astype(BF16)
        dwg_ref[...] += _dot_tn(n2, dpre)
        dn2 = _dot(dpre, wgt_ref[...])
        dpg_ref[0:1, :] += jnp.sum(dn2 * h2 * r, axis=0, keepdims=True)
        wv = dn2 * pg
        dh2 = dho + r * wv - h2 * (r * r * r) * jnp.mean(wv * h2, axis=-1, keepdims=True)
        dh2_ref[...] = dh2
        dh2b = dh2.astype(BF16)
        ga, gb, oa, ob = ga_ref[...], gb_ref[...], oa_ref[...], ob_ref[...]
        sga, sgb = _sigmoid(ga), _sigmoid(gb)
        a = jnp.concatenate([oa * ga * sga, ob * gb * sgb], axis=1).astype(BF16)
        dwo_ref[...] += _dot_tn(a, dh2b)
        da = _dot(dh2b, wot_ref[...])
        da_a, da_b = da[:, :D_BRANCH], da[:, D_BRANCH:]
        doa = da_a * ga * sga
        dob = da_b * gb * sgb
        doa_ref[...] = doa.astype(BF16)
        dob_ref[...] = dob.astype(BF16)
        dga_ref[...] = (da_a * oa * sga * (1.0 + ga * (1.0 - sga))).astype(BF16)
        dgb_ref[...] = (da_b * ob * sgb * (1.0 + gb * (1.0 - sgb))).astype(BF16)
        prod = jnp.concatenate([doa * oa, dob * ob], axis=1)
        dt = _split_dot(prod, hsel_ref[...]).T
        for pp in range(2 * N_PAIRS):
            delta_ref[pp, :, :] = dt[2 * pp:2 * pp + 2, :]

    row = lambda w: pl.BlockSpec((tm, w), lambda i: (i, 0))
    zcol = lambda k: pl.BlockSpec((tm, D_BRANCH), lambda i: (i, k))
    grid = (s // tm,)
    c_in, c_ispec, c_ospec, c_oshape, c_scr = _comm_args(comm)
    return pl.pallas_call(
        _fuse_comm(body, 13, 10, comm, grid), name=f"outproj_bwd_l{l}", grid=grid,
        in_specs=[row(D_MODEL)] * 4 + [row(PLE_DIM), row(D_BRANCH), row(D_BRANCH), zcol(3), zcol(7),
                                        _full((D_MODEL, D_MODEL)), _full((1, D_MODEL)), _full((D_MODEL, D_MODEL)),
                                        _full((2 * D_BRANCH, LANE))] + c_ispec,
        out_specs=[row(D_MODEL)] + [row(D_BRANCH)] * 4
        + [pl.BlockSpec((2 * N_PAIRS, 2, tm), lambda i: (0, 0, i)), _full((D_MODEL, D_MODEL)),
           _full((D_MODEL, D_MODEL)), _full((PLE_DIM, D_MODEL)), _full((8, D_MODEL))] + c_ospec,
        out_shape=[jax.ShapeDtypeStruct((s, D_MODEL), F32)] + [jax.ShapeDtypeStruct((s, D_BRANCH), BF16)] * 4
        + [jax.ShapeDtypeStruct((2 * N_PAIRS, 2, s), F32), jax.ShapeDtypeStruct((D_MODEL, D_MODEL), F32),
           jax.ShapeDtypeStruct((D_MODEL, D_MODEL), F32), jax.ShapeDtypeStruct((PLE_DIM, D_MODEL), F32),
           jax.ShapeDtypeStruct((8, D_MODEL), F32)] + c_oshape,
        scratch_shapes=c_scr, compiler_params=_params(1),
    )(dout, h2, e, gate, p_i, oa, ob, z, z, w_out_t, ple_g, w_gate_t, hsel, *c_in)


def _inproj_bwd_prep(l, z, dqt_a, dk_a, dv_a, dqt_b, dk_b, dv_b, dga, dgb, dlogf, b_f, qkg, bsum, rope, tm):
    s = z.shape[0]
    rc, rs1, rs2 = rope

    def body(z_ref, dqta_ref, dka_ref, dva_ref, dqtb_ref, dkb_ref, dvb_ref, dga_ref, dgb_ref, dlf_ref,
             bf_ref, qkg_ref, bsum_ref, rc_ref, rs1_ref, rs2_ref, dz_ref, dqkg_ref, dbf_ref):
        @pl.when(pl.program_id(0) == 0)
        def _():
            dqkg_ref[...] = jnp.zeros_like(dqkg_ref)
            dbf_ref[...] = jnp.zeros_like(dbf_ref)

        bs = bsum_ref[...]
        c, s1, s2 = rc_ref[...], rs1_ref[...], rs2_ref[...]

        def unrope(dy):
            return jnp.concatenate([_rope_bwd(dy[:, LANE * k:LANE * (k + 1)], c, s1, s2)
                                    for k in range(D_BRANCH // LANE)], axis=1)

        def norm_bwd(k, row, dy):
            x = z_ref[:, D_BRANCH * k:D_BRANCH * (k + 1)]
            r = lax.rsqrt(_head_sums(x * x, bs) * (1.0 / HEAD_DIM) + EPS)
            dqkg_ref[row:row + 1, :] += jnp.sum(dy * x * r, axis=0, keepdims=True)
            w = dy * qkg_ref[row:row + 1, :]
            dx = r * w - x * (r * r * r) * (_head_sums(w * x, bs) * (1.0 / HEAD_DIM))
            dz_ref[:, D_BRANCH * k:D_BRANCH * (k + 1)] = dx.astype(BF16)

        norm_bwd(0, 0, dqta_ref[...].T * Q_SCALE)
        norm_bwd(1, 1, dka_ref[...] * LN2)
        dz_ref[:, 2 * D_BRANCH:3 * D_BRANCH] = dva_ref[...].astype(BF16)
        dz_ref[:, 3 * D_BRANCH:4 * D_BRANCH] = dga_ref[...]
        norm_bwd(4, 2, unrope(dqtb_ref[...].T * Q_SCALE))
        norm_bwd(5, 3, unrope(dkb_ref[...] * LN2))
        dz_ref[:, 6 * D_BRANCH:7 * D_BRANCH] = dvb_ref[...].astype(BF16)
        dz_ref[:, 7 * D_BRANCH:8 * D_BRANCH] = dgb_ref[...]
        dfa = dlf_ref[...] * _sigmoid(-(z_ref[:, N_MAIN:N_ALL] + bf_ref[...]))
        dz_ref[:, N_MAIN:N_ALL] = dfa.astype(BF16)
        dbf_ref[0:1, :] += jnp.sum(dfa, axis=0, keepdims=True)

    row = lambda w: pl.BlockSpec((tm, w), lambda i: (i, 0))
    colt = pl.BlockSpec((D_BRANCH, tm), lambda i: (0, i))
    return pl.pallas_call(
        body, name=f"inproj_bwd_prep_l{l}", grid=(s // tm,),
        in_specs=[row(N_ALL), colt, row(D_BRANCH), row(D_BRANCH), colt, row(D_BRANCH), row(D_BRANCH),
                  row(D_BRANCH), row(D_BRANCH), row(LANE), _full((1, LANE)), _full((8, D_BRANCH)),
                  _full((HEADS_PER_BLOCK * HEAD_DIM,) * 2), row(LANE), row(LANE), row(LANE)],
        out_specs=[row(N_ALL), _full((8, D_BRANCH)), _full((8, LANE))],
        out_shape=[jax.ShapeDtypeStruct((s, N_ALL), BF16), jax.ShapeDtypeStruct((8, D_BRANCH), F32),
                   jax.ShapeDtypeStruct((8, LANE), F32)],
        compiler_params=_params(1),
    )(z, dqt_a, dk_a, dv_a, dqt_b, dk_b, dv_b, dga, dgb, dlogf, b_f, qkg, bsum, rc, rs1, rs2)


def _inproj_bwd_dx(l, dz, w_all_t, h, norm_g, dh2, tm):
    s = dz.shape[0]

    def body(dz_ref, wt_ref, h_ref, g_ref, dh2_ref, dh_ref, dg_ref):
        @pl.when(pl.program_id(0) == 0)
        def _():
            dg_ref[...] = jnp.zeros_like(dg_ref)

        du = _dot(dz_ref[...], wt_ref[...])
        hh = h_ref[...]
        g = g_ref[...]
        r = lax.rsqrt(jnp.mean(hh * hh, axis=-1, keepdims=True) + EPS)
        dg_ref[0:1, :] += jnp.sum(du * hh * r, axis=0, keepdims=True)
        wv = du * g
        dh_ref[...] = dh2_ref[...] + r * wv - hh * (r * r * r) * jnp.mean(wv * hh, axis=-1, keepdims=True)

    row = lambda w: pl.BlockSpec((tm, w), lambda i: (i, 0))
    return pl.pallas_call(
        body, name=f"inproj_bwd_dx_l{l}", grid=(s // tm,),
        in_specs=[row(N_ALL), _full((N_ALL, D_MODEL)), row(D_MODEL), _full((1, D_MODEL)), row(D_MODEL)],
        out_specs=[row(D_MODEL), _full((8, D_MODEL))],
        out_shape=[jax.ShapeDtypeStruct((s, D_MODEL), F32), jax.ShapeDtypeStruct((8, D_MODEL), F32)],
        compiler_params=_params(1),
    )(dz, w_all_t, h, norm_g, dh2)


def _inproj_bwd_dw(l, u, dz, tm, tn):
    s = u.shape[0]

    def body(u_ref, dz_ref, dw_ref):
        @pl.when(pl.program_id(1) == 0)
        def _():
            dw_ref[...] = jnp.zeros_like(dw_ref)

        dw_ref[...] += _dot_tn(u_ref[...], dz_ref[...])

    return pl.pallas_call(
        body, name=f"inproj_bwd_dw_l{l}", grid=(N_ALL // tn, s // tm),
        in_specs=[pl.BlockSpec((tm, D_MODEL), lambda n, i: (i, 0)), pl.BlockSpec((tm, tn), lambda n, i: (i, n))],
        out_specs=pl.BlockSpec((D_MODEL, tn), lambda n, i: (0, n)),
        out_shape=jax.ShapeDtypeStruct((D_MODEL, N_ALL), F32),
        compiler_params=_params(2),
    )(u, dz)


def _adamw_math(w, g, m, v):
    m = ADAM_B1 * m + (1.0 - ADAM_B1) * g
    v = ADAM_B2 * v + (1.0 - ADAM_B2) * (g * g)
    m_hat = m / (1.0 - ADAM_B1 ** ADAM_STEP)
    v_hat = v / (1.0 - ADAM_B2 ** ADAM_STEP)
    delta = -ADAM_LR * (m_hat / (jnp.sqrt(v_hat) + ADAM_EPS) + ADAM_WD * w)
    return delta, m, v


def _adamw(name, w, halves, m, v, core):
    nl, r, c = w.shape
    hr = r // 2
    tr = 128 if hr % 128 == 0 else hr
    nb = hr // tr

    def body(core_ref, w_ref, own0_ref, oth0_ref, own1_ref, oth1_ref, m_ref, v_ref, g_ref, d_ref, nm_ref, nv_ref):
        first = pl.program_id(0) == 0
        own = jnp.where(first, own0_ref[...], own1_ref[...])
        oth = jnp.where(first, oth0_ref[...], oth1_ref[...])
        g = jnp.where(pl.program_id(1) // nb == core_ref[0], own, oth)
        d, nm, nv = _adamw_math(w_ref[...], g, m_ref[...], v_ref[...])
        g_ref[...] = g
        d_ref[...] = d
        nm_ref[...] = nm
        nv_ref[...] = nv

    spec = pl.BlockSpec((None, tr, c), lambda a, b, core_ref: (a, b, 0))
    gspec = pl.BlockSpec((tr, c), lambda a, b, core_ref: (b % nb, 0))
    shp = jax.ShapeDtypeStruct(w.shape, F32)
    return pl.pallas_call(
        body, name=name,
        grid_spec=pltpu.PrefetchScalarGridSpec(
            num_scalar_prefetch=1, grid=(nl, r // tr), in_specs=[spec] + [gspec] * 4 + [spec, spec],
            out_specs=[spec] * 4),
        out_shape=[shp, shp, shp, shp], compiler_params=_params(2),
    )(core, w, halves[0][0], halves[0][1], halves[1][0], halves[1][1], m, v)


def _pair_sum(name, g, x, c, narrow=False):
    n, r, cc = g.shape
    hr = r // 2
    tr = 128 if hr % 128 == 0 else hr
    nb = hr // tr

    def body(c_ref, g_ref, x_ref, o_ref, *narrow_ref):
        total = g_ref[...] + x_ref[...]
        o_ref[...] = total
        if narrow:
            narrow_ref[0][...] = total.astype(BF16)

    spec = pl.BlockSpec((None, tr, cc), lambda i, j, c_ref: (i, j, 0))
    shapes = [jax.ShapeDtypeStruct((n, hr, cc), F32)] + ([jax.ShapeDtypeStruct((n, hr, cc), BF16)] if narrow else [])
    return pl.pallas_call(
        body, name=name,
        grid_spec=pltpu.PrefetchScalarGridSpec(
            num_scalar_prefetch=1, grid=(n, nb),
            in_specs=[pl.BlockSpec((None, tr, cc), lambda i, j, c_ref: (i, c_ref[0] * nb + j, 0)), spec],
            out_specs=[spec] * len(shapes)),
        out_shape=shapes, compiler_params=_params(2),
    )(c, g, x)


def _sum_slots(name, own, landed, chip):
    n, r, c = own.shape
    tr = 128 if r % 128 == 0 else r

    def body(chip_ref, a_ref, b_ref, c_ref, d_ref, o_ref):
        o_ref[...] = ((a_ref[...] + b_ref[...].astype(F32)) + c_ref[...].astype(F32)) + d_ref[...].astype(F32)

    slot = lambda d: pl.BlockSpec((None, tr, c), lambda j, chip_ref: ((chip_ref[0] + d) % n, j, 0))
    return pl.pallas_call(
        body, name=name,
        grid_spec=pltpu.PrefetchScalarGridSpec(
            num_scalar_prefetch=1, grid=(r // tr,), in_specs=[slot(0), slot(1), slot(2), slot(3)],
            out_specs=pl.BlockSpec((tr, c), lambda j, chip_ref: (j, 0))),
        out_shape=jax.ShapeDtypeStruct((r, c), F32), compiler_params=_params(1),
    )(chip, own, landed, landed, landed)


def _me():
    return lax.axis_index("x"), lax.axis_index("y"), lax.axis_index("c")


def _other_chips(x, y):
    return [(1 - x, y), (x, 1 - y), (1 - x, 1 - y)]


def _dma_sems(*counts):
    return [pltpu.SemaphoreType.DMA((n,)) for n in counts]


def _half_rows(rows, which, align):
    return pl.ds(pl.multiple_of(which * (rows // 2), align), rows // 2)


def _gather_first_layer(shards):
    n = len(shards)

    def body(*refs):
        ins, outs, keep, stage = refs[:n], refs[n:2 * n], refs[2 * n:3 * n], refs[3 * n:4 * n]
        ici_send, ici_recv, d2d_send, d2d_recv, local_sems = refs[4 * n:]
        x, y, c = _me()
        k = 2 * x + y
        chips = _other_chips(x, y)
        local, first, passed = [], [], []
        for t in range(n):
            stage[t][...] = ins[t][0].astype(BF16)
            keep[t][...] = ins[t][1].astype(BF16)
            cp = pltpu.make_async_copy(stage[t], outs[t].at[k], local_sems.at[t])
            cp.start()
            local.append(cp)
        for t in range(n):
            mine = _half_rows(shards[t].shape[1], c, 16)
            for j, (px, py) in enumerate(chips):
                cp = pltpu.make_async_remote_copy(
                    src_ref=stage[t].at[mine], dst_ref=outs[t].at[k, mine], send_sem=ici_send.at[3 * t + j],
                    recv_sem=ici_recv.at[3 * t + j], device_id=(px, py, c), device_id_type=MESH)
                cp.start()
                first.append(cp)
        for t in range(n):
            mine = _half_rows(shards[t].shape[1], c, 16)
            for j, (px, py) in enumerate(chips):
                landed = outs[t].at[2 * px + py, mine]
                first[3 * t + j].wait_recv()
                cp = pltpu.make_async_remote_copy(
                    src_ref=landed, dst_ref=landed, send_sem=d2d_send.at[3 * t + j],
                    recv_sem=d2d_recv.at[3 * t + j], device_id=(x, y, 1 - c), device_id_type=MESH)
                cp.start()
                passed.append(cp)
        for cp in passed:
            cp.wait_recv()
        for cp in first + passed:
            cp.wait_send()
        for cp in local:
            cp.wait()

    return pl.pallas_call(
        body, name="gather_first_layer",
        in_specs=[VMEM_SPEC] * n, out_specs=[ANY] * n + [VMEM_SPEC] * n,
        out_shape=[jax.ShapeDtypeStruct((4,) + s.shape[1:], BF16) for s in shards]
        + [jax.ShapeDtypeStruct(s.shape[1:], BF16) for s in shards],
        scratch_shapes=[pltpu.VMEM(s.shape[1:], BF16) for s in shards] + _dma_sems(3 * n, 3 * n, 3 * n, 3 * n, n),
        compiler_params=pltpu.CompilerParams(vmem_limit_bytes=VMEM_LIMIT),
    )(*shards)


def _run_comm(name, comm):
    nci, nco = len(comm.ins), len(comm.out_shapes)

    def body(*refs):
        copies = comm.make(refs[:nci], refs[nci:nci + nco], refs[nci + nco:])
        for cp in copies:
            cp.start()
        for cp in copies:
            cp.wait()

    return pl.pallas_call(body, name=name, in_specs=[ANY] * nci, out_specs=[ANY] * nco,
                          out_shape=list(comm.out_shapes), scratch_shapes=list(comm.sems))(*comm.ins)


def _gather_comm(mine):
    n = len(mine)

    def make(ins, outs, sems):
        send_sems, recv_sems, local_sems = sems
        x, y, c = _me()
        k = 2 * x + y
        copies = []
        for t in range(n):
            copies.append(pltpu.make_async_copy(ins[t], outs[t].at[k], local_sems.at[t]))
            for j, (px, py) in enumerate(_other_chips(x, y)):
                copies.append(pltpu.make_async_remote_copy(
                    src_ref=ins[t], dst_ref=outs[t].at[k], send_sem=send_sems.at[3 * t + j],
                    recv_sem=recv_sems.at[3 * t + j], device_id=(px, py, c), device_id_type=MESH))
        return copies

    return _Comm(mine, [jax.ShapeDtypeStruct((4,) + a.shape, a.dtype) for a in mine], _dma_sems(3 * n, 3 * n, n), make)


def _swap_comm(grads):
    n = len(grads)

    def make(ins, outs, sems):
        send_sems, recv_sems = sems
        x, y, c = _me()
        return [pltpu.make_async_remote_copy(
            src_ref=ins[t].at[:, _half_rows(grads[t].shape[1], 1 - c, 8)], dst_ref=outs[t],
            send_sem=send_sems.at[t], recv_sem=recv_sems.at[t], device_id=(x, y, 1 - c), device_id_type=MESH)
            for t in range(n)]

    shapes = [jax.ShapeDtypeStruct((g.shape[0], g.shape[1] // 2, g.shape[2]), F32) for g in grads]
    return _Comm(grads, shapes, _dma_sems(n, n), make)


def _scatter_comm(parts):
    n = len(parts)

    def make(ins, outs, sems):
        send_sems, recv_sems = sems
        x, y, c = _me()
        k = 2 * x + y
        return [pltpu.make_async_remote_copy(
            src_ref=ins[t].at[2 * px + py], dst_ref=outs[t].at[k], send_sem=send_sems.at[3 * t + j],
            recv_sem=recv_sems.at[3 * t + j], device_id=(px, py, c), device_id_type=MESH)
            for t in range(n) for j, (px, py) in enumerate(_other_chips(x, y))]

    return _Comm(parts, [jax.ShapeDtypeStruct(p.shape, p.dtype) for p in parts], _dma_sems(3 * n, 3 * n), make)


def _share_comm(totals):
    n = len(totals)

    def make(ins, outs, sems):
        send_sems, recv_sems = sems
        x, y, c = _me()
        return [pltpu.make_async_remote_copy(
            src_ref=ins[t], dst_ref=outs[t], send_sem=send_sems.at[t], recv_sem=recv_sems.at[t],
            device_id=(x, y, 1 - c), device_id_type=MESH) for t in range(n)]

    return _Comm(totals, [jax.ShapeDtypeStruct(t.shape, F32) for t in totals], _dma_sems(n, n), make)


def _small_allreduce_adamw(part, w, m, v):
    shape = part.shape

    def body(part_ref, w_ref, m_ref, v_ref, g_ref, d_ref, nm_ref, nv_ref, slots, send_sems, recv_sems):
        x, y, c = _me()
        me = 4 * x + 2 * y + c
        slots[me] = part_ref[...]
        copies = []
        for d in range(1, 8):
            peer = (x ^ (d >> 2), y ^ ((d >> 1) & 1), c ^ (d & 1))
            cp = pltpu.make_async_remote_copy(
                src_ref=part_ref, dst_ref=slots.at[me], send_sem=send_sems.at[d - 1], recv_sem=recv_sems.at[d - 1],
                device_id=peer, device_id_type=MESH)
            cp.start()
            copies.append(cp)
        for cp in copies:
            cp.wait()
        g = slots[0]
        for i in range(1, 8):
            g = g + slots[i]
        g_ref[...] = g
        d, nm, nv = _adamw_math(w_ref[...], g, m_ref[...], v_ref[...])
        d_ref[...] = d
        nm_ref[...] = nm
        nv_ref[...] = nv

    shp = jax.ShapeDtypeStruct(shape, F32)
    return pl.pallas_call(
        body, name="small_allreduce_adamw", in_specs=[VMEM_SPEC] * 4, out_specs=[VMEM_SPEC] * 4,
        out_shape=[shp, shp, shp, shp],
        scratch_shapes=[pltpu.VMEM((8,) + shape, F32), pltpu.SemaphoreType.DMA((7,)), pltpu.SemaphoreType.DMA((7,))],
    )(part, w, m, v)


TM = 512
T_FOX = 1024
T_DIL_FWD = 1024
T_DIL_BWD = 512
T_SCAN = 512
TN_DW = 1408


def _layer_fwd(l, h, p_i, wts, consts, comm=None, target=None):
    w_all, _, w_out, _, w_gate, _, w_ple, norm_g, b_f, qkg, ple_g = wts
    bsum, _, bias_t, _, rope = consts
    u, z, qa, ka, va, qb, kb, vb, logf, va_t, vb_t = _inproj(l, h, norm_g, w_all, b_f, qkg, bsum, rope, TM)
    c_spread, c_t = _cumsum_gates(l, logf, T_SCAN)
    oa, lse_a, *landed = _attn_fwd(f"fox_fwd_l{l}", True, qa, ka, va_t, (c_spread, c_t), T_FOX, comm)
    ob, lse_b = _attn_fwd(f"dil_fwd_l{l}", False, qb, kb, vb_t, (bias_t,), T_DIL_FWD)
    h2, e, gate, *out = _outproj(l, h, oa, ob, z, p_i, w_out, ple_g, w_gate, w_ple, TM, target)
    out = out[0] if target is None else tuple(out)
    saved = (h, u, z, qa, ka, va, qb, kb, vb, c_spread, c_t, oa, lse_a, ob, lse_b, h2, e, gate)
    return out, saved, landed


def _reduce_names(tag):
    return [f"reduce_{tag}_{w}" for w in ("w_in", "w_out", "w_ple", "w_gate")]


def _layer_bwd(l, dout, p_i, wts, consts, saved, pending=None, core=None, chip=None):
    _, w_all_t, _, w_out_t, _, w_gate_t, _, norm_g, b_f, qkg, ple_g = wts
    bsum, hsel, _, bias_t, rope = consts
    h, u, z, qa, ka, va, qb, kb, vb, c_spread, c_t, oa, lse_a, ob, lse_b, h2, e, gate = saved
    fused = pending is not None
    dh2, doa, dob, dga, dgb, delta_t, dw_out, dw_gate, dw_ple, dple_g, *sib = _outproj_bwd(
        l, dout, h2, e, gate, p_i, oa, ob, z, w_out_t, ple_g, w_gate_t, hsel, TM,
        _swap_comm(pending) if fused else None)
    if fused:
        pair = [_pair_sum(n, g, x, core)[0] for n, g, x in zip(_reduce_names(f"pair_l{l + 1}"), pending, sib)]
    dqt_a, dk_a, dv_a, dc, drow, *landed = _attn_bwd(
        f"fox_bwd_l{l}", True, qa, ka, va, doa, lse_a, delta_t, 0, (c_spread, c_t), T_FOX,
        _scatter_comm(pair) if fused else None)
    if fused:
        totals = [_sum_slots(n, a, y, chip) for n, a, y in zip(_reduce_names(f"chips_l{l + 1}"), pair, landed)]
    dqt_b, dk_b, dv_b, *other = _attn_bwd(f"dil_bwd_l{l}", False, qb, kb, vb, dob, lse_b, delta_t, N_PAIRS,
                                            (bias_t,), T_DIL_BWD, _share_comm(totals) if fused else None)
    dlogf = _rev_cumsum_gates(l, dc, drow, T_SCAN)
    dz, dqkg, dbf = _inproj_bwd_prep(l, z, dqt_a, dk_a, dv_a, dqt_b, dk_b, dv_b, dga, dgb, dlogf, b_f, qkg,
                                     bsum, rope, TM)
    dh, dnorm_g = _inproj_bwd_dx(l, dz, w_all_t, h, norm_g, dh2, TM)
    dw_all = _inproj_bwd_dw(l, u, dz, TM, TN_DW)
    reduced = list(zip(totals, other)) if fused else None
    return dh, (dw_all, dw_out, dw_ple, dw_gate, dnorm_g[0], dbf[0, :N_HEADS], dqkg[:4], dple_g[0]), reduced


def _reduce_last(grads, core, chip, l):
    sib = _run_comm(f"reduce_swap_l{l}", _swap_comm(grads))
    pair = [_pair_sum(n, g, x, core, narrow=True) for n, g, x in zip(_reduce_names(f"pair_l{l}"), grads, sib)]
    landed = _run_comm(f"reduce_scatter_l{l}", _scatter_comm([p[1] for p in pair]))
    totals = [_sum_slots(n, p[0], y, chip) for n, p, y in zip(_reduce_names(f"chips_l{l}"), pair, landed)]
    return list(zip(totals, _run_comm(f"reduce_share_l{l}", _share_comm(totals))))


N_FA = 2048


def _layer_weights(l, gathered, norm_g, b_f, qk_norm_g, ple_norm_g):
    g_in, g_out, g_ple, g_gate = gathered
    w_in = jnp.transpose(g_in, (1, 0, 2)).reshape(D_MODEL, N_IN)
    w_all = jnp.concatenate([w_in[:, :N_FA], w_in[:, N_FA + N_HEADS:],
                             jnp.pad(w_in[:, N_FA:N_FA + N_HEADS], ((0, 0), (0, LANE - N_HEADS)))], axis=1)
    w_out = g_out.reshape(D_MODEL, D_MODEL)
    w_gate = g_gate.reshape(D_MODEL, D_MODEL)
    w_ple = jnp.transpose(g_ple, (1, 0, 2)).reshape(PLE_DIM, D_MODEL)
    qkg = jnp.pad(jnp.tile(qk_norm_g[l], (1, N_HEADS)), ((0, 4), (0, 0)))
    bf = jnp.pad(b_f[l], (0, LANE - N_HEADS))[None, :]
    return (w_all, w_all.T, w_out, w_out.T, w_gate, w_gate.T, w_ple, norm_g[l][None, :], bf, qkg,
            ple_norm_g[l][None, :])


def _slot_layout(dw_all, dw_out, dw_ple, dw_gate):
    dw_in = jnp.concatenate([dw_all[:, :N_FA], dw_all[:, N_MAIN:N_MAIN + N_HEADS], dw_all[:, N_FA:N_MAIN]], axis=1)
    return (jnp.transpose(dw_in.reshape(D_MODEL, 4, N_IN // 4), (1, 0, 2)),
            dw_out.reshape(4, D_MODEL // 4, D_MODEL),
            jnp.transpose(dw_ple.reshape(PLE_DIM, 4, D_MODEL // 4), (1, 0, 2)),
            dw_gate.reshape(4, D_MODEL // 4, D_MODEL))


SMALL_ROWS = 40


def _pack_small(norm_g, ple_norm_g, qk_norm_g, b_f, last=0.0):
    flat = jnp.concatenate([norm_g.reshape(-1), ple_norm_g.reshape(-1), qk_norm_g.reshape(-1), b_f.reshape(-1)])
    flat = jnp.pad(flat, (0, SMALL_ROWS * LANE - flat.shape[0] - 1))
    return jnp.concatenate([flat, jnp.reshape(last, (1,)).astype(F32)]).reshape(SMALL_ROWS, LANE)


def _unpack_small(packed):
    flat = packed.reshape(-1)
    n1, n2, n3 = 2 * D_MODEL, 4 * D_MODEL, 4 * D_MODEL + 2 * 4 * HEAD_DIM
    return (flat[:n1].reshape(2, D_MODEL), flat[n1:n2].reshape(2, D_MODEL), flat[n2:n3].reshape(2, 4, HEAD_DIM),
            flat[n3:n3 + 2 * N_HEADS].reshape(2, N_HEADS))


def kernel(x, p, positions, norm_g, w_in, b_f, qk_norm_g, w_out, w_ple, ple_norm_g, w_ple_gate, loss_target,
           m_norm_g, m_w_in, m_b_f, m_qk_norm_g, m_w_out, m_w_ple, m_ple_norm_g, m_w_ple_gate,
           v_norm_g, v_w_in, v_b_f, v_qk_norm_g, v_w_out, v_w_ple, v_ple_norm_g, v_w_ple_gate):
    assert w_in.shape[0] == 2, "the schedule below is written for two layers"
    *first, = _gather_first_layer([w_in, w_out, w_ple, w_ple_gate])
    consts = (_head_block_diag(), _head_select(), _dil_bias(T_DIL_FWD), _dil_bias(T_DIL_BWD),
              _rope_tables(positions[0]))
    small_w = (norm_g, b_f, qk_norm_g, ple_norm_g)
    wts0 = _layer_weights(0, first[:4], *small_w)
    h1, saved0, second = _layer_fwd(0, x[0], p[0, 0], wts0, consts, _gather_comm(first[4:]))
    wts1 = _layer_weights(1, second, *small_w)
    (dh, sq), saved1, _ = _layer_fwd(1, h1, p[1, 0], wts1, consts, target=loss_target[0])

    core = lax.axis_index("c").astype(jnp.int32).reshape(1)
    chip = (2 * lax.axis_index("x") + lax.axis_index("y")).astype(jnp.int32).reshape(1)
    dh, grads1, _ = _layer_bwd(1, dh, p[1, 0], wts1, consts, saved1)
    dh, grads0, reduced1 = _layer_bwd(0, dh, p[0, 0], wts0, consts, saved0, _slot_layout(*grads1[:4]), core, chip)
    reduced0 = _reduce_last(_slot_layout(*grads0[:4]), core, chip, 0)
    grad_x = dh[None]
    small = [grads0[4:], grads1[4:]]
    n_layers = 2

    outs = {}
    for t, (name, w, m, v) in enumerate((("w_in", w_in, m_w_in, v_w_in), ("w_out", w_out, m_w_out, v_w_out),
                                         ("w_ple", w_ple, m_w_ple, v_w_ple),
                                         ("w_ple_gate", w_ple_gate, m_w_ple_gate, v_w_ple_gate))):
        outs[name] = tuple(_adamw(f"adamw_{name}", w, (reduced0[t], reduced1[t]), m, v, core))

    part = _pack_small(jnp.stack([s[0] for s in small]), jnp.stack([s[3] for s in small]),
                       jnp.stack([s[2] for s in small]).reshape(n_layers, 4, N_HEADS, HEAD_DIM).sum(axis=2),
                       jnp.stack([s[1] for s in small]), 0.5 / D_MODEL * jnp.sum(sq))
    packed = _small_allreduce_adamw(part, _pack_small(norm_g, ple_norm_g, qk_norm_g, b_f),
                                    _pack_small(m_norm_g, m_ple_norm_g, m_qk_norm_g, m_b_f),
                                    _pack_small(v_norm_g, v_ple_norm_g, v_qk_norm_g, v_b_f))
    loss = packed[0][SMALL_ROWS - 1, LANE - 1]
    sm = [_unpack_small(a) for a in packed]
    for i, name in enumerate(("norm_g", "ple_norm_g", "qk_norm_g", "b_f")):
        outs[name] = tuple(sm[j][i] for j in range(4))

    order = ("norm_g", "w_in", "b_f", "qk_norm_g", "w_out", "w_ple", "ple_norm_g", "w_ple_gate")
    return (loss, grad_x) + tuple(outs[n][j] for j in range(4) for n in order)
```

```python
import functools
from typing import Any, Callable, NamedTuple, Sequence

import numpy as np
import jax
import jax.numpy as jnp
from jax import lax
from jax.experimental import pallas as pl
from jax.experimental.pallas import tpu as pltpu

F32 = jnp.float32
BF16 = jnp.bfloat16
MESH = pl.DeviceIdType.MESH

D_MODEL = 1024
HEAD_DIM = 64
D_BRANCH = 512
N_HEADS = 8
N_PAIRS = 4
N_IN = 4104
N_MAIN = 4096
N_ALL = 4224
PLE_DIM = 256
ROPE_THETA = 500000.0
ROPE_HALF = 8
EPS = 1e-6
NEG = -1e30
M_INIT = -1e29
Q_SCALE = HEAD_DIM ** -0.5
LOG2E = 1.4426950408889634
LN2 = 0.6931471805599453
DIL_PATTERNS = ((128, 1), (512, 4), (2048, 16))
DIL_BACK = 2048
ADAM_LR, ADAM_B1, ADAM_B2, ADAM_EPS, ADAM_WD, ADAM_STEP = 0.001, 0.9, 0.999, 1e-08, 0.01, 10
VMEM_LIMIT = 56 * 1024 * 1024
LANE = 128


def _dot(a, b):
    return jnp.dot(a, b, preferred_element_type=F32)


def _dot_nt(a, b):
    return lax.dot_general(a, b, (((1,), (1,)), ((), ())), preferred_element_type=F32)


def _dot_tn(a, b):
    return lax.dot_general(a, b, (((0,), (0,)), ((), ())), preferred_element_type=F32)


def _split_dot(x, w):
    hi = x.astype(BF16)
    lo = (x - hi.astype(F32)).astype(BF16)
    return _dot(hi, w) + _dot(lo, w)


def _head_sums(x, bs):
    w = bs.shape[0]
    return jnp.concatenate([_split_dot(x[:, w * k:w * (k + 1)], bs) for k in range(x.shape[1] // w)], axis=1)


def _split3_dot(w, x):
    hi = x.astype(BF16)
    r1 = x - hi.astype(F32)
    mid = r1.astype(BF16)
    lo = (r1 - mid.astype(F32)).astype(BF16)
    return _dot(w, hi) + _dot(w, mid) + _dot(w, lo)


def _sigmoid(x):
    return 1.0 / (1.0 + jnp.exp(-x))


def _params(n_grid):
    return pltpu.CompilerParams(dimension_semantics=("arbitrary",) * n_grid,
                                vmem_limit_bytes=VMEM_LIMIT)


def _full(shape):
    nd = len(shape)
    return pl.BlockSpec(shape, lambda *_: (0,) * nd)


ANY = pl.BlockSpec(memory_space=pl.ANY)
VMEM_SPEC = pl.BlockSpec(memory_space=pltpu.VMEM)


class _Comm(NamedTuple):
    ins: Sequence[Any]
    out_shapes: Sequence[Any]
    sems: Sequence[Any]
    make: Callable[..., Any]


def _fuse_comm(body, n_in, n_out, comm, grid):
    if comm is None:
        return body
    nci, nco, ncs = len(comm.ins), len(comm.out_shapes), len(comm.sems)

    def fused(*refs):
        a, b = n_in + nci, n_in + nci + n_out
        ins, cins, outs, couts = refs[:n_in], refs[n_in:a], refs[a:b], refs[b:b + nco]
        scratch, sems = refs[b + nco:len(refs) - ncs], refs[len(refs) - ncs:]
        first = functools.reduce(jnp.logical_and, [pl.program_id(d) == 0 for d in range(len(grid))])
        last = functools.reduce(jnp.logical_and, [pl.program_id(d) == n - 1 for d, n in enumerate(grid)])

        @pl.when(first)
        def _():
            for cp in comm.make(cins, couts, sems):
                cp.start()

        body(*ins, *outs, *scratch)

        @pl.when(last)
        def _():
            for cp in comm.make(cins, couts, sems):
                cp.wait()

    return fused


def _comm_args(comm):
    if comm is None:
        return [], [], [], [], []
    return (list(comm.ins), [ANY] * len(comm.ins), [ANY] * len(comm.out_shapes), list(comm.out_shapes),
            list(comm.sems))


HEADS_PER_BLOCK = 4


def _head_block_diag():
    i = np.arange(HEADS_PER_BLOCK * HEAD_DIM)
    return jnp.asarray((i[:, None] // HEAD_DIM == i[None, :] // HEAD_DIM).astype(np.float32), BF16)


def _head_select():
    i = np.arange(2 * D_BRANCH)
    j = np.arange(LANE)
    return jnp.asarray((i[:, None] // HEAD_DIM == j[None, :]).astype(np.float32), BF16)


def _dil_bias(t):
    nb = DIL_BACK // t + 1
    qi = np.arange(t)[:, None]
    ki = np.arange(t)[None, :]
    tiles = []
    for r in range(nb):
        d = r * t + qi - ki
        mult = np.zeros((t, t), np.int64)
        for window, dil in DIL_PATTERNS:
            mult += ((d >= 0) & (d <= window) & (d % dil == 0)).astype(np.int64)
        b = np.where(mult > 0, np.log2(np.maximum(mult, 1)), NEG).astype(np.float32)
        tiles.append(b.T)
    return jnp.asarray(np.stack(tiles))


def _rope_tables(positions):
    inv_freq = ROPE_THETA ** (-jnp.arange(ROPE_HALF, dtype=F32) / ROPE_HALF)
    ang = positions.astype(F32)[:, None] * inv_freq
    cos, sin = jnp.cos(ang), jnp.sin(ang)
    s = positions.shape[0]
    rest = HEAD_DIM - 2 * ROPE_HALF
    one, zero, zero8 = jnp.ones((s, rest), F32), jnp.zeros((s, rest), F32), jnp.zeros((s, ROPE_HALF), F32)
    c = jnp.concatenate([cos, cos, one], axis=1)
    s1 = jnp.concatenate([zero8, sin, zero], axis=1)
    s2 = jnp.concatenate([-sin, zero8, zero], axis=1)
    return tuple(jnp.tile(t, (1, 2)) for t in (c, s1, s2))


def _rope_fwd(x, c, s1, s2):
    return x * c + pltpu.roll(x, ROPE_HALF, 1) * s1 + pltpu.roll(x, LANE - ROPE_HALF, 1) * s2


def _rope_bwd(dy, c, s1, s2):
    return dy * c + pltpu.roll(dy * s1, LANE - ROPE_HALF, 1) + pltpu.roll(dy * s2, ROPE_HALF, 1)


def _log_sigmoid(x):
    return jnp.minimum(x, 0.0) - jnp.log(1.0 + jnp.exp(-jnp.abs(x)))


def _inproj(l, h, norm_g, w_all, b_f, qkg, bsum, rope, tm):
    s = h.shape[0]
    rc, rs1, rs2 = rope

    def body(h_ref, g_ref, w_ref, bf_ref, qkg_ref, bsum_ref, rc_ref, rs1_ref, rs2_ref,
             u_ref, z_ref, qa_ref, ka_ref, va_ref, qb_ref, kb_ref, vb_ref, cs_ref, ct_ref, vat_ref, vbt_ref, carry):
        @pl.when(pl.program_id(0) == 0)
        def _():
            carry[...] = jnp.zeros_like(carry)

        hh = h_ref[...]
        r = lax.rsqrt(jnp.mean(hh * hh, axis=-1, keepdims=True) + EPS)
        u = (hh * r * g_ref[...]).astype(BF16)
        u_ref[...] = u
        for k in range(N_ALL // LANE // 3):
            cols = slice(3 * LANE * k, 3 * LANE * (k + 1))
            z_ref[:, cols] = _dot(u, w_ref[:, cols])
        bs = bsum_ref[...]

        def head_norm(x, row):
            ms = _head_sums(x * x, bs) * (1.0 / HEAD_DIM)
            return x * lax.rsqrt(ms + EPS) * qkg_ref[row:row + 1, :]

        def seg(k):
            return z_ref[:, D_BRANCH * k:D_BRANCH * (k + 1)]

        qa_ref[...] = (head_norm(seg(0), 0) * (Q_SCALE * LOG2E)).astype(BF16)
        ka_ref[...] = head_norm(seg(1), 1).astype(BF16)
        va_ref[...] = seg(2).astype(BF16)
        vat_ref[...] = seg(2).T.astype(BF16)
        qn = head_norm(seg(4), 2) * (Q_SCALE * LOG2E)
        kn = head_norm(seg(5), 3)
        c, s1, s2 = rc_ref[...], rs1_ref[...], rs2_ref[...]
        for k in range(D_BRANCH // LANE):
            cols = slice(LANE * k, LANE * (k + 1))
            qb_ref[:, cols] = _rope_fwd(qn[:, cols], c, s1, s2).astype(BF16)
            kb_ref[:, cols] = _rope_fwd(kn[:, cols], c, s1, s2).astype(BF16)
        vb_ref[...] = seg(6).astype(BF16)
        vbt_ref[...] = seg(6).T.astype(BF16)
        logf = _log_sigmoid(z_ref[:, N_MAIN:N_ALL] + bf_ref[...])
        csum = _split3_dot(_tri(tm, False), logf) + carry[0:1, :]
        carry[...] = jnp.broadcast_to(csum[tm - 1:tm, :], carry.shape)
        csum = csum * LOG2E
        ct = csum.T
        for p in range(N_PAIRS):
            cs_ref[:, LANE * p:LANE * (p + 1)] = csum if p == 0 else pltpu.roll(csum, LANE - 2 * p, 1)
            ct_ref[p, :, :] = ct[2 * p:2 * p + 2, :]

    row = lambda w: pl.BlockSpec((tm, w), lambda i: (i, 0))
    colt = pl.BlockSpec((D_BRANCH, tm), lambda i: (0, i))
    bf = lambda: jax.ShapeDtypeStruct((s, D_BRANCH), BF16)
    bft = lambda: jax.ShapeDtypeStruct((D_BRANCH, s), BF16)
    return pl.pallas_call(
        body, name=f"inproj_l{l}", grid=(s // tm,),
        in_specs=[row(D_MODEL), _full((1, D_MODEL)), _full((D_MODEL, N_ALL)), _full((1, LANE)),
                  _full((8, D_BRANCH)), _full((HEADS_PER_BLOCK * HEAD_DIM,) * 2), row(LANE), row(LANE), row(LANE)],
        out_specs=[row(D_MODEL), row(N_ALL)] + [row(D_BRANCH)] * 6
        + [row(N_PAIRS * LANE), pl.BlockSpec((N_PAIRS, 2, tm), lambda i: (0, 0, i)), colt, colt],
        out_shape=[jax.ShapeDtypeStruct((s, D_MODEL), BF16), jax.ShapeDtypeStruct((s, N_ALL), F32),
                   bf(), bf(), bf(), bf(), bf(), bf(), jax.ShapeDtypeStruct((s, N_PAIRS * LANE), F32),
                   jax.ShapeDtypeStruct((N_PAIRS, 2, s), F32), bft(), bft()],
        scratch_shapes=[pltpu.VMEM((8, LANE), F32)],
        compiler_params=_params(1),
    )(h, norm_g, w_all, b_f, qkg, bsum, rc, rs1, rs2)


def _tri(t, upper):
    a = lax.broadcasted_iota(jnp.int32, (t, t), 0)
    b = lax.broadcasted_iota(jnp.int32, (t, t), 1)
    return jnp.where((b >= a) if upper else (b <= a), 1.0, 0.0).astype(BF16)


def _attn_fwd(name, fox, q, k, vt, extra, t, comm=None):
    s = q.shape[0]
    nq = s // t
    nb = DIL_BACK // t + 1

    def body(*refs):
        if fox:
            q_ref, k_ref, vt_ref, ccol_ref, crow_ref, o_ref, lse_ref, m_scr, l_scr, acc_scr = refs
        else:
            q_ref, k_ref, vt_ref, bias_ref, o_ref, lse_ref, m_scr, l_scr, acc_scr = refs
        i = pl.program_id(1)
        lane = lax.broadcasted_iota(jnp.int32, (t, LANE), 1)
        first = lane < HEAD_DIM
        qq = q_ref[...]
        zero = jnp.zeros_like(qq)
        qh = (jnp.where(first, qq, zero), jnp.where(first, zero, qq))
        m_scr[...] = jnp.full(m_scr.shape, M_INIT, F32)
        l_scr[...] = jnp.zeros_like(l_scr)
        acc_scr[...] = jnp.zeros_like(acc_scr)
        ones = jnp.ones((16, t), BF16)

        half = t // 2
        whole, lo, hi = slice(0, t), slice(0, half), slice(half, t)

        def block(j, ksl, qsl, causal):
            nk_, nq_ = ksl.stop - ksl.start, qsl.stop - qsl.start
            rows = pl.ds(pl.multiple_of(j * t + ksl.start, LANE), nk_)
            ks = k_ref[rows, :]
            vts = jnp.concatenate([vt_ref[:, rows], ones[:, :nk_]], axis=0)
            if fox:
                ccol = ccol_ref[rows, :]
            for h in range(2):
                st = _dot_nt(ks, qh[h][qsl, :])
                if fox:
                    st = st + (crow_ref[h:h + 1, qsl] - ccol[:, h:h + 1])
                    if causal:
                        ki = lax.broadcasted_iota(jnp.int32, (nk_, nq_), 0) + ksl.start
                        qi = lax.broadcasted_iota(jnp.int32, (nk_, nq_), 1) + qsl.start
                        st = jnp.where(ki <= qi, st, NEG)
                else:
                    st = st + bias_ref[i - j, ksl, qsl]
                m_old = m_scr[h, :, qsl]
                m_new = jnp.maximum(m_old, jnp.max(st, axis=0, keepdims=True))
                alpha = jnp.exp2(m_old - m_new)
                pb = jnp.exp2(st - m_new).astype(BF16)
                pv = _dot(vts, pb)
                l_scr[h, :, qsl] = alpha * l_scr[h, :, qsl] + pv[LANE:LANE + 1, :]
                acc_scr[h, :, qsl] = alpha * acc_scr[h, :, qsl] + pv[:LANE, :]
                m_scr[h, :, qsl] = m_new

        def full(j, c):
            block(j, whole, whole, False)
            return c

        if fox:
            lax.fori_loop(0, i, full, 0)
        else:
            @pl.when(i >= nb - 1)
            def _():
                block(i - (nb - 1), lo, lo, False)
                block(i - (nb - 1), hi, whole, False)

            lax.fori_loop(jnp.maximum(i - (nb - 2), 0), i, full, 0)
        if fox:
            block(i, whole, whole, True)
        else:
            block(i, lo, lo, False)
            block(i, whole, hi, False)

        sub = lax.broadcasted_iota(jnp.int32, (LANE, t), 0)
        ot = jnp.where(sub < HEAD_DIM, acc_scr[0] / l_scr[0], acc_scr[1] / l_scr[1])
        o_ref[...] = ot.T
        for h in range(2):
            lse_ref[h:h + 1, :] = m_scr[h] + jnp.log2(l_scr[h])

    qspec = pl.BlockSpec((t, LANE), lambda hp, i: (i, hp))
    kspec = pl.BlockSpec((s, LANE), lambda hp, i: (0, hp))
    vtspec = pl.BlockSpec((LANE, s), lambda hp, i: (hp, 0))
    in_specs = [qspec, kspec, vtspec]
    if fox:
        in_specs += [kspec, pl.BlockSpec((None, 2, t), lambda hp, i: (hp, 0, i))]
    else:
        in_specs += [_full((nb, t, t))]
    grid = (N_PAIRS, nq)
    c_in, c_ispec, c_ospec, c_oshape, c_scr = _comm_args(comm)
    return pl.pallas_call(
        _fuse_comm(body, len(in_specs), 2, comm, grid), name=name, grid=grid,
        in_specs=in_specs + c_ispec,
        out_specs=[qspec, pl.BlockSpec((None, 2, t), lambda hp, i: (hp, 0, i))] + c_ospec,
        out_shape=[jax.ShapeDtypeStruct((s, D_BRANCH), F32), jax.ShapeDtypeStruct((N_PAIRS, 2, s), F32)] + c_oshape,
        scratch_shapes=[pltpu.VMEM((2, 1, t), F32), pltpu.VMEM((2, 1, t), F32), pltpu.VMEM((2, LANE, t), F32)]
        + c_scr,
        compiler_params=_params(2),
    )(q, k, vt, *extra, *c_in)


def _attn_bwd(name, fox, q, k, v, do, lse_t, delta_t, pair_offset, extra, t, comm=None):
    s = q.shape[0]
    nk = s // t
    nb = DIL_BACK // t + 1

    def body(*refs):
        if fox:
            (q_ref, k_ref, v_ref, do_ref, lse_ref, delta_ref, ccol_ref, crow_ref,
             dqt_ref, dk_ref, dv_ref, dc_ref, drow_ref) = refs
        else:
            q_ref, k_ref, v_ref, do_ref, lse_ref, delta_ref, bias_ref, dqt_ref, dk_ref, dv_ref = refs
        j = pl.program_id(1)

        @pl.when(j == 0)
        def _():
            dqt_ref[...] = jnp.zeros_like(dqt_ref)
            if fox:
                drow_ref[...] = jnp.zeros_like(drow_ref)

        lane = lax.broadcasted_iota(jnp.int32, (t, LANE), 1)
        first = lane < HEAD_DIM
        ks = k_ref[...]
        vs = v_ref[...]
        kt = ks.astype(F32).T
        sub = lax.broadcasted_iota(jnp.int32, (LANE, t), 0)
        kth = (jnp.where(sub < HEAD_DIM, kt, 0.0).astype(BF16), jnp.where(sub < HEAD_DIM, 0.0, kt).astype(BF16))
        dk_ref[...] = jnp.zeros_like(dk_ref)
        dv_ref[...] = jnp.zeros_like(dv_ref)
        if fox:
            dc_ref[...] = jnp.zeros_like(dc_ref)
            ccol = ccol_ref[...]

        half = t // 2
        whole, lo, hi = slice(0, t), slice(0, half), slice(half, t)

        def block(i, ksl, qsl, causal):
            nk_, nq_ = ksl.stop - ksl.start, qsl.stop - qsl.start
            rows = pl.ds(pl.multiple_of(i * t + qsl.start, LANE), nq_)
            qq = q_ref[rows, :]
            dd = do_ref[rows, :]
            zero = jnp.zeros_like(qq)
            qh = (jnp.where(first[:nq_], qq, zero), jnp.where(first[:nq_], zero, qq))
            dh = (jnp.where(first[:nq_], dd, zero), jnp.where(first[:nq_], zero, dd))
            for h in range(2):
                st = _dot_nt(ks[ksl, :], qh[h])
                if fox:
                    st = st + (crow_ref[h:h + 1, rows] - ccol[ksl, h:h + 1])
                    if causal:
                        ki = lax.broadcasted_iota(jnp.int32, (nk_, nq_), 0) + ksl.start
                        qi = lax.broadcasted_iota(jnp.int32, (nk_, nq_), 1) + qsl.start
                        st = jnp.where(ki <= qi, st, NEG)
                else:
                    st = st + bias_ref[i - j, ksl, qsl]
                pt = jnp.exp2(st - lse_ref[h:h + 1, rows])
                dpt = _dot_nt(vs[ksl, :], dh[h])
                dst = pt * (dpt - delta_ref[h:h + 1, rows])
                dv_ref[ksl, :] += _dot(pt.astype(BF16), dh[h])
                dsb = dst.astype(BF16)
                dk_ref[ksl, :] += _dot(dsb, qh[h])
                dqt_ref[:, rows] += _dot(kth[h][:, ksl], dsb)
                if fox:
                    dc_ref[ksl, :] -= jnp.where(lane[:nk_] == h, jnp.sum(dst, axis=1, keepdims=True), 0.0)
                    drow_ref[h:h + 1, rows] += jnp.sum(dst, axis=0, keepdims=True)

        def full(i, c):
            block(i, whole, whole, False)
            return c

        block(j, lo, whole, True)
        block(j, hi, hi, True)
        if fox:
            lax.fori_loop(j + 1, nk, full, 0)
        else:
            lax.fori_loop(j + 1, jnp.minimum(j + nb - 1, nk), full, 0)

            @pl.when(j + nb - 1 < nk)
            def _():
                block(j + nb - 1, lo, lo, False)
                block(j + nb - 1, hi, whole, False)

    kspec = pl.BlockSpec((t, LANE), lambda hp, j: (j, hp))
    qspec = pl.BlockSpec((s, LANE), lambda hp, j: (0, hp))
    rowspec = pl.BlockSpec((None, 2, s), lambda hp, j: (hp, 0, 0))
    drowspec = pl.BlockSpec((None, 2, s), lambda hp, j: (hp + pair_offset, 0, 0))
    in_specs = [qspec, kspec, kspec, qspec, rowspec, drowspec]
    out_specs = [pl.BlockSpec((LANE, s), lambda hp, j: (hp, 0)), kspec, kspec]
    out_shape = [jax.ShapeDtypeStruct((D_BRANCH, s), F32), jax.ShapeDtypeStruct((s, D_BRANCH), F32),
                 jax.ShapeDtypeStruct((s, D_BRANCH), F32)]
    if fox:
        in_specs += [kspec, rowspec]
        out_specs += [kspec, rowspec]
        out_shape += [jax.ShapeDtypeStruct((s, N_PAIRS * LANE), F32), jax.ShapeDtypeStruct((N_PAIRS, 2, s), F32)]
    else:
        in_specs += [_full((nb, t, t))]
    grid = (N_PAIRS, nk)
    c_in, c_ispec, c_ospec, c_oshape, c_scr = _comm_args(comm)
    return pl.pallas_call(
        _fuse_comm(body, len(in_specs), len(out_specs), comm, grid), name=name, grid=grid,
        in_specs=in_specs + c_ispec, out_specs=out_specs + c_ospec, out_shape=out_shape + c_oshape,
        scratch_shapes=c_scr, compiler_params=_params(2),
    )(q, k, v, do, lse_t, delta_t, *extra, *c_in)


def _silu(x):
    return x * _sigmoid(x)


def _outproj(l, h, oa, ob, z, p_i, w_out, ple_g, w_gate, w_ple, tm, target=None):
    s = h.shape[0]
    last = target is not None

    def body(h_ref, oa_ref, ob_ref, ga_ref, gb_ref, p_ref, wo_ref, pg_ref, wg_ref, wp_ref, *rest):
        if last:
            t_ref, h2_ref, e_ref, gate_ref, out_ref, acc_ref = rest
        else:
            h2_ref, e_ref, gate_ref, out_ref = rest
        a = jnp.concatenate([oa_ref[...] * _silu(ga_ref[...]), ob_ref[...] * _silu(gb_ref[...])], axis=1)
        h2 = h_ref[...] + _dot(a.astype(BF16), wo_ref[...])
        h2_ref[...] = h2
        r = lax.rsqrt(jnp.mean(h2 * h2, axis=-1, keepdims=True) + EPS)
        n2 = (h2 * r * pg_ref[...]).astype(BF16)
        gate = _sigmoid(_dot(n2, wg_ref[...]))
        e = _dot(p_ref[...].astype(BF16), wp_ref[...])
        e_ref[...] = e
        gate_ref[...] = gate
        out = h2 + e * gate
        if not last:
            out_ref[...] = out
            return

        @pl.when(pl.program_id(0) == 0)
        def _():
            acc_ref[...] = jnp.zeros_like(acc_ref)

        err = out - t_ref[...]
        out_ref[...] = err * (1.0 / D_MODEL)
        e2 = err * err
        rows = e2[0:8, :]
        for k in range(1, tm // 8):
            rows = rows + e2[8 * k:8 * (k + 1), :]
        part = rows[:, 0:LANE]
        for k in range(1, D_MODEL // LANE):
            part = part + rows[:, LANE * k:LANE * (k + 1)]
        acc_ref[...] += part

    row = lambda w: pl.BlockSpec((tm, w), lambda i: (i, 0))
    zcol = lambda k: pl.BlockSpec((tm, D_BRANCH), lambda i: (i, k))
    f = lambda: jax.ShapeDtypeStruct((s, D_MODEL), F32)
    return pl.pallas_call(
        body, name=f"outproj_l{l}", grid=(s // tm,),
        in_specs=[row(D_MODEL), row(D_BRANCH), row(D_BRANCH), zcol(3), zcol(7), row(PLE_DIM),
                  _full((D_MODEL, D_MODEL)), _full((1, D_MODEL)), _full((D_MODEL, D_MODEL)),
                  _full((PLE_DIM, D_MODEL))] + ([row(D_MODEL)] if last else []),
        out_specs=[row(D_MODEL)] * 4 + ([_full((8, LANE))] if last else []),
        out_shape=[f(), f(), f(), f()] + ([jax.ShapeDtypeStruct((8, LANE), F32)] if last else []),
        compiler_params=_params(1),
    )(h, oa, ob, z, z, p_i, w_out, ple_g, w_gate, w_ple, *([target] if last else []))


def _outproj_bwd(l, dout, h2, e, gate, p_i, oa, ob, z, w_out_t, ple_g, w_gate_t, hsel, tm, comm=None):
    s = dout.shape[0]

    def body(do_ref, h2_ref, e_ref, gate_ref, p_ref, oa_ref, ob_ref, ga_ref, gb_ref, wot_ref, pg_ref,
             wgt_ref, hsel_ref,
             dh2_ref, doa_ref, dob_ref, dga_ref, dgb_ref, delta_ref, dwo_ref, dwg_ref, dwp_ref, dpg_ref):
        @pl.when(pl.program_id(0) == 0)
        def _():
            dwo_ref[...] = jnp.zeros_like(dwo_ref)
            dwg_ref[...] = jnp.zeros_like(dwg_ref)
            dwp_ref[...] = jnp.zeros_like(dwp_ref)
            dpg_ref[...] = jnp.zeros_like(dpg_ref)

        dho = do_ref[...]
        g = gate_ref[...]
        de = (dho * g).astype(BF16)
        dwp_ref[...] += _dot_tn(p_ref[...].astype(BF16), de)
        dpre = (dho * e_ref[...] * g * (1.0 - g)).astype(BF16)
        h2 = h2_ref[...]
        pg = pg_ref[...]
        r = lax.rsqrt(jnp.mean(h2 * h2, axis=-1, keepdims=True) + EPS)
        n2 = (h2 * r * pg).astype(BF16)
        dwg_ref[...] += _dot_tn(n2, dpre)
        dn2 = _dot(dpre, wgt_ref[...])
        dpg_ref[0:1, :] += jnp.sum(dn2 * h2 * r, axis=0, keepdims=True)
        wv = dn2 * pg
        dh2 = dho + r * wv - h2 * (r * r * r) * jnp.mean(wv * h2, axis=-1, keepdims=True)
        dh2_ref[...] = dh2
        dh2b = dh2.astype(BF16)
        ga, gb, oa, ob = ga_ref[...], gb_ref[...], oa_ref[...], ob_ref[...]
        sga, sgb = _sigmoid(ga), _sigmoid(gb)
        a = jnp.concatenate([oa * ga * sga, ob * gb * sgb], axis=1).astype(BF16)
        dwo_ref[...] += _dot_tn(a, dh2b)
        da = _dot(dh2b, wot_ref[...])
        da_a, da_b = da[:, :D_BRANCH], da[:, D_BRANCH:]
        doa = da_a * ga * sga
        dob = da_b * gb * sgb
        doa_ref[...] = doa.astype(BF16)
        dob_ref[...] = dob.astype(BF16)
        dga_ref[...] = (da_a * oa * sga * (1.0 + ga * (1.0 - sga))).astype(BF16)
        dgb_ref[...] = (da_b * ob * sgb * (1.0 + gb * (1.0 - sgb))).astype(BF16)
        prod = jnp.concatenate([doa * oa, dob * ob], axis=1)
        dt = _split_dot(prod, hsel_ref[...]).T
        for pp in range(2 * N_PAIRS):
            delta_ref[pp, :, :] = dt[2 * pp:2 * pp + 2, :]

    row = lambda w: pl.BlockSpec((tm, w), lambda i: (i, 0))
    zcol = lambda k: pl.BlockSpec((tm, D_BRANCH), lambda i: (i, k))
    grid = (s // tm,)
    c_in, c_ispec, c_ospec, c_oshape, c_scr = _comm_args(comm)
    return pl.pallas_call(
        _fuse_comm(body, 13, 10, comm, grid), name=f"outproj_bwd_l{l}", grid=grid,
        in_specs=[row(D_MODEL)] * 4 + [row(PLE_DIM), row(D_BRANCH), row(D_BRANCH), zcol(3), zcol(7),
                                        _full((D_MODEL, D_MODEL)), _full((1, D_MODEL)), _full((D_MODEL, D_MODEL)),
                                        _full((2 * D_BRANCH, LANE))] + c_ispec,
        out_specs=[row(D_MODEL)] + [row(D_BRANCH)] * 4
        + [pl.BlockSpec((2 * N_PAIRS, 2, tm), lambda i: (0, 0, i)), _full((D_MODEL, D_MODEL)),
           _full((D_MODEL, D_MODEL)), _full((PLE_DIM, D_MODEL)), _full((8, D_MODEL))] + c_ospec,
        out_shape=[jax.ShapeDtypeStruct((s, D_MODEL), F32)] + [jax.ShapeDtypeStruct((s, D_BRANCH), BF16)] * 4
        + [jax.ShapeDtypeStruct((2 * N_PAIRS, 2, s), F32), jax.ShapeDtypeStruct((D_MODEL, D_MODEL), F32),
           jax.ShapeDtypeStruct((D_MODEL, D_MODEL), F32), jax.ShapeDtypeStruct((PLE_DIM, D_MODEL), F32),
           jax.ShapeDtypeStruct((8, D_MODEL), F32)] + c_oshape,
        scratch_shapes=c_scr, compiler_params=_params(1),
    )(dout, h2, e, gate, p_i, oa, ob, z, z, w_out_t, ple_g, w_gate_t, hsel, *c_in)


def _inproj_bwd_prep(l, z, dqt_a, dk_a, dv_a, dqt_b, dk_b, dv_b, dga, dgb, dc_spread, dc_rows, b_f, qkg, bsum,
                     rope, tm):
    s = z.shape[0]
    n = s // tm
    rc, rs1, rs2 = rope

    def body(z_ref, dqta_ref, dka_ref, dva_ref, dqtb_ref, dkb_ref, dvb_ref, dga_ref, dgb_ref, dc_ref, drow_ref,
             bf_ref, qkg_ref, bsum_ref, rc_ref, rs1_ref, rs2_ref, dz_ref, dqkg_ref, dbf_ref, carry):
        @pl.when(pl.program_id(0) == 0)
        def _():
            dqkg_ref[...] = jnp.zeros_like(dqkg_ref)
            dbf_ref[...] = jnp.zeros_like(dbf_ref)
            carry[...] = jnp.zeros_like(carry)

        lane = lax.broadcasted_iota(jnp.int32, (tm, LANE), 1)
        dc = jnp.concatenate([drow_ref[p] for p in range(N_PAIRS)] + [jnp.zeros((LANE - N_HEADS, tm), F32)], axis=0).T
        for p in range(N_PAIRS):
            part = jnp.where(lane < 2, dc_ref[:, LANE * p:LANE * (p + 1)], 0.0)
            dc = dc + (part if p == 0 else pltpu.roll(part, 2 * p, 1))
        dlogf = _split3_dot(_tri(tm, True), dc) + carry[0:1, :]
        carry[...] = jnp.broadcast_to(dlogf[0:1, :], carry.shape)

        bs = bsum_ref[...]
        c, s1, s2 = rc_ref[...], rs1_ref[...], rs2_ref[...]

        def unrope(dy):
            return jnp.concatenate([_rope_bwd(dy[:, LANE * k:LANE * (k + 1)], c, s1, s2)
                                    for k in range(D_BRANCH // LANE)], axis=1)

        def norm_bwd(k, row, dy):
            x = z_ref[:, D_BRANCH * k:D_BRANCH * (k + 1)]
            r = lax.rsqrt(_head_sums(x * x, bs) * (1.0 / HEAD_DIM) + EPS)
            dqkg_ref[row:row + 1, :] += jnp.sum(dy * x * r, axis=0, keepdims=True)
            w = dy * qkg_ref[row:row + 1, :]
            dx = r * w - x * (r * r * r) * (_head_sums(w * x, bs) * (1.0 / HEAD_DIM))
            dz_ref[:, D_BRANCH * k:D_BRANCH * (k + 1)] = dx.astype(BF16)

        norm_bwd(0, 0, dqta_ref[...].T * Q_SCALE)
        norm_bwd(1, 1, dka_ref[...] * LN2)
        dz_ref[:, 2 * D_BRANCH:3 * D_BRANCH] = dva_ref[...].astype(BF16)
        dz_ref[:, 3 * D_BRANCH:4 * D_BRANCH] = dga_ref[...]
        norm_bwd(4, 2, unrope(dqtb_ref[...].T * Q_SCALE))
        norm_bwd(5, 3, unrope(dkb_ref[...] * LN2))
        dz_ref[:, 6 * D_BRANCH:7 * D_BRANCH] = dvb_ref[...].astype(BF16)
        dz_ref[:, 7 * D_BRANCH:8 * D_BRANCH] = dgb_ref[...]
        dfa = dlogf * _sigmoid(-(z_ref[:, N_MAIN:N_ALL] + bf_ref[...]))
        dz_ref[:, N_MAIN:N_ALL] = dfa.astype(BF16)
        dbf_ref[0:1, :] += jnp.sum(dfa, axis=0, keepdims=True)

    row = lambda w: pl.BlockSpec((tm, w), lambda i: (n - 1 - i, 0))
    colt = pl.BlockSpec((D_BRANCH, tm), lambda i: (0, n - 1 - i))
    return pl.pallas_call(
        body, name=f"inproj_bwd_prep_l{l}", grid=(n,),
        in_specs=[row(N_ALL), colt, row(D_BRANCH), row(D_BRANCH), colt, row(D_BRANCH), row(D_BRANCH),
                  row(D_BRANCH), row(D_BRANCH), row(N_PAIRS * LANE),
                  pl.BlockSpec((N_PAIRS, 2, tm), lambda i: (0, 0, n - 1 - i)), _full((1, LANE)),
                  _full((8, D_BRANCH)), _full((HEADS_PER_BLOCK * HEAD_DIM,) * 2), row(LANE), row(LANE), row(LANE)],
        out_specs=[row(N_ALL), _full((8, D_BRANCH)), _full((8, LANE))],
        out_shape=[jax.ShapeDtypeStruct((s, N_ALL), BF16), jax.ShapeDtypeStruct((8, D_BRANCH), F32),
                   jax.ShapeDtypeStruct((8, LANE), F32)],
        scratch_shapes=[pltpu.VMEM((8, LANE), F32)],
        compiler_params=_params(1),
    )(z, dqt_a, dk_a, dv_a, dqt_b, dk_b, dv_b, dga, dgb, dc_spread, dc_rows, b_f, qkg, bsum, rc, rs1, rs2)


def _inproj_bwd_dx(l, dz, w_all_t, h, norm_g, dh2, tm):
    s = dz.shape[0]

    def body(dz_ref, wt_ref, h_ref, g_ref, dh2_ref, dh_ref, dg_ref):
        @pl.when(pl.program_id(0) == 0)
        def _():
            dg_ref[...] = jnp.zeros_like(dg_ref)

        du = _dot(dz_ref[...], wt_ref[...])
        hh = h_ref[...]
        g = g_ref[...]
        r = lax.rsqrt(jnp.mean(hh * hh, axis=-1, keepdims=True) + EPS)
        dg_ref[0:1, :] += jnp.sum(du * hh * r, axis=0, keepdims=True)
        wv = du * g
        dh_ref[...] = dh2_ref[...] + r * wv - hh * (r * r * r) * jnp.mean(wv * hh, axis=-1, keepdims=True)

    row = lambda w: pl.BlockSpec((tm, w), lambda i: (i, 0))
    return pl.pallas_call(
        body, name=f"inproj_bwd_dx_l{l}", grid=(s // tm,),
        in_specs=[row(N_ALL), _full((N_ALL, D_MODEL)), row(D_MODEL), _full((1, D_MODEL)), row(D_MODEL)],
        out_specs=[row(D_MODEL), _full((8, D_MODEL))],
        out_shape=[jax.ShapeDtypeStruct((s, D_MODEL), F32), jax.ShapeDtypeStruct((8, D_MODEL), F32)],
        compiler_params=_params(1),
    )(dz, w_all_t, h, norm_g, dh2)


def _inproj_bwd_dw(l, u, dz, tm, tn):
    s = u.shape[0]

    def body(u_ref, dz_ref, dw_ref):
        @pl.when(pl.program_id(1) == 0)
        def _():
            dw_ref[...] = jnp.zeros_like(dw_ref)

        dw_ref[...] += _dot_tn(u_ref[...], dz_ref[...])

    return pl.pallas_call(
        body, name=f"inproj_bwd_dw_l{l}", grid=(N_ALL // tn, s // tm),
        in_specs=[pl.BlockSpec((tm, D_MODEL), lambda n, i: (i, 0)), pl.BlockSpec((tm, tn), lambda n, i: (i, n))],
        out_specs=pl.BlockSpec((D_MODEL, tn), lambda n, i: (0, n)),
        out_shape=jax.ShapeDtypeStruct((D_MODEL, N_ALL), F32),
        compiler_params=_params(2),
    )(u, dz)


def _adamw_math(w, g, m, v):
    m = ADAM_B1 * m + (1.0 - ADAM_B1) * g
    v = ADAM_B2 * v + (1.0 - ADAM_B2) * (g * g)
    m_hat = m / (1.0 - ADAM_B1 ** ADAM_STEP)
    v_hat = v / (1.0 - ADAM_B2 ** ADAM_STEP)
    delta = -ADAM_LR * (m_hat / (jnp.sqrt(v_hat) + ADAM_EPS) + ADAM_WD * w)
    return delta, m, v


def _adamw(name, w, halves, m, v, core):
    nl, r, c = w.shape
    hr = r // 2
    tr = 128 if hr % 128 == 0 else hr
    nb = hr // tr

    def body(core_ref, w_ref, own0_ref, oth0_ref, own1_ref, oth1_ref, m_ref, v_ref, g_ref, d_ref, nm_ref, nv_ref):
        first = pl.program_id(0) == 0
        own = jnp.where(first, own0_ref[...], own1_ref[...])
        oth = jnp.where(first, oth0_ref[...], oth1_ref[...])
        g = jnp.where(pl.program_id(1) // nb == core_ref[0], own, oth)
        d, nm, nv = _adamw_math(w_ref[...], g, m_ref[...], v_ref[...])
        g_ref[...] = g
        d_ref[...] = d
        nm_ref[...] = nm
        nv_ref[...] = nv

    spec = pl.BlockSpec((None, tr, c), lambda a, b, core_ref: (a, b, 0))
    gspec = pl.BlockSpec((tr, c), lambda a, b, core_ref: (b % nb, 0))
    shp = jax.ShapeDtypeStruct(w.shape, F32)
    return pl.pallas_call(
        body, name=name,
        grid_spec=pltpu.PrefetchScalarGridSpec(
            num_scalar_prefetch=1, grid=(nl, r // tr), in_specs=[spec] + [gspec] * 4 + [spec, spec],
            out_specs=[spec] * 4),
        out_shape=[shp, shp, shp, shp], compiler_params=_params(2),
    )(core, w, halves[0][0], halves[0][1], halves[1][0], halves[1][1], m, v)


def _pair_sum(name, g, x, c, narrow=False):
    n, r, cc = g.shape
    hr = r // 2
    tr = 128 if hr % 128 == 0 else hr
    nb = hr // tr

    def body(c_ref, g_ref, x_ref, o_ref, *narrow_ref):
        total = g_ref[...] + x_ref[...]
        o_ref[...] = total
        if narrow:
            narrow_ref[0][...] = total.astype(BF16)

    spec = pl.BlockSpec((None, tr, cc), lambda i, j, c_ref: (i, j, 0))
    shapes = [jax.ShapeDtypeStruct((n, hr, cc), F32)] + ([jax.ShapeDtypeStruct((n, hr, cc), BF16)] if narrow else [])
    return pl.pallas_call(
        body, name=name,
        grid_spec=pltpu.PrefetchScalarGridSpec(
            num_scalar_prefetch=1, grid=(n, nb),
            in_specs=[pl.BlockSpec((None, tr, cc), lambda i, j, c_ref: (i, c_ref[0] * nb + j, 0)), spec],
            out_specs=[spec] * len(shapes)),
        out_shape=shapes, compiler_params=_params(2),
    )(c, g, x)


def _sum_slots(name, own, landed, chip):
    n, r, c = own.shape
    tr = 128 if r % 128 == 0 else r

    def body(chip_ref, a_ref, b_ref, c_ref, d_ref, o_ref):
        o_ref[...] = ((a_ref[...] + b_ref[...].astype(F32)) + c_ref[...].astype(F32)) + d_ref[...].astype(F32)

    slot = lambda d: pl.BlockSpec((None, tr, c), lambda j, chip_ref: ((chip_ref[0] + d) % n, j, 0))
    return pl.pallas_call(
        body, name=name,
        grid_spec=pltpu.PrefetchScalarGridSpec(
            num_scalar_prefetch=1, grid=(r // tr,), in_specs=[slot(0), slot(1), slot(2), slot(3)],
            out_specs=pl.BlockSpec((tr, c), lambda j, chip_ref: (j, 0))),
        out_shape=jax.ShapeDtypeStruct((r, c), F32), compiler_params=_params(1),
    )(chip, own, landed, landed, landed)


def _me():
    return lax.axis_index("x"), lax.axis_index("y"), lax.axis_index("c")


def _other_chips(x, y):
    return [(1 - x, y), (x, 1 - y), (1 - x, 1 - y)]


def _dma_sems(*counts):
    return [pltpu.SemaphoreType.DMA((n,)) for n in counts]


def _half_rows(rows, which, align):
    return pl.ds(pl.multiple_of(which * (rows // 2), align), rows // 2)


def _gather_first_layer(shards):
    n = len(shards)

    def body(*refs):
        ins, outs, keep, stage = refs[:n], refs[n:2 * n], refs[2 * n:3 * n], refs[3 * n:4 * n]
        ici_send, ici_recv, d2d_send, d2d_recv, local_sems = refs[4 * n:]
        x, y, c = _me()
        k = 2 * x + y
        chips = _other_chips(x, y)
        local, first, passed = [], [], []
        for t in range(n):
            stage[t][...] = ins[t][0].astype(BF16)
            keep[t][...] = ins[t][1].astype(BF16)
            cp = pltpu.make_async_copy(stage[t], outs[t].at[k], local_sems.at[t])
            cp.start()
            local.append(cp)
        for t in range(n):
            mine = _half_rows(shards[t].shape[1], c, 16)
            for j, (px, py) in enumerate(chips):
                cp = pltpu.make_async_remote_copy(
                    src_ref=stage[t].at[mine], dst_ref=outs[t].at[k, mine], send_sem=ici_send.at[3 * t + j],
                    recv_sem=ici_recv.at[3 * t + j], device_id=(px, py, c), device_id_type=MESH)
                cp.start()
                first.append(cp)
        for t in range(n):
            mine = _half_rows(shards[t].shape[1], c, 16)
            for j, (px, py) in enumerate(chips):
                landed = outs[t].at[2 * px + py, mine]
                first[3 * t + j].wait_recv()
                cp = pltpu.make_async_remote_copy(
                    src_ref=landed, dst_ref=landed, send_sem=d2d_send.at[3 * t + j],
                    recv_sem=d2d_recv.at[3 * t + j], device_id=(x, y, 1 - c), device_id_type=MESH)
                cp.start()
                passed.append(cp)
        for cp in passed:
            cp.wait_recv()
        for cp in first + passed:
            cp.wait_send()
        for cp in local:
            cp.wait()

    return pl.pallas_call(
        body, name="gather_first_layer",
        in_specs=[VMEM_SPEC] * n, out_specs=[ANY] * n + [VMEM_SPEC] * n,
        out_shape=[jax.ShapeDtypeStruct((4,) + s.shape[1:], BF16) for s in shards]
        + [jax.ShapeDtypeStruct(s.shape[1:], BF16) for s in shards],
        scratch_shapes=[pltpu.VMEM(s.shape[1:], BF16) for s in shards] + _dma_sems(3 * n, 3 * n, 3 * n, 3 * n, n),
        compiler_params=pltpu.CompilerParams(vmem_limit_bytes=VMEM_LIMIT),
    )(*shards)


def _run_comm(name, comm):
    nci, nco = len(comm.ins), len(comm.out_shapes)

    def body(*refs):
        copies = comm.make(refs[:nci], refs[nci:nci + nco], refs[nci + nco:])
        for cp in copies:
            cp.start()
        for cp in copies:
            cp.wait()

    return pl.pallas_call(body, name=name, in_specs=[ANY] * nci, out_specs=[ANY] * nco,
                          out_shape=list(comm.out_shapes), scratch_shapes=list(comm.sems))(*comm.ins)


def _gather_comm(mine):
    n = len(mine)

    def make(ins, outs, sems):
        send_sems, recv_sems, local_sems = sems
        x, y, c = _me()
        k = 2 * x + y
        copies = []
        for t in range(n):
            copies.append(pltpu.make_async_copy(ins[t], outs[t].at[k], local_sems.at[t]))
            for j, (px, py) in enumerate(_other_chips(x, y)):
                copies.append(pltpu.make_async_remote_copy(
                    src_ref=ins[t], dst_ref=outs[t].at[k], send_sem=send_sems.at[3 * t + j],
                    recv_sem=recv_sems.at[3 * t + j], device_id=(px, py, c), device_id_type=MESH))
        return copies

    return _Comm(mine, [jax.ShapeDtypeStruct((4,) + a.shape, a.dtype) for a in mine], _dma_sems(3 * n, 3 * n, n), make)


def _swap_comm(grads):
    n = len(grads)

    def make(ins, outs, sems):
        send_sems, recv_sems = sems
        x, y, c = _me()
        return [pltpu.make_async_remote_copy(
            src_ref=ins[t].at[:, _half_rows(grads[t].shape[1], 1 - c, 8)], dst_ref=outs[t],
            send_sem=send_sems.at[t], recv_sem=recv_sems.at[t], device_id=(x, y, 1 - c), device_id_type=MESH)
            for t in range(n)]

    shapes = [jax.ShapeDtypeStruct((g.shape[0], g.shape[1] // 2, g.shape[2]), F32) for g in grads]
    return _Comm(grads, shapes, _dma_sems(n, n), make)


def _scatter_comm(parts):
    n = len(parts)

    def make(ins, outs, sems):
        send_sems, recv_sems = sems
        x, y, c = _me()
        k = 2 * x + y
        return [pltpu.make_async_remote_copy(
            src_ref=ins[t].at[2 * px + py], dst_ref=outs[t].at[k], send_sem=send_sems.at[3 * t + j],
            recv_sem=recv_sems.at[3 * t + j], device_id=(px, py, c), device_id_type=MESH)
            for t in range(n) for j, (px, py) in enumerate(_other_chips(x, y))]

    return _Comm(parts, [jax.ShapeDtypeStruct(p.shape, p.dtype) for p in parts], _dma_sems(3 * n, 3 * n), make)


def _share_comm(totals):
    n = len(totals)

    def make(ins, outs, sems):
        send_sems, recv_sems = sems
        x, y, c = _me()
        return [pltpu.make_async_remote_copy(
            src_ref=ins[t], dst_ref=outs[t], send_sem=send_sems.at[t], recv_sem=recv_sems.at[t],
            device_id=(x, y, 1 - c), device_id_type=MESH) for t in range(n)]

    return _Comm(totals, [jax.ShapeDtypeStruct(t.shape, F32) for t in totals], _dma_sems(n, n), make)


def _small_allreduce_adamw(part, w, m, v):
    shape = part.shape

    def body(part_ref, w_ref, m_ref, v_ref, g_ref, d_ref, nm_ref, nv_ref, slots, send_sems, recv_sems):
        x, y, c = _me()
        me = 4 * x + 2 * y + c
        slots[me] = part_ref[...]
        copies = []
        for d in range(1, 8):
            peer = (x ^ (d >> 2), y ^ ((d >> 1) & 1), c ^ (d & 1))
            cp = pltpu.make_async_remote_copy(
                src_ref=part_ref, dst_ref=slots.at[me], send_sem=send_sems.at[d - 1], recv_sem=recv_sems.at[d - 1],
                device_id=peer, device_id_type=MESH)
            cp.start()
            copies.append(cp)
        for cp in copies:
            cp.wait()
        g = slots[0]
        for i in range(1, 8):
            g = g + slots[i]
        g_ref[...] = g
        d, nm, nv = _adamw_math(w_ref[...], g, m_ref[...], v_ref[...])
        d_ref[...] = d
        nm_ref[...] = nm
        nv_ref[...] = nv

    shp = jax.ShapeDtypeStruct(shape, F32)
    return pl.pallas_call(
        body, name="small_allreduce_adamw", in_specs=[VMEM_SPEC] * 4, out_specs=[VMEM_SPEC] * 4,
        out_shape=[shp, shp, shp, shp],
        scratch_shapes=[pltpu.VMEM((8,) + shape, F32), pltpu.SemaphoreType.DMA((7,)), pltpu.SemaphoreType.DMA((7,))],
    )(part, w, m, v)


TM = 512
T_FOX = 1024
T_DIL_FWD = 1024
T_DIL_BWD = 512
TN_DW = 1408


def _layer_fwd(l, h, p_i, wts, consts, comm=None, target=None):
    w_all, _, w_out, _, w_gate, _, w_ple, norm_g, b_f, qkg, ple_g = wts
    bsum, _, bias_t, _, rope = consts
    u, z, qa, ka, va, qb, kb, vb, c_spread, c_t, va_t, vb_t = _inproj(l, h, norm_g, w_all, b_f, qkg, bsum, rope, TM)
    oa, lse_a, *landed = _attn_fwd(f"fox_fwd_l{l}", True, qa, ka, va_t, (c_spread, c_t), T_FOX, comm)
    ob, lse_b = _attn_fwd(f"dil_fwd_l{l}", False, qb, kb, vb_t, (bias_t,), T_DIL_FWD)
    h2, e, gate, *out = _outproj(l, h, oa, ob, z, p_i, w_out, ple_g, w_gate, w_ple, TM, target)
    out = out[0] if target is None else tuple(out)
    saved = (h, u, z, qa, ka, va, qb, kb, vb, c_spread, c_t, oa, lse_a, ob, lse_b, h2, e, gate)
    return out, saved, landed


def _reduce_names(tag):
    return [f"reduce_{tag}_{w}" for w in ("w_in", "w_out", "w_ple", "w_gate")]


def _layer_bwd(l, dout, p_i, wts, consts, saved, pending=None, core=None, chip=None):
    _, w_all_t, _, w_out_t, _, w_gate_t, _, norm_g, b_f, qkg, ple_g = wts
    bsum, hsel, _, bias_t, rope = consts
    h, u, z, qa, ka, va, qb, kb, vb, c_spread, c_t, oa, lse_a, ob, lse_b, h2, e, gate = saved
    fused = pending is not None
    dh2, doa, dob, dga, dgb, delta_t, dw_out, dw_gate, dw_ple, dple_g, *sib = _outproj_bwd(
        l, dout, h2, e, gate, p_i, oa, ob, z, w_out_t, ple_g, w_gate_t, hsel, TM,
        _swap_comm(pending) if fused else None)
    if fused:
        pair = [_pair_sum(n, g, x, core)[0] for n, g, x in zip(_reduce_names(f"pair_l{l + 1}"), pending, sib)]
    dqt_a, dk_a, dv_a, dc, drow, *landed = _attn_bwd(
        f"fox_bwd_l{l}", True, qa, ka, va, doa, lse_a, delta_t, 0, (c_spread, c_t), T_FOX,
        _scatter_comm(pair) if fused else None)
    if fused:
        totals = [_sum_slots(n, a, y, chip) for n, a, y in zip(_reduce_names(f"chips_l{l + 1}"), pair, landed)]
    dqt_b, dk_b, dv_b, *other = _attn_bwd(f"dil_bwd_l{l}", False, qb, kb, vb, dob, lse_b, delta_t, N_PAIRS,
                                            (bias_t,), T_DIL_BWD, _share_comm(totals) if fused else None)
    dz, dqkg, dbf = _inproj_bwd_prep(l, z, dqt_a, dk_a, dv_a, dqt_b, dk_b, dv_b, dga, dgb, dc, drow, b_f, qkg,
                                     bsum, rope, TM)
    dh, dnorm_g = _inproj_bwd_dx(l, dz, w_all_t, h, norm_g, dh2, TM)
    dw_all = _inproj_bwd_dw(l, u, dz, TM, TN_DW)
    reduced = list(zip(totals, other)) if fused else None
    return dh, (dw_all, dw_out, dw_ple, dw_gate, dnorm_g[0], dbf[0, :N_HEADS], dqkg[:4], dple_g[0]), reduced


def _reduce_last(grads, core, chip, l):
    sib = _run_comm(f"reduce_swap_l{l}", _swap_comm(grads))
    pair = [_pair_sum(n, g, x, core, narrow=True) for n, g, x in zip(_reduce_names(f"pair_l{l}"), grads, sib)]
    landed = _run_comm(f"reduce_scatter_l{l}", _scatter_comm([p[1] for p in pair]))
    totals = [_sum_slots(n, p[0], y, chip) for n, p, y in zip(_reduce_names(f"chips_l{l}"), pair, landed)]
    return list(zip(totals, _run_comm(f"reduce_share_l{l}", _share_comm(totals))))


N_FA = 2048


def _layer_weights(l, gathered, norm_g, b_f, qk_norm_g, ple_norm_g):
    g_in, g_out, g_ple, g_gate = gathered
    w_in = jnp.transpose(g_in, (1, 0, 2)).reshape(D_MODEL, N_IN)
    w_all = jnp.concatenate([w_in[:, :N_FA], w_in[:, N_FA + N_HEADS:],
                             jnp.pad(w_in[:, N_FA:N_FA + N_HEADS], ((0, 0), (0, LANE - N_HEADS)))], axis=1)
    w_out = g_out.reshape(D_MODEL, D_MODEL)
    w_gate = g_gate.reshape(D_MODEL, D_MODEL)
    w_ple = jnp.transpose(g_ple, (1, 0, 2)).reshape(PLE_DIM, D_MODEL)
    qkg = jnp.pad(jnp.tile(qk_norm_g[l], (1, N_HEADS)), ((0, 4), (0, 0)))
    bf = jnp.pad(b_f[l], (0, LANE - N_HEADS))[None, :]
    return (w_all, w_all.T, w_out, w_out.T, w_gate, w_gate.T, w_ple, norm_g[l][None, :], bf, qkg,
            ple_norm_g[l][None, :])


def _slot_layout(dw_all, dw_out, dw_ple, dw_gate):
    dw_in = jnp.concatenate([dw_all[:, :N_FA], dw_all[:, N_MAIN:N_MAIN + N_HEADS], dw_all[:, N_FA:N_MAIN]], axis=1)
    return (jnp.transpose(dw_in.reshape(D_MODEL, 4, N_IN // 4), (1, 0, 2)),
            dw_out.reshape(4, D_MODEL // 4, D_MODEL),
            jnp.transpose(dw_ple.reshape(PLE_DIM, 4, D_MODEL // 4), (1, 0, 2)),
            dw_gate.reshape(4, D_MODEL // 4, D_MODEL))


SMALL_ROWS = 40


def _pack_small(norm_g, ple_norm_g, qk_norm_g, b_f, last=0.0):
    flat = jnp.concatenate([norm_g.reshape(-1), ple_norm_g.reshape(-1), qk_norm_g.reshape(-1), b_f.reshape(-1)])
    flat = jnp.pad(flat, (0, SMALL_ROWS * LANE - flat.shape[0] - 1))
    return jnp.concatenate([flat, jnp.reshape(last, (1,)).astype(F32)]).reshape(SMALL_ROWS, LANE)


def _unpack_small(packed):
    flat = packed.reshape(-1)
    n1, n2, n3 = 2 * D_MODEL, 4 * D_MODEL, 4 * D_MODEL + 2 * 4 * HEAD_DIM
    return (flat[:n1].reshape(2, D_MODEL), flat[n1:n2].reshape(2, D_MODEL), flat[n2:n3].reshape(2, 4, HEAD_DIM),
            flat[n3:n3 + 2 * N_HEADS].reshape(2, N_HEADS))


def kernel(x, p, positions, norm_g, w_in, b_f, qk_norm_g, w_out, w_ple, ple_norm_g, w_ple_gate, loss_target,
           m_norm_g, m_w_in, m_b_f, m_qk_norm_g, m_w_out, m_w_ple, m_ple_norm_g, m_w_ple_gate,
           v_norm_g, v_w_in, v_b_f, v_qk_norm_g, v_w_out, v_w_ple, v_ple_norm_g, v_w_ple_gate):
    assert w_in.shape[0] == 2, "the schedule below is written for two layers"
    *first, = _gather_first_layer([w_in, w_out, w_ple, w_ple_gate])
    consts = (_head_block_diag(), _head_select(), _dil_bias(T_DIL_FWD), _dil_bias(T_DIL_BWD),
              _rope_tables(positions[0]))
    small_w = (norm_g, b_f, qk_norm_g, ple_norm_g)
    wts0 = _layer_weights(0, first[:4], *small_w)
    h1, saved0, second = _layer_fwd(0, x[0], p[0, 0], wts0, consts, _gather_comm(first[4:]))
    wts1 = _layer_weights(1, second, *small_w)
    (dh, sq), saved1, _ = _layer_fwd(1, h1, p[1, 0], wts1, consts, target=loss_target[0])

    core = lax.axis_index("c").astype(jnp.int32).reshape(1)
    chip = (2 * lax.axis_index("x") + lax.axis_index("y")).astype(jnp.int32).reshape(1)
    dh, grads1, _ = _layer_bwd(1, dh, p[1, 0], wts1, consts, saved1)
    dh, grads0, reduced1 = _layer_bwd(0, dh, p[0, 0], wts0, consts, saved0, _slot_layout(*grads1[:4]), core, chip)
    reduced0 = _reduce_last(_slot_layout(*grads0[:4]), core, chip, 0)
    grad_x = dh[None]
    small = [grads0[4:], grads1[4:]]
    n_layers = 2

    outs = {}
    for t, (name, w, m, v) in enumerate((("w_in", w_in, m_w_in, v_w_in), ("w_out", w_out, m_w_out, v_w_out),
                                         ("w_ple", w_ple, m_w_ple, v_w_ple),
                                         ("w_ple_gate", w_ple_gate, m_w_ple_gate, v_w_ple_gate))):
        outs[name] = tuple(_adamw(f"adamw_{name}", w, (reduced0[t], reduced1[t]), m, v, core))

    part = _pack_small(jnp.stack([s[0] for s in small]), jnp.stack([s[3] for s in small]),
                       jnp.stack([s[2] for s in small]).reshape(n_layers, 4, N_HEADS, HEAD_DIM).sum(axis=2),
                       jnp.stack([s[1] for s in small]), 0.5 / D_MODEL * jnp.sum(sq))
    packed = _small_allreduce_adamw(part, _pack_small(norm_g, ple_norm_g, qk_norm_g, b_f),
                                    _pack_small(m_norm_g, m_ple_norm_g, m_qk_norm_g, m_b_f),
                                    _pack_small(v_norm_g, v_ple_norm_g, v_qk_norm_g, v_b_f))
    loss = packed[0][SMALL_ROWS - 1, LANE - 1]
    sm = [_unpack_small(a) for a in packed]
    for i, name in enumerate(("norm_g", "ple_norm_g", "qk_norm_g", "b_f")):
        outs[name] = tuple(sm[j][i] for j in range(4))

    order = ("norm_g", "w_in", "b_f", "qk_norm_g", "w_out", "w_ple", "ple_norm_g", "w_ple_gate")
    return (loss, grad_x) + tuple(outs[n][j] for j in range(4) for n in order)
```

```python
import functools
from typing import Any, Callable, NamedTuple, Sequence

import numpy as np
import jax
import jax.numpy as jnp
from jax import lax
from jax.experimental import pallas as pl
from jax.experimental.pallas import tpu as pltpu

F32 = jnp.float32
BF16 = jnp.bfloat16
MESH = pl.DeviceIdType.MESH

D_MODEL = 1024
HEAD_DIM = 64
D_BRANCH = 512
N_HEADS = 8
N_PAIRS = 4
N_IN = 4104
N_MAIN = 4096
N_ALL = 4224
PLE_DIM = 256
ROPE_THETA = 500000.0
ROPE_HALF = 8
EPS = 1e-6
NEG = -1e30
M_INIT = -1e29
Q_SCALE = HEAD_DIM ** -0.5
LOG2E = 1.4426950408889634
LN2 = 0.6931471805599453
DIL_PATTERNS = ((128, 1), (512, 4), (2048, 16))
DIL_BACK = 2048
ADAM_LR, ADAM_B1, ADAM_B2, ADAM_EPS, ADAM_WD, ADAM_STEP = 0.001, 0.9, 0.999, 1e-08, 0.01, 10
VMEM_LIMIT = 56 * 1024 * 1024
LANE = 128


def _dot(a, b):
    return jnp.dot(a, b, preferred_element_type=F32)


def _dot_nt(a, b):
    return lax.dot_general(a, b, (((1,), (1,)), ((), ())), preferred_element_type=F32)


def _dot_tn(a, b):
    return lax.dot_general(a, b, (((0,), (0,)), ((), ())), preferred_element_type=F32)


def _split_dot(x, w):
    hi = x.astype(BF16)
    lo = (x - hi.astype(F32)).astype(BF16)
    return _dot(hi, w) + _dot(lo, w)


def _head_sums(x, bs):
    w = bs.shape[0]
    return jnp.concatenate([_split_dot(x[:, w * k:w * (k + 1)], bs) for k in range(x.shape[1] // w)], axis=1)


def _split3_dot(w, x):
    hi = x.astype(BF16)
    r1 = x - hi.astype(F32)
    mid = r1.astype(BF16)
    lo = (r1 - mid.astype(F32)).astype(BF16)
    return _dot(w, hi) + _dot(w, mid) + _dot(w, lo)


def _sigmoid(x):
    return 1.0 / (1.0 + jnp.exp(-x))


def _params(n_grid):
    return pltpu.CompilerParams(dimension_semantics=("arbitrary",) * n_grid,
                                vmem_limit_bytes=VMEM_LIMIT)


def _full(shape):
    nd = len(shape)
    return pl.BlockSpec(shape, lambda *_: (0,) * nd)


ANY = pl.BlockSpec(memory_space=pl.ANY)
VMEM_SPEC = pl.BlockSpec(memory_space=pltpu.VMEM)


class _Comm(NamedTuple):
    ins: Sequence[Any]
    out_shapes: Sequence[Any]
    sems: Sequence[Any]
    make: Callable[..., Any]


def _fuse_comm(body, n_in, n_out, comm, grid):
    if comm is None:
        return body
    nci, nco, ncs = len(comm.ins), len(comm.out_shapes), len(comm.sems)

    def fused(*refs):
        a, b = n_in + nci, n_in + nci + n_out
        ins, cins, outs, couts = refs[:n_in], refs[n_in:a], refs[a:b], refs[b:b + nco]
        scratch, sems = refs[b + nco:len(refs) - ncs], refs[len(refs) - ncs:]
        first = functools.reduce(jnp.logical_and, [pl.program_id(d) == 0 for d in range(len(grid))])
        last = functools.reduce(jnp.logical_and, [pl.program_id(d) == n - 1 for d, n in enumerate(grid)])

        @pl.when(first)
        def _():
            for cp in comm.make(cins, couts, sems):
                cp.start()

        body(*ins, *outs, *scratch)

        @pl.when(last)
        def _():
            for cp in comm.make(cins, couts, sems):
                cp.wait()

    return fused


def _comm_args(comm):
    if comm is None:
        return [], [], [], [], []
    return (list(comm.ins), [ANY] * len(comm.ins), [ANY] * len(comm.out_shapes), list(comm.out_shapes),
            list(comm.sems))


HEADS_PER_BLOCK = 4


def _head_block_diag():
    i = np.arange(HEADS_PER_BLOCK * HEAD_DIM)
    return jnp.asarray((i[:, None] // HEAD_DIM == i[None, :] // HEAD_DIM).astype(np.float32), BF16)


def _head_select():
    i = np.arange(2 * D_BRANCH)
    j = np.arange(LANE)
    return jnp.asarray((i[:, None] // HEAD_DIM == j[None, :]).astype(np.float32), BF16)


def _dil_bias(t):
    nb = DIL_BACK // t + 1
    qi = np.arange(t)[:, None]
    ki = np.arange(t)[None, :]
    tiles = []
    for r in range(nb):
        d = r * t + qi - ki
        mult = np.zeros((t, t), np.int64)
        for window, dil in DIL_PATTERNS:
            mult += ((d >= 0) & (d <= window) & (d % dil == 0)).astype(np.int64)
        b = np.where(mult > 0, np.log2(np.maximum(mult, 1)), NEG).astype(np.float32)
        tiles.append(b.T)
    return jnp.asarray(np.stack(tiles))


def _rope_tables(positions):
    inv_freq = ROPE_THETA ** (-jnp.arange(ROPE_HALF, dtype=F32) / ROPE_HALF)
    ang = positions.astype(F32)[:, None] * inv_freq
    cos, sin = jnp.cos(ang), jnp.sin(ang)
    s = positions.shape[0]
    rest = HEAD_DIM - 2 * ROPE_HALF
    one, zero, zero8 = jnp.ones((s, rest), F32), jnp.zeros((s, rest), F32), jnp.zeros((s, ROPE_HALF), F32)
    c = jnp.concatenate([cos, cos, one], axis=1)
    s1 = jnp.concatenate([zero8, sin, zero], axis=1)
    s2 = jnp.concatenate([-sin, zero8, zero], axis=1)
    return tuple(jnp.tile(t, (1, 2)) for t in (c, s1, s2))


def _rope_fwd(x, c, s1, s2):
    return x * c + pltpu.roll(x, ROPE_HALF, 1) * s1 + pltpu.roll(x, LANE - ROPE_HALF, 1) * s2


def _rope_bwd(dy, c, s1, s2):
    return dy * c + pltpu.roll(dy * s1, LANE - ROPE_HALF, 1) + pltpu.roll(dy * s2, ROPE_HALF, 1)


def _log_sigmoid(x):
    return jnp.minimum(x, 0.0) - jnp.log(1.0 + jnp.exp(-jnp.abs(x)))


def _inproj(l, h, norm_g, w_all, b_f, qkg, bsum, rope, tm):
    s = h.shape[0]
    rc, rs1, rs2 = rope

    def body(h_ref, g_ref, w_ref, bf_ref, qkg_ref, bsum_ref, rc_ref, rs1_ref, rs2_ref,
             u_ref, z_ref, qa_ref, ka_ref, va_ref, qb_ref, kb_ref, vb_ref, cs_ref, ct_ref, vat_ref, vbt_ref, carry):
        @pl.when(pl.program_id(0) == 0)
        def _():
            carry[...] = jnp.zeros_like(carry)

        hh = h_ref[...]
        r = lax.rsqrt(jnp.mean(hh * hh, axis=-1, keepdims=True) + EPS)
        u = (hh * r * g_ref[...]).astype(BF16)
        u_ref[...] = u
        for k in range(N_ALL // LANE // 3):
            cols = slice(3 * LANE * k, 3 * LANE * (k + 1))
            z_ref[:, cols] = _dot(u, w_ref[:, cols])
        bs = bsum_ref[...]

        def head_norm(x, row):
            ms = _head_sums(x * x, bs) * (1.0 / HEAD_DIM)
            return x * lax.rsqrt(ms + EPS) * qkg_ref[row:row + 1, :]

        def seg(k):
            return z_ref[:, D_BRANCH * k:D_BRANCH * (k + 1)]

        qa_ref[...] = (head_norm(seg(0), 0) * (Q_SCALE * LOG2E)).astype(BF16)
        ka_ref[...] = head_norm(seg(1), 1).astype(BF16)
        va_ref[...] = seg(2).astype(BF16)
        vat_ref[...] = seg(2).T.astype(BF16)
        qn = head_norm(seg(4), 2) * (Q_SCALE * LOG2E)
        kn = head_norm(seg(5), 3)
        c, s1, s2 = rc_ref[...], rs1_ref[...], rs2_ref[...]
        for k in range(D_BRANCH // LANE):
            cols = slice(LANE * k, LANE * (k + 1))
            qb_ref[:, cols] = _rope_fwd(qn[:, cols], c, s1, s2).astype(BF16)
            kb_ref[:, cols] = _rope_fwd(kn[:, cols], c, s1, s2).astype(BF16)
        vb_ref[...] = seg(6).astype(BF16)
        vbt_ref[...] = seg(6).T.astype(BF16)
        logf = _log_sigmoid(z_ref[:, N_MAIN:N_ALL] + bf_ref[...])
        csum = _split3_dot(_tri(tm, False), logf) + carry[0:1, :]
        carry[...] = jnp.broadcast_to(csum[tm - 1:tm, :], carry.shape)
        csum = csum * LOG2E
        ct = csum.T
        for p in range(N_PAIRS):
            cs_ref[:, LANE * p:LANE * (p + 1)] = csum if p == 0 else pltpu.roll(csum, LANE - 2 * p, 1)
            ct_ref[p, :, :] = ct[2 * p:2 * p + 2, :]

    row = lambda w: pl.BlockSpec((tm, w), lambda i: (i, 0))
    colt = pl.BlockSpec((D_BRANCH, tm), lambda i: (0, i))
    bf = lambda: jax.ShapeDtypeStruct((s, D_BRANCH), BF16)
    bft = lambda: jax.ShapeDtypeStruct((D_BRANCH, s), BF16)
    return pl.pallas_call(
        body, name=f"inproj_l{l}", grid=(s // tm,),
        in_specs=[row(D_MODEL), _full((1, D_MODEL)), _full((D_MODEL, N_ALL)), _full((1, LANE)),
                  _full((8, D_BRANCH)), _full((HEADS_PER_BLOCK * HEAD_DIM,) * 2), row(LANE), row(LANE), row(LANE)],
        out_specs=[row(D_MODEL), row(N_ALL)] + [row(D_BRANCH)] * 6
        + [row(N_PAIRS * LANE), pl.BlockSpec((N_PAIRS, 2, tm), lambda i: (0, 0, i)), colt, colt],
        out_shape=[jax.ShapeDtypeStruct((s, D_MODEL), BF16), jax.ShapeDtypeStruct((s, N_ALL), F32),
                   bf(), bf(), bf(), bf(), bf(), bf(), jax.ShapeDtypeStruct((s, N_PAIRS * LANE), F32),
                   jax.ShapeDtypeStruct((N_PAIRS, 2, s), F32), bft(), bft()],
        scratch_shapes=[pltpu.VMEM((8, LANE), F32)],
        compiler_params=_params(1),
    )(h, norm_g, w_all, b_f, qkg, bsum, rc, rs1, rs2)


def _tri(t, upper):
    a = lax.broadcasted_iota(jnp.int32, (t, t), 0)
    b = lax.broadcasted_iota(jnp.int32, (t, t), 1)
    return jnp.where((b >= a) if upper else (b <= a), 1.0, 0.0).astype(BF16)


def _attn_fwd(name, fox, q, k, vt, extra, t, comm=None):
    s = q.shape[0]
    nq = s // t
    nb = DIL_BACK // t + 1

    def body(*refs):
        if fox:
            q_ref, k_ref, vt_ref, ccol_ref, crow_ref, o_ref, lse_ref, m_scr, l_scr, acc_scr = refs
        else:
            q_ref, k_ref, vt_ref, bias_ref, o_ref, lse_ref, m_scr, l_scr, acc_scr = refs
        i = pl.program_id(1)
        lane = lax.broadcasted_iota(jnp.int32, (t, LANE), 1)
        first = lane < HEAD_DIM
        qq = q_ref[...]
        zero = jnp.zeros_like(qq)
        qh = (jnp.where(first, qq, zero), jnp.where(first, zero, qq))
        m_scr[...] = jnp.full(m_scr.shape, M_INIT, F32)
        l_scr[...] = jnp.zeros_like(l_scr)
        acc_scr[...] = jnp.zeros_like(acc_scr)
        ones = jnp.ones((16, t), BF16)

        half = t // 2
        whole, lo, hi = slice(0, t), slice(0, half), slice(half, t)

        def block(j, ksl, qsl, causal):
            nk_, nq_ = ksl.stop - ksl.start, qsl.stop - qsl.start
            rows = pl.ds(pl.multiple_of(j * t + ksl.start, LANE), nk_)
            ks = k_ref[rows, :]
            vts = jnp.concatenate([vt_ref[:, rows], ones[:, :nk_]], axis=0)
            if fox:
                ccol = ccol_ref[rows, :]
            for h in range(2):
                st = _dot_nt(ks, qh[h][qsl, :])
                if fox:
                    st = st + (crow_ref[h:h + 1, qsl] - ccol[:, h:h + 1])
                    if causal:
                        ki = lax.broadcasted_iota(jnp.int32, (nk_, nq_), 0) + ksl.start
                        qi = lax.broadcasted_iota(jnp.int32, (nk_, nq_), 1) + qsl.start
                        st = jnp.where(ki <= qi, st, NEG)
                else:
                    st = st + bias_ref[i - j, ksl, qsl]
                m_old = m_scr[h, :, qsl]
                m_new = jnp.maximum(m_old, jnp.max(st, axis=0, keepdims=True))
                alpha = jnp.exp2(m_old - m_new)
                pb = jnp.exp2(st - m_new).astype(BF16)
                pv = _dot(vts, pb)
                l_scr[h, :, qsl] = alpha * l_scr[h, :, qsl] + pv[LANE:LANE + 1, :]
                acc_scr[h, :, qsl] = alpha * acc_scr[h, :, qsl] + pv[:LANE, :]
                m_scr[h, :, qsl] = m_new

        def full(j, c):
            block(j, whole, whole, False)
            return c

        if fox:
            lax.fori_loop(0, i, full, 0)
        else:
            @pl.when(i >= nb - 1)
            def _():
                block(i - (nb - 1), lo, lo, False)
                block(i - (nb - 1), hi, whole, False)

            lax.fori_loop(jnp.maximum(i - (nb - 2), 0), i, full, 0)
        if fox:
            block(i, whole, whole, True)
        else:
            block(i, lo, lo, False)
            block(i, whole, hi, False)

        sub = lax.broadcasted_iota(jnp.int32, (LANE, t), 0)
        ot = jnp.where(sub < HEAD_DIM, acc_scr[0] / l_scr[0], acc_scr[1] / l_scr[1])
        o_ref[...] = ot.T
        for h in range(2):
            lse_ref[h:h + 1, :] = m_scr[h] + jnp.log2(l_scr[h])

    qspec = pl.BlockSpec((t, LANE), lambda hp, i: (i, hp))
    kspec = pl.BlockSpec((s, LANE), lambda hp, i: (0, hp))
    vtspec = pl.BlockSpec((LANE, s), lambda hp, i: (hp, 0))
    in_specs = [qspec, kspec, vtspec]
    if fox:
        in_specs += [kspec, pl.BlockSpec((None, 2, t), lambda hp, i: (hp, 0, i))]
    else:
        in_specs += [_full((nb, t, t))]
    grid = (N_PAIRS, nq)
    c_in, c_ispec, c_ospec, c_oshape, c_scr = _comm_args(comm)
    return pl.pallas_call(
        _fuse_comm(body, len(in_specs), 2, comm, grid), name=name, grid=grid,
        in_specs=in_specs + c_ispec,
        out_specs=[qspec, pl.BlockSpec((None, 2, t), lambda hp, i: (hp, 0, i))] + c_ospec,
        out_shape=[jax.ShapeDtypeStruct((s, D_BRANCH), F32), jax.ShapeDtypeStruct((N_PAIRS, 2, s), F32)] + c_oshape,
        scratch_shapes=[pltpu.VMEM((2, 1, t), F32), pltpu.VMEM((2, 1, t), F32), pltpu.VMEM((2, LANE, t), F32)]
        + c_scr,
        compiler_params=_params(2),
    )(q, k, vt, *extra, *c_in)


def _attn_bwd(name, fox, q, k, v, do, lse_t, delta_t, pair_offset, extra, t, comm=None):
    s = q.shape[0]
    nk = s // t
    nb = DIL_BACK // t + 1

    def body(*refs):
        if fox:
            (q_ref, k_ref, v_ref, do_ref, lse_ref, delta_ref, ccol_ref, crow_ref,
             dqt_ref, dk_ref, dv_ref, dc_ref, drow_ref) = refs
        else:
            q_ref, k_ref, v_ref, do_ref, lse_ref, delta_ref, bias_ref, dqt_ref, dk_ref, dv_ref = refs
        j = pl.program_id(1)

        @pl.when(j == 0)
        def _():
            dqt_ref[...] = jnp.zeros_like(dqt_ref)
            if fox:
                drow_ref[...] = jnp.zeros_like(drow_ref)

        lane = lax.broadcasted_iota(jnp.int32, (t, LANE), 1)
        first = lane < HEAD_DIM
        ks = k_ref[...]
        vs = v_ref[...]
        kt = ks.astype(F32).T
        sub = lax.broadcasted_iota(jnp.int32, (LANE, t), 0)
        kth = (jnp.where(sub < HEAD_DIM, kt, 0.0).astype(BF16), jnp.where(sub < HEAD_DIM, 0.0, kt).astype(BF16))
        dk_ref[...] = jnp.zeros_like(dk_ref)
        dv_ref[...] = jnp.zeros_like(dv_ref)
        if fox:
            dc_ref[...] = jnp.zeros_like(dc_ref)
            ccol = ccol_ref[...]

        half = t // 2
        whole, lo, hi = slice(0, t), slice(0, half), slice(half, t)

        def block(i, ksl, qsl, causal):
            nk_, nq_ = ksl.stop - ksl.start, qsl.stop - qsl.start
            rows = pl.ds(pl.multiple_of(i * t + qsl.start, LANE), nq_)
            qq = q_ref[rows, :]
            dd = do_ref[rows, :]
            zero = jnp.zeros_like(qq)
            qh = (jnp.where(first[:nq_], qq, zero), jnp.where(first[:nq_], zero, qq))
            dh = (jnp.where(first[:nq_], dd, zero), jnp.where(first[:nq_], zero, dd))
            for h in range(2):
                st = _dot_nt(ks[ksl, :], qh[h])
                if fox:
                    st = st + (crow_ref[h:h + 1, rows] - ccol[ksl, h:h + 1])
                    if causal:
                        ki = lax.broadcasted_iota(jnp.int32, (nk_, nq_), 0) + ksl.start
                        qi = lax.broadcasted_iota(jnp.int32, (nk_, nq_), 1) + qsl.start
                        st = jnp.where(ki <= qi, st, NEG)
                else:
                    st = st + bias_ref[i - j, ksl, qsl]
                pt = jnp.exp2(st - lse_ref[h:h + 1, rows])
                dpt = _dot_nt(vs[ksl, :], dh[h])
                dst = pt * (dpt - delta_ref[h:h + 1, rows])
                dv_ref[ksl, :] += _dot(pt.astype(BF16), dh[h])
                dsb = dst.astype(BF16)
                dk_ref[ksl, :] += _dot(dsb, qh[h])
                dqt_ref[:, rows] += _dot(kth[h][:, ksl], dsb)
                if fox:
                    dc_ref[ksl, :] -= jnp.where(lane[:nk_] == h, jnp.sum(dst, axis=1, keepdims=True), 0.0)
                    drow_ref[h:h + 1, rows] += jnp.sum(dst, axis=0, keepdims=True)

        def full(i, c):
            block(i, whole, whole, False)
            return c

        block(j, lo, whole, True)
        block(j, hi, hi, True)
        if fox:
            lax.fori_loop(j + 1, nk, full, 0)
        else:
            lax.fori_loop(j + 1, jnp.minimum(j + nb - 1, nk), full, 0)

            @pl.when(j + nb - 1 < nk)
            def _():
                block(j + nb - 1, lo, lo, False)
                block(j + nb - 1, hi, whole, False)

    kspec = pl.BlockSpec((t, LANE), lambda hp, j: (j, hp))
    qspec = pl.BlockSpec((s, LANE), lambda hp, j: (0, hp))
    rowspec = pl.BlockSpec((None, 2, s), lambda hp, j: (hp, 0, 0))
    drowspec = pl.BlockSpec((None, 2, s), lambda hp, j: (hp + pair_offset, 0, 0))
    in_specs = [qspec, kspec, kspec, qspec, rowspec, drowspec]
    out_specs = [pl.BlockSpec((LANE, s), lambda hp, j: (hp, 0)), kspec, kspec]
    out_shape = [jax.ShapeDtypeStruct((D_BRANCH, s), F32), jax.ShapeDtypeStruct((s, D_BRANCH), F32),
                 jax.ShapeDtypeStruct((s, D_BRANCH), F32)]
    if fox:
        in_specs += [kspec, rowspec]
        out_specs += [kspec, rowspec]
        out_shape += [jax.ShapeDtypeStruct((s, N_PAIRS * LANE), F32), jax.ShapeDtypeStruct((N_PAIRS, 2, s), F32)]
    else:
        in_specs += [_full((nb, t, t))]
    grid = (N_PAIRS, nk)
    c_in, c_ispec, c_ospec, c_oshape, c_scr = _comm_args(comm)
    return pl.pallas_call(
        _fuse_comm(body, len(in_specs), len(out_specs), comm, grid), name=name, grid=grid,
        in_specs=in_specs + c_ispec, out_specs=out_specs + c_ospec, out_shape=out_shape + c_oshape,
        scratch_shapes=c_scr, compiler_params=_params(2),
    )(q, k, v, do, lse_t, delta_t, *extra, *c_in)


def _silu(x):
    return x * _sigmoid(x)


def _outproj(l, h, oa, ob, z, p_i, w_out, ple_g, w_gate, w_ple, tm, target=None):
    s = h.shape[0]
    last = target is not None

    def body(h_ref, oa_ref, ob_ref, ga_ref, gb_ref, p_ref, wo_ref, pg_ref, wg_ref, wp_ref, *rest):
        if last:
            t_ref, h2_ref, e_ref, gate_ref, out_ref, acc_ref = rest
        else:
            h2_ref, e_ref, gate_ref, out_ref = rest
        a = jnp.concatenate([oa_ref[...] * _silu(ga_ref[...]), ob_ref[...] * _silu(gb_ref[...])], axis=1)
        h2 = h_ref[...] + _dot(a.astype(BF16), wo_ref[...])
        h2_ref[...] = h2
        r = lax.rsqrt(jnp.mean(h2 * h2, axis=-1, keepdims=True) + EPS)
        n2 = (h2 * r * pg_ref[...]).astype(BF16)
        gate = _sigmoid(_dot(n2, wg_ref[...]))
        e = _dot(p_ref[...].astype(BF16), wp_ref[...])
        e_ref[...] = e
        gate_ref[...] = gate
        out = h2 + e * gate
        if not last:
            out_ref[...] = out
            return

        @pl.when(pl.program_id(0) == 0)
        def _():
            acc_ref[...] = jnp.zeros_like(acc_ref)

        err = out - t_ref[...]
        out_ref[...] = err * (1.0 / D_MODEL)
        e2 = err * err
        rows = e2[0:8, :]
        for k in range(1, tm // 8):
            rows = rows + e2[8 * k:8 * (k + 1), :]
        part = rows[:, 0:LANE]
        for k in range(1, D_MODEL // LANE):
            part = part + rows[:, LANE * k:LANE * (k + 1)]
        acc_ref[...] += part

    row = lambda w: pl.BlockSpec((tm, w), lambda i: (i, 0))
    zcol = lambda k: pl.BlockSpec((tm, D_BRANCH), lambda i: (i, k))
    f = lambda: jax.ShapeDtypeStruct((s, D_MODEL), F32)
    return pl.pallas_call(
        body, name=f"outproj_l{l}", grid=(s // tm,),
        in_specs=[row(D_MODEL), row(D_BRANCH), row(D_BRANCH), zcol(3), zcol(7), row(PLE_DIM),
                  _full((D_MODEL, D_MODEL)), _full((1, D_MODEL)), _full((D_MODEL, D_MODEL)),
                  _full((PLE_DIM, D_MODEL))] + ([row(D_MODEL)] if last else []),
        out_specs=[row(D_MODEL)] * 4 + ([_full((8, LANE))] if last else []),
        out_shape=[f(), f(), f(), f()] + ([jax.ShapeDtypeStruct((8, LANE), F32)] if last else []),
        compiler_params=_params(1),
    )(h, oa, ob, z, z, p_i, w_out, ple_g, w_gate, w_ple, *([target] if last else []))


def _outproj_bwd(l, dout, h2, e, gate, p_i, oa, ob, z, w_out_t, ple_g, w_gate_t, hsel, tm, comm=None):
    s = dout.shape[0]

    def body(do_ref, h2_ref, e_ref, gate_ref, p_ref, oa_ref, ob_ref, ga_ref, gb_ref, wot_ref, pg_ref,
             wgt_ref, hsel_ref,
             dh2_ref, doa_ref, dob_ref, dga_ref, dgb_ref, delta_ref, dwo_ref, dwg_ref, dwp_ref, dpg_ref):
        @pl.when(pl.program_id(0) == 0)
        def _():
            dwo_ref[...] = jnp.zeros_like(dwo_ref)
            dwg_ref[...] = jnp.zeros_like(dwg_ref)
            dwp_ref[...] = jnp.zeros_like(dwp_ref)
            dpg_ref[...] = jnp.zeros_like(dpg_ref)

        dho = do_ref[...]
        g = gate_ref[...]
        de = (dho * g).astype(BF16)
        dwp_ref[...] += _dot_tn(p_ref[...].astype(BF16), de)
        dpre = (dho * e_ref[...] * g * (1.0 - g)).astype(BF16)
        h2 = h2_ref[...]
        pg = pg_ref[...]
        r = lax.rsqrt(jnp.mean(h2 * h2, axis=-1, keepdims=True) + EPS)
        n2 = (h2 * r * pg).astype(BF16)
        dwg_ref[...] += _dot_tn(n2, dpre)
        dn2 = _dot(dpre, wgt_ref[...])
        dpg_ref[0:1, :] += jnp.sum(dn2 * h2 * r, axis=0, keepdims=True)
        wv = dn2 * pg
        dh2 = dho + r * wv - h2 * (r * r * r) * jnp.mean(wv * h2, axis=-1, keepdims=True)
        dh2_ref[...] = dh2
        dh2b = dh2.astype(BF16)
        ga, gb, oa, ob = ga_ref[...], gb_ref[...], oa_ref[...], ob_ref[...]
        sga, sgb = _sigmoid(ga), _sigmoid(gb)
        a = jnp.concatenate([oa * ga * sga, ob * gb * sgb], axis=1).astype(BF16)
        dwo_ref[...] += _dot_tn(a, dh2b)
        da = _dot(dh2b, wot_ref[...])
        da_a, da_b = da[:, :D_BRANCH], da[:, D_BRANCH:]
        doa = da_a * ga * sga
        dob = da_b * gb * sgb
        doa_ref[...] = doa.astype(BF16)
        dob_ref[...] = dob.astype(BF16)
        dga_ref[...] = (da_a * oa * sga * (1.0 + ga * (1.0 - sga))).astype(BF16)
        dgb_ref[...] = (da_b * ob * sgb * (1.0 + gb * (1.0 - sgb))).astype(BF16)
        prod = jnp.concatenate([doa * oa, dob * ob], axis=1)
        dt = _split_dot(prod, hsel_ref[...]).T
        for pp in range(2 * N_PAIRS):
            delta_ref[pp, :, :] = dt[2 * pp:2 * pp + 2, :]

    row = lambda w: pl.BlockSpec((tm, w), lambda i: (i, 0))
    zcol = lambda k: pl.BlockSpec((tm, D_BRANCH), lambda i: (i, k))
    grid = (s // tm,)
    c_in, c_ispec, c_ospec, c_oshape, c_scr = _comm_args(comm)
    return pl.pallas_call(
        _fuse_comm(body, 13, 10, comm, grid), name=f"outproj_bwd_l{l}", grid=grid,
        in_specs=[row(D_MODEL)] * 4 + [row(PLE_DIM), row(D_BRANCH), row(D_BRANCH), zcol(3), zcol(7),
                                        _full((D_MODEL, D_MODEL)), _full((1, D_MODEL)), _full((D_MODEL, D_MODEL)),
                                        _full((2 * D_BRANCH, LANE))] + c_ispec,
        out_specs=[row(D_MODEL)] + [row(D_BRANCH)] * 4
        + [pl.BlockSpec((2 * N_PAIRS, 2, tm), lambda i: (0, 0, i)), _full((D_MODEL, D_MODEL)),
           _full((D_MODEL, D_MODEL)), _full((PLE_DIM, D_MODEL)), _full((8, D_MODEL))] + c_ospec,
        out_shape=[jax.ShapeDtypeStruct((s, D_MODEL), F32)] + [jax.ShapeDtypeStruct((s, D_BRANCH), BF16)] * 4
        + [jax.ShapeDtypeStruct((2 * N_PAIRS, 2, s), F32), jax.ShapeDtypeStruct((D_MODEL, D_MODEL), F32),
           jax.ShapeDtypeStruct((D_MODEL, D_MODEL), F32), jax.ShapeDtypeStruct((PLE_DIM, D_MODEL), F32),
           jax.ShapeDtypeStruct((8, D_MODEL), F32)] + c_oshape,
        scratch_shapes=c_scr, compiler_params=_params(1),
    )(dout, h2, e, gate, p_i, oa, ob, z, z, w_out_t, ple_g, w_gate_t, hsel, *c_in)


def _inproj_bwd_prep(l, z, dqt_a, dk_a, dv_a, dqt_b, dk_b, dv_b, dga, dgb, dc_spread, dc_rows, b_f, qkg, bsum,
                     rope, tm):
    s = z.shape[0]
    n = s // tm
    rc, rs1, rs2 = rope

    def body(z_ref, dqta_ref, dka_ref, dva_ref, dqtb_ref, dkb_ref, dvb_ref, dga_ref, dgb_ref, dc_ref, drow_ref,
             bf_ref, qkg_ref, bsum_ref, rc_ref, rs1_ref, rs2_ref, dz_ref, dqkg_ref, dbf_ref, carry):
        @pl.when(pl.program_id(0) == 0)
        def _():
            dqkg_ref[...] = jnp.zeros_like(dqkg_ref)
            dbf_ref[...] = jnp.zeros_like(dbf_ref)
            carry[...] = jnp.zeros_like(carry)

        lane = lax.broadcasted_iota(jnp.int32, (tm, LANE), 1)
        dc = jnp.concatenate([drow_ref[p] for p in range(N_PAIRS)] + [jnp.zeros((LANE - N_HEADS, tm), F32)], axis=0).T
        for p in range(N_PAIRS):
            part = jnp.where(lane < 2, dc_ref[:, LANE * p:LANE * (p + 1)], 0.0)
            dc = dc + (part if p == 0 else pltpu.roll(part, 2 * p, 1))
        dlogf = _split3_dot(_tri(tm, True), dc) + carry[0:1, :]
        carry[...] = jnp.broadcast_to(dlogf[0:1, :], carry.shape)

        bs = bsum_ref[...]
        c, s1, s2 = rc_ref[...], rs1_ref[...], rs2_ref[...]

        def unrope(dy):
            return jnp.concatenate([_rope_bwd(dy[:, LANE * k:LANE * (k + 1)], c, s1, s2)
                                    for k in range(D_BRANCH // LANE)], axis=1)

        def norm_bwd(k, row, dy):
            x = z_ref[:, D_BRANCH * k:D_BRANCH * (k + 1)]
            r = lax.rsqrt(_head_sums(x * x, bs) * (1.0 / HEAD_DIM) + EPS)
            dqkg_ref[row:row + 1, :] += jnp.sum(dy * x * r, axis=0, keepdims=True)
            w = dy * qkg_ref[row:row + 1, :]
            dx = r * w - x * (r * r * r) * (_head_sums(w * x, bs) * (1.0 / HEAD_DIM))
            dz_ref[:, D_BRANCH * k:D_BRANCH * (k + 1)] = dx.astype(BF16)

        norm_bwd(0, 0, dqta_ref[...].T * Q_SCALE)
        norm_bwd(1, 1, dka_ref[...] * LN2)
        dz_ref[:, 2 * D_BRANCH:3 * D_BRANCH] = dva_ref[...].astype(BF16)
        dz_ref[:, 3 * D_BRANCH:4 * D_BRANCH] = dga_ref[...]
        norm_bwd(4, 2, unrope(dqtb_ref[...].T * Q_SCALE))
        norm_bwd(5, 3, unrope(dkb_ref[...] * LN2))
        dz_ref[:, 6 * D_BRANCH:7 * D_BRANCH] = dvb_ref[...].astype(BF16)
        dz_ref[:, 7 * D_BRANCH:8 * D_BRANCH] = dgb_ref[...]
        dfa = dlogf * _sigmoid(-(z_ref[:, N_MAIN:N_ALL] + bf_ref[...]))
        dz_ref[:, N_MAIN:N_ALL] = dfa.astype(BF16)
        dbf_ref[0:1, :] += jnp.sum(dfa, axis=0, keepdims=True)

    row = lambda w: pl.BlockSpec((tm, w), lambda i: (n - 1 - i, 0))
    colt = pl.BlockSpec((D_BRANCH, tm), lambda i: (0, n - 1 - i))
    return pl.pallas_call(
        body, name=f"inproj_bwd_prep_l{l}", grid=(n,),
        in_specs=[row(N_ALL), colt, row(D_BRANCH), row(D_BRANCH), colt, row(D_BRANCH), row(D_BRANCH),
                  row(D_BRANCH), row(D_BRANCH), row(N_PAIRS * LANE),
                  pl.BlockSpec((N_PAIRS, 2, tm), lambda i: (0, 0, n - 1 - i)), _full((1, LANE)),
                  _full((8, D_BRANCH)), _full((HEADS_PER_BLOCK * HEAD_DIM,) * 2), row(LANE), row(LANE), row(LANE)],
        out_specs=[row(N_ALL), _full((8, D_BRANCH)), _full((8, LANE))],
        out_shape=[jax.ShapeDtypeStruct((s, N_ALL), BF16), jax.ShapeDtypeStruct((8, D_BRANCH), F32),
                   jax.ShapeDtypeStruct((8, LANE), F32)],
        scratch_shapes=[pltpu.VMEM((8, LANE), F32)],
        compiler_params=_params(1),
    )(z, dqt_a, dk_a, dv_a, dqt_b, dk_b, dv_b, dga, dgb, dc_spread, dc_rows, b_f, qkg, bsum, rc, rs1, rs2)


def _inproj_bwd_dx(l, dz, w_all_t, h, norm_g, dh2, tm):
    s = dz.shape[0]

    def body(dz_ref, wt_ref, h_ref, g_ref, dh2_ref, dh_ref, dg_ref):
        @pl.when(pl.program_id(0) == 0)
        def _():
            dg_ref[...] = jnp.zeros_like(dg_ref)

        du = _dot(dz_ref[...], wt_ref[...])
        hh = h_ref[...]
        g = g_ref[...]
        r = lax.rsqrt(jnp.mean(hh * hh, axis=-1, keepdims=True) + EPS)
        dg_ref[0:1, :] += jnp.sum(du * hh * r, axis=0, keepdims=True)
        wv = du * g
        dh_ref[...] = dh2_ref[...] + r * wv - hh * (r * r * r) * jnp.mean(wv * hh, axis=-1, keepdims=True)

    row = lambda w: pl.BlockSpec((tm, w), lambda i: (i, 0))
    return pl.pallas_call(
        body, name=f"inproj_bwd_dx_l{l}", grid=(s // tm,),
        in_specs=[row(N_ALL), _full((N_ALL, D_MODEL)), row(D_MODEL), _full((1, D_MODEL)), row(D_MODEL)],
        out_specs=[row(D_MODEL), _full((8, D_MODEL))],
        out_shape=[jax.ShapeDtypeStruct((s, D_MODEL), F32), jax.ShapeDtypeStruct((8, D_MODEL), F32)],
        compiler_params=_params(1),
    )(dz, w_all_t, h, norm_g, dh2)


def _inproj_bwd_dw(l, u, dz, tm, tn):
    s = u.shape[0]

    def body(u_ref, dz_ref, dw_ref):
        @pl.when(pl.program_id(1) == 0)
        def _():
            dw_ref[...] = jnp.zeros_like(dw_ref)

        dw_ref[...] += _dot_tn(u_ref[...], dz_ref[...])

    return pl.pallas_call(
        body, name=f"inproj_bwd_dw_l{l}", grid=(N_ALL // tn, s // tm),
        in_specs=[pl.BlockSpec((tm, D_MODEL), lambda n, i: (i, 0)), pl.BlockSpec((tm, tn), lambda n, i: (i, n))],
        out_specs=pl.BlockSpec((D_MODEL, tn), lambda n, i: (0, n)),
        out_shape=jax.ShapeDtypeStruct((D_MODEL, N_ALL), F32),
        compiler_params=_params(2),
    )(u, dz)


def _adamw_math(w, g, m, v):
    m = ADAM_B1 * m + (1.0 - ADAM_B1) * g
    v = ADAM_B2 * v + (1.0 - ADAM_B2) * (g * g)
    m_hat = m / (1.0 - ADAM_B1 ** ADAM_STEP)
    v_hat = v / (1.0 - ADAM_B2 ** ADAM_STEP)
    delta = -ADAM_LR * (m_hat / (jnp.sqrt(v_hat) + ADAM_EPS) + ADAM_WD * w)
    return delta, m, v


def _adamw(name, w, halves, m, v, core):
    nl, r, c = w.shape
    hr = r // 2
    tr = 128 if hr % 128 == 0 else hr
    nb = hr // tr

    def body(core_ref, w_ref, own0_ref, oth0_ref, own1_ref, oth1_ref, m_ref, v_ref, g_ref, d_ref, nm_ref, nv_ref):
        first = pl.program_id(0) == 0
        own = jnp.where(first, own0_ref[...], own1_ref[...])
        oth = jnp.where(first, oth0_ref[...], oth1_ref[...])
        g = jnp.where(pl.program_id(1) // nb == core_ref[0], own, oth)
        d, nm, nv = _adamw_math(w_ref[...], g, m_ref[...], v_ref[...])
        g_ref[...] = g
        d_ref[...] = d
        nm_ref[...] = nm
        nv_ref[...] = nv

    spec = pl.BlockSpec((None, tr, c), lambda a, b, core_ref: (a, b, 0))
    gspec = pl.BlockSpec((tr, c), lambda a, b, core_ref: (b % nb, 0))
    shp = jax.ShapeDtypeStruct(w.shape, F32)
    return pl.pallas_call(
        body, name=name,
        grid_spec=pltpu.PrefetchScalarGridSpec(
            num_scalar_prefetch=1, grid=(nl, r // tr), in_specs=[spec] + [gspec] * 4 + [spec, spec],
            out_specs=[spec] * 4),
        out_shape=[shp, shp, shp, shp], compiler_params=_params(2),
    )(core, w, halves[0][0], halves[0][1], halves[1][0], halves[1][1], m, v)


def _assemble_halves(name, shape, halves, core):
    nl, r, c = shape
    hr = r // 2
    tr = 128 if hr % 128 == 0 else hr
    nb = hr // tr

    def body(core_ref, own0_ref, oth0_ref, own1_ref, oth1_ref, g_ref):
        first = pl.program_id(0) == 0
        own = jnp.where(first, own0_ref[...], own1_ref[...])
        oth = jnp.where(first, oth0_ref[...], oth1_ref[...])
        g_ref[...] = jnp.where(pl.program_id(1) // nb == core_ref[0], own, oth)

    gspec = pl.BlockSpec((tr, c), lambda a, b, core_ref: (b % nb, 0))
    return pl.pallas_call(
        body, name=name,
        grid_spec=pltpu.PrefetchScalarGridSpec(
            num_scalar_prefetch=1, grid=(nl, r // tr), in_specs=[gspec] * 4,
            out_specs=pl.BlockSpec((None, tr, c), lambda a, b, core_ref: (a, b, 0))),
        out_shape=jax.ShapeDtypeStruct(shape, F32), compiler_params=_params(2),
    )(core, halves[0][0], halves[0][1], halves[1][0], halves[1][1])


W_IN_FLAT_STEPS = 19


def _to_flat(a):
    nl, r, c = a.shape
    return jnp.transpose(jnp.transpose(a, (2, 0, 1)).reshape(c, nl, r // LANE, LANE), (0, 2, 1, 3)).reshape(-1, LANE)


def _from_flat(f, shape):
    nl, r, c = shape
    return jnp.transpose(jnp.transpose(f.reshape(c, r // LANE, nl, LANE), (0, 2, 1, 3)).reshape(c, nl, r), (1, 2, 0))


def _adamw_flat(name, w, g, m, v):
    n = w.shape[0]
    tr = n // W_IN_FLAT_STEPS

    def body(w_ref, g_ref, m_ref, v_ref, d_ref, nm_ref, nv_ref):
        d, nm, nv = _adamw_math(w_ref[...], g_ref[...], m_ref[...], v_ref[...])
        d_ref[...] = d
        nm_ref[...] = nm
        nv_ref[...] = nv

    spec = pl.BlockSpec((tr, LANE), lambda i: (i, 0))
    shp = jax.ShapeDtypeStruct(w.shape, F32)
    return pl.pallas_call(body, name=name, grid=(W_IN_FLAT_STEPS,), in_specs=[spec] * 4, out_specs=[spec] * 3,
                          out_shape=[shp, shp, shp], compiler_params=_params(1))(w, g, m, v)


def _pair_sum(name, g, x, c, narrow=False):
    n, r, cc = g.shape
    hr = r // 2
    tr = 128 if hr % 128 == 0 else hr
    nb = hr // tr

    def body(c_ref, g_ref, x_ref, o_ref, *narrow_ref):
        total = g_ref[...] + x_ref[...]
        o_ref[...] = total
        if narrow:
            narrow_ref[0][...] = total.astype(BF16)

    spec = pl.BlockSpec((None, tr, cc), lambda i, j, c_ref: (i, j, 0))
    shapes = [jax.ShapeDtypeStruct((n, hr, cc), F32)] + ([jax.ShapeDtypeStruct((n, hr, cc), BF16)] if narrow else [])
    return pl.pallas_call(
        body, name=name,
        grid_spec=pltpu.PrefetchScalarGridSpec(
            num_scalar_prefetch=1, grid=(n, nb),
            in_specs=[pl.BlockSpec((None, tr, cc), lambda i, j, c_ref: (i, c_ref[0] * nb + j, 0)), spec],
            out_specs=[spec] * len(shapes)),
        out_shape=shapes, compiler_params=_params(2),
    )(c, g, x)


def _sum_slots(name, own, landed, chip):
    n, r, c = own.shape
    tr = 128 if r % 128 == 0 else r

    def body(chip_ref, a_ref, b_ref, c_ref, d_ref, o_ref):
        o_ref[...] = ((a_ref[...] + b_ref[...].astype(F32)) + c_ref[...].astype(F32)) + d_ref[...].astype(F32)

    slot = lambda d: pl.BlockSpec((None, tr, c), lambda j, chip_ref: ((chip_ref[0] + d) % n, j, 0))
    return pl.pallas_call(
        body, name=name,
        grid_spec=pltpu.PrefetchScalarGridSpec(
            num_scalar_prefetch=1, grid=(r // tr,), in_specs=[slot(0), slot(1), slot(2), slot(3)],
            out_specs=pl.BlockSpec((tr, c), lambda j, chip_ref: (j, 0))),
        out_shape=jax.ShapeDtypeStruct((r, c), F32), compiler_params=_params(1),
    )(chip, own, landed, landed, landed)


def _me():
    return lax.axis_index("x"), lax.axis_index("y"), lax.axis_index("c")


def _other_chips(x, y):
    return [(1 - x, y), (x, 1 - y), (1 - x, 1 - y)]


def _dma_sems(*counts):
    return [pltpu.SemaphoreType.DMA((n,)) for n in counts]


def _half_rows(rows, which, align):
    return pl.ds(pl.multiple_of(which * (rows // 2), align), rows // 2)


def _gather_first_layer(shards):
    n = len(shards)

    def body(*refs):
        ins, outs, keep, stage = refs[:n], refs[n:2 * n], refs[2 * n:3 * n], refs[3 * n:4 * n]
        ici_send, ici_recv, d2d_send, d2d_recv, local_sems = refs[4 * n:]
        x, y, c = _me()
        k = 2 * x + y
        chips = _other_chips(x, y)
        local, first, passed = [], [], []
        for t in range(n):
            stage[t][...] = ins[t][0].astype(BF16)
            keep[t][...] = ins[t][1].astype(BF16)
            cp = pltpu.make_async_copy(stage[t], outs[t].at[k], local_sems.at[t])
            cp.start()
            local.append(cp)
        for t in range(n):
            mine = _half_rows(shards[t].shape[1], c, 16)
            for j, (px, py) in enumerate(chips):
                cp = pltpu.make_async_remote_copy(
                    src_ref=stage[t].at[mine], dst_ref=outs[t].at[k, mine], send_sem=ici_send.at[3 * t + j],
                    recv_sem=ici_recv.at[3 * t + j], device_id=(px, py, c), device_id_type=MESH)
                cp.start()
                first.append(cp)
        for t in range(n):
            mine = _half_rows(shards[t].shape[1], c, 16)
            for j, (px, py) in enumerate(chips):
                landed = outs[t].at[2 * px + py, mine]
                first[3 * t + j].wait_recv()
                cp = pltpu.make_async_remote_copy(
                    src_ref=landed, dst_ref=landed, send_sem=d2d_send.at[3 * t + j],
                    recv_sem=d2d_recv.at[3 * t + j], device_id=(x, y, 1 - c), device_id_type=MESH)
                cp.start()
                passed.append(cp)
        for cp in passed:
            cp.wait_recv()
        for cp in first + passed:
            cp.wait_send()
        for cp in local:
            cp.wait()

    return pl.pallas_call(
        body, name="gather_first_layer",
        in_specs=[VMEM_SPEC] * n, out_specs=[ANY] * n + [VMEM_SPEC] * n,
        out_shape=[jax.ShapeDtypeStruct((4,) + s.shape[1:], BF16) for s in shards]
        + [jax.ShapeDtypeStruct(s.shape[1:], BF16) for s in shards],
        scratch_shapes=[pltpu.VMEM(s.shape[1:], BF16) for s in shards] + _dma_sems(3 * n, 3 * n, 3 * n, 3 * n, n),
        compiler_params=pltpu.CompilerParams(vmem_limit_bytes=VMEM_LIMIT),
    )(*shards)


def _run_comm(name, comm):
    nci, nco = len(comm.ins), len(comm.out_shapes)

    def body(*refs):
        copies = comm.make(refs[:nci], refs[nci:nci + nco], refs[nci + nco:])
        for cp in copies:
            cp.start()
        for cp in copies:
            cp.wait()

    return pl.pallas_call(body, name=name, in_specs=[ANY] * nci, out_specs=[ANY] * nco,
                          out_shape=list(comm.out_shapes), scratch_shapes=list(comm.sems))(*comm.ins)


def _gather_comm(mine):
    n = len(mine)

    def make(ins, outs, sems):
        send_sems, recv_sems, local_sems = sems
        x, y, c = _me()
        k = 2 * x + y
        copies = []
        for t in range(n):
            copies.append(pltpu.make_async_copy(ins[t], outs[t].at[k], local_sems.at[t]))
            for j, (px, py) in enumerate(_other_chips(x, y)):
                copies.append(pltpu.make_async_remote_copy(
                    src_ref=ins[t], dst_ref=outs[t].at[k], send_sem=send_sems.at[3 * t + j],
                    recv_sem=recv_sems.at[3 * t + j], device_id=(px, py, c), device_id_type=MESH))
        return copies

    return _Comm(mine, [jax.ShapeDtypeStruct((4,) + a.shape, a.dtype) for a in mine], _dma_sems(3 * n, 3 * n, n), make)


def _swap_comm(grads):
    n = len(grads)

    def make(ins, outs, sems):
        send_sems, recv_sems = sems
        x, y, c = _me()
        return [pltpu.make_async_remote_copy(
            src_ref=ins[t].at[:, _half_rows(grads[t].shape[1], 1 - c, 8)], dst_ref=outs[t],
            send_sem=send_sems.at[t], recv_sem=recv_sems.at[t], device_id=(x, y, 1 - c), device_id_type=MESH)
            for t in range(n)]

    shapes = [jax.ShapeDtypeStruct((g.shape[0], g.shape[1] // 2, g.shape[2]), F32) for g in grads]
    return _Comm(grads, shapes, _dma_sems(n, n), make)


def _scatter_comm(parts):
    n = len(parts)

    def make(ins, outs, sems):
        send_sems, recv_sems = sems
        x, y, c = _me()
        k = 2 * x + y
        return [pltpu.make_async_remote_copy(
            src_ref=ins[t].at[2 * px + py], dst_ref=outs[t].at[k], send_sem=send_sems.at[3 * t + j],
            recv_sem=recv_sems.at[3 * t + j], device_id=(px, py, c), device_id_type=MESH)
            for t in range(n) for j, (px, py) in enumerate(_other_chips(x, y))]

    return _Comm(parts, [jax.ShapeDtypeStruct(p.shape, p.dtype) for p in parts], _dma_sems(3 * n, 3 * n), make)


def _share_comm(totals):
    n = len(totals)

    def make(ins, outs, sems):
        send_sems, recv_sems = sems
        x, y, c = _me()
        return [pltpu.make_async_remote_copy(
            src_ref=ins[t], dst_ref=outs[t], send_sem=send_sems.at[t], recv_sem=recv_sems.at[t],
            device_id=(x, y, 1 - c), device_id_type=MESH) for t in range(n)]

    return _Comm(totals, [jax.ShapeDtypeStruct(t.shape, F32) for t in totals], _dma_sems(n, n), make)


def _small_allreduce_adamw(part, w, m, v):
    shape = part.shape

    def body(part_ref, w_ref, m_ref, v_ref, g_ref, d_ref, nm_ref, nv_ref, slots, send_sems, recv_sems):
        x, y, c = _me()
        me = 4 * x + 2 * y + c
        slots[me] = part_ref[...]
        copies = []
        for d in range(1, 8):
            peer = (x ^ (d >> 2), y ^ ((d >> 1) & 1), c ^ (d & 1))
            cp = pltpu.make_async_remote_copy(
                src_ref=part_ref, dst_ref=slots.at[me], send_sem=send_sems.at[d - 1], recv_sem=recv_sems.at[d - 1],
                device_id=peer, device_id_type=MESH)
            cp.start()
            copies.append(cp)
        for cp in copies:
            cp.wait()
        g = slots[0]
        for i in range(1, 8):
            g = g + slots[i]
        g_ref[...] = g
        d, nm, nv = _adamw_math(w_ref[...], g, m_ref[...], v_ref[...])
        d_ref[...] = d
        nm_ref[...] = nm
        nv_ref[...] = nv

    shp = jax.ShapeDtypeStruct(shape, F32)
    return pl.pallas_call(
        body, name="small_allreduce_adamw", in_specs=[VMEM_SPEC] * 4, out_specs=[VMEM_SPEC] * 4,
        out_shape=[shp, shp, shp, shp],
        scratch_shapes=[pltpu.VMEM((8,) + shape, F32), pltpu.SemaphoreType.DMA((7,)), pltpu.SemaphoreType.DMA((7,))],
    )(part, w, m, v)


TM = 512
T_FOX = 1024
T_DIL_FWD = 1024
T_DIL_BWD = 512
TN_DW = 1408


def _layer_fwd(l, h, p_i, wts, consts, comm=None, target=None):
    w_all, _, w_out, _, w_gate, _, w_ple, norm_g, b_f, qkg, ple_g = wts
    bsum, _, bias_t, _, rope = consts
    u, z, qa, ka, va, qb, kb, vb, c_spread, c_t, va_t, vb_t = _inproj(l, h, norm_g, w_all, b_f, qkg, bsum, rope, TM)
    oa, lse_a, *landed = _attn_fwd(f"fox_fwd_l{l}", True, qa, ka, va_t, (c_spread, c_t), T_FOX, comm)
    ob, lse_b = _attn_fwd(f"dil_fwd_l{l}", False, qb, kb, vb_t, (bias_t,), T_DIL_FWD)
    h2, e, gate, *out = _outproj(l, h, oa, ob, z, p_i, w_out, ple_g, w_gate, w_ple, TM, target)
    out = out[0] if target is None else tuple(out)
    saved = (h, u, z, qa, ka, va, qb, kb, vb, c_spread, c_t, oa, lse_a, ob, lse_b, h2, e, gate)
    return out, saved, landed


def _reduce_names(tag):
    return [f"reduce_{tag}_{w}" for w in ("w_in", "w_out", "w_ple", "w_gate")]


def _layer_bwd(l, dout, p_i, wts, consts, saved, pending=None, core=None, chip=None):
    _, w_all_t, _, w_out_t, _, w_gate_t, _, norm_g, b_f, qkg, ple_g = wts
    bsum, hsel, _, bias_t, rope = consts
    h, u, z, qa, ka, va, qb, kb, vb, c_spread, c_t, oa, lse_a, ob, lse_b, h2, e, gate = saved
    fused = pending is not None
    dh2, doa, dob, dga, dgb, delta_t, dw_out, dw_gate, dw_ple, dple_g, *sib = _outproj_bwd(
        l, dout, h2, e, gate, p_i, oa, ob, z, w_out_t, ple_g, w_gate_t, hsel, TM,
        _swap_comm(pending) if fused else None)
    if fused:
        pair = [_pair_sum(n, g, x, core)[0] for n, g, x in zip(_reduce_names(f"pair_l{l + 1}"), pending, sib)]
    dqt_a, dk_a, dv_a, dc, drow, *landed = _attn_bwd(
        f"fox_bwd_l{l}", True, qa, ka, va, doa, lse_a, delta_t, 0, (c_spread, c_t), T_FOX,
        _scatter_comm(pair) if fused else None)
    if fused:
        totals = [_sum_slots(n, a, y, chip) for n, a, y in zip(_reduce_names(f"chips_l{l + 1}"), pair, landed)]
    dqt_b, dk_b, dv_b, *other = _attn_bwd(f"dil_bwd_l{l}", False, qb, kb, vb, dob, lse_b, delta_t, N_PAIRS,
                                            (bias_t,), T_DIL_BWD, _share_comm(totals) if fused else None)
    dz, dqkg, dbf = _inproj_bwd_prep(l, z, dqt_a, dk_a, dv_a, dqt_b, dk_b, dv_b, dga, dgb, dc, drow, b_f, qkg,
                                     bsum, rope, TM)
    dh, dnorm_g = _inproj_bwd_dx(l, dz, w_all_t, h, norm_g, dh2, TM)
    dw_all = _inproj_bwd_dw(l, u, dz, TM, TN_DW)
    reduced = list(zip(totals, other)) if fused else None
    return dh, (dw_all, dw_out, dw_ple, dw_gate, dnorm_g[0], dbf[0, :N_HEADS], dqkg[:4], dple_g[0]), reduced


def _reduce_last(grads, core, chip, l):
    sib = _run_comm(f"reduce_swap_l{l}", _swap_comm(grads))
    pair = [_pair_sum(n, g, x, core, narrow=True) for n, g, x in zip(_reduce_names(f"pair_l{l}"), grads, sib)]
    landed = _run_comm(f"reduce_scatter_l{l}", _scatter_comm([p[1] for p in pair]))
    totals = [_sum_slots(n, p[0], y, chip) for n, p, y in zip(_reduce_names(f"chips_l{l}"), pair, landed)]
    return list(zip(totals, _run_comm(f"reduce_share_l{l}", _share_comm(totals))))


N_FA = 2048


def _layer_weights(l, gathered, norm_g, b_f, qk_norm_g, ple_norm_g):
    g_in, g_out, g_ple, g_gate = gathered
    w_in = jnp.transpose(g_in, (1, 0, 2)).reshape(D_MODEL, N_IN)
    w_all = jnp.concatenate([w_in[:, :N_FA], w_in[:, N_FA + N_HEADS:],
                             jnp.pad(w_in[:, N_FA:N_FA + N_HEADS], ((0, 0), (0, LANE - N_HEADS)))], axis=1)
    w_out = g_out.reshape(D_MODEL, D_MODEL)
    w_gate = g_gate.reshape(D_MODEL, D_MODEL)
    w_ple = jnp.transpose(g_ple, (1, 0, 2)).reshape(PLE_DIM, D_MODEL)
    qkg = jnp.pad(jnp.tile(qk_norm_g[l], (1, N_HEADS)), ((0, 4), (0, 0)))
    bf = jnp.pad(b_f[l], (0, LANE - N_HEADS))[None, :]
    return (w_all, w_all.T, w_out, w_out.T, w_gate, w_gate.T, w_ple, norm_g[l][None, :], bf, qkg,
            ple_norm_g[l][None, :])


def _slot_layout(dw_all, dw_out, dw_ple, dw_gate):
    dw_in = jnp.concatenate([dw_all[:, :N_FA], dw_all[:, N_MAIN:N_MAIN + N_HEADS], dw_all[:, N_FA:N_MAIN]], axis=1)
    return (jnp.transpose(dw_in.reshape(D_MODEL, 4, N_IN // 4), (1, 0, 2)),
            dw_out.reshape(4, D_MODEL // 4, D_MODEL),
            jnp.transpose(dw_ple.reshape(PLE_DIM, 4, D_MODEL // 4), (1, 0, 2)),
            dw_gate.reshape(4, D_MODEL // 4, D_MODEL))


SMALL_ROWS = 40


def _pack_small(norm_g, ple_norm_g, qk_norm_g, b_f, last=0.0):
    flat = jnp.concatenate([norm_g.reshape(-1), ple_norm_g.reshape(-1), qk_norm_g.reshape(-1), b_f.reshape(-1)])
    flat = jnp.pad(flat, (0, SMALL_ROWS * LANE - flat.shape[0] - 1))
    return jnp.concatenate([flat, jnp.reshape(last, (1,)).astype(F32)]).reshape(SMALL_ROWS, LANE)


def _unpack_small(packed):
    flat = packed.reshape(-1)
    n1, n2, n3 = 2 * D_MODEL, 4 * D_MODEL, 4 * D_MODEL + 2 * 4 * HEAD_DIM
    return (flat[:n1].reshape(2, D_MODEL), flat[n1:n2].reshape(2, D_MODEL), flat[n2:n3].reshape(2, 4, HEAD_DIM),
            flat[n3:n3 + 2 * N_HEADS].reshape(2, N_HEADS))


def kernel(x, p, positions, norm_g, w_in, b_f, qk_norm_g, w_out, w_ple, ple_norm_g, w_ple_gate, loss_target,
           m_norm_g, m_w_in, m_b_f, m_qk_norm_g, m_w_out, m_w_ple, m_ple_norm_g, m_w_ple_gate,
           v_norm_g, v_w_in, v_b_f, v_qk_norm_g, v_w_out, v_w_ple, v_ple_norm_g, v_w_ple_gate):
    assert w_in.shape[0] == 2, "the schedule below is written for two layers"
    *first, = _gather_first_layer([w_in, w_out, w_ple, w_ple_gate])
    consts = (_head_block_diag(), _head_select(), _dil_bias(T_DIL_FWD), _dil_bias(T_DIL_BWD),
              _rope_tables(positions[0]))
    small_w = (norm_g, b_f, qk_norm_g, ple_norm_g)
    wts0 = _layer_weights(0, first[:4], *small_w)
    h1, saved0, second = _layer_fwd(0, x[0], p[0, 0], wts0, consts, _gather_comm(first[4:]))
    wts1 = _layer_weights(1, second, *small_w)
    (dh, sq), saved1, _ = _layer_fwd(1, h1, p[1, 0], wts1, consts, target=loss_target[0])

    core = lax.axis_index("c").astype(jnp.int32).reshape(1)
    chip = (2 * lax.axis_index("x") + lax.axis_index("y")).astype(jnp.int32).reshape(1)
    dh, grads1, _ = _layer_bwd(1, dh, p[1, 0], wts1, consts, saved1)
    dh, grads0, reduced1 = _layer_bwd(0, dh, p[0, 0], wts0, consts, saved0, _slot_layout(*grads1[:4]), core, chip)
    reduced0 = _reduce_last(_slot_layout(*grads0[:4]), core, chip, 0)
    grad_x = dh[None]
    small = [grads0[4:], grads1[4:]]
    n_layers = 2

    outs = {}
    for t, (name, w, m, v) in enumerate((("w_in", w_in, m_w_in, v_w_in), ("w_out", w_out, m_w_out, v_w_out),
                                         ("w_ple", w_ple, m_w_ple, v_w_ple),
                                         ("w_ple_gate", w_ple_gate, m_w_ple_gate, v_w_ple_gate))):
        if name == "w_in":
            g = _assemble_halves("assemble_w_in", w.shape, (reduced0[t], reduced1[t]), core)
            g_flat = _to_flat(g)
            flat = _adamw_flat("adamw_w_in", _to_flat(w), g_flat, _to_flat(m), _to_flat(v))
            outs[name] = tuple(_from_flat(f, w.shape) for f in (g_flat,) + tuple(flat))
        else:
            outs[name] = tuple(_adamw(f"adamw_{name}", w, (reduced0[t], reduced1[t]), m, v, core))

    part = _pack_small(jnp.stack([s[0] for s in small]), jnp.stack([s[3] for s in small]),
                       jnp.stack([s[2] for s in small]).reshape(n_layers, 4, N_HEADS, HEAD_DIM).sum(axis=2),
                       jnp.stack([s[1] for s in small]), 0.5 / D_MODEL * jnp.sum(sq))
    packed = _small_allreduce_adamw(part, _pack_small(norm_g, ple_norm_g, qk_norm_g, b_f),
                                    _pack_small(m_norm_g, m_ple_norm_g, m_qk_norm_g, m_b_f),
                                    _pack_small(v_norm_g, v_ple_norm_g, v_qk_norm_g, v_b_f))
    loss = packed[0][SMALL_ROWS - 1, LANE - 1]
    sm = [_unpack_small(a) for a in packed]
    for i, name in enumerate(("norm_g", "ple_norm_g", "qk_norm_g", "b_f")):
        outs[name] = tuple(sm[j][i] for j in range(4))

    order = ("norm_g", "w_in", "b_f", "qk_norm_g", "w_out", "w_ple", "ple_norm_g", "w_ple_gate")
    return (loss, grad_x) + tuple(outs[n][j] for j in range(4) for n in order)
```

```python
import functools
from typing import Any, Callable, NamedTuple, Sequence

import numpy as np
import jax
import jax.numpy as jnp
from jax import lax
from jax.experimental import pallas as pl
from jax.experimental.pallas import tpu as pltpu

F32 = jnp.float32
BF16 = jnp.bfloat16
MESH = pl.DeviceIdType.MESH

D_MODEL = 1024
HEAD_DIM = 64
D_BRANCH = 512
N_HEADS = 8
N_PAIRS = 4
N_IN = 4104
N_MAIN = 4096
N_ALL = 4224
PLE_DIM = 256
ROPE_THETA = 500000.0
ROPE_HALF = 8
EPS = 1e-6
NEG = -1e30
M_INIT = -1e29
Q_SCALE = HEAD_DIM ** -0.5
LOG2E = 1.4426950408889634
LN2 = 0.6931471805599453
DIL_PATTERNS = ((128, 1), (512, 4), (2048, 16))
DIL_BACK = 2048
ADAM_LR, ADAM_B1, ADAM_B2, ADAM_EPS, ADAM_WD, ADAM_STEP = 0.001, 0.9, 0.999, 1e-08, 0.01, 10
VMEM_LIMIT = 56 * 1024 * 1024
LANE = 128


def _dot(a, b):
    return jnp.dot(a, b, preferred_element_type=F32)


def _dot_nt(a, b):
    return lax.dot_general(a, b, (((1,), (1,)), ((), ())), preferred_element_type=F32)


def _dot_tn(a, b):
    return lax.dot_general(a, b, (((0,), (0,)), ((), ())), preferred_element_type=F32)


def _split_dot(x, w):
    hi = x.astype(BF16)
    lo = (x - hi.astype(F32)).astype(BF16)
    return _dot(hi, w) + _dot(lo, w)


def _head_sums(x, bs):
    w = bs.shape[0]
    return jnp.concatenate([_split_dot(x[:, w * k:w * (k + 1)], bs) for k in range(x.shape[1] // w)], axis=1)


def _split3_dot(w, x):
    hi = x.astype(BF16)
    r1 = x - hi.astype(F32)
    mid = r1.astype(BF16)
    lo = (r1 - mid.astype(F32)).astype(BF16)
    return _dot(w, hi) + _dot(w, mid) + _dot(w, lo)


def _sigmoid(x):
    return 1.0 / (1.0 + jnp.exp(-x))


def _params(n_grid):
    return pltpu.CompilerParams(dimension_semantics=("arbitrary",) * n_grid,
                                vmem_limit_bytes=VMEM_LIMIT)


def _full(shape):
    nd = len(shape)
    return pl.BlockSpec(shape, lambda *_: (0,) * nd)


ANY = pl.BlockSpec(memory_space=pl.ANY)
VMEM_SPEC = pl.BlockSpec(memory_space=pltpu.VMEM)


class _Comm(NamedTuple):
    ins: Sequence[Any]
    out_shapes: Sequence[Any]
    sems: Sequence[Any]
    make: Callable[..., Any]


def _fuse_comm(body, n_in, n_out, comm, grid):
    if comm is None:
        return body
    nci, nco, ncs = len(comm.ins), len(comm.out_shapes), len(comm.sems)

    def fused(*refs):
        a, b = n_in + nci, n_in + nci + n_out
        ins, cins, outs, couts = refs[:n_in], refs[n_in:a], refs[a:b], refs[b:b + nco]
        scratch, sems = refs[b + nco:len(refs) - ncs], refs[len(refs) - ncs:]
        first = functools.reduce(jnp.logical_and, [pl.program_id(d) == 0 for d in range(len(grid))])
        last = functools.reduce(jnp.logical_and, [pl.program_id(d) == n - 1 for d, n in enumerate(grid)])

        @pl.when(first)
        def _():
            for cp in comm.make(cins, couts, sems):
                cp.start()

        body(*ins, *outs, *scratch)

        @pl.when(last)
        def _():
            for cp in comm.make(cins, couts, sems):
                cp.wait()

    return fused


def _comm_args(comm):
    if comm is None:
        return [], [], [], [], []
    return (list(comm.ins), [ANY] * len(comm.ins), [ANY] * len(comm.out_shapes), list(comm.out_shapes),
            list(comm.sems))


HEADS_PER_BLOCK = 4


def _head_block_diag():
    i = np.arange(HEADS_PER_BLOCK * HEAD_DIM)
    return jnp.asarray((i[:, None] // HEAD_DIM == i[None, :] // HEAD_DIM).astype(np.float32), BF16)


def _head_select():
    i = np.arange(2 * D_BRANCH)
    j = np.arange(LANE)
    return jnp.asarray((i[:, None] // HEAD_DIM == j[None, :]).astype(np.float32), BF16)


def _dil_bias(t):
    nb = DIL_BACK // t + 1
    qi = np.arange(t)[:, None]
    ki = np.arange(t)[None, :]
    tiles = []
    for r in range(nb):
        d = r * t + qi - ki
        mult = np.zeros((t, t), np.int64)
        for window, dil in DIL_PATTERNS:
            mult += ((d >= 0) & (d <= window) & (d % dil == 0)).astype(np.int64)
        b = np.where(mult > 0, np.log2(np.maximum(mult, 1)), NEG).astype(np.float32)
        tiles.append(b.T)
    return jnp.asarray(np.stack(tiles))


def _rope_tables(positions):
    inv_freq = ROPE_THETA ** (-jnp.arange(ROPE_HALF, dtype=F32) / ROPE_HALF)
    ang = positions.astype(F32)[:, None] * inv_freq
    cos, sin = jnp.cos(ang), jnp.sin(ang)
    s = positions.shape[0]
    rest = HEAD_DIM - 2 * ROPE_HALF
    one, zero, zero8 = jnp.ones((s, rest), F32), jnp.zeros((s, rest), F32), jnp.zeros((s, ROPE_HALF), F32)
    c = jnp.concatenate([cos, cos, one], axis=1)
    s1 = jnp.concatenate([zero8, sin, zero], axis=1)
    s2 = jnp.concatenate([-sin, zero8, zero], axis=1)
    return tuple(jnp.tile(t, (1, 2)) for t in (c, s1, s2))


def _rope_fwd(x, c, s1, s2):
    return x * c + pltpu.roll(x, ROPE_HALF, 1) * s1 + pltpu.roll(x, LANE - ROPE_HALF, 1) * s2


def _rope_bwd(dy, c, s1, s2):
    return dy * c + pltpu.roll(dy * s1, LANE - ROPE_HALF, 1) + pltpu.roll(dy * s2, ROPE_HALF, 1)


def _log_sigmoid(x):
    return jnp.minimum(x, 0.0) - jnp.log(1.0 + jnp.exp(-jnp.abs(x)))


def _inproj(l, h, norm_g, w_all, b_f, qkg, bsum, rope, tm):
    s = h.shape[0]
    rc, rs1, rs2 = rope

    def body(h_ref, g_ref, w_ref, bf_ref, qkg_ref, bsum_ref, rc_ref, rs1_ref, rs2_ref,
             u_ref, z_ref, qa_ref, ka_ref, va_ref, qb_ref, kb_ref, vb_ref, cs_ref, ct_ref, vat_ref, vbt_ref, carry):
        @pl.when(pl.program_id(0) == 0)
        def _():
            carry[...] = jnp.zeros_like(carry)

        hh = h_ref[...]
        r = lax.rsqrt(jnp.mean(hh * hh, axis=-1, keepdims=True) + EPS)
        u = (hh * r * g_ref[...]).astype(BF16)
        u_ref[...] = u
        for k in range(N_ALL // LANE // 3):
            cols = slice(3 * LANE * k, 3 * LANE * (k + 1))
            z_ref[:, cols] = _dot(u, w_ref[:, cols])
        bs = bsum_ref[...]

        def head_norm(x, row):
            ms = _head_sums(x * x, bs) * (1.0 / HEAD_DIM)
            return x * lax.rsqrt(ms + EPS) * qkg_ref[row:row + 1, :]

        def seg(k):
            return z_ref[:, D_BRANCH * k:D_BRANCH * (k + 1)]

        qa_ref[...] = (head_norm(seg(0), 0) * (Q_SCALE * LOG2E)).astype(BF16)
        ka_ref[...] = head_norm(seg(1), 1).astype(BF16)
        va_ref[...] = seg(2).astype(BF16)
        vat_ref[...] = seg(2).T.astype(BF16)
        qn = head_norm(seg(4), 2) * (Q_SCALE * LOG2E)
        kn = head_norm(seg(5), 3)
        c, s1, s2 = rc_ref[...], rs1_ref[...], rs2_ref[...]
        for k in range(D_BRANCH // LANE):
            cols = slice(LANE * k, LANE * (k + 1))
            qb_ref[:, cols] = _rope_fwd(qn[:, cols], c, s1, s2).astype(BF16)
            kb_ref[:, cols] = _rope_fwd(kn[:, cols], c, s1, s2).astype(BF16)
        vb_ref[...] = seg(6).astype(BF16)
        vbt_ref[...] = seg(6).T.astype(BF16)
        logf = _log_sigmoid(z_ref[:, N_MAIN:N_ALL] + bf_ref[...])
        csum = _split3_dot(_tri(tm, False), logf) + carry[0:1, :]
        carry[...] = jnp.broadcast_to(csum[tm - 1:tm, :], carry.shape)
        csum = csum * LOG2E
        ct = csum.T
        for p in range(N_PAIRS):
            cs_ref[:, LANE * p:LANE * (p + 1)] = csum if p == 0 else pltpu.roll(csum, LANE - 2 * p, 1)
            ct_ref[p, :, :] = ct[2 * p:2 * p + 2, :]

    row = lambda w: pl.BlockSpec((tm, w), lambda i: (i, 0))
    colt = pl.BlockSpec((D_BRANCH, tm), lambda i: (0, i))
    bf = lambda: jax.ShapeDtypeStruct((s, D_BRANCH), BF16)
    bft = lambda: jax.ShapeDtypeStruct((D_BRANCH, s), BF16)
    return pl.pallas_call(
        body, name=f"inproj_l{l}", grid=(s // tm,),
        in_specs=[row(D_MODEL), _full((1, D_MODEL)), _full((D_MODEL, N_ALL)), _full((1, LANE)),
                  _full((8, D_BRANCH)), _full((HEADS_PER_BLOCK * HEAD_DIM,) * 2), row(LANE), row(LANE), row(LANE)],
        out_specs=[row(D_MODEL), row(N_ALL)] + [row(D_BRANCH)] * 6
        + [row(N_PAIRS * LANE), pl.BlockSpec((N_PAIRS, 2, tm), lambda i: (0, 0, i)), colt, colt],
        out_shape=[jax.ShapeDtypeStruct((s, D_MODEL), BF16), jax.ShapeDtypeStruct((s, N_ALL), F32),
                   bf(), bf(), bf(), bf(), bf(), bf(), jax.ShapeDtypeStruct((s, N_PAIRS * LANE), F32),
                   jax.ShapeDtypeStruct((N_PAIRS, 2, s), F32), bft(), bft()],
        scratch_shapes=[pltpu.VMEM((8, LANE), F32)],
        compiler_params=_params(1),
    )(h, norm_g, w_all, b_f, qkg, bsum, rc, rs1, rs2)


def _tri(t, upper):
    a = lax.broadcasted_iota(jnp.int32, (t, t), 0)
    b = lax.broadcasted_iota(jnp.int32, (t, t), 1)
    return jnp.where((b >= a) if upper else (b <= a), 1.0, 0.0).astype(BF16)


def _attn_fwd(name, fox, q, k, vt, extra, t, comm=None):
    s = q.shape[0]
    nq = s // t
    nb = DIL_BACK // t + 1

    def body(*refs):
        if fox:
            q_ref, k_ref, vt_ref, ccol_ref, crow_ref, o_ref, lse_ref, m_scr, l_scr, acc_scr = refs
        else:
            q_ref, k_ref, vt_ref, bias_ref, o_ref, lse_ref, m_scr, l_scr, acc_scr = refs
        i = pl.program_id(1)
        lane = lax.broadcasted_iota(jnp.int32, (t, LANE), 1)
        first = lane < HEAD_DIM
        qq = q_ref[...]
        zero = jnp.zeros_like(qq)
        qh = (jnp.where(first, qq, zero), jnp.where(first, zero, qq))
        m_scr[...] = jnp.full(m_scr.shape, M_INIT, F32)
        l_scr[...] = jnp.zeros_like(l_scr)
        acc_scr[...] = jnp.zeros_like(acc_scr)
        ones = jnp.ones((16, t), BF16)

        half = t // 2
        whole, lo, hi = slice(0, t), slice(0, half), slice(half, t)

        def block(j, ksl, qsl, causal):
            nk_, nq_ = ksl.stop - ksl.start, qsl.stop - qsl.start
            rows = pl.ds(pl.multiple_of(j * t + ksl.start, LANE), nk_)
            ks = k_ref[rows, :]
            vts = jnp.concatenate([vt_ref[:, rows], ones[:, :nk_]], axis=0)
            if fox:
                ccol = ccol_ref[rows, :]
            for h in range(2):
                st = _dot_nt(ks, qh[h][qsl, :])
                if fox:
                    st = st + (crow_ref[h:h + 1, qsl] - ccol[:, h:h + 1])
                    if causal:
                        ki = lax.broadcasted_iota(jnp.int32, (nk_, nq_), 0) + ksl.start
                        qi = lax.broadcasted_iota(jnp.int32, (nk_, nq_), 1) + qsl.start
                        st = jnp.where(ki <= qi, st, NEG)
                else:
                    st = st + bias_ref[i - j, ksl, qsl]
                m_old = m_scr[h, :, qsl]
                m_new = jnp.maximum(m_old, jnp.max(st, axis=0, keepdims=True))
                alpha = jnp.exp2(m_old - m_new)
                pb = jnp.exp2(st - m_new).astype(BF16)
                pv = _dot(vts, pb)
                l_scr[h, :, qsl] = alpha * l_scr[h, :, qsl] + pv[LANE:LANE + 1, :]
                acc_scr[h, :, qsl] = alpha * acc_scr[h, :, qsl] + pv[:LANE, :]
                m_scr[h, :, qsl] = m_new

        def full(j, c):
            block(j, whole, whole, False)
            return c

        if fox:
            lax.fori_loop(0, i, full, 0)
        else:
            @pl.when(i >= nb - 1)
            def _():
                block(i - (nb - 1), lo, lo, False)
                block(i - (nb - 1), hi, whole, False)

            lax.fori_loop(jnp.maximum(i - (nb - 2), 0), i, full, 0)
        if fox:
            block(i, whole, whole, True)
        else:
            block(i, lo, lo, False)
            block(i, whole, hi, False)

        sub = lax.broadcasted_iota(jnp.int32, (LANE, t), 0)
        ot = jnp.where(sub < HEAD_DIM, acc_scr[0] / l_scr[0], acc_scr[1] / l_scr[1])
        o_ref[...] = ot.T
        for h in range(2):
            lse_ref[h:h + 1, :] = m_scr[h] + jnp.log2(l_scr[h])

    qspec = pl.BlockSpec((t, LANE), lambda hp, i: (i, hp))
    kspec = pl.BlockSpec((s, LANE), lambda hp, i: (0, hp))
    vtspec = pl.BlockSpec((LANE, s), lambda hp, i: (hp, 0))
    in_specs = [qspec, kspec, vtspec]
    if fox:
        in_specs += [kspec, pl.BlockSpec((None, 2, t), lambda hp, i: (hp, 0, i))]
    else:
        in_specs += [_full((nb, t, t))]
    grid = (N_PAIRS, nq)
    c_in, c_ispec, c_ospec, c_oshape, c_scr = _comm_args(comm)
    return pl.pallas_call(
        _fuse_comm(body, len(in_specs), 2, comm, grid), name=name, grid=grid,
        in_specs=in_specs + c_ispec,
        out_specs=[qspec, pl.BlockSpec((None, 2, t), lambda hp, i: (hp, 0, i))] + c_ospec,
        out_shape=[jax.ShapeDtypeStruct((s, D_BRANCH), F32), jax.ShapeDtypeStruct((N_PAIRS, 2, s), F32)] + c_oshape,
        scratch_shapes=[pltpu.VMEM((2, 1, t), F32), pltpu.VMEM((2, 1, t), F32), pltpu.VMEM((2, LANE, t), F32)]
        + c_scr,
        compiler_params=_params(2),
    )(q, k, vt, *extra, *c_in)


def _attn_bwd(name, fox, q, k, v, do, lse_t, delta_t, pair_offset, extra, t, comm=None):
    s = q.shape[0]
    nk = s // t
    nb = DIL_BACK // t + 1

    def body(*refs):
        if fox:
            (q_ref, k_ref, v_ref, do_ref, lse_ref, delta_ref, ccol_ref, crow_ref,
             dqt_ref, dk_ref, dv_ref, dc_ref, drow_ref) = refs
        else:
            q_ref, k_ref, v_ref, do_ref, lse_ref, delta_ref, bias_ref, dqt_ref, dk_ref, dv_ref = refs
        j = pl.program_id(1)

        @pl.when(j == 0)
        def _():
            dqt_ref[...] = jnp.zeros_like(dqt_ref)
            if fox:
                drow_ref[...] = jnp.zeros_like(drow_ref)

        lane = lax.broadcasted_iota(jnp.int32, (t, LANE), 1)
        first = lane < HEAD_DIM
        ks = k_ref[...]
        vs = v_ref[...]
        kt = ks.astype(F32).T
        sub = lax.broadcasted_iota(jnp.int32, (LANE, t), 0)
        kth = (jnp.where(sub < HEAD_DIM, kt, 0.0).astype(BF16), jnp.where(sub < HEAD_DIM, 0.0, kt).astype(BF16))
        dk_ref[...] = jnp.zeros_like(dk_ref)
        dv_ref[...] = jnp.zeros_like(dv_ref)
        if fox:
            dc_ref[...] = jnp.zeros_like(dc_ref)
            ccol = ccol_ref[...]

        half = t // 2
        whole, lo, hi = slice(0, t), slice(0, half), slice(half, t)

        def block(i, ksl, qsl, causal):
            nk_, nq_ = ksl.stop - ksl.start, qsl.stop - qsl.start
            rows = pl.ds(pl.multiple_of(i * t + qsl.start, LANE), nq_)
            qq = q_ref[rows, :]
            dd = do_ref[rows, :]
            zero = jnp.zeros_like(qq)
            qh = (jnp.where(first[:nq_], qq, zero), jnp.where(first[:nq_], zero, qq))
            dh = (jnp.where(first[:nq_], dd, zero), jnp.where(first[:nq_], zero, dd))
            for h in range(2):
                st = _dot_nt(ks[ksl, :], qh[h])
                if fox:
                    st = st + (crow_ref[h:h + 1, rows] - ccol[ksl, h:h + 1])
                    if causal:
                        ki = lax.broadcasted_iota(jnp.int32, (nk_, nq_), 0) + ksl.start
                        qi = lax.broadcasted_iota(jnp.int32, (nk_, nq_), 1) + qsl.start
                        st = jnp.where(ki <= qi, st, NEG)
                else:
                    st = st + bias_ref[i - j, ksl, qsl]
                pt = jnp.exp2(st - lse_ref[h:h + 1, rows])
                dpt = _dot_nt(vs[ksl, :], dh[h])
                dst = pt * (dpt - delta_ref[h:h + 1, rows])
                dv_ref[ksl, :] += _dot(pt.astype(BF16), dh[h])
                dsb = dst.astype(BF16)
                dk_ref[ksl, :] += _dot(dsb, qh[h])
                dqt_ref[:, rows] += _dot(kth[h][:, ksl], dsb)
                if fox:
                    dc_ref[ksl, :] -= jnp.where(lane[:nk_] == h, jnp.sum(dst, axis=1, keepdims=True), 0.0)
                    drow_ref[h:h + 1, rows] += jnp.sum(dst, axis=0, keepdims=True)

        def full(i, c):
            block(i, whole, whole, False)
            return c

        block(j, lo, whole, True)
        block(j, hi, hi, True)
        if fox:
            lax.fori_loop(j + 1, nk, full, 0)
        else:
            lax.fori_loop(j + 1, jnp.minimum(j + nb - 1, nk), full, 0)

            @pl.when(j + nb - 1 < nk)
            def _():
                block(j + nb - 1, lo, lo, False)
                block(j + nb - 1, hi, whole, False)

    kspec = pl.BlockSpec((t, LANE), lambda hp, j: (j, hp))
    qspec = pl.BlockSpec((s, LANE), lambda hp, j: (0, hp))
    rowspec = pl.BlockSpec((None, 2, s), lambda hp, j: (hp, 0, 0))
    drowspec = pl.BlockSpec((None, 2, s), lambda hp, j: (hp + pair_offset, 0, 0))
    in_specs = [qspec, kspec, kspec, qspec, rowspec, drowspec]
    out_specs = [pl.BlockSpec((LANE, s), lambda hp, j: (hp, 0)), kspec, kspec]
    out_shape = [jax.ShapeDtypeStruct((D_BRANCH, s), F32), jax.ShapeDtypeStruct((s, D_BRANCH), F32),
                 jax.ShapeDtypeStruct((s, D_BRANCH), F32)]
    if fox:
        in_specs += [kspec, rowspec]
        out_specs += [kspec, rowspec]
        out_shape += [jax.ShapeDtypeStruct((s, N_PAIRS * LANE), F32), jax.ShapeDtypeStruct((N_PAIRS, 2, s), F32)]
    else:
        in_specs += [_full((nb, t, t))]
    grid = (N_PAIRS, nk)
    c_in, c_ispec, c_ospec, c_oshape, c_scr = _comm_args(comm)
    return pl.pallas_call(
        _fuse_comm(body, len(in_specs), len(out_specs), comm, grid), name=name, grid=grid,
        in_specs=in_specs + c_ispec, out_specs=out_specs + c_ospec, out_shape=out_shape + c_oshape,
        scratch_shapes=c_scr, compiler_params=_params(2),
    )(q, k, v, do, lse_t, delta_t, *extra, *c_in)


def _silu(x):
    return x * _sigmoid(x)


def _outproj(l, h, oa, ob, z, p_i, w_out, ple_g, w_gate, w_ple, tm, target=None):
    s = h.shape[0]
    last = target is not None

    def body(h_ref, oa_ref, ob_ref, ga_ref, gb_ref, p_ref, wo_ref, pg_ref, wg_ref, wp_ref, *rest):
        if last:
            t_ref, h2_ref, e_ref, gate_ref, out_ref, acc_ref = rest
        else:
            h2_ref, e_ref, gate_ref, out_ref = rest
        a = jnp.concatenate([oa_ref[...] * _silu(ga_ref[...]), ob_ref[...] * _silu(gb_ref[...])], axis=1)
        h2 = h_ref[...] + _dot(a.astype(BF16), wo_ref[...])
        h2_ref[...] = h2
        r = lax.rsqrt(jnp.mean(h2 * h2, axis=-1, keepdims=True) + EPS)
        n2 = (h2 * r * pg_ref[...]).astype(BF16)
        gate = _sigmoid(_dot(n2, wg_ref[...]))
        e = _dot(p_ref[...].astype(BF16), wp_ref[...])
        e_ref[...] = e
        gate_ref[...] = gate
        out = h2 + e * gate
        if not last:
            out_ref[...] = out
            return

        @pl.when(pl.program_id(0) == 0)
        def _():
            acc_ref[...] = jnp.zeros_like(acc_ref)

        err = out - t_ref[...]
        out_ref[...] = err * (1.0 / D_MODEL)
        e2 = err * err
        rows = e2[0:8, :]
        for k in range(1, tm // 8):
            rows = rows + e2[8 * k:8 * (k + 1), :]
        part = rows[:, 0:LANE]
        for k in range(1, D_MODEL // LANE):
            part = part + rows[:, LANE * k:LANE * (k + 1)]
        acc_ref[...] += part

    row = lambda w: pl.BlockSpec((tm, w), lambda i: (i, 0))
    zcol = lambda k: pl.BlockSpec((tm, D_BRANCH), lambda i: (i, k))
    f = lambda: jax.ShapeDtypeStruct((s, D_MODEL), F32)
    return pl.pallas_call(
        body, name=f"outproj_l{l}", grid=(s // tm,),
        in_specs=[row(D_MODEL), row(D_BRANCH), row(D_BRANCH), zcol(3), zcol(7), row(PLE_DIM),
                  _full((D_MODEL, D_MODEL)), _full((1, D_MODEL)), _full((D_MODEL, D_MODEL)),
                  _full((PLE_DIM, D_MODEL))] + ([row(D_MODEL)] if last else []),
        out_specs=[row(D_MODEL)] * 4 + ([_full((8, LANE))] if last else []),
        out_shape=[f(), f(), f(), f()] + ([jax.ShapeDtypeStruct((8, LANE), F32)] if last else []),
        compiler_params=_params(1),
    )(h, oa, ob, z, z, p_i, w_out, ple_g, w_gate, w_ple, *([target] if last else []))


def _outproj_bwd(l, dout, h2, e, gate, p_i, oa, ob, z, w_out_t, ple_g, w_gate_t, hsel, tm, comm=None):
    s = dout.shape[0]

    def body(do_ref, h2_ref, e_ref, gate_ref, p_ref, oa_ref, ob_ref, ga_ref, gb_ref, wot_ref, pg_ref,
             wgt_ref, hsel_ref,
             dh2_ref, doa_ref, dob_ref, dga_ref, dgb_ref, delta_ref, dwo_ref, dwg_ref, dwp_ref, dpg_ref):
        @pl.when(pl.program_id(0) == 0)
        def _():
            dwo_ref[...] = jnp.zeros_like(dwo_ref)
            dwg_ref[...] = jnp.zeros_like(dwg_ref)
            dwp_ref[...] = jnp.zeros_like(dwp_ref)
            dpg_ref[...] = jnp.zeros_like(dpg_ref)

        dho = do_ref[...]
        g = gate_ref[...]
        de = (dho * g).astype(BF16)
        dwp_ref[...] += _dot_tn(p_ref[...].astype(BF16), de)
        dpre = (dho * e_ref[...] * g * (1.0 - g)).astype(BF16)
        h2 = h2_ref[...]
        pg = pg_ref[...]
        r = lax.rsqrt(jnp.mean(h2 * h2, axis=-1, keepdims=True) + EPS)
        n2 = (h2 * r * pg).astype(BF16)
        dwg_ref[...] += _dot_tn(n2, dpre)
        dn2 = _dot(dpre, wgt_ref[...])
        dpg_ref[0:1, :] += jnp.sum(dn2 * h2 * r, axis=0, keepdims=True)
        wv = dn2 * pg
        dh2 = dho + r * wv - h2 * (r * r * r) * jnp.mean(wv * h2, axis=-1, keepdims=True)
        dh2_ref[...] = dh2
        dh2b = dh2.astype(BF16)
        ga, gb, oa, ob = ga_ref[...], gb_ref[...], oa_ref[...], ob_ref[...]
        sga, sgb = _sigmoid(ga), _sigmoid(gb)
        a = jnp.concatenate([oa * ga * sga, ob * gb * sgb], axis=1).astype(BF16)
        dwo_ref[...] += _dot_tn(a, dh2b)
        da = _dot(dh2b, wot_ref[...])
        da_a, da_b = da[:, :D_BRANCH], da[:, D_BRANCH:]
        doa = da_a * ga * sga
        dob = da_b * gb * sgb
        doa_ref[...] = doa.astype(BF16)
        dob_ref[...] = dob.astype(BF16)
        dga_ref[...] = (da_a * oa * sga * (1.0 + ga * (1.0 - sga))).astype(BF16)
        dgb_ref[...] = (da_b * ob * sgb * (1.0 + gb * (1.0 - sgb))).astype(BF16)
        prod = jnp.concatenate([doa * oa, dob * ob], axis=1)
        dt = _split_dot(prod, hsel_ref[...]).T
        for pp in range(2 * N_PAIRS):
            delta_ref[pp, :, :] = dt[2 * pp:2 * pp + 2, :]

    row = lambda w: pl.BlockSpec((tm, w), lambda i: (i, 0))
    zcol = lambda k: pl.BlockSpec((tm, D_BRANCH), lambda i: (i, k))
    grid = (s // tm,)
    c_in, c_ispec, c_ospec, c_oshape, c_scr = _comm_args(comm)
    return pl.pallas_call(
        _fuse_comm(body, 13, 10, comm, grid), name=f"outproj_bwd_l{l}", grid=grid,
        in_specs=[row(D_MODEL)] * 4 + [row(PLE_DIM), row(D_BRANCH), row(D_BRANCH), zcol(3), zcol(7),
                                        _full((D_MODEL, D_MODEL)), _full((1, D_MODEL)), _full((D_MODEL, D_MODEL)),
                                        _full((2 * D_BRANCH, LANE))] + c_ispec,
        out_specs=[row(D_MODEL)] + [row(D_BRANCH)] * 4
        + [pl.BlockSpec((2 * N_PAIRS, 2, tm), lambda i: (0, 0, i)), _full((D_MODEL, D_MODEL)),
           _full((D_MODEL, D_MODEL)), _full((PLE_DIM, D_MODEL)), _full((8, D_MODEL))] + c_ospec,
        out_shape=[jax.ShapeDtypeStruct((s, D_MODEL), F32)] + [jax.ShapeDtypeStruct((s, D_BRANCH), BF16)] * 4
        + [jax.ShapeDtypeStruct((2 * N_PAIRS, 2, s), F32), jax.ShapeDtypeStruct((D_MODEL, D_MODEL), F32),
           jax.ShapeDtypeStruct((D_MODEL, D_MODEL), F32), jax.ShapeDtypeStruct((PLE_DIM, D_MODEL), F32),
           jax.ShapeDtypeStruct((8, D_MODEL), F32)] + c_oshape,
        scratch_shapes=c_scr, compiler_params=_params(1),
    )(dout, h2, e, gate, p_i, oa, ob, z, z, w_out_t, ple_g, w_gate_t, hsel, *c_in)


def _inproj_bwd_prep(l, z, dqt_a, dk_a, dv_a, dqt_b, dk_b, dv_b, dga, dgb, dc_spread, dc_rows, b_f, qkg, bsum,
                     rope, tm):
    s = z.shape[0]
    n = s // tm
    rc, rs1, rs2 = rope

    def body(z_ref, dqta_ref, dka_ref, dva_ref, dqtb_ref, dkb_ref, dvb_ref, dga_ref, dgb_ref, dc_ref, drow_ref,
             bf_ref, qkg_ref, bsum_ref, rc_ref, rs1_ref, rs2_ref, dz_ref, dqkg_ref, dbf_ref, carry):
        @pl.when(pl.program_id(0) == 0)
        def _():
            dqkg_ref[...] = jnp.zeros_like(dqkg_ref)
            dbf_ref[...] = jnp.zeros_like(dbf_ref)
            carry[...] = jnp.zeros_like(carry)

        lane = lax.broadcasted_iota(jnp.int32, (tm, LANE), 1)
        dc = jnp.concatenate([drow_ref[p] for p in range(N_PAIRS)] + [jnp.zeros((LANE - N_HEADS, tm), F32)], axis=0).T
        for p in range(N_PAIRS):
            part = jnp.where(lane < 2, dc_ref[:, LANE * p:LANE * (p + 1)], 0.0)
            dc = dc + (part if p == 0 else pltpu.roll(part, 2 * p, 1))
        dlogf = _split3_dot(_tri(tm, True), dc) + carry[0:1, :]
        carry[...] = jnp.broadcast_to(dlogf[0:1, :], carry.shape)

        bs = bsum_ref[...]
        c, s1, s2 = rc_ref[...], rs1_ref[...], rs2_ref[...]

        def unrope(dy):
            return jnp.concatenate([_rope_bwd(dy[:, LANE * k:LANE * (k + 1)], c, s1, s2)
                                    for k in range(D_BRANCH // LANE)], axis=1)

        def norm_bwd(k, row, dy):
            x = z_ref[:, D_BRANCH * k:D_BRANCH * (k + 1)]
            r = lax.rsqrt(_head_sums(x * x, bs) * (1.0 / HEAD_DIM) + EPS)
            dqkg_ref[row:row + 1, :] += jnp.sum(dy * x * r, axis=0, keepdims=True)
            w = dy * qkg_ref[row:row + 1, :]
            dx = r * w - x * (r * r * r) * (_head_sums(w * x, bs) * (1.0 / HEAD_DIM))
            dz_ref[:, D_BRANCH * k:D_BRANCH * (k + 1)] = dx.astype(BF16)

        norm_bwd(0, 0, dqta_ref[...].T * Q_SCALE)
        norm_bwd(1, 1, dka_ref[...] * LN2)
        dz_ref[:, 2 * D_BRANCH:3 * D_BRANCH] = dva_ref[...].astype(BF16)
        dz_ref[:, 3 * D_BRANCH:4 * D_BRANCH] = dga_ref[...]
        norm_bwd(4, 2, unrope(dqtb_ref[...].T * Q_SCALE))
        norm_bwd(5, 3, unrope(dkb_ref[...] * LN2))
        dz_ref[:, 6 * D_BRANCH:7 * D_BRANCH] = dvb_ref[...].astype(BF16)
        dz_ref[:, 7 * D_BRANCH:8 * D_BRANCH] = dgb_ref[...]
        dfa = dlogf * _sigmoid(-(z_ref[:, N_MAIN:N_ALL] + bf_ref[...]))
        dz_ref[:, N_MAIN:N_ALL] = dfa.astype(BF16)
        dbf_ref[0:1, :] += jnp.sum(dfa, axis=0, keepdims=True)

    row = lambda w: pl.BlockSpec((tm, w), lambda i: (n - 1 - i, 0))
    colt = pl.BlockSpec((D_BRANCH, tm), lambda i: (0, n - 1 - i))
    return pl.pallas_call(
        body, name=f"inproj_bwd_prep_l{l}", grid=(n,),
        in_specs=[row(N_ALL), colt, row(D_BRANCH), row(D_BRANCH), colt, row(D_BRANCH), row(D_BRANCH),
                  row(D_BRANCH), row(D_BRANCH), row(N_PAIRS * LANE),
                  pl.BlockSpec((N_PAIRS, 2, tm), lambda i: (0, 0, n - 1 - i)), _full((1, LANE)),
                  _full((8, D_BRANCH)), _full((HEADS_PER_BLOCK * HEAD_DIM,) * 2), row(LANE), row(LANE), row(LANE)],
        out_specs=[row(N_ALL), _full((8, D_BRANCH)), _full((8, LANE))],
        out_shape=[jax.ShapeDtypeStruct((s, N_ALL), BF16), jax.ShapeDtypeStruct((8, D_BRANCH), F32),
                   jax.ShapeDtypeStruct((8, LANE), F32)],
        scratch_shapes=[pltpu.VMEM((8, LANE), F32)],
        compiler_params=_params(1),
    )(z, dqt_a, dk_a, dv_a, dqt_b, dk_b, dv_b, dga, dgb, dc_spread, dc_rows, b_f, qkg, bsum, rc, rs1, rs2)


def _inproj_bwd_dx(l, dz, w_all_t, h, norm_g, dh2, tm):
    s = dz.shape[0]

    def body(dz_ref, wt_ref, h_ref, g_ref, dh2_ref, dh_ref, dg_ref):
        @pl.when(pl.program_id(0) == 0)
        def _():
            dg_ref[...] = jnp.zeros_like(dg_ref)

        du = _dot(dz_ref[...], wt_ref[...])
        hh = h_ref[...]
        g = g_ref[...]
        r = lax.rsqrt(jnp.mean(hh * hh, axis=-1, keepdims=True) + EPS)
        dg_ref[0:1, :] += jnp.sum(du * hh * r, axis=0, keepdims=True)
        wv = du * g
        dh_ref[...] = dh2_ref[...] + r * wv - hh * (r * r * r) * jnp.mean(wv * hh, axis=-1, keepdims=True)

    row = lambda w: pl.BlockSpec((tm, w), lambda i: (i, 0))
    return pl.pallas_call(
        body, name=f"inproj_bwd_dx_l{l}", grid=(s // tm,),
        in_specs=[row(N_ALL), _full((N_ALL, D_MODEL)), row(D_MODEL), _full((1, D_MODEL)), row(D_MODEL)],
        out_specs=[row(D_MODEL), _full((8, D_MODEL))],
        out_shape=[jax.ShapeDtypeStruct((s, D_MODEL), F32), jax.ShapeDtypeStruct((8, D_MODEL), F32)],
        compiler_params=_params(1),
    )(dz, w_all_t, h, norm_g, dh2)


def _inproj_bwd_dw(l, u, dz, tm, tn):
    s = u.shape[0]

    def body(u_ref, dz_ref, dw_ref):
        @pl.when(pl.program_id(1) == 0)
        def _():
            dw_ref[...] = jnp.zeros_like(dw_ref)

        dw_ref[...] += _dot_tn(u_ref[...], dz_ref[...])

    return pl.pallas_call(
        body, name=f"inproj_bwd_dw_l{l}", grid=(N_ALL // tn, s // tm),
        in_specs=[pl.BlockSpec((tm, D_MODEL), lambda n, i: (i, 0)), pl.BlockSpec((tm, tn), lambda n, i: (i, n))],
        out_specs=pl.BlockSpec((D_MODEL, tn), lambda n, i: (0, n)),
        out_shape=jax.ShapeDtypeStruct((D_MODEL, N_ALL), F32),
        compiler_params=_params(2),
    )(u, dz)


def _adamw_math(w, g, m, v):
    m = ADAM_B1 * m + (1.0 - ADAM_B1) * g
    v = ADAM_B2 * v + (1.0 - ADAM_B2) * (g * g)
    m_hat = m / (1.0 - ADAM_B1 ** ADAM_STEP)
    v_hat = v / (1.0 - ADAM_B2 ** ADAM_STEP)
    delta = -ADAM_LR * (m_hat / (jnp.sqrt(v_hat) + ADAM_EPS) + ADAM_WD * w)
    return delta, m, v


def _adamw(name, w, halves, m, v, core):
    nl, r, c = w.shape
    hr = r // 2
    tr = 128 if hr % 128 == 0 else hr
    nb = hr // tr

    def body(core_ref, w_ref, own0_ref, oth0_ref, own1_ref, oth1_ref, m_ref, v_ref, g_ref, d_ref, nm_ref, nv_ref):
        first = pl.program_id(0) == 0
        own = jnp.where(first, own0_ref[...], own1_ref[...])
        oth = jnp.where(first, oth0_ref[...], oth1_ref[...])
        g = jnp.where(pl.program_id(1) // nb == core_ref[0], own, oth)
        d, nm, nv = _adamw_math(w_ref[...], g, m_ref[...], v_ref[...])
        g_ref[...] = g
        d_ref[...] = d
        nm_ref[...] = nm
        nv_ref[...] = nv

    spec = pl.BlockSpec((None, tr, c), lambda a, b, core_ref: (a, b, 0))
    gspec = pl.BlockSpec((tr, c), lambda a, b, core_ref: (b % nb, 0))
    shp = jax.ShapeDtypeStruct(w.shape, F32)
    return pl.pallas_call(
        body, name=name,
        grid_spec=pltpu.PrefetchScalarGridSpec(
            num_scalar_prefetch=1, grid=(nl, r // tr), in_specs=[spec] + [gspec] * 4 + [spec, spec],
            out_specs=[spec] * 4),
        out_shape=[shp, shp, shp, shp], compiler_params=_params(2),
    )(core, w, halves[0][0], halves[0][1], halves[1][0], halves[1][1], m, v)


def _assemble_halves(name, shape, halves, core):
    nl, r, c = shape
    hr = r // 2
    tr = 128 if hr % 128 == 0 else hr
    nb = hr // tr

    def body(core_ref, own0_ref, oth0_ref, own1_ref, oth1_ref, g_ref):
        first = pl.program_id(0) == 0
        own = jnp.where(first, own0_ref[...], own1_ref[...])
        oth = jnp.where(first, oth0_ref[...], oth1_ref[...])
        g_ref[...] = jnp.where(pl.program_id(1) // nb == core_ref[0], own, oth)

    gspec = pl.BlockSpec((tr, c), lambda a, b, core_ref: (b % nb, 0))
    return pl.pallas_call(
        body, name=name,
        grid_spec=pltpu.PrefetchScalarGridSpec(
            num_scalar_prefetch=1, grid=(nl, r // tr), in_specs=[gspec] * 4,
            out_specs=pl.BlockSpec((None, tr, c), lambda a, b, core_ref: (a, b, 0))),
        out_shape=jax.ShapeDtypeStruct(shape, F32), compiler_params=_params(2),
    )(core, halves[0][0], halves[0][1], halves[1][0], halves[1][1])


W_IN_FLAT_STEPS = 19


def _to_flat(a):
    nl, r, c = a.shape
    return jnp.transpose(jnp.transpose(a, (2, 0, 1)).reshape(c, nl, r // LANE, LANE), (0, 2, 1, 3)).reshape(-1, LANE)


def _from_flat(f, shape):
    nl, r, c = shape
    return jnp.transpose(jnp.transpose(f.reshape(c, r // LANE, nl, LANE), (0, 2, 1, 3)).reshape(c, nl, r), (1, 2, 0))


def _adamw_flat(name, w, g, m, v):
    n = w.shape[0]
    tr = n // W_IN_FLAT_STEPS

    def body(w_ref, g_ref, m_ref, v_ref, d_ref, nm_ref, nv_ref):
        d, nm, nv = _adamw_math(w_ref[...], g_ref[...], m_ref[...], v_ref[...])
        d_ref[...] = d
        nm_ref[...] = nm
        nv_ref[...] = nv

    spec = pl.BlockSpec((tr, LANE), lambda i: (i, 0))
    shp = jax.ShapeDtypeStruct(w.shape, F32)
    return pl.pallas_call(body, name=name, grid=(W_IN_FLAT_STEPS,), in_specs=[spec] * 4, out_specs=[spec] * 3,
                          out_shape=[shp, shp, shp], compiler_params=_params(1))(w, g, m, v)


def _pair_sum(name, g, x, c, narrow=False):
    n, r, cc = g.shape
    hr = r // 2
    tr = 128 if hr % 128 == 0 else hr
    nb = hr // tr

    def body(c_ref, g_ref, x_ref, o_ref, *narrow_ref):
        total = g_ref[...] + x_ref[...]
        o_ref[...] = total
        if narrow:
            narrow_ref[0][...] = total.astype(BF16)

    spec = pl.BlockSpec((None, tr, cc), lambda i, j, c_ref: (i, j, 0))
    shapes = [jax.ShapeDtypeStruct((n, hr, cc), F32)] + ([jax.ShapeDtypeStruct((n, hr, cc), BF16)] if narrow else [])
    return pl.pallas_call(
        body, name=name,
        grid_spec=pltpu.PrefetchScalarGridSpec(
            num_scalar_prefetch=1, grid=(n, nb),
            in_specs=[pl.BlockSpec((None, tr, cc), lambda i, j, c_ref: (i, c_ref[0] * nb + j, 0)), spec],
            out_specs=[spec] * len(shapes)),
        out_shape=shapes, compiler_params=_params(2),
    )(c, g, x)


def _sum_slots(name, own, landed, chip):
    n, r, c = own.shape
    tr = 128 if r % 128 == 0 else r

    def body(chip_ref, a_ref, b_ref, c_ref, d_ref, o_ref):
        o_ref[...] = ((a_ref[...] + b_ref[...].astype(F32)) + c_ref[...].astype(F32)) + d_ref[...].astype(F32)

    slot = lambda d: pl.BlockSpec((None, tr, c), lambda j, chip_ref: ((chip_ref[0] + d) % n, j, 0))
    return pl.pallas_call(
        body, name=name,
        grid_spec=pltpu.PrefetchScalarGridSpec(
            num_scalar_prefetch=1, grid=(r // tr,), in_specs=[slot(0), slot(1), slot(2), slot(3)],
            out_specs=pl.BlockSpec((tr, c), lambda j, chip_ref: (j, 0))),
        out_shape=jax.ShapeDtypeStruct((r, c), F32), compiler_params=_params(1),
    )(chip, own, landed, landed, landed)


def _me():
    return lax.axis_index("x"), lax.axis_index("y"), lax.axis_index("c")


def _other_chips(x, y):
    return [(1 - x, y), (x, 1 - y), (1 - x, 1 - y)]


def _dma_sems(*counts):
    return [pltpu.SemaphoreType.DMA((n,)) for n in counts]


def _half_rows(rows, which, align):
    return pl.ds(pl.multiple_of(which * (rows // 2), align), rows // 2)


def _gather_first_layer(w_in, others):
    n = len(others)
    rows = w_in.shape[1]

    def body(*refs):
        w_ref, o_refs = refs[0], refs[1:1 + n]
        out, keep = refs[1 + n], refs[2 + n:3 + 3 * n]
        stage, ici_send, ici_recv, d2d_send, d2d_recv, local_sem = refs[3 + 3 * n:]
        x, y, c = _me()
        k = 2 * x + y
        chips = _other_chips(x, y)
        stage[...] = w_ref[0].astype(BF16)
        local = pltpu.make_async_copy(stage, out.at[k], local_sem)
        local.start()
        mine = _half_rows(rows, c, 16)
        first, passed = [], []
        for j, (px, py) in enumerate(chips):
            cp = pltpu.make_async_remote_copy(
                src_ref=stage.at[mine], dst_ref=out.at[k, mine], send_sem=ici_send.at[j], recv_sem=ici_recv.at[j],
                device_id=(px, py, c), device_id_type=MESH)
            cp.start()
            first.append(cp)
        keep[0][...] = w_ref[1].astype(BF16)
        for t in range(n):
            for l in range(2):
                keep[1 + 2 * t + l][...] = o_refs[t][l].astype(BF16)
        for j, (px, py) in enumerate(chips):
            landed = out.at[2 * px + py, mine]
            first[j].wait_recv()
            cp = pltpu.make_async_remote_copy(
                src_ref=landed, dst_ref=landed, send_sem=d2d_send.at[j], recv_sem=d2d_recv.at[j],
                device_id=(x, y, 1 - c), device_id_type=MESH)
            cp.start()
            passed.append(cp)
        for cp in passed:
            cp.wait_recv()
        for cp in first + passed:
            cp.wait_send()
        local.wait()

    kept = [jax.ShapeDtypeStruct(w_in.shape[1:], BF16)]
    for o in others:
        kept += [jax.ShapeDtypeStruct(o.shape[1:], BF16)] * 2
    return pl.pallas_call(
        body, name="gather_first_layer",
        in_specs=[VMEM_SPEC] * (1 + n), out_specs=[ANY] + [VMEM_SPEC] * len(kept),
        out_shape=[jax.ShapeDtypeStruct((4,) + w_in.shape[1:], BF16)] + kept,
        scratch_shapes=[pltpu.VMEM(w_in.shape[1:], BF16)] + _dma_sems(3, 3, 3, 3) + [pltpu.SemaphoreType.DMA],
        compiler_params=pltpu.CompilerParams(vmem_limit_bytes=VMEM_LIMIT),
    )(w_in, *others)


def _run_comm(name, comm):
    nci, nco = len(comm.ins), len(comm.out_shapes)

    def body(*refs):
        copies = comm.make(refs[:nci], refs[nci:nci + nco], refs[nci + nco:])
        for cp in copies:
            cp.start()
        for cp in copies:
            cp.wait()

    return pl.pallas_call(body, name=name, in_specs=[ANY] * nci, out_specs=[ANY] * nco,
                          out_shape=list(comm.out_shapes), scratch_shapes=list(comm.sems))(*comm.ins)


def _gather_comm(mine):
    n = len(mine)

    def make(ins, outs, sems):
        send_sems, recv_sems, local_sems = sems
        x, y, c = _me()
        k = 2 * x + y
        copies = []
        for t in range(n):
            copies.append(pltpu.make_async_copy(ins[t], outs[t].at[k], local_sems.at[t]))
            for j, (px, py) in enumerate(_other_chips(x, y)):
                copies.append(pltpu.make_async_remote_copy(
                    src_ref=ins[t], dst_ref=outs[t].at[k], send_sem=send_sems.at[3 * t + j],
                    recv_sem=recv_sems.at[3 * t + j], device_id=(px, py, c), device_id_type=MESH))
        return copies

    return _Comm(mine, [jax.ShapeDtypeStruct((4,) + a.shape, a.dtype) for a in mine], _dma_sems(3 * n, 3 * n, n), make)


def _swap_comm(grads):
    n = len(grads)

    def make(ins, outs, sems):
        send_sems, recv_sems = sems
        x, y, c = _me()
        return [pltpu.make_async_remote_copy(
            src_ref=ins[t].at[:, _half_rows(grads[t].shape[1], 1 - c, 8)], dst_ref=outs[t],
            send_sem=send_sems.at[t], recv_sem=recv_sems.at[t], device_id=(x, y, 1 - c), device_id_type=MESH)
            for t in range(n)]

    shapes = [jax.ShapeDtypeStruct((g.shape[0], g.shape[1] // 2, g.shape[2]), F32) for g in grads]
    return _Comm(grads, shapes, _dma_sems(n, n), make)


def _scatter_comm(parts):
    n = len(parts)

    def make(ins, outs, sems):
        send_sems, recv_sems = sems
        x, y, c = _me()
        k = 2 * x + y
        return [pltpu.make_async_remote_copy(
            src_ref=ins[t].at[2 * px + py], dst_ref=outs[t].at[k], send_sem=send_sems.at[3 * t + j],
            recv_sem=recv_sems.at[3 * t + j], device_id=(px, py, c), device_id_type=MESH)
            for t in range(n) for j, (px, py) in enumerate(_other_chips(x, y))]

    return _Comm(parts, [jax.ShapeDtypeStruct(p.shape, p.dtype) for p in parts], _dma_sems(3 * n, 3 * n), make)


def _share_comm(totals):
    n = len(totals)

    def make(ins, outs, sems):
        send_sems, recv_sems = sems
        x, y, c = _me()
        return [pltpu.make_async_remote_copy(
            src_ref=ins[t], dst_ref=outs[t], send_sem=send_sems.at[t], recv_sem=recv_sems.at[t],
            device_id=(x, y, 1 - c), device_id_type=MESH) for t in range(n)]

    return _Comm(totals, [jax.ShapeDtypeStruct(t.shape, F32) for t in totals], _dma_sems(n, n), make)


def _small_allreduce_adamw(part, w, m, v):
    shape = part.shape

    def body(part_ref, w_ref, m_ref, v_ref, g_ref, d_ref, nm_ref, nv_ref, slots, send_sems, recv_sems):
        x, y, c = _me()
        me = 4 * x + 2 * y + c
        slots[me] = part_ref[...]
        copies = []
        for d in range(1, 8):
            peer = (x ^ (d >> 2), y ^ ((d >> 1) & 1), c ^ (d & 1))
            cp = pltpu.make_async_remote_copy(
                src_ref=part_ref, dst_ref=slots.at[me], send_sem=send_sems.at[d - 1], recv_sem=recv_sems.at[d - 1],
                device_id=peer, device_id_type=MESH)
            cp.start()
            copies.append(cp)
        for cp in copies:
            cp.wait()
        g = slots[0]
        for i in range(1, 8):
            g = g + slots[i]
        g_ref[...] = g
        d, nm, nv = _adamw_math(w_ref[...], g, m_ref[...], v_ref[...])
        d_ref[...] = d
        nm_ref[...] = nm
        nv_ref[...] = nv

    shp = jax.ShapeDtypeStruct(shape, F32)
    return pl.pallas_call(
        body, name="small_allreduce_adamw", in_specs=[VMEM_SPEC] * 4, out_specs=[VMEM_SPEC] * 4,
        out_shape=[shp, shp, shp, shp],
        scratch_shapes=[pltpu.VMEM((8,) + shape, F32), pltpu.SemaphoreType.DMA((7,)), pltpu.SemaphoreType.DMA((7,))],
    )(part, w, m, v)


TM = 512
T_FOX = 1024
T_DIL_FWD = 1024
T_DIL_BWD = 512
TN_DW = 1408


def _layer_fwd(l, h, p_i, w_inside, w_outside, consts, comm=None, target=None):
    w_all, _, norm_g, b_f, qkg = w_inside
    bsum, _, bias_t, _, rope = consts
    u, z, qa, ka, va, qb, kb, vb, c_spread, c_t, va_t, vb_t = _inproj(l, h, norm_g, w_all, b_f, qkg, bsum, rope, TM)
    oa, lse_a, *landed = _attn_fwd(f"fox_fwd_l{l}", True, qa, ka, va_t, (c_spread, c_t), T_FOX, comm)
    ob, lse_b = _attn_fwd(f"dil_fwd_l{l}", False, qb, kb, vb_t, (bias_t,), T_DIL_FWD)
    if comm is not None:
        w_outside = w_outside(landed)
    w_out, _, w_gate, _, w_ple, ple_g = w_outside
    h2, e, gate, *out = _outproj(l, h, oa, ob, z, p_i, w_out, ple_g, w_gate, w_ple, TM, target)
    out = out[0] if target is None else tuple(out)
    saved = (h, u, z, qa, ka, va, qb, kb, vb, c_spread, c_t, oa, lse_a, ob, lse_b, h2, e, gate)
    return out, saved, w_outside


def _reduce_names(tag):
    return [f"reduce_{tag}_{w}" for w in ("w_in", "w_out", "w_ple", "w_gate")]


def _layer_bwd(l, dout, p_i, wts, consts, saved, pending=None, core=None, chip=None):
    (_, w_all_t, norm_g, b_f, qkg), (_, w_out_t, _, w_gate_t, _, ple_g) = wts
    bsum, hsel, _, bias_t, rope = consts
    h, u, z, qa, ka, va, qb, kb, vb, c_spread, c_t, oa, lse_a, ob, lse_b, h2, e, gate = saved
    fused = pending is not None
    dh2, doa, dob, dga, dgb, delta_t, dw_out, dw_gate, dw_ple, dple_g, *sib = _outproj_bwd(
        l, dout, h2, e, gate, p_i, oa, ob, z, w_out_t, ple_g, w_gate_t, hsel, TM,
        _swap_comm(pending) if fused else None)
    if fused:
        pair = [_pair_sum(n, g, x, core)[0] for n, g, x in zip(_reduce_names(f"pair_l{l + 1}"), pending, sib)]
    dqt_a, dk_a, dv_a, dc, drow, *landed = _attn_bwd(
        f"fox_bwd_l{l}", True, qa, ka, va, doa, lse_a, delta_t, 0, (c_spread, c_t), T_FOX,
        _scatter_comm(pair) if fused else None)
    if fused:
        totals = [_sum_slots(n, a, y, chip) for n, a, y in zip(_reduce_names(f"chips_l{l + 1}"), pair, landed)]
    dqt_b, dk_b, dv_b, *other = _attn_bwd(f"dil_bwd_l{l}", False, qb, kb, vb, dob, lse_b, delta_t, N_PAIRS,
                                            (bias_t,), T_DIL_BWD, _share_comm(totals) if fused else None)
    dz, dqkg, dbf = _inproj_bwd_prep(l, z, dqt_a, dk_a, dv_a, dqt_b, dk_b, dv_b, dga, dgb, dc, drow, b_f, qkg,
                                     bsum, rope, TM)
    dh, dnorm_g = _inproj_bwd_dx(l, dz, w_all_t, h, norm_g, dh2, TM)
    dw_all = _inproj_bwd_dw(l, u, dz, TM, TN_DW)
    reduced = list(zip(totals, other)) if fused else None
    return dh, (dw_all, dw_out, dw_ple, dw_gate, dnorm_g[0], dbf[0, :N_HEADS], dqkg[:4], dple_g[0]), reduced


def _reduce_last(grads, core, chip, l):
    sib = _run_comm(f"reduce_swap_l{l}", _swap_comm(grads))
    pair = [_pair_sum(n, g, x, core, narrow=True) for n, g, x in zip(_reduce_names(f"pair_l{l}"), grads, sib)]
    landed = _run_comm(f"reduce_scatter_l{l}", _scatter_comm([p[1] for p in pair]))
    totals = [_sum_slots(n, p[0], y, chip) for n, p, y in zip(_reduce_names(f"chips_l{l}"), pair, landed)]
    return list(zip(totals, _run_comm(f"reduce_share_l{l}", _share_comm(totals))))


N_FA = 2048


W_SHARD = N_IN // 4


def _shard_pieces(lo, hi):
    cuts = [(k, max(lo, k * W_SHARD), min(hi, (k + 1) * W_SHARD)) for k in range(4)]
    return [(k, a - k * W_SHARD, b - k * W_SHARD) for k, a, b in cuts if a < b]


def _in_weights(l, g_in, norm_g, b_f, qk_norm_g):
    order = _shard_pieces(0, N_FA) + _shard_pieces(N_FA + N_HEADS, N_IN) + _shard_pieces(N_FA, N_FA + N_HEADS)
    w_all = jnp.concatenate([g_in[k, :, a:b] for k, a, b in order]
                            + [jnp.zeros((D_MODEL, LANE - N_HEADS), g_in.dtype)], axis=1)
    qkg = jnp.pad(jnp.tile(qk_norm_g[l], (1, N_HEADS)), ((0, 4), (0, 0)))
    bf = jnp.pad(b_f[l], (0, LANE - N_HEADS))[None, :]
    return w_all, w_all.T, norm_g[l][None, :], bf, qkg


def _out_weights(l, g_out, g_ple, g_gate, ple_norm_g):
    w_out = g_out.reshape(D_MODEL, D_MODEL)
    w_gate = g_gate.reshape(D_MODEL, D_MODEL)
    w_ple = jnp.transpose(g_ple, (1, 0, 2)).reshape(PLE_DIM, D_MODEL)
    return w_out, w_out.T, w_gate, w_gate.T, w_ple, ple_norm_g[l][None, :]


def _slot_layout(dw_all, dw_out, dw_ple, dw_gate):
    regions = ((0, N_FA, 0), (N_FA, N_FA + N_HEADS, N_MAIN - N_FA), (N_FA + N_HEADS, N_IN, -N_HEADS))

    def shard(k):
        cuts = [(max(k * W_SHARD, a) + shift, min((k + 1) * W_SHARD, b) + shift) for a, b, shift in regions]
        return jnp.concatenate([dw_all[:, a:b] for a, b in cuts if a < b], axis=1)

    return (jnp.stack([shard(k) for k in range(4)]),
            dw_out.reshape(4, D_MODEL // 4, D_MODEL),
            jnp.transpose(dw_ple.reshape(PLE_DIM, 4, D_MODEL // 4), (1, 0, 2)),
            dw_gate.reshape(4, D_MODEL // 4, D_MODEL))


SMALL_ROWS = 40


def _pack_small(norm_g, ple_norm_g, qk_norm_g, b_f, last=0.0):
    flat = jnp.concatenate([norm_g.reshape(-1), ple_norm_g.reshape(-1), qk_norm_g.reshape(-1), b_f.reshape(-1)])
    flat = jnp.pad(flat, (0, SMALL_ROWS * LANE - flat.shape[0] - 1))
    return jnp.concatenate([flat, jnp.reshape(last, (1,)).astype(F32)]).reshape(SMALL_ROWS, LANE)


def _unpack_small(packed):
    flat = packed.reshape(-1)
    n1, n2, n3 = 2 * D_MODEL, 4 * D_MODEL, 4 * D_MODEL + 2 * 4 * HEAD_DIM
    return (flat[:n1].reshape(2, D_MODEL), flat[n1:n2].reshape(2, D_MODEL), flat[n2:n3].reshape(2, 4, HEAD_DIM),
            flat[n3:n3 + 2 * N_HEADS].reshape(2, N_HEADS))


def kernel(x, p, positions, norm_g, w_in, b_f, qk_norm_g, w_out, w_ple, ple_norm_g, w_ple_gate, loss_target,
           m_norm_g, m_w_in, m_b_f, m_qk_norm_g, m_w_out, m_w_ple, m_ple_norm_g, m_w_ple_gate,
           v_norm_g, v_w_in, v_b_f, v_qk_norm_g, v_w_out, v_w_ple, v_ple_norm_g, v_w_ple_gate):
    assert w_in.shape[0] == 2, "the schedule below is written for two layers"
    w_in0, *kept = _gather_first_layer(w_in, [w_out, w_ple, w_ple_gate])
    consts = (_head_block_diag(), _head_select(), _dil_bias(T_DIL_FWD), _dil_bias(T_DIL_BWD),
              _rope_tables(positions[0]))
    later = {}

    def outside0(landed):
        later["w_in1"] = landed[0]
        later["out1"] = landed[2::2]
        return _out_weights(0, *landed[1::2], ple_norm_g)

    inside0 = _in_weights(0, w_in0, norm_g, b_f, qk_norm_g)
    h1, saved0, outside0 = _layer_fwd(0, x[0], p[0, 0], inside0, outside0, consts, _gather_comm(kept))
    wts0 = (inside0, outside0)
    wts1 = (_in_weights(1, later["w_in1"], norm_g, b_f, qk_norm_g), _out_weights(1, *later["out1"], ple_norm_g))
    (dh, sq), saved1, _ = _layer_fwd(1, h1, p[1, 0], *wts1, consts, target=loss_target[0])

    core = lax.axis_index("c").astype(jnp.int32).reshape(1)
    chip = (2 * lax.axis_index("x") + lax.axis_index("y")).astype(jnp.int32).reshape(1)
    dh, grads1, _ = _layer_bwd(1, dh, p[1, 0], wts1, consts, saved1)
    dh, grads0, reduced1 = _layer_bwd(0, dh, p[0, 0], wts0, consts, saved0, _slot_layout(*grads1[:4]), core, chip)
    reduced0 = _reduce_last(_slot_layout(*grads0[:4]), core, chip, 0)
    grad_x = dh[None]
    small = [grads0[4:], grads1[4:]]
    n_layers = 2

    outs = {}
    for t, (name, w, m, v) in enumerate((("w_in", w_in, m_w_in, v_w_in), ("w_out", w_out, m_w_out, v_w_out),
                                         ("w_ple", w_ple, m_w_ple, v_w_ple),
                                         ("w_ple_gate", w_ple_gate, m_w_ple_gate, v_w_ple_gate))):
        if name == "w_in":
            g = _assemble_halves("assemble_w_in", w.shape, (reduced0[t], reduced1[t]), core)
            g_flat = _to_flat(g)
            flat = _adamw_flat("adamw_w_in", _to_flat(w), g_flat, _to_flat(m), _to_flat(v))
            outs[name] = tuple(_from_flat(f, w.shape) for f in (g_flat,) + tuple(flat))
        else:
            outs[name] = tuple(_adamw(f"adamw_{name}", w, (reduced0[t], reduced1[t]), m, v, core))

    part = _pack_small(jnp.stack([s[0] for s in small]), jnp.stack([s[3] for s in small]),
                       jnp.stack([s[2] for s in small]).reshape(n_layers, 4, N_HEADS, HEAD_DIM).sum(axis=2),
                       jnp.stack([s[1] for s in small]), 0.5 / D_MODEL * jnp.sum(sq))
    packed = _small_allreduce_adamw(part, _pack_small(norm_g, ple_norm_g, qk_norm_g, b_f),
                                    _pack_small(m_norm_g, m_ple_norm_g, m_qk_norm_g, m_b_f),
                                    _pack_small(v_norm_g, v_ple_norm_g, v_qk_norm_g, v_b_f))
    loss = packed[0][SMALL_ROWS - 1, LANE - 1]
    sm = [_unpack_small(a) for a in packed]
    for i, name in enumerate(("norm_g", "ple_norm_g", "qk_norm_g", "b_f")):
        outs[name] = tuple(sm[j][i] for j in range(4))

    order = ("norm_g", "w_in", "b_f", "qk_norm_g", "w_out", "w_ple", "ple_norm_g", "w_ple_gate")
    return (loss, grad_x) + tuple(outs[n][j] for j in range(4) for n in order)
```

```python
import functools
from typing import Any, Callable, NamedTuple, Sequence

import numpy as np
import jax
import jax.numpy as jnp
from jax import lax
from jax.experimental import pallas as pl
from jax.experimental.pallas import tpu as pltpu

F32 = jnp.float32
BF16 = jnp.bfloat16
MESH = pl.DeviceIdType.MESH

D_MODEL = 1024
HEAD_DIM = 64
D_BRANCH = 512
N_HEADS = 8
N_PAIRS = 4
N_IN = 4104
N_MAIN = 4096
N_ALL = 4224
PLE_DIM = 256
ROPE_THETA = 500000.0
ROPE_HALF = 8
EPS = 1e-6
NEG = -1e30
M_INIT = -1e29
Q_SCALE = HEAD_DIM ** -0.5
LOG2E = 1.4426950408889634
LN2 = 0.6931471805599453
DIL_PATTERNS = ((128, 1), (512, 4), (2048, 16))
DIL_BACK = 2048
ADAM_LR, ADAM_B1, ADAM_B2, ADAM_EPS, ADAM_WD, ADAM_STEP = 0.001, 0.9, 0.999, 1e-08, 0.01, 10
VMEM_LIMIT = 56 * 1024 * 1024
LANE = 128


def _dot(a, b):
    return jnp.dot(a, b, preferred_element_type=F32)


def _dot_nt(a, b):
    return lax.dot_general(a, b, (((1,), (1,)), ((), ())), preferred_element_type=F32)


def _dot_tn(a, b):
    return lax.dot_general(a, b, (((0,), (0,)), ((), ())), preferred_element_type=F32)


def _split_dot(x, w):
    hi = x.astype(BF16)
    lo = (x - hi.astype(F32)).astype(BF16)
    return _dot(hi, w) + _dot(lo, w)


def _head_sums(x, bs):
    w = bs.shape[0]
    return jnp.concatenate([_split_dot(x[:, w * k:w * (k + 1)], bs) for k in range(x.shape[1] // w)], axis=1)


def _split3_dot(w, x):
    hi = x.astype(BF16)
    r1 = x - hi.astype(F32)
    mid = r1.astype(BF16)
    lo = (r1 - mid.astype(F32)).astype(BF16)
    return _dot(w, hi) + _dot(w, mid) + _dot(w, lo)


def _sigmoid(x):
    return 1.0 / (1.0 + jnp.exp(-x))


def _params(n_grid):
    return pltpu.CompilerParams(dimension_semantics=("arbitrary",) * n_grid,
                                vmem_limit_bytes=VMEM_LIMIT)


def _full(shape):
    nd = len(shape)
    return pl.BlockSpec(shape, lambda *_: (0,) * nd)


ANY = pl.BlockSpec(memory_space=pl.ANY)
VMEM_SPEC = pl.BlockSpec(memory_space=pltpu.VMEM)


class _Comm(NamedTuple):
    ins: Sequence[Any]
    out_shapes: Sequence[Any]
    sems: Sequence[Any]
    make: Callable[..., Any]


def _fuse_comm(body, n_in, n_out, comm, grid):
    if comm is None:
        return body
    nci, nco, ncs = len(comm.ins), len(comm.out_shapes), len(comm.sems)

    def fused(*refs):
        a, b = n_in + nci, n_in + nci + n_out
        ins, cins, outs, couts = refs[:n_in], refs[n_in:a], refs[a:b], refs[b:b + nco]
        scratch, sems = refs[b + nco:len(refs) - ncs], refs[len(refs) - ncs:]
        first = functools.reduce(jnp.logical_and, [pl.program_id(d) == 0 for d in range(len(grid))])
        last = functools.reduce(jnp.logical_and, [pl.program_id(d) == n - 1 for d, n in enumerate(grid)])

        @pl.when(first)
        def _():
            for cp in comm.make(cins, couts, sems):
                cp.start()

        body(*ins, *outs, *scratch)

        @pl.when(last)
        def _():
            for cp in comm.make(cins, couts, sems):
                cp.wait()

    return fused


def _comm_args(comm):
    if comm is None:
        return [], [], [], [], []
    return (list(comm.ins), [ANY] * len(comm.ins), [ANY] * len(comm.out_shapes), list(comm.out_shapes),
            list(comm.sems))


HEADS_PER_BLOCK = 4


def _head_block_diag():
    i = np.arange(HEADS_PER_BLOCK * HEAD_DIM)
    return jnp.asarray((i[:, None] // HEAD_DIM == i[None, :] // HEAD_DIM).astype(np.float32), BF16)


def _head_select():
    i = np.arange(2 * D_BRANCH)
    j = np.arange(LANE)
    return jnp.asarray((i[:, None] // HEAD_DIM == j[None, :]).astype(np.float32), BF16)


def _dil_bias(t):
    nb = DIL_BACK // t + 1
    qi = np.arange(t)[:, None]
    ki = np.arange(t)[None, :]
    tiles = []
    for r in range(nb):
        d = r * t + qi - ki
        mult = np.zeros((t, t), np.int64)
        for window, dil in DIL_PATTERNS:
            mult += ((d >= 0) & (d <= window) & (d % dil == 0)).astype(np.int64)
        b = np.where(mult > 0, np.log2(np.maximum(mult, 1)), NEG).astype(np.float32)
        tiles.append(b.T)
    return jnp.asarray(np.stack(tiles))


def _rope_tables(positions):
    inv_freq = ROPE_THETA ** (-jnp.arange(ROPE_HALF, dtype=F32) / ROPE_HALF)
    ang = positions.astype(F32)[:, None] * inv_freq
    cos, sin = jnp.cos(ang), jnp.sin(ang)
    s = positions.shape[0]
    rest = HEAD_DIM - 2 * ROPE_HALF
    one, zero, zero8 = jnp.ones((s, rest), F32), jnp.zeros((s, rest), F32), jnp.zeros((s, ROPE_HALF), F32)
    c = jnp.concatenate([cos, cos, one], axis=1)
    s1 = jnp.concatenate([zero8, sin, zero], axis=1)
    s2 = jnp.concatenate([-sin, zero8, zero], axis=1)
    return tuple(jnp.tile(t, (1, 2)) for t in (c, s1, s2))


def _rope_fwd(x, c, s1, s2):
    return x * c + pltpu.roll(x, ROPE_HALF, 1) * s1 + pltpu.roll(x, LANE - ROPE_HALF, 1) * s2


def _rope_bwd(dy, c, s1, s2):
    return dy * c + pltpu.roll(dy * s1, LANE - ROPE_HALF, 1) + pltpu.roll(dy * s2, ROPE_HALF, 1)


def _log_sigmoid(x):
    return jnp.minimum(x, 0.0) - jnp.log(1.0 + jnp.exp(-jnp.abs(x)))


def _inproj(l, h, norm_g, w_all, b_f, qkg, bsum, rope, tm):
    s = h.shape[0]
    rc, rs1, rs2 = rope

    def body(h_ref, g_ref, w_ref, bf_ref, qkg_ref, bsum_ref, rc_ref, rs1_ref, rs2_ref,
             u_ref, z_ref, qa_ref, ka_ref, va_ref, qb_ref, kb_ref, vb_ref, cs_ref, ct_ref, vat_ref, vbt_ref, carry):
        @pl.when(pl.program_id(0) == 0)
        def _():
            carry[...] = jnp.zeros_like(carry)

        hh = h_ref[...]
        r = lax.rsqrt(jnp.mean(hh * hh, axis=-1, keepdims=True) + EPS)
        u = (hh * r * g_ref[...]).astype(BF16)
        u_ref[...] = u
        for k in range(N_ALL // LANE // 3):
            cols = slice(3 * LANE * k, 3 * LANE * (k + 1))
            z_ref[:, cols] = _dot(u, w_ref[:, cols])
        bs = bsum_ref[...]

        def head_norm(x, row):
            ms = _head_sums(x * x, bs) * (1.0 / HEAD_DIM)
            return x * lax.rsqrt(ms + EPS) * qkg_ref[row:row + 1, :]

        def seg(k):
            return z_ref[:, D_BRANCH * k:D_BRANCH * (k + 1)]

        qa_ref[...] = (head_norm(seg(0), 0) * (Q_SCALE * LOG2E)).astype(BF16)
        ka_ref[...] = head_norm(seg(1), 1).astype(BF16)
        va_ref[...] = seg(2).astype(BF16)
        vat_ref[...] = seg(2).T.astype(BF16)
        qn = head_norm(seg(4), 2) * (Q_SCALE * LOG2E)
        kn = head_norm(seg(5), 3)
        c, s1, s2 = rc_ref[...], rs1_ref[...], rs2_ref[...]
        for k in range(D_BRANCH // LANE):
            cols = slice(LANE * k, LANE * (k + 1))
            qb_ref[:, cols] = _rope_fwd(qn[:, cols], c, s1, s2).astype(BF16)
            kb_ref[:, cols] = _rope_fwd(kn[:, cols], c, s1, s2).astype(BF16)
        vb_ref[...] = seg(6).astype(BF16)
        vbt_ref[...] = seg(6).T.astype(BF16)
        logf = _log_sigmoid(z_ref[:, N_MAIN:N_ALL] + bf_ref[...])
        csum = _split3_dot(_tri(tm, False), logf) + carry[0:1, :]
        carry[...] = jnp.broadcast_to(csum[tm - 1:tm, :], carry.shape)
        csum = csum * LOG2E
        ct = csum.T
        for p in range(N_PAIRS):
            cs_ref[:, LANE * p:LANE * (p + 1)] = csum if p == 0 else pltpu.roll(csum, LANE - 2 * p, 1)
            ct_ref[p, :, :] = ct[2 * p:2 * p + 2, :]

    row = lambda w: pl.BlockSpec((tm, w), lambda i: (i, 0))
    colt = pl.BlockSpec((D_BRANCH, tm), lambda i: (0, i))
    bf = lambda: jax.ShapeDtypeStruct((s, D_BRANCH), BF16)
    bft = lambda: jax.ShapeDtypeStruct((D_BRANCH, s), BF16)
    return pl.pallas_call(
        body, name=f"inproj_l{l}", grid=(s // tm,),
        in_specs=[row(D_MODEL), _full((1, D_MODEL)), _full((D_MODEL, N_ALL)), _full((1, LANE)),
                  _full((8, D_BRANCH)), _full((HEADS_PER_BLOCK * HEAD_DIM,) * 2), row(LANE), row(LANE), row(LANE)],
        out_specs=[row(D_MODEL), row(N_ALL)] + [row(D_BRANCH)] * 6
        + [row(N_PAIRS * LANE), pl.BlockSpec((N_PAIRS, 2, tm), lambda i: (0, 0, i)), colt, colt],
        out_shape=[jax.ShapeDtypeStruct((s, D_MODEL), BF16), jax.ShapeDtypeStruct((s, N_ALL), F32),
                   bf(), bf(), bf(), bf(), bf(), bf(), jax.ShapeDtypeStruct((s, N_PAIRS * LANE), F32),
                   jax.ShapeDtypeStruct((N_PAIRS, 2, s), F32), bft(), bft()],
        scratch_shapes=[pltpu.VMEM((8, LANE), F32)],
        compiler_params=_params(1),
    )(h, norm_g, w_all, b_f, qkg, bsum, rc, rs1, rs2)


def _tri(t, upper):
    a = lax.broadcasted_iota(jnp.int32, (t, t), 0)
    b = lax.broadcasted_iota(jnp.int32, (t, t), 1)
    return jnp.where((b >= a) if upper else (b <= a), 1.0, 0.0).astype(BF16)


def _attn_fwd(name, fox, q, k, vt, extra, t, comm=None):
    s = q.shape[0]
    nq = s // t
    nb = DIL_BACK // t + 1

    def body(*refs):
        if fox:
            q_ref, k_ref, vt_ref, ccol_ref, crow_ref, o_ref, lse_ref, m_scr, l_scr, acc_scr = refs
        else:
            q_ref, k_ref, vt_ref, bias_ref, o_ref, lse_ref, m_scr, l_scr, acc_scr = refs
        i = pl.program_id(1)
        lane = lax.broadcasted_iota(jnp.int32, (t, LANE), 1)
        first = lane < HEAD_DIM
        qq = q_ref[...]
        zero = jnp.zeros_like(qq)
        qh = (jnp.where(first, qq, zero), jnp.where(first, zero, qq))
        m_scr[...] = jnp.full(m_scr.shape, M_INIT, F32)
        l_scr[...] = jnp.zeros_like(l_scr)
        acc_scr[...] = jnp.zeros_like(acc_scr)
        ones = jnp.ones((16, t), BF16)

        half = t // 2
        whole, lo, hi = slice(0, t), slice(0, half), slice(half, t)

        def block(j, ksl, qsl, causal):
            nk_, nq_ = ksl.stop - ksl.start, qsl.stop - qsl.start
            rows = pl.ds(pl.multiple_of(j * t + ksl.start, LANE), nk_)
            ks = k_ref[rows, :]
            vts = jnp.concatenate([vt_ref[:, rows], ones[:, :nk_]], axis=0)
            if fox:
                ccol = ccol_ref[rows, :]
            for h in range(2):
                st = _dot_nt(ks, qh[h][qsl, :])
                if fox:
                    st = st + (crow_ref[h:h + 1, qsl] - ccol[:, h:h + 1])
                    if causal:
                        ki = lax.broadcasted_iota(jnp.int32, (nk_, nq_), 0) + ksl.start
                        qi = lax.broadcasted_iota(jnp.int32, (nk_, nq_), 1) + qsl.start
                        st = jnp.where(ki <= qi, st, NEG)
                else:
                    st = st + bias_ref[i - j, ksl, qsl]
                m_old = m_scr[h, :, qsl]
                m_new = jnp.maximum(m_old, jnp.max(st, axis=0, keepdims=True))
                alpha = jnp.exp2(m_old - m_new)
                pb = jnp.exp2(st - m_new).astype(BF16)
                pv = _dot(vts, pb)
                l_scr[h, :, qsl] = alpha * l_scr[h, :, qsl] + pv[LANE:LANE + 1, :]
                acc_scr[h, :, qsl] = alpha * acc_scr[h, :, qsl] + pv[:LANE, :]
                m_scr[h, :, qsl] = m_new

        def full(j, c):
            block(j, whole, whole, False)
            return c

        if fox:
            lax.fori_loop(0, i, full, 0)
        else:
            @pl.when(i >= nb - 1)
            def _():
                block(i - (nb - 1), lo, lo, False)
                block(i - (nb - 1), hi, whole, False)

            lax.fori_loop(jnp.maximum(i - (nb - 2), 0), i, full, 0)
        if fox:
            block(i, whole, whole, True)
        else:
            block(i, lo, lo, False)
            block(i, whole, hi, False)

        sub = lax.broadcasted_iota(jnp.int32, (LANE, t), 0)
        ot = jnp.where(sub < HEAD_DIM, acc_scr[0] / l_scr[0], acc_scr[1] / l_scr[1])
        o_ref[...] = ot.T
        for h in range(2):
            lse_ref[h:h + 1, :] = m_scr[h] + jnp.log2(l_scr[h])

    qspec = pl.BlockSpec((t, LANE), lambda hp, i: (i, hp))
    kspec = pl.BlockSpec((s, LANE), lambda hp, i: (0, hp))
    vtspec = pl.BlockSpec((LANE, s), lambda hp, i: (hp, 0))
    in_specs = [qspec, kspec, vtspec]
    if fox:
        in_specs += [kspec, pl.BlockSpec((None, 2, t), lambda hp, i: (hp, 0, i))]
    else:
        in_specs += [_full((nb, t, t))]
    grid = (N_PAIRS, nq)
    c_in, c_ispec, c_ospec, c_oshape, c_scr = _comm_args(comm)
    return pl.pallas_call(
        _fuse_comm(body, len(in_specs), 2, comm, grid), name=name, grid=grid,
        in_specs=in_specs + c_ispec,
        out_specs=[qspec, pl.BlockSpec((None, 2, t), lambda hp, i: (hp, 0, i))] + c_ospec,
        out_shape=[jax.ShapeDtypeStruct((s, D_BRANCH), F32), jax.ShapeDtypeStruct((N_PAIRS, 2, s), F32)] + c_oshape,
        scratch_shapes=[pltpu.VMEM((2, 1, t), F32), pltpu.VMEM((2, 1, t), F32), pltpu.VMEM((2, LANE, t), F32)]
        + c_scr,
        compiler_params=_params(2),
    )(q, k, vt, *extra, *c_in)


def _attn_bwd(name, fox, q, k, v, do, lse_t, delta_t, pair_offset, extra, t, comm=None):
    s = q.shape[0]
    nk = s // t
    nb = DIL_BACK // t + 1

    def body(*refs):
        if fox:
            (q_ref, k_ref, v_ref, do_ref, lse_ref, delta_ref, ccol_ref, crow_ref,
             dqt_ref, dk_ref, dv_ref, dc_ref, drow_ref) = refs
        else:
            q_ref, k_ref, v_ref, do_ref, lse_ref, delta_ref, bias_ref, dqt_ref, dk_ref, dv_ref = refs
        j = pl.program_id(1)

        @pl.when(j == 0)
        def _():
            dqt_ref[...] = jnp.zeros_like(dqt_ref)
            if fox:
                drow_ref[...] = jnp.zeros_like(drow_ref)

        lane = lax.broadcasted_iota(jnp.int32, (t, LANE), 1)
        first = lane < HEAD_DIM
        ks = k_ref[...]
        vs = v_ref[...]
        kt = ks.astype(F32).T
        sub = lax.broadcasted_iota(jnp.int32, (LANE, t), 0)
        kth = (jnp.where(sub < HEAD_DIM, kt, 0.0).astype(BF16), jnp.where(sub < HEAD_DIM, 0.0, kt).astype(BF16))
        dk_ref[...] = jnp.zeros_like(dk_ref)
        dv_ref[...] = jnp.zeros_like(dv_ref)
        if fox:
            dc_ref[...] = jnp.zeros_like(dc_ref)
            ccol = ccol_ref[...]

        half = t // 2
        whole, lo, hi = slice(0, t), slice(0, half), slice(half, t)

        def block(i, ksl, qsl, causal):
            nk_, nq_ = ksl.stop - ksl.start, qsl.stop - qsl.start
            rows = pl.ds(pl.multiple_of(i * t + qsl.start, LANE), nq_)
            qq = q_ref[rows, :]
            dd = do_ref[rows, :]
            zero = jnp.zeros_like(qq)
            qh = (jnp.where(first[:nq_], qq, zero), jnp.where(first[:nq_], zero, qq))
            dh = (jnp.where(first[:nq_], dd, zero), jnp.where(first[:nq_], zero, dd))
            for h in range(2):
                st = _dot_nt(ks[ksl, :], qh[h])
                if fox:
                    st = st + (crow_ref[h:h + 1, rows] - ccol[ksl, h:h + 1])
                    if causal:
                        ki = lax.broadcasted_iota(jnp.int32, (nk_, nq_), 0) + ksl.start
                        qi = lax.broadcasted_iota(jnp.int32, (nk_, nq_), 1) + qsl.start
                        st = jnp.where(ki <= qi, st, NEG)
                else:
                    st = st + bias_ref[i - j, ksl, qsl]
                pt = jnp.exp2(st - lse_ref[h:h + 1, rows])
                dpt = _dot_nt(vs[ksl, :], dh[h])
                dst = pt * (dpt - delta_ref[h:h + 1, rows])
                dv_ref[ksl, :] += _dot(pt.astype(BF16), dh[h])
                dsb = dst.astype(BF16)
                dk_ref[ksl, :] += _dot(dsb, qh[h])
                dqt_ref[:, rows] += _dot(kth[h][:, ksl], dsb)
                if fox:
                    dc_ref[ksl, :] -= jnp.where(lane[:nk_] == h, jnp.sum(dst, axis=1, keepdims=True), 0.0)
                    drow_ref[h:h + 1, rows] += jnp.sum(dst, axis=0, keepdims=True)

        def full(i, c):
            block(i, whole, whole, False)
            return c

        block(j, lo, whole, True)
        block(j, hi, hi, True)
        if fox:
            lax.fori_loop(j + 1, nk, full, 0)
        else:
            lax.fori_loop(j + 1, jnp.minimum(j + nb - 1, nk), full, 0)

            @pl.when(j + nb - 1 < nk)
            def _():
                block(j + nb - 1, lo, lo, False)
                block(j + nb - 1, hi, whole, False)

    kspec = pl.BlockSpec((t, LANE), lambda hp, j: (j, hp))
    qspec = pl.BlockSpec((s, LANE), lambda hp, j: (0, hp))
    rowspec = pl.BlockSpec((None, 2, s), lambda hp, j: (hp, 0, 0))
    drowspec = pl.BlockSpec((None, 2, s), lambda hp, j: (hp + pair_offset, 0, 0))
    in_specs = [qspec, kspec, kspec, qspec, rowspec, drowspec]
    out_specs = [pl.BlockSpec((LANE, s), lambda hp, j: (hp, 0)), kspec, kspec]
    out_shape = [jax.ShapeDtypeStruct((D_BRANCH, s), F32), jax.ShapeDtypeStruct((s, D_BRANCH), F32),
                 jax.ShapeDtypeStruct((s, D_BRANCH), F32)]
    if fox:
        in_specs += [kspec, rowspec]
        out_specs += [kspec, rowspec]
        out_shape += [jax.ShapeDtypeStruct((s, N_PAIRS * LANE), F32), jax.ShapeDtypeStruct((N_PAIRS, 2, s), F32)]
    else:
        in_specs += [_full((nb, t, t))]
    grid = (N_PAIRS, nk)
    c_in, c_ispec, c_ospec, c_oshape, c_scr = _comm_args(comm)
    return pl.pallas_call(
        _fuse_comm(body, len(in_specs), len(out_specs), comm, grid), name=name, grid=grid,
        in_specs=in_specs + c_ispec, out_specs=out_specs + c_ospec, out_shape=out_shape + c_oshape,
        scratch_shapes=c_scr, compiler_params=_params(2),
    )(q, k, v, do, lse_t, delta_t, *extra, *c_in)


def _silu(x):
    return x * _sigmoid(x)


def _outproj(l, h, oa, ob, z, p_i, w_out, ple_g, w_gate, w_ple, tm, target=None):
    s = h.shape[0]
    last = target is not None

    def body(h_ref, oa_ref, ob_ref, ga_ref, gb_ref, p_ref, wo_ref, pg_ref, wg_ref, wp_ref, *rest):
        if last:
            t_ref, h2_ref, e_ref, gate_ref, out_ref, acc_ref = rest
        else:
            h2_ref, e_ref, gate_ref, out_ref = rest
        a = jnp.concatenate([oa_ref[...] * _silu(ga_ref[...]), ob_ref[...] * _silu(gb_ref[...])], axis=1)
        h2 = h_ref[...] + _dot(a.astype(BF16), wo_ref[...])
        h2_ref[...] = h2
        r = lax.rsqrt(jnp.mean(h2 * h2, axis=-1, keepdims=True) + EPS)
        n2 = (h2 * r * pg_ref[...]).astype(BF16)
        gate = _sigmoid(_dot(n2, wg_ref[...]))
        e = _dot(p_ref[...].astype(BF16), wp_ref[...])
        e_ref[...] = e
        gate_ref[...] = gate
        out = h2 + e * gate
        if not last:
            out_ref[...] = out
            return

        @pl.when(pl.program_id(0) == 0)
        def _():
            acc_ref[...] = jnp.zeros_like(acc_ref)

        err = out - t_ref[...]
        out_ref[...] = err * (1.0 / D_MODEL)
        e2 = err * err
        rows = e2[0:8, :]
        for k in range(1, tm // 8):
            rows = rows + e2[8 * k:8 * (k + 1), :]
        part = rows[:, 0:LANE]
        for k in range(1, D_MODEL // LANE):
            part = part + rows[:, LANE * k:LANE * (k + 1)]
        acc_ref[...] += part

    row = lambda w: pl.BlockSpec((tm, w), lambda i: (i, 0))
    zcol = lambda k: pl.BlockSpec((tm, D_BRANCH), lambda i: (i, k))
    f = lambda: jax.ShapeDtypeStruct((s, D_MODEL), F32)
    return pl.pallas_call(
        body, name=f"outproj_l{l}", grid=(s // tm,),
        in_specs=[row(D_MODEL), row(D_BRANCH), row(D_BRANCH), zcol(3), zcol(7), row(PLE_DIM),
                  _full((D_MODEL, D_MODEL)), _full((1, D_MODEL)), _full((D_MODEL, D_MODEL)),
                  _full((PLE_DIM, D_MODEL))] + ([row(D_MODEL)] if last else []),
        out_specs=[row(D_MODEL)] * 4 + ([_full((8, LANE))] if last else []),
        out_shape=[f(), f(), f(), f()] + ([jax.ShapeDtypeStruct((8, LANE), F32)] if last else []),
        compiler_params=_params(1),
    )(h, oa, ob, z, z, p_i, w_out, ple_g, w_gate, w_ple, *([target] if last else []))


def _outproj_bwd(l, dout, h2, e, gate, p_i, oa, ob, z, w_out_t, ple_g, w_gate_t, hsel, tm, comm=None):
    s = dout.shape[0]

    def body(do_ref, h2_ref, e_ref, gate_ref, p_ref, oa_ref, ob_ref, ga_ref, gb_ref, wot_ref, pg_ref,
             wgt_ref, hsel_ref,
             dh2_ref, doa_ref, dob_ref, dga_ref, dgb_ref, delta_ref, dwo_ref, dwg_ref, dwp_ref, dpg_ref):
        @pl.when(pl.program_id(0) == 0)
        def _():
            dwo_ref[...] = jnp.zeros_like(dwo_ref)
            dwg_ref[...] = jnp.zeros_like(dwg_ref)
            dwp_ref[...] = jnp.zeros_like(dwp_ref)
            dpg_ref[...] = jnp.zeros_like(dpg_ref)

        dho = do_ref[...]
        g = gate_ref[...]
        de = (dho * g).astype(BF16)
        dwp_ref[...] += _dot_tn(p_ref[...].astype(BF16), de)
        dpre = (dho * e_ref[...] * g * (1.0 - g)).astype(BF16)
        h2 = h2_ref[...]
        pg = pg_ref[...]
        r = lax.rsqrt(jnp.mean(h2 * h2, axis=-1, keepdims=True) + EPS)
        n2 = (h2 * r * pg).astype(BF16)
        dwg_ref[...] += _dot_tn(n2, dpre)
        dn2 = _dot(dpre, wgt_ref[...])
        dpg_ref[0:1, :] += jnp.sum(dn2 * h2 * r, axis=0, keepdims=True)
        wv = dn2 * pg
        dh2 = dho + r * wv - h2 * (r * r * r) * jnp.mean(wv * h2, axis=-1, keepdims=True)
        dh2_ref[...] = dh2
        dh2b = dh2.astype(BF16)
        ga, gb, oa, ob = ga_ref[...], gb_ref[...], oa_ref[...], ob_ref[...]
        sga, sgb = _sigmoid(ga), _sigmoid(gb)
        a = jnp.concatenate([oa * ga * sga, ob * gb * sgb], axis=1).astype(BF16)
        dwo_ref[...] += _dot_tn(a, dh2b)
        da = _dot(dh2b, wot_ref[...])
        da_a, da_b = da[:, :D_BRANCH], da[:, D_BRANCH:]
        doa = da_a * ga * sga
        dob = da_b * gb * sgb
        doa_ref[...] = doa.astype(BF16)
        dob_ref[...] = dob.astype(BF16)
        dga_ref[...] = (da_a * oa * sga * (1.0 + ga * (1.0 - sga))).astype(BF16)
        dgb_ref[...] = (da_b * ob * sgb * (1.0 + gb * (1.0 - sgb))).astype(BF16)
        prod = jnp.concatenate([doa * oa, dob * ob], axis=1)
        dt = _split_dot(prod, hsel_ref[...]).T
        for pp in range(2 * N_PAIRS):
            delta_ref[pp, :, :] = dt[2 * pp:2 * pp + 2, :]

    row = lambda w: pl.BlockSpec((tm, w), lambda i: (i, 0))
    zcol = lambda k: pl.BlockSpec((tm, D_BRANCH), lambda i: (i, k))
    grid = (s // tm,)
    c_in, c_ispec, c_ospec, c_oshape, c_scr = _comm_args(comm)
    return pl.pallas_call(
        _fuse_comm(body, 13, 10, comm, grid), name=f"outproj_bwd_l{l}", grid=grid,
        in_specs=[row(D_MODEL)] * 4 + [row(PLE_DIM), row(D_BRANCH), row(D_BRANCH), zcol(3), zcol(7),
                                        _full((D_MODEL, D_MODEL)), _full((1, D_MODEL)), _full((D_MODEL, D_MODEL)),
                                        _full((2 * D_BRANCH, LANE))] + c_ispec,
        out_specs=[row(D_MODEL)] + [row(D_BRANCH)] * 4
        + [pl.BlockSpec((2 * N_PAIRS, 2, tm), lambda i: (0, 0, i)), _full((D_MODEL, D_MODEL)),
           _full((D_MODEL, D_MODEL)), _full((PLE_DIM, D_MODEL)), _full((8, D_MODEL))] + c_ospec,
        out_shape=[jax.ShapeDtypeStruct((s, D_MODEL), F32)] + [jax.ShapeDtypeStruct((s, D_BRANCH), BF16)] * 4
        + [jax.ShapeDtypeStruct((2 * N_PAIRS, 2, s), F32), jax.ShapeDtypeStruct((D_MODEL, D_MODEL), F32),
           jax.ShapeDtypeStruct((D_MODEL, D_MODEL), F32), jax.ShapeDtypeStruct((PLE_DIM, D_MODEL), F32),
           jax.ShapeDtypeStruct((8, D_MODEL), F32)] + c_oshape,
        scratch_shapes=c_scr, compiler_params=_params(1),
    )(dout, h2, e, gate, p_i, oa, ob, z, z, w_out_t, ple_g, w_gate_t, hsel, *c_in)


def _inproj_bwd_prep(l, z, dqt_a, dk_a, dv_a, dqt_b, dk_b, dv_b, dga, dgb, dc_spread, dc_rows, b_f, qkg, bsum,
                     rope, tm):
    s = z.shape[0]
    n = s // tm
    rc, rs1, rs2 = rope

    def body(zqa_ref, zka_ref, zqb_ref, zkb_ref, zf_ref, dqta_ref, dka_ref, dva_ref, dqtb_ref, dkb_ref, dvb_ref,
             dga_ref, dgb_ref, dc_ref, drow_ref, bf_ref, qkg_ref, bsum_ref, rc_ref, rs1_ref, rs2_ref,
             dz_ref, dqkg_ref, dbf_ref, carry):
        @pl.when(pl.program_id(0) == 0)
        def _():
            dqkg_ref[...] = jnp.zeros_like(dqkg_ref)
            dbf_ref[...] = jnp.zeros_like(dbf_ref)
            carry[...] = jnp.zeros_like(carry)

        lane = lax.broadcasted_iota(jnp.int32, (tm, LANE), 1)
        dc = jnp.concatenate([drow_ref[p] for p in range(N_PAIRS)] + [jnp.zeros((LANE - N_HEADS, tm), F32)], axis=0).T
        for p in range(N_PAIRS):
            part = jnp.where(lane < 2, dc_ref[:, LANE * p:LANE * (p + 1)], 0.0)
            dc = dc + (part if p == 0 else pltpu.roll(part, 2 * p, 1))
        dlogf = _split3_dot(_tri(tm, True), dc) + carry[0:1, :]
        carry[...] = jnp.broadcast_to(dlogf[0:1, :], carry.shape)

        bs = bsum_ref[...]
        c, s1, s2 = rc_ref[...], rs1_ref[...], rs2_ref[...]

        def unrope(dy):
            return jnp.concatenate([_rope_bwd(dy[:, LANE * k:LANE * (k + 1)], c, s1, s2)
                                    for k in range(D_BRANCH // LANE)], axis=1)

        def norm_bwd(k, row, dy, x_ref):
            x = x_ref[...]
            r = lax.rsqrt(_head_sums(x * x, bs) * (1.0 / HEAD_DIM) + EPS)
            dqkg_ref[row:row + 1, :] += jnp.sum(dy * x * r, axis=0, keepdims=True)
            w = dy * qkg_ref[row:row + 1, :]
            dx = r * w - x * (r * r * r) * (_head_sums(w * x, bs) * (1.0 / HEAD_DIM))
            dz_ref[:, D_BRANCH * k:D_BRANCH * (k + 1)] = dx.astype(BF16)

        norm_bwd(0, 0, dqta_ref[...].T * Q_SCALE, zqa_ref)
        norm_bwd(1, 1, dka_ref[...] * LN2, zka_ref)
        dz_ref[:, 2 * D_BRANCH:3 * D_BRANCH] = dva_ref[...].astype(BF16)
        dz_ref[:, 3 * D_BRANCH:4 * D_BRANCH] = dga_ref[...]
        norm_bwd(4, 2, unrope(dqtb_ref[...].T * Q_SCALE), zqb_ref)
        norm_bwd(5, 3, unrope(dkb_ref[...] * LN2), zkb_ref)
        dz_ref[:, 6 * D_BRANCH:7 * D_BRANCH] = dvb_ref[...].astype(BF16)
        dz_ref[:, 7 * D_BRANCH:8 * D_BRANCH] = dgb_ref[...]
        dfa = dlogf * _sigmoid(-(zf_ref[...] + bf_ref[...]))
        dz_ref[:, N_MAIN:N_ALL] = dfa.astype(BF16)
        dbf_ref[0:1, :] += jnp.sum(dfa, axis=0, keepdims=True)

    row = lambda w: pl.BlockSpec((tm, w), lambda i: (n - 1 - i, 0))
    colt = pl.BlockSpec((D_BRANCH, tm), lambda i: (0, n - 1 - i))
    zcol = lambda k: pl.BlockSpec((tm, D_BRANCH), lambda i: (n - 1 - i, k))
    zf = pl.BlockSpec((tm, LANE), lambda i: (n - 1 - i, N_MAIN // LANE))
    return pl.pallas_call(
        body, name=f"inproj_bwd_prep_l{l}", grid=(n,),
        in_specs=[zcol(0), zcol(1), zcol(4), zcol(5), zf, colt, row(D_BRANCH), row(D_BRANCH), colt, row(D_BRANCH), row(D_BRANCH),
                  row(D_BRANCH), row(D_BRANCH), row(N_PAIRS * LANE),
                  pl.BlockSpec((N_PAIRS, 2, tm), lambda i: (0, 0, n - 1 - i)), _full((1, LANE)),
                  _full((8, D_BRANCH)), _full((HEADS_PER_BLOCK * HEAD_DIM,) * 2), row(LANE), row(LANE), row(LANE)],
        out_specs=[row(N_ALL), _full((8, D_BRANCH)), _full((8, LANE))],
        out_shape=[jax.ShapeDtypeStruct((s, N_ALL), BF16), jax.ShapeDtypeStruct((8, D_BRANCH), F32),
                   jax.ShapeDtypeStruct((8, LANE), F32)],
        scratch_shapes=[pltpu.VMEM((8, LANE), F32)],
        compiler_params=_params(1),
    )(z, z, z, z, z, dqt_a, dk_a, dv_a, dqt_b, dk_b, dv_b, dga, dgb, dc_spread, dc_rows, b_f, qkg, bsum, rc, rs1, rs2)


def _inproj_bwd_dx(l, dz, w_all_t, h, norm_g, dh2, tm):
    s = dz.shape[0]

    def body(dz_ref, wt_ref, h_ref, g_ref, dh2_ref, dh_ref, dg_ref):
        @pl.when(pl.program_id(0) == 0)
        def _():
            dg_ref[...] = jnp.zeros_like(dg_ref)

        du = _dot(dz_ref[...], wt_ref[...])
        hh = h_ref[...]
        g = g_ref[...]
        r = lax.rsqrt(jnp.mean(hh * hh, axis=-1, keepdims=True) + EPS)
        dg_ref[0:1, :] += jnp.sum(du * hh * r, axis=0, keepdims=True)
        wv = du * g
        dh_ref[...] = dh2_ref[...] + r * wv - hh * (r * r * r) * jnp.mean(wv * hh, axis=-1, keepdims=True)

    row = lambda w: pl.BlockSpec((tm, w), lambda i: (i, 0))
    return pl.pallas_call(
        body, name=f"inproj_bwd_dx_l{l}", grid=(s // tm,),
        in_specs=[row(N_ALL), _full((N_ALL, D_MODEL)), row(D_MODEL), _full((1, D_MODEL)), row(D_MODEL)],
        out_specs=[row(D_MODEL), _full((8, D_MODEL))],
        out_shape=[jax.ShapeDtypeStruct((s, D_MODEL), F32), jax.ShapeDtypeStruct((8, D_MODEL), F32)],
        compiler_params=_params(1),
    )(dz, w_all_t, h, norm_g, dh2)


def _inproj_bwd_dw(l, u, dz, tm, tn):
    s = u.shape[0]

    def body(u_ref, dz_ref, dw_ref):
        @pl.when(pl.program_id(1) == 0)
        def _():
            dw_ref[...] = jnp.zeros_like(dw_ref)

        dw_ref[...] += _dot_tn(u_ref[...], dz_ref[...])

    return pl.pallas_call(
        body, name=f"inproj_bwd_dw_l{l}", grid=(N_ALL // tn, s // tm),
        in_specs=[pl.BlockSpec((tm, D_MODEL), lambda n, i: (i, 0)), pl.BlockSpec((tm, tn), lambda n, i: (i, n))],
        out_specs=pl.BlockSpec((D_MODEL, tn), lambda n, i: (0, n)),
        out_shape=jax.ShapeDtypeStruct((D_MODEL, N_ALL), F32),
        compiler_params=_params(2),
    )(u, dz)


def _adamw_math(w, g, m, v):
    m = ADAM_B1 * m + (1.0 - ADAM_B1) * g
    v = ADAM_B2 * v + (1.0 - ADAM_B2) * (g * g)
    m_hat = m / (1.0 - ADAM_B1 ** ADAM_STEP)
    v_hat = v / (1.0 - ADAM_B2 ** ADAM_STEP)
    delta = -ADAM_LR * (m_hat / (jnp.sqrt(v_hat) + ADAM_EPS) + ADAM_WD * w)
    return delta, m, v


def _adamw(name, w, halves, m, v, core):
    nl, r, c = w.shape
    hr = r // 2
    tr = 128 if hr % 128 == 0 else hr
    nb = hr // tr

    def body(core_ref, w_ref, own0_ref, oth0_ref, own1_ref, oth1_ref, m_ref, v_ref, g_ref, d_ref, nm_ref, nv_ref):
        first = pl.program_id(0) == 0
        own = jnp.where(first, own0_ref[...], own1_ref[...])
        oth = jnp.where(first, oth0_ref[...], oth1_ref[...])
        g = jnp.where(pl.program_id(1) // nb == core_ref[0], own, oth)
        d, nm, nv = _adamw_math(w_ref[...], g, m_ref[...], v_ref[...])
        g_ref[...] = g
        d_ref[...] = d
        nm_ref[...] = nm
        nv_ref[...] = nv

    spec = pl.BlockSpec((None, tr, c), lambda a, b, core_ref: (a, b, 0))
    gspec = pl.BlockSpec((tr, c), lambda a, b, core_ref: (b % nb, 0))
    shp = jax.ShapeDtypeStruct(w.shape, F32)
    return pl.pallas_call(
        body, name=name,
        grid_spec=pltpu.PrefetchScalarGridSpec(
            num_scalar_prefetch=1, grid=(nl, r // tr), in_specs=[spec] + [gspec] * 4 + [spec, spec],
            out_specs=[spec] * 4),
        out_shape=[shp, shp, shp, shp], compiler_params=_params(2),
    )(core, w, halves[0][0], halves[0][1], halves[1][0], halves[1][1], m, v)


def _assemble_halves(name, shape, halves, core):
    nl, r, c = shape
    hr = r // 2
    tr = 128 if hr % 128 == 0 else hr
    nb = hr // tr

    def body(core_ref, own0_ref, oth0_ref, own1_ref, oth1_ref, g_ref):
        first = pl.program_id(0) == 0
        own = jnp.where(first, own0_ref[...], own1_ref[...])
        oth = jnp.where(first, oth0_ref[...], oth1_ref[...])
        g_ref[...] = jnp.where(pl.program_id(1) // nb == core_ref[0], own, oth)

    of_layer = lambda l: pl.BlockSpec((tr, c), lambda a, b, core_ref: ((b % nb) * (a if l else 1 - a), 0))
    return pl.pallas_call(
        body, name=name,
        grid_spec=pltpu.PrefetchScalarGridSpec(
            num_scalar_prefetch=1, grid=(nl, r // tr), in_specs=[of_layer(0), of_layer(0), of_layer(1), of_layer(1)],
            out_specs=pl.BlockSpec((None, tr, c), lambda a, b, core_ref: (a, b, 0))),
        out_shape=jax.ShapeDtypeStruct(shape, F32), compiler_params=_params(2),
    )(core, halves[0][0], halves[0][1], halves[1][0], halves[1][1])


W_IN_FLAT_STEPS = 19


def _to_flat(a):
    nl, r, c = a.shape
    return jnp.transpose(jnp.transpose(a, (2, 0, 1)).reshape(c, nl, r // LANE, LANE), (0, 2, 1, 3)).reshape(-1, LANE)


def _from_flat(f, shape):
    nl, r, c = shape
    return jnp.transpose(jnp.transpose(f.reshape(c, r // LANE, nl, LANE), (0, 2, 1, 3)).reshape(c, nl, r), (1, 2, 0))


def _adamw_flat(name, w, g, m, v):
    n = w.shape[0]
    tr = n // W_IN_FLAT_STEPS

    def body(w_ref, g_ref, m_ref, v_ref, d_ref, nm_ref, nv_ref):
        d, nm, nv = _adamw_math(w_ref[...], g_ref[...], m_ref[...], v_ref[...])
        d_ref[...] = d
        nm_ref[...] = nm
        nv_ref[...] = nv

    spec = pl.BlockSpec((tr, LANE), lambda i: (i, 0))
    shp = jax.ShapeDtypeStruct(w.shape, F32)
    return pl.pallas_call(body, name=name, grid=(W_IN_FLAT_STEPS,), in_specs=[spec] * 4, out_specs=[spec] * 3,
                          out_shape=[shp, shp, shp], compiler_params=_params(1))(w, g, m, v)


def _pair_sum(name, g, x, c, narrow=False):
    n, r, cc = g.shape
    hr = r // 2
    tr = 128 if hr % 128 == 0 else hr
    nb = hr // tr

    def body(c_ref, g_ref, x_ref, o_ref, *narrow_ref):
        total = g_ref[...] + x_ref[...]
        o_ref[...] = total
        if narrow:
            narrow_ref[0][...] = total.astype(BF16)

    spec = pl.BlockSpec((None, tr, cc), lambda i, j, c_ref: (i, j, 0))
    shapes = [jax.ShapeDtypeStruct((n, hr, cc), F32)] + ([jax.ShapeDtypeStruct((n, hr, cc), BF16)] if narrow else [])
    return pl.pallas_call(
        body, name=name,
        grid_spec=pltpu.PrefetchScalarGridSpec(
            num_scalar_prefetch=1, grid=(n, nb),
            in_specs=[pl.BlockSpec((None, tr, cc), lambda i, j, c_ref: (i, c_ref[0] * nb + j, 0)), spec],
            out_specs=[spec] * len(shapes)),
        out_shape=shapes, compiler_params=_params(2),
    )(c, g, x)


def _sum_slots(name, own, landed, chip):
    n, r, c = own.shape
    tr = 128 if r % 128 == 0 else r

    def body(chip_ref, a_ref, b_ref, c_ref, d_ref, o_ref):
        o_ref[...] = ((a_ref[...] + b_ref[...].astype(F32)) + c_ref[...].astype(F32)) + d_ref[...].astype(F32)

    slot = lambda d: pl.BlockSpec((None, tr, c), lambda j, chip_ref: ((chip_ref[0] + d) % n, j, 0))
    return pl.pallas_call(
        body, name=name,
        grid_spec=pltpu.PrefetchScalarGridSpec(
            num_scalar_prefetch=1, grid=(r // tr,), in_specs=[slot(0), slot(1), slot(2), slot(3)],
            out_specs=pl.BlockSpec((tr, c), lambda j, chip_ref: (j, 0))),
        out_shape=jax.ShapeDtypeStruct((r, c), F32), compiler_params=_params(1),
    )(chip, own, landed, landed, landed)


def _me():
    return lax.axis_index("x"), lax.axis_index("y"), lax.axis_index("c")


def _other_chips(x, y):
    return [(1 - x, y), (x, 1 - y), (1 - x, 1 - y)]


def _dma_sems(*counts):
    return [pltpu.SemaphoreType.DMA((n,)) for n in counts]


def _half_rows(rows, which, align):
    return pl.ds(pl.multiple_of(which * (rows // 2), align), rows // 2)


def _gather_first_layer(w_in, others):
    n = len(others)
    rows = w_in.shape[1]

    def body(*refs):
        w_ref, o_refs = refs[0], refs[1:1 + n]
        out, keep = refs[1 + n], refs[2 + n:3 + 3 * n]
        stage, ici_send, ici_recv, d2d_send, d2d_recv, local_sem = refs[3 + 3 * n:]
        x, y, c = _me()
        k = 2 * x + y
        chips = _other_chips(x, y)
        stage[...] = w_ref[0].astype(BF16)
        local = pltpu.make_async_copy(stage, out.at[k], local_sem)
        local.start()
        mine = _half_rows(rows, c, 16)
        first, passed = [], []
        for j, (px, py) in enumerate(chips):
            cp = pltpu.make_async_remote_copy(
                src_ref=stage.at[mine], dst_ref=out.at[k, mine], send_sem=ici_send.at[j], recv_sem=ici_recv.at[j],
                device_id=(px, py, c), device_id_type=MESH)
            cp.start()
            first.append(cp)
        keep[0][...] = w_ref[1].astype(BF16)
        for t in range(n):
            for l in range(2):
                keep[1 + 2 * t + l][...] = o_refs[t][l].astype(BF16)
        for j, (px, py) in enumerate(chips):
            landed = out.at[2 * px + py, mine]
            first[j].wait_recv()
            cp = pltpu.make_async_remote_copy(
                src_ref=landed, dst_ref=landed, send_sem=d2d_send.at[j], recv_sem=d2d_recv.at[j],
                device_id=(x, y, 1 - c), device_id_type=MESH)
            cp.start()
            passed.append(cp)
        for cp in passed:
            cp.wait_recv()
        for cp in first + passed:
            cp.wait_send()
        local.wait()

    kept = [jax.ShapeDtypeStruct(w_in.shape[1:], BF16)]
    for o in others:
        kept += [jax.ShapeDtypeStruct(o.shape[1:], BF16)] * 2
    return pl.pallas_call(
        body, name="gather_first_layer",
        in_specs=[VMEM_SPEC] * (1 + n), out_specs=[ANY] + [VMEM_SPEC] * len(kept),
        out_shape=[jax.ShapeDtypeStruct((4,) + w_in.shape[1:], BF16)] + kept,
        scratch_shapes=[pltpu.VMEM(w_in.shape[1:], BF16)] + _dma_sems(3, 3, 3, 3) + [pltpu.SemaphoreType.DMA],
        compiler_params=pltpu.CompilerParams(vmem_limit_bytes=VMEM_LIMIT),
    )(w_in, *others)


def _run_comm(name, comm):
    nci, nco = len(comm.ins), len(comm.out_shapes)

    def body(*refs):
        copies = comm.make(refs[:nci], refs[nci:nci + nco], refs[nci + nco:])
        for cp in copies:
            cp.start()
        for cp in copies:
            cp.wait()

    return pl.pallas_call(body, name=name, in_specs=[ANY] * nci, out_specs=[ANY] * nco,
                          out_shape=list(comm.out_shapes), scratch_shapes=list(comm.sems))(*comm.ins)


def _gather_comm(mine):
    n = len(mine)

    def make(ins, outs, sems):
        send_sems, recv_sems, local_sems = sems
        x, y, c = _me()
        k = 2 * x + y
        copies = []
        for t in range(n):
            copies.append(pltpu.make_async_copy(ins[t], outs[t].at[k], local_sems.at[t]))
            for j, (px, py) in enumerate(_other_chips(x, y)):
                copies.append(pltpu.make_async_remote_copy(
                    src_ref=ins[t], dst_ref=outs[t].at[k], send_sem=send_sems.at[3 * t + j],
                    recv_sem=recv_sems.at[3 * t + j], device_id=(px, py, c), device_id_type=MESH))
        return copies

    return _Comm(mine, [jax.ShapeDtypeStruct((4,) + a.shape, a.dtype) for a in mine], _dma_sems(3 * n, 3 * n, n), make)


def _swap_comm(grads):
    n = len(grads)

    def make(ins, outs, sems):
        send_sems, recv_sems = sems
        x, y, c = _me()
        return [pltpu.make_async_remote_copy(
            src_ref=ins[t].at[:, _half_rows(grads[t].shape[1], 1 - c, 8)], dst_ref=outs[t],
            send_sem=send_sems.at[t], recv_sem=recv_sems.at[t], device_id=(x, y, 1 - c), device_id_type=MESH)
            for t in range(n)]

    shapes = [jax.ShapeDtypeStruct((g.shape[0], g.shape[1] // 2, g.shape[2]), F32) for g in grads]
    return _Comm(grads, shapes, _dma_sems(n, n), make)


def _scatter_comm(parts):
    n = len(parts)

    def make(ins, outs, sems):
        send_sems, recv_sems = sems
        x, y, c = _me()
        k = 2 * x + y
        return [pltpu.make_async_remote_copy(
            src_ref=ins[t].at[2 * px + py], dst_ref=outs[t].at[k], send_sem=send_sems.at[3 * t + j],
            recv_sem=recv_sems.at[3 * t + j], device_id=(px, py, c), device_id_type=MESH)
            for t in range(n) for j, (px, py) in enumerate(_other_chips(x, y))]

    return _Comm(parts, [jax.ShapeDtypeStruct(p.shape, p.dtype) for p in parts], _dma_sems(3 * n, 3 * n), make)


def _share_comm(totals):
    n = len(totals)

    def make(ins, outs, sems):
        send_sems, recv_sems = sems
        x, y, c = _me()
        return [pltpu.make_async_remote_copy(
            src_ref=ins[t], dst_ref=outs[t], send_sem=send_sems.at[t], recv_sem=recv_sems.at[t],
            device_id=(x, y, 1 - c), device_id_type=MESH) for t in range(n)]

    return _Comm(totals, [jax.ShapeDtypeStruct(t.shape, F32) for t in totals], _dma_sems(n, n), make)


def _small_allreduce_adamw(part, w, m, v):
    shape = part.shape

    def body(part_ref, w_ref, m_ref, v_ref, g_ref, d_ref, nm_ref, nv_ref, slots, send_sems, recv_sems):
        x, y, c = _me()
        me = 4 * x + 2 * y + c
        slots[me] = part_ref[...]
        copies = []
        for d in range(1, 8):
            peer = (x ^ (d >> 2), y ^ ((d >> 1) & 1), c ^ (d & 1))
            cp = pltpu.make_async_remote_copy(
                src_ref=part_ref, dst_ref=slots.at[me], send_sem=send_sems.at[d - 1], recv_sem=recv_sems.at[d - 1],
                device_id=peer, device_id_type=MESH)
            cp.start()
            copies.append(cp)
        for cp in copies:
            cp.wait()
        g = slots[0]
        for i in range(1, 8):
            g = g + slots[i]
        g_ref[...] = g
        d, nm, nv = _adamw_math(w_ref[...], g, m_ref[...], v_ref[...])
        d_ref[...] = d
        nm_ref[...] = nm
        nv_ref[...] = nv

    shp = jax.ShapeDtypeStruct(shape, F32)
    return pl.pallas_call(
        body, name="small_allreduce_adamw", in_specs=[VMEM_SPEC] * 4, out_specs=[VMEM_SPEC] * 4,
        out_shape=[shp, shp, shp, shp],
        scratch_shapes=[pltpu.VMEM((8,) + shape, F32), pltpu.SemaphoreType.DMA((7,)), pltpu.SemaphoreType.DMA((7,))],
    )(part, w, m, v)


TM = 512
T_FOX = 1024
T_DIL_FWD = 1024
T_DIL_BWD = 512
TN_DW = 1408


def _layer_fwd(l, h, p_i, w_inside, w_outside, consts, comm=None, target=None):
    w_all, _, norm_g, b_f, qkg = w_inside
    bsum, _, bias_t, _, rope = consts
    u, z, qa, ka, va, qb, kb, vb, c_spread, c_t, va_t, vb_t = _inproj(l, h, norm_g, w_all, b_f, qkg, bsum, rope, TM)
    oa, lse_a, *landed = _attn_fwd(f"fox_fwd_l{l}", True, qa, ka, va_t, (c_spread, c_t), T_FOX, comm)
    ob, lse_b = _attn_fwd(f"dil_fwd_l{l}", False, qb, kb, vb_t, (bias_t,), T_DIL_FWD)
    if comm is not None:
        w_outside = w_outside(landed)
    w_out, _, w_gate, _, w_ple, ple_g = w_outside
    h2, e, gate, *out = _outproj(l, h, oa, ob, z, p_i, w_out, ple_g, w_gate, w_ple, TM, target)
    out = out[0] if target is None else tuple(out)
    saved = (h, u, z, qa, ka, va, qb, kb, vb, c_spread, c_t, oa, lse_a, ob, lse_b, h2, e, gate)
    return out, saved, w_outside


def _reduce_names(tag):
    return [f"reduce_{tag}_{w}" for w in ("w_in", "w_out", "w_ple", "w_gate")]


def _layer_bwd(l, dout, p_i, wts, consts, saved, pending=None, core=None, chip=None):
    (_, w_all_t, norm_g, b_f, qkg), (_, w_out_t, _, w_gate_t, _, ple_g) = wts
    bsum, hsel, _, bias_t, rope = consts
    h, u, z, qa, ka, va, qb, kb, vb, c_spread, c_t, oa, lse_a, ob, lse_b, h2, e, gate = saved
    fused = pending is not None
    dh2, doa, dob, dga, dgb, delta_t, dw_out, dw_gate, dw_ple, dple_g, *sib = _outproj_bwd(
        l, dout, h2, e, gate, p_i, oa, ob, z, w_out_t, ple_g, w_gate_t, hsel, TM,
        _swap_comm(pending) if fused else None)
    if fused:
        pair = [_pair_sum(n, g, x, core)[0] for n, g, x in zip(_reduce_names(f"pair_l{l + 1}"), pending, sib)]
    dqt_a, dk_a, dv_a, dc, drow, *landed = _attn_bwd(
        f"fox_bwd_l{l}", True, qa, ka, va, doa, lse_a, delta_t, 0, (c_spread, c_t), T_FOX,
        _scatter_comm(pair) if fused else None)
    if fused:
        totals = [_sum_slots(n, a, y, chip) for n, a, y in zip(_reduce_names(f"chips_l{l + 1}"), pair, landed)]
    dqt_b, dk_b, dv_b, *other = _attn_bwd(f"dil_bwd_l{l}", False, qb, kb, vb, dob, lse_b, delta_t, N_PAIRS,
                                            (bias_t,), T_DIL_BWD, _share_comm(totals) if fused else None)
    dz, dqkg, dbf = _inproj_bwd_prep(l, z, dqt_a, dk_a, dv_a, dqt_b, dk_b, dv_b, dga, dgb, dc, drow, b_f, qkg,
                                     bsum, rope, TM)
    dh, dnorm_g = _inproj_bwd_dx(l, dz, w_all_t, h, norm_g, dh2, TM)
    dw_all = _inproj_bwd_dw(l, u, dz, TM, TN_DW)
    reduced = list(zip(totals, other)) if fused else None
    return dh, (dw_all, dw_out, dw_ple, dw_gate, dnorm_g[0], dbf[0, :N_HEADS], dqkg[:4], dple_g[0]), reduced


def _reduce_last(grads, core, chip, l):
    sib = _run_comm(f"reduce_swap_l{l}", _swap_comm(grads))
    pair = [_pair_sum(n, g, x, core, narrow=True) for n, g, x in zip(_reduce_names(f"pair_l{l}"), grads, sib)]
    landed = _run_comm(f"reduce_scatter_l{l}", _scatter_comm([p[1] for p in pair]))
    totals = [_sum_slots(n, p[0], y, chip) for n, p, y in zip(_reduce_names(f"chips_l{l}"), pair, landed)]
    return list(zip(totals, _run_comm(f"reduce_share_l{l}", _share_comm(totals))))


N_FA = 2048


W_SHARD = N_IN // 4


def _shard_pieces(lo, hi):
    cuts = [(k, max(lo, k * W_SHARD), min(hi, (k + 1) * W_SHARD)) for k in range(4)]
    return [(k, a - k * W_SHARD, b - k * W_SHARD) for k, a, b in cuts if a < b]


def _in_weights(l, g_in, norm_g, b_f, qk_norm_g):
    order = _shard_pieces(0, N_FA) + _shard_pieces(N_FA + N_HEADS, N_IN) + _shard_pieces(N_FA, N_FA + N_HEADS)
    w_all = jnp.concatenate([g_in[k, :, a:b] for k, a, b in order]
                            + [jnp.zeros((D_MODEL, LANE - N_HEADS), g_in.dtype)], axis=1)
    qkg = jnp.pad(jnp.tile(qk_norm_g[l], (1, N_HEADS)), ((0, 4), (0, 0)))
    bf = jnp.pad(b_f[l], (0, LANE - N_HEADS))[None, :]
    return w_all, w_all.T, norm_g[l][None, :], bf, qkg


def _out_weights(l, g_out, g_ple, g_gate, ple_norm_g):
    w_out = g_out.reshape(D_MODEL, D_MODEL)
    w_gate = g_gate.reshape(D_MODEL, D_MODEL)
    w_ple = jnp.transpose(g_ple, (1, 0, 2)).reshape(PLE_DIM, D_MODEL)
    return w_out, w_out.T, w_gate, w_gate.T, w_ple, ple_norm_g[l][None, :]


def _slot_layout(dw_all, dw_out, dw_ple, dw_gate):
    regions = ((0, N_FA, 0), (N_FA, N_FA + N_HEADS, N_MAIN - N_FA), (N_FA + N_HEADS, N_IN, -N_HEADS))

    def shard(k):
        cuts = [(max(k * W_SHARD, a) + shift, min((k + 1) * W_SHARD, b) + shift) for a, b, shift in regions]
        return jnp.concatenate([dw_all[:, a:b] for a, b in cuts if a < b], axis=1)

    return (jnp.stack([shard(k) for k in range(4)]),
            dw_out.reshape(4, D_MODEL // 4, D_MODEL),
            jnp.transpose(dw_ple.reshape(PLE_DIM, 4, D_MODEL // 4), (1, 0, 2)),
            dw_gate.reshape(4, D_MODEL // 4, D_MODEL))


SMALL_ROWS = 40


def _pack_small(norm_g, ple_norm_g, qk_norm_g, b_f, last=0.0):
    flat = jnp.concatenate([norm_g.reshape(-1), ple_norm_g.reshape(-1), qk_norm_g.reshape(-1), b_f.reshape(-1)])
    flat = jnp.pad(flat, (0, SMALL_ROWS * LANE - flat.shape[0] - 1))
    return jnp.concatenate([flat, jnp.reshape(last, (1,)).astype(F32)]).reshape(SMALL_ROWS, LANE)


def _unpack_small(packed):
    flat = packed.reshape(-1)
    n1, n2, n3 = 2 * D_MODEL, 4 * D_MODEL, 4 * D_MODEL + 2 * 4 * HEAD_DIM
    return (flat[:n1].reshape(2, D_MODEL), flat[n1:n2].reshape(2, D_MODEL), flat[n2:n3].reshape(2, 4, HEAD_DIM),
            flat[n3:n3 + 2 * N_HEADS].reshape(2, N_HEADS))


def kernel(x, p, positions, norm_g, w_in, b_f, qk_norm_g, w_out, w_ple, ple_norm_g, w_ple_gate, loss_target,
           m_norm_g, m_w_in, m_b_f, m_qk_norm_g, m_w_out, m_w_ple, m_ple_norm_g, m_w_ple_gate,
           v_norm_g, v_w_in, v_b_f, v_qk_norm_g, v_w_out, v_w_ple, v_ple_norm_g, v_w_ple_gate):
    assert w_in.shape[0] == 2, "the schedule below is written for two layers"
    w_in0, *kept = _gather_first_layer(w_in, [w_out, w_ple, w_ple_gate])
    consts = (_head_block_diag(), _head_select(), _dil_bias(T_DIL_FWD), _dil_bias(T_DIL_BWD),
              _rope_tables(positions[0]))
    later = {}

    def outside0(landed):
        later["w_in1"] = landed[0]
        later["out1"] = landed[2::2]
        return _out_weights(0, *landed[1::2], ple_norm_g)

    inside0 = _in_weights(0, w_in0, norm_g, b_f, qk_norm_g)
    h1, saved0, outside0 = _layer_fwd(0, x[0], p[0, 0], inside0, outside0, consts, _gather_comm(kept))
    wts0 = (inside0, outside0)
    wts1 = (_in_weights(1, later["w_in1"], norm_g, b_f, qk_norm_g), _out_weights(1, *later["out1"], ple_norm_g))
    (dh, sq), saved1, _ = _layer_fwd(1, h1, p[1, 0], *wts1, consts, target=loss_target[0])

    core = lax.axis_index("c").astype(jnp.int32).reshape(1)
    chip = (2 * lax.axis_index("x") + lax.axis_index("y")).astype(jnp.int32).reshape(1)
    dh, grads1, _ = _layer_bwd(1, dh, p[1, 0], wts1, consts, saved1)
    dh, grads0, reduced1 = _layer_bwd(0, dh, p[0, 0], wts0, consts, saved0, _slot_layout(*grads1[:4]), core, chip)
    reduced0 = _reduce_last(_slot_layout(*grads0[:4]), core, chip, 0)
    grad_x = dh[None]
    small = [grads0[4:], grads1[4:]]
    n_layers = 2

    outs = {}
    for t, (name, w, m, v) in enumerate((("w_in", w_in, m_w_in, v_w_in), ("w_out", w_out, m_w_out, v_w_out),
                                         ("w_ple", w_ple, m_w_ple, v_w_ple),
                                         ("w_ple_gate", w_ple_gate, m_w_ple_gate, v_w_ple_gate))):
        if name == "w_in":
            g = _assemble_halves("assemble_w_in", w.shape, (reduced0[t], reduced1[t]), core)
            g_flat = _to_flat(g)
            flat = _adamw_flat("adamw_w_in", _to_flat(w), g_flat, _to_flat(m), _to_flat(v))
            outs[name] = tuple(_from_flat(f, w.shape) for f in (g_flat,) + tuple(flat))
        else:
            outs[name] = tuple(_adamw(f"adamw_{name}", w, (reduced0[t], reduced1[t]), m, v, core))

    part = _pack_small(jnp.stack([s[0] for s in small]), jnp.stack([s[3] for s in small]),
                       jnp.stack([s[2] for s in small]).reshape(n_layers, 4, N_HEADS, HEAD_DIM).sum(axis=2),
                       jnp.stack([s[1] for s in small]), 0.5 / D_MODEL * jnp.sum(sq))
    packed = _small_allreduce_adamw(part, _pack_small(norm_g, ple_norm_g, qk_norm_g, b_f),
                                    _pack_small(m_norm_g, m_ple_norm_g, m_qk_norm_g, m_b_f),
                                    _pack_small(v_norm_g, v_ple_norm_g, v_qk_norm_g, v_b_f))
    loss = packed[0][SMALL_ROWS - 1, LANE - 1]
    sm = [_unpack_small(a) for a in packed]
    for i, name in enumerate(("norm_g", "ple_norm_g", "qk_norm_g", "b_f")):
        outs[name] = tuple(sm[j][i] for j in range(4))

    order = ("norm_g", "w_in", "b_f", "qk_norm_g", "w_out", "w_ple", "ple_norm_g", "w_ple_gate")
    return (loss, grad_x) + tuple(outs[n][j] for j in range(4) for n in order)
```

```python
import functools
from typing import Any, Callable, NamedTuple, Sequence

import numpy as np
import jax
import jax.numpy as jnp
from jax import lax
from jax.experimental import pallas as pl
from jax.experimental.pallas import tpu as pltpu

F32 = jnp.float32
BF16 = jnp.bfloat16
MESH = pl.DeviceIdType.MESH

D_MODEL = 1024
HEAD_DIM = 64
D_BRANCH = 512
N_HEADS = 8
N_PAIRS = 4
N_IN = 4104
N_MAIN = 4096
N_ALL = 4224
PLE_DIM = 256
ROPE_THETA = 500000.0
ROPE_HALF = 8
EPS = 1e-6
NEG = -1e30
M_INIT = -1e29
Q_SCALE = HEAD_DIM ** -0.5
LOG2E = 1.4426950408889634
LN2 = 0.6931471805599453
DIL_PATTERNS = ((128, 1), (512, 4), (2048, 16))
DIL_BACK = 2048
ADAM_LR, ADAM_B1, ADAM_B2, ADAM_EPS, ADAM_WD, ADAM_STEP = 0.001, 0.9, 0.999, 1e-08, 0.01, 10
VMEM_LIMIT = 56 * 1024 * 1024
LANE = 128


def _dot(a, b):
    return jnp.dot(a, b, preferred_element_type=F32)


def _dot_nt(a, b):
    return lax.dot_general(a, b, (((1,), (1,)), ((), ())), preferred_element_type=F32)


def _dot_tn(a, b):
    return lax.dot_general(a, b, (((0,), (0,)), ((), ())), preferred_element_type=F32)


def _split_dot(x, w):
    hi = x.astype(BF16)
    lo = (x - hi.astype(F32)).astype(BF16)
    return _dot(hi, w) + _dot(lo, w)


def _head_sums(x, bs):
    w = bs.shape[0]
    return jnp.concatenate([_split_dot(x[:, w * k:w * (k + 1)], bs) for k in range(x.shape[1] // w)], axis=1)


def _split3_dot(w, x):
    hi = x.astype(BF16)
    r1 = x - hi.astype(F32)
    mid = r1.astype(BF16)
    lo = (r1 - mid.astype(F32)).astype(BF16)
    return _dot(w, hi) + _dot(w, mid) + _dot(w, lo)


def _sigmoid(x):
    return 1.0 / (1.0 + jnp.exp(-x))


def _params(n_grid):
    return pltpu.CompilerParams(dimension_semantics=("arbitrary",) * n_grid,
                                vmem_limit_bytes=VMEM_LIMIT)


def _full(shape):
    nd = len(shape)
    return pl.BlockSpec(shape, lambda *_: (0,) * nd)


ANY = pl.BlockSpec(memory_space=pl.ANY)
VMEM_SPEC = pl.BlockSpec(memory_space=pltpu.VMEM)


class _Comm(NamedTuple):
    ins: Sequence[Any]
    out_shapes: Sequence[Any]
    sems: Sequence[Any]
    make: Callable[..., Any]


def _fuse_comm(body, n_in, n_out, comm, grid):
    if comm is None:
        return body
    nci, nco, ncs = len(comm.ins), len(comm.out_shapes), len(comm.sems)

    def fused(*refs):
        a, b = n_in + nci, n_in + nci + n_out
        ins, cins, outs, couts = refs[:n_in], refs[n_in:a], refs[a:b], refs[b:b + nco]
        scratch, sems = refs[b + nco:len(refs) - ncs], refs[len(refs) - ncs:]
        first = functools.reduce(jnp.logical_and, [pl.program_id(d) == 0 for d in range(len(grid))])
        last = functools.reduce(jnp.logical_and, [pl.program_id(d) == n - 1 for d, n in enumerate(grid)])

        @pl.when(first)
        def _():
            for cp in comm.make(cins, couts, sems):
                cp.start()

        body(*ins, *outs, *scratch)

        @pl.when(last)
        def _():
            for cp in comm.make(cins, couts, sems):
                cp.wait()

    return fused


def _comm_args(comm):
    if comm is None:
        return [], [], [], [], []
    return (list(comm.ins), [ANY] * len(comm.ins), [ANY] * len(comm.out_shapes), list(comm.out_shapes),
            list(comm.sems))


HEADS_PER_BLOCK = 4


def _head_block_diag():
    i = np.arange(HEADS_PER_BLOCK * HEAD_DIM)
    return jnp.asarray((i[:, None] // HEAD_DIM == i[None, :] // HEAD_DIM).astype(np.float32), BF16)


def _head_select():
    i = np.arange(2 * D_BRANCH)
    j = np.arange(LANE)
    return jnp.asarray((i[:, None] // HEAD_DIM == j[None, :]).astype(np.float32), BF16)


def _dil_bias(t):
    nb = DIL_BACK // t + 1
    qi = np.arange(t)[:, None]
    ki = np.arange(t)[None, :]
    tiles = []
    for r in range(nb):
        d = r * t + qi - ki
        mult = np.zeros((t, t), np.int64)
        for window, dil in DIL_PATTERNS:
            mult += ((d >= 0) & (d <= window) & (d % dil == 0)).astype(np.int64)
        b = np.where(mult > 0, np.log2(np.maximum(mult, 1)), NEG).astype(np.float32)
        tiles.append(b.T)
    return jnp.asarray(np.stack(tiles))


def _rope_tables(positions):
    inv_freq = ROPE_THETA ** (-jnp.arange(ROPE_HALF, dtype=F32) / ROPE_HALF)
    ang = positions.astype(F32)[:, None] * inv_freq
    cos, sin = jnp.cos(ang), jnp.sin(ang)
    s = positions.shape[0]
    rest = HEAD_DIM - 2 * ROPE_HALF
    one, zero, zero8 = jnp.ones((s, rest), F32), jnp.zeros((s, rest), F32), jnp.zeros((s, ROPE_HALF), F32)
    c = jnp.concatenate([cos, cos, one], axis=1)
    s1 = jnp.concatenate([zero8, sin, zero], axis=1)
    s2 = jnp.concatenate([-sin, zero8, zero], axis=1)
    return tuple(jnp.tile(t, (1, 2)) for t in (c, s1, s2))


def _rope_fwd(x, c, s1, s2):
    return x * c + pltpu.roll(x, ROPE_HALF, 1) * s1 + pltpu.roll(x, LANE - ROPE_HALF, 1) * s2


def _rope_bwd(dy, c, s1, s2):
    return dy * c + pltpu.roll(dy * s1, LANE - ROPE_HALF, 1) + pltpu.roll(dy * s2, ROPE_HALF, 1)


def _log_sigmoid(x):
    return jnp.minimum(x, 0.0) - jnp.log(1.0 + jnp.exp(-jnp.abs(x)))


def _inproj(l, h, norm_g, w_all, b_f, qkg, bsum, rope, tm):
    s = h.shape[0]
    rc, rs1, rs2 = rope

    def body(h_ref, g_ref, w_ref, bf_ref, qkg_ref, bsum_ref, rc_ref, rs1_ref, rs2_ref,
             u_ref, z_ref, qa_ref, ka_ref, va_ref, qb_ref, kb_ref, vb_ref, cs_ref, ct_ref, vat_ref, vbt_ref, carry):
        @pl.when(pl.program_id(0) == 0)
        def _():
            carry[...] = jnp.zeros_like(carry)

        hh = h_ref[...]
        r = lax.rsqrt(jnp.mean(hh * hh, axis=-1, keepdims=True) + EPS)
        u = (hh * r * g_ref[...]).astype(BF16)
        u_ref[...] = u
        for k in range(N_ALL // LANE // 3):
            cols = slice(3 * LANE * k, 3 * LANE * (k + 1))
            z_ref[:, cols] = _dot(u, w_ref[:, cols])
        bs = bsum_ref[...]

        def head_norm(x, row):
            ms = _head_sums(x * x, bs) * (1.0 / HEAD_DIM)
            return x * lax.rsqrt(ms + EPS) * qkg_ref[row:row + 1, :]

        def seg(k):
            return z_ref[:, D_BRANCH * k:D_BRANCH * (k + 1)]

        qa_ref[...] = (head_norm(seg(0), 0) * (Q_SCALE * LOG2E)).astype(BF16)
        ka_ref[...] = head_norm(seg(1), 1).astype(BF16)
        va_ref[...] = seg(2).astype(BF16)
        vat_ref[...] = seg(2).T.astype(BF16)
        qn = head_norm(seg(4), 2) * (Q_SCALE * LOG2E)
        kn = head_norm(seg(5), 3)
        c, s1, s2 = rc_ref[...], rs1_ref[...], rs2_ref[...]
        for k in range(D_BRANCH // LANE):
            cols = slice(LANE * k, LANE * (k + 1))
            qb_ref[:, cols] = _rope_fwd(qn[:, cols], c, s1, s2).astype(BF16)
            kb_ref[:, cols] = _rope_fwd(kn[:, cols], c, s1, s2).astype(BF16)
        vb_ref[...] = seg(6).astype(BF16)
        vbt_ref[...] = seg(6).T.astype(BF16)
        logf = _log_sigmoid(z_ref[:, N_MAIN:N_ALL] + bf_ref[...])
        csum = _split3_dot(_tri(tm, False), logf) + carry[0:1, :]
        carry[...] = jnp.broadcast_to(csum[tm - 1:tm, :], carry.shape)
        csum = csum * LOG2E
        ct = csum.T
        for p in range(N_PAIRS):
            cs_ref[:, LANE * p:LANE * (p + 1)] = csum if p == 0 else pltpu.roll(csum, LANE - 2 * p, 1)
            ct_ref[p, :, :] = ct[2 * p:2 * p + 2, :]

    row = lambda w: pl.BlockSpec((tm, w), lambda i: (i, 0))
    colt = pl.BlockSpec((D_BRANCH, tm), lambda i: (0, i))
    bf = lambda: jax.ShapeDtypeStruct((s, D_BRANCH), BF16)
    bft = lambda: jax.ShapeDtypeStruct((D_BRANCH, s), BF16)
    return pl.pallas_call(
        body, name=f"inproj_l{l}", grid=(s // tm,),
        in_specs=[row(D_MODEL), _full((1, D_MODEL)), _full((D_MODEL, N_ALL)), _full((1, LANE)),
                  _full((8, D_BRANCH)), _full((HEADS_PER_BLOCK * HEAD_DIM,) * 2), row(LANE), row(LANE), row(LANE)],
        out_specs=[row(D_MODEL), row(N_ALL)] + [row(D_BRANCH)] * 6
        + [row(N_PAIRS * LANE), pl.BlockSpec((N_PAIRS, 2, tm), lambda i: (0, 0, i)), colt, colt],
        out_shape=[jax.ShapeDtypeStruct((s, D_MODEL), BF16), jax.ShapeDtypeStruct((s, N_ALL), F32),
                   bf(), bf(), bf(), bf(), bf(), bf(), jax.ShapeDtypeStruct((s, N_PAIRS * LANE), F32),
                   jax.ShapeDtypeStruct((N_PAIRS, 2, s), F32), bft(), bft()],
        scratch_shapes=[pltpu.VMEM((8, LANE), F32)],
        compiler_params=_params(1),
    )(h, norm_g, w_all, b_f, qkg, bsum, rc, rs1, rs2)


def _tri(t, upper):
    a = lax.broadcasted_iota(jnp.int32, (t, t), 0)
    b = lax.broadcasted_iota(jnp.int32, (t, t), 1)
    return jnp.where((b >= a) if upper else (b <= a), 1.0, 0.0).astype(BF16)


def _attn_fwd(name, fox, q, k, vt, extra, t, comm=None):
    s = q.shape[0]
    nq = s // t
    nb = DIL_BACK // t + 1

    def body(*refs):
        if fox:
            q_ref, k_ref, vt_ref, ccol_ref, crow_ref, o_ref, lse_ref, m_scr, l_scr, acc_scr = refs
        else:
            q_ref, k_ref, vt_ref, bias_ref, o_ref, lse_ref, m_scr, l_scr, acc_scr = refs
        i = pl.program_id(1)
        lane = lax.broadcasted_iota(jnp.int32, (t, LANE), 1)
        first = lane < HEAD_DIM
        qq = q_ref[...]
        zero = jnp.zeros_like(qq)
        qh = (jnp.where(first, qq, zero), jnp.where(first, zero, qq))
        m_scr[...] = jnp.full(m_scr.shape, M_INIT, F32)
        l_scr[...] = jnp.zeros_like(l_scr)
        acc_scr[...] = jnp.zeros_like(acc_scr)
        ones = jnp.ones((16, t), BF16)

        half = t // 2
        whole, lo, hi = slice(0, t), slice(0, half), slice(half, t)

        def block(j, ksl, qsl, causal):
            nk_, nq_ = ksl.stop - ksl.start, qsl.stop - qsl.start
            rows = pl.ds(pl.multiple_of(j * t + ksl.start, LANE), nk_)
            ks = k_ref[rows, :]
            vts = jnp.concatenate([vt_ref[:, rows], ones[:, :nk_]], axis=0)
            if fox:
                ccol = ccol_ref[rows, :]
            for h in range(2):
                st = _dot_nt(ks, qh[h][qsl, :])
                if fox:
                    st = st + (crow_ref[h:h + 1, qsl] - ccol[:, h:h + 1])
                    if causal:
                        ki = lax.broadcasted_iota(jnp.int32, (nk_, nq_), 0) + ksl.start
                        qi = lax.broadcasted_iota(jnp.int32, (nk_, nq_), 1) + qsl.start
                        st = jnp.where(ki <= qi, st, NEG)
                else:
                    st = st + bias_ref[i - j, ksl, qsl]
                m_old = m_scr[h, :, qsl]
                m_new = jnp.maximum(m_old, jnp.max(st, axis=0, keepdims=True))
                alpha = jnp.exp2(m_old - m_new)
                pb = jnp.exp2(st - m_new).astype(BF16)
                pv = _dot(vts, pb)
                l_scr[h, :, qsl] = alpha * l_scr[h, :, qsl] + pv[LANE:LANE + 1, :]
                acc_scr[h, :, qsl] = alpha * acc_scr[h, :, qsl] + pv[:LANE, :]
                m_scr[h, :, qsl] = m_new

        def full(j, c):
            block(j, whole, whole, False)
            return c

        if fox:
            lax.fori_loop(0, i, full, 0)
        else:
            @pl.when(i >= nb - 1)
            def _():
                block(i - (nb - 1), lo, lo, False)
                block(i - (nb - 1), hi, whole, False)

            lax.fori_loop(jnp.maximum(i - (nb - 2), 0), i, full, 0)
        if fox:
            block(i, whole, whole, True)
        else:
            block(i, lo, lo, False)
            block(i, whole, hi, False)

        sub = lax.broadcasted_iota(jnp.int32, (LANE, t), 0)
        ot = jnp.where(sub < HEAD_DIM, acc_scr[0] / l_scr[0], acc_scr[1] / l_scr[1])
        o_ref[...] = ot.T
        for h in range(2):
            lse_ref[h:h + 1, :] = m_scr[h] + jnp.log2(l_scr[h])

    qspec = pl.BlockSpec((t, LANE), lambda hp, i: (i, hp))
    kspec = pl.BlockSpec((s, LANE), lambda hp, i: (0, hp))
    vtspec = pl.BlockSpec((LANE, s), lambda hp, i: (hp, 0))
    in_specs = [qspec, kspec, vtspec]
    if fox:
        in_specs += [kspec, pl.BlockSpec((None, 2, t), lambda hp, i: (hp, 0, i))]
    else:
        in_specs += [_full((nb, t, t))]
    grid = (N_PAIRS, nq)
    c_in, c_ispec, c_ospec, c_oshape, c_scr = _comm_args(comm)
    return pl.pallas_call(
        _fuse_comm(body, len(in_specs), 2, comm, grid), name=name, grid=grid,
        in_specs=in_specs + c_ispec,
        out_specs=[qspec, pl.BlockSpec((None, 2, t), lambda hp, i: (hp, 0, i))] + c_ospec,
        out_shape=[jax.ShapeDtypeStruct((s, D_BRANCH), F32), jax.ShapeDtypeStruct((N_PAIRS, 2, s), F32)] + c_oshape,
        scratch_shapes=[pltpu.VMEM((2, 1, t), F32), pltpu.VMEM((2, 1, t), F32), pltpu.VMEM((2, LANE, t), F32)]
        + c_scr,
        compiler_params=_params(2),
    )(q, k, vt, *extra, *c_in)


def _attn_bwd(name, fox, q, k, v, do, lse_t, delta_t, pair_offset, extra, t, comm=None):
    s = q.shape[0]
    nk = s // t
    nb = DIL_BACK // t + 1

    def body(*refs):
        if fox:
            (q_ref, k_ref, v_ref, do_ref, lse_ref, delta_ref, ccol_ref, crow_ref,
             dqt_ref, dk_ref, dv_ref, dc_ref, drow_ref) = refs
        else:
            q_ref, k_ref, v_ref, do_ref, lse_ref, delta_ref, bias_ref, dqt_ref, dk_ref, dv_ref = refs
        j = pl.program_id(1)

        @pl.when(j == 0)
        def _():
            dqt_ref[...] = jnp.zeros_like(dqt_ref)
            if fox:
                drow_ref[...] = jnp.zeros_like(drow_ref)

        lane = lax.broadcasted_iota(jnp.int32, (t, LANE), 1)
        first = lane < HEAD_DIM
        ks = k_ref[...]
        vs = v_ref[...]
        kt = ks.astype(F32).T
        sub = lax.broadcasted_iota(jnp.int32, (LANE, t), 0)
        kth = (jnp.where(sub < HEAD_DIM, kt, 0.0).astype(BF16), jnp.where(sub < HEAD_DIM, 0.0, kt).astype(BF16))
        dk_ref[...] = jnp.zeros_like(dk_ref)
        dv_ref[...] = jnp.zeros_like(dv_ref)
        if fox:
            dc_ref[...] = jnp.zeros_like(dc_ref)
            ccol = ccol_ref[...]

        half = t // 2
        whole, lo, hi = slice(0, t), slice(0, half), slice(half, t)

        def block(i, ksl, qsl, causal):
            nk_, nq_ = ksl.stop - ksl.start, qsl.stop - qsl.start
            rows = pl.ds(pl.multiple_of(i * t + qsl.start, LANE), nq_)
            qq = q_ref[rows, :]
            dd = do_ref[rows, :]
            zero = jnp.zeros_like(qq)
            qh = (jnp.where(first[:nq_], qq, zero), jnp.where(first[:nq_], zero, qq))
            dh = (jnp.where(first[:nq_], dd, zero), jnp.where(first[:nq_], zero, dd))
            for h in range(2):
                st = _dot_nt(ks[ksl, :], qh[h])
                if fox:
                    st = st + (crow_ref[h:h + 1, rows] - ccol[ksl, h:h + 1])
                    if causal:
                        ki = lax.broadcasted_iota(jnp.int32, (nk_, nq_), 0) + ksl.start
                        qi = lax.broadcasted_iota(jnp.int32, (nk_, nq_), 1) + qsl.start
                        st = jnp.where(ki <= qi, st, NEG)
                else:
                    st = st + bias_ref[i - j, ksl, qsl]
                pt = jnp.exp2(st - lse_ref[h:h + 1, rows])
                dpt = _dot_nt(vs[ksl, :], dh[h])
                dst = pt * (dpt - delta_ref[h:h + 1, rows])
                dv_ref[ksl, :] += _dot(pt.astype(BF16), dh[h])
                dsb = dst.astype(BF16)
                dk_ref[ksl, :] += _dot(dsb, qh[h])
                dqt_ref[:, rows] += _dot(kth[h][:, ksl], dsb)
                if fox:
                    dc_ref[ksl, :] -= jnp.where(lane[:nk_] == h, jnp.sum(dst, axis=1, keepdims=True), 0.0)
                    drow_ref[h:h + 1, rows] += jnp.sum(dst, axis=0, keepdims=True)

        def full(i, c):
            block(i, whole, whole, False)
            return c

        block(j, lo, whole, True)
        block(j, hi, hi, True)
        if fox:
            lax.fori_loop(j + 1, nk, full, 0)
        else:
            lax.fori_loop(j + 1, jnp.minimum(j + nb - 1, nk), full, 0)

            @pl.when(j + nb - 1 < nk)
            def _():
                block(j + nb - 1, lo, lo, False)
                block(j + nb - 1, hi, whole, False)

    kspec = pl.BlockSpec((t, LANE), lambda hp, j: (j, hp))
    qspec = pl.BlockSpec((s, LANE), lambda hp, j: (0, hp))
    rowspec = pl.BlockSpec((None, 2, s), lambda hp, j: (hp, 0, 0))
    drowspec = pl.BlockSpec((None, 2, s), lambda hp, j: (hp + pair_offset, 0, 0))
    in_specs = [qspec, kspec, kspec, qspec, rowspec, drowspec]
    out_specs = [pl.BlockSpec((LANE, s), lambda hp, j: (hp, 0)), kspec, kspec]
    out_shape = [jax.ShapeDtypeStruct((D_BRANCH, s), F32), jax.ShapeDtypeStruct((s, D_BRANCH), F32),
                 jax.ShapeDtypeStruct((s, D_BRANCH), F32)]
    if fox:
        in_specs += [kspec, rowspec]
        out_specs += [kspec, rowspec]
        out_shape += [jax.ShapeDtypeStruct((s, N_PAIRS * LANE), F32), jax.ShapeDtypeStruct((N_PAIRS, 2, s), F32)]
    else:
        in_specs += [_full((nb, t, t))]
    grid = (N_PAIRS, nk)
    c_in, c_ispec, c_ospec, c_oshape, c_scr = _comm_args(comm)
    return pl.pallas_call(
        _fuse_comm(body, len(in_specs), len(out_specs), comm, grid), name=name, grid=grid,
        in_specs=in_specs + c_ispec, out_specs=out_specs + c_ospec, out_shape=out_shape + c_oshape,
        scratch_shapes=c_scr, compiler_params=_params(2),
    )(q, k, v, do, lse_t, delta_t, *extra, *c_in)


def _silu(x):
    return x * _sigmoid(x)


def _outproj(l, h, oa, ob, z, p_i, w_out, ple_g, w_gate, w_ple, tm, target=None):
    s = h.shape[0]
    last = target is not None

    def body(h_ref, oa_ref, ob_ref, ga_ref, gb_ref, p_ref, wo_ref, pg_ref, wg_ref, wp_ref, *rest):
        if last:
            t_ref, h2_ref, e_ref, gate_ref, out_ref, acc_ref = rest
        else:
            h2_ref, e_ref, gate_ref, out_ref = rest
        a = jnp.concatenate([oa_ref[...] * _silu(ga_ref[...]), ob_ref[...] * _silu(gb_ref[...])], axis=1)
        h2 = h_ref[...] + _dot(a.astype(BF16), wo_ref[...])
        h2_ref[...] = h2
        r = lax.rsqrt(jnp.mean(h2 * h2, axis=-1, keepdims=True) + EPS)
        n2 = (h2 * r * pg_ref[...]).astype(BF16)
        gate = _sigmoid(_dot(n2, wg_ref[...]))
        e = _dot(p_ref[...].astype(BF16), wp_ref[...])
        e_ref[...] = e
        gate_ref[...] = gate
        out = h2 + e * gate
        if not last:
            out_ref[...] = out
            return

        @pl.when(pl.program_id(0) == 0)
        def _():
            acc_ref[...] = jnp.zeros_like(acc_ref)

        err = out - t_ref[...]
        out_ref[...] = err * (1.0 / D_MODEL)
        e2 = err * err
        rows = e2[0:8, :]
        for k in range(1, tm // 8):
            rows = rows + e2[8 * k:8 * (k + 1), :]
        part = rows[:, 0:LANE]
        for k in range(1, D_MODEL // LANE):
            part = part + rows[:, LANE * k:LANE * (k + 1)]
        acc_ref[...] += part

    row = lambda w: pl.BlockSpec((tm, w), lambda i: (i, 0))
    zcol = lambda k: pl.BlockSpec((tm, D_BRANCH), lambda i: (i, k))
    f = lambda: jax.ShapeDtypeStruct((s, D_MODEL), F32)
    return pl.pallas_call(
        body, name=f"outproj_l{l}", grid=(s // tm,),
        in_specs=[row(D_MODEL), row(D_BRANCH), row(D_BRANCH), zcol(3), zcol(7), row(PLE_DIM),
                  _full((D_MODEL, D_MODEL)), _full((1, D_MODEL)), _full((D_MODEL, D_MODEL)),
                  _full((PLE_DIM, D_MODEL))] + ([row(D_MODEL)] if last else []),
        out_specs=[row(D_MODEL)] * 4 + ([_full((8, LANE))] if last else []),
        out_shape=[f(), f(), f(), f()] + ([jax.ShapeDtypeStruct((8, LANE), F32)] if last else []),
        compiler_params=_params(1),
    )(h, oa, ob, z, z, p_i, w_out, ple_g, w_gate, w_ple, *([target] if last else []))


def _outproj_bwd(l, dout, h2, e, gate, p_i, oa, ob, z, w_out_t, ple_g, w_gate_t, hsel, tm, comm=None):
    s = dout.shape[0]

    def body(do_ref, h2_ref, e_ref, gate_ref, p_ref, oa_ref, ob_ref, ga_ref, gb_ref, wot_ref, pg_ref,
             wgt_ref, hsel_ref,
             dh2_ref, doa_ref, dob_ref, dga_ref, dgb_ref, delta_ref, dwo_ref, dwg_ref, dwp_ref, dpg_ref):
        @pl.when(pl.program_id(0) == 0)
        def _():
            dwo_ref[...] = jnp.zeros_like(dwo_ref)
            dwg_ref[...] = jnp.zeros_like(dwg_ref)
            dwp_ref[...] = jnp.zeros_like(dwp_ref)
            dpg_ref[...] = jnp.zeros_like(dpg_ref)

        dho = do_ref[...]
        g = gate_ref[...]
        de = (dho * g).astype(BF16)
        dwp_ref[...] += _dot_tn(p_ref[...].astype(BF16), de)
        dpre = (dho * e_ref[...] * g * (1.0 - g)).astype(BF16)
        h2 = h2_ref[...]
        pg = pg_ref[...]
        r = lax.rsqrt(jnp.mean(h2 * h2, axis=-1, keepdims=True) + EPS)
        n2 = (h2 * r * pg).astype(BF16)
        dwg_ref[...] += _dot_tn(n2, dpre)
        dn2 = _dot(dpre, wgt_ref[...])
        dpg_ref[0:1, :] += jnp.sum(dn2 * h2 * r, axis=0, keepdims=True)
        wv = dn2 * pg
        dh2 = dho + r * wv - h2 * (r * r * r) * jnp.mean(wv * h2, axis=-1, keepdims=True)
        dh2_ref[...] = dh2
        dh2b = dh2.astype(BF16)
        ga, gb, oa, ob = ga_ref[...], gb_ref[...], oa_ref[...], ob_ref[...]
        sga, sgb = _sigmoid(ga), _sigmoid(gb)
        a = jnp.concatenate([oa * ga * sga, ob * gb * sgb], axis=1).astype(BF16)
        dwo_ref[...] += _dot_tn(a, dh2b)
        da = _dot(dh2b, wot_ref[...])
        da_a, da_b = da[:, :D_BRANCH], da[:, D_BRANCH:]
        doa = da_a * ga * sga
        dob = da_b * gb * sgb
        doa_ref[...] = doa.astype(BF16)
        dob_ref[...] = dob.astype(BF16)
        dga_ref[...] = (da_a * oa * sga * (1.0 + ga * (1.0 - sga))).astype(BF16)
        dgb_ref[...] = (da_b * ob * sgb * (1.0 + gb * (1.0 - sgb))).astype(BF16)
        prod = jnp.concatenate([doa * oa, dob * ob], axis=1)
        dt = _split_dot(prod, hsel_ref[...]).T
        for pp in range(2 * N_PAIRS):
            delta_ref[pp, :, :] = dt[2 * pp:2 * pp + 2, :]

    row = lambda w: pl.BlockSpec((tm, w), lambda i: (i, 0))
    zcol = lambda k: pl.BlockSpec((tm, D_BRANCH), lambda i: (i, k))
    grid = (s // tm,)
    c_in, c_ispec, c_ospec, c_oshape, c_scr = _comm_args(comm)
    return pl.pallas_call(
        _fuse_comm(body, 13, 10, comm, grid), name=f"outproj_bwd_l{l}", grid=grid,
        in_specs=[row(D_MODEL)] * 4 + [row(PLE_DIM), row(D_BRANCH), row(D_BRANCH), zcol(3), zcol(7),
                                        _full((D_MODEL, D_MODEL)), _full((1, D_MODEL)), _full((D_MODEL, D_MODEL)),
                                        _full((2 * D_BRANCH, LANE))] + c_ispec,
        out_specs=[row(D_MODEL)] + [row(D_BRANCH)] * 4
        + [pl.BlockSpec((2 * N_PAIRS, 2, tm), lambda i: (0, 0, i)), _full((D_MODEL, D_MODEL)),
           _full((D_MODEL, D_MODEL)), _full((PLE_DIM, D_MODEL)), _full((8, D_MODEL))] + c_ospec,
        out_shape=[jax.ShapeDtypeStruct((s, D_MODEL), F32)] + [jax.ShapeDtypeStruct((s, D_BRANCH), BF16)] * 4
        + [jax.ShapeDtypeStruct((2 * N_PAIRS, 2, s), F32), jax.ShapeDtypeStruct((D_MODEL, D_MODEL), F32),
           jax.ShapeDtypeStruct((D_MODEL, D_MODEL), F32), jax.ShapeDtypeStruct((PLE_DIM, D_MODEL), F32),
           jax.ShapeDtypeStruct((8, D_MODEL), F32)] + c_oshape,
        scratch_shapes=c_scr, compiler_params=_params(1),
    )(dout, h2, e, gate, p_i, oa, ob, z, z, w_out_t, ple_g, w_gate_t, hsel, *c_in)


def _inproj_bwd_prep(l, z, dqt_a, dk_a, dv_a, dqt_b, dk_b, dv_b, dga, dgb, dc_spread, dc_rows, b_f, qkg, bsum,
                     rope, tm):
    s = z.shape[0]
    n = s // tm
    rc, rs1, rs2 = rope

    def body(zqa_ref, zka_ref, zqb_ref, zkb_ref, zf_ref, dqta_ref, dka_ref, dva_ref, dqtb_ref, dkb_ref, dvb_ref,
             dga_ref, dgb_ref, dc_ref, drow_ref, bf_ref, qkg_ref, bsum_ref, rc_ref, rs1_ref, rs2_ref,
             dz_ref, dqkg_ref, dbf_ref, carry):
        @pl.when(pl.program_id(0) == 0)
        def _():
            dqkg_ref[...] = jnp.zeros_like(dqkg_ref)
            dbf_ref[...] = jnp.zeros_like(dbf_ref)
            carry[...] = jnp.zeros_like(carry)

        lane = lax.broadcasted_iota(jnp.int32, (tm, LANE), 1)
        dc = jnp.concatenate([drow_ref[p] for p in range(N_PAIRS)] + [jnp.zeros((LANE - N_HEADS, tm), F32)], axis=0).T
        for p in range(N_PAIRS):
            part = jnp.where(lane < 2, dc_ref[:, LANE * p:LANE * (p + 1)], 0.0)
            dc = dc + (part if p == 0 else pltpu.roll(part, 2 * p, 1))
        dlogf = _split3_dot(_tri(tm, True), dc) + carry[0:1, :]
        carry[...] = jnp.broadcast_to(dlogf[0:1, :], carry.shape)

        bs = bsum_ref[...]
        c, s1, s2 = rc_ref[...], rs1_ref[...], rs2_ref[...]

        def unrope(dy):
            return jnp.concatenate([_rope_bwd(dy[:, LANE * k:LANE * (k + 1)], c, s1, s2)
                                    for k in range(D_BRANCH // LANE)], axis=1)

        def norm_bwd(k, row, dy, x_ref):
            x = x_ref[...]
            r = lax.rsqrt(_head_sums(x * x, bs) * (1.0 / HEAD_DIM) + EPS)
            dqkg_ref[row:row + 1, :] += jnp.sum(dy * x * r, axis=0, keepdims=True)
            w = dy * qkg_ref[row:row + 1, :]
            dx = r * w - x * (r * r * r) * (_head_sums(w * x, bs) * (1.0 / HEAD_DIM))
            dz_ref[:, D_BRANCH * k:D_BRANCH * (k + 1)] = dx.astype(BF16)

        norm_bwd(0, 0, dqta_ref[...].T * Q_SCALE, zqa_ref)
        norm_bwd(1, 1, dka_ref[...] * LN2, zka_ref)
        dz_ref[:, 2 * D_BRANCH:3 * D_BRANCH] = dva_ref[...].astype(BF16)
        dz_ref[:, 3 * D_BRANCH:4 * D_BRANCH] = dga_ref[...]
        norm_bwd(4, 2, unrope(dqtb_ref[...].T * Q_SCALE), zqb_ref)
        norm_bwd(5, 3, unrope(dkb_ref[...] * LN2), zkb_ref)
        dz_ref[:, 6 * D_BRANCH:7 * D_BRANCH] = dvb_ref[...].astype(BF16)
        dz_ref[:, 7 * D_BRANCH:8 * D_BRANCH] = dgb_ref[...]
        dfa = dlogf * _sigmoid(-(zf_ref[...] + bf_ref[...]))
        dz_ref[:, N_MAIN:N_ALL] = dfa.astype(BF16)
        dbf_ref[0:1, :] += jnp.sum(dfa, axis=0, keepdims=True)

    row = lambda w: pl.BlockSpec((tm, w), lambda i: (n - 1 - i, 0))
    colt = pl.BlockSpec((D_BRANCH, tm), lambda i: (0, n - 1 - i))
    zcol = lambda k: pl.BlockSpec((tm, D_BRANCH), lambda i: (n - 1 - i, k))
    zf = pl.BlockSpec((tm, LANE), lambda i: (n - 1 - i, N_MAIN // LANE))
    return pl.pallas_call(
        body, name=f"inproj_bwd_prep_l{l}", grid=(n,),
        in_specs=[zcol(0), zcol(1), zcol(4), zcol(5), zf, colt, row(D_BRANCH), row(D_BRANCH), colt, row(D_BRANCH), row(D_BRANCH),
                  row(D_BRANCH), row(D_BRANCH), row(N_PAIRS * LANE),
                  pl.BlockSpec((N_PAIRS, 2, tm), lambda i: (0, 0, n - 1 - i)), _full((1, LANE)),
                  _full((8, D_BRANCH)), _full((HEADS_PER_BLOCK * HEAD_DIM,) * 2), row(LANE), row(LANE), row(LANE)],
        out_specs=[row(N_ALL), _full((8, D_BRANCH)), _full((8, LANE))],
        out_shape=[jax.ShapeDtypeStruct((s, N_ALL), BF16), jax.ShapeDtypeStruct((8, D_BRANCH), F32),
                   jax.ShapeDtypeStruct((8, LANE), F32)],
        scratch_shapes=[pltpu.VMEM((8, LANE), F32)],
        compiler_params=_params(1),
    )(z, z, z, z, z, dqt_a, dk_a, dv_a, dqt_b, dk_b, dv_b, dga, dgb, dc_spread, dc_rows, b_f, qkg, bsum, rc, rs1, rs2)


def _inproj_bwd_dx(l, dz, w_all_t, h, norm_g, dh2, tm, comm=None):
    s = dz.shape[0]

    def body(dz_ref, wt_ref, h_ref, g_ref, dh2_ref, dh_ref, dg_ref):
        @pl.when(pl.program_id(0) == 0)
        def _():
            dg_ref[...] = jnp.zeros_like(dg_ref)

        du = _dot(dz_ref[...], wt_ref[...])
        hh = h_ref[...]
        g = g_ref[...]
        r = lax.rsqrt(jnp.mean(hh * hh, axis=-1, keepdims=True) + EPS)
        dg_ref[0:1, :] += jnp.sum(du * hh * r, axis=0, keepdims=True)
        wv = du * g
        dh_ref[...] = dh2_ref[...] + r * wv - hh * (r * r * r) * jnp.mean(wv * hh, axis=-1, keepdims=True)

    row = lambda w: pl.BlockSpec((tm, w), lambda i: (i, 0))
    grid = (s // tm,)
    c_in, c_ispec, c_ospec, c_oshape, c_scr = _comm_args(comm)
    return pl.pallas_call(
        _fuse_comm(body, 5, 2, comm, grid), name=f"inproj_bwd_dx_l{l}", grid=grid,
        in_specs=[row(N_ALL), _full((N_ALL, D_MODEL)), row(D_MODEL), _full((1, D_MODEL)), row(D_MODEL)] + c_ispec,
        out_specs=[row(D_MODEL), _full((8, D_MODEL))] + c_ospec,
        out_shape=[jax.ShapeDtypeStruct((s, D_MODEL), F32), jax.ShapeDtypeStruct((8, D_MODEL), F32)] + c_oshape,
        scratch_shapes=c_scr, compiler_params=_params(1),
    )(dz, w_all_t, h, norm_g, dh2, *c_in)


def _inproj_bwd_dw(l, u, dz, tm, tn):
    s = u.shape[0]

    def body(u_ref, dz_ref, dw_ref):
        @pl.when(pl.program_id(1) == 0)
        def _():
            dw_ref[...] = jnp.zeros_like(dw_ref)

        dw_ref[...] += _dot_tn(u_ref[...], dz_ref[...])

    return pl.pallas_call(
        body, name=f"inproj_bwd_dw_l{l}", grid=(N_ALL // tn, s // tm),
        in_specs=[pl.BlockSpec((tm, D_MODEL), lambda n, i: (i, 0)), pl.BlockSpec((tm, tn), lambda n, i: (i, n))],
        out_specs=pl.BlockSpec((D_MODEL, tn), lambda n, i: (0, n)),
        out_shape=jax.ShapeDtypeStruct((D_MODEL, N_ALL), F32),
        compiler_params=_params(2),
    )(u, dz)


def _adamw_math(w, g, m, v):
    m = ADAM_B1 * m + (1.0 - ADAM_B1) * g
    v = ADAM_B2 * v + (1.0 - ADAM_B2) * (g * g)
    m_hat = m / (1.0 - ADAM_B1 ** ADAM_STEP)
    v_hat = v / (1.0 - ADAM_B2 ** ADAM_STEP)
    delta = -ADAM_LR * (m_hat / (jnp.sqrt(v_hat) + ADAM_EPS) + ADAM_WD * w)
    return delta, m, v


def _adamw(name, w, halves, m, v, core):
    nl, r, c = w.shape
    hr = r // 2
    tr = 128 if hr % 128 == 0 else hr
    nb = hr // tr

    def body(core_ref, w_ref, own0_ref, oth0_ref, own1_ref, oth1_ref, m_ref, v_ref, g_ref, d_ref, nm_ref, nv_ref):
        first = pl.program_id(0) == 0
        own = jnp.where(first, own0_ref[...], own1_ref[...])
        oth = jnp.where(first, oth0_ref[...], oth1_ref[...])
        g = jnp.where(pl.program_id(1) // nb == core_ref[0], own, oth)
        d, nm, nv = _adamw_math(w_ref[...], g, m_ref[...], v_ref[...])
        g_ref[...] = g
        d_ref[...] = d
        nm_ref[...] = nm
        nv_ref[...] = nv

    spec = pl.BlockSpec((None, tr, c), lambda a, b, core_ref: (a, b, 0))
    gspec = pl.BlockSpec((tr, c), lambda a, b, core_ref: (b % nb, 0))
    shp = jax.ShapeDtypeStruct(w.shape, F32)
    return pl.pallas_call(
        body, name=name,
        grid_spec=pltpu.PrefetchScalarGridSpec(
            num_scalar_prefetch=1, grid=(nl, r // tr), in_specs=[spec] + [gspec] * 4 + [spec, spec],
            out_specs=[spec] * 4),
        out_shape=[shp, shp, shp, shp], compiler_params=_params(2),
    )(core, w, halves[0][0], halves[0][1], halves[1][0], halves[1][1], m, v)


def _assemble_halves(name, shape, halves, core):
    nl, r, c = shape
    hr = r // 2
    tr = 128 if hr % 128 == 0 else hr
    nb = hr // tr

    def body(core_ref, own0_ref, oth0_ref, own1_ref, oth1_ref, g_ref):
        first = pl.program_id(0) == 0
        own = jnp.where(first, own0_ref[...], own1_ref[...])
        oth = jnp.where(first, oth0_ref[...], oth1_ref[...])
        g_ref[...] = jnp.where(pl.program_id(1) // nb == core_ref[0], own, oth)

    of_layer = lambda l: pl.BlockSpec((tr, c), lambda a, b, core_ref: ((b % nb) * (a if l else 1 - a), 0))
    return pl.pallas_call(
        body, name=name,
        grid_spec=pltpu.PrefetchScalarGridSpec(
            num_scalar_prefetch=1, grid=(nl, r // tr), in_specs=[of_layer(0), of_layer(0), of_layer(1), of_layer(1)],
            out_specs=pl.BlockSpec((None, tr, c), lambda a, b, core_ref: (a, b, 0))),
        out_shape=jax.ShapeDtypeStruct(shape, F32), compiler_params=_params(2),
    )(core, halves[0][0], halves[0][1], halves[1][0], halves[1][1])


W_IN_FLAT_STEPS = 19


def _to_flat(a):
    nl, r, c = a.shape
    return jnp.transpose(jnp.transpose(a, (2, 0, 1)).reshape(c, nl, r // LANE, LANE), (0, 2, 1, 3)).reshape(-1, LANE)


def _from_flat(f, shape):
    nl, r, c = shape
    return jnp.transpose(jnp.transpose(f.reshape(c, r // LANE, nl, LANE), (0, 2, 1, 3)).reshape(c, nl, r), (1, 2, 0))


def _adamw_flat(name, w, g, m, v):
    n = w.shape[0]
    tr = n // W_IN_FLAT_STEPS

    def body(w_ref, g_ref, m_ref, v_ref, d_ref, nm_ref, nv_ref):
        d, nm, nv = _adamw_math(w_ref[...], g_ref[...], m_ref[...], v_ref[...])
        d_ref[...] = d
        nm_ref[...] = nm
        nv_ref[...] = nv

    spec = pl.BlockSpec((tr, LANE), lambda i: (i, 0))
    shp = jax.ShapeDtypeStruct(w.shape, F32)
    return pl.pallas_call(body, name=name, grid=(W_IN_FLAT_STEPS,), in_specs=[spec] * 4, out_specs=[spec] * 3,
                          out_shape=[shp, shp, shp], compiler_params=_params(1))(w, g, m, v)


def _pair_sum(name, g, x, c, narrow=False):
    n, r, cc = g.shape
    hr = r // 2
    tr = 128 if hr % 128 == 0 else hr
    nb = hr // tr

    def body(c_ref, g_ref, x_ref, o_ref, *narrow_ref):
        total = g_ref[...] + x_ref[...]
        o_ref[...] = total
        if narrow:
            narrow_ref[0][...] = total.astype(BF16)

    spec = pl.BlockSpec((None, tr, cc), lambda i, j, c_ref: (i, j, 0))
    shapes = [jax.ShapeDtypeStruct((n, hr, cc), F32)] + ([jax.ShapeDtypeStruct((n, hr, cc), BF16)] if narrow else [])
    return pl.pallas_call(
        body, name=name,
        grid_spec=pltpu.PrefetchScalarGridSpec(
            num_scalar_prefetch=1, grid=(n, nb),
            in_specs=[pl.BlockSpec((None, tr, cc), lambda i, j, c_ref: (i, c_ref[0] * nb + j, 0)), spec],
            out_specs=[spec] * len(shapes)),
        out_shape=shapes, compiler_params=_params(2),
    )(c, g, x)


def _sum_slots(name, own, landed, chip):
    n, r, c = own.shape
    tr = 128 if r % 128 == 0 else r

    def body(chip_ref, a_ref, b_ref, c_ref, d_ref, o_ref):
        o_ref[...] = ((a_ref[...] + b_ref[...].astype(F32)) + c_ref[...].astype(F32)) + d_ref[...].astype(F32)

    slot = lambda d: pl.BlockSpec((None, tr, c), lambda j, chip_ref: ((chip_ref[0] + d) % n, j, 0))
    return pl.pallas_call(
        body, name=name,
        grid_spec=pltpu.PrefetchScalarGridSpec(
            num_scalar_prefetch=1, grid=(r // tr,), in_specs=[slot(0), slot(1), slot(2), slot(3)],
            out_specs=pl.BlockSpec((tr, c), lambda j, chip_ref: (j, 0))),
        out_shape=jax.ShapeDtypeStruct((r, c), F32), compiler_params=_params(1),
    )(chip, own, landed, landed, landed)


def _me():
    return lax.axis_index("x"), lax.axis_index("y"), lax.axis_index("c")


def _other_chips(x, y):
    return [(1 - x, y), (x, 1 - y), (1 - x, 1 - y)]


def _dma_sems(*counts):
    return [pltpu.SemaphoreType.DMA((n,)) for n in counts]


def _half_rows(rows, which, align):
    return pl.ds(pl.multiple_of(which * (rows // 2), align), rows // 2)


def _gather_first_layer(w_in, others):
    n = len(others)
    rows = w_in.shape[1]

    def body(*refs):
        w_ref, o_refs = refs[0], refs[1:1 + n]
        out, keep = refs[1 + n], refs[2 + n:3 + 3 * n]
        stage, ici_send, ici_recv, d2d_send, d2d_recv, local_sem = refs[3 + 3 * n:]
        x, y, c = _me()
        k = 2 * x + y
        chips = _other_chips(x, y)
        stage[...] = w_ref[0].astype(BF16)
        local = pltpu.make_async_copy(stage, out.at[k], local_sem)
        local.start()
        mine = _half_rows(rows, c, 16)
        first, passed = [], []
        for j, (px, py) in enumerate(chips):
            cp = pltpu.make_async_remote_copy(
                src_ref=stage.at[mine], dst_ref=out.at[k, mine], send_sem=ici_send.at[j], recv_sem=ici_recv.at[j],
                device_id=(px, py, c), device_id_type=MESH)
            cp.start()
            first.append(cp)
        keep[0][...] = w_ref[1].astype(BF16)
        for t in range(n):
            for l in range(2):
                keep[1 + 2 * t + l][...] = o_refs[t][l].astype(BF16)
        for j, (px, py) in enumerate(chips):
            landed = out.at[2 * px + py, mine]
            first[j].wait_recv()
            cp = pltpu.make_async_remote_copy(
                src_ref=landed, dst_ref=landed, send_sem=d2d_send.at[j], recv_sem=d2d_recv.at[j],
                device_id=(x, y, 1 - c), device_id_type=MESH)
            cp.start()
            passed.append(cp)
        for cp in passed:
            cp.wait_recv()
        for cp in first + passed:
            cp.wait_send()
        local.wait()

    kept = [jax.ShapeDtypeStruct(w_in.shape[1:], BF16)]
    for o in others:
        kept += [jax.ShapeDtypeStruct(o.shape[1:], BF16)] * 2
    return pl.pallas_call(
        body, name="gather_first_layer",
        in_specs=[VMEM_SPEC] * (1 + n), out_specs=[ANY] + [VMEM_SPEC] * len(kept),
        out_shape=[jax.ShapeDtypeStruct((4,) + w_in.shape[1:], BF16)] + kept,
        scratch_shapes=[pltpu.VMEM(w_in.shape[1:], BF16)] + _dma_sems(3, 3, 3, 3) + [pltpu.SemaphoreType.DMA],
        compiler_params=pltpu.CompilerParams(vmem_limit_bytes=VMEM_LIMIT),
    )(w_in, *others)


def _run_comm(name, comm):
    nci, nco = len(comm.ins), len(comm.out_shapes)

    def body(*refs):
        copies = comm.make(refs[:nci], refs[nci:nci + nco], refs[nci + nco:])
        for cp in copies:
            cp.start()
        for cp in copies:
            cp.wait()

    return pl.pallas_call(body, name=name, in_specs=[ANY] * nci, out_specs=[ANY] * nco,
                          out_shape=list(comm.out_shapes), scratch_shapes=list(comm.sems))(*comm.ins)


def _gather_comm(mine):
    n = len(mine)

    def make(ins, outs, sems):
        send_sems, recv_sems, local_sems = sems
        x, y, c = _me()
        k = 2 * x + y
        copies = []
        for t in range(n):
            copies.append(pltpu.make_async_copy(ins[t], outs[t].at[k], local_sems.at[t]))
            for j, (px, py) in enumerate(_other_chips(x, y)):
                copies.append(pltpu.make_async_remote_copy(
                    src_ref=ins[t], dst_ref=outs[t].at[k], send_sem=send_sems.at[3 * t + j],
                    recv_sem=recv_sems.at[3 * t + j], device_id=(px, py, c), device_id_type=MESH))
        return copies

    return _Comm(mine, [jax.ShapeDtypeStruct((4,) + a.shape, a.dtype) for a in mine], _dma_sems(3 * n, 3 * n, n), make)


def _swap_comm(grads):
    n = len(grads)

    def make(ins, outs, sems):
        send_sems, recv_sems = sems
        x, y, c = _me()
        return [pltpu.make_async_remote_copy(
            src_ref=ins[t].at[:, _half_rows(grads[t].shape[1], 1 - c, 8)], dst_ref=outs[t],
            send_sem=send_sems.at[t], recv_sem=recv_sems.at[t], device_id=(x, y, 1 - c), device_id_type=MESH)
            for t in range(n)]

    shapes = [jax.ShapeDtypeStruct((g.shape[0], g.shape[1] // 2, g.shape[2]), F32) for g in grads]
    return _Comm(grads, shapes, _dma_sems(n, n), make)


def _scatter_comm(parts):
    n = len(parts)

    def make(ins, outs, sems):
        send_sems, recv_sems = sems
        x, y, c = _me()
        k = 2 * x + y
        return [pltpu.make_async_remote_copy(
            src_ref=ins[t].at[2 * px + py], dst_ref=outs[t].at[k], send_sem=send_sems.at[3 * t + j],
            recv_sem=recv_sems.at[3 * t + j], device_id=(px, py, c), device_id_type=MESH)
            for t in range(n) for j, (px, py) in enumerate(_other_chips(x, y))]

    return _Comm(parts, [jax.ShapeDtypeStruct(p.shape, p.dtype) for p in parts], _dma_sems(3 * n, 3 * n), make)


def _share_comm(totals):
    n = len(totals)

    def make(ins, outs, sems):
        send_sems, recv_sems = sems
        x, y, c = _me()
        return [pltpu.make_async_remote_copy(
            src_ref=ins[t], dst_ref=outs[t], send_sem=send_sems.at[t], recv_sem=recv_sems.at[t],
            device_id=(x, y, 1 - c), device_id_type=MESH) for t in range(n)]

    return _Comm(totals, [jax.ShapeDtypeStruct(t.shape, F32) for t in totals], _dma_sems(n, n), make)


def _small_allreduce_adamw(part, w, m, v):
    shape = part.shape

    def body(part_ref, w_ref, m_ref, v_ref, g_ref, d_ref, nm_ref, nv_ref, slots, send_sems, recv_sems):
        x, y, c = _me()
        me = 4 * x + 2 * y + c
        slots[me] = part_ref[...]
        copies = []
        for d in range(1, 8):
            peer = (x ^ (d >> 2), y ^ ((d >> 1) & 1), c ^ (d & 1))
            cp = pltpu.make_async_remote_copy(
                src_ref=part_ref, dst_ref=slots.at[me], send_sem=send_sems.at[d - 1], recv_sem=recv_sems.at[d - 1],
                device_id=peer, device_id_type=MESH)
            cp.start()
            copies.append(cp)
        for cp in copies:
            cp.wait()
        g = slots[0]
        for i in range(1, 8):
            g = g + slots[i]
        g_ref[...] = g
        d, nm, nv = _adamw_math(w_ref[...], g, m_ref[...], v_ref[...])
        d_ref[...] = d
        nm_ref[...] = nm
        nv_ref[...] = nv

    shp = jax.ShapeDtypeStruct(shape, F32)
    return pl.pallas_call(
        body, name="small_allreduce_adamw", in_specs=[VMEM_SPEC] * 4, out_specs=[VMEM_SPEC] * 4,
        out_shape=[shp, shp, shp, shp],
        scratch_shapes=[pltpu.VMEM((8,) + shape, F32), pltpu.SemaphoreType.DMA((7,)), pltpu.SemaphoreType.DMA((7,))],
    )(part, w, m, v)


TM = 512
T_FOX = 1024
T_DIL_FWD = 1024
T_DIL_BWD = 512
TN_DW = 1408


def _layer_fwd(l, h, p_i, w_inside, w_outside, consts, comm=None, target=None):
    w_all, _, norm_g, b_f, qkg = w_inside
    bsum, _, bias_t, _, rope = consts
    u, z, qa, ka, va, qb, kb, vb, c_spread, c_t, va_t, vb_t = _inproj(l, h, norm_g, w_all, b_f, qkg, bsum, rope, TM)
    oa, lse_a, *landed = _attn_fwd(f"fox_fwd_l{l}", True, qa, ka, va_t, (c_spread, c_t), T_FOX, comm)
    ob, lse_b = _attn_fwd(f"dil_fwd_l{l}", False, qb, kb, vb_t, (bias_t,), T_DIL_FWD)
    if comm is not None:
        w_outside = w_outside(landed)
    w_out, _, w_gate, _, w_ple, ple_g = w_outside
    h2, e, gate, *out = _outproj(l, h, oa, ob, z, p_i, w_out, ple_g, w_gate, w_ple, TM, target)
    out = out[0] if target is None else tuple(out)
    saved = (h, u, z, qa, ka, va, qb, kb, vb, c_spread, c_t, oa, lse_a, ob, lse_b, h2, e, gate)
    return out, saved, w_outside


def _reduce_names(tag, n):
    return [f"reduce_{tag}_{i}" for i in range(n)]


def _layer_bwd(l, dout, p_i, wts, consts, saved, pending=None, core=None, chip=None):
    (_, w_all_t, norm_g, b_f, qkg), (_, w_out_t, _, w_gate_t, _, ple_g) = wts
    bsum, hsel, _, bias_t, rope = consts
    h, u, z, qa, ka, va, qb, kb, vb, c_spread, c_t, oa, lse_a, ob, lse_b, h2, e, gate = saved
    fused = pending is not None
    dh2, doa, dob, dga, dgb, delta_t, dw_out, dw_gate, dw_ple, dple_g = _outproj_bwd(
        l, dout, h2, e, gate, p_i, oa, ob, z, w_out_t, ple_g, w_gate_t, hsel, TM)
    if fused:
        group = list(pending) + list(_slot_layout_out(dw_out, dw_ple, dw_gate))
        n = len(group)
    dqt_a, dk_a, dv_a, dc, drow, *sib = _attn_bwd(
        f"fox_bwd_l{l}", True, qa, ka, va, doa, lse_a, delta_t, 0, (c_spread, c_t), T_FOX,
        _swap_comm(group) if fused else None)
    if fused:
        pair = [_pair_sum(nm, g, x, core)[0] for nm, g, x in zip(_reduce_names("pair_hidden", n), group, sib)]
    dqt_b, dk_b, dv_b, *landed = _attn_bwd(f"dil_bwd_l{l}", False, qb, kb, vb, dob, lse_b, delta_t, N_PAIRS,
                                             (bias_t,), T_DIL_BWD, _scatter_comm(pair) if fused else None)
    if fused:
        totals = [_sum_slots(nm, a, y, chip) for nm, a, y in zip(_reduce_names("chips_hidden", n), pair, landed)]
    dz, dqkg, dbf = _inproj_bwd_prep(l, z, dqt_a, dk_a, dv_a, dqt_b, dk_b, dv_b, dga, dgb, dc, drow, b_f, qkg,
                                     bsum, rope, TM)
    dh, dnorm_g, *other = _inproj_bwd_dx(l, dz, w_all_t, h, norm_g, dh2, TM, _share_comm(totals) if fused else None)
    dw_all = _inproj_bwd_dw(l, u, dz, TM, TN_DW)
    reduced = list(zip(totals, other)) if fused else None
    return dh, (dw_all, dw_out, dw_ple, dw_gate, dnorm_g[0], dbf[0, :N_HEADS], dqkg[:4], dple_g[0]), reduced


def _reduce_last(grads, core, chip):
    n = len(grads)
    sib = _run_comm("reduce_swap_last", _swap_comm(grads))
    pair = [_pair_sum(nm, g, x, core, narrow=True) for nm, g, x in zip(_reduce_names("pair_last", n), grads, sib)]
    landed = _run_comm("reduce_scatter_last", _scatter_comm([p[1] for p in pair]))
    totals = [_sum_slots(nm, p[0], y, chip) for nm, p, y in zip(_reduce_names("chips_last", n), pair, landed)]
    return list(zip(totals, _run_comm("reduce_share_last", _share_comm(totals))))


N_FA = 2048


W_SHARD = N_IN // 4


def _shard_pieces(lo, hi):
    cuts = [(k, max(lo, k * W_SHARD), min(hi, (k + 1) * W_SHARD)) for k in range(4)]
    return [(k, a - k * W_SHARD, b - k * W_SHARD) for k, a, b in cuts if a < b]


def _in_weights(l, g_in, norm_g, b_f, qk_norm_g):
    order = _shard_pieces(0, N_FA) + _shard_pieces(N_FA + N_HEADS, N_IN) + _shard_pieces(N_FA, N_FA + N_HEADS)
    w_all = jnp.concatenate([g_in[k, :, a:b] for k, a, b in order]
                            + [jnp.zeros((D_MODEL, LANE - N_HEADS), g_in.dtype)], axis=1)
    qkg = jnp.pad(jnp.tile(qk_norm_g[l], (1, N_HEADS)), ((0, 4), (0, 0)))
    bf = jnp.pad(b_f[l], (0, LANE - N_HEADS))[None, :]
    return w_all, w_all.T, norm_g[l][None, :], bf, qkg


def _out_weights(l, g_out, g_ple, g_gate, ple_norm_g):
    w_out = g_out.reshape(D_MODEL, D_MODEL)
    w_gate = g_gate.reshape(D_MODEL, D_MODEL)
    w_ple = jnp.transpose(g_ple, (1, 0, 2)).reshape(PLE_DIM, D_MODEL)
    return w_out, w_out.T, w_gate, w_gate.T, w_ple, ple_norm_g[l][None, :]


def _slot_layout_in(dw_all):
    regions = ((0, N_FA, 0), (N_FA, N_FA + N_HEADS, N_MAIN - N_FA), (N_FA + N_HEADS, N_IN, -N_HEADS))

    def shard(k):
        cuts = [(max(k * W_SHARD, a) + shift, min((k + 1) * W_SHARD, b) + shift) for a, b, shift in regions]
        return jnp.concatenate([dw_all[:, a:b] for a, b in cuts if a < b], axis=1)

    return jnp.stack([shard(k) for k in range(4)])


def _slot_layout_out(dw_out, dw_ple, dw_gate):
    return (dw_out.reshape(4, D_MODEL // 4, D_MODEL),
            jnp.transpose(dw_ple.reshape(PLE_DIM, 4, D_MODEL // 4), (1, 0, 2)),
            dw_gate.reshape(4, D_MODEL // 4, D_MODEL))


SMALL_ROWS = 40


def _pack_small(norm_g, ple_norm_g, qk_norm_g, b_f, last=0.0):
    flat = jnp.concatenate([norm_g.reshape(-1), ple_norm_g.reshape(-1), qk_norm_g.reshape(-1), b_f.reshape(-1)])
    flat = jnp.pad(flat, (0, SMALL_ROWS * LANE - flat.shape[0] - 1))
    return jnp.concatenate([flat, jnp.reshape(last, (1,)).astype(F32)]).reshape(SMALL_ROWS, LANE)


def _unpack_small(packed):
    flat = packed.reshape(-1)
    n1, n2, n3 = 2 * D_MODEL, 4 * D_MODEL, 4 * D_MODEL + 2 * 4 * HEAD_DIM
    return (flat[:n1].reshape(2, D_MODEL), flat[n1:n2].reshape(2, D_MODEL), flat[n2:n3].reshape(2, 4, HEAD_DIM),
            flat[n3:n3 + 2 * N_HEADS].reshape(2, N_HEADS))


def kernel(x, p, positions, norm_g, w_in, b_f, qk_norm_g, w_out, w_ple, ple_norm_g, w_ple_gate, loss_target,
           m_norm_g, m_w_in, m_b_f, m_qk_norm_g, m_w_out, m_w_ple, m_ple_norm_g, m_w_ple_gate,
           v_norm_g, v_w_in, v_b_f, v_qk_norm_g, v_w_out, v_w_ple, v_ple_norm_g, v_w_ple_gate):
    assert w_in.shape[0] == 2, "the schedule below is written for two layers"
    w_in0, *kept = _gather_first_layer(w_in, [w_out, w_ple, w_ple_gate])
    consts = (_head_block_diag(), _head_select(), _dil_bias(T_DIL_FWD), _dil_bias(T_DIL_BWD),
              _rope_tables(positions[0]))
    later = {}

    def outside0(landed):
        later["w_in1"] = landed[0]
        later["out1"] = landed[2::2]
        return _out_weights(0, *landed[1::2], ple_norm_g)

    inside0 = _in_weights(0, w_in0, norm_g, b_f, qk_norm_g)
    h1, saved0, outside0 = _layer_fwd(0, x[0], p[0, 0], inside0, outside0, consts, _gather_comm(kept))
    wts0 = (inside0, outside0)
    wts1 = (_in_weights(1, later["w_in1"], norm_g, b_f, qk_norm_g), _out_weights(1, *later["out1"], ple_norm_g))
    (dh, sq), saved1, _ = _layer_fwd(1, h1, p[1, 0], *wts1, consts, target=loss_target[0])

    core = lax.axis_index("c").astype(jnp.int32).reshape(1)
    chip = (2 * lax.axis_index("x") + lax.axis_index("y")).astype(jnp.int32).reshape(1)
    dh, grads1, _ = _layer_bwd(1, dh, p[1, 0], wts1, consts, saved1)
    pending = (_slot_layout_in(grads1[0]),) + _slot_layout_out(*grads1[1:4])
    dh, grads0, hidden = _layer_bwd(0, dh, p[0, 0], wts0, consts, saved0, pending, core, chip)
    last = _reduce_last([_slot_layout_in(grads0[0])], core, chip)
    reduced0 = [last[0]] + hidden[4:7]
    reduced1 = hidden[0:4]
    grad_x = dh[None]
    small = [grads0[4:], grads1[4:]]
    n_layers = 2

    outs = {}
    for t, (name, w, m, v) in enumerate((("w_in", w_in, m_w_in, v_w_in), ("w_out", w_out, m_w_out, v_w_out),
                                         ("w_ple", w_ple, m_w_ple, v_w_ple),
                                         ("w_ple_gate", w_ple_gate, m_w_ple_gate, v_w_ple_gate))):
        if name == "w_in":
            g = _assemble_halves("assemble_w_in", w.shape, (reduced0[t], reduced1[t]), core)
            g_flat = _to_flat(g)
            flat = _adamw_flat("adamw_w_in", _to_flat(w), g_flat, _to_flat(m), _to_flat(v))
            outs[name] = tuple(_from_flat(f, w.shape) for f in (g_flat,) + tuple(flat))
        else:
            outs[name] = tuple(_adamw(f"adamw_{name}", w, (reduced0[t], reduced1[t]), m, v, core))

    part = _pack_small(jnp.stack([s[0] for s in small]), jnp.stack([s[3] for s in small]),
                       jnp.stack([s[2] for s in small]).reshape(n_layers, 4, N_HEADS, HEAD_DIM).sum(axis=2),
                       jnp.stack([s[1] for s in small]), 0.5 / D_MODEL * jnp.sum(sq))
    packed = _small_allreduce_adamw(part, _pack_small(norm_g, ple_norm_g, qk_norm_g, b_f),
                                    _pack_small(m_norm_g, m_ple_norm_g, m_qk_norm_g, m_b_f),
                                    _pack_small(v_norm_g, v_ple_norm_g, v_qk_norm_g, v_b_f))
    loss = packed[0][SMALL_ROWS - 1, LANE - 1]
    sm = [_unpack_small(a) for a in packed]
    for i, name in enumerate(("norm_g", "ple_norm_g", "qk_norm_g", "b_f")):
        outs[name] = tuple(sm[j][i] for j in range(4))

    order = ("norm_g", "w_in", "b_f", "qk_norm_g", "w_out", "w_ple", "ple_norm_g", "w_ple_gate")
    return (loss, grad_x) + tuple(outs[n][j] for j in range(4) for n in order)
```

```python
import functools
from typing import Any, Callable, NamedTuple, Sequence

import numpy as np
import jax
import jax.numpy as jnp
from jax import lax
from jax.experimental import pallas as pl
from jax.experimental.pallas import tpu as pltpu

F32 = jnp.float32
BF16 = jnp.bfloat16
MESH = pl.DeviceIdType.MESH

D_MODEL = 1024
HEAD_DIM = 64
D_BRANCH = 512
N_HEADS = 8
N_PAIRS = 4
N_IN = 4104
N_MAIN = 4096
N_ALL = 4224
PLE_DIM = 256
ROPE_THETA = 500000.0
ROPE_HALF = 8
EPS = 1e-6
NEG = -1e30
M_INIT = -1e29
Q_SCALE = HEAD_DIM ** -0.5
LOG2E = 1.4426950408889634
LN2 = 0.6931471805599453
DIL_PATTERNS = ((128, 1), (512, 4), (2048, 16))
DIL_BACK = 2048
ADAM_LR, ADAM_B1, ADAM_B2, ADAM_EPS, ADAM_WD, ADAM_STEP = 0.001, 0.9, 0.999, 1e-08, 0.01, 10
VMEM_LIMIT = 56 * 1024 * 1024
LANE = 128


def _dot(a, b):
    return jnp.dot(a, b, preferred_element_type=F32)


def _dot_nt(a, b):
    return lax.dot_general(a, b, (((1,), (1,)), ((), ())), preferred_element_type=F32)


def _dot_tn(a, b):
    return lax.dot_general(a, b, (((0,), (0,)), ((), ())), preferred_element_type=F32)


def _split_dot(x, w):
    hi = x.astype(BF16)
    lo = (x - hi.astype(F32)).astype(BF16)
    return _dot(hi, w) + _dot(lo, w)


def _head_sums(x, bs):
    w = bs.shape[0]
    return jnp.concatenate([_split_dot(x[:, w * k:w * (k + 1)], bs) for k in range(x.shape[1] // w)], axis=1)


def _split3_dot(w, x):
    hi = x.astype(BF16)
    r1 = x - hi.astype(F32)
    mid = r1.astype(BF16)
    lo = (r1 - mid.astype(F32)).astype(BF16)
    return _dot(w, hi) + _dot(w, mid) + _dot(w, lo)


def _sigmoid(x):
    return 1.0 / (1.0 + jnp.exp(-x))


def _params(n_grid):
    return pltpu.CompilerParams(dimension_semantics=("arbitrary",) * n_grid,
                                vmem_limit_bytes=VMEM_LIMIT)


def _full(shape):
    nd = len(shape)
    return pl.BlockSpec(shape, lambda *_: (0,) * nd)


ANY = pl.BlockSpec(memory_space=pl.ANY)
VMEM_SPEC = pl.BlockSpec(memory_space=pltpu.VMEM)


class _Comm(NamedTuple):
    ins: Sequence[Any]
    out_shapes: Sequence[Any]
    sems: Sequence[Any]
    make: Callable[..., Any]


def _fuse_comm(body, n_in, n_out, comm, grid):
    if comm is None:
        return body
    nci, nco, ncs = len(comm.ins), len(comm.out_shapes), len(comm.sems)

    def fused(*refs):
        a, b = n_in + nci, n_in + nci + n_out
        ins, cins, outs, couts = refs[:n_in], refs[n_in:a], refs[a:b], refs[b:b + nco]
        scratch, sems = refs[b + nco:len(refs) - ncs], refs[len(refs) - ncs:]
        first = functools.reduce(jnp.logical_and, [pl.program_id(d) == 0 for d in range(len(grid))])
        last = functools.reduce(jnp.logical_and, [pl.program_id(d) == n - 1 for d, n in enumerate(grid)])

        @pl.when(first)
        def _():
            for cp in comm.make(cins, couts, sems):
                cp.start()

        body(*ins, *outs, *scratch)

        @pl.when(last)
        def _():
            for cp in comm.make(cins, couts, sems):
                cp.wait()

    return fused


def _comm_args(comm):
    if comm is None:
        return [], [], [], [], []
    return (list(comm.ins), [ANY] * len(comm.ins), [ANY] * len(comm.out_shapes), list(comm.out_shapes),
            list(comm.sems))


HEADS_PER_BLOCK = 4


def _head_block_diag():
    i = np.arange(HEADS_PER_BLOCK * HEAD_DIM)
    return jnp.asarray((i[:, None] // HEAD_DIM == i[None, :] // HEAD_DIM).astype(np.float32), BF16)


def _head_select():
    i = np.arange(2 * D_BRANCH)
    j = np.arange(LANE)
    return jnp.asarray((i[:, None] // HEAD_DIM == j[None, :]).astype(np.float32), BF16)


def _dil_bias(t):
    nb = DIL_BACK // t + 1
    qi = np.arange(t)[:, None]
    ki = np.arange(t)[None, :]
    tiles = []
    for r in range(nb):
        d = r * t + qi - ki
        mult = np.zeros((t, t), np.int64)
        for window, dil in DIL_PATTERNS:
            mult += ((d >= 0) & (d <= window) & (d % dil == 0)).astype(np.int64)
        b = np.where(mult > 0, np.log2(np.maximum(mult, 1)), NEG).astype(np.float32)
        tiles.append(b.T)
    return jnp.asarray(np.stack(tiles))


def _rope_tables(positions):
    inv_freq = ROPE_THETA ** (-jnp.arange(ROPE_HALF, dtype=F32) / ROPE_HALF)
    ang = positions.astype(F32)[:, None] * inv_freq
    cos, sin = jnp.cos(ang), jnp.sin(ang)
    s = positions.shape[0]
    rest = HEAD_DIM - 2 * ROPE_HALF
    one, zero, zero8 = jnp.ones((s, rest), F32), jnp.zeros((s, rest), F32), jnp.zeros((s, ROPE_HALF), F32)
    c = jnp.concatenate([cos, cos, one], axis=1)
    s1 = jnp.concatenate([zero8, sin, zero], axis=1)
    s2 = jnp.concatenate([-sin, zero8, zero], axis=1)
    return tuple(jnp.tile(t, (1, 2)) for t in (c, s1, s2))


def _rope_fwd(x, c, s1, s2):
    return x * c + pltpu.roll(x, ROPE_HALF, 1) * s1 + pltpu.roll(x, LANE - ROPE_HALF, 1) * s2


def _rope_bwd(dy, c, s1, s2):
    return dy * c + pltpu.roll(dy * s1, LANE - ROPE_HALF, 1) + pltpu.roll(dy * s2, ROPE_HALF, 1)


def _log_sigmoid(x):
    return jnp.minimum(x, 0.0) - jnp.log(1.0 + jnp.exp(-jnp.abs(x)))


def _inproj(l, h, norm_g, w_all, b_f, qkg, bsum, rope, tm):
    s = h.shape[0]
    rc, rs1, rs2 = rope

    def body(h_ref, g_ref, w_ref, bf_ref, qkg_ref, bsum_ref, rc_ref, rs1_ref, rs2_ref,
             u_ref, z_ref, qa_ref, ka_ref, va_ref, qb_ref, kb_ref, vb_ref, cs_ref, ct_ref, vat_ref, vbt_ref, carry):
        @pl.when(pl.program_id(0) == 0)
        def _():
            carry[...] = jnp.zeros_like(carry)

        hh = h_ref[...]
        r = lax.rsqrt(jnp.mean(hh * hh, axis=-1, keepdims=True) + EPS)
        u = (hh * r * g_ref[...]).astype(BF16)
        u_ref[...] = u
        for k in range(N_ALL // LANE // 3):
            cols = slice(3 * LANE * k, 3 * LANE * (k + 1))
            z_ref[:, cols] = _dot(u, w_ref[:, cols])
        bs = bsum_ref[...]

        def head_norm(x, row):
            ms = _head_sums(x * x, bs) * (1.0 / HEAD_DIM)
            return x * lax.rsqrt(ms + EPS) * qkg_ref[row:row + 1, :]

        def seg(k):
            return z_ref[:, D_BRANCH * k:D_BRANCH * (k + 1)]

        qa_ref[...] = (head_norm(seg(0), 0) * (Q_SCALE * LOG2E)).astype(BF16)
        ka_ref[...] = head_norm(seg(1), 1).astype(BF16)
        va_ref[...] = seg(2).astype(BF16)
        vat_ref[...] = seg(2).T.astype(BF16)
        qn = head_norm(seg(4), 2) * (Q_SCALE * LOG2E)
        kn = head_norm(seg(5), 3)
        c, s1, s2 = rc_ref[...], rs1_ref[...], rs2_ref[...]
        for k in range(D_BRANCH // LANE):
            cols = slice(LANE * k, LANE * (k + 1))
            qb_ref[:, cols] = _rope_fwd(qn[:, cols], c, s1, s2).astype(BF16)
            kb_ref[:, cols] = _rope_fwd(kn[:, cols], c, s1, s2).astype(BF16)
        vb_ref[...] = seg(6).astype(BF16)
        vbt_ref[...] = seg(6).T.astype(BF16)
        logf = _log_sigmoid(z_ref[:, N_MAIN:N_ALL] + bf_ref[...])
        csum = _split3_dot(_tri(tm, False), logf) + carry[0:1, :]
        carry[...] = jnp.broadcast_to(csum[tm - 1:tm, :], carry.shape)
        csum = csum * LOG2E
        ct = csum.T
        for p in range(N_PAIRS):
            cs_ref[:, LANE * p:LANE * (p + 1)] = csum if p == 0 else pltpu.roll(csum, LANE - 2 * p, 1)
            ct_ref[p, :, :] = ct[2 * p:2 * p + 2, :]

    row = lambda w: pl.BlockSpec((tm, w), lambda i: (i, 0))
    colt = pl.BlockSpec((D_BRANCH, tm), lambda i: (0, i))
    bf = lambda: jax.ShapeDtypeStruct((s, D_BRANCH), BF16)
    bft = lambda: jax.ShapeDtypeStruct((D_BRANCH, s), BF16)
    return pl.pallas_call(
        body, name=f"inproj_l{l}", grid=(s // tm,),
        in_specs=[row(D_MODEL), _full((1, D_MODEL)), _full((D_MODEL, N_ALL)), _full((1, LANE)),
                  _full((8, D_BRANCH)), _full((HEADS_PER_BLOCK * HEAD_DIM,) * 2), row(LANE), row(LANE), row(LANE)],
        out_specs=[row(D_MODEL), row(N_ALL)] + [row(D_BRANCH)] * 6
        + [row(N_PAIRS * LANE), pl.BlockSpec((N_PAIRS, 2, tm), lambda i: (0, 0, i)), colt, colt],
        out_shape=[jax.ShapeDtypeStruct((s, D_MODEL), BF16), jax.ShapeDtypeStruct((s, N_ALL), F32),
                   bf(), bf(), bf(), bf(), bf(), bf(), jax.ShapeDtypeStruct((s, N_PAIRS * LANE), F32),
                   jax.ShapeDtypeStruct((N_PAIRS, 2, s), F32), bft(), bft()],
        scratch_shapes=[pltpu.VMEM((8, LANE), F32)],
        compiler_params=_params(1),
    )(h, norm_g, w_all, b_f, qkg, bsum, rc, rs1, rs2)


def _tri(t, upper):
    a = lax.broadcasted_iota(jnp.int32, (t, t), 0)
    b = lax.broadcasted_iota(jnp.int32, (t, t), 1)
    return jnp.where((b >= a) if upper else (b <= a), 1.0, 0.0).astype(BF16)


def _attn_fwd(name, fox, q, k, vt, extra, t, comm=None):
    s = q.shape[0]
    nq = s // t
    nb = DIL_BACK // t + 1

    def body(*refs):
        if fox:
            q_ref, k_ref, vt_ref, ccol_ref, crow_ref, o_ref, lse_ref, m_scr, l_scr, acc_scr = refs
        else:
            q_ref, k_ref, vt_ref, bias_ref, o_ref, lse_ref, m_scr, l_scr, acc_scr = refs
        i = pl.program_id(1)
        lane = lax.broadcasted_iota(jnp.int32, (t, LANE), 1)
        first = lane < HEAD_DIM
        qq = q_ref[...]
        zero = jnp.zeros_like(qq)
        qh = (jnp.where(first, qq, zero), jnp.where(first, zero, qq))
        m_scr[...] = jnp.full(m_scr.shape, M_INIT, F32)
        l_scr[...] = jnp.zeros_like(l_scr)
        acc_scr[...] = jnp.zeros_like(acc_scr)
        ones = jnp.ones((16, t), BF16)

        half = t // 2
        whole, lo, hi = slice(0, t), slice(0, half), slice(half, t)

        def block(j, ksl, qsl, causal):
            nk_, nq_ = ksl.stop - ksl.start, qsl.stop - qsl.start
            rows = pl.ds(pl.multiple_of(j * t + ksl.start, LANE), nk_)
            ks = k_ref[rows, :]
            vts = jnp.concatenate([vt_ref[:, rows], ones[:, :nk_]], axis=0)
            if fox:
                ccol = ccol_ref[rows, :]
            for h in range(2):
                st = _dot_nt(ks, qh[h][qsl, :])
                if fox:
                    st = st + (crow_ref[h:h + 1, qsl] - ccol[:, h:h + 1])
                    if causal:
                        ki = lax.broadcasted_iota(jnp.int32, (nk_, nq_), 0) + ksl.start
                        qi = lax.broadcasted_iota(jnp.int32, (nk_, nq_), 1) + qsl.start
                        st = jnp.where(ki <= qi, st, NEG)
                else:
                    st = st + bias_ref[i - j, ksl, qsl]
                m_old = m_scr[h, :, qsl]
                m_new = jnp.maximum(m_old, jnp.max(st, axis=0, keepdims=True))
                alpha = jnp.exp2(m_old - m_new)
                pb = jnp.exp2(st - m_new).astype(BF16)
                pv = _dot(vts, pb)
                l_scr[h, :, qsl] = alpha * l_scr[h, :, qsl] + pv[LANE:LANE + 1, :]
                acc_scr[h, :, qsl] = alpha * acc_scr[h, :, qsl] + pv[:LANE, :]
                m_scr[h, :, qsl] = m_new

        def full(j, c):
            block(j, whole, whole, False)
            return c

        if fox:
            lax.fori_loop(0, i, full, 0)
        else:
            @pl.when(i >= nb - 1)
            def _():
                block(i - (nb - 1), lo, lo, False)
                block(i - (nb - 1), hi, whole, False)

            lax.fori_loop(jnp.maximum(i - (nb - 2), 0), i, full, 0)
        if fox:
            block(i, whole, whole, True)
        else:
            block(i, lo, lo, False)
            block(i, whole, hi, False)

        sub = lax.broadcasted_iota(jnp.int32, (LANE, t), 0)
        ot = jnp.where(sub < HEAD_DIM, acc_scr[0] / l_scr[0], acc_scr[1] / l_scr[1])
        o_ref[...] = ot.T
        for h in range(2):
            lse_ref[h:h + 1, :] = m_scr[h] + jnp.log2(l_scr[h])

    qspec = pl.BlockSpec((t, LANE), lambda hp, i: (i, hp))
    kspec = pl.BlockSpec((s, LANE), lambda hp, i: (0, hp))
    vtspec = pl.BlockSpec((LANE, s), lambda hp, i: (hp, 0))
    in_specs = [qspec, kspec, vtspec]
    if fox:
        in_specs += [kspec, pl.BlockSpec((None, 2, t), lambda hp, i: (hp, 0, i))]
    else:
        in_specs += [_full((nb, t, t))]
    grid = (N_PAIRS, nq)
    c_in, c_ispec, c_ospec, c_oshape, c_scr = _comm_args(comm)
    return pl.pallas_call(
        _fuse_comm(body, len(in_specs), 2, comm, grid), name=name, grid=grid,
        in_specs=in_specs + c_ispec,
        out_specs=[qspec, pl.BlockSpec((None, 2, t), lambda hp, i: (hp, 0, i))] + c_ospec,
        out_shape=[jax.ShapeDtypeStruct((s, D_BRANCH), F32), jax.ShapeDtypeStruct((N_PAIRS, 2, s), F32)] + c_oshape,
        scratch_shapes=[pltpu.VMEM((2, 1, t), F32), pltpu.VMEM((2, 1, t), F32), pltpu.VMEM((2, LANE, t), F32)]
        + c_scr,
        compiler_params=_params(2),
    )(q, k, vt, *extra, *c_in)


def _attn_bwd(name, fox, q, k, v, do, lse_t, delta_t, pair_offset, extra, t, comm=None):
    s = q.shape[0]
    nk = s // t
    nb = DIL_BACK // t + 1

    def body(*refs):
        if fox:
            (q_ref, k_ref, v_ref, do_ref, lse_ref, delta_ref, ccol_ref, crow_ref,
             dqt_ref, dk_ref, dv_ref, dc_ref, drow_ref) = refs
        else:
            q_ref, k_ref, v_ref, do_ref, lse_ref, delta_ref, bias_ref, dqt_ref, dk_ref, dv_ref = refs
        j = pl.program_id(1)

        @pl.when(j == 0)
        def _():
            dqt_ref[...] = jnp.zeros_like(dqt_ref)
            if fox:
                drow_ref[...] = jnp.zeros_like(drow_ref)

        lane = lax.broadcasted_iota(jnp.int32, (t, LANE), 1)
        first = lane < HEAD_DIM
        ks = k_ref[...]
        vs = v_ref[...]
        kt = ks.astype(F32).T
        sub = lax.broadcasted_iota(jnp.int32, (LANE, t), 0)
        kth = (jnp.where(sub < HEAD_DIM, kt, 0.0).astype(BF16), jnp.where(sub < HEAD_DIM, 0.0, kt).astype(BF16))
        dk_ref[...] = jnp.zeros_like(dk_ref)
        dv_ref[...] = jnp.zeros_like(dv_ref)
        if fox:
            dc_ref[...] = jnp.zeros_like(dc_ref)
            ccol = ccol_ref[...]

        half = t // 2
        whole, lo, hi = slice(0, t), slice(0, half), slice(half, t)

        def block(i, ksl, qsl, causal):
            nk_, nq_ = ksl.stop - ksl.start, qsl.stop - qsl.start
            rows = pl.ds(pl.multiple_of(i * t + qsl.start, LANE), nq_)
            qq = q_ref[rows, :]
            dd = do_ref[rows, :]
            zero = jnp.zeros_like(qq)
            qh = (jnp.where(first[:nq_], qq, zero), jnp.where(first[:nq_], zero, qq))
            dh = (jnp.where(first[:nq_], dd, zero), jnp.where(first[:nq_], zero, dd))
            for h in range(2):
                st = _dot_nt(ks[ksl, :], qh[h])
                if fox:
                    st = st + (crow_ref[h:h + 1, rows] - ccol[ksl, h:h + 1])
                    if causal:
                        ki = lax.broadcasted_iota(jnp.int32, (nk_, nq_), 0) + ksl.start
                        qi = lax.broadcasted_iota(jnp.int32, (nk_, nq_), 1) + qsl.start
                        st = jnp.where(ki <= qi, st, NEG)
                else:
                    st = st + bias_ref[i - j, ksl, qsl]
                pt = jnp.exp2(st - lse_ref[h:h + 1, rows])
                dpt = _dot_nt(vs[ksl, :], dh[h])
                dst = pt * (dpt - delta_ref[h:h + 1, rows])
                dv_ref[ksl, :] += _dot(pt.astype(BF16), dh[h])
                dsb = dst.astype(BF16)
                dk_ref[ksl, :] += _dot(dsb, qh[h])
                dqt_ref[:, rows] += _dot(kth[h][:, ksl], dsb)
                if fox:
                    dc_ref[ksl, :] -= jnp.where(lane[:nk_] == h, jnp.sum(dst, axis=1, keepdims=True), 0.0)
                    drow_ref[h:h + 1, rows] += jnp.sum(dst, axis=0, keepdims=True)

        def full(i, c):
            block(i, whole, whole, False)
            return c

        block(j, lo, whole, True)
        block(j, hi, hi, True)
        if fox:
            lax.fori_loop(j + 1, nk, full, 0)
        else:
            lax.fori_loop(j + 1, jnp.minimum(j + nb - 1, nk), full, 0)

            @pl.when(j + nb - 1 < nk)
            def _():
                block(j + nb - 1, lo, lo, False)
                block(j + nb - 1, hi, whole, False)

    kspec = pl.BlockSpec((t, LANE), lambda hp, j: (j, hp))
    qspec = pl.BlockSpec((s, LANE), lambda hp, j: (0, hp))
    rowspec = pl.BlockSpec((None, 2, s), lambda hp, j: (hp, 0, 0))
    drowspec = pl.BlockSpec((None, 2, s), lambda hp, j: (hp + pair_offset, 0, 0))
    in_specs = [qspec, kspec, kspec, qspec, rowspec, drowspec]
    out_specs = [pl.BlockSpec((LANE, s), lambda hp, j: (hp, 0)), kspec, kspec]
    out_shape = [jax.ShapeDtypeStruct((D_BRANCH, s), F32), jax.ShapeDtypeStruct((s, D_BRANCH), F32),
                 jax.ShapeDtypeStruct((s, D_BRANCH), F32)]
    if fox:
        in_specs += [kspec, rowspec]
        out_specs += [kspec, rowspec]
        out_shape += [jax.ShapeDtypeStruct((s, N_PAIRS * LANE), F32), jax.ShapeDtypeStruct((N_PAIRS, 2, s), F32)]
    else:
        in_specs += [_full((nb, t, t))]
    grid = (N_PAIRS, nk)
    c_in, c_ispec, c_ospec, c_oshape, c_scr = _comm_args(comm)
    return pl.pallas_call(
        _fuse_comm(body, len(in_specs), len(out_specs), comm, grid), name=name, grid=grid,
        in_specs=in_specs + c_ispec, out_specs=out_specs + c_ospec, out_shape=out_shape + c_oshape,
        scratch_shapes=c_scr, compiler_params=_params(2),
    )(q, k, v, do, lse_t, delta_t, *extra, *c_in)


def _silu(x):
    return x * _sigmoid(x)


def _outproj(l, h, oa, ob, z, p_i, w_out, ple_g, w_gate, w_ple, tm, target=None):
    s = h.shape[0]
    last = target is not None

    def body(h_ref, oa_ref, ob_ref, ga_ref, gb_ref, p_ref, wo_ref, pg_ref, wg_ref, wp_ref, *rest):
        if last:
            t_ref, h2_ref, e_ref, gate_ref, out_ref, acc_ref = rest
        else:
            h2_ref, e_ref, gate_ref, out_ref = rest
        a = jnp.concatenate([oa_ref[...] * _silu(ga_ref[...]), ob_ref[...] * _silu(gb_ref[...])], axis=1)
        h2 = h_ref[...] + _dot(a.astype(BF16), wo_ref[...])
        h2_ref[...] = h2
        r = lax.rsqrt(jnp.mean(h2 * h2, axis=-1, keepdims=True) + EPS)
        n2 = (h2 * r * pg_ref[...]).astype(BF16)
        gate = _sigmoid(_dot(n2, wg_ref[...]))
        e = _dot(p_ref[...].astype(BF16), wp_ref[...])
        e_ref[...] = e
        gate_ref[...] = gate
        out = h2 + e * gate
        if not last:
            out_ref[...] = out
            return

        @pl.when(pl.program_id(0) == 0)
        def _():
            acc_ref[...] = jnp.zeros_like(acc_ref)

        err = out - t_ref[...]
        out_ref[...] = err * (1.0 / D_MODEL)
        e2 = err * err
        rows = e2[0:8, :]
        for k in range(1, tm // 8):
            rows = rows + e2[8 * k:8 * (k + 1), :]
        part = rows[:, 0:LANE]
        for k in range(1, D_MODEL // LANE):
            part = part + rows[:, LANE * k:LANE * (k + 1)]
        acc_ref[...] += part

    row = lambda w: pl.BlockSpec((tm, w), lambda i: (i, 0))
    zcol = lambda k: pl.BlockSpec((tm, D_BRANCH), lambda i: (i, k))
    f = lambda: jax.ShapeDtypeStruct((s, D_MODEL), F32)
    return pl.pallas_call(
        body, name=f"outproj_l{l}", grid=(s // tm,),
        in_specs=[row(D_MODEL), row(D_BRANCH), row(D_BRANCH), zcol(3), zcol(7), row(PLE_DIM),
                  _full((D_MODEL, D_MODEL)), _full((1, D_MODEL)), _full((D_MODEL, D_MODEL)),
                  _full((PLE_DIM, D_MODEL))] + ([row(D_MODEL)] if last else []),
        out_specs=[row(D_MODEL)] * 4 + ([_full((8, LANE))] if last else []),
        out_shape=[f(), f(), f(), f()] + ([jax.ShapeDtypeStruct((8, LANE), F32)] if last else []),
        compiler_params=_params(1),
    )(h, oa, ob, z, z, p_i, w_out, ple_g, w_gate, w_ple, *([target] if last else []))


def _outproj_bwd(l, dout, h2, e, gate, p_i, oa, ob, z, w_out_t, ple_g, w_gate_t, hsel, tm, comm=None):
    s = dout.shape[0]

    def body(do_ref, h2_ref, e_ref, gate_ref, p_ref, oa_ref, ob_ref, ga_ref, gb_ref, wot_ref, pg_ref,
             wgt_ref, hsel_ref,
             dh2_ref, doa_ref, dob_ref, dga_ref, dgb_ref, delta_ref, dwo_ref, dwg_ref, dwp_ref, dpg_ref):
        @pl.when(pl.program_id(0) == 0)
        def _():
            dwo_ref[...] = jnp.zeros_like(dwo_ref)
            dwg_ref[...] = jnp.zeros_like(dwg_ref)
            dwp_ref[...] = jnp.zeros_like(dwp_ref)
            dpg_ref[...] = jnp.zeros_like(dpg_ref)

        dho = do_ref[...]
        g = gate_ref[...]
        de = (dho * g).astype(BF16)
        dwp_ref[...] += _dot_tn(p_ref[...].astype(BF16), de)
        dpre = (dho * e_ref[...] * g * (1.0 - g)).astype(BF16)
        h2 = h2_ref[...]
        pg = pg_ref[...]
        r = lax.rsqrt(jnp.mean(h2 * h2, axis=-1, keepdims=True) + EPS)
        n2 = (h2 * r * pg).astype(BF16)
        dwg_ref[...] += _dot_tn(n2, dpre)
        dn2 = _dot(dpre, wgt_ref[...])
        dpg_ref[0:1, :] += jnp.sum(dn2 * h2 * r, axis=0, keepdims=True)
        wv = dn2 * pg
        dh2 = dho + r * wv - h2 * (r * r * r) * jnp.mean(wv * h2, axis=-1, keepdims=True)
        dh2_ref[...] = dh2
        dh2b = dh2.astype(BF16)
        ga, gb, oa, ob = ga_ref[...], gb_ref[...], oa_ref[...], ob_ref[...]
        sga, sgb = _sigmoid(ga), _sigmoid(gb)
        a = jnp.concatenate([oa * ga * sga, ob * gb * sgb], axis=1).astype(BF16)
        dwo_ref[...] += _dot_tn(a, dh2b)
        da = _dot(dh2b, wot_ref[...])
        da_a, da_b = da[:, :D_BRANCH], da[:, D_BRANCH:]
        doa = da_a * ga * sga
        dob = da_b * gb * sgb
        doa_ref[...] = doa.astype(BF16)
        dob_ref[...] = dob.astype(BF16)
        dga_ref[...] = (da_a * oa * sga * (1.0 + ga * (1.0 - sga))).astype(BF16)
        dgb_ref[...] = (da_b * ob * sgb * (1.0 + gb * (1.0 - sgb))).astype(BF16)
        prod = jnp.concatenate([doa * oa, dob * ob], axis=1)
        dt = _split_dot(prod, hsel_ref[...]).T
        for pp in range(2 * N_PAIRS):
            delta_ref[pp, :, :] = dt[2 * pp:2 * pp + 2, :]

    row = lambda w: pl.BlockSpec((tm, w), lambda i: (i, 0))
    zcol = lambda k: pl.BlockSpec((tm, D_BRANCH), lambda i: (i, k))
    grid = (s // tm,)
    c_in, c_ispec, c_ospec, c_oshape, c_scr = _comm_args(comm)
    return pl.pallas_call(
        _fuse_comm(body, 13, 10, comm, grid), name=f"outproj_bwd_l{l}", grid=grid,
        in_specs=[row(D_MODEL)] * 4 + [row(PLE_DIM), row(D_BRANCH), row(D_BRANCH), zcol(3), zcol(7),
                                        _full((D_MODEL, D_MODEL)), _full((1, D_MODEL)), _full((D_MODEL, D_MODEL)),
                                        _full((2 * D_BRANCH, LANE))] + c_ispec,
        out_specs=[row(D_MODEL)] + [row(D_BRANCH)] * 4
        + [pl.BlockSpec((2 * N_PAIRS, 2, tm), lambda i: (0, 0, i)), _full((D_MODEL, D_MODEL)),
           _full((D_MODEL, D_MODEL)), _full((PLE_DIM, D_MODEL)), _full((8, D_MODEL))] + c_ospec,
        out_shape=[jax.ShapeDtypeStruct((s, D_MODEL), F32)] + [jax.ShapeDtypeStruct((s, D_BRANCH), BF16)] * 4
        + [jax.ShapeDtypeStruct((2 * N_PAIRS, 2, s), F32), jax.ShapeDtypeStruct((D_MODEL, D_MODEL), F32),
           jax.ShapeDtypeStruct((D_MODEL, D_MODEL), F32), jax.ShapeDtypeStruct((PLE_DIM, D_MODEL), F32),
           jax.ShapeDtypeStruct((8, D_MODEL), F32)] + c_oshape,
        scratch_shapes=c_scr, compiler_params=_params(1),
    )(dout, h2, e, gate, p_i, oa, ob, z, z, w_out_t, ple_g, w_gate_t, hsel, *c_in)


def _inproj_bwd_prep(l, z, dqt_a, dk_a, dv_a, dqt_b, dk_b, dv_b, dga, dgb, dc_spread, dc_rows, b_f, qkg, bsum,
                     rope, tm):
    s = z.shape[0]
    n = s // tm
    rc, rs1, rs2 = rope

    def body(zqa_ref, zka_ref, zqb_ref, zkb_ref, zf_ref, dqta_ref, dka_ref, dva_ref, dqtb_ref, dkb_ref, dvb_ref,
             dga_ref, dgb_ref, dc_ref, drow_ref, bf_ref, qkg_ref, bsum_ref, rc_ref, rs1_ref, rs2_ref,
             dz_ref, dqkg_ref, dbf_ref, carry):
        @pl.when(pl.program_id(0) == 0)
        def _():
            dqkg_ref[...] = jnp.zeros_like(dqkg_ref)
            dbf_ref[...] = jnp.zeros_like(dbf_ref)
            carry[...] = jnp.zeros_like(carry)

        lane = lax.broadcasted_iota(jnp.int32, (tm, LANE), 1)
        dc = jnp.concatenate([drow_ref[p] for p in range(N_PAIRS)] + [jnp.zeros((LANE - N_HEADS, tm), F32)], axis=0).T
        for p in range(N_PAIRS):
            part = jnp.where(lane < 2, dc_ref[:, LANE * p:LANE * (p + 1)], 0.0)
            dc = dc + (part if p == 0 else pltpu.roll(part, 2 * p, 1))
        dlogf = _split3_dot(_tri(tm, True), dc) + carry[0:1, :]
        carry[...] = jnp.broadcast_to(dlogf[0:1, :], carry.shape)

        bs = bsum_ref[...]
        c, s1, s2 = rc_ref[...], rs1_ref[...], rs2_ref[...]

        def unrope(dy):
            return jnp.concatenate([_rope_bwd(dy[:, LANE * k:LANE * (k + 1)], c, s1, s2)
                                    for k in range(D_BRANCH // LANE)], axis=1)

        def norm_bwd(k, row, dy, x_ref):
            x = x_ref[...]
            r = lax.rsqrt(_head_sums(x * x, bs) * (1.0 / HEAD_DIM) + EPS)
            dqkg_ref[row:row + 1, :] += jnp.sum(dy * x * r, axis=0, keepdims=True)
            w = dy * qkg_ref[row:row + 1, :]
            dx = r * w - x * (r * r * r) * (_head_sums(w * x, bs) * (1.0 / HEAD_DIM))
            dz_ref[:, D_BRANCH * k:D_BRANCH * (k + 1)] = dx.astype(BF16)

        norm_bwd(0, 0, dqta_ref[...].T * Q_SCALE, zqa_ref)
        norm_bwd(1, 1, dka_ref[...] * LN2, zka_ref)
        dz_ref[:, 2 * D_BRANCH:3 * D_BRANCH] = dva_ref[...].astype(BF16)
        dz_ref[:, 3 * D_BRANCH:4 * D_BRANCH] = dga_ref[...]
        norm_bwd(4, 2, unrope(dqtb_ref[...].T * Q_SCALE), zqb_ref)
        norm_bwd(5, 3, unrope(dkb_ref[...] * LN2), zkb_ref)
        dz_ref[:, 6 * D_BRANCH:7 * D_BRANCH] = dvb_ref[...].astype(BF16)
        dz_ref[:, 7 * D_BRANCH:8 * D_BRANCH] = dgb_ref[...]
        dfa = dlogf * _sigmoid(-(zf_ref[...] + bf_ref[...]))
        dz_ref[:, N_MAIN:N_ALL] = dfa.astype(BF16)
        dbf_ref[0:1, :] += jnp.sum(dfa, axis=0, keepdims=True)

    row = lambda w: pl.BlockSpec((tm, w), lambda i: (n - 1 - i, 0))
    colt = pl.BlockSpec((D_BRANCH, tm), lambda i: (0, n - 1 - i))
    zcol = lambda k: pl.BlockSpec((tm, D_BRANCH), lambda i: (n - 1 - i, k))
    zf = pl.BlockSpec((tm, LANE), lambda i: (n - 1 - i, N_MAIN // LANE))
    return pl.pallas_call(
        body, name=f"inproj_bwd_prep_l{l}", grid=(n,),
        in_specs=[zcol(0), zcol(1), zcol(4), zcol(5), zf, colt, row(D_BRANCH), row(D_BRANCH), colt, row(D_BRANCH), row(D_BRANCH),
                  row(D_BRANCH), row(D_BRANCH), row(N_PAIRS * LANE),
                  pl.BlockSpec((N_PAIRS, 2, tm), lambda i: (0, 0, n - 1 - i)), _full((1, LANE)),
                  _full((8, D_BRANCH)), _full((HEADS_PER_BLOCK * HEAD_DIM,) * 2), row(LANE), row(LANE), row(LANE)],
        out_specs=[row(N_ALL), _full((8, D_BRANCH)), _full((8, LANE))],
        out_shape=[jax.ShapeDtypeStruct((s, N_ALL), BF16), jax.ShapeDtypeStruct((8, D_BRANCH), F32),
                   jax.ShapeDtypeStruct((8, LANE), F32)],
        scratch_shapes=[pltpu.VMEM((8, LANE), F32)],
        compiler_params=_params(1),
    )(z, z, z, z, z, dqt_a, dk_a, dv_a, dqt_b, dk_b, dv_b, dga, dgb, dc_spread, dc_rows, b_f, qkg, bsum, rc, rs1, rs2)


def _inproj_bwd_dx(l, dz, w_all_t, h, norm_g, dh2, tm, comm=None):
    s = dz.shape[0]

    def body(dz_ref, wt_ref, h_ref, g_ref, dh2_ref, dh_ref, dg_ref):
        @pl.when(pl.program_id(0) == 0)
        def _():
            dg_ref[...] = jnp.zeros_like(dg_ref)

        du = _dot(dz_ref[...], wt_ref[...])
        hh = h_ref[...]
        g = g_ref[...]
        r = lax.rsqrt(jnp.mean(hh * hh, axis=-1, keepdims=True) + EPS)
        dg_ref[0:1, :] += jnp.sum(du * hh * r, axis=0, keepdims=True)
        wv = du * g
        dh_ref[...] = dh2_ref[...] + r * wv - hh * (r * r * r) * jnp.mean(wv * hh, axis=-1, keepdims=True)

    row = lambda w: pl.BlockSpec((tm, w), lambda i: (i, 0))
    grid = (s // tm,)
    c_in, c_ispec, c_ospec, c_oshape, c_scr = _comm_args(comm)
    return pl.pallas_call(
        _fuse_comm(body, 5, 2, comm, grid), name=f"inproj_bwd_dx_l{l}", grid=grid,
        in_specs=[row(N_ALL), _full((N_ALL, D_MODEL)), row(D_MODEL), _full((1, D_MODEL)), row(D_MODEL)] + c_ispec,
        out_specs=[row(D_MODEL), _full((8, D_MODEL))] + c_ospec,
        out_shape=[jax.ShapeDtypeStruct((s, D_MODEL), F32), jax.ShapeDtypeStruct((8, D_MODEL), F32)] + c_oshape,
        scratch_shapes=c_scr, compiler_params=_params(1),
    )(dz, w_all_t, h, norm_g, dh2, *c_in)


def _inproj_bwd_dw(l, u, dz, tm, tn, comm=None):
    s = u.shape[0]

    def body(u_ref, dz_ref, dw_ref):
        @pl.when(pl.program_id(1) == 0)
        def _():
            dw_ref[...] = jnp.zeros_like(dw_ref)

        dw_ref[...] += _dot_tn(u_ref[...], dz_ref[...])

    grid = (N_ALL // tn, s // tm)
    c_in, c_ispec, c_ospec, c_oshape, c_scr = _comm_args(comm)
    return pl.pallas_call(
        _fuse_comm(body, 2, 1, comm, grid), name=f"inproj_bwd_dw_l{l}", grid=grid,
        in_specs=[pl.BlockSpec((tm, D_MODEL), lambda n, i: (i, 0)), pl.BlockSpec((tm, tn), lambda n, i: (i, n))]
        + c_ispec,
        out_specs=[pl.BlockSpec((D_MODEL, tn), lambda n, i: (0, n))] + c_ospec,
        out_shape=[jax.ShapeDtypeStruct((D_MODEL, N_ALL), F32)] + c_oshape,
        scratch_shapes=c_scr, compiler_params=_params(2),
    )(u, dz, *c_in)


def _adamw_math(w, g, m, v):
    m = ADAM_B1 * m + (1.0 - ADAM_B1) * g
    v = ADAM_B2 * v + (1.0 - ADAM_B2) * (g * g)
    m_hat = m / (1.0 - ADAM_B1 ** ADAM_STEP)
    v_hat = v / (1.0 - ADAM_B2 ** ADAM_STEP)
    delta = -ADAM_LR * (m_hat / (jnp.sqrt(v_hat) + ADAM_EPS) + ADAM_WD * w)
    return delta, m, v


def _adamw(name, w, halves, m, v, core):
    nl, r, c = w.shape
    hr = r // 2
    tr = 128 if hr % 128 == 0 else hr
    nb = hr // tr

    def body(core_ref, w_ref, own0_ref, oth0_ref, own1_ref, oth1_ref, m_ref, v_ref, g_ref, d_ref, nm_ref, nv_ref):
        first = pl.program_id(0) == 0
        own = jnp.where(first, own0_ref[...], own1_ref[...])
        oth = jnp.where(first, oth0_ref[...], oth1_ref[...])
        g = jnp.where(pl.program_id(1) // nb == core_ref[0], own, oth)
        d, nm, nv = _adamw_math(w_ref[...], g, m_ref[...], v_ref[...])
        g_ref[...] = g
        d_ref[...] = d
        nm_ref[...] = nm
        nv_ref[...] = nv

    spec = pl.BlockSpec((None, tr, c), lambda a, b, core_ref: (a, b, 0))
    gspec = pl.BlockSpec((tr, c), lambda a, b, core_ref: (b % nb, 0))
    shp = jax.ShapeDtypeStruct(w.shape, F32)
    return pl.pallas_call(
        body, name=name,
        grid_spec=pltpu.PrefetchScalarGridSpec(
            num_scalar_prefetch=1, grid=(nl, r // tr), in_specs=[spec] + [gspec] * 4 + [spec, spec],
            out_specs=[spec] * 4),
        out_shape=[shp, shp, shp, shp], compiler_params=_params(2),
    )(core, w, halves[0][0], halves[0][1], halves[1][0], halves[1][1], m, v)


def _assemble_halves(name, shape, halves, core):
    nl, r, c = shape
    hr = r // 2
    tr = 128 if hr % 128 == 0 else hr
    nb = hr // tr

    def body(core_ref, own0_ref, oth0_ref, own1_ref, oth1_ref, g_ref):
        first = pl.program_id(0) == 0
        own = jnp.where(first, own0_ref[...], own1_ref[...])
        oth = jnp.where(first, oth0_ref[...], oth1_ref[...])
        g_ref[...] = jnp.where(pl.program_id(1) // nb == core_ref[0], own, oth)

    of_layer = lambda l: pl.BlockSpec((tr, c), lambda a, b, core_ref: ((b % nb) * (a if l else 1 - a), 0))
    return pl.pallas_call(
        body, name=name,
        grid_spec=pltpu.PrefetchScalarGridSpec(
            num_scalar_prefetch=1, grid=(nl, r // tr), in_specs=[of_layer(0), of_layer(0), of_layer(1), of_layer(1)],
            out_specs=pl.BlockSpec((None, tr, c), lambda a, b, core_ref: (a, b, 0))),
        out_shape=jax.ShapeDtypeStruct(shape, F32), compiler_params=_params(2),
    )(core, halves[0][0], halves[0][1], halves[1][0], halves[1][1])


W_IN_FLAT_STEPS = 19


def _to_flat(a):
    nl, r, c = a.shape
    return jnp.transpose(jnp.transpose(a, (2, 0, 1)).reshape(c, nl, r // LANE, LANE), (0, 2, 1, 3)).reshape(-1, LANE)


def _from_flat(f, shape):
    nl, r, c = shape
    return jnp.transpose(jnp.transpose(f.reshape(c, r // LANE, nl, LANE), (0, 2, 1, 3)).reshape(c, nl, r), (1, 2, 0))


def _adamw_flat(name, w, g, m, v):
    n = w.shape[0]
    tr = n // W_IN_FLAT_STEPS

    def body(w_ref, g_ref, m_ref, v_ref, d_ref, nm_ref, nv_ref):
        d, nm, nv = _adamw_math(w_ref[...], g_ref[...], m_ref[...], v_ref[...])
        d_ref[...] = d
        nm_ref[...] = nm
        nv_ref[...] = nv

    spec = pl.BlockSpec((tr, LANE), lambda i: (i, 0))
    shp = jax.ShapeDtypeStruct(w.shape, F32)
    return pl.pallas_call(body, name=name, grid=(W_IN_FLAT_STEPS,), in_specs=[spec] * 4, out_specs=[spec] * 3,
                          out_shape=[shp, shp, shp], compiler_params=_params(1))(w, g, m, v)


def _pair_sum(name, g, x, c, narrow=False):
    n, r, cc = g.shape
    hr = r // 2
    tr = 128 if hr % 128 == 0 else hr
    nb = hr // tr

    def body(c_ref, g_ref, x_ref, o_ref, *narrow_ref):
        total = g_ref[...] + x_ref[...]
        o_ref[...] = total
        if narrow:
            narrow_ref[0][...] = total.astype(BF16)

    spec = pl.BlockSpec((None, tr, cc), lambda i, j, c_ref: (i, j, 0))
    shapes = [jax.ShapeDtypeStruct((n, hr, cc), F32)] + ([jax.ShapeDtypeStruct((n, hr, cc), BF16)] if narrow else [])
    return pl.pallas_call(
        body, name=name,
        grid_spec=pltpu.PrefetchScalarGridSpec(
            num_scalar_prefetch=1, grid=(n, nb),
            in_specs=[pl.BlockSpec((None, tr, cc), lambda i, j, c_ref: (i, c_ref[0] * nb + j, 0)), spec],
            out_specs=[spec] * len(shapes)),
        out_shape=shapes, compiler_params=_params(2),
    )(c, g, x)


def _sum_slots(name, own, landed, chip):
    n, r, c = own.shape
    tr = 128 if r % 128 == 0 else r

    def body(chip_ref, a_ref, b_ref, c_ref, d_ref, o_ref):
        o_ref[...] = ((a_ref[...] + b_ref[...].astype(F32)) + c_ref[...].astype(F32)) + d_ref[...].astype(F32)

    slot = lambda d: pl.BlockSpec((None, tr, c), lambda j, chip_ref: ((chip_ref[0] + d) % n, j, 0))
    return pl.pallas_call(
        body, name=name,
        grid_spec=pltpu.PrefetchScalarGridSpec(
            num_scalar_prefetch=1, grid=(r // tr,), in_specs=[slot(0), slot(1), slot(2), slot(3)],
            out_specs=pl.BlockSpec((tr, c), lambda j, chip_ref: (j, 0))),
        out_shape=jax.ShapeDtypeStruct((r, c), F32), compiler_params=_params(1),
    )(chip, own, landed, landed, landed)


def _me():
    return lax.axis_index("x"), lax.axis_index("y"), lax.axis_index("c")


def _other_chips(x, y):
    return [(1 - x, y), (x, 1 - y), (1 - x, 1 - y)]


def _dma_sems(*counts):
    return [pltpu.SemaphoreType.DMA((n,)) for n in counts]


def _half_rows(rows, which, align):
    return pl.ds(pl.multiple_of(which * (rows // 2), align), rows // 2)


def _gather_first_layer(w_in, others):
    n = len(others)
    rows = w_in.shape[1]

    def body(*refs):
        w_ref, o_refs = refs[0], refs[1:1 + n]
        out, keep = refs[1 + n], refs[2 + n:3 + 3 * n]
        stage, ici_send, ici_recv, d2d_send, d2d_recv, local_sem = refs[3 + 3 * n:]
        x, y, c = _me()
        k = 2 * x + y
        chips = _other_chips(x, y)
        stage[...] = w_ref[0].astype(BF16)
        local = pltpu.make_async_copy(stage, out.at[k], local_sem)
        local.start()
        mine = _half_rows(rows, c, 16)
        first, passed = [], []
        for j, (px, py) in enumerate(chips):
            cp = pltpu.make_async_remote_copy(
                src_ref=stage.at[mine], dst_ref=out.at[k, mine], send_sem=ici_send.at[j], recv_sem=ici_recv.at[j],
                device_id=(px, py, c), device_id_type=MESH)
            cp.start()
            first.append(cp)
        keep[0][...] = w_ref[1].astype(BF16)
        for t in range(n):
            for l in range(2):
                keep[1 + 2 * t + l][...] = o_refs[t][l].astype(BF16)
        for j, (px, py) in enumerate(chips):
            landed = out.at[2 * px + py, mine]
            first[j].wait_recv()
            cp = pltpu.make_async_remote_copy(
                src_ref=landed, dst_ref=landed, send_sem=d2d_send.at[j], recv_sem=d2d_recv.at[j],
                device_id=(x, y, 1 - c), device_id_type=MESH)
            cp.start()
            passed.append(cp)
        for cp in passed:
            cp.wait_recv()
        for cp in first + passed:
            cp.wait_send()
        local.wait()

    kept = [jax.ShapeDtypeStruct(w_in.shape[1:], BF16)]
    for o in others:
        kept += [jax.ShapeDtypeStruct(o.shape[1:], BF16)] * 2
    return pl.pallas_call(
        body, name="gather_first_layer",
        in_specs=[VMEM_SPEC] * (1 + n), out_specs=[ANY] + [VMEM_SPEC] * len(kept),
        out_shape=[jax.ShapeDtypeStruct((4,) + w_in.shape[1:], BF16)] + kept,
        scratch_shapes=[pltpu.VMEM(w_in.shape[1:], BF16)] + _dma_sems(3, 3, 3, 3) + [pltpu.SemaphoreType.DMA],
        compiler_params=pltpu.CompilerParams(vmem_limit_bytes=VMEM_LIMIT),
    )(w_in, *others)


def _run_comm(name, comm):
    nci, nco = len(comm.ins), len(comm.out_shapes)

    def body(*refs):
        copies = comm.make(refs[:nci], refs[nci:nci + nco], refs[nci + nco:])
        for cp in copies:
            cp.start()
        for cp in copies:
            cp.wait()

    return pl.pallas_call(body, name=name, in_specs=[ANY] * nci, out_specs=[ANY] * nco,
                          out_shape=list(comm.out_shapes), scratch_shapes=list(comm.sems))(*comm.ins)


def _gather_comm(mine):
    n = len(mine)

    def make(ins, outs, sems):
        send_sems, recv_sems, local_sems = sems
        x, y, c = _me()
        k = 2 * x + y
        copies = []
        for t in range(n):
            copies.append(pltpu.make_async_copy(ins[t], outs[t].at[k], local_sems.at[t]))
            for j, (px, py) in enumerate(_other_chips(x, y)):
                copies.append(pltpu.make_async_remote_copy(
                    src_ref=ins[t], dst_ref=outs[t].at[k], send_sem=send_sems.at[3 * t + j],
                    recv_sem=recv_sems.at[3 * t + j], device_id=(px, py, c), device_id_type=MESH))
        return copies

    return _Comm(mine, [jax.ShapeDtypeStruct((4,) + a.shape, a.dtype) for a in mine], _dma_sems(3 * n, 3 * n, n), make)


def _swap_comm(grads):
    n = len(grads)

    def make(ins, outs, sems):
        send_sems, recv_sems = sems
        x, y, c = _me()
        return [pltpu.make_async_remote_copy(
            src_ref=ins[t].at[:, _half_rows(grads[t].shape[1], 1 - c, 8)], dst_ref=outs[t],
            send_sem=send_sems.at[t], recv_sem=recv_sems.at[t], device_id=(x, y, 1 - c), device_id_type=MESH)
            for t in range(n)]

    shapes = [jax.ShapeDtypeStruct((g.shape[0], g.shape[1] // 2, g.shape[2]), F32) for g in grads]
    return _Comm(grads, shapes, _dma_sems(n, n), make)


def _scatter_comm(parts):
    n = len(parts)

    def make(ins, outs, sems):
        send_sems, recv_sems = sems
        x, y, c = _me()
        k = 2 * x + y
        return [pltpu.make_async_remote_copy(
            src_ref=ins[t].at[2 * px + py], dst_ref=outs[t].at[k], send_sem=send_sems.at[3 * t + j],
            recv_sem=recv_sems.at[3 * t + j], device_id=(px, py, c), device_id_type=MESH)
            for t in range(n) for j, (px, py) in enumerate(_other_chips(x, y))]

    return _Comm(parts, [jax.ShapeDtypeStruct(p.shape, p.dtype) for p in parts], _dma_sems(3 * n, 3 * n), make)


def _share_comm(totals):
    n = len(totals)

    def make(ins, outs, sems):
        send_sems, recv_sems = sems
        x, y, c = _me()
        return [pltpu.make_async_remote_copy(
            src_ref=ins[t], dst_ref=outs[t], send_sem=send_sems.at[t], recv_sem=recv_sems.at[t],
            device_id=(x, y, 1 - c), device_id_type=MESH) for t in range(n)]

    return _Comm(totals, [jax.ShapeDtypeStruct(t.shape, F32) for t in totals], _dma_sems(n, n), make)


def _small_allreduce_adamw(part, w, m, v):
    shape = part.shape

    def body(part_ref, w_ref, m_ref, v_ref, g_ref, d_ref, nm_ref, nv_ref, slots, send_sems, recv_sems):
        x, y, c = _me()
        me = 4 * x + 2 * y + c
        slots[me] = part_ref[...]
        copies = []
        for d in range(1, 8):
            peer = (x ^ (d >> 2), y ^ ((d >> 1) & 1), c ^ (d & 1))
            cp = pltpu.make_async_remote_copy(
                src_ref=part_ref, dst_ref=slots.at[me], send_sem=send_sems.at[d - 1], recv_sem=recv_sems.at[d - 1],
                device_id=peer, device_id_type=MESH)
            cp.start()
            copies.append(cp)
        for cp in copies:
            cp.wait()
        g = slots[0]
        for i in range(1, 8):
            g = g + slots[i]
        g_ref[...] = g
        d, nm, nv = _adamw_math(w_ref[...], g, m_ref[...], v_ref[...])
        d_ref[...] = d
        nm_ref[...] = nm
        nv_ref[...] = nv

    shp = jax.ShapeDtypeStruct(shape, F32)
    return pl.pallas_call(
        body, name="small_allreduce_adamw", in_specs=[VMEM_SPEC] * 4, out_specs=[VMEM_SPEC] * 4,
        out_shape=[shp, shp, shp, shp],
        scratch_shapes=[pltpu.VMEM((8,) + shape, F32), pltpu.SemaphoreType.DMA((7,)), pltpu.SemaphoreType.DMA((7,))],
    )(part, w, m, v)


TM = 512
T_FOX = 1024
T_DIL_FWD = 1024
T_DIL_BWD = 512
TN_DW = 1408


def _layer_fwd(l, h, p_i, w_inside, w_outside, consts, comm=None, target=None):
    w_all, _, norm_g, b_f, qkg = w_inside
    bsum, _, bias_t, _, rope = consts
    u, z, qa, ka, va, qb, kb, vb, c_spread, c_t, va_t, vb_t = _inproj(l, h, norm_g, w_all, b_f, qkg, bsum, rope, TM)
    oa, lse_a, *landed = _attn_fwd(f"fox_fwd_l{l}", True, qa, ka, va_t, (c_spread, c_t), T_FOX, comm)
    ob, lse_b = _attn_fwd(f"dil_fwd_l{l}", False, qb, kb, vb_t, (bias_t,), T_DIL_FWD)
    if comm is not None:
        w_outside = w_outside(landed)
    w_out, _, w_gate, _, w_ple, ple_g = w_outside
    h2, e, gate, *out = _outproj(l, h, oa, ob, z, p_i, w_out, ple_g, w_gate, w_ple, TM, target)
    out = out[0] if target is None else tuple(out)
    saved = (h, u, z, qa, ka, va, qb, kb, vb, c_spread, c_t, oa, lse_a, ob, lse_b, h2, e, gate)
    return out, saved, w_outside


def _reduce_names(tag, n):
    return [f"reduce_{tag}_{i}" for i in range(n)]


def _layer_bwd(l, dout, p_i, wts, consts, saved, pending=None, core=None, chip=None):
    (_, w_all_t, norm_g, b_f, qkg), (_, w_out_t, _, w_gate_t, _, ple_g) = wts
    bsum, hsel, _, bias_t, rope = consts
    h, u, z, qa, ka, va, qb, kb, vb, c_spread, c_t, oa, lse_a, ob, lse_b, h2, e, gate = saved
    fused = pending is not None
    dh2, doa, dob, dga, dgb, delta_t, dw_out, dw_gate, dw_ple, dple_g = _outproj_bwd(
        l, dout, h2, e, gate, p_i, oa, ob, z, w_out_t, ple_g, w_gate_t, hsel, TM)
    if fused:
        group = list(pending) + list(_slot_layout_out(dw_out, dw_ple, dw_gate))
        n = len(group)
    dqt_a, dk_a, dv_a, dc, drow, *sib = _attn_bwd(
        f"fox_bwd_l{l}", True, qa, ka, va, doa, lse_a, delta_t, 0, (c_spread, c_t), T_FOX,
        _swap_comm(group) if fused else None)
    if fused:
        pair = [_pair_sum(nm, g, x, core)[0] for nm, g, x in zip(_reduce_names("pair_hidden", n), group, sib)]
    dqt_b, dk_b, dv_b, *landed = _attn_bwd(f"dil_bwd_l{l}", False, qb, kb, vb, dob, lse_b, delta_t, N_PAIRS,
                                             (bias_t,), T_DIL_BWD, _scatter_comm(pair) if fused else None)
    if fused:
        totals = [_sum_slots(nm, a, y, chip) for nm, a, y in zip(_reduce_names("chips_hidden", n), pair, landed)]
    dz, dqkg, dbf = _inproj_bwd_prep(l, z, dqt_a, dk_a, dv_a, dqt_b, dk_b, dv_b, dga, dgb, dc, drow, b_f, qkg,
                                     bsum, rope, TM)
    reduced = None
    if fused:
        dw_all, *other = _inproj_bwd_dw(l, u, dz, TM, TN_DW, _share_comm(totals))
        own = [_slot_layout_in(dw_all)]
        sib = _run_comm("reduce_swap_last", _swap_comm(own))
        wide, narrow = _pair_sum("reduce_pair_last", own[0], sib[0], core, narrow=True)
        dh, dnorm_g, landed = _inproj_bwd_dx(l, dz, w_all_t, h, norm_g, dh2, TM, _scatter_comm([narrow]))
        total = _sum_slots("reduce_chips_last", wide, landed, chip)
        reduced = list(zip(totals, other)) + [(total, _run_comm("reduce_share_last", _share_comm([total]))[0])]
    else:
        dh, dnorm_g = _inproj_bwd_dx(l, dz, w_all_t, h, norm_g, dh2, TM)
        dw_all, = _inproj_bwd_dw(l, u, dz, TM, TN_DW)
    return dh, (dw_all, dw_out, dw_ple, dw_gate, dnorm_g[0], dbf[0, :N_HEADS], dqkg[:4], dple_g[0]), reduced


N_FA = 2048


W_SHARD = N_IN // 4


def _shard_pieces(lo, hi):
    cuts = [(k, max(lo, k * W_SHARD), min(hi, (k + 1) * W_SHARD)) for k in range(4)]
    return [(k, a - k * W_SHARD, b - k * W_SHARD) for k, a, b in cuts if a < b]


def _in_weights(l, g_in, norm_g, b_f, qk_norm_g):
    order = _shard_pieces(0, N_FA) + _shard_pieces(N_FA + N_HEADS, N_IN) + _shard_pieces(N_FA, N_FA + N_HEADS)
    w_all = jnp.concatenate([g_in[k, :, a:b] for k, a, b in order]
                            + [jnp.zeros((D_MODEL, LANE - N_HEADS), g_in.dtype)], axis=1)
    qkg = jnp.pad(jnp.tile(qk_norm_g[l], (1, N_HEADS)), ((0, 4), (0, 0)))
    bf = jnp.pad(b_f[l], (0, LANE - N_HEADS))[None, :]
    return w_all, w_all.T, norm_g[l][None, :], bf, qkg


def _out_weights(l, g_out, g_ple, g_gate, ple_norm_g):
    w_out = g_out.reshape(D_MODEL, D_MODEL)
    w_gate = g_gate.reshape(D_MODEL, D_MODEL)
    w_ple = jnp.transpose(g_ple, (1, 0, 2)).reshape(PLE_DIM, D_MODEL)
    return w_out, w_out.T, w_gate, w_gate.T, w_ple, ple_norm_g[l][None, :]


def _slot_layout_in(dw_all):
    regions = ((0, N_FA, 0), (N_FA, N_FA + N_HEADS, N_MAIN - N_FA), (N_FA + N_HEADS, N_IN, -N_HEADS))

    def shard(k):
        cuts = [(max(k * W_SHARD, a) + shift, min((k + 1) * W_SHARD, b) + shift) for a, b, shift in regions]
        return jnp.concatenate([dw_all[:, a:b] for a, b in cuts if a < b], axis=1)

    return jnp.stack([shard(k) for k in range(4)])


def _slot_layout_out(dw_out, dw_ple, dw_gate):
    return (dw_out.reshape(4, D_MODEL // 4, D_MODEL),
            jnp.transpose(dw_ple.reshape(PLE_DIM, 4, D_MODEL // 4), (1, 0, 2)),
            dw_gate.reshape(4, D_MODEL // 4, D_MODEL))


SMALL_ROWS = 40


def _pack_small(norm_g, ple_norm_g, qk_norm_g, b_f, last=0.0):
    flat = jnp.concatenate([norm_g.reshape(-1), ple_norm_g.reshape(-1), qk_norm_g.reshape(-1), b_f.reshape(-1)])
    flat = jnp.pad(flat, (0, SMALL_ROWS * LANE - flat.shape[0] - 1))
    return jnp.concatenate([flat, jnp.reshape(last, (1,)).astype(F32)]).reshape(SMALL_ROWS, LANE)


def _unpack_small(packed):
    flat = packed.reshape(-1)
    n1, n2, n3 = 2 * D_MODEL, 4 * D_MODEL, 4 * D_MODEL + 2 * 4 * HEAD_DIM
    return (flat[:n1].reshape(2, D_MODEL), flat[n1:n2].reshape(2, D_MODEL), flat[n2:n3].reshape(2, 4, HEAD_DIM),
            flat[n3:n3 + 2 * N_HEADS].reshape(2, N_HEADS))


def kernel(x, p, positions, norm_g, w_in, b_f, qk_norm_g, w_out, w_ple, ple_norm_g, w_ple_gate, loss_target,
           m_norm_g, m_w_in, m_b_f, m_qk_norm_g, m_w_out, m_w_ple, m_ple_norm_g, m_w_ple_gate,
           v_norm_g, v_w_in, v_b_f, v_qk_norm_g, v_w_out, v_w_ple, v_ple_norm_g, v_w_ple_gate):
    assert w_in.shape[0] == 2, "the schedule below is written for two layers"
    w_in0, *kept = _gather_first_layer(w_in, [w_out, w_ple, w_ple_gate])
    consts = (_head_block_diag(), _head_select(), _dil_bias(T_DIL_FWD), _dil_bias(T_DIL_BWD),
              _rope_tables(positions[0]))
    later = {}

    def outside0(landed):
        later["w_in1"] = landed[0]
        later["out1"] = landed[2::2]
        return _out_weights(0, *landed[1::2], ple_norm_g)

    inside0 = _in_weights(0, w_in0, norm_g, b_f, qk_norm_g)
    h1, saved0, outside0 = _layer_fwd(0, x[0], p[0, 0], inside0, outside0, consts, _gather_comm(kept))
    wts0 = (inside0, outside0)
    wts1 = (_in_weights(1, later["w_in1"], norm_g, b_f, qk_norm_g), _out_weights(1, *later["out1"], ple_norm_g))
    (dh, sq), saved1, _ = _layer_fwd(1, h1, p[1, 0], *wts1, consts, target=loss_target[0])

    core = lax.axis_index("c").astype(jnp.int32).reshape(1)
    chip = (2 * lax.axis_index("x") + lax.axis_index("y")).astype(jnp.int32).reshape(1)
    dh, grads1, _ = _layer_bwd(1, dh, p[1, 0], wts1, consts, saved1)
    pending = (_slot_layout_in(grads1[0]),) + _slot_layout_out(*grads1[1:4])
    dh, grads0, hidden = _layer_bwd(0, dh, p[0, 0], wts0, consts, saved0, pending, core, chip)
    reduced0 = [hidden[7]] + hidden[4:7]
    reduced1 = hidden[0:4]
    grad_x = dh[None]
    small = [grads0[4:], grads1[4:]]
    n_layers = 2

    outs = {}
    for t, (name, w, m, v) in enumerate((("w_in", w_in, m_w_in, v_w_in), ("w_out", w_out, m_w_out, v_w_out),
                                         ("w_ple", w_ple, m_w_ple, v_w_ple),
                                         ("w_ple_gate", w_ple_gate, m_w_ple_gate, v_w_ple_gate))):
        if name == "w_in":
            g = _assemble_halves("assemble_w_in", w.shape, (reduced0[t], reduced1[t]), core)
            g_flat = _to_flat(g)
            flat = _adamw_flat("adamw_w_in", _to_flat(w), g_flat, _to_flat(m), _to_flat(v))
            outs[name] = tuple(_from_flat(f, w.shape) for f in (g_flat,) + tuple(flat))
        else:
            outs[name] = tuple(_adamw(f"adamw_{name}", w, (reduced0[t], reduced1[t]), m, v, core))

    part = _pack_small(jnp.stack([s[0] for s in small]), jnp.stack([s[3] for s in small]),
                       jnp.stack([s[2] for s in small]).reshape(n_layers, 4, N_HEADS, HEAD_DIM).sum(axis=2),
                       jnp.stack([s[1] for s in small]), 0.5 / D_MODEL * jnp.sum(sq))
    packed = _small_allreduce_adamw(part, _pack_small(norm_g, ple_norm_g, qk_norm_g, b_f),
                                    _pack_small(m_norm_g, m_ple_norm_g, m_qk_norm_g, m_b_f),
                                    _pack_small(v_norm_g, v_ple_norm_g, v_qk_norm_g, v_b_f))
    loss = packed[0][SMALL_ROWS - 1, LANE - 1]
    sm = [_unpack_small(a) for a in packed]
    for i, name in enumerate(("norm_g", "ple_norm_g", "qk_norm_g", "b_f")):
        outs[name] = tuple(sm[j][i] for j in range(4))

    order = ("norm_g", "w_in", "b_f", "qk_norm_g", "w_out", "w_ple", "ple_norm_g", "w_ple_gate")
    return (loss, grad_x) + tuple(outs[n][j] for j in range(4) for n in order)
```

```python
import functools
from typing import Any, Callable, NamedTuple, Sequence

import numpy as np
import jax
import jax.numpy as jnp
from jax import lax
from jax.experimental import pallas as pl
from jax.experimental.pallas import tpu as pltpu

F32 = jnp.float32
BF16 = jnp.bfloat16
MESH = pl.DeviceIdType.MESH

D_MODEL = 1024
HEAD_DIM = 64
D_BRANCH = 512
N_HEADS = 8
N_PAIRS = 4
N_IN = 4104
N_MAIN = 4096
N_ALL = 4224
PLE_DIM = 256
ROPE_THETA = 500000.0
ROPE_HALF = 8
EPS = 1e-6
NEG = -1e30
M_INIT = -1e29
Q_SCALE = HEAD_DIM ** -0.5
LOG2E = 1.4426950408889634
LN2 = 0.6931471805599453
DIL_PATTERNS = ((128, 1), (512, 4), (2048, 16))
DIL_BACK = 2048
ADAM_LR, ADAM_B1, ADAM_B2, ADAM_EPS, ADAM_WD, ADAM_STEP = 0.001, 0.9, 0.999, 1e-08, 0.01, 10
VMEM_LIMIT = 56 * 1024 * 1024
LANE = 128


def _dot(a, b):
    return jnp.dot(a, b, preferred_element_type=F32)


def _dot_nt(a, b):
    return lax.dot_general(a, b, (((1,), (1,)), ((), ())), preferred_element_type=F32)


def _dot_tn(a, b):
    return lax.dot_general(a, b, (((0,), (0,)), ((), ())), preferred_element_type=F32)


def _split_dot(x, w):
    hi = x.astype(BF16)
    lo = (x - hi.astype(F32)).astype(BF16)
    return _dot(hi, w) + _dot(lo, w)


def _head_sums(x, bs):
    w = bs.shape[0]
    return jnp.concatenate([_split_dot(x[:, w * k:w * (k + 1)], bs) for k in range(x.shape[1] // w)], axis=1)


def _split3_dot(w, x):
    hi = x.astype(BF16)
    r1 = x - hi.astype(F32)
    mid = r1.astype(BF16)
    lo = (r1 - mid.astype(F32)).astype(BF16)
    return _dot(w, hi) + _dot(w, mid) + _dot(w, lo)


def _sigmoid(x):
    return 1.0 / (1.0 + jnp.exp(-x))


def _params(n_grid):
    return pltpu.CompilerParams(dimension_semantics=("arbitrary",) * n_grid,
                                vmem_limit_bytes=VMEM_LIMIT)


def _full(shape):
    nd = len(shape)
    return pl.BlockSpec(shape, lambda *_: (0,) * nd)


ANY = pl.BlockSpec(memory_space=pl.ANY)
VMEM_SPEC = pl.BlockSpec(memory_space=pltpu.VMEM)


class _Comm(NamedTuple):
    ins: Sequence[Any]
    out_shapes: Sequence[Any]
    sems: Sequence[Any]
    make: Callable[..., Any]


def _fuse_comm(body, n_in, n_out, comm, grid):
    if comm is None:
        return body
    nci, nco, ncs = len(comm.ins), len(comm.out_shapes), len(comm.sems)

    def fused(*refs):
        a, b = n_in + nci, n_in + nci + n_out
        ins, cins, outs, couts = refs[:n_in], refs[n_in:a], refs[a:b], refs[b:b + nco]
        scratch, sems = refs[b + nco:len(refs) - ncs], refs[len(refs) - ncs:]
        first = functools.reduce(jnp.logical_and, [pl.program_id(d) == 0 for d in range(len(grid))])
        last = functools.reduce(jnp.logical_and, [pl.program_id(d) == n - 1 for d, n in enumerate(grid)])

        @pl.when(first)
        def _():
            for cp in comm.make(cins, couts, sems):
                cp.start()

        body(*ins, *outs, *scratch)

        @pl.when(last)
        def _():
            for cp in comm.make(cins, couts, sems):
                cp.wait()

    return fused


def _comm_args(comm):
    if comm is None:
        return [], [], [], [], []
    return (list(comm.ins), [ANY] * len(comm.ins), [ANY] * len(comm.out_shapes), list(comm.out_shapes),
            list(comm.sems))


HEADS_PER_BLOCK = 4


def _head_block_diag():
    i = np.arange(HEADS_PER_BLOCK * HEAD_DIM)
    return jnp.asarray((i[:, None] // HEAD_DIM == i[None, :] // HEAD_DIM).astype(np.float32), BF16)


def _head_select():
    i = np.arange(2 * D_BRANCH)
    j = np.arange(LANE)
    return jnp.asarray((i[:, None] // HEAD_DIM == j[None, :]).astype(np.float32), BF16)


def _dil_bias(t):
    nb = DIL_BACK // t + 1
    qi = np.arange(t)[:, None]
    ki = np.arange(t)[None, :]
    tiles = []
    for r in range(nb):
        d = r * t + qi - ki
        mult = np.zeros((t, t), np.int64)
        for window, dil in DIL_PATTERNS:
            mult += ((d >= 0) & (d <= window) & (d % dil == 0)).astype(np.int64)
        b = np.where(mult > 0, np.log2(np.maximum(mult, 1)), NEG).astype(np.float32)
        tiles.append(b.T)
    return jnp.asarray(np.stack(tiles))


def _rope_tables(positions):
    inv_freq = ROPE_THETA ** (-jnp.arange(ROPE_HALF, dtype=F32) / ROPE_HALF)
    ang = positions.astype(F32)[:, None] * inv_freq
    cos, sin = jnp.cos(ang), jnp.sin(ang)
    s = positions.shape[0]
    rest = HEAD_DIM - 2 * ROPE_HALF
    one, zero, zero8 = jnp.ones((s, rest), F32), jnp.zeros((s, rest), F32), jnp.zeros((s, ROPE_HALF), F32)
    c = jnp.concatenate([cos, cos, one], axis=1)
    s1 = jnp.concatenate([zero8, sin, zero], axis=1)
    s2 = jnp.concatenate([-sin, zero8, zero], axis=1)
    return tuple(jnp.tile(t, (1, 2)) for t in (c, s1, s2))


def _rope_fwd(x, c, s1, s2):
    return x * c + pltpu.roll(x, ROPE_HALF, 1) * s1 + pltpu.roll(x, LANE - ROPE_HALF, 1) * s2


def _rope_bwd(dy, c, s1, s2):
    return dy * c + pltpu.roll(dy * s1, LANE - ROPE_HALF, 1) + pltpu.roll(dy * s2, ROPE_HALF, 1)


def _log_sigmoid(x):
    return jnp.minimum(x, 0.0) - jnp.log(1.0 + jnp.exp(-jnp.abs(x)))


def _inproj(l, h, norm_g, w_all, b_f, qkg, bsum, rope, tm):
    s = h.shape[0]
    rc, rs1, rs2 = rope

    def body(h_ref, g_ref, w_ref, bf_ref, qkg_ref, bsum_ref, rc_ref, rs1_ref, rs2_ref,
             u_ref, z_ref, qa_ref, ka_ref, va_ref, qb_ref, kb_ref, vb_ref, cs_ref, ct_ref, vat_ref, vbt_ref,
             qat_ref, qbt_ref, carry):
        @pl.when(pl.program_id(0) == 0)
        def _():
            carry[...] = jnp.zeros_like(carry)

        hh = h_ref[...]
        r = lax.rsqrt(jnp.mean(hh * hh, axis=-1, keepdims=True) + EPS)
        u = (hh * r * g_ref[...]).astype(BF16)
        u_ref[...] = u
        for k in range(N_ALL // LANE // 3):
            cols = slice(3 * LANE * k, 3 * LANE * (k + 1))
            z_ref[:, cols] = _dot(u, w_ref[:, cols])
        bs = bsum_ref[...]

        def head_norm(x, row):
            ms = _head_sums(x * x, bs) * (1.0 / HEAD_DIM)
            return x * lax.rsqrt(ms + EPS) * qkg_ref[row:row + 1, :]

        def seg(k):
            return z_ref[:, D_BRANCH * k:D_BRANCH * (k + 1)]

        qa = head_norm(seg(0), 0) * (Q_SCALE * LOG2E)
        qa_ref[...] = qa.astype(BF16)
        qat_ref[...] = qa.T.astype(BF16)
        ka_ref[...] = head_norm(seg(1), 1).astype(BF16)
        va_ref[...] = seg(2).astype(BF16)
        vat_ref[...] = seg(2).T.astype(BF16)
        qn = head_norm(seg(4), 2) * (Q_SCALE * LOG2E)
        kn = head_norm(seg(5), 3)
        c, s1, s2 = rc_ref[...], rs1_ref[...], rs2_ref[...]
        for k in range(D_BRANCH // LANE):
            cols = slice(LANE * k, LANE * (k + 1))
            qr = _rope_fwd(qn[:, cols], c, s1, s2)
            qb_ref[:, cols] = qr.astype(BF16)
            qbt_ref[cols, :] = qr.T.astype(BF16)
            kb_ref[:, cols] = _rope_fwd(kn[:, cols], c, s1, s2).astype(BF16)
        vb_ref[...] = seg(6).astype(BF16)
        vbt_ref[...] = seg(6).T.astype(BF16)
        logf = _log_sigmoid(z_ref[:, N_MAIN:N_ALL] + bf_ref[...])
        csum = _split3_dot(_tri(tm, False), logf) + carry[0:1, :]
        carry[...] = jnp.broadcast_to(csum[tm - 1:tm, :], carry.shape)
        csum = csum * LOG2E
        ct = csum.T
        for p in range(N_PAIRS):
            cs_ref[:, LANE * p:LANE * (p + 1)] = csum if p == 0 else pltpu.roll(csum, LANE - 2 * p, 1)
            ct_ref[p, :, :] = ct[2 * p:2 * p + 2, :]

    row = lambda w: pl.BlockSpec((tm, w), lambda i: (i, 0))
    colt = pl.BlockSpec((D_BRANCH, tm), lambda i: (0, i))
    bf = lambda: jax.ShapeDtypeStruct((s, D_BRANCH), BF16)
    bft = lambda: jax.ShapeDtypeStruct((D_BRANCH, s), BF16)
    return pl.pallas_call(
        body, name=f"inproj_l{l}", grid=(s // tm,),
        in_specs=[row(D_MODEL), _full((1, D_MODEL)), _full((D_MODEL, N_ALL)), _full((1, LANE)),
                  _full((8, D_BRANCH)), _full((HEADS_PER_BLOCK * HEAD_DIM,) * 2), row(LANE), row(LANE), row(LANE)],
        out_specs=[row(D_MODEL), row(N_ALL)] + [row(D_BRANCH)] * 6
        + [row(N_PAIRS * LANE), pl.BlockSpec((N_PAIRS, 2, tm), lambda i: (0, 0, i)), colt, colt, colt, colt],
        out_shape=[jax.ShapeDtypeStruct((s, D_MODEL), BF16), jax.ShapeDtypeStruct((s, N_ALL), F32),
                   bf(), bf(), bf(), bf(), bf(), bf(), jax.ShapeDtypeStruct((s, N_PAIRS * LANE), F32),
                   jax.ShapeDtypeStruct((N_PAIRS, 2, s), F32), bft(), bft(), bft(), bft()],
        scratch_shapes=[pltpu.VMEM((8, LANE), F32)],
        compiler_params=_params(1),
    )(h, norm_g, w_all, b_f, qkg, bsum, rc, rs1, rs2)


def _tri(t, upper):
    a = lax.broadcasted_iota(jnp.int32, (t, t), 0)
    b = lax.broadcasted_iota(jnp.int32, (t, t), 1)
    return jnp.where((b >= a) if upper else (b <= a), 1.0, 0.0).astype(BF16)


def _attn_fwd(name, fox, qt, k, vt, extra, t, comm=None):
    s = k.shape[0]
    nq = s // t
    nb = DIL_BACK // t + 1

    def body(*refs):
        if fox:
            q_ref, k_ref, vt_ref, ccol_ref, crow_ref, o_ref, lse_ref, m_scr, l_scr, acc_scr = refs
        else:
            q_ref, k_ref, vt_ref, bias_ref, o_ref, lse_ref, m_scr, l_scr, acc_scr = refs
        i = pl.program_id(1)
        sub = lax.broadcasted_iota(jnp.int32, (LANE, t), 0)
        first = sub < HEAD_DIM
        qq = q_ref[...]
        zero = jnp.zeros_like(qq)
        qh = (jnp.where(first, qq, zero), jnp.where(first, zero, qq))
        m_scr[...] = jnp.full(m_scr.shape, M_INIT, F32)
        l_scr[...] = jnp.zeros_like(l_scr)
        acc_scr[...] = jnp.zeros_like(acc_scr)
        ones = jnp.ones((16, t), BF16)

        half = t // 2
        whole, lo, hi = slice(0, t), slice(0, half), slice(half, t)

        def block(j, ksl, qsl, causal):
            nk_, nq_ = ksl.stop - ksl.start, qsl.stop - qsl.start
            rows = pl.ds(pl.multiple_of(j * t + ksl.start, LANE), nk_)
            ks = k_ref[rows, :]
            vts = jnp.concatenate([vt_ref[:, rows], ones[:, :nk_]], axis=0)
            if fox:
                ccol = ccol_ref[rows, :]
            for h in range(2):
                st = _dot(ks, qh[h][:, qsl])
                if fox:
                    st = st + (crow_ref[h:h + 1, qsl] - ccol[:, h:h + 1])
                    if causal:
                        ki = lax.broadcasted_iota(jnp.int32, (nk_, nq_), 0) + ksl.start
                        qi = lax.broadcasted_iota(jnp.int32, (nk_, nq_), 1) + qsl.start
                        st = jnp.where(ki <= qi, st, NEG)
                else:
                    st = st + bias_ref[i - j, ksl, qsl]
                m_old = m_scr[h, :, qsl]
                m_new = jnp.maximum(m_old, jnp.max(st, axis=0, keepdims=True))
                alpha = jnp.exp2(m_old - m_new)
                pb = jnp.exp2(st - m_new).astype(BF16)
                pv = _dot(vts, pb)
                l_scr[h, :, qsl] = alpha * l_scr[h, :, qsl] + pv[LANE:LANE + 1, :]
                acc_scr[h, :, qsl] = alpha * acc_scr[h, :, qsl] + pv[:LANE, :]
                m_scr[h, :, qsl] = m_new

        def full(j, c):
            block(j, whole, whole, False)
            return c

        if fox:
            lax.fori_loop(0, i, full, 0)
        else:
            @pl.when(i >= nb - 1)
            def _():
                block(i - (nb - 1), lo, lo, False)
                block(i - (nb - 1), hi, whole, False)

            lax.fori_loop(jnp.maximum(i - (nb - 2), 0), i, full, 0)
        if fox:
            block(i, whole, whole, True)
        else:
            block(i, lo, lo, False)
            block(i, whole, hi, False)

        ot = jnp.where(first, acc_scr[0] / l_scr[0], acc_scr[1] / l_scr[1])
        o_ref[...] = ot.T
        for h in range(2):
            lse_ref[h:h + 1, :] = m_scr[h] + jnp.log2(l_scr[h])

    qspec = pl.BlockSpec((t, LANE), lambda hp, i: (i, hp))
    qtspec = pl.BlockSpec((LANE, t), lambda hp, i: (hp, i))
    kspec = pl.BlockSpec((s, LANE), lambda hp, i: (0, hp))
    vtspec = pl.BlockSpec((LANE, s), lambda hp, i: (hp, 0))
    in_specs = [qtspec, kspec, vtspec]
    if fox:
        in_specs += [kspec, pl.BlockSpec((None, 2, t), lambda hp, i: (hp, 0, i))]
    else:
        in_specs += [_full((nb, t, t))]
    grid = (N_PAIRS, nq)
    c_in, c_ispec, c_ospec, c_oshape, c_scr = _comm_args(comm)
    return pl.pallas_call(
        _fuse_comm(body, len(in_specs), 2, comm, grid), name=name, grid=grid,
        in_specs=in_specs + c_ispec,
        out_specs=[qspec, pl.BlockSpec((None, 2, t), lambda hp, i: (hp, 0, i))] + c_ospec,
        out_shape=[jax.ShapeDtypeStruct((s, D_BRANCH), F32), jax.ShapeDtypeStruct((N_PAIRS, 2, s), F32)] + c_oshape,
        scratch_shapes=[pltpu.VMEM((2, 1, t), F32), pltpu.VMEM((2, 1, t), F32), pltpu.VMEM((2, LANE, t), F32)]
        + c_scr,
        compiler_params=_params(2),
    )(qt, k, vt, *extra, *c_in)


def _attn_bwd(name, fox, q, k, v, do, lse_t, delta_t, pair_offset, extra, t, comm=None):
    s = q.shape[0]
    nk = s // t
    nb = DIL_BACK // t + 1

    def body(*refs):
        if fox:
            (q_ref, k_ref, v_ref, do_ref, lse_ref, delta_ref, ccol_ref, crow_ref,
             dqt_ref, dk_ref, dv_ref, dc_ref, drow_ref) = refs
        else:
            q_ref, k_ref, v_ref, do_ref, lse_ref, delta_ref, bias_ref, dqt_ref, dk_ref, dv_ref = refs
        j = pl.program_id(1)

        @pl.when(j == 0)
        def _():
            dqt_ref[...] = jnp.zeros_like(dqt_ref)
            if fox:
                drow_ref[...] = jnp.zeros_like(drow_ref)

        lane = lax.broadcasted_iota(jnp.int32, (t, LANE), 1)
        first = lane < HEAD_DIM
        ks = k_ref[...]
        vs = v_ref[...]
        kt = ks.astype(F32).T
        sub = lax.broadcasted_iota(jnp.int32, (LANE, t), 0)
        kth = (jnp.where(sub < HEAD_DIM, kt, 0.0).astype(BF16), jnp.where(sub < HEAD_DIM, 0.0, kt).astype(BF16))
        dk_ref[...] = jnp.zeros_like(dk_ref)
        dv_ref[...] = jnp.zeros_like(dv_ref)
        if fox:
            dc_ref[...] = jnp.zeros_like(dc_ref)
            ccol = ccol_ref[...]

        half = t // 2
        whole, lo, hi = slice(0, t), slice(0, half), slice(half, t)

        def block(i, ksl, qsl, causal):
            nk_, nq_ = ksl.stop - ksl.start, qsl.stop - qsl.start
            rows = pl.ds(pl.multiple_of(i * t + qsl.start, LANE), nq_)
            qq = q_ref[rows, :]
            dd = do_ref[rows, :]
            zero = jnp.zeros_like(qq)
            qh = (jnp.where(first[:nq_], qq, zero), jnp.where(first[:nq_], zero, qq))
            dh = (jnp.where(first[:nq_], dd, zero), jnp.where(first[:nq_], zero, dd))
            for h in range(2):
                st = _dot_nt(ks[ksl, :], qh[h])
                if fox:
                    st = st + (crow_ref[h:h + 1, rows] - ccol[ksl, h:h + 1])
                    if causal:
                        ki = lax.broadcasted_iota(jnp.int32, (nk_, nq_), 0) + ksl.start
                        qi = lax.broadcasted_iota(jnp.int32, (nk_, nq_), 1) + qsl.start
                        st = jnp.where(ki <= qi, st, NEG)
                else:
                    st = st + bias_ref[i - j, ksl, qsl]
                pt = jnp.exp2(st - lse_ref[h:h + 1, rows])
                dpt = _dot_nt(vs[ksl, :], dh[h])
                dst = pt * (dpt - delta_ref[h:h + 1, rows])
                dv_ref[ksl, :] += _dot(pt.astype(BF16), dh[h])
                dsb = dst.astype(BF16)
                dk_ref[ksl, :] += _dot(dsb, qh[h])
                dqt_ref[:, rows] += _dot(kth[h][:, ksl], dsb)
                if fox:
                    dc_ref[ksl, :] -= jnp.where(lane[:nk_] == h, jnp.sum(dst, axis=1, keepdims=True), 0.0)
                    drow_ref[h:h + 1, rows] += jnp.sum(dst, axis=0, keepdims=True)

        def full(i, c):
            block(i, whole, whole, False)
            return c

        block(j, lo, whole, True)
        block(j, hi, hi, True)
        if fox:
            lax.fori_loop(j + 1, nk, full, 0)
        else:
            lax.fori_loop(j + 1, jnp.minimum(j + nb - 1, nk), full, 0)

            @pl.when(j + nb - 1 < nk)
            def _():
                block(j + nb - 1, lo, lo, False)
                block(j + nb - 1, hi, whole, False)

    kspec = pl.BlockSpec((t, LANE), lambda hp, j: (j, hp))
    qspec = pl.BlockSpec((s, LANE), lambda hp, j: (0, hp))
    rowspec = pl.BlockSpec((None, 2, s), lambda hp, j: (hp, 0, 0))
    drowspec = pl.BlockSpec((None, 2, s), lambda hp, j: (hp + pair_offset, 0, 0))
    in_specs = [qspec, kspec, kspec, qspec, rowspec, drowspec]
    out_specs = [pl.BlockSpec((LANE, s), lambda hp, j: (hp, 0)), kspec, kspec]
    out_shape = [jax.ShapeDtypeStruct((D_BRANCH, s), F32), jax.ShapeDtypeStruct((s, D_BRANCH), F32),
                 jax.ShapeDtypeStruct((s, D_BRANCH), F32)]
    if fox:
        in_specs += [kspec, rowspec]
        out_specs += [kspec, rowspec]
        out_shape += [jax.ShapeDtypeStruct((s, N_PAIRS * LANE), F32), jax.ShapeDtypeStruct((N_PAIRS, 2, s), F32)]
    else:
        in_specs += [_full((nb, t, t))]
    grid = (N_PAIRS, nk)
    c_in, c_ispec, c_ospec, c_oshape, c_scr = _comm_args(comm)
    return pl.pallas_call(
        _fuse_comm(body, len(in_specs), len(out_specs), comm, grid), name=name, grid=grid,
        in_specs=in_specs + c_ispec, out_specs=out_specs + c_ospec, out_shape=out_shape + c_oshape,
        scratch_shapes=c_scr, compiler_params=_params(2),
    )(q, k, v, do, lse_t, delta_t, *extra, *c_in)


def _silu(x):
    return x * _sigmoid(x)


def _outproj(l, h, oa, ob, z, p_i, w_out, ple_g, w_gate, w_ple, tm, target=None):
    s = h.shape[0]
    last = target is not None

    def body(h_ref, oa_ref, ob_ref, ga_ref, gb_ref, p_ref, wo_ref, pg_ref, wg_ref, wp_ref, *rest):
        if last:
            t_ref, h2_ref, e_ref, gate_ref, out_ref, acc_ref = rest
        else:
            h2_ref, e_ref, gate_ref, out_ref = rest
        a = jnp.concatenate([oa_ref[...] * _silu(ga_ref[...]), ob_ref[...] * _silu(gb_ref[...])], axis=1)
        h2 = h_ref[...] + _dot(a.astype(BF16), wo_ref[...])
        h2_ref[...] = h2
        r = lax.rsqrt(jnp.mean(h2 * h2, axis=-1, keepdims=True) + EPS)
        n2 = (h2 * r * pg_ref[...]).astype(BF16)
        gate = _sigmoid(_dot(n2, wg_ref[...]))
        e = _dot(p_ref[...].astype(BF16), wp_ref[...])
        e_ref[...] = e
        gate_ref[...] = gate
        out = h2 + e * gate
        if not last:
            out_ref[...] = out
            return

        @pl.when(pl.program_id(0) == 0)
        def _():
            acc_ref[...] = jnp.zeros_like(acc_ref)

        err = out - t_ref[...]
        out_ref[...] = err * (1.0 / D_MODEL)
        e2 = err * err
        rows = e2[0:8, :]
        for k in range(1, tm // 8):
            rows = rows + e2[8 * k:8 * (k + 1), :]
        part = rows[:, 0:LANE]
        for k in range(1, D_MODEL // LANE):
            part = part + rows[:, LANE * k:LANE * (k + 1)]
        acc_ref[...] += part

    row = lambda w: pl.BlockSpec((tm, w), lambda i: (i, 0))
    zcol = lambda k: pl.BlockSpec((tm, D_BRANCH), lambda i: (i, k))
    f = lambda: jax.ShapeDtypeStruct((s, D_MODEL), F32)
    return pl.pallas_call(
        body, name=f"outproj_l{l}", grid=(s // tm,),
        in_specs=[row(D_MODEL), row(D_BRANCH), row(D_BRANCH), zcol(3), zcol(7), row(PLE_DIM),
                  _full((D_MODEL, D_MODEL)), _full((1, D_MODEL)), _full((D_MODEL, D_MODEL)),
                  _full((PLE_DIM, D_MODEL))] + ([row(D_MODEL)] if last else []),
        out_specs=[row(D_MODEL)] * 4 + ([_full((8, LANE))] if last else []),
        out_shape=[f(), f(), f(), f()] + ([jax.ShapeDtypeStruct((8, LANE), F32)] if last else []),
        compiler_params=_params(1),
    )(h, oa, ob, z, z, p_i, w_out, ple_g, w_gate, w_ple, *([target] if last else []))


def _outproj_bwd(l, dout, h2, e, gate, p_i, oa, ob, z, w_out_t, ple_g, w_gate_t, hsel, tm, comm=None):
    s = dout.shape[0]

    def body(do_ref, h2_ref, e_ref, gate_ref, p_ref, oa_ref, ob_ref, ga_ref, gb_ref, wot_ref, pg_ref,
             wgt_ref, hsel_ref,
             dh2_ref, doa_ref, dob_ref, dga_ref, dgb_ref, delta_ref, dwo_ref, dwg_ref, dwp_ref, dpg_ref):
        @pl.when(pl.program_id(0) == 0)
        def _():
            dwo_ref[...] = jnp.zeros_like(dwo_ref)
            dwg_ref[...] = jnp.zeros_like(dwg_ref)
            dwp_ref[...] = jnp.zeros_like(dwp_ref)
            dpg_ref[...] = jnp.zeros_like(dpg_ref)

        dho = do_ref[...]
        g = gate_ref[...]
        de = (dho * g).astype(BF16)
        dwp_ref[...] += _dot_tn(p_ref[...].astype(BF16), de)
        dpre = (dho * e_ref[...] * g * (1.0 - g)).astype(BF16)
        h2 = h2_ref[...]
        pg = pg_ref[...]
        r = lax.rsqrt(jnp.mean(h2 * h2, axis=-1, keepdims=True) + EPS)
        n2 = (h2 * r * pg).astype(BF16)
        dwg_ref[...] += _dot_tn(n2, dpre)
        dn2 = _dot(dpre, wgt_ref[...])
        dpg_ref[0:1, :] += jnp.sum(dn2 * h2 * r, axis=0, keepdims=True)
        wv = dn2 * pg
        dh2 = dho + r * wv - h2 * (r * r * r) * jnp.mean(wv * h2, axis=-1, keepdims=True)
        dh2_ref[...] = dh2
        dh2b = dh2.astype(BF16)
        ga, gb, oa, ob = ga_ref[...], gb_ref[...], oa_ref[...], ob_ref[...]
        sga, sgb = _sigmoid(ga), _sigmoid(gb)
        a = jnp.concatenate([oa * ga * sga, ob * gb * sgb], axis=1).astype(BF16)
        dwo_ref[...] += _dot_tn(a, dh2b)
        da = _dot(dh2b, wot_ref[...])
        da_a, da_b = da[:, :D_BRANCH], da[:, D_BRANCH:]
        doa = da_a * ga * sga
        dob = da_b * gb * sgb
        doa_ref[...] = doa.astype(BF16)
        dob_ref[...] = dob.astype(BF16)
        dga_ref[...] = (da_a * oa * sga * (1.0 + ga * (1.0 - sga))).astype(BF16)
        dgb_ref[...] = (da_b * ob * sgb * (1.0 + gb * (1.0 - sgb))).astype(BF16)
        prod = jnp.concatenate([doa * oa, dob * ob], axis=1)
        dt = _split_dot(prod, hsel_ref[...]).T
        for pp in range(2 * N_PAIRS):
            delta_ref[pp, :, :] = dt[2 * pp:2 * pp + 2, :]

    row = lambda w: pl.BlockSpec((tm, w), lambda i: (i, 0))
    zcol = lambda k: pl.BlockSpec((tm, D_BRANCH), lambda i: (i, k))
    grid = (s // tm,)
    c_in, c_ispec, c_ospec, c_oshape, c_scr = _comm_args(comm)
    return pl.pallas_call(
        _fuse_comm(body, 13, 10, comm, grid), name=f"outproj_bwd_l{l}", grid=grid,
        in_specs=[row(D_MODEL)] * 4 + [row(PLE_DIM), row(D_BRANCH), row(D_BRANCH), zcol(3), zcol(7),
                                        _full((D_MODEL, D_MODEL)), _full((1, D_MODEL)), _full((D_MODEL, D_MODEL)),
                                        _full((2 * D_BRANCH, LANE))] + c_ispec,
        out_specs=[row(D_MODEL)] + [row(D_BRANCH)] * 4
        + [pl.BlockSpec((2 * N_PAIRS, 2, tm), lambda i: (0, 0, i)), _full((D_MODEL, D_MODEL)),
           _full((D_MODEL, D_MODEL)), _full((PLE_DIM, D_MODEL)), _full((8, D_MODEL))] + c_ospec,
        out_shape=[jax.ShapeDtypeStruct((s, D_MODEL), F32)] + [jax.ShapeDtypeStruct((s, D_BRANCH), BF16)] * 4
        + [jax.ShapeDtypeStruct((2 * N_PAIRS, 2, s), F32), jax.ShapeDtypeStruct((D_MODEL, D_MODEL), F32),
           jax.ShapeDtypeStruct((D_MODEL, D_MODEL), F32), jax.ShapeDtypeStruct((PLE_DIM, D_MODEL), F32),
           jax.ShapeDtypeStruct((8, D_MODEL), F32)] + c_oshape,
        scratch_shapes=c_scr, compiler_params=_params(1),
    )(dout, h2, e, gate, p_i, oa, ob, z, z, w_out_t, ple_g, w_gate_t, hsel, *c_in)


def _inproj_bwd_prep(l, z, dqt_a, dk_a, dv_a, dqt_b, dk_b, dv_b, dga, dgb, dc_spread, dc_rows, b_f, qkg, bsum,
                     rope, tm):
    s = z.shape[0]
    n = s // tm
    rc, rs1, rs2 = rope

    def body(zqa_ref, zka_ref, zqb_ref, zkb_ref, zf_ref, dqta_ref, dka_ref, dva_ref, dqtb_ref, dkb_ref, dvb_ref,
             dga_ref, dgb_ref, dc_ref, drow_ref, bf_ref, qkg_ref, bsum_ref, rc_ref, rs1_ref, rs2_ref,
             dz_ref, dqkg_ref, dbf_ref, carry):
        @pl.when(pl.program_id(0) == 0)
        def _():
            dqkg_ref[...] = jnp.zeros_like(dqkg_ref)
            dbf_ref[...] = jnp.zeros_like(dbf_ref)
            carry[...] = jnp.zeros_like(carry)

        lane = lax.broadcasted_iota(jnp.int32, (tm, LANE), 1)
        dc = jnp.concatenate([drow_ref[p] for p in range(N_PAIRS)] + [jnp.zeros((LANE - N_HEADS, tm), F32)], axis=0).T
        for p in range(N_PAIRS):
            part = jnp.where(lane < 2, dc_ref[:, LANE * p:LANE * (p + 1)], 0.0)
            dc = dc + (part if p == 0 else pltpu.roll(part, 2 * p, 1))
        dlogf = _split3_dot(_tri(tm, True), dc) + carry[0:1, :]
        carry[...] = jnp.broadcast_to(dlogf[0:1, :], carry.shape)

        bs = bsum_ref[...]
        c, s1, s2 = rc_ref[...], rs1_ref[...], rs2_ref[...]

        def unrope(dy):
            return jnp.concatenate([_rope_bwd(dy[:, LANE * k:LANE * (k + 1)], c, s1, s2)
                                    for k in range(D_BRANCH // LANE)], axis=1)

        def norm_bwd(k, row, dy, x_ref):
            x = x_ref[...]
            r = lax.rsqrt(_head_sums(x * x, bs) * (1.0 / HEAD_DIM) + EPS)
            dqkg_ref[row:row + 1, :] += jnp.sum(dy * x * r, axis=0, keepdims=True)
            w = dy * qkg_ref[row:row + 1, :]
            dx = r * w - x * (r * r * r) * (_head_sums(w * x, bs) * (1.0 / HEAD_DIM))
            dz_ref[:, D_BRANCH * k:D_BRANCH * (k + 1)] = dx.astype(BF16)

        norm_bwd(0, 0, dqta_ref[...].T * Q_SCALE, zqa_ref)
        norm_bwd(1, 1, dka_ref[...] * LN2, zka_ref)
        dz_ref[:, 2 * D_BRANCH:3 * D_BRANCH] = dva_ref[...].astype(BF16)
        dz_ref[:, 3 * D_BRANCH:4 * D_BRANCH] = dga_ref[...]
        norm_bwd(4, 2, unrope(dqtb_ref[...].T * Q_SCALE), zqb_ref)
        norm_bwd(5, 3, unrope(dkb_ref[...] * LN2), zkb_ref)
        dz_ref[:, 6 * D_BRANCH:7 * D_BRANCH] = dvb_ref[...].astype(BF16)
        dz_ref[:, 7 * D_BRANCH:8 * D_BRANCH] = dgb_ref[...]
        dfa = dlogf * _sigmoid(-(zf_ref[...] + bf_ref[...]))
        dz_ref[:, N_MAIN:N_ALL] = dfa.astype(BF16)
        dbf_ref[0:1, :] += jnp.sum(dfa, axis=0, keepdims=True)

    row = lambda w: pl.BlockSpec((tm, w), lambda i: (n - 1 - i, 0))
    colt = pl.BlockSpec((D_BRANCH, tm), lambda i: (0, n - 1 - i))
    zcol = lambda k: pl.BlockSpec((tm, D_BRANCH), lambda i: (n - 1 - i, k))
    zf = pl.BlockSpec((tm, LANE), lambda i: (n - 1 - i, N_MAIN // LANE))
    return pl.pallas_call(
        body, name=f"inproj_bwd_prep_l{l}", grid=(n,),
        in_specs=[zcol(0), zcol(1), zcol(4), zcol(5), zf, colt, row(D_BRANCH), row(D_BRANCH), colt, row(D_BRANCH), row(D_BRANCH),
                  row(D_BRANCH), row(D_BRANCH), row(N_PAIRS * LANE),
                  pl.BlockSpec((N_PAIRS, 2, tm), lambda i: (0, 0, n - 1 - i)), _full((1, LANE)),
                  _full((8, D_BRANCH)), _full((HEADS_PER_BLOCK * HEAD_DIM,) * 2), row(LANE), row(LANE), row(LANE)],
        out_specs=[row(N_ALL), _full((8, D_BRANCH)), _full((8, LANE))],
        out_shape=[jax.ShapeDtypeStruct((s, N_ALL), BF16), jax.ShapeDtypeStruct((8, D_BRANCH), F32),
                   jax.ShapeDtypeStruct((8, LANE), F32)],
        scratch_shapes=[pltpu.VMEM((8, LANE), F32)],
        compiler_params=_params(1),
    )(z, z, z, z, z, dqt_a, dk_a, dv_a, dqt_b, dk_b, dv_b, dga, dgb, dc_spread, dc_rows, b_f, qkg, bsum, rc, rs1, rs2)


def _inproj_bwd_dx(l, dz, w_all_t, h, norm_g, dh2, tm, comm=None):
    s = dz.shape[0]

    def body(dz_ref, wt_ref, h_ref, g_ref, dh2_ref, dh_ref, dg_ref):
        @pl.when(pl.program_id(0) == 0)
        def _():
            dg_ref[...] = jnp.zeros_like(dg_ref)

        du = _dot(dz_ref[...], wt_ref[...])
        hh = h_ref[...]
        g = g_ref[...]
        r = lax.rsqrt(jnp.mean(hh * hh, axis=-1, keepdims=True) + EPS)
        dg_ref[0:1, :] += jnp.sum(du * hh * r, axis=0, keepdims=True)
        wv = du * g
        dh_ref[...] = dh2_ref[...] + r * wv - hh * (r * r * r) * jnp.mean(wv * hh, axis=-1, keepdims=True)

    row = lambda w: pl.BlockSpec((tm, w), lambda i: (i, 0))
    grid = (s // tm,)
    c_in, c_ispec, c_ospec, c_oshape, c_scr = _comm_args(comm)
    return pl.pallas_call(
        _fuse_comm(body, 5, 2, comm, grid), name=f"inproj_bwd_dx_l{l}", grid=grid,
        in_specs=[row(N_ALL), _full((N_ALL, D_MODEL)), row(D_MODEL), _full((1, D_MODEL)), row(D_MODEL)] + c_ispec,
        out_specs=[row(D_MODEL), _full((8, D_MODEL))] + c_ospec,
        out_shape=[jax.ShapeDtypeStruct((s, D_MODEL), F32), jax.ShapeDtypeStruct((8, D_MODEL), F32)] + c_oshape,
        scratch_shapes=c_scr, compiler_params=_params(1),
    )(dz, w_all_t, h, norm_g, dh2, *c_in)


def _inproj_bwd_dw(l, u, dz, tm, tn, comm=None):
    s = u.shape[0]

    def body(u_ref, dz_ref, dw_ref):
        @pl.when(pl.program_id(1) == 0)
        def _():
            dw_ref[...] = jnp.zeros_like(dw_ref)

        dw_ref[...] += _dot_tn(u_ref[...], dz_ref[...])

    grid = (N_ALL // tn, s // tm)
    c_in, c_ispec, c_ospec, c_oshape, c_scr = _comm_args(comm)
    return pl.pallas_call(
        _fuse_comm(body, 2, 1, comm, grid), name=f"inproj_bwd_dw_l{l}", grid=grid,
        in_specs=[pl.BlockSpec((tm, D_MODEL), lambda n, i: (i, 0)), pl.BlockSpec((tm, tn), lambda n, i: (i, n))]
        + c_ispec,
        out_specs=[pl.BlockSpec((D_MODEL, tn), lambda n, i: (0, n))] + c_ospec,
        out_shape=[jax.ShapeDtypeStruct((D_MODEL, N_ALL), F32)] + c_oshape,
        scratch_shapes=c_scr, compiler_params=_params(2),
    )(u, dz, *c_in)


def _adamw_math(w, g, m, v):
    m = ADAM_B1 * m + (1.0 - ADAM_B1) * g
    v = ADAM_B2 * v + (1.0 - ADAM_B2) * (g * g)
    m_hat = m / (1.0 - ADAM_B1 ** ADAM_STEP)
    v_hat = v / (1.0 - ADAM_B2 ** ADAM_STEP)
    delta = -ADAM_LR * (m_hat / (jnp.sqrt(v_hat) + ADAM_EPS) + ADAM_WD * w)
    return delta, m, v


def _adamw(name, w, halves, m, v, core):
    nl, r, c = w.shape
    hr = r // 2
    tr = 128 if hr % 128 == 0 else hr
    nb = hr // tr

    def body(core_ref, w_ref, own0_ref, oth0_ref, own1_ref, oth1_ref, m_ref, v_ref, g_ref, d_ref, nm_ref, nv_ref):
        first = pl.program_id(0) == 0
        own = jnp.where(first, own0_ref[...], own1_ref[...])
        oth = jnp.where(first, oth0_ref[...], oth1_ref[...])
        g = jnp.where(pl.program_id(1) // nb == core_ref[0], own, oth)
        d, nm, nv = _adamw_math(w_ref[...], g, m_ref[...], v_ref[...])
        g_ref[...] = g
        d_ref[...] = d
        nm_ref[...] = nm
        nv_ref[...] = nv

    spec = pl.BlockSpec((None, tr, c), lambda a, b, core_ref: (a, b, 0))
    gspec = pl.BlockSpec((tr, c), lambda a, b, core_ref: (b % nb, 0))
    shp = jax.ShapeDtypeStruct(w.shape, F32)
    return pl.pallas_call(
        body, name=name,
        grid_spec=pltpu.PrefetchScalarGridSpec(
            num_scalar_prefetch=1, grid=(nl, r // tr), in_specs=[spec] + [gspec] * 4 + [spec, spec],
            out_specs=[spec] * 4),
        out_shape=[shp, shp, shp, shp], compiler_params=_params(2),
    )(core, w, halves[0][0], halves[0][1], halves[1][0], halves[1][1], m, v)


def _assemble_halves(name, shape, halves, core):
    nl, r, c = shape
    hr = r // 2
    tr = 128 if hr % 128 == 0 else hr
    nb = hr // tr

    def body(core_ref, own0_ref, oth0_ref, own1_ref, oth1_ref, g_ref):
        first = pl.program_id(0) == 0
        own = jnp.where(first, own0_ref[...], own1_ref[...])
        oth = jnp.where(first, oth0_ref[...], oth1_ref[...])
        g_ref[...] = jnp.where(pl.program_id(1) // nb == core_ref[0], own, oth)

    of_layer = lambda l: pl.BlockSpec((tr, c), lambda a, b, core_ref: ((b % nb) * (a if l else 1 - a), 0))
    return pl.pallas_call(
        body, name=name,
        grid_spec=pltpu.PrefetchScalarGridSpec(
            num_scalar_prefetch=1, grid=(nl, r // tr), in_specs=[of_layer(0), of_layer(0), of_layer(1), of_layer(1)],
            out_specs=pl.BlockSpec((None, tr, c), lambda a, b, core_ref: (a, b, 0))),
        out_shape=jax.ShapeDtypeStruct(shape, F32), compiler_params=_params(2),
    )(core, halves[0][0], halves[0][1], halves[1][0], halves[1][1])


W_IN_FLAT_STEPS = 19


def _to_flat(a):
    nl, r, c = a.shape
    return jnp.transpose(jnp.transpose(a, (2, 0, 1)).reshape(c, nl, r // LANE, LANE), (0, 2, 1, 3)).reshape(-1, LANE)


def _from_flat(f, shape):
    nl, r, c = shape
    return jnp.transpose(jnp.transpose(f.reshape(c, r // LANE, nl, LANE), (0, 2, 1, 3)).reshape(c, nl, r), (1, 2, 0))


def _adamw_flat(name, w, g, m, v):
    n = w.shape[0]
    tr = n // W_IN_FLAT_STEPS

    def body(w_ref, g_ref, m_ref, v_ref, d_ref, nm_ref, nv_ref):
        d, nm, nv = _adamw_math(w_ref[...], g_ref[...], m_ref[...], v_ref[...])
        d_ref[...] = d
        nm_ref[...] = nm
        nv_ref[...] = nv

    spec = pl.BlockSpec((tr, LANE), lambda i: (i, 0))
    shp = jax.ShapeDtypeStruct(w.shape, F32)
    return pl.pallas_call(body, name=name, grid=(W_IN_FLAT_STEPS,), in_specs=[spec] * 4, out_specs=[spec] * 3,
                          out_shape=[shp, shp, shp], compiler_params=_params(1))(w, g, m, v)


def _pair_sum(name, g, x, c, narrow=False):
    n, r, cc = g.shape
    hr = r // 2
    tr = 128 if hr % 128 == 0 else hr
    nb = hr // tr

    def body(c_ref, g_ref, x_ref, o_ref, *narrow_ref):
        total = g_ref[...] + x_ref[...]
        o_ref[...] = total
        if narrow:
            narrow_ref[0][...] = total.astype(BF16)

    spec = pl.BlockSpec((None, tr, cc), lambda i, j, c_ref: (i, j, 0))
    shapes = [jax.ShapeDtypeStruct((n, hr, cc), F32)] + ([jax.ShapeDtypeStruct((n, hr, cc), BF16)] if narrow else [])
    return pl.pallas_call(
        body, name=name,
        grid_spec=pltpu.PrefetchScalarGridSpec(
            num_scalar_prefetch=1, grid=(n, nb),
            in_specs=[pl.BlockSpec((None, tr, cc), lambda i, j, c_ref: (i, c_ref[0] * nb + j, 0)), spec],
            out_specs=[spec] * len(shapes)),
        out_shape=shapes, compiler_params=_params(2),
    )(c, g, x)


def _sum_slots(name, own, landed, chip):
    n, r, c = own.shape
    tr = 128 if r % 128 == 0 else r

    def body(chip_ref, a_ref, b_ref, c_ref, d_ref, o_ref):
        o_ref[...] = ((a_ref[...] + b_ref[...].astype(F32)) + c_ref[...].astype(F32)) + d_ref[...].astype(F32)

    slot = lambda d: pl.BlockSpec((None, tr, c), lambda j, chip_ref: ((chip_ref[0] + d) % n, j, 0))
    return pl.pallas_call(
        body, name=name,
        grid_spec=pltpu.PrefetchScalarGridSpec(
            num_scalar_prefetch=1, grid=(r // tr,), in_specs=[slot(0), slot(1), slot(2), slot(3)],
            out_specs=pl.BlockSpec((tr, c), lambda j, chip_ref: (j, 0))),
        out_shape=jax.ShapeDtypeStruct((r, c), F32), compiler_params=_params(1),
    )(chip, own, landed, landed, landed)


def _me():
    return lax.axis_index("x"), lax.axis_index("y"), lax.axis_index("c")


def _other_chips(x, y):
    return [(1 - x, y), (x, 1 - y), (1 - x, 1 - y)]


def _dma_sems(*counts):
    return [pltpu.SemaphoreType.DMA((n,)) for n in counts]


def _half_rows(rows, which, align):
    return pl.ds(pl.multiple_of(which * (rows // 2), align), rows // 2)


def _gather_first_layer(w_in, others):
    n = len(others)
    rows = w_in.shape[1]

    def body(*refs):
        w_ref, o_refs = refs[0], refs[1:1 + n]
        out, keep = refs[1 + n], refs[2 + n:3 + 3 * n]
        stage, ici_send, ici_recv, d2d_send, d2d_recv, local_sem = refs[3 + 3 * n:]
        x, y, c = _me()
        k = 2 * x + y
        chips = _other_chips(x, y)
        stage[...] = w_ref[0].astype(BF16)
        local = pltpu.make_async_copy(stage, out.at[k], local_sem)
        local.start()
        mine = _half_rows(rows, c, 16)
        first, passed = [], []
        for j, (px, py) in enumerate(chips):
            cp = pltpu.make_async_remote_copy(
                src_ref=stage.at[mine], dst_ref=out.at[k, mine], send_sem=ici_send.at[j], recv_sem=ici_recv.at[j],
                device_id=(px, py, c), device_id_type=MESH)
            cp.start()
            first.append(cp)
        keep[0][...] = w_ref[1].astype(BF16)
        for t in range(n):
            for l in range(2):
                keep[1 + 2 * t + l][...] = o_refs[t][l].astype(BF16)
        for j, (px, py) in enumerate(chips):
            landed = out.at[2 * px + py, mine]
            first[j].wait_recv()
            cp = pltpu.make_async_remote_copy(
                src_ref=landed, dst_ref=landed, send_sem=d2d_send.at[j], recv_sem=d2d_recv.at[j],
                device_id=(x, y, 1 - c), device_id_type=MESH)
            cp.start()
            passed.append(cp)
        for cp in passed:
            cp.wait_recv()
        for cp in first + passed:
            cp.wait_send()
        local.wait()

    kept = [jax.ShapeDtypeStruct(w_in.shape[1:], BF16)]
    for o in others:
        kept += [jax.ShapeDtypeStruct(o.shape[1:], BF16)] * 2
    return pl.pallas_call(
        body, name="gather_first_layer",
        in_specs=[VMEM_SPEC] * (1 + n), out_specs=[ANY] + [VMEM_SPEC] * len(kept),
        out_shape=[jax.ShapeDtypeStruct((4,) + w_in.shape[1:], BF16)] + kept,
        scratch_shapes=[pltpu.VMEM(w_in.shape[1:], BF16)] + _dma_sems(3, 3, 3, 3) + [pltpu.SemaphoreType.DMA],
        compiler_params=pltpu.CompilerParams(vmem_limit_bytes=VMEM_LIMIT),
    )(w_in, *others)


def _run_comm(name, comm):
    nci, nco = len(comm.ins), len(comm.out_shapes)

    def body(*refs):
        copies = comm.make(refs[:nci], refs[nci:nci + nco], refs[nci + nco:])
        for cp in copies:
            cp.start()
        for cp in copies:
            cp.wait()

    return pl.pallas_call(body, name=name, in_specs=[ANY] * nci, out_specs=[ANY] * nco,
                          out_shape=list(comm.out_shapes), scratch_shapes=list(comm.sems))(*comm.ins)


def _gather_comm(mine):
    n = len(mine)

    def make(ins, outs, sems):
        send_sems, recv_sems, local_sems = sems
        x, y, c = _me()
        k = 2 * x + y
        copies = []
        for t in range(n):
            copies.append(pltpu.make_async_copy(ins[t], outs[t].at[k], local_sems.at[t]))
            for j, (px, py) in enumerate(_other_chips(x, y)):
                copies.append(pltpu.make_async_remote_copy(
                    src_ref=ins[t], dst_ref=outs[t].at[k], send_sem=send_sems.at[3 * t + j],
                    recv_sem=recv_sems.at[3 * t + j], device_id=(px, py, c), device_id_type=MESH))
        return copies

    return _Comm(mine, [jax.ShapeDtypeStruct((4,) + a.shape, a.dtype) for a in mine], _dma_sems(3 * n, 3 * n, n), make)


def _swap_comm(grads):
    n = len(grads)

    def make(ins, outs, sems):
        send_sems, recv_sems = sems
        x, y, c = _me()
        return [pltpu.make_async_remote_copy(
            src_ref=ins[t].at[:, _half_rows(grads[t].shape[1], 1 - c, 8)], dst_ref=outs[t],
            send_sem=send_sems.at[t], recv_sem=recv_sems.at[t], device_id=(x, y, 1 - c), device_id_type=MESH)
            for t in range(n)]

    shapes = [jax.ShapeDtypeStruct((g.shape[0], g.shape[1] // 2, g.shape[2]), F32) for g in grads]
    return _Comm(grads, shapes, _dma_sems(n, n), make)


def _scatter_comm(parts):
    n = len(parts)

    def make(ins, outs, sems):
        send_sems, recv_sems = sems
        x, y, c = _me()
        k = 2 * x + y
        return [pltpu.make_async_remote_copy(
            src_ref=ins[t].at[2 * px + py], dst_ref=outs[t].at[k], send_sem=send_sems.at[3 * t + j],
            recv_sem=recv_sems.at[3 * t + j], device_id=(px, py, c), device_id_type=MESH)
            for t in range(n) for j, (px, py) in enumerate(_other_chips(x, y))]

    return _Comm(parts, [jax.ShapeDtypeStruct(p.shape, p.dtype) for p in parts], _dma_sems(3 * n, 3 * n), make)


def _share_comm(totals):
    n = len(totals)

    def make(ins, outs, sems):
        send_sems, recv_sems = sems
        x, y, c = _me()
        return [pltpu.make_async_remote_copy(
            src_ref=ins[t], dst_ref=outs[t], send_sem=send_sems.at[t], recv_sem=recv_sems.at[t],
            device_id=(x, y, 1 - c), device_id_type=MESH) for t in range(n)]

    return _Comm(totals, [jax.ShapeDtypeStruct(t.shape, F32) for t in totals], _dma_sems(n, n), make)


def _small_allreduce_adamw(part, w, m, v):
    shape = part.shape

    def body(part_ref, w_ref, m_ref, v_ref, g_ref, d_ref, nm_ref, nv_ref, slots, send_sems, recv_sems):
        x, y, c = _me()
        me = 4 * x + 2 * y + c
        slots[me] = part_ref[...]
        copies = []
        for d in range(1, 8):
            peer = (x ^ (d >> 2), y ^ ((d >> 1) & 1), c ^ (d & 1))
            cp = pltpu.make_async_remote_copy(
                src_ref=part_ref, dst_ref=slots.at[me], send_sem=send_sems.at[d - 1], recv_sem=recv_sems.at[d - 1],
                device_id=peer, device_id_type=MESH)
            cp.start()
            copies.append(cp)
        for cp in copies:
            cp.wait()
        g = slots[0]
        for i in range(1, 8):
            g = g + slots[i]
        g_ref[...] = g
        d, nm, nv = _adamw_math(w_ref[...], g, m_ref[...], v_ref[...])
        d_ref[...] = d
        nm_ref[...] = nm
        nv_ref[...] = nv

    shp = jax.ShapeDtypeStruct(shape, F32)
    return pl.pallas_call(
        body, name="small_allreduce_adamw", in_specs=[VMEM_SPEC] * 4, out_specs=[VMEM_SPEC] * 4,
        out_shape=[shp, shp, shp, shp],
        scratch_shapes=[pltpu.VMEM((8,) + shape, F32), pltpu.SemaphoreType.DMA((7,)), pltpu.SemaphoreType.DMA((7,))],
    )(part, w, m, v)


TM = 512
T_FOX = 1024
T_DIL_FWD = 1024
T_DIL_BWD = 512
TN_DW = 1408


def _layer_fwd(l, h, p_i, w_inside, w_outside, consts, comm=None, target=None):
    w_all, _, norm_g, b_f, qkg = w_inside
    bsum, _, bias_t, _, rope = consts
    u, z, qa, ka, va, qb, kb, vb, c_spread, c_t, va_t, vb_t, qa_t, qb_t = _inproj(
        l, h, norm_g, w_all, b_f, qkg, bsum, rope, TM)
    oa, lse_a, *landed = _attn_fwd(f"fox_fwd_l{l}", True, qa_t, ka, va_t, (c_spread, c_t), T_FOX, comm)
    ob, lse_b = _attn_fwd(f"dil_fwd_l{l}", False, qb_t, kb, vb_t, (bias_t,), T_DIL_FWD)
    if comm is not None:
        w_outside = w_outside(landed)
    w_out, _, w_gate, _, w_ple, ple_g = w_outside
    h2, e, gate, *out = _outproj(l, h, oa, ob, z, p_i, w_out, ple_g, w_gate, w_ple, TM, target)
    out = out[0] if target is None else tuple(out)
    saved = (h, u, z, qa, ka, va, qb, kb, vb, c_spread, c_t, oa, lse_a, ob, lse_b, h2, e, gate)
    return out, saved, w_outside


def _reduce_names(tag, n):
    return [f"reduce_{tag}_{i}" for i in range(n)]


def _layer_bwd(l, dout, p_i, wts, consts, saved, pending=None, core=None, chip=None):
    (_, w_all_t, norm_g, b_f, qkg), (_, w_out_t, _, w_gate_t, _, ple_g) = wts
    bsum, hsel, _, bias_t, rope = consts
    h, u, z, qa, ka, va, qb, kb, vb, c_spread, c_t, oa, lse_a, ob, lse_b, h2, e, gate = saved
    fused = pending is not None
    dh2, doa, dob, dga, dgb, delta_t, dw_out, dw_gate, dw_ple, dple_g = _outproj_bwd(
        l, dout, h2, e, gate, p_i, oa, ob, z, w_out_t, ple_g, w_gate_t, hsel, TM)
    if fused:
        group = list(pending) + list(_slot_layout_out(dw_out, dw_ple, dw_gate))
        n = len(group)
    dqt_a, dk_a, dv_a, dc, drow, *sib = _attn_bwd(
        f"fox_bwd_l{l}", True, qa, ka, va, doa, lse_a, delta_t, 0, (c_spread, c_t), T_FOX,
        _swap_comm(group) if fused else None)
    if fused:
        pair = [_pair_sum(nm, g, x, core)[0] for nm, g, x in zip(_reduce_names("pair_hidden", n), group, sib)]
    dqt_b, dk_b, dv_b, *landed = _attn_bwd(f"dil_bwd_l{l}", False, qb, kb, vb, dob, lse_b, delta_t, N_PAIRS,
                                             (bias_t,), T_DIL_BWD, _scatter_comm(pair) if fused else None)
    if fused:
        totals = [_sum_slots(nm, a, y, chip) for nm, a, y in zip(_reduce_names("chips_hidden", n), pair, landed)]
    dz, dqkg, dbf = _inproj_bwd_prep(l, z, dqt_a, dk_a, dv_a, dqt_b, dk_b, dv_b, dga, dgb, dc, drow, b_f, qkg,
                                     bsum, rope, TM)
    reduced = None
    if fused:
        dw_all, *other = _inproj_bwd_dw(l, u, dz, TM, TN_DW, _share_comm(totals))
        own = [_slot_layout_in(dw_all)]
        sib = _run_comm("reduce_swap_last", _swap_comm(own))
        wide, narrow = _pair_sum("reduce_pair_last", own[0], sib[0], core, narrow=True)
        dh, dnorm_g, landed = _inproj_bwd_dx(l, dz, w_all_t, h, norm_g, dh2, TM, _scatter_comm([narrow]))
        total = _sum_slots("reduce_chips_last", wide, landed, chip)
        reduced = list(zip(totals, other)) + [(total, _run_comm("reduce_share_last", _share_comm([total]))[0])]
    else:
        dh, dnorm_g = _inproj_bwd_dx(l, dz, w_all_t, h, norm_g, dh2, TM)
        dw_all, = _inproj_bwd_dw(l, u, dz, TM, TN_DW)
    return dh, (dw_all, dw_out, dw_ple, dw_gate, dnorm_g[0], dbf[0, :N_HEADS], dqkg[:4], dple_g[0]), reduced


N_FA = 2048


W_SHARD = N_IN // 4


def _shard_pieces(lo, hi):
    cuts = [(k, max(lo, k * W_SHARD), min(hi, (k + 1) * W_SHARD)) for k in range(4)]
    return [(k, a - k * W_SHARD, b - k * W_SHARD) for k, a, b in cuts if a < b]


def _in_weights(l, g_in, norm_g, b_f, qk_norm_g):
    order = _shard_pieces(0, N_FA) + _shard_pieces(N_FA + N_HEADS, N_IN) + _shard_pieces(N_FA, N_FA + N_HEADS)
    w_all = jnp.concatenate([g_in[k, :, a:b] for k, a, b in order]
                            + [jnp.zeros((D_MODEL, LANE - N_HEADS), g_in.dtype)], axis=1)
    qkg = jnp.pad(jnp.tile(qk_norm_g[l], (1, N_HEADS)), ((0, 4), (0, 0)))
    bf = jnp.pad(b_f[l], (0, LANE - N_HEADS))[None, :]
    return w_all, w_all.T, norm_g[l][None, :], bf, qkg


def _out_weights(l, g_out, g_ple, g_gate, ple_norm_g):
    w_out = g_out.reshape(D_MODEL, D_MODEL)
    w_gate = g_gate.reshape(D_MODEL, D_MODEL)
    w_ple = jnp.transpose(g_ple, (1, 0, 2)).reshape(PLE_DIM, D_MODEL)
    return w_out, w_out.T, w_gate, w_gate.T, w_ple, ple_norm_g[l][None, :]


def _slot_layout_in(dw_all):
    regions = ((0, N_FA, 0), (N_FA, N_FA + N_HEADS, N_MAIN - N_FA), (N_FA + N_HEADS, N_IN, -N_HEADS))

    def shard(k):
        cuts = [(max(k * W_SHARD, a) + shift, min((k + 1) * W_SHARD, b) + shift) for a, b, shift in regions]
        return jnp.concatenate([dw_all[:, a:b] for a, b in cuts if a < b], axis=1)

    return jnp.stack([shard(k) for k in range(4)])


def _slot_layout_out(dw_out, dw_ple, dw_gate):
    return (dw_out.reshape(4, D_MODEL // 4, D_MODEL),
            jnp.transpose(dw_ple.reshape(PLE_DIM, 4, D_MODEL // 4), (1, 0, 2)),
            dw_gate.reshape(4, D_MODEL // 4, D_MODEL))


SMALL_ROWS = 40


def _pack_small(norm_g, ple_norm_g, qk_norm_g, b_f, last=0.0):
    flat = jnp.concatenate([norm_g.reshape(-1), ple_norm_g.reshape(-1), qk_norm_g.reshape(-1), b_f.reshape(-1)])
    flat = jnp.pad(flat, (0, SMALL_ROWS * LANE - flat.shape[0] - 1))
    return jnp.concatenate([flat, jnp.reshape(last, (1,)).astype(F32)]).reshape(SMALL_ROWS, LANE)


def _unpack_small(packed):
    flat = packed.reshape(-1)
    n1, n2, n3 = 2 * D_MODEL, 4 * D_MODEL, 4 * D_MODEL + 2 * 4 * HEAD_DIM
    return (flat[:n1].reshape(2, D_MODEL), flat[n1:n2].reshape(2, D_MODEL), flat[n2:n3].reshape(2, 4, HEAD_DIM),
            flat[n3:n3 + 2 * N_HEADS].reshape(2, N_HEADS))


def kernel(x, p, positions, norm_g, w_in, b_f, qk_norm_g, w_out, w_ple, ple_norm_g, w_ple_gate, loss_target,
           m_norm_g, m_w_in, m_b_f, m_qk_norm_g, m_w_out, m_w_ple, m_ple_norm_g, m_w_ple_gate,
           v_norm_g, v_w_in, v_b_f, v_qk_norm_g, v_w_out, v_w_ple, v_ple_norm_g, v_w_ple_gate):
    assert w_in.shape[0] == 2, "the schedule below is written for two layers"
    w_in0, *kept = _gather_first_layer(w_in, [w_out, w_ple, w_ple_gate])
    consts = (_head_block_diag(), _head_select(), _dil_bias(T_DIL_FWD), _dil_bias(T_DIL_BWD),
              _rope_tables(positions[0]))
    later = {}

    def outside0(landed):
        later["w_in1"] = landed[0]
        later["out1"] = landed[2::2]
        return _out_weights(0, *landed[1::2], ple_norm_g)

    inside0 = _in_weights(0, w_in0, norm_g, b_f, qk_norm_g)
    h1, saved0, outside0 = _layer_fwd(0, x[0], p[0, 0], inside0, outside0, consts, _gather_comm(kept))
    wts0 = (inside0, outside0)
    wts1 = (_in_weights(1, later["w_in1"], norm_g, b_f, qk_norm_g), _out_weights(1, *later["out1"], ple_norm_g))
    (dh, sq), saved1, _ = _layer_fwd(1, h1, p[1, 0], *wts1, consts, target=loss_target[0])

    core = lax.axis_index("c").astype(jnp.int32).reshape(1)
    chip = (2 * lax.axis_index("x") + lax.axis_index("y")).astype(jnp.int32).reshape(1)
    dh, grads1, _ = _layer_bwd(1, dh, p[1, 0], wts1, consts, saved1)
    pending = (_slot_layout_in(grads1[0]),) + _slot_layout_out(*grads1[1:4])
    dh, grads0, hidden = _layer_bwd(0, dh, p[0, 0], wts0, consts, saved0, pending, core, chip)
    reduced0 = [hidden[7]] + hidden[4:7]
    reduced1 = hidden[0:4]
    grad_x = dh[None]
    small = [grads0[4:], grads1[4:]]
    n_layers = 2

    outs = {}
    for t, (name, w, m, v) in enumerate((("w_in", w_in, m_w_in, v_w_in), ("w_out", w_out, m_w_out, v_w_out),
                                         ("w_ple", w_ple, m_w_ple, v_w_ple),
                                         ("w_ple_gate", w_ple_gate, m_w_ple_gate, v_w_ple_gate))):
        if name == "w_in":
            g = _assemble_halves("assemble_w_in", w.shape, (reduced0[t], reduced1[t]), core)
            g_flat = _to_flat(g)
            flat = _adamw_flat("adamw_w_in", _to_flat(w), g_flat, _to_flat(m), _to_flat(v))
            outs[name] = tuple(_from_flat(f, w.shape) for f in (g_flat,) + tuple(flat))
        else:
            outs[name] = tuple(_adamw(f"adamw_{name}", w, (reduced0[t], reduced1[t]), m, v, core))

    part = _pack_small(jnp.stack([s[0] for s in small]), jnp.stack([s[3] for s in small]),
                       jnp.stack([s[2] for s in small]).reshape(n_layers, 4, N_HEADS, HEAD_DIM).sum(axis=2),
                       jnp.stack([s[1] for s in small]), 0.5 / D_MODEL * jnp.sum(sq))
    packed = _small_allreduce_adamw(part, _pack_small(norm_g, ple_norm_g, qk_norm_g, b_f),
                                    _pack_small(m_norm_g, m_ple_norm_g, m_qk_norm_g, m_b_f),
                                    _pack_small(v_norm_g, v_ple_norm_g, v_qk_norm_g, v_b_f))
    loss = packed[0][SMALL_ROWS - 1, LANE - 1]
    sm = [_unpack_small(a) for a in packed]
    for i, name in enumerate(("norm_g", "ple_norm_g", "qk_norm_g", "b_f")):
        outs[name] = tuple(sm[j][i] for j in range(4))

    order = ("norm_g", "w_in", "b_f", "qk_norm_g", "w_out", "w_ple", "ple_norm_g", "w_ple_gate")
    return (loss, grad_x) + tuple(outs[n][j] for j in range(4) for n in order)
```

```python
import functools
from typing import Any, Callable, NamedTuple, Sequence

import numpy as np
import jax
import jax.numpy as jnp
from jax import lax
from jax.experimental import pallas as pl
from jax.experimental.pallas import tpu as pltpu

F32 = jnp.float32
BF16 = jnp.bfloat16
MESH = pl.DeviceIdType.MESH

D_MODEL = 1024
HEAD_DIM = 64
D_BRANCH = 512
N_HEADS = 8
N_PAIRS = 4
N_IN = 4104
N_MAIN = 4096
N_ALL = 4224
PLE_DIM = 256
ROPE_THETA = 500000.0
ROPE_HALF = 8
EPS = 1e-6
NEG = -1e30
M_INIT = -1e29
Q_SCALE = HEAD_DIM ** -0.5
LOG2E = 1.4426950408889634
LN2 = 0.6931471805599453
DIL_PATTERNS = ((128, 1), (512, 4), (2048, 16))
DIL_BACK = 2048
ADAM_LR, ADAM_B1, ADAM_B2, ADAM_EPS, ADAM_WD, ADAM_STEP = 0.001, 0.9, 0.999, 1e-08, 0.01, 10
VMEM_LIMIT = 56 * 1024 * 1024
LANE = 128


def _dot(a, b):
    return jnp.dot(a, b, preferred_element_type=F32)


def _dot_nt(a, b):
    return lax.dot_general(a, b, (((1,), (1,)), ((), ())), preferred_element_type=F32)


def _dot_tn(a, b):
    return lax.dot_general(a, b, (((0,), (0,)), ((), ())), preferred_element_type=F32)


def _split_dot(x, w):
    hi = x.astype(BF16)
    lo = (x - hi.astype(F32)).astype(BF16)
    return _dot(hi, w) + _dot(lo, w)


def _head_sums(x, bs):
    w = bs.shape[0]
    return jnp.concatenate([_split_dot(x[:, w * k:w * (k + 1)], bs) for k in range(x.shape[1] // w)], axis=1)


def _split3_dot(w, x):
    hi = x.astype(BF16)
    r1 = x - hi.astype(F32)
    mid = r1.astype(BF16)
    lo = (r1 - mid.astype(F32)).astype(BF16)
    return _dot(w, hi) + _dot(w, mid) + _dot(w, lo)


def _sigmoid(x):
    return 1.0 / (1.0 + jnp.exp(-x))


def _params(n_grid):
    return pltpu.CompilerParams(dimension_semantics=("arbitrary",) * n_grid,
                                vmem_limit_bytes=VMEM_LIMIT)


def _full(shape):
    nd = len(shape)
    return pl.BlockSpec(shape, lambda *_: (0,) * nd)


ANY = pl.BlockSpec(memory_space=pl.ANY)
VMEM_SPEC = pl.BlockSpec(memory_space=pltpu.VMEM)


class _Comm(NamedTuple):
    ins: Sequence[Any]
    out_shapes: Sequence[Any]
    sems: Sequence[Any]
    make: Callable[..., Any]


def _fuse_comm(body, n_in, n_out, comm, grid):
    if comm is None:
        return body
    nci, nco, ncs = len(comm.ins), len(comm.out_shapes), len(comm.sems)

    def fused(*refs):
        a, b = n_in + nci, n_in + nci + n_out
        ins, cins, outs, couts = refs[:n_in], refs[n_in:a], refs[a:b], refs[b:b + nco]
        scratch, sems = refs[b + nco:len(refs) - ncs], refs[len(refs) - ncs:]
        first = functools.reduce(jnp.logical_and, [pl.program_id(d) == 0 for d in range(len(grid))])
        last = functools.reduce(jnp.logical_and, [pl.program_id(d) == n - 1 for d, n in enumerate(grid)])

        @pl.when(first)
        def _():
            for cp in comm.make(cins, couts, sems):
                cp.start()

        body(*ins, *outs, *scratch)

        @pl.when(last)
        def _():
            for cp in comm.make(cins, couts, sems):
                cp.wait()

    return fused


def _comm_args(comm):
    if comm is None:
        return [], [], [], [], []
    return (list(comm.ins), [ANY] * len(comm.ins), [ANY] * len(comm.out_shapes), list(comm.out_shapes),
            list(comm.sems))


HEADS_PER_BLOCK = 4


def _head_block_diag():
    i = np.arange(HEADS_PER_BLOCK * HEAD_DIM)
    return jnp.asarray((i[:, None] // HEAD_DIM == i[None, :] // HEAD_DIM).astype(np.float32), BF16)


def _head_select():
    i = np.arange(2 * D_BRANCH)
    j = np.arange(LANE)
    return jnp.asarray((i[:, None] // HEAD_DIM == j[None, :]).astype(np.float32), BF16)


def _dil_bias(t):
    nb = DIL_BACK // t + 1
    qi = np.arange(t)[:, None]
    ki = np.arange(t)[None, :]
    tiles = []
    for r in range(nb):
        d = r * t + qi - ki
        mult = np.zeros((t, t), np.int64)
        for window, dil in DIL_PATTERNS:
            mult += ((d >= 0) & (d <= window) & (d % dil == 0)).astype(np.int64)
        b = np.where(mult > 0, np.log2(np.maximum(mult, 1)), NEG).astype(np.float32)
        tiles.append(b.T)
    return jnp.asarray(np.stack(tiles))


def _rope_tables(positions):
    inv_freq = ROPE_THETA ** (-jnp.arange(ROPE_HALF, dtype=F32) / ROPE_HALF)
    ang = positions.astype(F32)[:, None] * inv_freq
    cos, sin = jnp.cos(ang), jnp.sin(ang)
    s = positions.shape[0]
    rest = HEAD_DIM - 2 * ROPE_HALF
    one, zero, zero8 = jnp.ones((s, rest), F32), jnp.zeros((s, rest), F32), jnp.zeros((s, ROPE_HALF), F32)
    c = jnp.concatenate([cos, cos, one], axis=1)
    s1 = jnp.concatenate([zero8, sin, zero], axis=1)
    s2 = jnp.concatenate([-sin, zero8, zero], axis=1)
    return tuple(jnp.tile(t, (1, 2)) for t in (c, s1, s2))


def _rope_fwd(x, c, s1, s2):
    return x * c + pltpu.roll(x, ROPE_HALF, 1) * s1 + pltpu.roll(x, LANE - ROPE_HALF, 1) * s2


def _rope_bwd(dy, c, s1, s2):
    return dy * c + pltpu.roll(dy * s1, LANE - ROPE_HALF, 1) + pltpu.roll(dy * s2, ROPE_HALF, 1)


def _log_sigmoid(x):
    return jnp.minimum(x, 0.0) - jnp.log(1.0 + jnp.exp(-jnp.abs(x)))


def _inproj(l, h, norm_g, w_all, b_f, qkg, bsum, rope, tm):
    s = h.shape[0]
    rc, rs1, rs2 = rope

    def body(h_ref, g_ref, w_ref, bf_ref, qkg_ref, bsum_ref, rc_ref, rs1_ref, rs2_ref,
             u_ref, z_ref, qa_ref, ka_ref, va_ref, qb_ref, kb_ref, vb_ref, cs_ref, ct_ref, vat_ref, vbt_ref,
             qat_ref, qbt_ref, carry):
        @pl.when(pl.program_id(0) == 0)
        def _():
            carry[...] = jnp.zeros_like(carry)

        hh = h_ref[...]
        r = lax.rsqrt(jnp.mean(hh * hh, axis=-1, keepdims=True) + EPS)
        u = (hh * r * g_ref[...]).astype(BF16)
        u_ref[...] = u
        for k in range(N_ALL // LANE // 3):
            cols = slice(3 * LANE * k, 3 * LANE * (k + 1))
            z_ref[:, cols] = _dot(u, w_ref[:, cols])
        bs = bsum_ref[...]

        def head_norm(x, row):
            ms = _head_sums(x * x, bs) * (1.0 / HEAD_DIM)
            return x * lax.rsqrt(ms + EPS) * qkg_ref[row:row + 1, :]

        def seg(k):
            return z_ref[:, D_BRANCH * k:D_BRANCH * (k + 1)]

        qa = head_norm(seg(0), 0) * (Q_SCALE * LOG2E)
        qa_ref[...] = qa.astype(BF16)
        qat_ref[...] = qa.T.astype(BF16)
        ka_ref[...] = head_norm(seg(1), 1).astype(BF16)
        va_ref[...] = seg(2).astype(BF16)
        vat_ref[...] = seg(2).T.astype(BF16)
        qn = head_norm(seg(4), 2) * (Q_SCALE * LOG2E)
        kn = head_norm(seg(5), 3)
        c, s1, s2 = rc_ref[...], rs1_ref[...], rs2_ref[...]
        for k in range(D_BRANCH // LANE):
            cols = slice(LANE * k, LANE * (k + 1))
            qr = _rope_fwd(qn[:, cols], c, s1, s2)
            qb_ref[:, cols] = qr.astype(BF16)
            qbt_ref[cols, :] = qr.T.astype(BF16)
            kb_ref[:, cols] = _rope_fwd(kn[:, cols], c, s1, s2).astype(BF16)
        vb_ref[...] = seg(6).astype(BF16)
        vbt_ref[...] = seg(6).T.astype(BF16)
        logf = _log_sigmoid(z_ref[:, N_MAIN:N_ALL] + bf_ref[...])
        csum = _split3_dot(_tri(tm, False), logf) + carry[0:1, :]
        carry[...] = jnp.broadcast_to(csum[tm - 1:tm, :], carry.shape)
        csum = csum * LOG2E
        ct = csum.T
        for p in range(N_PAIRS):
            cs_ref[:, LANE * p:LANE * (p + 1)] = csum if p == 0 else pltpu.roll(csum, LANE - 2 * p, 1)
            ct_ref[p, :, :] = ct[2 * p:2 * p + 2, :]

    row = lambda w: pl.BlockSpec((tm, w), lambda i: (i, 0))
    colt = pl.BlockSpec((D_BRANCH, tm), lambda i: (0, i))
    bf = lambda: jax.ShapeDtypeStruct((s, D_BRANCH), BF16)
    bft = lambda: jax.ShapeDtypeStruct((D_BRANCH, s), BF16)
    return pl.pallas_call(
        body, name=f"inproj_l{l}", grid=(s // tm,),
        in_specs=[row(D_MODEL), _full((1, D_MODEL)), _full((D_MODEL, N_ALL)), _full((1, LANE)),
                  _full((8, D_BRANCH)), _full((HEADS_PER_BLOCK * HEAD_DIM,) * 2), row(LANE), row(LANE), row(LANE)],
        out_specs=[row(D_MODEL), row(N_ALL)] + [row(D_BRANCH)] * 6
        + [row(N_PAIRS * LANE), pl.BlockSpec((N_PAIRS, 2, tm), lambda i: (0, 0, i)), colt, colt, colt, colt],
        out_shape=[jax.ShapeDtypeStruct((s, D_MODEL), BF16), jax.ShapeDtypeStruct((s, N_ALL), F32),
                   bf(), bf(), bf(), bf(), bf(), bf(), jax.ShapeDtypeStruct((s, N_PAIRS * LANE), F32),
                   jax.ShapeDtypeStruct((N_PAIRS, 2, s), F32), bft(), bft(), bft(), bft()],
        scratch_shapes=[pltpu.VMEM((8, LANE), F32)],
        compiler_params=_params(1),
    )(h, norm_g, w_all, b_f, qkg, bsum, rc, rs1, rs2)


def _tri(t, upper):
    a = lax.broadcasted_iota(jnp.int32, (t, t), 0)
    b = lax.broadcasted_iota(jnp.int32, (t, t), 1)
    return jnp.where((b >= a) if upper else (b <= a), 1.0, 0.0).astype(BF16)


def _attn_fwd(name, fox, qt, k, vt, extra, t, comm=None):
    s = k.shape[0]
    nq = s // t
    nb = DIL_BACK // t + 1

    def body(*refs):
        if fox:
            q_ref, k_ref, vt_ref, ccol_ref, crow_ref, o_ref, lse_ref, m_scr, l_scr, acc_scr = refs
        else:
            q_ref, k_ref, vt_ref, bias_ref, o_ref, lse_ref, m_scr, l_scr, acc_scr = refs
        i = pl.program_id(1)
        sub = lax.broadcasted_iota(jnp.int32, (LANE, t), 0)
        first = sub < HEAD_DIM
        qq = q_ref[...]
        zero = jnp.zeros_like(qq)
        qh = (jnp.where(first, qq, zero), jnp.where(first, zero, qq))
        m_scr[...] = jnp.full(m_scr.shape, M_INIT, F32)
        l_scr[...] = jnp.zeros_like(l_scr)
        acc_scr[...] = jnp.zeros_like(acc_scr)
        ones = jnp.ones((16, t), BF16)

        half = t // 2
        whole, lo, hi = slice(0, t), slice(0, half), slice(half, t)

        def block(j, ksl, qsl, causal):
            nk_, nq_ = ksl.stop - ksl.start, qsl.stop - qsl.start
            rows = pl.ds(pl.multiple_of(j * t + ksl.start, LANE), nk_)
            ks = k_ref[rows, :]
            vts = jnp.concatenate([vt_ref[:, rows], ones[:, :nk_]], axis=0)
            if fox:
                ccol = ccol_ref[rows, :]
            for h in range(2):
                st = _dot(ks, qh[h][:, qsl])
                if fox:
                    st = st + (crow_ref[h:h + 1, qsl] - ccol[:, h:h + 1])
                    if causal:
                        ki = lax.broadcasted_iota(jnp.int32, (nk_, nq_), 0) + ksl.start
                        qi = lax.broadcasted_iota(jnp.int32, (nk_, nq_), 1) + qsl.start
                        st = jnp.where(ki <= qi, st, NEG)
                else:
                    st = st + bias_ref[i - j, ksl, qsl]
                m_old = m_scr[h, :, qsl]
                m_new = jnp.maximum(m_old, jnp.max(st, axis=0, keepdims=True))
                alpha = jnp.exp2(m_old - m_new)
                pb = jnp.exp2(st - m_new).astype(BF16)
                pv = _dot(vts, pb)
                l_scr[h, :, qsl] = alpha * l_scr[h, :, qsl] + pv[LANE:LANE + 1, :]
                acc_scr[h, :, qsl] = alpha * acc_scr[h, :, qsl] + pv[:LANE, :]
                m_scr[h, :, qsl] = m_new

        def full(j, c):
            block(j, whole, whole, False)
            return c

        if fox:
            lax.fori_loop(0, i, full, 0)
        else:
            @pl.when(i >= nb - 1)
            def _():
                block(i - (nb - 1), lo, lo, False)
                block(i - (nb - 1), hi, whole, False)

            lax.fori_loop(jnp.maximum(i - (nb - 2), 0), i, full, 0)
        if fox:
            block(i, whole, whole, True)
        else:
            block(i, lo, lo, False)
            block(i, whole, hi, False)

        ot = jnp.where(first, acc_scr[0] / l_scr[0], acc_scr[1] / l_scr[1])
        o_ref[...] = ot.T
        for h in range(2):
            lse_ref[h:h + 1, :] = m_scr[h] + jnp.log2(l_scr[h])

    qspec = pl.BlockSpec((t, LANE), lambda hp, i: (i, hp))
    qtspec = pl.BlockSpec((LANE, t), lambda hp, i: (hp, i))
    kspec = pl.BlockSpec((s, LANE), lambda hp, i: (0, hp))
    vtspec = pl.BlockSpec((LANE, s), lambda hp, i: (hp, 0))
    in_specs = [qtspec, kspec, vtspec]
    if fox:
        in_specs += [kspec, pl.BlockSpec((None, 2, t), lambda hp, i: (hp, 0, i))]
    else:
        in_specs += [_full((nb, t, t))]
    grid = (N_PAIRS, nq)
    c_in, c_ispec, c_ospec, c_oshape, c_scr = _comm_args(comm)
    return pl.pallas_call(
        _fuse_comm(body, len(in_specs), 2, comm, grid), name=name, grid=grid,
        in_specs=in_specs + c_ispec,
        out_specs=[qspec, pl.BlockSpec((None, 2, t), lambda hp, i: (hp, 0, i))] + c_ospec,
        out_shape=[jax.ShapeDtypeStruct((s, D_BRANCH), F32), jax.ShapeDtypeStruct((N_PAIRS, 2, s), F32)] + c_oshape,
        scratch_shapes=[pltpu.VMEM((2, 1, t), F32), pltpu.VMEM((2, 1, t), F32), pltpu.VMEM((2, LANE, t), F32)]
        + c_scr,
        compiler_params=_params(2),
    )(qt, k, vt, *extra, *c_in)


def _attn_bwd(name, fox, q, k, v, do, lse_t, delta_t, pair_offset, extra, t, comm=None):
    s = q.shape[0]
    nk = s // t
    nb = DIL_BACK // t + 1

    def body(*refs):
        if fox:
            (q_ref, k_ref, v_ref, do_ref, lse_ref, delta_ref, ccol_ref, crow_ref,
             dqt_ref, dk_ref, dv_ref, dc_ref, drow_ref) = refs
        else:
            q_ref, k_ref, v_ref, do_ref, lse_ref, delta_ref, bias_ref, dqt_ref, dk_ref, dv_ref = refs
        j = pl.program_id(1)

        @pl.when(j == 0)
        def _():
            dqt_ref[...] = jnp.zeros_like(dqt_ref)
            if fox:
                drow_ref[...] = jnp.zeros_like(drow_ref)

        lane = lax.broadcasted_iota(jnp.int32, (t, LANE), 1)
        first = lane < HEAD_DIM
        ks = k_ref[...]
        vs = v_ref[...]
        kt = ks.astype(F32).T
        sub = lax.broadcasted_iota(jnp.int32, (LANE, t), 0)
        kth = (jnp.where(sub < HEAD_DIM, kt, 0.0).astype(BF16), jnp.where(sub < HEAD_DIM, 0.0, kt).astype(BF16))
        dk_ref[...] = jnp.zeros_like(dk_ref)
        dv_ref[...] = jnp.zeros_like(dv_ref)
        if fox:
            dc_ref[...] = jnp.zeros_like(dc_ref)
            ccol = ccol_ref[...]

        half = t // 2
        whole, lo, hi = slice(0, t), slice(0, half), slice(half, t)

        def block(i, ksl, qsl, causal):
            nk_, nq_ = ksl.stop - ksl.start, qsl.stop - qsl.start
            rows = pl.ds(pl.multiple_of(i * t + qsl.start, LANE), nq_)
            qq = q_ref[rows, :]
            dd = do_ref[rows, :]
            zero = jnp.zeros_like(qq)
            qh = (jnp.where(first[:nq_], qq, zero), jnp.where(first[:nq_], zero, qq))
            dh = (jnp.where(first[:nq_], dd, zero), jnp.where(first[:nq_], zero, dd))
            for h in range(2):
                st = _dot_nt(ks[ksl, :], qh[h])
                if fox:
                    st = st + (crow_ref[h:h + 1, rows] - ccol[ksl, h:h + 1])
                    if causal:
                        ki = lax.broadcasted_iota(jnp.int32, (nk_, nq_), 0) + ksl.start
                        qi = lax.broadcasted_iota(jnp.int32, (nk_, nq_), 1) + qsl.start
                        st = jnp.where(ki <= qi, st, NEG)
                else:
                    st = st + bias_ref[i - j, ksl, qsl]
                pt = jnp.exp2(st - lse_ref[h:h + 1, rows])
                dpt = _dot_nt(vs[ksl, :], dh[h])
                dst = pt * (dpt - delta_ref[h:h + 1, rows])
                dv_ref[ksl, :] += _dot(pt.astype(BF16), dh[h])
                dsb = dst.astype(BF16)
                dk_ref[ksl, :] += _dot(dsb, qh[h])
                dqt_ref[:, rows] += _dot(kth[h][:, ksl], dsb)
                if fox:
                    dc_ref[ksl, :] -= jnp.where(lane[:nk_] == h, jnp.sum(dst, axis=1, keepdims=True), 0.0)
                    drow_ref[h:h + 1, rows] += jnp.sum(dst, axis=0, keepdims=True)

        def full(i, c):
            block(i, whole, whole, False)
            return c

        block(j, lo, whole, True)
        block(j, hi, hi, True)
        if fox:
            lax.fori_loop(j + 1, nk, full, 0)
        else:
            lax.fori_loop(j + 1, jnp.minimum(j + nb - 1, nk), full, 0)

            @pl.when(j + nb - 1 < nk)
            def _():
                block(j + nb - 1, lo, lo, False)
                block(j + nb - 1, hi, whole, False)

    kspec = pl.BlockSpec((t, LANE), lambda hp, j: (j, hp))
    qspec = pl.BlockSpec((s, LANE), lambda hp, j: (0, hp))
    rowspec = pl.BlockSpec((None, 2, s), lambda hp, j: (hp, 0, 0))
    drowspec = pl.BlockSpec((None, 2, s), lambda hp, j: (hp + pair_offset, 0, 0))
    in_specs = [qspec, kspec, kspec, qspec, rowspec, drowspec]
    out_specs = [pl.BlockSpec((LANE, s), lambda hp, j: (hp, 0)), kspec, kspec]
    out_shape = [jax.ShapeDtypeStruct((D_BRANCH, s), F32), jax.ShapeDtypeStruct((s, D_BRANCH), F32),
                 jax.ShapeDtypeStruct((s, D_BRANCH), F32)]
    if fox:
        in_specs += [kspec, rowspec]
        out_specs += [kspec, rowspec]
        out_shape += [jax.ShapeDtypeStruct((s, N_PAIRS * LANE), F32), jax.ShapeDtypeStruct((N_PAIRS, 2, s), F32)]
    else:
        in_specs += [_full((nb, t, t))]
    grid = (N_PAIRS, nk)
    c_in, c_ispec, c_ospec, c_oshape, c_scr = _comm_args(comm)
    return pl.pallas_call(
        _fuse_comm(body, len(in_specs), len(out_specs), comm, grid), name=name, grid=grid,
        in_specs=in_specs + c_ispec, out_specs=out_specs + c_ospec, out_shape=out_shape + c_oshape,
        scratch_shapes=c_scr, compiler_params=_params(2),
    )(q, k, v, do, lse_t, delta_t, *extra, *c_in)


def _silu(x):
    return x * _sigmoid(x)


def _outproj(l, h, oa, ob, z, p_i, w_out, ple_g, w_gate, w_ple, tm, target=None):
    s = h.shape[0]
    last = target is not None

    def body(h_ref, oa_ref, ob_ref, ga_ref, gb_ref, p_ref, wo_ref, pg_ref, wg_ref, wp_ref, *rest):
        if last:
            t_ref, h2_ref, out_ref, acc_ref = rest
        else:
            h2_ref, out_ref = rest
        a = jnp.concatenate([oa_ref[...] * _silu(ga_ref[...]), ob_ref[...] * _silu(gb_ref[...])], axis=1)
        h2 = h_ref[...] + _dot(a.astype(BF16), wo_ref[...])
        h2_ref[...] = h2
        r = lax.rsqrt(jnp.mean(h2 * h2, axis=-1, keepdims=True) + EPS)
        n2 = (h2 * r * pg_ref[...]).astype(BF16)
        gate = _sigmoid(_dot(n2, wg_ref[...]))
        e = _dot(p_ref[...].astype(BF16), wp_ref[...])
        out = h2 + e * gate
        if not last:
            out_ref[...] = out
            return

        @pl.when(pl.program_id(0) == 0)
        def _():
            acc_ref[...] = jnp.zeros_like(acc_ref)

        err = out - t_ref[...]
        out_ref[...] = err * (1.0 / D_MODEL)
        e2 = err * err
        rows = e2[0:8, :]
        for k in range(1, tm // 8):
            rows = rows + e2[8 * k:8 * (k + 1), :]
        part = rows[:, 0:LANE]
        for k in range(1, D_MODEL // LANE):
            part = part + rows[:, LANE * k:LANE * (k + 1)]
        acc_ref[...] += part

    row = lambda w: pl.BlockSpec((tm, w), lambda i: (i, 0))
    zcol = lambda k: pl.BlockSpec((tm, D_BRANCH), lambda i: (i, k))
    f = lambda: jax.ShapeDtypeStruct((s, D_MODEL), F32)
    return pl.pallas_call(
        body, name=f"outproj_l{l}", grid=(s // tm,),
        in_specs=[row(D_MODEL), row(D_BRANCH), row(D_BRANCH), zcol(3), zcol(7), row(PLE_DIM),
                  _full((D_MODEL, D_MODEL)), _full((1, D_MODEL)), _full((D_MODEL, D_MODEL)),
                  _full((PLE_DIM, D_MODEL))] + ([row(D_MODEL)] if last else []),
        out_specs=[row(D_MODEL)] * 2 + ([_full((8, LANE))] if last else []),
        out_shape=[f(), f()] + ([jax.ShapeDtypeStruct((8, LANE), F32)] if last else []),
        compiler_params=_params(1),
    )(h, oa, ob, z, z, p_i, w_out, ple_g, w_gate, w_ple, *([target] if last else []))


def _outproj_bwd(l, dout, h2, w_gate, w_ple, p_i, oa, ob, z, w_out_t, ple_g, w_gate_t, hsel, tm, comm=None):
    s = dout.shape[0]

    def body(do_ref, h2_ref, wg_ref, wp_ref, p_ref, oa_ref, ob_ref, ga_ref, gb_ref, wot_ref, pg_ref,
             wgt_ref, hsel_ref,
             dh2_ref, doa_ref, dob_ref, dga_ref, dgb_ref, delta_ref, dwo_ref, dwg_ref, dwp_ref, dpg_ref):
        @pl.when(pl.program_id(0) == 0)
        def _():
            dwo_ref[...] = jnp.zeros_like(dwo_ref)
            dwg_ref[...] = jnp.zeros_like(dwg_ref)
            dwp_ref[...] = jnp.zeros_like(dwp_ref)
            dpg_ref[...] = jnp.zeros_like(dpg_ref)

        dho = do_ref[...]
        h2 = h2_ref[...]
        pg = pg_ref[...]
        r = lax.rsqrt(jnp.mean(h2 * h2, axis=-1, keepdims=True) + EPS)
        n2 = (h2 * r * pg).astype(BF16)
        pb = p_ref[...].astype(BF16)
        g = _sigmoid(_dot(n2, wg_ref[...]))
        e = _dot(pb, wp_ref[...])
        de = (dho * g).astype(BF16)
        dwp_ref[...] += _dot_tn(pb, de)
        dpre = (dho * e * g * (1.0 - g)).astype(BF16)
        dwg_ref[...] += _dot_tn(n2, dpre)
        dn2 = _dot(dpre, wgt_ref[...])
        dpg_ref[0:1, :] += jnp.sum(dn2 * h2 * r, axis=0, keepdims=True)
        wv = dn2 * pg
        dh2 = dho + r * wv - h2 * (r * r * r) * jnp.mean(wv * h2, axis=-1, keepdims=True)
        dh2_ref[...] = dh2
        dh2b = dh2.astype(BF16)
        ga, gb, oa, ob = ga_ref[...], gb_ref[...], oa_ref[...], ob_ref[...]
        sga, sgb = _sigmoid(ga), _sigmoid(gb)
        a = jnp.concatenate([oa * ga * sga, ob * gb * sgb], axis=1).astype(BF16)
        dwo_ref[...] += _dot_tn(a, dh2b)
        da = _dot(dh2b, wot_ref[...])
        da_a, da_b = da[:, :D_BRANCH], da[:, D_BRANCH:]
        doa = da_a * ga * sga
        dob = da_b * gb * sgb
        doa_ref[...] = doa.astype(BF16)
        dob_ref[...] = dob.astype(BF16)
        dga_ref[...] = (da_a * oa * sga * (1.0 + ga * (1.0 - sga))).astype(BF16)
        dgb_ref[...] = (da_b * ob * sgb * (1.0 + gb * (1.0 - sgb))).astype(BF16)
        prod = jnp.concatenate([doa * oa, dob * ob], axis=1)
        dt = _split_dot(prod, hsel_ref[...]).T
        for pp in range(2 * N_PAIRS):
            delta_ref[pp, :, :] = dt[2 * pp:2 * pp + 2, :]

    row = lambda w: pl.BlockSpec((tm, w), lambda i: (i, 0))
    zcol = lambda k: pl.BlockSpec((tm, D_BRANCH), lambda i: (i, k))
    grid = (s // tm,)
    c_in, c_ispec, c_ospec, c_oshape, c_scr = _comm_args(comm)
    return pl.pallas_call(
        _fuse_comm(body, 13, 10, comm, grid), name=f"outproj_bwd_l{l}", grid=grid,
        in_specs=[row(D_MODEL)] * 2 + [_full((D_MODEL, D_MODEL)), _full((PLE_DIM, D_MODEL)),
                  row(PLE_DIM), row(D_BRANCH), row(D_BRANCH), zcol(3), zcol(7),
                                        _full((D_MODEL, D_MODEL)), _full((1, D_MODEL)), _full((D_MODEL, D_MODEL)),
                                        _full((2 * D_BRANCH, LANE))] + c_ispec,
        out_specs=[row(D_MODEL)] + [row(D_BRANCH)] * 4
        + [pl.BlockSpec((2 * N_PAIRS, 2, tm), lambda i: (0, 0, i)), _full((D_MODEL, D_MODEL)),
           _full((D_MODEL, D_MODEL)), _full((PLE_DIM, D_MODEL)), _full((8, D_MODEL))] + c_ospec,
        out_shape=[jax.ShapeDtypeStruct((s, D_MODEL), F32)] + [jax.ShapeDtypeStruct((s, D_BRANCH), BF16)] * 4
        + [jax.ShapeDtypeStruct((2 * N_PAIRS, 2, s), F32), jax.ShapeDtypeStruct((D_MODEL, D_MODEL), F32),
           jax.ShapeDtypeStruct((D_MODEL, D_MODEL), F32), jax.ShapeDtypeStruct((PLE_DIM, D_MODEL), F32),
           jax.ShapeDtypeStruct((8, D_MODEL), F32)] + c_oshape,
        scratch_shapes=c_scr, compiler_params=_params(1),
    )(dout, h2, w_gate, w_ple, p_i, oa, ob, z, z, w_out_t, ple_g, w_gate_t, hsel, *c_in)


def _inproj_bwd_prep(l, z, dqt_a, dk_a, dv_a, dqt_b, dk_b, dv_b, dga, dgb, dc_spread, dc_rows, b_f, qkg, bsum,
                     rope, tm):
    s = z.shape[0]
    n = s // tm
    rc, rs1, rs2 = rope

    def body(zqa_ref, zka_ref, zqb_ref, zkb_ref, zf_ref, dqta_ref, dka_ref, dva_ref, dqtb_ref, dkb_ref, dvb_ref,
             dga_ref, dgb_ref, dc_ref, drow_ref, bf_ref, qkg_ref, bsum_ref, rc_ref, rs1_ref, rs2_ref,
             dz_ref, dqkg_ref, dbf_ref, carry):
        @pl.when(pl.program_id(0) == 0)
        def _():
            dqkg_ref[...] = jnp.zeros_like(dqkg_ref)
            dbf_ref[...] = jnp.zeros_like(dbf_ref)
            carry[...] = jnp.zeros_like(carry)

        lane = lax.broadcasted_iota(jnp.int32, (tm, LANE), 1)
        dc = jnp.concatenate([drow_ref[p] for p in range(N_PAIRS)] + [jnp.zeros((LANE - N_HEADS, tm), F32)], axis=0).T
        for p in range(N_PAIRS):
            part = jnp.where(lane < 2, dc_ref[:, LANE * p:LANE * (p + 1)], 0.0)
            dc = dc + (part if p == 0 else pltpu.roll(part, 2 * p, 1))
        dlogf = _split3_dot(_tri(tm, True), dc) + carry[0:1, :]
        carry[...] = jnp.broadcast_to(dlogf[0:1, :], carry.shape)

        bs = bsum_ref[...]
        c, s1, s2 = rc_ref[...], rs1_ref[...], rs2_ref[...]

        def unrope(dy):
            return jnp.concatenate([_rope_bwd(dy[:, LANE * k:LANE * (k + 1)], c, s1, s2)
                                    for k in range(D_BRANCH // LANE)], axis=1)

        def norm_bwd(k, row, dy, x_ref):
            x = x_ref[...]
            r = lax.rsqrt(_head_sums(x * x, bs) * (1.0 / HEAD_DIM) + EPS)
            dqkg_ref[row:row + 1, :] += jnp.sum(dy * x * r, axis=0, keepdims=True)
            w = dy * qkg_ref[row:row + 1, :]
            dx = r * w - x * (r * r * r) * (_head_sums(w * x, bs) * (1.0 / HEAD_DIM))
            dz_ref[:, D_BRANCH * k:D_BRANCH * (k + 1)] = dx.astype(BF16)

        norm_bwd(0, 0, dqta_ref[...].T * Q_SCALE, zqa_ref)
        norm_bwd(1, 1, dka_ref[...] * LN2, zka_ref)
        dz_ref[:, 2 * D_BRANCH:3 * D_BRANCH] = dva_ref[...].astype(BF16)
        dz_ref[:, 3 * D_BRANCH:4 * D_BRANCH] = dga_ref[...]
        norm_bwd(4, 2, unrope(dqtb_ref[...].T * Q_SCALE), zqb_ref)
        norm_bwd(5, 3, unrope(dkb_ref[...] * LN2), zkb_ref)
        dz_ref[:, 6 * D_BRANCH:7 * D_BRANCH] = dvb_ref[...].astype(BF16)
        dz_ref[:, 7 * D_BRANCH:8 * D_BRANCH] = dgb_ref[...]
        dfa = dlogf * _sigmoid(-(zf_ref[...] + bf_ref[...]))
        dz_ref[:, N_MAIN:N_ALL] = dfa.astype(BF16)
        dbf_ref[0:1, :] += jnp.sum(dfa, axis=0, keepdims=True)

    row = lambda w: pl.BlockSpec((tm, w), lambda i: (n - 1 - i, 0))
    colt = pl.BlockSpec((D_BRANCH, tm), lambda i: (0, n - 1 - i))
    zcol = lambda k: pl.BlockSpec((tm, D_BRANCH), lambda i: (n - 1 - i, k))
    zf = pl.BlockSpec((tm, LANE), lambda i: (n - 1 - i, N_MAIN // LANE))
    return pl.pallas_call(
        body, name=f"inproj_bwd_prep_l{l}", grid=(n,),
        in_specs=[zcol(0), zcol(1), zcol(4), zcol(5), zf, colt, row(D_BRANCH), row(D_BRANCH), colt, row(D_BRANCH), row(D_BRANCH),
                  row(D_BRANCH), row(D_BRANCH), row(N_PAIRS * LANE),
                  pl.BlockSpec((N_PAIRS, 2, tm), lambda i: (0, 0, n - 1 - i)), _full((1, LANE)),
                  _full((8, D_BRANCH)), _full((HEADS_PER_BLOCK * HEAD_DIM,) * 2), row(LANE), row(LANE), row(LANE)],
        out_specs=[row(N_ALL), _full((8, D_BRANCH)), _full((8, LANE))],
        out_shape=[jax.ShapeDtypeStruct((s, N_ALL), BF16), jax.ShapeDtypeStruct((8, D_BRANCH), F32),
                   jax.ShapeDtypeStruct((8, LANE), F32)],
        scratch_shapes=[pltpu.VMEM((8, LANE), F32)],
        compiler_params=_params(1),
    )(z, z, z, z, z, dqt_a, dk_a, dv_a, dqt_b, dk_b, dv_b, dga, dgb, dc_spread, dc_rows, b_f, qkg, bsum, rc, rs1, rs2)


def _inproj_bwd_dx(l, dz, w_all_t, h, norm_g, dh2, tm, comm=None):
    s = dz.shape[0]

    def body(dz_ref, wt_ref, h_ref, g_ref, dh2_ref, dh_ref, dg_ref):
        @pl.when(pl.program_id(0) == 0)
        def _():
            dg_ref[...] = jnp.zeros_like(dg_ref)

        du = _dot(dz_ref[...], wt_ref[...])
        hh = h_ref[...]
        g = g_ref[...]
        r = lax.rsqrt(jnp.mean(hh * hh, axis=-1, keepdims=True) + EPS)
        dg_ref[0:1, :] += jnp.sum(du * hh * r, axis=0, keepdims=True)
        wv = du * g
        dh_ref[...] = dh2_ref[...] + r * wv - hh * (r * r * r) * jnp.mean(wv * hh, axis=-1, keepdims=True)

    row = lambda w: pl.BlockSpec((tm, w), lambda i: (i, 0))
    grid = (s // tm,)
    c_in, c_ispec, c_ospec, c_oshape, c_scr = _comm_args(comm)
    return pl.pallas_call(
        _fuse_comm(body, 5, 2, comm, grid), name=f"inproj_bwd_dx_l{l}", grid=grid,
        in_specs=[row(N_ALL), _full((N_ALL, D_MODEL)), row(D_MODEL), _full((1, D_MODEL)), row(D_MODEL)] + c_ispec,
        out_specs=[row(D_MODEL), _full((8, D_MODEL))] + c_ospec,
        out_shape=[jax.ShapeDtypeStruct((s, D_MODEL), F32), jax.ShapeDtypeStruct((8, D_MODEL), F32)] + c_oshape,
        scratch_shapes=c_scr, compiler_params=_params(1),
    )(dz, w_all_t, h, norm_g, dh2, *c_in)


def _inproj_bwd_dw(l, u, dz, tm, tn, comm=None):
    s = u.shape[0]

    def body(u_ref, dz_ref, dw_ref):
        @pl.when(pl.program_id(1) == 0)
        def _():
            dw_ref[...] = jnp.zeros_like(dw_ref)

        dw_ref[...] += _dot_tn(u_ref[...], dz_ref[...])

    grid = (N_ALL // tn, s // tm)
    c_in, c_ispec, c_ospec, c_oshape, c_scr = _comm_args(comm)
    return pl.pallas_call(
        _fuse_comm(body, 2, 1, comm, grid), name=f"inproj_bwd_dw_l{l}", grid=grid,
        in_specs=[pl.BlockSpec((tm, D_MODEL), lambda n, i: (i, 0)), pl.BlockSpec((tm, tn), lambda n, i: (i, n))]
        + c_ispec,
        out_specs=[pl.BlockSpec((D_MODEL, tn), lambda n, i: (0, n))] + c_ospec,
        out_shape=[jax.ShapeDtypeStruct((D_MODEL, N_ALL), F32)] + c_oshape,
        scratch_shapes=c_scr, compiler_params=_params(2),
    )(u, dz, *c_in)


def _adamw_math(w, g, m, v):
    m = ADAM_B1 * m + (1.0 - ADAM_B1) * g
    v = ADAM_B2 * v + (1.0 - ADAM_B2) * (g * g)
    m_hat = m / (1.0 - ADAM_B1 ** ADAM_STEP)
    v_hat = v / (1.0 - ADAM_B2 ** ADAM_STEP)
    delta = -ADAM_LR * (m_hat / (jnp.sqrt(v_hat) + ADAM_EPS) + ADAM_WD * w)
    return delta, m, v


def _adamw(name, w, halves, m, v, core):
    nl, r, c = w.shape
    hr = r // 2
    tr = 128 if hr % 128 == 0 else hr
    nb = hr // tr

    def body(core_ref, w_ref, own0_ref, oth0_ref, own1_ref, oth1_ref, m_ref, v_ref, g_ref, d_ref, nm_ref, nv_ref):
        first = pl.program_id(0) == 0
        own = jnp.where(first, own0_ref[...], own1_ref[...])
        oth = jnp.where(first, oth0_ref[...], oth1_ref[...])
        g = jnp.where(pl.program_id(1) // nb == core_ref[0], own, oth)
        d, nm, nv = _adamw_math(w_ref[...], g, m_ref[...], v_ref[...])
        g_ref[...] = g
        d_ref[...] = d
        nm_ref[...] = nm
        nv_ref[...] = nv

    spec = pl.BlockSpec((None, tr, c), lambda a, b, core_ref: (a, b, 0))
    gspec = pl.BlockSpec((tr, c), lambda a, b, core_ref: (b % nb, 0))
    shp = jax.ShapeDtypeStruct(w.shape, F32)
    return pl.pallas_call(
        body, name=name,
        grid_spec=pltpu.PrefetchScalarGridSpec(
            num_scalar_prefetch=1, grid=(nl, r // tr), in_specs=[spec] + [gspec] * 4 + [spec, spec],
            out_specs=[spec] * 4),
        out_shape=[shp, shp, shp, shp], compiler_params=_params(2),
    )(core, w, halves[0][0], halves[0][1], halves[1][0], halves[1][1], m, v)


def _assemble_halves(name, shape, halves, core):
    nl, r, c = shape
    hr = r // 2
    tr = 128 if hr % 128 == 0 else hr
    nb = hr // tr

    def body(core_ref, own0_ref, oth0_ref, own1_ref, oth1_ref, g_ref):
        first = pl.program_id(0) == 0
        own = jnp.where(first, own0_ref[...], own1_ref[...])
        oth = jnp.where(first, oth0_ref[...], oth1_ref[...])
        g_ref[...] = jnp.where(pl.program_id(1) // nb == core_ref[0], own, oth)

    of_layer = lambda l: pl.BlockSpec((tr, c), lambda a, b, core_ref: ((b % nb) * (a if l else 1 - a), 0))
    return pl.pallas_call(
        body, name=name,
        grid_spec=pltpu.PrefetchScalarGridSpec(
            num_scalar_prefetch=1, grid=(nl, r // tr), in_specs=[of_layer(0), of_layer(0), of_layer(1), of_layer(1)],
            out_specs=pl.BlockSpec((None, tr, c), lambda a, b, core_ref: (a, b, 0))),
        out_shape=jax.ShapeDtypeStruct(shape, F32), compiler_params=_params(2),
    )(core, halves[0][0], halves[0][1], halves[1][0], halves[1][1])


W_IN_FLAT_STEPS = 19


def _to_flat(a):
    nl, r, c = a.shape
    return jnp.transpose(jnp.transpose(a, (2, 0, 1)).reshape(c, nl, r // LANE, LANE), (0, 2, 1, 3)).reshape(-1, LANE)


def _from_flat(f, shape):
    nl, r, c = shape
    return jnp.transpose(jnp.transpose(f.reshape(c, r // LANE, nl, LANE), (0, 2, 1, 3)).reshape(c, nl, r), (1, 2, 0))


def _adamw_flat(name, w, g, m, v):
    n = w.shape[0]
    tr = n // W_IN_FLAT_STEPS

    def body(w_ref, g_ref, m_ref, v_ref, d_ref, nm_ref, nv_ref):
        d, nm, nv = _adamw_math(w_ref[...], g_ref[...], m_ref[...], v_ref[...])
        d_ref[...] = d
        nm_ref[...] = nm
        nv_ref[...] = nv

    spec = pl.BlockSpec((tr, LANE), lambda i: (i, 0))
    shp = jax.ShapeDtypeStruct(w.shape, F32)
    return pl.pallas_call(body, name=name, grid=(W_IN_FLAT_STEPS,), in_specs=[spec] * 4, out_specs=[spec] * 3,
                          out_shape=[shp, shp, shp], compiler_params=_params(1))(w, g, m, v)


def _pair_sum(name, g, x, c, narrow=False):
    n, r, cc = g.shape
    hr = r // 2
    tr = 128 if hr % 128 == 0 else hr
    nb = hr // tr

    def body(c_ref, g_ref, x_ref, o_ref, *narrow_ref):
        total = g_ref[...] + x_ref[...]
        o_ref[...] = total
        if narrow:
            narrow_ref[0][...] = total.astype(BF16)

    spec = pl.BlockSpec((None, tr, cc), lambda i, j, c_ref: (i, j, 0))
    shapes = [jax.ShapeDtypeStruct((n, hr, cc), F32)] + ([jax.ShapeDtypeStruct((n, hr, cc), BF16)] if narrow else [])
    return pl.pallas_call(
        body, name=name,
        grid_spec=pltpu.PrefetchScalarGridSpec(
            num_scalar_prefetch=1, grid=(n, nb),
            in_specs=[pl.BlockSpec((None, tr, cc), lambda i, j, c_ref: (i, c_ref[0] * nb + j, 0)), spec],
            out_specs=[spec] * len(shapes)),
        out_shape=shapes, compiler_params=_params(2),
    )(c, g, x)


def _sum_slots(name, own, landed, chip):
    n, r, c = own.shape
    tr = 128 if r % 128 == 0 else r

    def body(chip_ref, a_ref, b_ref, c_ref, d_ref, o_ref):
        o_ref[...] = ((a_ref[...] + b_ref[...].astype(F32)) + c_ref[...].astype(F32)) + d_ref[...].astype(F32)

    slot = lambda d: pl.BlockSpec((None, tr, c), lambda j, chip_ref: ((chip_ref[0] + d) % n, j, 0))
    return pl.pallas_call(
        body, name=name,
        grid_spec=pltpu.PrefetchScalarGridSpec(
            num_scalar_prefetch=1, grid=(r // tr,), in_specs=[slot(0), slot(1), slot(2), slot(3)],
            out_specs=pl.BlockSpec((tr, c), lambda j, chip_ref: (j, 0))),
        out_shape=jax.ShapeDtypeStruct((r, c), F32), compiler_params=_params(1),
    )(chip, own, landed, landed, landed)


def _me():
    return lax.axis_index("x"), lax.axis_index("y"), lax.axis_index("c")


def _other_chips(x, y):
    return [(1 - x, y), (x, 1 - y), (1 - x, 1 - y)]


def _dma_sems(*counts):
    return [pltpu.SemaphoreType.DMA((n,)) for n in counts]


def _half_rows(rows, which, align):
    return pl.ds(pl.multiple_of(which * (rows // 2), align), rows // 2)


def _gather_first_layer(w_in, others):
    n = len(others)
    rows = w_in.shape[1]

    def body(*refs):
        w_ref, o_refs = refs[0], refs[1:1 + n]
        out, keep = refs[1 + n], refs[2 + n:3 + 3 * n]
        stage, ici_send, ici_recv, d2d_send, d2d_recv, local_sem = refs[3 + 3 * n:]
        x, y, c = _me()
        k = 2 * x + y
        chips = _other_chips(x, y)
        stage[...] = w_ref[0].astype(BF16)
        local = pltpu.make_async_copy(stage, out.at[k], local_sem)
        local.start()
        mine = _half_rows(rows, c, 16)
        first, passed = [], []
        for j, (px, py) in enumerate(chips):
            cp = pltpu.make_async_remote_copy(
                src_ref=stage.at[mine], dst_ref=out.at[k, mine], send_sem=ici_send.at[j], recv_sem=ici_recv.at[j],
                device_id=(px, py, c), device_id_type=MESH)
            cp.start()
            first.append(cp)
        keep[0][...] = w_ref[1].astype(BF16)
        for t in range(n):
            for l in range(2):
                keep[1 + 2 * t + l][...] = o_refs[t][l].astype(BF16)
        for j, (px, py) in enumerate(chips):
            landed = out.at[2 * px + py, mine]
            first[j].wait_recv()
            cp = pltpu.make_async_remote_copy(
                src_ref=landed, dst_ref=landed, send_sem=d2d_send.at[j], recv_sem=d2d_recv.at[j],
                device_id=(x, y, 1 - c), device_id_type=MESH)
            cp.start()
            passed.append(cp)
        for cp in passed:
            cp.wait_recv()
        for cp in first + passed:
            cp.wait_send()
        local.wait()

    kept = [jax.ShapeDtypeStruct(w_in.shape[1:], BF16)]
    for o in others:
        kept += [jax.ShapeDtypeStruct(o.shape[1:], BF16)] * 2
    return pl.pallas_call(
        body, name="gather_first_layer",
        in_specs=[VMEM_SPEC] * (1 + n), out_specs=[ANY] + [VMEM_SPEC] * len(kept),
        out_shape=[jax.ShapeDtypeStruct((4,) + w_in.shape[1:], BF16)] + kept,
        scratch_shapes=[pltpu.VMEM(w_in.shape[1:], BF16)] + _dma_sems(3, 3, 3, 3) + [pltpu.SemaphoreType.DMA],
        compiler_params=pltpu.CompilerParams(vmem_limit_bytes=VMEM_LIMIT),
    )(w_in, *others)


def _run_comm(name, comm):
    nci, nco = len(comm.ins), len(comm.out_shapes)

    def body(*refs):
        copies = comm.make(refs[:nci], refs[nci:nci + nco], refs[nci + nco:])
        for cp in copies:
            cp.start()
        for cp in copies:
            cp.wait()

    return pl.pallas_call(body, name=name, in_specs=[ANY] * nci, out_specs=[ANY] * nco,
                          out_shape=list(comm.out_shapes), scratch_shapes=list(comm.sems))(*comm.ins)


def _gather_comm(mine):
    n = len(mine)

    def make(ins, outs, sems):
        send_sems, recv_sems, local_sems = sems
        x, y, c = _me()
        k = 2 * x + y
        copies = []
        for t in range(n):
            copies.append(pltpu.make_async_copy(ins[t], outs[t].at[k], local_sems.at[t]))
            for j, (px, py) in enumerate(_other_chips(x, y)):
                copies.append(pltpu.make_async_remote_copy(
                    src_ref=ins[t], dst_ref=outs[t].at[k], send_sem=send_sems.at[3 * t + j],
                    recv_sem=recv_sems.at[3 * t + j], device_id=(px, py, c), device_id_type=MESH))
        return copies

    return _Comm(mine, [jax.ShapeDtypeStruct((4,) + a.shape, a.dtype) for a in mine], _dma_sems(3 * n, 3 * n, n), make)


def _swap_comm(grads):
    n = len(grads)

    def make(ins, outs, sems):
        send_sems, recv_sems = sems
        x, y, c = _me()
        return [pltpu.make_async_remote_copy(
            src_ref=ins[t].at[:, _half_rows(grads[t].shape[1], 1 - c, 8)], dst_ref=outs[t],
            send_sem=send_sems.at[t], recv_sem=recv_sems.at[t], device_id=(x, y, 1 - c), device_id_type=MESH)
            for t in range(n)]

    shapes = [jax.ShapeDtypeStruct((g.shape[0], g.shape[1] // 2, g.shape[2]), F32) for g in grads]
    return _Comm(grads, shapes, _dma_sems(n, n), make)


def _scatter_comm(parts):
    n = len(parts)

    def make(ins, outs, sems):
        send_sems, recv_sems = sems
        x, y, c = _me()
        k = 2 * x + y
        return [pltpu.make_async_remote_copy(
            src_ref=ins[t].at[2 * px + py], dst_ref=outs[t].at[k], send_sem=send_sems.at[3 * t + j],
            recv_sem=recv_sems.at[3 * t + j], device_id=(px, py, c), device_id_type=MESH)
            for t in range(n) for j, (px, py) in enumerate(_other_chips(x, y))]

    return _Comm(parts, [jax.ShapeDtypeStruct(p.shape, p.dtype) for p in parts], _dma_sems(3 * n, 3 * n), make)


def _share_comm(totals):
    n = len(totals)

    def make(ins, outs, sems):
        send_sems, recv_sems = sems
        x, y, c = _me()
        return [pltpu.make_async_remote_copy(
            src_ref=ins[t], dst_ref=outs[t], send_sem=send_sems.at[t], recv_sem=recv_sems.at[t],
            device_id=(x, y, 1 - c), device_id_type=MESH) for t in range(n)]

    return _Comm(totals, [jax.ShapeDtypeStruct(t.shape, F32) for t in totals], _dma_sems(n, n), make)


def _small_allreduce_adamw(part, w, m, v):
    shape = part.shape

    def body(part_ref, w_ref, m_ref, v_ref, g_ref, d_ref, nm_ref, nv_ref, slots, send_sems, recv_sems):
        x, y, c = _me()
        me = 4 * x + 2 * y + c
        slots[me] = part_ref[...]
        copies = []
        for d in range(1, 8):
            peer = (x ^ (d >> 2), y ^ ((d >> 1) & 1), c ^ (d & 1))
            cp = pltpu.make_async_remote_copy(
                src_ref=part_ref, dst_ref=slots.at[me], send_sem=send_sems.at[d - 1], recv_sem=recv_sems.at[d - 1],
                device_id=peer, device_id_type=MESH)
            cp.start()
            copies.append(cp)
        for cp in copies:
            cp.wait()
        g = slots[0]
        for i in range(1, 8):
            g = g + slots[i]
        g_ref[...] = g
        d, nm, nv = _adamw_math(w_ref[...], g, m_ref[...], v_ref[...])
        d_ref[...] = d
        nm_ref[...] = nm
        nv_ref[...] = nv

    shp = jax.ShapeDtypeStruct(shape, F32)
    return pl.pallas_call(
        body, name="small_allreduce_adamw", in_specs=[VMEM_SPEC] * 4, out_specs=[VMEM_SPEC] * 4,
        out_shape=[shp, shp, shp, shp],
        scratch_shapes=[pltpu.VMEM((8,) + shape, F32), pltpu.SemaphoreType.DMA((7,)), pltpu.SemaphoreType.DMA((7,))],
    )(part, w, m, v)


TM = 512
T_FOX = 1024
T_DIL_FWD = 1024
T_DIL_BWD = 512
TN_DW = 1408


def _layer_fwd(l, h, p_i, w_inside, w_outside, consts, comm=None, target=None):
    w_all, _, norm_g, b_f, qkg = w_inside
    bsum, _, bias_t, _, rope = consts
    u, z, qa, ka, va, qb, kb, vb, c_spread, c_t, va_t, vb_t, qa_t, qb_t = _inproj(
        l, h, norm_g, w_all, b_f, qkg, bsum, rope, TM)
    oa, lse_a, *landed = _attn_fwd(f"fox_fwd_l{l}", True, qa_t, ka, va_t, (c_spread, c_t), T_FOX, comm)
    ob, lse_b = _attn_fwd(f"dil_fwd_l{l}", False, qb_t, kb, vb_t, (bias_t,), T_DIL_FWD)
    if comm is not None:
        w_outside = w_outside(landed)
    w_out, _, w_gate, _, w_ple, ple_g = w_outside
    h2, *out = _outproj(l, h, oa, ob, z, p_i, w_out, ple_g, w_gate, w_ple, TM, target)
    out = out[0] if target is None else tuple(out)
    saved = (h, u, z, qa, ka, va, qb, kb, vb, c_spread, c_t, oa, lse_a, ob, lse_b, h2)
    return out, saved, w_outside


def _reduce_names(tag, n):
    return [f"reduce_{tag}_{i}" for i in range(n)]


def _layer_bwd(l, dout, p_i, wts, consts, saved, pending=None, core=None, chip=None):
    (_, w_all_t, norm_g, b_f, qkg), (_, w_out_t, w_gate, w_gate_t, w_ple, ple_g) = wts
    bsum, hsel, _, bias_t, rope = consts
    h, u, z, qa, ka, va, qb, kb, vb, c_spread, c_t, oa, lse_a, ob, lse_b, h2 = saved
    fused = pending is not None
    dh2, doa, dob, dga, dgb, delta_t, dw_out, dw_gate, dw_ple, dple_g = _outproj_bwd(
        l, dout, h2, w_gate, w_ple, p_i, oa, ob, z, w_out_t, ple_g, w_gate_t, hsel, TM)
    if fused:
        group = list(pending) + list(_slot_layout_out(dw_out, dw_ple, dw_gate))
        n = len(group)
    dqt_a, dk_a, dv_a, dc, drow, *sib = _attn_bwd(
        f"fox_bwd_l{l}", True, qa, ka, va, doa, lse_a, delta_t, 0, (c_spread, c_t), T_FOX,
        _swap_comm(group) if fused else None)
    if fused:
        pair = [_pair_sum(nm, g, x, core)[0] for nm, g, x in zip(_reduce_names("pair_hidden", n), group, sib)]
    dqt_b, dk_b, dv_b, *landed = _attn_bwd(f"dil_bwd_l{l}", False, qb, kb, vb, dob, lse_b, delta_t, N_PAIRS,
                                             (bias_t,), T_DIL_BWD, _scatter_comm(pair) if fused else None)
    if fused:
        totals = [_sum_slots(nm, a, y, chip) for nm, a, y in zip(_reduce_names("chips_hidden", n), pair, landed)]
    dz, dqkg, dbf = _inproj_bwd_prep(l, z, dqt_a, dk_a, dv_a, dqt_b, dk_b, dv_b, dga, dgb, dc, drow, b_f, qkg,
                                     bsum, rope, TM)
    reduced = None
    if fused:
        dw_all, *other = _inproj_bwd_dw(l, u, dz, TM, TN_DW, _share_comm(totals))
        own = [_slot_layout_in(dw_all)]
        sib = _run_comm("reduce_swap_last", _swap_comm(own))
        wide, narrow = _pair_sum("reduce_pair_last", own[0], sib[0], core, narrow=True)
        dh, dnorm_g, landed = _inproj_bwd_dx(l, dz, w_all_t, h, norm_g, dh2, TM, _scatter_comm([narrow]))
        total = _sum_slots("reduce_chips_last", wide, landed, chip)
        reduced = list(zip(totals, other)) + [(total, _run_comm("reduce_share_last", _share_comm([total]))[0])]
    else:
        dh, dnorm_g = _inproj_bwd_dx(l, dz, w_all_t, h, norm_g, dh2, TM)
        dw_all, = _inproj_bwd_dw(l, u, dz, TM, TN_DW)
    return dh, (dw_all, dw_out, dw_ple, dw_gate, dnorm_g[0], dbf[0, :N_HEADS], dqkg[:4], dple_g[0]), reduced


N_FA = 2048


W_SHARD = N_IN // 4


def _shard_pieces(lo, hi):
    cuts = [(k, max(lo, k * W_SHARD), min(hi, (k + 1) * W_SHARD)) for k in range(4)]
    return [(k, a - k * W_SHARD, b - k * W_SHARD) for k, a, b in cuts if a < b]


def _in_weights(l, g_in, norm_g, b_f, qk_norm_g):
    order = _shard_pieces(0, N_FA) + _shard_pieces(N_FA + N_HEADS, N_IN) + _shard_pieces(N_FA, N_FA + N_HEADS)
    w_all = jnp.concatenate([g_in[k, :, a:b] for k, a, b in order]
                            + [jnp.zeros((D_MODEL, LANE - N_HEADS), g_in.dtype)], axis=1)
    qkg = jnp.pad(jnp.tile(qk_norm_g[l], (1, N_HEADS)), ((0, 4), (0, 0)))
    bf = jnp.pad(b_f[l], (0, LANE - N_HEADS))[None, :]
    return w_all, w_all.T, norm_g[l][None, :], bf, qkg


def _out_weights(l, g_out, g_ple, g_gate, ple_norm_g):
    w_out = g_out.reshape(D_MODEL, D_MODEL)
    w_gate = g_gate.reshape(D_MODEL, D_MODEL)
    w_ple = jnp.transpose(g_ple, (1, 0, 2)).reshape(PLE_DIM, D_MODEL)
    return w_out, w_out.T, w_gate, w_gate.T, w_ple, ple_norm_g[l][None, :]


def _slot_layout_in(dw_all):
    regions = ((0, N_FA, 0), (N_FA, N_FA + N_HEADS, N_MAIN - N_FA), (N_FA + N_HEADS, N_IN, -N_HEADS))

    def shard(k):
        cuts = [(max(k * W_SHARD, a) + shift, min((k + 1) * W_SHARD, b) + shift) for a, b, shift in regions]
        return jnp.concatenate([dw_all[:, a:b] for a, b in cuts if a < b], axis=1)

    return jnp.stack([shard(k) for k in range(4)])


def _slot_layout_out(dw_out, dw_ple, dw_gate):
    return (dw_out.reshape(4, D_MODEL // 4, D_MODEL),
            jnp.transpose(dw_ple.reshape(PLE_DIM, 4, D_MODEL // 4), (1, 0, 2)),
            dw_gate.reshape(4, D_MODEL // 4, D_MODEL))


SMALL_ROWS = 40


def _pack_small(norm_g, ple_norm_g, qk_norm_g, b_f, last=0.0):
    flat = jnp.concatenate([norm_g.reshape(-1), ple_norm_g.reshape(-1), qk_norm_g.reshape(-1), b_f.reshape(-1)])
    flat = jnp.pad(flat, (0, SMALL_ROWS * LANE - flat.shape[0] - 1))
    return jnp.concatenate([flat, jnp.reshape(last, (1,)).astype(F32)]).reshape(SMALL_ROWS, LANE)


def _unpack_small(packed):
    flat = packed.reshape(-1)
    n1, n2, n3 = 2 * D_MODEL, 4 * D_MODEL, 4 * D_MODEL + 2 * 4 * HEAD_DIM
    return (flat[:n1].reshape(2, D_MODEL), flat[n1:n2].reshape(2, D_MODEL), flat[n2:n3].reshape(2, 4, HEAD_DIM),
            flat[n3:n3 + 2 * N_HEADS].reshape(2, N_HEADS))


def kernel(x, p, positions, norm_g, w_in, b_f, qk_norm_g, w_out, w_ple, ple_norm_g, w_ple_gate, loss_target,
           m_norm_g, m_w_in, m_b_f, m_qk_norm_g, m_w_out, m_w_ple, m_ple_norm_g, m_w_ple_gate,
           v_norm_g, v_w_in, v_b_f, v_qk_norm_g, v_w_out, v_w_ple, v_ple_norm_g, v_w_ple_gate):
    assert w_in.shape[0] == 2, "the schedule below is written for two layers"
    w_in0, *kept = _gather_first_layer(w_in, [w_out, w_ple, w_ple_gate])
    consts = (_head_block_diag(), _head_select(), _dil_bias(T_DIL_FWD), _dil_bias(T_DIL_BWD),
              _rope_tables(positions[0]))
    later = {}

    def outside0(landed):
        later["w_in1"] = landed[0]
        later["out1"] = landed[2::2]
        return _out_weights(0, *landed[1::2], ple_norm_g)

    inside0 = _in_weights(0, w_in0, norm_g, b_f, qk_norm_g)
    h1, saved0, outside0 = _layer_fwd(0, x[0], p[0, 0], inside0, outside0, consts, _gather_comm(kept))
    wts0 = (inside0, outside0)
    wts1 = (_in_weights(1, later["w_in1"], norm_g, b_f, qk_norm_g), _out_weights(1, *later["out1"], ple_norm_g))
    (dh, sq), saved1, _ = _layer_fwd(1, h1, p[1, 0], *wts1, consts, target=loss_target[0])

    core = lax.axis_index("c").astype(jnp.int32).reshape(1)
    chip = (2 * lax.axis_index("x") + lax.axis_index("y")).astype(jnp.int32).reshape(1)
    dh, grads1, _ = _layer_bwd(1, dh, p[1, 0], wts1, consts, saved1)
    pending = (_slot_layout_in(grads1[0]),) + _slot_layout_out(*grads1[1:4])
    dh, grads0, hidden = _layer_bwd(0, dh, p[0, 0], wts0, consts, saved0, pending, core, chip)
    reduced0 = [hidden[7]] + hidden[4:7]
    reduced1 = hidden[0:4]
    grad_x = dh[None]
    small = [grads0[4:], grads1[4:]]
    n_layers = 2

    outs = {}
    for t, (name, w, m, v) in enumerate((("w_in", w_in, m_w_in, v_w_in), ("w_out", w_out, m_w_out, v_w_out),
                                         ("w_ple", w_ple, m_w_ple, v_w_ple),
                                         ("w_ple_gate", w_ple_gate, m_w_ple_gate, v_w_ple_gate))):
        if name == "w_in":
            g = _assemble_halves("assemble_w_in", w.shape, (reduced0[t], reduced1[t]), core)
            g_flat = _to_flat(g)
            flat = _adamw_flat("adamw_w_in", _to_flat(w), g_flat, _to_flat(m), _to_flat(v))
            outs[name] = tuple(_from_flat(f, w.shape) for f in (g_flat,) + tuple(flat))
        else:
            outs[name] = tuple(_adamw(f"adamw_{name}", w, (reduced0[t], reduced1[t]), m, v, core))

    part = _pack_small(jnp.stack([s[0] for s in small]), jnp.stack([s[3] for s in small]),
                       jnp.stack([s[2] for s in small]).reshape(n_layers, 4, N_HEADS, HEAD_DIM).sum(axis=2),
                       jnp.stack([s[1] for s in small]), 0.5 / D_MODEL * jnp.sum(sq))
    packed = _small_allreduce_adamw(part, _pack_small(norm_g, ple_norm_g, qk_norm_g, b_f),
                                    _pack_small(m_norm_g, m_ple_norm_g, m_qk_norm_g, m_b_f),
                                    _pack_small(v_norm_g, v_ple_norm_g, v_qk_norm_g, v_b_f))
    loss = packed[0][SMALL_ROWS - 1, LANE - 1]
    sm = [_unpack_small(a) for a in packed]
    for i, name in enumerate(("norm_g", "ple_norm_g", "qk_norm_g", "b_f")):
        outs[name] = tuple(sm[j][i] for j in range(4))

    order = ("norm_g", "w_in", "b_f", "qk_norm_g", "w_out", "w_ple", "ple_norm_g", "w_ple_gate")
    return (loss, grad_x) + tuple(outs[n][j] for j in range(4) for n in order)
```

```python
import functools
from typing import Any, Callable, NamedTuple, Sequence

import numpy as np
import jax
import jax.numpy as jnp
from jax import lax
from jax.experimental import pallas as pl
from jax.experimental.pallas import tpu as pltpu

F32 = jnp.float32
BF16 = jnp.bfloat16
MESH = pl.DeviceIdType.MESH

D_MODEL = 1024
HEAD_DIM = 64
D_BRANCH = 512
N_HEADS = 8
N_PAIRS = 4
N_IN = 4104
N_MAIN = 4096
N_ALL = 4224
PLE_DIM = 256
ROPE_THETA = 500000.0
ROPE_HALF = 8
EPS = 1e-6
NEG = -1e30
M_INIT = -1e29
Q_SCALE = HEAD_DIM ** -0.5
LOG2E = 1.4426950408889634
LN2 = 0.6931471805599453
DIL_PATTERNS = ((128, 1), (512, 4), (2048, 16))
DIL_BACK = 2048
ADAM_LR, ADAM_B1, ADAM_B2, ADAM_EPS, ADAM_WD, ADAM_STEP = 0.001, 0.9, 0.999, 1e-08, 0.01, 10
VMEM_LIMIT = 56 * 1024 * 1024
LANE = 128


def _dot(a, b):
    return jnp.dot(a, b, preferred_element_type=F32)


def _dot_nt(a, b):
    return lax.dot_general(a, b, (((1,), (1,)), ((), ())), preferred_element_type=F32)


def _dot_tn(a, b):
    return lax.dot_general(a, b, (((0,), (0,)), ((), ())), preferred_element_type=F32)


def _split_dot(x, w):
    hi = x.astype(BF16)
    lo = (x - hi.astype(F32)).astype(BF16)
    return _dot(hi, w) + _dot(lo, w)


def _head_sums(x, bs):
    w = bs.shape[0]
    return jnp.concatenate([_split_dot(x[:, w * k:w * (k + 1)], bs) for k in range(x.shape[1] // w)], axis=1)


def _split3_dot(w, x):
    hi = x.astype(BF16)
    r1 = x - hi.astype(F32)
    mid = r1.astype(BF16)
    lo = (r1 - mid.astype(F32)).astype(BF16)
    return _dot(w, hi) + _dot(w, mid) + _dot(w, lo)


def _sigmoid(x):
    return 1.0 / (1.0 + jnp.exp(-x))


def _params(n_grid):
    return pltpu.CompilerParams(dimension_semantics=("arbitrary",) * n_grid,
                                vmem_limit_bytes=VMEM_LIMIT)


def _full(shape):
    nd = len(shape)
    return pl.BlockSpec(shape, lambda *_: (0,) * nd)


ANY = pl.BlockSpec(memory_space=pl.ANY)
VMEM_SPEC = pl.BlockSpec(memory_space=pltpu.VMEM)


class _Comm(NamedTuple):
    ins: Sequence[Any]
    out_shapes: Sequence[Any]
    sems: Sequence[Any]
    make: Callable[..., Any]


def _fuse_comm(body, n_in, n_out, comm, grid):
    if comm is None:
        return body
    nci, nco, ncs = len(comm.ins), len(comm.out_shapes), len(comm.sems)

    def fused(*refs):
        a, b = n_in + nci, n_in + nci + n_out
        ins, cins, outs, couts = refs[:n_in], refs[n_in:a], refs[a:b], refs[b:b + nco]
        scratch, sems = refs[b + nco:len(refs) - ncs], refs[len(refs) - ncs:]
        first = functools.reduce(jnp.logical_and, [pl.program_id(d) == 0 for d in range(len(grid))])
        last = functools.reduce(jnp.logical_and, [pl.program_id(d) == n - 1 for d, n in enumerate(grid)])

        @pl.when(first)
        def _():
            for cp in comm.make(cins, couts, sems):
                cp.start()

        body(*ins, *outs, *scratch)

        @pl.when(last)
        def _():
            for cp in comm.make(cins, couts, sems):
                cp.wait()

    return fused


def _comm_args(comm):
    if comm is None:
        return [], [], [], [], []
    return (list(comm.ins), [ANY] * len(comm.ins), [ANY] * len(comm.out_shapes), list(comm.out_shapes),
            list(comm.sems))


HEADS_PER_BLOCK = 4


def _head_block_diag():
    i = np.arange(HEADS_PER_BLOCK * HEAD_DIM)
    return jnp.asarray((i[:, None] // HEAD_DIM == i[None, :] // HEAD_DIM).astype(np.float32), BF16)


def _head_select():
    i = np.arange(2 * D_BRANCH)
    j = np.arange(LANE)
    return jnp.asarray((i[:, None] // HEAD_DIM == j[None, :]).astype(np.float32), BF16)


def _dil_bias(t):
    nb = DIL_BACK // t + 1
    qi = np.arange(t)[:, None]
    ki = np.arange(t)[None, :]
    tiles = []
    for r in range(nb):
        d = r * t + qi - ki
        mult = np.zeros((t, t), np.int64)
        for window, dil in DIL_PATTERNS:
            mult += ((d >= 0) & (d <= window) & (d % dil == 0)).astype(np.int64)
        b = np.where(mult > 0, np.log2(np.maximum(mult, 1)), NEG).astype(np.float32)
        tiles.append(b.T)
    return jnp.asarray(np.stack(tiles))


def _rope_tables(positions):
    inv_freq = ROPE_THETA ** (-jnp.arange(ROPE_HALF, dtype=F32) / ROPE_HALF)
    ang = positions.astype(F32)[:, None] * inv_freq
    cos, sin = jnp.cos(ang), jnp.sin(ang)
    s = positions.shape[0]
    rest = HEAD_DIM - 2 * ROPE_HALF
    one, zero, zero8 = jnp.ones((s, rest), F32), jnp.zeros((s, rest), F32), jnp.zeros((s, ROPE_HALF), F32)
    c = jnp.concatenate([cos, cos, one], axis=1)
    s1 = jnp.concatenate([zero8, sin, zero], axis=1)
    s2 = jnp.concatenate([-sin, zero8, zero], axis=1)
    return tuple(jnp.tile(t, (1, 2)) for t in (c, s1, s2))


def _rope_fwd(x, c, s1, s2):
    return x * c + pltpu.roll(x, ROPE_HALF, 1) * s1 + pltpu.roll(x, LANE - ROPE_HALF, 1) * s2


def _rope_bwd(dy, c, s1, s2):
    return dy * c + pltpu.roll(dy * s1, LANE - ROPE_HALF, 1) + pltpu.roll(dy * s2, ROPE_HALF, 1)


def _log_sigmoid(x):
    return jnp.minimum(x, 0.0) - jnp.log(1.0 + jnp.exp(-jnp.abs(x)))


def _inproj(l, h, norm_g, w_all, b_f, qkg, bsum, rope, tm):
    s = h.shape[0]
    rc, rs1, rs2 = rope

    def body(h_ref, g_ref, w_ref, bf_ref, qkg_ref, bsum_ref, rc_ref, rs1_ref, rs2_ref,
             u_ref, z_ref, qa_ref, ka_ref, va_ref, qb_ref, kb_ref, vb_ref, cs_ref, ct_ref, vat_ref, vbt_ref,
             qat_ref, qbt_ref, carry):
        @pl.when(pl.program_id(0) == 0)
        def _():
            carry[...] = jnp.zeros_like(carry)

        hh = h_ref[...]
        r = lax.rsqrt(jnp.mean(hh * hh, axis=-1, keepdims=True) + EPS)
        u = (hh * r * g_ref[...]).astype(BF16)
        u_ref[...] = u
        for k in range(N_ALL // LANE // 3):
            cols = slice(3 * LANE * k, 3 * LANE * (k + 1))
            z_ref[:, cols] = _dot(u, w_ref[:, cols])
        bs = bsum_ref[...]

        def head_norm(x, row):
            ms = _head_sums(x * x, bs) * (1.0 / HEAD_DIM)
            return x * lax.rsqrt(ms + EPS) * qkg_ref[row:row + 1, :]

        def seg(k):
            return z_ref[:, D_BRANCH * k:D_BRANCH * (k + 1)]

        qa = head_norm(seg(0), 0) * (Q_SCALE * LOG2E)
        qa_ref[...] = qa.astype(BF16)
        qat_ref[...] = qa.T.astype(BF16)
        ka_ref[...] = head_norm(seg(1), 1).astype(BF16)
        va_ref[...] = seg(2).astype(BF16)
        vat_ref[...] = seg(2).T.astype(BF16)
        qn = head_norm(seg(4), 2) * (Q_SCALE * LOG2E)
        kn = head_norm(seg(5), 3)
        c, s1, s2 = rc_ref[...], rs1_ref[...], rs2_ref[...]
        for k in range(D_BRANCH // LANE):
            cols = slice(LANE * k, LANE * (k + 1))
            qr = _rope_fwd(qn[:, cols], c, s1, s2)
            qb_ref[:, cols] = qr.astype(BF16)
            qbt_ref[cols, :] = qr.T.astype(BF16)
            kb_ref[:, cols] = _rope_fwd(kn[:, cols], c, s1, s2).astype(BF16)
        vb_ref[...] = seg(6).astype(BF16)
        vbt_ref[...] = seg(6).T.astype(BF16)
        logf = _log_sigmoid(z_ref[:, N_MAIN:N_ALL] + bf_ref[...])
        csum = _split3_dot(_tri(tm, False), logf) + carry[0:1, :]
        carry[...] = jnp.broadcast_to(csum[tm - 1:tm, :], carry.shape)
        csum = csum * LOG2E
        ct = csum.T
        for p in range(N_PAIRS):
            cs_ref[:, LANE * p:LANE * (p + 1)] = csum if p == 0 else pltpu.roll(csum, LANE - 2 * p, 1)
            ct_ref[p, :, :] = ct[2 * p:2 * p + 2, :]

    row = lambda w: pl.BlockSpec((tm, w), lambda i: (i, 0))
    colt = pl.BlockSpec((D_BRANCH, tm), lambda i: (0, i))
    bf = lambda: jax.ShapeDtypeStruct((s, D_BRANCH), BF16)
    bft = lambda: jax.ShapeDtypeStruct((D_BRANCH, s), BF16)
    return pl.pallas_call(
        body, name=f"inproj_l{l}", grid=(s // tm,),
        in_specs=[row(D_MODEL), _full((1, D_MODEL)), _full((D_MODEL, N_ALL)), _full((1, LANE)),
                  _full((8, D_BRANCH)), _full((HEADS_PER_BLOCK * HEAD_DIM,) * 2), row(LANE), row(LANE), row(LANE)],
        out_specs=[row(D_MODEL), row(N_ALL)] + [row(D_BRANCH)] * 6
        + [row(N_PAIRS * LANE), pl.BlockSpec((N_PAIRS, 2, tm), lambda i: (0, 0, i)), colt, colt, colt, colt],
        out_shape=[jax.ShapeDtypeStruct((s, D_MODEL), BF16), jax.ShapeDtypeStruct((s, N_ALL), F32),
                   bf(), bf(), bf(), bf(), bf(), bf(), jax.ShapeDtypeStruct((s, N_PAIRS * LANE), F32),
                   jax.ShapeDtypeStruct((N_PAIRS, 2, s), F32), bft(), bft(), bft(), bft()],
        scratch_shapes=[pltpu.VMEM((8, LANE), F32)],
        compiler_params=_params(1),
    )(h, norm_g, w_all, b_f, qkg, bsum, rc, rs1, rs2)


def _tri(t, upper):
    a = lax.broadcasted_iota(jnp.int32, (t, t), 0)
    b = lax.broadcasted_iota(jnp.int32, (t, t), 1)
    return jnp.where((b >= a) if upper else (b <= a), 1.0, 0.0).astype(BF16)


def _attn_fwd(name, fox, qt, k, vt, extra, t, comm=None):
    s = k.shape[0]
    nq = s // t
    nb = DIL_BACK // t + 1

    def body(*refs):
        if fox:
            q_ref, k_ref, vt_ref, ccol_ref, crow_ref, o_ref, lse_ref, m_scr, l_scr, acc_scr = refs
        else:
            q_ref, k_ref, vt_ref, bias_ref, o_ref, lse_ref, m_scr, l_scr, acc_scr = refs
        i = pl.program_id(1)
        sub = lax.broadcasted_iota(jnp.int32, (LANE, t), 0)
        first = sub < HEAD_DIM
        qq = q_ref[...]
        zero = jnp.zeros_like(qq)
        qh = (jnp.where(first, qq, zero), jnp.where(first, zero, qq))
        m_scr[...] = jnp.full(m_scr.shape, M_INIT, F32)
        l_scr[...] = jnp.zeros_like(l_scr)
        acc_scr[...] = jnp.zeros_like(acc_scr)
        ones = jnp.ones((16, t), BF16)

        half = t // 2
        whole, lo, hi = slice(0, t), slice(0, half), slice(half, t)

        def block(j, ksl, qsl, causal):
            nk_, nq_ = ksl.stop - ksl.start, qsl.stop - qsl.start
            rows = pl.ds(pl.multiple_of(j * t + ksl.start, LANE), nk_)
            ks = k_ref[rows, :]
            vts = jnp.concatenate([vt_ref[:, rows], ones[:, :nk_]], axis=0)
            if fox:
                ccol = ccol_ref[rows, :]
            for h in range(2):
                st = _dot(ks, qh[h][:, qsl])
                if fox:
                    st = st + (crow_ref[h:h + 1, qsl] - ccol[:, h:h + 1])
                    if causal:
                        ki = lax.broadcasted_iota(jnp.int32, (nk_, nq_), 0) + ksl.start
                        qi = lax.broadcasted_iota(jnp.int32, (nk_, nq_), 1) + qsl.start
                        st = jnp.where(ki <= qi, st, NEG)
                else:
                    st = st + bias_ref[i - j, ksl, qsl]
                m_old = m_scr[h, :, qsl]
                m_new = jnp.maximum(m_old, jnp.max(st, axis=0, keepdims=True))
                alpha = jnp.exp2(m_old - m_new)
                pb = jnp.exp2(st - m_new).astype(BF16)
                pv = _dot(vts, pb)
                l_scr[h, :, qsl] = alpha * l_scr[h, :, qsl] + pv[LANE:LANE + 1, :]
                acc_scr[h, :, qsl] = alpha * acc_scr[h, :, qsl] + pv[:LANE, :]
                m_scr[h, :, qsl] = m_new

        def full(j, c):
            block(j, whole, whole, False)
            return c

        if fox:
            lax.fori_loop(0, i, full, 0)
        else:
            @pl.when(i >= nb - 1)
            def _():
                block(i - (nb - 1), lo, lo, False)
                block(i - (nb - 1), hi, whole, False)

            lax.fori_loop(jnp.maximum(i - (nb - 2), 0), i, full, 0)
        if fox:
            block(i, whole, whole, True)
        else:
            block(i, lo, lo, False)
            block(i, whole, hi, False)

        ot = jnp.where(first, acc_scr[0] / l_scr[0], acc_scr[1] / l_scr[1])
        o_ref[...] = ot.T
        for h in range(2):
            lse_ref[h:h + 1, :] = m_scr[h] + jnp.log2(l_scr[h])

    qspec = pl.BlockSpec((t, LANE), lambda hp, i: (i, hp))
    qtspec = pl.BlockSpec((LANE, t), lambda hp, i: (hp, i))
    kspec = pl.BlockSpec((s, LANE), lambda hp, i: (0, hp))
    vtspec = pl.BlockSpec((LANE, s), lambda hp, i: (hp, 0))
    in_specs = [qtspec, kspec, vtspec]
    if fox:
        in_specs += [kspec, pl.BlockSpec((None, 2, t), lambda hp, i: (hp, 0, i))]
    else:
        in_specs += [_full((nb, t, t))]
    grid = (N_PAIRS, nq)
    c_in, c_ispec, c_ospec, c_oshape, c_scr = _comm_args(comm)
    return pl.pallas_call(
        _fuse_comm(body, len(in_specs), 2, comm, grid), name=name, grid=grid,
        in_specs=in_specs + c_ispec,
        out_specs=[qspec, pl.BlockSpec((None, 2, t), lambda hp, i: (hp, 0, i))] + c_ospec,
        out_shape=[jax.ShapeDtypeStruct((s, D_BRANCH), F32), jax.ShapeDtypeStruct((N_PAIRS, 2, s), F32)] + c_oshape,
        scratch_shapes=[pltpu.VMEM((2, 1, t), F32), pltpu.VMEM((2, 1, t), F32), pltpu.VMEM((2, LANE, t), F32)]
        + c_scr,
        compiler_params=_params(2),
    )(qt, k, vt, *extra, *c_in)


def _attn_bwd(name, fox, q, k, v, do, lse_t, delta_t, pair_offset, extra, t, comm=None):
    s = q.shape[0]
    nk = s // t
    nb = DIL_BACK // t + 1

    def body(*refs):
        if fox:
            (q_ref, k_ref, v_ref, do_ref, lse_ref, delta_ref, ccol_ref, crow_ref,
             dqt_ref, dk_ref, dv_ref, dc_ref, drow_ref) = refs
        else:
            q_ref, k_ref, v_ref, do_ref, lse_ref, delta_ref, bias_ref, dqt_ref, dk_ref, dv_ref = refs
        j = pl.program_id(1)

        @pl.when(j == 0)
        def _():
            dqt_ref[...] = jnp.zeros_like(dqt_ref)
            if fox:
                drow_ref[...] = jnp.zeros_like(drow_ref)

        lane = lax.broadcasted_iota(jnp.int32, (t, LANE), 1)
        first = lane < HEAD_DIM
        ks = k_ref[...]
        vs = v_ref[...]
        kt = ks.astype(F32).T
        sub = lax.broadcasted_iota(jnp.int32, (LANE, t), 0)
        kth = (jnp.where(sub < HEAD_DIM, kt, 0.0).astype(BF16), jnp.where(sub < HEAD_DIM, 0.0, kt).astype(BF16))
        dk_ref[...] = jnp.zeros_like(dk_ref)
        dv_ref[...] = jnp.zeros_like(dv_ref)
        if fox:
            dc_ref[...] = jnp.zeros_like(dc_ref)
            ccol = ccol_ref[...]

        half = t // 2
        whole, lo, hi = slice(0, t), slice(0, half), slice(half, t)

        def block(i, ksl, qsl, causal):
            nk_, nq_ = ksl.stop - ksl.start, qsl.stop - qsl.start
            rows = pl.ds(pl.multiple_of(i * t + qsl.start, LANE), nq_)
            qq = q_ref[rows, :]
            dd = do_ref[rows, :]
            zero = jnp.zeros_like(qq)
            qh = (jnp.where(first[:nq_], qq, zero), jnp.where(first[:nq_], zero, qq))
            dh = (jnp.where(first[:nq_], dd, zero), jnp.where(first[:nq_], zero, dd))
            for h in range(2):
                st = _dot_nt(ks[ksl, :], qh[h])
                if fox:
                    st = st + (crow_ref[h:h + 1, rows] - ccol[ksl, h:h + 1])
                    if causal:
                        ki = lax.broadcasted_iota(jnp.int32, (nk_, nq_), 0) + ksl.start
                        qi = lax.broadcasted_iota(jnp.int32, (nk_, nq_), 1) + qsl.start
                        st = jnp.where(ki <= qi, st, NEG)
                else:
                    st = st + bias_ref[i - j, ksl, qsl]
                pt = jnp.exp2(st - lse_ref[h:h + 1, rows])
                dpt = _dot_nt(vs[ksl, :], dh[h])
                dst = pt * (dpt - delta_ref[h:h + 1, rows])
                dv_ref[ksl, :] += _dot(pt.astype(BF16), dh[h])
                dsb = dst.astype(BF16)
                dk_ref[ksl, :] += _dot(dsb, qh[h])
                dqt_ref[:, rows] += _dot(kth[h][:, ksl], dsb)
                if fox:
                    dc_ref[ksl, :] -= jnp.where(lane[:nk_] == h, jnp.sum(dst, axis=1, keepdims=True), 0.0)
                    drow_ref[h:h + 1, rows] += jnp.sum(dst, axis=0, keepdims=True)

        def full(i, c):
            block(i, whole, whole, False)
            return c

        block(j, lo, whole, True)
        block(j, hi, hi, True)
        if fox:
            lax.fori_loop(j + 1, nk, full, 0)
        else:
            lax.fori_loop(j + 1, jnp.minimum(j + nb - 1, nk), full, 0)

            @pl.when(j + nb - 1 < nk)
            def _():
                block(j + nb - 1, lo, lo, False)
                block(j + nb - 1, hi, whole, False)

    kspec = pl.BlockSpec((t, LANE), lambda hp, j: (j, hp))
    qspec = pl.BlockSpec((s, LANE), lambda hp, j: (0, hp))
    rowspec = pl.BlockSpec((None, 2, s), lambda hp, j: (hp, 0, 0))
    drowspec = pl.BlockSpec((None, 2, s), lambda hp, j: (hp + pair_offset, 0, 0))
    in_specs = [qspec, kspec, kspec, qspec, rowspec, drowspec]
    out_specs = [pl.BlockSpec((LANE, s), lambda hp, j: (hp, 0)), kspec, kspec]
    out_shape = [jax.ShapeDtypeStruct((D_BRANCH, s), F32), jax.ShapeDtypeStruct((s, D_BRANCH), F32),
                 jax.ShapeDtypeStruct((s, D_BRANCH), F32)]
    if fox:
        in_specs += [kspec, rowspec]
        out_specs += [kspec, rowspec]
        out_shape += [jax.ShapeDtypeStruct((s, N_PAIRS * LANE), F32), jax.ShapeDtypeStruct((N_PAIRS, 2, s), F32)]
    else:
        in_specs += [_full((nb, t, t))]
    grid = (N_PAIRS, nk)
    c_in, c_ispec, c_ospec, c_oshape, c_scr = _comm_args(comm)
    return pl.pallas_call(
        _fuse_comm(body, len(in_specs), len(out_specs), comm, grid), name=name, grid=grid,
        in_specs=in_specs + c_ispec, out_specs=out_specs + c_ospec, out_shape=out_shape + c_oshape,
        scratch_shapes=c_scr, compiler_params=_params(2),
    )(q, k, v, do, lse_t, delta_t, *extra, *c_in)


def _silu(x):
    return x * _sigmoid(x)


def _outproj(l, h, oa, ob, z, p_i, w_out, ple_g, w_gate, w_ple, tm, target=None):
    s = h.shape[0]
    last = target is not None

    def body(h_ref, oa_ref, ob_ref, ga_ref, gb_ref, p_ref, wo_ref, pg_ref, wg_ref, wp_ref, *rest):
        if last:
            t_ref, h2_ref, gate_ref, out_ref, acc_ref = rest
        else:
            h2_ref, gate_ref, out_ref = rest
        a = jnp.concatenate([oa_ref[...] * _silu(ga_ref[...]), ob_ref[...] * _silu(gb_ref[...])], axis=1)
        h2 = h_ref[...] + _dot(a.astype(BF16), wo_ref[...])
        h2_ref[...] = h2
        r = lax.rsqrt(jnp.mean(h2 * h2, axis=-1, keepdims=True) + EPS)
        n2 = (h2 * r * pg_ref[...]).astype(BF16)
        gate = _sigmoid(_dot(n2, wg_ref[...]))
        e = _dot(p_ref[...].astype(BF16), wp_ref[...])
        gate_ref[...] = gate
        out = h2 + e * gate
        if not last:
            out_ref[...] = out
            return

        @pl.when(pl.program_id(0) == 0)
        def _():
            acc_ref[...] = jnp.zeros_like(acc_ref)

        err = out - t_ref[...]
        out_ref[...] = err * (1.0 / D_MODEL)
        e2 = err * err
        rows = e2[0:8, :]
        for k in range(1, tm // 8):
            rows = rows + e2[8 * k:8 * (k + 1), :]
        part = rows[:, 0:LANE]
        for k in range(1, D_MODEL // LANE):
            part = part + rows[:, LANE * k:LANE * (k + 1)]
        acc_ref[...] += part

    row = lambda w: pl.BlockSpec((tm, w), lambda i: (i, 0))
    zcol = lambda k: pl.BlockSpec((tm, D_BRANCH), lambda i: (i, k))
    f = lambda: jax.ShapeDtypeStruct((s, D_MODEL), F32)
    return pl.pallas_call(
        body, name=f"outproj_l{l}", grid=(s // tm,),
        in_specs=[row(D_MODEL), row(D_BRANCH), row(D_BRANCH), zcol(3), zcol(7), row(PLE_DIM),
                  _full((D_MODEL, D_MODEL)), _full((1, D_MODEL)), _full((D_MODEL, D_MODEL)),
                  _full((PLE_DIM, D_MODEL))] + ([row(D_MODEL)] if last else []),
        out_specs=[row(D_MODEL)] * 3 + ([_full((8, LANE))] if last else []),
        out_shape=[f(), f(), f()] + ([jax.ShapeDtypeStruct((8, LANE), F32)] if last else []),
        compiler_params=_params(1),
    )(h, oa, ob, z, z, p_i, w_out, ple_g, w_gate, w_ple, *([target] if last else []))


def _outproj_bwd(l, dout, h2, gate, w_ple, p_i, oa, ob, z, w_out_t, ple_g, w_gate_t, hsel, tm, comm=None):
    s = dout.shape[0]

    def body(do_ref, h2_ref, gate_ref, wp_ref, p_ref, oa_ref, ob_ref, ga_ref, gb_ref, wot_ref, pg_ref,
             wgt_ref, hsel_ref,
             dh2_ref, doa_ref, dob_ref, dga_ref, dgb_ref, delta_ref, dwo_ref, dwg_ref, dwp_ref, dpg_ref):
        @pl.when(pl.program_id(0) == 0)
        def _():
            dwo_ref[...] = jnp.zeros_like(dwo_ref)
            dwg_ref[...] = jnp.zeros_like(dwg_ref)
            dwp_ref[...] = jnp.zeros_like(dwp_ref)
            dpg_ref[...] = jnp.zeros_like(dpg_ref)

        dho = do_ref[...]
        h2 = h2_ref[...]
        pg = pg_ref[...]
        r = lax.rsqrt(jnp.mean(h2 * h2, axis=-1, keepdims=True) + EPS)
        n2 = (h2 * r * pg).astype(BF16)
        pb = p_ref[...].astype(BF16)
        g = gate_ref[...]
        e = _dot(pb, wp_ref[...])
        de = (dho * g).astype(BF16)
        dwp_ref[...] += _dot_tn(pb, de)
        dpre = (dho * e * g * (1.0 - g)).astype(BF16)
        dwg_ref[...] += _dot_tn(n2, dpre)
        dn2 = _dot(dpre, wgt_ref[...])
        dpg_ref[0:1, :] += jnp.sum(dn2 * h2 * r, axis=0, keepdims=True)
        wv = dn2 * pg
        dh2 = dho + r * wv - h2 * (r * r * r) * jnp.mean(wv * h2, axis=-1, keepdims=True)
        dh2_ref[...] = dh2
        dh2b = dh2.astype(BF16)
        ga, gb, oa, ob = ga_ref[...], gb_ref[...], oa_ref[...], ob_ref[...]
        sga, sgb = _sigmoid(ga), _sigmoid(gb)
        a = jnp.concatenate([oa * ga * sga, ob * gb * sgb], axis=1).astype(BF16)
        dwo_ref[...] += _dot_tn(a, dh2b)
        da = _dot(dh2b, wot_ref[...])
        da_a, da_b = da[:, :D_BRANCH], da[:, D_BRANCH:]
        doa = da_a * ga * sga
        dob = da_b * gb * sgb
        doa_ref[...] = doa.astype(BF16)
        dob_ref[...] = dob.astype(BF16)
        dga_ref[...] = (da_a * oa * sga * (1.0 + ga * (1.0 - sga))).astype(BF16)
        dgb_ref[...] = (da_b * ob * sgb * (1.0 + gb * (1.0 - sgb))).astype(BF16)
        prod = jnp.concatenate([doa * oa, dob * ob], axis=1)
        dt = _split_dot(prod, hsel_ref[...]).T
        for pp in range(2 * N_PAIRS):
            delta_ref[pp, :, :] = dt[2 * pp:2 * pp + 2, :]

    row = lambda w: pl.BlockSpec((tm, w), lambda i: (i, 0))
    zcol = lambda k: pl.BlockSpec((tm, D_BRANCH), lambda i: (i, k))
    grid = (s // tm,)
    c_in, c_ispec, c_ospec, c_oshape, c_scr = _comm_args(comm)
    return pl.pallas_call(
        _fuse_comm(body, 13, 10, comm, grid), name=f"outproj_bwd_l{l}", grid=grid,
        in_specs=[row(D_MODEL)] * 3 + [_full((PLE_DIM, D_MODEL)),
                  row(PLE_DIM), row(D_BRANCH), row(D_BRANCH), zcol(3), zcol(7),
                                        _full((D_MODEL, D_MODEL)), _full((1, D_MODEL)), _full((D_MODEL, D_MODEL)),
                                        _full((2 * D_BRANCH, LANE))] + c_ispec,
        out_specs=[row(D_MODEL)] + [row(D_BRANCH)] * 4
        + [pl.BlockSpec((2 * N_PAIRS, 2, tm), lambda i: (0, 0, i)), _full((D_MODEL, D_MODEL)),
           _full((D_MODEL, D_MODEL)), _full((PLE_DIM, D_MODEL)), _full((8, D_MODEL))] + c_ospec,
        out_shape=[jax.ShapeDtypeStruct((s, D_MODEL), F32)] + [jax.ShapeDtypeStruct((s, D_BRANCH), BF16)] * 4
        + [jax.ShapeDtypeStruct((2 * N_PAIRS, 2, s), F32), jax.ShapeDtypeStruct((D_MODEL, D_MODEL), F32),
           jax.ShapeDtypeStruct((D_MODEL, D_MODEL), F32), jax.ShapeDtypeStruct((PLE_DIM, D_MODEL), F32),
           jax.ShapeDtypeStruct((8, D_MODEL), F32)] + c_oshape,
        scratch_shapes=c_scr, compiler_params=_params(1),
    )(dout, h2, gate, w_ple, p_i, oa, ob, z, z, w_out_t, ple_g, w_gate_t, hsel, *c_in)


def _inproj_bwd_prep(l, z, dqt_a, dk_a, dv_a, dqt_b, dk_b, dv_b, dga, dgb, dc_spread, dc_rows, b_f, qkg, bsum,
                     rope, tm):
    s = z.shape[0]
    n = s // tm
    rc, rs1, rs2 = rope

    def body(zqa_ref, zka_ref, zqb_ref, zkb_ref, zf_ref, dqta_ref, dka_ref, dva_ref, dqtb_ref, dkb_ref, dvb_ref,
             dga_ref, dgb_ref, dc_ref, drow_ref, bf_ref, qkg_ref, bsum_ref, rc_ref, rs1_ref, rs2_ref,
             dz_ref, dqkg_ref, dbf_ref, carry):
        @pl.when(pl.program_id(0) == 0)
        def _():
            dqkg_ref[...] = jnp.zeros_like(dqkg_ref)
            dbf_ref[...] = jnp.zeros_like(dbf_ref)
            carry[...] = jnp.zeros_like(carry)

        lane = lax.broadcasted_iota(jnp.int32, (tm, LANE), 1)
        dc = jnp.concatenate([drow_ref[p] for p in range(N_PAIRS)] + [jnp.zeros((LANE - N_HEADS, tm), F32)], axis=0).T
        for p in range(N_PAIRS):
            part = jnp.where(lane < 2, dc_ref[:, LANE * p:LANE * (p + 1)], 0.0)
            dc = dc + (part if p == 0 else pltpu.roll(part, 2 * p, 1))
        dlogf = _split3_dot(_tri(tm, True), dc) + carry[0:1, :]
        carry[...] = jnp.broadcast_to(dlogf[0:1, :], carry.shape)

        bs = bsum_ref[...]
        c, s1, s2 = rc_ref[...], rs1_ref[...], rs2_ref[...]

        def unrope(dy):
            return jnp.concatenate([_rope_bwd(dy[:, LANE * k:LANE * (k + 1)], c, s1, s2)
                                    for k in range(D_BRANCH // LANE)], axis=1)

        def norm_bwd(k, row, dy, x_ref):
            x = x_ref[...]
            r = lax.rsqrt(_head_sums(x * x, bs) * (1.0 / HEAD_DIM) + EPS)
            dqkg_ref[row:row + 1, :] += jnp.sum(dy * x * r, axis=0, keepdims=True)
            w = dy * qkg_ref[row:row + 1, :]
            dx = r * w - x * (r * r * r) * (_head_sums(w * x, bs) * (1.0 / HEAD_DIM))
            dz_ref[:, D_BRANCH * k:D_BRANCH * (k + 1)] = dx.astype(BF16)

        norm_bwd(0, 0, dqta_ref[...].T * Q_SCALE, zqa_ref)
        norm_bwd(1, 1, dka_ref[...] * LN2, zka_ref)
        dz_ref[:, 2 * D_BRANCH:3 * D_BRANCH] = dva_ref[...].astype(BF16)
        dz_ref[:, 3 * D_BRANCH:4 * D_BRANCH] = dga_ref[...]
        norm_bwd(4, 2, unrope(dqtb_ref[...].T * Q_SCALE), zqb_ref)
        norm_bwd(5, 3, unrope(dkb_ref[...] * LN2), zkb_ref)
        dz_ref[:, 6 * D_BRANCH:7 * D_BRANCH] = dvb_ref[...].astype(BF16)
        dz_ref[:, 7 * D_BRANCH:8 * D_BRANCH] = dgb_ref[...]
        dfa = dlogf * _sigmoid(-(zf_ref[...] + bf_ref[...]))
        dz_ref[:, N_MAIN:N_ALL] = dfa.astype(BF16)
        dbf_ref[0:1, :] += jnp.sum(dfa, axis=0, keepdims=True)

    row = lambda w: pl.BlockSpec((tm, w), lambda i: (n - 1 - i, 0))
    colt = pl.BlockSpec((D_BRANCH, tm), lambda i: (0, n - 1 - i))
    zcol = lambda k: pl.BlockSpec((tm, D_BRANCH), lambda i: (n - 1 - i, k))
    zf = pl.BlockSpec((tm, LANE), lambda i: (n - 1 - i, N_MAIN // LANE))
    return pl.pallas_call(
        body, name=f"inproj_bwd_prep_l{l}", grid=(n,),
        in_specs=[zcol(0), zcol(1), zcol(4), zcol(5), zf, colt, row(D_BRANCH), row(D_BRANCH), colt, row(D_BRANCH), row(D_BRANCH),
                  row(D_BRANCH), row(D_BRANCH), row(N_PAIRS * LANE),
                  pl.BlockSpec((N_PAIRS, 2, tm), lambda i: (0, 0, n - 1 - i)), _full((1, LANE)),
                  _full((8, D_BRANCH)), _full((HEADS_PER_BLOCK * HEAD_DIM,) * 2), row(LANE), row(LANE), row(LANE)],
        out_specs=[row(N_ALL), _full((8, D_BRANCH)), _full((8, LANE))],
        out_shape=[jax.ShapeDtypeStruct((s, N_ALL), BF16), jax.ShapeDtypeStruct((8, D_BRANCH), F32),
                   jax.ShapeDtypeStruct((8, LANE), F32)],
        scratch_shapes=[pltpu.VMEM((8, LANE), F32)],
        compiler_params=_params(1),
    )(z, z, z, z, z, dqt_a, dk_a, dv_a, dqt_b, dk_b, dv_b, dga, dgb, dc_spread, dc_rows, b_f, qkg, bsum, rc, rs1, rs2)


def _inproj_bwd_dx(l, dz, w_all_t, h, norm_g, dh2, tm, comm=None):
    s = dz.shape[0]

    def body(dz_ref, wt_ref, h_ref, g_ref, dh2_ref, dh_ref, dg_ref):
        @pl.when(pl.program_id(0) == 0)
        def _():
            dg_ref[...] = jnp.zeros_like(dg_ref)

        du = _dot(dz_ref[...], wt_ref[...])
        hh = h_ref[...]
        g = g_ref[...]
        r = lax.rsqrt(jnp.mean(hh * hh, axis=-1, keepdims=True) + EPS)
        dg_ref[0:1, :] += jnp.sum(du * hh * r, axis=0, keepdims=True)
        wv = du * g
        dh_ref[...] = dh2_ref[...] + r * wv - hh * (r * r * r) * jnp.mean(wv * hh, axis=-1, keepdims=True)

    row = lambda w: pl.BlockSpec((tm, w), lambda i: (i, 0))
    grid = (s // tm,)
    c_in, c_ispec, c_ospec, c_oshape, c_scr = _comm_args(comm)
    return pl.pallas_call(
        _fuse_comm(body, 5, 2, comm, grid), name=f"inproj_bwd_dx_l{l}", grid=grid,
        in_specs=[row(N_ALL), _full((N_ALL, D_MODEL)), row(D_MODEL), _full((1, D_MODEL)), row(D_MODEL)] + c_ispec,
        out_specs=[row(D_MODEL), _full((8, D_MODEL))] + c_ospec,
        out_shape=[jax.ShapeDtypeStruct((s, D_MODEL), F32), jax.ShapeDtypeStruct((8, D_MODEL), F32)] + c_oshape,
        scratch_shapes=c_scr, compiler_params=_params(1),
    )(dz, w_all_t, h, norm_g, dh2, *c_in)


def _inproj_bwd_dw(l, u, dz, tm, tn, comm=None):
    s = u.shape[0]

    def body(u_ref, dz_ref, dw_ref):
        @pl.when(pl.program_id(1) == 0)
        def _():
            dw_ref[...] = jnp.zeros_like(dw_ref)

        dw_ref[...] += _dot_tn(u_ref[...], dz_ref[...])

    grid = (N_ALL // tn, s // tm)
    c_in, c_ispec, c_ospec, c_oshape, c_scr = _comm_args(comm)
    return pl.pallas_call(
        _fuse_comm(body, 2, 1, comm, grid), name=f"inproj_bwd_dw_l{l}", grid=grid,
        in_specs=[pl.BlockSpec((tm, D_MODEL), lambda n, i: (i, 0)), pl.BlockSpec((tm, tn), lambda n, i: (i, n))]
        + c_ispec,
        out_specs=[pl.BlockSpec((D_MODEL, tn), lambda n, i: (0, n))] + c_ospec,
        out_shape=[jax.ShapeDtypeStruct((D_MODEL, N_ALL), F32)] + c_oshape,
        scratch_shapes=c_scr, compiler_params=_params(2),
    )(u, dz, *c_in)


def _adamw_math(w, g, m, v):
    m = ADAM_B1 * m + (1.0 - ADAM_B1) * g
    v = ADAM_B2 * v + (1.0 - ADAM_B2) * (g * g)
    m_hat = m / (1.0 - ADAM_B1 ** ADAM_STEP)
    v_hat = v / (1.0 - ADAM_B2 ** ADAM_STEP)
    delta = -ADAM_LR * (m_hat / (jnp.sqrt(v_hat) + ADAM_EPS) + ADAM_WD * w)
    return delta, m, v


def _adamw(name, w, halves, m, v, core):
    nl, r, c = w.shape
    hr = r // 2
    tr = 128 if hr % 128 == 0 else hr
    nb = hr // tr

    def body(core_ref, w_ref, own0_ref, oth0_ref, own1_ref, oth1_ref, m_ref, v_ref, g_ref, d_ref, nm_ref, nv_ref):
        first = pl.program_id(0) == 0
        own = jnp.where(first, own0_ref[...], own1_ref[...])
        oth = jnp.where(first, oth0_ref[...], oth1_ref[...])
        g = jnp.where(pl.program_id(1) // nb == core_ref[0], own, oth)
        d, nm, nv = _adamw_math(w_ref[...], g, m_ref[...], v_ref[...])
        g_ref[...] = g
        d_ref[...] = d
        nm_ref[...] = nm
        nv_ref[...] = nv

    spec = pl.BlockSpec((None, tr, c), lambda a, b, core_ref: (a, b, 0))
    gspec = pl.BlockSpec((tr, c), lambda a, b, core_ref: (b % nb, 0))
    shp = jax.ShapeDtypeStruct(w.shape, F32)
    return pl.pallas_call(
        body, name=name,
        grid_spec=pltpu.PrefetchScalarGridSpec(
            num_scalar_prefetch=1, grid=(nl, r // tr), in_specs=[spec] + [gspec] * 4 + [spec, spec],
            out_specs=[spec] * 4),
        out_shape=[shp, shp, shp, shp], compiler_params=_params(2),
    )(core, w, halves[0][0], halves[0][1], halves[1][0], halves[1][1], m, v)


def _assemble_halves(name, shape, halves, core):
    nl, r, c = shape
    hr = r // 2
    tr = 128 if hr % 128 == 0 else hr
    nb = hr // tr

    def body(core_ref, own0_ref, oth0_ref, own1_ref, oth1_ref, g_ref):
        first = pl.program_id(0) == 0
        own = jnp.where(first, own0_ref[...], own1_ref[...])
        oth = jnp.where(first, oth0_ref[...], oth1_ref[...])
        g_ref[...] = jnp.where(pl.program_id(1) // nb == core_ref[0], own, oth)

    of_layer = lambda l: pl.BlockSpec((tr, c), lambda a, b, core_ref: ((b % nb) * (a if l else 1 - a), 0))
    return pl.pallas_call(
        body, name=name,
        grid_spec=pltpu.PrefetchScalarGridSpec(
            num_scalar_prefetch=1, grid=(nl, r // tr), in_specs=[of_layer(0), of_layer(0), of_layer(1), of_layer(1)],
            out_specs=pl.BlockSpec((None, tr, c), lambda a, b, core_ref: (a, b, 0))),
        out_shape=jax.ShapeDtypeStruct(shape, F32), compiler_params=_params(2),
    )(core, halves[0][0], halves[0][1], halves[1][0], halves[1][1])


W_IN_FLAT_STEPS = 19


def _to_flat(a):
    nl, r, c = a.shape
    return jnp.transpose(jnp.transpose(a, (2, 0, 1)).reshape(c, nl, r // LANE, LANE), (0, 2, 1, 3)).reshape(-1, LANE)


def _from_flat(f, shape):
    nl, r, c = shape
    return jnp.transpose(jnp.transpose(f.reshape(c, r // LANE, nl, LANE), (0, 2, 1, 3)).reshape(c, nl, r), (1, 2, 0))


def _adamw_flat(name, w, g, m, v):
    n = w.shape[0]
    tr = n // W_IN_FLAT_STEPS

    def body(w_ref, g_ref, m_ref, v_ref, d_ref, nm_ref, nv_ref):
        d, nm, nv = _adamw_math(w_ref[...], g_ref[...], m_ref[...], v_ref[...])
        d_ref[...] = d
        nm_ref[...] = nm
        nv_ref[...] = nv

    spec = pl.BlockSpec((tr, LANE), lambda i: (i, 0))
    shp = jax.ShapeDtypeStruct(w.shape, F32)
    return pl.pallas_call(body, name=name, grid=(W_IN_FLAT_STEPS,), in_specs=[spec] * 4, out_specs=[spec] * 3,
                          out_shape=[shp, shp, shp], compiler_params=_params(1))(w, g, m, v)


def _pair_sum(name, g, x, c, narrow=False):
    n, r, cc = g.shape
    hr = r // 2
    tr = 128 if hr % 128 == 0 else hr
    nb = hr // tr

    def body(c_ref, g_ref, x_ref, o_ref, *narrow_ref):
        total = g_ref[...] + x_ref[...]
        o_ref[...] = total
        if narrow:
            narrow_ref[0][...] = total.astype(BF16)

    spec = pl.BlockSpec((None, tr, cc), lambda i, j, c_ref: (i, j, 0))
    shapes = [jax.ShapeDtypeStruct((n, hr, cc), F32)] + ([jax.ShapeDtypeStruct((n, hr, cc), BF16)] if narrow else [])
    return pl.pallas_call(
        body, name=name,
        grid_spec=pltpu.PrefetchScalarGridSpec(
            num_scalar_prefetch=1, grid=(n, nb),
            in_specs=[pl.BlockSpec((None, tr, cc), lambda i, j, c_ref: (i, c_ref[0] * nb + j, 0)), spec],
            out_specs=[spec] * len(shapes)),
        out_shape=shapes, compiler_params=_params(2),
    )(c, g, x)


def _sum_slots(name, own, landed, chip):
    n, r, c = own.shape
    tr = 128 if r % 128 == 0 else r

    def body(chip_ref, a_ref, b_ref, c_ref, d_ref, o_ref):
        o_ref[...] = ((a_ref[...] + b_ref[...].astype(F32)) + c_ref[...].astype(F32)) + d_ref[...].astype(F32)

    slot = lambda d: pl.BlockSpec((None, tr, c), lambda j, chip_ref: ((chip_ref[0] + d) % n, j, 0))
    return pl.pallas_call(
        body, name=name,
        grid_spec=pltpu.PrefetchScalarGridSpec(
            num_scalar_prefetch=1, grid=(r // tr,), in_specs=[slot(0), slot(1), slot(2), slot(3)],
            out_specs=pl.BlockSpec((tr, c), lambda j, chip_ref: (j, 0))),
        out_shape=jax.ShapeDtypeStruct((r, c), F32), compiler_params=_params(1),
    )(chip, own, landed, landed, landed)


def _me():
    return lax.axis_index("x"), lax.axis_index("y"), lax.axis_index("c")


def _other_chips(x, y):
    return [(1 - x, y), (x, 1 - y), (1 - x, 1 - y)]


def _dma_sems(*counts):
    return [pltpu.SemaphoreType.DMA((n,)) for n in counts]


def _half_rows(rows, which, align):
    return pl.ds(pl.multiple_of(which * (rows // 2), align), rows // 2)


def _gather_first_layer(w_in, others):
    n = len(others)
    rows = w_in.shape[1]

    def body(*refs):
        w_ref, o_refs = refs[0], refs[1:1 + n]
        out, keep = refs[1 + n], refs[2 + n:3 + 3 * n]
        stage, ici_send, ici_recv, d2d_send, d2d_recv, local_sem = refs[3 + 3 * n:]
        x, y, c = _me()
        k = 2 * x + y
        chips = _other_chips(x, y)
        stage[...] = w_ref[0].astype(BF16)
        local = pltpu.make_async_copy(stage, out.at[k], local_sem)
        local.start()
        mine = _half_rows(rows, c, 16)
        first, passed = [], []
        for j, (px, py) in enumerate(chips):
            cp = pltpu.make_async_remote_copy(
                src_ref=stage.at[mine], dst_ref=out.at[k, mine], send_sem=ici_send.at[j], recv_sem=ici_recv.at[j],
                device_id=(px, py, c), device_id_type=MESH)
            cp.start()
            first.append(cp)
        keep[0][...] = w_ref[1].astype(BF16)
        for t in range(n):
            for l in range(2):
                keep[1 + 2 * t + l][...] = o_refs[t][l].astype(BF16)
        for j, (px, py) in enumerate(chips):
            landed = out.at[2 * px + py, mine]
            first[j].wait_recv()
            cp = pltpu.make_async_remote_copy(
                src_ref=landed, dst_ref=landed, send_sem=d2d_send.at[j], recv_sem=d2d_recv.at[j],
                device_id=(x, y, 1 - c), device_id_type=MESH)
            cp.start()
            passed.append(cp)
        for cp in passed:
            cp.wait_recv()
        for cp in first + passed:
            cp.wait_send()
        local.wait()

    kept = [jax.ShapeDtypeStruct(w_in.shape[1:], BF16)]
    for o in others:
        kept += [jax.ShapeDtypeStruct(o.shape[1:], BF16)] * 2
    return pl.pallas_call(
        body, name="gather_first_layer",
        in_specs=[VMEM_SPEC] * (1 + n), out_specs=[ANY] + [VMEM_SPEC] * len(kept),
        out_shape=[jax.ShapeDtypeStruct((4,) + w_in.shape[1:], BF16)] + kept,
        scratch_shapes=[pltpu.VMEM(w_in.shape[1:], BF16)] + _dma_sems(3, 3, 3, 3) + [pltpu.SemaphoreType.DMA],
        compiler_params=pltpu.CompilerParams(vmem_limit_bytes=VMEM_LIMIT),
    )(w_in, *others)


def _run_comm(name, comm):
    nci, nco = len(comm.ins), len(comm.out_shapes)

    def body(*refs):
        copies = comm.make(refs[:nci], refs[nci:nci + nco], refs[nci + nco:])
        for cp in copies:
            cp.start()
        for cp in copies:
            cp.wait()

    return pl.pallas_call(body, name=name, in_specs=[ANY] * nci, out_specs=[ANY] * nco,
                          out_shape=list(comm.out_shapes), scratch_shapes=list(comm.sems))(*comm.ins)


def _gather_comm(mine):
    n = len(mine)

    def make(ins, outs, sems):
        send_sems, recv_sems, local_sems = sems
        x, y, c = _me()
        k = 2 * x + y
        copies = []
        for t in range(n):
            copies.append(pltpu.make_async_copy(ins[t], outs[t].at[k], local_sems.at[t]))
            for j, (px, py) in enumerate(_other_chips(x, y)):
                copies.append(pltpu.make_async_remote_copy(
                    src_ref=ins[t], dst_ref=outs[t].at[k], send_sem=send_sems.at[3 * t + j],
                    recv_sem=recv_sems.at[3 * t + j], device_id=(px, py, c), device_id_type=MESH))
        return copies

    return _Comm(mine, [jax.ShapeDtypeStruct((4,) + a.shape, a.dtype) for a in mine], _dma_sems(3 * n, 3 * n, n), make)


def _swap_comm(grads):
    n = len(grads)

    def make(ins, outs, sems):
        send_sems, recv_sems = sems
        x, y, c = _me()
        return [pltpu.make_async_remote_copy(
            src_ref=ins[t].at[:, _half_rows(grads[t].shape[1], 1 - c, 8)], dst_ref=outs[t],
            send_sem=send_sems.at[t], recv_sem=recv_sems.at[t], device_id=(x, y, 1 - c), device_id_type=MESH)
            for t in range(n)]

    shapes = [jax.ShapeDtypeStruct((g.shape[0], g.shape[1] // 2, g.shape[2]), F32) for g in grads]
    return _Comm(grads, shapes, _dma_sems(n, n), make)


def _scatter_comm(parts):
    n = len(parts)

    def make(ins, outs, sems):
        send_sems, recv_sems = sems
        x, y, c = _me()
        k = 2 * x + y
        return [pltpu.make_async_remote_copy(
            src_ref=ins[t].at[2 * px + py], dst_ref=outs[t].at[k], send_sem=send_sems.at[3 * t + j],
            recv_sem=recv_sems.at[3 * t + j], device_id=(px, py, c), device_id_type=MESH)
            for t in range(n) for j, (px, py) in enumerate(_other_chips(x, y))]

    return _Comm(parts, [jax.ShapeDtypeStruct(p.shape, p.dtype) for p in parts], _dma_sems(3 * n, 3 * n), make)


def _share_comm(totals):
    n = len(totals)

    def make(ins, outs, sems):
        send_sems, recv_sems = sems
        x, y, c = _me()
        return [pltpu.make_async_remote_copy(
            src_ref=ins[t], dst_ref=outs[t], send_sem=send_sems.at[t], recv_sem=recv_sems.at[t],
            device_id=(x, y, 1 - c), device_id_type=MESH) for t in range(n)]

    return _Comm(totals, [jax.ShapeDtypeStruct(t.shape, F32) for t in totals], _dma_sems(n, n), make)


def _small_allreduce_adamw(part, w, m, v):
    shape = part.shape

    def body(part_ref, w_ref, m_ref, v_ref, g_ref, d_ref, nm_ref, nv_ref, slots, send_sems, recv_sems):
        x, y, c = _me()
        me = 4 * x + 2 * y + c
        slots[me] = part_ref[...]
        copies = []
        for d in range(1, 8):
            peer = (x ^ (d >> 2), y ^ ((d >> 1) & 1), c ^ (d & 1))
            cp = pltpu.make_async_remote_copy(
                src_ref=part_ref, dst_ref=slots.at[me], send_sem=send_sems.at[d - 1], recv_sem=recv_sems.at[d - 1],
                device_id=peer, device_id_type=MESH)
            cp.start()
            copies.append(cp)
        for cp in copies:
            cp.wait()
        g = slots[0]
        for i in range(1, 8):
            g = g + slots[i]
        g_ref[...] = g
        d, nm, nv = _adamw_math(w_ref[...], g, m_ref[...], v_ref[...])
        d_ref[...] = d
        nm_ref[...] = nm
        nv_ref[...] = nv

    shp = jax.ShapeDtypeStruct(shape, F32)
    return pl.pallas_call(
        body, name="small_allreduce_adamw", in_specs=[VMEM_SPEC] * 4, out_specs=[VMEM_SPEC] * 4,
        out_shape=[shp, shp, shp, shp],
        scratch_shapes=[pltpu.VMEM((8,) + shape, F32), pltpu.SemaphoreType.DMA((7,)), pltpu.SemaphoreType.DMA((7,))],
    )(part, w, m, v)


TM = 512
T_FOX = 1024
T_DIL_FWD = 1024
T_DIL_BWD = 512
TN_DW = 1408


def _layer_fwd(l, h, p_i, w_inside, w_outside, consts, comm=None, target=None):
    w_all, _, norm_g, b_f, qkg = w_inside
    bsum, _, bias_t, _, rope = consts
    u, z, qa, ka, va, qb, kb, vb, c_spread, c_t, va_t, vb_t, qa_t, qb_t = _inproj(
        l, h, norm_g, w_all, b_f, qkg, bsum, rope, TM)
    oa, lse_a, *landed = _attn_fwd(f"fox_fwd_l{l}", True, qa_t, ka, va_t, (c_spread, c_t), T_FOX, comm)
    ob, lse_b = _attn_fwd(f"dil_fwd_l{l}", False, qb_t, kb, vb_t, (bias_t,), T_DIL_FWD)
    if comm is not None:
        w_outside = w_outside(landed)
    w_out, _, w_gate, _, w_ple, ple_g = w_outside
    h2, gate, *out = _outproj(l, h, oa, ob, z, p_i, w_out, ple_g, w_gate, w_ple, TM, target)
    out = out[0] if target is None else tuple(out)
    saved = (h, u, z, qa, ka, va, qb, kb, vb, c_spread, c_t, oa, lse_a, ob, lse_b, h2, gate)
    return out, saved, w_outside


def _reduce_names(tag, n):
    return [f"reduce_{tag}_{i}" for i in range(n)]


def _layer_bwd(l, dout, p_i, wts, consts, saved, pending=None, core=None, chip=None):
    (_, w_all_t, norm_g, b_f, qkg), (_, w_out_t, _, w_gate_t, w_ple, ple_g) = wts
    bsum, hsel, _, bias_t, rope = consts
    h, u, z, qa, ka, va, qb, kb, vb, c_spread, c_t, oa, lse_a, ob, lse_b, h2, gate = saved
    fused = pending is not None
    dh2, doa, dob, dga, dgb, delta_t, dw_out, dw_gate, dw_ple, dple_g = _outproj_bwd(
        l, dout, h2, gate, w_ple, p_i, oa, ob, z, w_out_t, ple_g, w_gate_t, hsel, TM)
    if fused:
        group = list(pending) + list(_slot_layout_out(dw_out, dw_ple, dw_gate))
        n = len(group)
    dqt_a, dk_a, dv_a, dc, drow, *sib = _attn_bwd(
        f"fox_bwd_l{l}", True, qa, ka, va, doa, lse_a, delta_t, 0, (c_spread, c_t), T_FOX,
        _swap_comm(group) if fused else None)
    if fused:
        pair = [_pair_sum(nm, g, x, core)[0] for nm, g, x in zip(_reduce_names("pair_hidden", n), group, sib)]
    dqt_b, dk_b, dv_b, *landed = _attn_bwd(f"dil_bwd_l{l}", False, qb, kb, vb, dob, lse_b, delta_t, N_PAIRS,
                                             (bias_t,), T_DIL_BWD, _scatter_comm(pair) if fused else None)
    if fused:
        totals = [_sum_slots(nm, a, y, chip) for nm, a, y in zip(_reduce_names("chips_hidden", n), pair, landed)]
    dz, dqkg, dbf = _inproj_bwd_prep(l, z, dqt_a, dk_a, dv_a, dqt_b, dk_b, dv_b, dga, dgb, dc, drow, b_f, qkg,
                                     bsum, rope, TM)
    reduced = None
    if fused:
        dw_all, *other = _inproj_bwd_dw(l, u, dz, TM, TN_DW, _share_comm(totals))
        own = [_slot_layout_in(dw_all)]
        sib = _run_comm("reduce_swap_last", _swap_comm(own))
        wide, narrow = _pair_sum("reduce_pair_last", own[0], sib[0], core, narrow=True)
        dh, dnorm_g, landed = _inproj_bwd_dx(l, dz, w_all_t, h, norm_g, dh2, TM, _scatter_comm([narrow]))
        total = _sum_slots("reduce_chips_last", wide, landed, chip)
        reduced = list(zip(totals, other)) + [(total, _run_comm("reduce_share_last", _share_comm([total]))[0])]
    else:
        dh, dnorm_g = _inproj_bwd_dx(l, dz, w_all_t, h, norm_g, dh2, TM)
        dw_all, = _inproj_bwd_dw(l, u, dz, TM, TN_DW)
    return dh, (dw_all, dw_out, dw_ple, dw_gate, dnorm_g[0], dbf[0, :N_HEADS], dqkg[:4], dple_g[0]), reduced


N_FA = 2048


W_SHARD = N_IN // 4


def _shard_pieces(lo, hi):
    cuts = [(k, max(lo, k * W_SHARD), min(hi, (k + 1) * W_SHARD)) for k in range(4)]
    return [(k, a - k * W_SHARD, b - k * W_SHARD) for k, a, b in cuts if a < b]


def _in_weights(l, g_in, norm_g, b_f, qk_norm_g):
    order = _shard_pieces(0, N_FA) + _shard_pieces(N_FA + N_HEADS, N_IN) + _shard_pieces(N_FA, N_FA + N_HEADS)
    w_all = jnp.concatenate([g_in[k, :, a:b] for k, a, b in order]
                            + [jnp.zeros((D_MODEL, LANE - N_HEADS), g_in.dtype)], axis=1)
    qkg = jnp.pad(jnp.tile(qk_norm_g[l], (1, N_HEADS)), ((0, 4), (0, 0)))
    bf = jnp.pad(b_f[l], (0, LANE - N_HEADS))[None, :]
    return w_all, w_all.T, norm_g[l][None, :], bf, qkg


def _out_weights(l, g_out, g_ple, g_gate, ple_norm_g):
    w_out = g_out.reshape(D_MODEL, D_MODEL)
    w_gate = g_gate.reshape(D_MODEL, D_MODEL)
    w_ple = jnp.transpose(g_ple, (1, 0, 2)).reshape(PLE_DIM, D_MODEL)
    return w_out, w_out.T, w_gate, w_gate.T, w_ple, ple_norm_g[l][None, :]


def _slot_layout_in(dw_all):
    regions = ((0, N_FA, 0), (N_FA, N_FA + N_HEADS, N_MAIN - N_FA), (N_FA + N_HEADS, N_IN, -N_HEADS))

    def shard(k):
        cuts = [(max(k * W_SHARD, a) + shift, min((k + 1) * W_SHARD, b) + shift) for a, b, shift in regions]
        return jnp.concatenate([dw_all[:, a:b] for a, b in cuts if a < b], axis=1)

    return jnp.stack([shard(k) for k in range(4)])


def _slot_layout_out(dw_out, dw_ple, dw_gate):
    return (dw_out.reshape(4, D_MODEL // 4, D_MODEL),
            jnp.transpose(dw_ple.reshape(PLE_DIM, 4, D_MODEL // 4), (1, 0, 2)),
            dw_gate.reshape(4, D_MODEL // 4, D_MODEL))


SMALL_ROWS = 40


def _pack_small(norm_g, ple_norm_g, qk_norm_g, b_f, last=0.0):
    flat = jnp.concatenate([norm_g.reshape(-1), ple_norm_g.reshape(-1), qk_norm_g.reshape(-1), b_f.reshape(-1)])
    flat = jnp.pad(flat, (0, SMALL_ROWS * LANE - flat.shape[0] - 1))
    return jnp.concatenate([flat, jnp.reshape(last, (1,)).astype(F32)]).reshape(SMALL_ROWS, LANE)


def _unpack_small(packed):
    flat = packed.reshape(-1)
    n1, n2, n3 = 2 * D_MODEL, 4 * D_MODEL, 4 * D_MODEL + 2 * 4 * HEAD_DIM
    return (flat[:n1].reshape(2, D_MODEL), flat[n1:n2].reshape(2, D_MODEL), flat[n2:n3].reshape(2, 4, HEAD_DIM),
            flat[n3:n3 + 2 * N_HEADS].reshape(2, N_HEADS))


def kernel(x, p, positions, norm_g, w_in, b_f, qk_norm_g, w_out, w_ple, ple_norm_g, w_ple_gate, loss_target,
           m_norm_g, m_w_in, m_b_f, m_qk_norm_g, m_w_out, m_w_ple, m_ple_norm_g, m_w_ple_gate,
           v_norm_g, v_w_in, v_b_f, v_qk_norm_g, v_w_out, v_w_ple, v_ple_norm_g, v_w_ple_gate):
    assert w_in.shape[0] == 2, "the schedule below is written for two layers"
    w_in0, *kept = _gather_first_layer(w_in, [w_out, w_ple, w_ple_gate])
    consts = (_head_block_diag(), _head_select(), _dil_bias(T_DIL_FWD), _dil_bias(T_DIL_BWD),
              _rope_tables(positions[0]))
    later = {}

    def outside0(landed):
        later["w_in1"] = landed[0]
        later["out1"] = landed[2::2]
        return _out_weights(0, *landed[1::2], ple_norm_g)

    inside0 = _in_weights(0, w_in0, norm_g, b_f, qk_norm_g)
    h1, saved0, outside0 = _layer_fwd(0, x[0], p[0, 0], inside0, outside0, consts, _gather_comm(kept))
    wts0 = (inside0, outside0)
    wts1 = (_in_weights(1, later["w_in1"], norm_g, b_f, qk_norm_g), _out_weights(1, *later["out1"], ple_norm_g))
    (dh, sq), saved1, _ = _layer_fwd(1, h1, p[1, 0], *wts1, consts, target=loss_target[0])

    core = lax.axis_index("c").astype(jnp.int32).reshape(1)
    chip = (2 * lax.axis_index("x") + lax.axis_index("y")).astype(jnp.int32).reshape(1)
    dh, grads1, _ = _layer_bwd(1, dh, p[1, 0], wts1, consts, saved1)
    pending = (_slot_layout_in(grads1[0]),) + _slot_layout_out(*grads1[1:4])
    dh, grads0, hidden = _layer_bwd(0, dh, p[0, 0], wts0, consts, saved0, pending, core, chip)
    reduced0 = [hidden[7]] + hidden[4:7]
    reduced1 = hidden[0:4]
    grad_x = dh[None]
    small = [grads0[4:], grads1[4:]]
    n_layers = 2

    outs = {}
    for t, (name, w, m, v) in enumerate((("w_in", w_in, m_w_in, v_w_in), ("w_out", w_out, m_w_out, v_w_out),
                                         ("w_ple", w_ple, m_w_ple, v_w_ple),
                                         ("w_ple_gate", w_ple_gate, m_w_ple_gate, v_w_ple_gate))):
        if name == "w_in":
            g = _assemble_halves("assemble_w_in", w.shape, (reduced0[t], reduced1[t]), core)
            g_flat = _to_flat(g)
            flat = _adamw_flat("adamw_w_in", _to_flat(w), g_flat, _to_flat(m), _to_flat(v))
            outs[name] = tuple(_from_flat(f, w.shape) for f in (g_flat,) + tuple(flat))
        else:
            outs[name] = tuple(_adamw(f"adamw_{name}", w, (reduced0[t], reduced1[t]), m, v, core))

    part = _pack_small(jnp.stack([s[0] for s in small]), jnp.stack([s[3] for s in small]),
                       jnp.stack([s[2] for s in small]).reshape(n_layers, 4, N_HEADS, HEAD_DIM).sum(axis=2),
                       jnp.stack([s[1] for s in small]), 0.5 / D_MODEL * jnp.sum(sq))
    packed = _small_allreduce_adamw(part, _pack_small(norm_g, ple_norm_g, qk_norm_g, b_f),
                                    _pack_small(m_norm_g, m_ple_norm_g, m_qk_norm_g, m_b_f),
                                    _pack_small(v_norm_g, v_ple_norm_g, v_qk_norm_g, v_b_f))
    loss = packed[0][SMALL_ROWS - 1, LANE - 1]
    sm = [_unpack_small(a) for a in packed]
    for i, name in enumerate(("norm_g", "ple_norm_g", "qk_norm_g", "b_f")):
        outs[name] = tuple(sm[j][i] for j in range(4))

    order = ("norm_g", "w_in", "b_f", "qk_norm_g", "w_out", "w_ple", "ple_norm_g", "w_ple_gate")
    return (loss, grad_x) + tuple(outs[n][j] for j in range(4) for n in order)
```
